```python
import math
import jax, jax.numpy as jnp
from jax import lax
import numpy as np

D_MODEL = 1024
BATCH = 8
SEQ = 4096
DEPTH = 4

GRID_W = 64
CTX_LEN = 256

A_HEADS = 8
A_KV_HEADS = 2
A_HEAD_DIM = 64
Q_BLOCK = 128
ROPE_THETA = 10000.0
B_HEADS = 8
B_HEAD_DIM = 64
NA_ROWS = 8
NA_COLS = 16
C_HEADS = 8
C_HEAD_DIM = 64
CONV_K = 4
CHUNK = 64
N_BRANCH = 3
BRANCH_W = 512
N_EXPERTS = 16
D_EXPERT = 2048
CAPACITY = 2
N_MOD = 6
LN_EPS = 1e-6
DEEPNORM_ALPHA = (2 * DEPTH) ** 0.25
DEEPNORM_BETA = (8 * DEPTH) ** -0.25

A_Q_W = A_HEADS * A_HEAD_DIM
A_KV_W = A_KV_HEADS * A_HEAD_DIM
B_W = B_HEADS * B_HEAD_DIM
C_W = C_HEADS * C_HEAD_DIM
IN_SIZES = (A_Q_W, A_KV_W, A_KV_W, B_W, B_W, B_W, 3 * C_W, C_W, 2 * C_HEADS, 2 * C_HEADS, N_BRANCH * D_MODEL)
D_IN = sum(IN_SIZES)

kernel_name = 'hybrid_diffusion_trunk'


def layer_norm(x):
    xf = x.astype(jnp.float32)
    mu = jnp.mean(xf, axis=-1, keepdims=True)
    var = jnp.mean(jnp.square(xf - mu), axis=-1, keepdims=True)
    return ((xf - mu) * lax.rsqrt(var + LN_EPS)).astype(x.dtype)


def layer_norm_affine(x, g, b):
    return layer_norm(x) * g + b


def rms_norm(x, g):
    xf = x.astype(jnp.float32)
    y = xf * lax.rsqrt(jnp.mean(jnp.square(xf), axis=-1, keepdims=True) + LN_EPS)
    return y.astype(x.dtype) * g


def l2_normalize(x):
    return x * lax.rsqrt(jnp.sum(jnp.square(x), axis=-1, keepdims=True) + LN_EPS)


def modulate(x, shift, scale):
    return x * (1 + scale) + shift


def softmax_f32(s):
    return jax.nn.softmax(s.astype(jnp.float32), axis=-1)


def split_in(z):
    return jnp.split(z, np.cumsum(IN_SIZES)[:-1].tolist(), axis=-1)


def heads(t, n_heads, head_dim):
    return t.reshape(t.shape[0], t.shape[1], n_heads, head_dim)


def rope_tables(n_tokens, dtype):
    t = jnp.arange(n_tokens)
    rows = (t // GRID_W).astype(jnp.float32)
    cols = (t % GRID_W).astype(jnp.float32)
    n_freq = A_HEAD_DIM // 4
    inv_freq = ROPE_THETA ** (-jnp.arange(n_freq, dtype=jnp.float32) / n_freq)
    ang_r = rows[:, None] * inv_freq
    ang_c = cols[:, None] * inv_freq
    return tuple(a.astype(dtype) for a in (jnp.cos(ang_r), jnp.sin(ang_r), jnp.cos(ang_c), jnp.sin(ang_c)))


def apply_rope_2d(x, tables):
    cr, sr, cc, sc = [t[None, :, None, :] for t in tables]
    x1, x2, x3, x4 = jnp.split(x, 4, axis=-1)
    return jnp.concatenate([x1 * cr - x2 * sr, x2 * cr + x1 * sr,
                            x3 * cc - x4 * sc, x4 * cc + x3 * sc], axis=-1)


def dense_attention(q, k, v):
    b, m, h, d = q.shape
    kvh = k.shape[2]
    qg = q.reshape(b, m, kvh, h // kvh, d)
    s = jnp.einsum('bqkgd,bmkd->bkgqm', qg, k) * (d ** -0.5)
    p = softmax_f32(s).astype(v.dtype)
    return jnp.einsum('bkgqm,bmkd->bqkgd', p, v).reshape(b, m, h * d)


def gqa_latent(q, k_all, v_all):
    b, n = q.shape[:2]
    g = A_HEADS // A_KV_HEADS
    nb = n // Q_BLOCK
    qb = q.reshape(b, nb, Q_BLOCK, A_KV_HEADS, g, A_HEAD_DIM).transpose(1, 0, 3, 4, 2, 5)
    scale = A_HEAD_DIM ** -0.5

    def block(qi):
        s = jnp.einsum('bkgqd,bkmd->bkgqm', qi, k_all) * scale
        p = softmax_f32(s).astype(v_all.dtype)
        return jnp.einsum('bkgqm,bkmd->bkgqd', p, v_all)

    o = lax.map(block, qb)
    return o.transpose(1, 0, 4, 2, 3, 5).reshape(b, n, A_HEADS * A_HEAD_DIM)


def neighborhood_latent(q, k, v, k_ctx, v_ctx, rpb):
    b, n = q.shape[:2]
    rows = n // GRID_W
    win_r = min(NA_ROWS, rows)

    def to_grid(t):
        return t.reshape(b, rows, GRID_W, B_HEADS, B_HEAD_DIM).transpose(0, 3, 1, 2, 4)

    qg, kg, vg = to_grid(q), to_grid(k), to_grid(v)
    kc = k_ctx.transpose(0, 2, 1, 3)
    vc = v_ctx.transpose(0, 2, 1, 3)
    cols = np.arange(GRID_W)
    col_start = np.clip(cols - NA_COLS // 2, 0, GRID_W - NA_COLS)
    col_idx = col_start[:, None] + np.arange(NA_COLS)[None, :]
    col_off = col_idx - cols[:, None] + (NA_COLS - 1)
    rpb_cols = rpb[:, :, col_off]
    scale = B_HEAD_DIM ** -0.5

    def row_block(r):
        rs = jnp.clip(r - win_r // 2, 0, rows - win_r)
        qr = lax.dynamic_index_in_dim(qg, r, axis=2, keepdims=False)
        k_rows = lax.dynamic_slice_in_dim(kg, rs, win_r, axis=2)
        v_rows = lax.dynamic_slice_in_dim(vg, rs, win_r, axis=2)
        k_nb = k_rows[:, :, :, col_idx]
        v_nb = v_rows[:, :, :, col_idx]
        row_off = rs + jnp.arange(win_r) - r + (NA_ROWS - 1)
        bias = jnp.take(rpb_cols, row_off, axis=1).transpose(0, 2, 1, 3)
        s_nb = jnp.einsum('bhqd,bhrqjd->bhqrj', qr, k_nb) * scale + bias[None]
        s_nb = s_nb.reshape(b, B_HEADS, GRID_W, win_r * NA_COLS)
        s_ctx = jnp.einsum('bhqd,bhmd->bhqm', qr, kc) * scale
        p = softmax_f32(jnp.concatenate([s_nb, s_ctx], axis=-1)).astype(v.dtype)
        p_nb = p[..., :win_r * NA_COLS].reshape(b, B_HEADS, GRID_W, win_r, NA_COLS)
        p_ctx = p[..., win_r * NA_COLS:]
        return (jnp.einsum('bhqrj,bhrqjd->bhqd', p_nb, v_nb)
                + jnp.einsum('bhqm,bhmd->bhqd', p_ctx, vc))

    o = lax.map(row_block, jnp.arange(rows))
    return o.transpose(1, 0, 3, 2, 4).reshape(b, n, B_HEADS * B_HEAD_DIM)


def short_conv(x, w):
    n = x.shape[1]
    left = CONV_K // 2
    xp = jnp.pad(x, ((0, 0), (left, CONV_K - 1 - left), (0, 0)))
    y = xp[:, 0:n] * w[0]
    for j in range(1, CONV_K):
        y = y + xp[:, j:j + n] * w[j]
    return jax.nn.silu(y)


def gdn_prepare(qkv, a, bt, a_log, dt_bias):
    b, n = qkv.shape[:2]
    q, k, v = jnp.split(qkv.astype(jnp.float32), 3, axis=-1)
    q = l2_normalize(heads(q, C_HEADS, C_HEAD_DIM)) * (C_HEAD_DIM ** -0.5)
    k = l2_normalize(heads(k, C_HEADS, C_HEAD_DIM))
    v = heads(v, C_HEADS, C_HEAD_DIM)
    a = a.astype(jnp.float32).reshape(b, n, 2, C_HEADS)
    bt = bt.astype(jnp.float32).reshape(b, n, 2, C_HEADS)
    g = -jnp.exp(a_log.astype(jnp.float32)) * jax.nn.softplus(a + dt_bias.astype(jnp.float32))
    beta = jax.nn.sigmoid(bt)

    def both(t):
        return jnp.stack([t, jnp.flip(t, axis=1)], axis=0).transpose(0, 1, 3, 2, 4)

    def both_dir(t):
        return jnp.stack([t[:, :, 0], jnp.flip(t[:, :, 1], axis=1)], axis=0).transpose(0, 1, 3, 2)

    return both(q), both(k), both(v), both_dir(g), both_dir(beta)


def gated_delta_rule(q, k, v, g, beta, s0, with_output):
    lead = q.shape[:-2]
    nd = len(lead)
    n = q.shape[-2]
    nc = n // CHUNK

    def chunks(t):
        return t.reshape(lead + (nc, CHUNK) + t.shape[nd + 1:])

    qc, kc, vc, gc, bc = chunks(q), chunks(k), chunks(v), chunks(g), chunks(beta)
    gcum = jnp.cumsum(gc, axis=-1)
    idx = jnp.arange(CHUNK)
    incl = idx[:, None] >= idx[None, :]
    strict = idx[:, None] > idx[None, :]
    decay = jnp.exp(jnp.where(incl, gcum[..., :, None] - gcum[..., None, :], -jnp.inf))
    kb = kc * bc[..., None]
    vb = vc * bc[..., None]
    lmat = jnp.where(strict, jnp.einsum('...id,...jd->...ij', kb, kc) * decay, 0.0)
    eye = jnp.eye(CHUNK, dtype=jnp.float32)
    tmat = lax.linalg.triangular_solve(eye + lmat, jnp.broadcast_to(eye, lmat.shape),
                                       left_side=True, lower=True, unit_diagonal=True)
    u = tmat @ vb
    w = tmat @ (kb * jnp.exp(gcum)[..., None])
    glast = gcum[..., -1:]
    kdec = kc * jnp.exp(glast - gcum)[..., None]
    sdec = jnp.exp(glast[..., 0])
    xs = (u, w, kdec, sdec)
    if with_output:
        qdec = qc * jnp.exp(gcum)[..., None]
        intra = jnp.where(incl, jnp.einsum('...id,...jd->...ij', qc, kc) * decay, 0.0)
        xs = xs + (qdec, intra)
    xs = tuple(jnp.moveaxis(t, nd, 0) for t in xs)

    def step(s, inp):
        u_i, w_i, kd_i, sd_i = inp[:4]
        v_new = u_i - w_i @ s
        s_next = s * sd_i[..., None, None] + jnp.swapaxes(kd_i, -1, -2) @ v_new
        if with_output:
            qd_i, intra_i = inp[4:]
            return s_next, qd_i @ s + intra_i @ v_new
        return s_next, None

    s_fin, o = lax.scan(step, s0, xs)
    if not with_output:
        return None, s_fin
    o = jnp.moveaxis(o, 0, nd).reshape(lead + (n, v.shape[-1]))
    return o, s_fin


def gdn_branch(qkv, a, bt, gate, a_log, dt_bias, o_gain, s0, with_output):
    b, n = qkv.shape[:2]
    q2, k2, v2, g2, beta2 = gdn_prepare(qkv, a, bt, a_log, dt_bias)
    o2, s_fin = gated_delta_rule(q2, k2, v2, g2, beta2, s0, with_output)
    if not with_output:
        return None, s_fin
    o = (o2[0] + jnp.flip(o2[1], axis=-2)).transpose(0, 2, 1, 3)
    o = rms_norm(o, o_gain) * jax.nn.silu(heads(gate.astype(jnp.float32), C_HEADS, C_HEAD_DIM))
    return o.reshape(b, n, C_W).astype(qkv.dtype), s_fin


def merge_branches(branches, gate_logits, w_branch, w_out):
    b, n = gate_logits.shape[:2]
    gates = jax.nn.sigmoid(gate_logits.astype(jnp.float32)).astype(gate_logits.dtype)
    gates = gates.reshape(b, n, N_BRANCH, D_MODEL)
    m = gates[:, :, 0] * (branches[0] @ w_branch[0])
    for i in range(1, N_BRANCH):
        m = m + gates[:, :, i] * (branches[i] @ w_branch[i])
    return m @ w_out


def token_mixers(h, h_x, rope, w_in, qk_gain, rpb, conv_w, a_log, dt_bias, o_gain, w_branch, w_out, need_ctx):
    b = h.shape[0]
    qa, ka, va, qb, kb, vb, qkv_c, gate_c, a_c, b_c, mg = split_in(h @ w_in)
    qa_x, ka_x, va_x, qb_x, kb_x, vb_x, qkv_cx, gate_cx, a_cx, b_cx, mg_x = split_in(h_x @ w_in)
    qa = apply_rope_2d(rms_norm(heads(qa, A_HEADS, A_HEAD_DIM), qk_gain[0]), rope)
    ka = apply_rope_2d(rms_norm(heads(ka, A_KV_HEADS, A_HEAD_DIM), qk_gain[1]), rope)
    ka_x = rms_norm(heads(ka_x, A_KV_HEADS, A_HEAD_DIM), qk_gain[1])
    va_x = heads(va_x, A_KV_HEADS, A_HEAD_DIM)
    k_all = jnp.concatenate([ka, ka_x], axis=1).transpose(0, 2, 1, 3)
    v_all = jnp.concatenate([heads(va, A_KV_HEADS, A_HEAD_DIM), va_x], axis=1).transpose(0, 2, 1, 3)
    o_a = gqa_latent(qa, k_all, v_all)
    kb_x = heads(kb_x, B_HEADS, B_HEAD_DIM)
    vb_x = heads(vb_x, B_HEADS, B_HEAD_DIM)
    o_b = neighborhood_latent(heads(qb, B_HEADS, B_HEAD_DIM), heads(kb, B_HEADS, B_HEAD_DIM),
                              heads(vb, B_HEADS, B_HEAD_DIM), kb_x, vb_x, rpb)
    s0 = jnp.zeros((2, b, C_HEADS, C_HEAD_DIM, C_HEAD_DIM), jnp.float32)
    o_c_x, s_ctx = gdn_branch(short_conv(qkv_cx, conv_w), a_cx, b_cx, gate_cx, a_log, dt_bias,
                              o_gain, s0, need_ctx)
    o_c, _ = gdn_branch(short_conv(qkv_c, conv_w), a_c, b_c, gate_c, a_log, dt_bias,
                        o_gain, s_ctx, True)
    out = merge_branches((o_a, o_b, o_c), mg, w_branch, w_out)
    if not need_ctx:
        return out, None
    o_a_x = dense_attention(rms_norm(heads(qa_x, A_HEADS, A_HEAD_DIM), qk_gain[0]), ka_x, va_x)
    o_b_x = dense_attention(heads(qb_x, B_HEADS, B_HEAD_DIM), kb_x, vb_x)
    out_x = merge_branches((o_a_x, o_b_x, o_c_x), mg_x, w_branch, w_out)
    return out, out_x


def expert_choice_ffn(h, w_router, w_gate_up, w_down):
    b, n, _ = h.shape
    cap = CAPACITY * n // N_EXPERTS
    aff = jax.nn.softmax(jnp.einsum('bnd,de->bne', h, w_router).astype(jnp.float32), axis=-1)
    gate, idx = lax.top_k(jnp.swapaxes(aff, 1, 2), cap)
    bidx = jnp.arange(b)[:, None, None]
    xs = h[bidx, idx]
    gu = jnp.einsum('becd,edf->becf', xs, w_gate_up)
    g_, u_ = jnp.split(gu, 2, axis=-1)
    y = jnp.einsum('becf,efd->becd', jax.nn.silu(g_) * u_, w_down)
    y = y * gate[..., None].astype(y.dtype)
    return jnp.zeros_like(h).at[bidx, idx].add(y)


def setup_inputs(seed: int = 0) -> dict:
    key = jax.random.key(seed)
    ks = jax.random.split(key, 24)
    f32 = jnp.float32

    def nrm(k, shape, s):
        return jax.random.normal(k, shape, f32) * s

    x = nrm(ks[0], (BATCH, SEQ, D_MODEL), 1.0)
    c = nrm(ks[1], (BATCH, D_MODEL), 1.0)
    ctx = nrm(ks[2], (BATCH, CTX_LEN, D_MODEL), 1.0)
    c_ctx = nrm(ks[3], (D_MODEL,), 1.0)
    w_mod = nrm(ks[4], (DEPTH, D_MODEL, N_MOD * D_MODEL), 0.5 * D_MODEL ** -0.5)
    b_mod = nrm(ks[5], (DEPTH, N_MOD * D_MODEL), 0.02)
    w_in = nrm(ks[6], (DEPTH, D_MODEL, D_IN), D_MODEL ** -0.5)
    qk_gain = 1.0 + nrm(ks[7], (DEPTH, 2, A_HEAD_DIM), 0.05)
    rpb = nrm(ks[8], (DEPTH, B_HEADS, 2 * NA_ROWS - 1, 2 * NA_COLS - 1), 0.1)
    conv_w = nrm(ks[9], (DEPTH, CONV_K, 3 * C_W), CONV_K ** -0.5)
    a_log = jnp.log(jax.random.uniform(ks[10], (DEPTH, 2, C_HEADS), f32, 1.0, 16.0))
    dt = jnp.exp(jax.random.uniform(ks[11], (DEPTH, 2, C_HEADS), f32, math.log(1e-3), math.log(1e-1)))
    dt_bias = dt + jnp.log(-jnp.expm1(-dt))
    o_gain = 1.0 + nrm(ks[12], (DEPTH, C_HEAD_DIM), 0.05)
    w_branch = nrm(ks[13], (DEPTH, N_BRANCH, BRANCH_W, D_MODEL), BRANCH_W ** -0.5)
    w_out = nrm(ks[14], (DEPTH, D_MODEL, D_MODEL), D_MODEL ** -0.5 * DEEPNORM_BETA)
    ln1_g = 1.0 + nrm(ks[15], (DEPTH, D_MODEL), 0.05)
    ln1_b = nrm(ks[16], (DEPTH, D_MODEL), 0.02)
    w_router = nrm(ks[17], (DEPTH, D_MODEL, N_EXPERTS), D_MODEL ** -0.5)
    w_gate_up = nrm(ks[18], (DEPTH, N_EXPERTS, D_MODEL, 2 * D_EXPERT), D_MODEL ** -0.5)
    w_down = nrm(ks[19], (DEPTH, N_EXPERTS, D_EXPERT, D_MODEL), D_EXPERT ** -0.5 * DEEPNORM_BETA)
    ln2_g = 1.0 + nrm(ks[20], (DEPTH, D_MODEL), 0.05)
    ln2_b = nrm(ks[21], (DEPTH, D_MODEL), 0.02)
    return {'x': x, 'c': c, 'ctx': ctx, 'c_ctx': c_ctx, 'w_mod': w_mod, 'b_mod': b_mod,
            'w_in': w_in, 'qk_gain': qk_gain, 'rpb': rpb, 'conv_w': conv_w, 'a_log': a_log,
            'dt_bias': dt_bias, 'o_gain': o_gain, 'w_branch': w_branch, 'w_out': w_out,
            'ln1_g': ln1_g, 'ln1_b': ln1_b, 'w_router': w_router, 'w_gate_up': w_gate_up,
            'w_down': w_down, 'ln2_g': ln2_g, 'ln2_b': ln2_b}


def reference(x, c, ctx, c_ctx, w_mod, b_mod, w_in, qk_gain, rpb, conv_w, a_log, dt_bias, o_gain,
              w_branch, w_out, ln1_g, ln1_b, w_router, w_gate_up, w_down, ln2_g, ln2_b):
    rope = rope_tables(x.shape[1], x.dtype)
    c_act = jax.nn.silu(c)
    cc_act = jax.nn.silu(c_ctx)
    for layer in range(DEPTH):
        need_ctx = layer < DEPTH - 1
        mod = jnp.split((c_act @ w_mod[layer] + b_mod[layer])[:, None, :], N_MOD, axis=-1)
        mod_x = jnp.split(cc_act @ w_mod[layer] + b_mod[layer], N_MOD, axis=-1)
        h = modulate(layer_norm(x), mod[0], mod[1])
        h_x = modulate(layer_norm(ctx), mod_x[0], mod_x[1])
        mix, mix_x = token_mixers(h, h_x, rope, w_in[layer], qk_gain[layer], rpb[layer], conv_w[layer],
                                  a_log[layer], dt_bias[layer], o_gain[layer], w_branch[layer],
                                  w_out[layer], need_ctx)
        x = layer_norm_affine(DEEPNORM_ALPHA * x + mod[2] * mix, ln1_g[layer], ln1_b[layer])
        h = modulate(layer_norm(x), mod[3], mod[4])
        moe = expert_choice_ffn(h, w_router[layer], w_gate_up[layer], w_down[layer])
        x = layer_norm_affine(DEEPNORM_ALPHA * x + mod[5] * moe, ln2_g[layer], ln2_b[layer])
        if need_ctx:
            ctx = layer_norm_affine(DEEPNORM_ALPHA * ctx + mod_x[2] * mix_x, ln1_g[layer], ln1_b[layer])
            h_x = modulate(layer_norm(ctx), mod_x[3], mod_x[4])
            moe_x = expert_choice_ffn(h_x, w_router[layer], w_gate_up[layer], w_down[layer])
            ctx = layer_norm_affine(DEEPNORM_ALPHA * ctx + mod_x[5] * moe_x, ln2_g[layer], ln2_b[layer])
    return x
```

```python
import functools
import math

import numpy as np
import jax
import jax.numpy as jnp
from jax import lax
from jax.experimental import pallas as pl
from jax.experimental.pallas import tpu as pltpu

F32 = jnp.float32
BF16 = jnp.bfloat16
HIGHEST = lax.Precision.HIGHEST

GRID_W = 64
HEAD_DIM = 64
A_HEADS = 8
A_KV_HEADS = 2
B_HEADS = 8
C_HEADS = 8
NA_ROWS = 8
NA_COLS = 16
CONV_K = 4
CHUNK = 64
N_BRANCH = 3
BRANCH_W = 512
CAPACITY = 2
N_MOD = 6
LN_EPS = 1e-6
ROPE_THETA = 10000.0
NEG_BIG = -1e30

A_Q_W = A_HEADS * HEAD_DIM
A_KV_W = A_KV_HEADS * HEAD_DIM
B_W = B_HEADS * HEAD_DIM
C_W = C_HEADS * HEAD_DIM
LANE = 128
VMEM_LIMIT = 56 * 1024 * 1024


def _cparams(sem):
    return pltpu.CompilerParams(dimension_semantics=sem, vmem_limit_bytes=VMEM_LIMIT)


def _dot(a, b):
    return jnp.dot(a, b, preferred_element_type=F32)


def _dot_hi(a, b):
    return jnp.dot(a, b, precision=HIGHEST, preferred_element_type=F32)


def _dot_nt(a, b):
    return lax.dot_general(a, b, (((1,), (1,)), ((), ())), preferred_element_type=F32)


def _dot_tn(a, b):
    return lax.dot_general(a, b, (((0,), (0,)), ((), ())), preferred_element_type=F32)


def _ln(x):
    mu = jnp.mean(x, axis=-1, keepdims=True)
    xc = x - mu
    var = jnp.mean(xc * xc, axis=-1, keepdims=True)
    return xc * lax.rsqrt(var + LN_EPS)


def _sigmoid(x):
    return 1.0 / (1.0 + jnp.exp(-x))


def _silu(x):
    return x * _sigmoid(x)


def _group_sum_matrix(width, scale):
    g = np.arange(width) // HEAD_DIM
    return jnp.asarray((g[:, None] == g[None, :]).astype(np.float32) * scale)


def _mod_kernel(c_ref, w_ref, b_ref, o_ref):
    c = c_ref[...]
    o_ref[0] = _dot_hi(_silu(c), w_ref[0]) + b_ref[0]


def _modulation(cc, w_mod, b_mod):
    depth, d, dm = w_mod.shape
    r = cc.shape[0]
    tn = min(dm, 1536)
    return pl.pallas_call(
        _mod_kernel,
        grid=(depth, dm // tn),
        in_specs=[pl.BlockSpec((r, d), lambda l, j: (0, 0)),
                  pl.BlockSpec((1, d, tn), lambda l, j: (l, 0, j)),
                  pl.BlockSpec((1, 1, tn), lambda l, j: (l, 0, j))],
        out_specs=pl.BlockSpec((1, r, tn), lambda l, j: (l, 0, j)),
        out_shape=jax.ShapeDtypeStruct((depth, r, dm), F32),
        compiler_params=_cparams(("parallel", "parallel")),
    )(cc, w_mod, b_mod.reshape(depth, 1, dm))


def _rope_swap(y):
    w = y.shape[-1]
    lane = lax.broadcasted_iota(jnp.int32, y.shape, 1)
    up = pltpu.roll(y, w - 16, 1)
    down = pltpu.roll(y, 16, 1)
    return jnp.where((lane % 32) < 16, up, down)


def _inproj_kernel(x_ref, mod_ref, wa_ref, wb_ref, wc_ref, wab_ref, gq_ref, gk_ref, cos_ref, sin_ref,
                   gs_ref, qa_ref, ka_ref, va_ref, qb_ref, kb_ref, vb_ref, qkvc_ref, gatec_ref, ab_ref):
    x = x_ref[0]
    mod = mod_ref[0, 0]
    h = (_ln(x) * (1.0 + mod[1:2]) + mod[0:1]).astype(BF16)
    scale = HEAD_DIM ** -0.5

    za = _dot(h, wa_ref[...])
    q = za[:, :A_Q_W]
    k = za[:, A_Q_W:A_Q_W + A_KV_W]
    v = za[:, A_Q_W + A_KV_W:]
    cos = cos_ref[...]
    sin = sin_ref[...]
    gs = gs_ref[...]
    qn = q * lax.rsqrt(_dot_hi(q * q, gs) + LN_EPS) * gq_ref[...]
    qn = qn * cos + _rope_swap(qn) * sin
    qa_ref[0] = (qn * scale).astype(BF16)
    kn = k * lax.rsqrt(_dot_hi(k * k, gs[:A_KV_W, :A_KV_W]) + LN_EPS) * gk_ref[...]
    kn = kn * cos[:, :A_KV_W] + _rope_swap(kn) * sin[:, :A_KV_W]
    for j in range(A_KV_HEADS):
        ka_ref[0, j] = kn[:, j * HEAD_DIM:(j + 1) * HEAD_DIM].astype(BF16)
        va_ref[0, j] = v[:, j * HEAD_DIM:(j + 1) * HEAD_DIM].astype(BF16)

    zb = _dot(h, wb_ref[...])
    qb_ref[0] = (zb[:, :B_W] * scale).astype(BF16)
    kb_ref[0] = zb[:, B_W:2 * B_W].astype(BF16)
    vb_ref[0] = zb[:, 2 * B_W:].astype(BF16)

    zc = _dot(h, wc_ref[...])
    qkvc_ref[0] = zc[:, :3 * C_W]
    gatec_ref[0] = zc[:, 3 * C_W:]
    ab_ref[0] = _dot(h, wab_ref[...])


def _in_projection(xa, mods, wa, wb, wc, wab, gq, gk, cos, sin, gs, n_lat, tm):
    b, t, d = xa.shape
    nl = n_lat // tm
    row = lambda bi, i: (bi, i, 0)
    const = lambda bi, i: (0, 0)
    outs = [
        (jax.ShapeDtypeStruct((b, t, A_Q_W), BF16), pl.BlockSpec((1, tm, A_Q_W), row)),
        (jax.ShapeDtypeStruct((b, A_KV_HEADS, t, HEAD_DIM), BF16),
         pl.BlockSpec((1, A_KV_HEADS, tm, HEAD_DIM), lambda bi, i: (bi, 0, i, 0))),
        (jax.ShapeDtypeStruct((b, A_KV_HEADS, t, HEAD_DIM), BF16),
         pl.BlockSpec((1, A_KV_HEADS, tm, HEAD_DIM), lambda bi, i: (bi, 0, i, 0))),
        (jax.ShapeDtypeStruct((b, t, B_W), BF16), pl.BlockSpec((1, tm, B_W), row)),
        (jax.ShapeDtypeStruct((b, t, B_W), BF16), pl.BlockSpec((1, tm, B_W), row)),
        (jax.ShapeDtypeStruct((b, t, B_W), BF16), pl.BlockSpec((1, tm, B_W), row)),
        (jax.ShapeDtypeStruct((b, t, 3 * C_W), F32), pl.BlockSpec((1, tm, 3 * C_W), row)),
        (jax.ShapeDtypeStruct((b, t, C_W), F32), pl.BlockSpec((1, tm, C_W), row)),
        (jax.ShapeDtypeStruct((b, t, LANE), F32), pl.BlockSpec((1, tm, LANE), row)),
    ]
    return pl.pallas_call(
        _inproj_kernel,
        grid=(b, t // tm),
        in_specs=[pl.BlockSpec((1, tm, d), row),
                  pl.BlockSpec((1, 1, N_MOD, d), lambda bi, i: (bi, (i >= nl).astype(jnp.int32), 0, 0)),
                  pl.BlockSpec(wa.shape, const), pl.BlockSpec(wb.shape, const),
                  pl.BlockSpec(wc.shape, const), pl.BlockSpec(wab.shape, const),
                  pl.BlockSpec(gq.shape, const), pl.BlockSpec(gk.shape, const),
                  pl.BlockSpec((tm, A_Q_W), lambda bi, i: (i, 0)),
                  pl.BlockSpec((tm, A_Q_W), lambda bi, i: (i, 0)),
                  pl.BlockSpec(gs.shape, const)],
        out_specs=[o[1] for o in outs],
        out_shape=[o[0] for o in outs],
        compiler_params=_cparams(("parallel", "parallel")),
    )(xa, mods, wa, wb, wc, wab, gq, gk, cos, sin, gs)


def _gqa_kernel(q_ref, k_ref, v_ref, o_ref, *, tq, tk, n_lat, n_tot):
    i = pl.program_id(2)
    g = A_HEADS // A_KV_HEADS
    q = jnp.concatenate([q_ref[0, :, j * HEAD_DIM:(j + 1) * HEAD_DIM] for j in range(g)], axis=0)
    is_ctx = i >= n_lat // tq
    lo = jnp.where(is_ctx, n_lat // tk, 0)
    hi = n_tot // tk

    def body(j, carry):
        m, l, acc = carry
        rows = pl.ds(pl.multiple_of(j * tk, tk), tk)
        s = _dot_nt(q, k_ref[0, 0, rows, :])
        m_new = jnp.maximum(m, jnp.max(s, axis=-1, keepdims=True))
        alpha = jnp.exp(m - m_new)
        p = jnp.exp(s - m_new)
        l = alpha * l + jnp.sum(p, axis=-1, keepdims=True)
        acc = alpha * acc + _dot(p.astype(BF16), v_ref[0, 0, rows, :])
        return m_new, l, acc

    m0 = jnp.full((g * tq, 1), NEG_BIG, F32)
    l0 = jnp.zeros((g * tq, 1), F32)
    a0 = jnp.zeros((g * tq, HEAD_DIM), F32)
    _, l, acc = lax.fori_loop(lo, hi, body, (m0, l0, a0))
    o = (acc / l).astype(BF16)
    for j in range(g):
        o_ref[0, :, j * HEAD_DIM:(j + 1) * HEAD_DIM] = o[j * tq:(j + 1) * tq]


def _gqa(qa, ka, va, n_lat, tq, tk):
    b, t, _ = qa.shape
    gw = A_Q_W // A_KV_HEADS
    return pl.pallas_call(
        functools.partial(_gqa_kernel, tq=tq, tk=tk, n_lat=n_lat, n_tot=t),
        grid=(b, A_KV_HEADS, t // tq),
        in_specs=[pl.BlockSpec((1, tq, gw), lambda bi, kh, i: (bi, i, kh)),
                  pl.BlockSpec((1, 1, t, HEAD_DIM), lambda bi, kh, i: (bi, kh, 0, 0)),
                  pl.BlockSpec((1, 1, t, HEAD_DIM), lambda bi, kh, i: (bi, kh, 0, 0))],
        out_specs=pl.BlockSpec((1, tq, gw), lambda bi, kh, i: (bi, i, kh)),
        out_shape=jax.ShapeDtypeStruct((b, t, A_Q_W), BF16),
        compiler_params=_cparams(("parallel", "parallel", "parallel")),
    )(qa, ka, va)


def _na_bias_slabs(rpb):
    cols = np.arange(GRID_W)
    col_start = np.clip(cols - NA_COLS // 2, 0, GRID_W - NA_COLS)
    cj = np.arange(GRID_W)
    valid = (cj[None, :] >= col_start[:, None]) & (cj[None, :] < col_start[:, None] + NA_COLS)
    off = np.clip(cj[None, :] - cols[:, None] + (NA_COLS - 1), 0, 2 * NA_COLS - 2)
    slabs = []
    for o in range(NA_ROWS):
        rows = rpb[:, o:o + NA_ROWS]
        bias = rows[:, :, off]
        bias = jnp.where(valid[None, None], bias, NEG_BIG)
        slabs.append(bias.transpose(0, 2, 1, 3).reshape(rpb.shape[0], GRID_W, NA_ROWS * GRID_W))
    slabs.append(jnp.full_like(slabs[0], NEG_BIG))
    return jnp.stack(slabs, axis=1).astype(F32)


def _na_kernel(q_ref, k_ref, v_ref, bias_ref, o_ref, *, n_lat, n_tot):
    r = pl.program_id(1)
    rows = n_lat // GRID_W
    win = NA_ROWS * GRID_W
    rs = jnp.clip(r - NA_ROWS // 2, 0, rows - NA_ROWS)
    nb = pl.ds(pl.multiple_of(rs * GRID_W, GRID_W), win)
    for h in range(B_HEADS):
        hs = slice(h * HEAD_DIM, (h + 1) * HEAD_DIM)
        q = q_ref[0, :, hs]
        s_nb = _dot_nt(q, k_ref[0, nb, hs]) + bias_ref[h, 0]
        s_cx = _dot_nt(q, k_ref[0, n_lat:n_tot, hs])
        m = jnp.maximum(jnp.max(s_nb, axis=-1, keepdims=True), jnp.max(s_cx, axis=-1, keepdims=True))
        p_nb = jnp.exp(s_nb - m)
        p_cx = jnp.exp(s_cx - m)
        l = jnp.sum(p_nb, axis=-1, keepdims=True) + jnp.sum(p_cx, axis=-1, keepdims=True)
        o = _dot(p_nb.astype(BF16), v_ref[0, nb, hs]) + _dot(p_cx.astype(BF16), v_ref[0, n_lat:n_tot, hs])
        o_ref[0, :, hs] = (o / l).astype(BF16)


def _neighborhood(qb, kb, vb, slabs, n_lat):
    b, t, _ = qb.shape
    rows = n_lat // GRID_W
    n_slab = slabs.shape[1]

    def slab_index(bi, r):
        rs = jnp.clip(r - NA_ROWS // 2, 0, rows - NA_ROWS)
        o = rs - r + (NA_ROWS - 1)
        return (0, jnp.where(r >= rows, n_slab - 1, o), 0, 0)

    return pl.pallas_call(
        functools.partial(_na_kernel, n_lat=n_lat, n_tot=t),
        grid=(b, t // GRID_W),
        in_specs=[pl.BlockSpec((1, GRID_W, B_W), lambda bi, r: (bi, r, 0)),
                  pl.BlockSpec((1, t, B_W), lambda bi, r: (bi, 0, 0)),
                  pl.BlockSpec((1, t, B_W), lambda bi, r: (bi, 0, 0)),
                  pl.BlockSpec((B_HEADS, 1, GRID_W, NA_ROWS * GRID_W), slab_index)],
        out_specs=pl.BlockSpec((1, GRID_W, B_W), lambda bi, r: (bi, r, 0)),
        out_shape=jax.ShapeDtypeStruct((b, t, B_W), BF16),
        compiler_params=_cparams(("parallel", "parallel")),
    )(qb, kb, vb, slabs)


def _gdn_prep_kernel(x_ref, prev_ref, next_ref, ab_ref, cw_ref, alog_ref, dtb_ref, gs_ref, ex_ref,
                     q_ref, k_ref, v_ref, g_ref, bt_ref, *, tm, n_lat, n_tot):
    i = pl.program_id(1)
    nl = n_lat // tm
    nt = n_tot // tm
    first = jnp.logical_or(i == 0, i == nl)
    last = jnp.logical_or(i == nl - 1, i == nt - 1)
    x = x_ref[0]
    pv = jnp.where(first, 0.0, prev_ref[0])
    nx = jnp.where(last, 0.0, next_ref[0])
    row = lax.broadcasted_iota(jnp.int32, (tm, 1), 0)
    xm1 = jnp.where(row == 0, pv[7:8], pltpu.roll(x, 1, 0))
    xm2 = jnp.where(row == 0, pv[6:7], jnp.where(row == 1, pv[7:8], pltpu.roll(x, 2, 0)))
    xp1 = jnp.where(row == tm - 1, nx[0:1], pltpu.roll(x, tm - 1, 0))
    cw = cw_ref[...]
    y = _silu(xm2 * cw[0:1] + xm1 * cw[1:2] + x * cw[2:3] + xp1 * cw[3:4])
    q = y[:, :C_W]
    k = y[:, C_W:2 * C_W]
    gs = gs_ref[...]
    q_ref[0] = q * lax.rsqrt(_dot_hi(q * q, gs) + LN_EPS) * (HEAD_DIM ** -0.5)
    k_ref[0] = k * lax.rsqrt(_dot_hi(k * k, gs) + LN_EPS)
    v_ref[0] = y[:, 2 * C_W:]
    ab = ab_ref[0]
    z = ab + dtb_ref[...]
    softplus = jnp.maximum(z, 0.0) + jnp.log(1.0 + jnp.exp(-jnp.abs(z)))
    lane = lax.broadcasted_iota(jnp.int32, ab.shape, 1)
    gb = jnp.where(lane < 2 * C_HEADS, -jnp.exp(alog_ref[...]) * softplus, _sigmoid(ab))
    ex = ex_ref[...]
    for p in range(2):
        g_ref[p, 0] = _dot_hi(gb, ex[p])
        bt_ref[p, 0] = _dot_hi(gb, ex[2 + p])


def _gdn_prep(qkvc, ab, conv_w, a_log, dt_bias, gs, n_lat, tm):
    b, t, _ = qkvc.shape
    c3 = 3 * C_W
    alog_row = jnp.zeros((1, LANE), F32).at[0, :2 * C_HEADS].set(a_log.reshape(-1))
    dtb_row = jnp.zeros((1, LANE), F32).at[0, :2 * C_HEADS].set(dt_bias.reshape(-1))
    ex = np.zeros((4, LANE, C_W), np.float32)
    for p in range(2):
        for h in range(C_HEADS):
            ex[p, p * C_HEADS + h, h * HEAD_DIM:(h + 1) * HEAD_DIM] = 1.0
            ex[2 + p, 2 * C_HEADS + p * C_HEADS + h, h * HEAD_DIM:(h + 1) * HEAD_DIM] = 1.0
    ex = jnp.asarray(ex)
    row = lambda bi, i: (bi, i, 0)
    const2 = lambda bi, i: (0, 0)
    hb = tm // 8
    last8 = t // 8 - 1
    tok = jax.ShapeDtypeStruct((b, t, C_W), F32)
    dirs = jax.ShapeDtypeStruct((2, b, t, C_W), F32)
    return pl.pallas_call(
        functools.partial(_gdn_prep_kernel, tm=tm, n_lat=n_lat, n_tot=t),
        grid=(b, t // tm),
        in_specs=[pl.BlockSpec((1, tm, c3), row),
                  pl.BlockSpec((1, 8, c3), lambda bi, i: (bi, jnp.maximum(i * hb - 1, 0), 0)),
                  pl.BlockSpec((1, 8, c3), lambda bi, i: (bi, jnp.minimum((i + 1) * hb, last8), 0)),
                  pl.BlockSpec((1, tm, LANE), row),
                  pl.BlockSpec((CONV_K, c3), const2),
                  pl.BlockSpec((1, LANE), const2), pl.BlockSpec((1, LANE), const2),
                  pl.BlockSpec(gs.shape, const2),
                  pl.BlockSpec(ex.shape, lambda bi, i: (0, 0, 0))],
        out_specs=[pl.BlockSpec((1, tm, C_W), row)] * 3
                  + [pl.BlockSpec((2, 1, tm, C_W), lambda bi, i: (0, bi, i, 0))] * 2,
        out_shape=[tok, tok, tok, dirs, dirs],
        compiler_params=_cparams(("parallel", "parallel")),
    )(qkvc, qkvc, qkvc, ab, conv_w, alog_row, dtb_row, gs, ex)


def _unit_tri_inverse(lm, eye, bd16, bd32):
    def mm(a, b_):
        return _dot(a.astype(BF16), b_.astype(BF16))

    ld = jnp.where(bd16, lm, 0.0)
    x = eye - ld
    pw = mm(ld, ld)
    x = x + mm(x, pw)
    pw = mm(pw, pw)
    x = x + mm(x, pw)
    pw = mm(pw, pw)
    x = x + mm(x, pw)
    l1 = jnp.where(jnp.logical_and(bd32, jnp.logical_not(bd16)), lm, 0.0)
    x = x - mm(x, mm(l1, x))
    l2 = jnp.where(bd32, 0.0, lm)
    return x - mm(x, mm(l2, x))


def _gdn_kernel(q_ref, k_ref, v_ref, g_ref, bt_ref, o_ref, s_ref, *, cb):
    p = pl.program_id(0)
    j = pl.program_id(3)

    @pl.when(j == 0)
    def _():
        s_ref[...] = jnp.zeros_like(s_ref)

    n = CHUNK
    fwd = p == 0
    ii = lax.broadcasted_iota(jnp.int32, (n, n), 0)
    jj = lax.broadcasted_iota(jnp.int32, (n, n), 1)
    ti = jnp.where(fwd, ii, jj)
    tj = jnp.where(fwd, jj, ii)
    incl = ti >= tj
    strict = ti > tj
    tri = incl.astype(F32)
    tri_t = (ti <= tj).astype(F32)
    ones = jnp.ones((n, n), F32)
    eye = (ii == jj).astype(F32)
    bd16 = (ii // 16) == (jj // 16)
    bd32 = (ii // 32) == (jj // 32)
    lane = lax.broadcasted_iota(jnp.int32, (1, 2 * HEAD_DIM), 1)
    hmask = [lane < HEAD_DIM, lane >= HEAD_DIM]
    r2 = lax.broadcasted_iota(jnp.int32, (2 * HEAD_DIM, 2 * HEAD_DIM), 0)
    c2 = lax.broadcasted_iota(jnp.int32, (2 * HEAD_DIM, 2 * HEAD_DIM), 1)
    blockdiag = (r2 < HEAD_DIM) == (c2 < HEAD_DIM)

    def mm(a, b_):
        return _dot(a.astype(BF16), b_.astype(BF16))

    for cc in range(cb):
        c = jnp.where(fwd, cc, cb - 1 - cc)
        rows = pl.ds(pl.multiple_of(c * n, n), n)
        q = q_ref[0, rows, :]
        k = k_ref[0, rows, :]
        v = v_ref[0, rows, :]
        g = g_ref[0, 0, rows, :]
        beta = bt_ref[0, 0, rows, :]
        gc = _dot_hi(tri, g)
        gl = _dot_hi(ones, g)
        eg = jnp.exp(gc)
        kb = k * beta
        vb = v * beta
        kbe = kb * eg
        u = jnp.zeros((n, 2 * HEAD_DIM), F32)
        w = jnp.zeros((n, 2 * HEAD_DIM), F32)
        intra = []
        for hh in range(2):
            hs = slice(hh * HEAD_DIM, (hh + 1) * HEAD_DIM)
            gc_row = _dot_hi(ones, g[:, hs] * tri_t)
            decay = jnp.exp(jnp.where(incl, gc[:, hs] - gc_row, NEG_BIG))
            kh = jnp.where(hmask[hh], k, 0.0).astype(BF16)
            lm = jnp.where(strict, _dot_nt(kb.astype(BF16), kh) * decay, 0.0)
            intra.append(jnp.where(incl, _dot_nt(q.astype(BF16), kh) * decay, 0.0))
            tm_ = _unit_tri_inverse(lm, eye, bd16, bd32)
            u = u + mm(tm_, jnp.where(hmask[hh], vb, 0.0))
            w = w + mm(tm_, jnp.where(hmask[hh], kbe, 0.0))
        s = s_ref[...]
        v_new = u - mm(w, s)
        o = mm(q * eg, s)
        for hh in range(2):
            o = o + mm(intra[hh], jnp.where(hmask[hh], v_new, 0.0))
        kdec = k * jnp.exp(gl - gc)
        upd = _dot_tn(kdec.astype(BF16), v_new.astype(BF16))
        s_ref[...] = s * jnp.exp(gl[0:1, :]) + jnp.where(blockdiag, upd, 0.0)
        o_ref[0, 0, rows, :] = o


def _gdn_scan(q, k, v, g, bt, n_lat, cb):
    b, t, _ = q.shape
    blk = cb * CHUNK
    nl = n_lat // blk
    nx = (t - n_lat) // blk
    hp = 2 * HEAD_DIM

    def tok_block(p, j):
        ctx = jnp.where(p == 0, nl + j, nl + nx - 1 - j)
        lat = jnp.where(p == 0, j - nx, nl - 1 - (j - nx))
        return jnp.where(j < nx, ctx, lat)

    tok = lambda p, bi, h, j: (bi, tok_block(p, j), h)
    dirs = lambda p, bi, h, j: (p, bi, tok_block(p, j), h)
    return pl.pallas_call(
        functools.partial(_gdn_kernel, cb=cb),
        grid=(2, b, C_W // hp, nl + nx),
        in_specs=[pl.BlockSpec((1, blk, hp), tok)] * 3 + [pl.BlockSpec((1, 1, blk, hp), dirs)] * 2,
        out_specs=pl.BlockSpec((1, 1, blk, hp), dirs),
        out_shape=jax.ShapeDtypeStruct((2, b, t, C_W), F32),
        scratch_shapes=[pltpu.VMEM((hp, hp), F32)],
        compiler_params=_cparams(("parallel", "parallel", "parallel", "arbitrary")),
    )(q, k, v, g, bt)


def _merge_kernel(x_ref, mod_ref, oa_ref, ob_ref, oc_ref, gate_ref, wmg_ref, wbr_ref, wout_ref, og_ref, gs_ref,
                  lng_ref, lnb_ref, o_ref, *, alpha):
    x = x_ref[0]
    d = x.shape[-1]
    mod = mod_ref[0, 0]
    h = (_ln(x) * (1.0 + mod[1:2]) + mod[0:1]).astype(BF16)
    gates = _sigmoid(_dot(h, wmg_ref[...]))
    oc = oc_ref[0, 0] + oc_ref[1, 0]
    ocn = oc * lax.rsqrt(_dot_hi(oc * oc, gs_ref[...]) + LN_EPS) * og_ref[...] * _silu(gate_ref[0])
    m = gates[:, :d] * _dot(oa_ref[0], wbr_ref[0])
    m = m + gates[:, d:2 * d] * _dot(ob_ref[0], wbr_ref[1])
    m = m + gates[:, 2 * d:] * _dot(ocn.astype(BF16), wbr_ref[2])
    mix = _dot(m.astype(BF16), wout_ref[...])
    y = alpha * x + mod[2:3] * mix
    o_ref[0] = _ln(y) * lng_ref[...] + lnb_ref[...]


def _merge(xa, mods, oa, ob, oc, gate_c, wmg, wbr, wout, og, gs, lng, lnb, n_lat, tm, alpha):
    b, t, d = xa.shape
    nl = n_lat // tm
    row = lambda bi, i: (bi, i, 0)
    const2 = lambda bi, i: (0, 0)
    return pl.pallas_call(
        functools.partial(_merge_kernel, alpha=alpha),
        grid=(b, t // tm),
        in_specs=[pl.BlockSpec((1, tm, d), row),
                  pl.BlockSpec((1, 1, N_MOD, d), lambda bi, i: (bi, (i >= nl).astype(jnp.int32), 0, 0)),
                  pl.BlockSpec((1, tm, BRANCH_W), row), pl.BlockSpec((1, tm, BRANCH_W), row),
                  pl.BlockSpec((2, 1, tm, BRANCH_W), lambda bi, i: (0, bi, i, 0)),
                  pl.BlockSpec((1, tm, BRANCH_W), row),
                  pl.BlockSpec(wmg.shape, const2),
                  pl.BlockSpec(wbr.shape, lambda bi, i: (0, 0, 0)),
                  pl.BlockSpec(wout.shape, const2),
                  pl.BlockSpec(og.shape, const2), pl.BlockSpec(gs.shape, const2),
                  pl.BlockSpec(lng.shape, const2), pl.BlockSpec(lnb.shape, const2)],
        out_specs=pl.BlockSpec((1, tm, d), row),
        out_shape=jax.ShapeDtypeStruct((b, t, d), F32),
        compiler_params=_cparams(("parallel", "parallel")),
    )(xa, mods, oa, ob, oc, gate_c, wmg, wbr, wout, og, gs, lng, lnb)


def _route_kernel(x_ref, mod_ref, wr_ref, h_ref, aff_ref, *, n_exp):
    x = x_ref[0]
    mod = mod_ref[0, 0]
    h = _ln(x) * (1.0 + mod[4:5]) + mod[3:4]
    h_ref[0] = h.astype(BF16)
    logits = _dot_hi(h, wr_ref[...])
    lane = lax.broadcasted_iota(jnp.int32, logits.shape, 1)
    logits = jnp.where(lane < n_exp, logits, NEG_BIG)
    e = jnp.exp(logits - jnp.max(logits, axis=-1, keepdims=True))
    aff_ref[0] = e / jnp.sum(e, axis=-1, keepdims=True)


def _route(xa, mods, wr, n_lat, tm, n_exp):
    b, t, d = xa.shape
    nl = n_lat // tm
    row = lambda bi, i: (bi, i, 0)
    return pl.pallas_call(
        functools.partial(_route_kernel, n_exp=n_exp),
        grid=(b, t // tm),
        in_specs=[pl.BlockSpec((1, tm, d), row),
                  pl.BlockSpec((1, 1, N_MOD, d), lambda bi, i: (bi, (i >= nl).astype(jnp.int32), 0, 0)),
                  pl.BlockSpec(wr.shape, lambda bi, i: (0, 0))],
        out_specs=[pl.BlockSpec((1, tm, d), row), pl.BlockSpec((1, tm, LANE), row)],
        out_shape=[jax.ShapeDtypeStruct((b, t, d), BF16), jax.ShapeDtypeStruct((b, t, LANE), F32)],
        compiler_params=_cparams(("parallel", "parallel")),
    )(xa, mods, wr)


def _ffn_kernel(x_ref, wg_ref, wu_ref, wd_ref, gate_ref, o_ref):
    f = pl.program_id(2)

    @pl.when(f == 0)
    def _():
        o_ref[...] = jnp.zeros_like(o_ref)

    x = x_ref[0]
    g = _dot(x, wg_ref[0].astype(BF16))
    u = _dot(x, wu_ref[0].astype(BF16))
    a = (_silu(g) * u).astype(BF16)
    o_ref[0] += _dot(a, wd_ref[0].astype(BF16))

    @pl.when(f == pl.num_programs(2) - 1)
    def _():
        o_ref[0] = o_ref[0] * gate_ref[0]


def _expert_ffn(xs, gate, w_gate_up, w_down, tm, tf):
    e, r, d = xs.shape
    f = w_down.shape[1]
    nf = f // tf
    return pl.pallas_call(
        _ffn_kernel,
        grid=(e, r // tm, nf),
        in_specs=[pl.BlockSpec((1, tm, d), lambda ei, i, fi: (ei, i, 0)),
                  pl.BlockSpec((1, d, tf), lambda ei, i, fi: (ei, 0, fi)),
                  pl.BlockSpec((1, d, tf), lambda ei, i, fi: (ei, 0, nf + fi)),
                  pl.BlockSpec((1, tf, d), lambda ei, i, fi: (ei, fi, 0)),
                  pl.BlockSpec((1, tm, 1), lambda ei, i, fi: (ei, i, 0))],
        out_specs=pl.BlockSpec((1, tm, d), lambda ei, i, fi: (ei, i, 0)),
        out_shape=jax.ShapeDtypeStruct((e, r, d), F32),
        compiler_params=_cparams(("parallel", "parallel", "arbitrary")),
    )(xs, w_gate_up, w_gate_up, w_down, gate)


def _post_kernel(x_ref, moe_ref, mod_ref, lng_ref, lnb_ref, o_ref, *, alpha):
    mod = mod_ref[0, 0]
    y = alpha * x_ref[0] + mod[5:6] * moe_ref[0]
    o_ref[0] = _ln(y) * lng_ref[...] + lnb_ref[...]


def _post(xa, moe, mods, lng, lnb, n_lat, tm, alpha):
    b, t, d = xa.shape
    nl = n_lat // tm
    row = lambda bi, i: (bi, i, 0)
    const2 = lambda bi, i: (0, 0)
    return pl.pallas_call(
        functools.partial(_post_kernel, alpha=alpha),
        grid=(b, t // tm),
        in_specs=[pl.BlockSpec((1, tm, d), row), pl.BlockSpec((1, tm, d), row),
                  pl.BlockSpec((1, 1, N_MOD, d), lambda bi, i: (bi, (i >= nl).astype(jnp.int32), 0, 0)),
                  pl.BlockSpec(lng.shape, const2), pl.BlockSpec(lnb.shape, const2)],
        out_specs=pl.BlockSpec((1, tm, d), row),
        out_shape=jax.ShapeDtypeStruct((b, t, d), F32),
        compiler_params=_cparams(("parallel", "parallel")),
    )(xa, moe, mods, lng, lnb)


def _moe(xa, mods, w_router, w_gate_up, w_down, n_lat, tm):
    b, t, d = xa.shape
    n_exp = w_router.shape[-1]
    m_ctx = t - n_lat
    wr = jnp.zeros((d, LANE), F32).at[:, :n_exp].set(w_router)
    h, aff = _route(xa, mods, wr, n_lat, tm, n_exp)
    aff = jnp.swapaxes(aff[:, :, :n_exp], 1, 2)
    cap_l = CAPACITY * n_lat // n_exp
    cap_x = CAPACITY * m_ctx // n_exp
    gate_l, idx_l = lax.top_k(aff[:, :, :n_lat], cap_l)
    gate_x, idx_x = lax.top_k(aff[:, :, n_lat:], cap_x)
    idx = jnp.concatenate([idx_l, idx_x + n_lat], axis=-1)
    gate = jnp.concatenate([gate_l, gate_x], axis=-1)
    cap = cap_l + cap_x
    idx_e = jnp.swapaxes(idx, 0, 1)
    bidx = jnp.arange(b)[None, :, None]
    xs = h[bidx, idx_e].reshape(n_exp, b * cap, d)
    gate_e = jnp.swapaxes(gate, 0, 1).reshape(n_exp, b * cap, 1)
    rows = b * cap
    tmr = rows
    for cand in (1088, 1024, 512, 256, 128, 64, 32, 16):
        if rows % cand == 0:
            tmr = cand
            break
    tf = min(512, w_down.shape[1])
    y = _expert_ffn(xs, gate_e, w_gate_up, w_down, tmr, tf).reshape(n_exp, b, cap, d)
    return jnp.zeros((b, t, d), F32).at[bidx, idx_e].add(y)


def _rope_tables(n_lat, m_ctx):
    t = np.arange(n_lat)
    rows = (t // GRID_W).astype(np.float32)
    cols = (t % GRID_W).astype(np.float32)
    n_freq = HEAD_DIM // 4
    inv_freq = jnp.asarray(ROPE_THETA, F32) ** (-jnp.arange(n_freq, dtype=F32) / n_freq)
    ang_r = jnp.asarray(rows)[:, None] * inv_freq
    ang_c = jnp.asarray(cols)[:, None] * inv_freq
    cr, sr, cc, sc = jnp.cos(ang_r), jnp.sin(ang_r), jnp.cos(ang_c), jnp.sin(ang_c)
    cos = jnp.concatenate([cr, cr, cc, cc], axis=-1)
    sin = jnp.concatenate([-sr, sr, -sc, sc], axis=-1)
    cos = jnp.concatenate([cos, jnp.ones((m_ctx, HEAD_DIM), F32)], axis=0)
    sin = jnp.concatenate([sin, jnp.zeros((m_ctx, HEAD_DIM), F32)], axis=0)
    return jnp.tile(cos, (1, A_HEADS)), jnp.tile(sin, (1, A_HEADS))


def kernel(x, c, ctx, c_ctx, w_mod, b_mod, w_in, qk_gain, rpb, conv_w, a_log, dt_bias, o_gain, w_branch, w_out,
           ln1_g, ln1_b, w_router, w_gate_up, w_down, ln2_g, ln2_b):
    b, n_lat, d = x.shape
    m_ctx = ctx.shape[1]
    depth = w_mod.shape[0]
    alpha = (2 * depth) ** 0.25
    tm = math.gcd(256, math.gcd(n_lat, m_ctx))
    cb = math.gcd(4, math.gcd(n_lat, m_ctx) // CHUNK)

    xa = jnp.concatenate([x, ctx], axis=1)
    n_rows = -(-(b + 1) // 8) * 8
    cc = jnp.zeros((n_rows, d), F32).at[:b].set(c).at[b].set(c_ctx)
    mod_all = _modulation(cc, w_mod, b_mod).reshape(depth, n_rows, N_MOD, d)
    cos, sin = _rope_tables(n_lat, m_ctx)
    gs_mean = _group_sum_matrix(A_Q_W, 1.0 / HEAD_DIM)
    gs_sum = _group_sum_matrix(C_W, 1.0)

    offs = np.cumsum((0, A_Q_W, A_KV_W, A_KV_W, B_W, B_W, B_W, 3 * C_W, C_W, 2 * C_HEADS, 2 * C_HEADS, N_BRANCH * d))
    for layer in range(depth):
        ml = mod_all[layer]
        mods = jnp.stack([ml[:b], jnp.broadcast_to(ml[b][None], (b, N_MOD, d))], axis=1)
        wl = w_in[layer]
        wa = wl[:, offs[0]:offs[3]].astype(BF16)
        wb = wl[:, offs[3]:offs[6]].astype(BF16)
        wc = wl[:, offs[6]:offs[8]].astype(BF16)
        wab = jnp.zeros((d, LANE), F32).at[:, :4 * C_HEADS].set(wl[:, offs[8]:offs[10]]).astype(BF16)
        wmg = wl[:, offs[10]:offs[11]].astype(BF16)
        gq = jnp.tile(qk_gain[layer, 0], A_HEADS)[None]
        gk = jnp.tile(qk_gain[layer, 1], A_KV_HEADS)[None]
        qa, ka, va, qb, kb, vb, qkvc, gate_c, ab = _in_projection(
            xa, mods, wa, wb, wc, wab, gq, gk, cos, sin, gs_mean, n_lat, tm)
        oa = _gqa(qa, ka, va, n_lat, min(128, tm), min(256, tm))
        ob = _neighborhood(qb, kb, vb, _na_bias_slabs(rpb[layer]), n_lat)
        qc, kc, vc, gx, bx = _gdn_prep(qkvc, ab, conv_w[layer], a_log[layer], dt_bias[layer], gs_sum, n_lat, tm)
        oc = _gdn_scan(qc, kc, vc, gx, bx, n_lat, cb)
        og = jnp.tile(o_gain[layer], C_HEADS)[None]
        xa = _merge(xa, mods, oa, ob, oc, gate_c, wmg, w_branch[layer].astype(BF16), w_out[layer].astype(BF16),
                    og, gs_mean, ln1_g[layer][None], ln1_b[layer][None], n_lat, tm, alpha)
        moe = _moe(xa, mods, w_router[layer], w_gate_up[layer], w_down[layer], n_lat, tm)
        xa = _post(xa, moe, mods, ln2_g[layer][None], ln2_b[layer][None], n_lat, tm, alpha)
    return xa[:, :n_lat]
```

```python
import functools
import math

import numpy as np
import jax
import jax.numpy as jnp
from jax import lax
from jax.experimental import pallas as pl
from jax.experimental.pallas import tpu as pltpu

F32 = jnp.float32
BF16 = jnp.bfloat16
HIGHEST = lax.Precision.HIGHEST

GRID_W = 64
HEAD_DIM = 64
A_HEADS = 8
A_KV_HEADS = 2
B_HEADS = 8
C_HEADS = 8
NA_ROWS = 8
NA_COLS = 16
CONV_K = 4
CHUNK = 64
N_BRANCH = 3
BRANCH_W = 512
CAPACITY = 2
N_MOD = 6
LN_EPS = 1e-6
ROPE_THETA = 10000.0
NEG_BIG = -1e30
LOG2E = 1.4426950408889634

A_Q_W = A_HEADS * HEAD_DIM
A_KV_W = A_KV_HEADS * HEAD_DIM
B_W = B_HEADS * HEAD_DIM
C_W = C_HEADS * HEAD_DIM
LANE = 128
VMEM_LIMIT = 56 * 1024 * 1024
GQA_TQ = 64


def _cparams(sem):
    return pltpu.CompilerParams(dimension_semantics=sem, vmem_limit_bytes=VMEM_LIMIT)


def _dot(a, b):
    return jnp.dot(a, b, preferred_element_type=F32)


def _dot_hi(a, b):
    return jnp.dot(a, b, precision=HIGHEST, preferred_element_type=F32)


def _dot_nt(a, b):
    return lax.dot_general(a, b, (((1,), (1,)), ((), ())), preferred_element_type=F32)


def _dot_tn(a, b):
    return lax.dot_general(a, b, (((0,), (0,)), ((), ())), preferred_element_type=F32)


def _ln(x):
    mu = jnp.mean(x, axis=-1, keepdims=True)
    xc = x - mu
    var = jnp.mean(xc * xc, axis=-1, keepdims=True)
    return xc * lax.rsqrt(var + LN_EPS)


def _sigmoid(x):
    return 1.0 / (1.0 + jnp.exp(-x))


def _silu(x):
    return x * _sigmoid(x)


def _group_sum_matrix(width, scale):
    g = np.arange(width) // HEAD_DIM
    return jnp.asarray((g[:, None] == g[None, :]).astype(np.float32) * scale)


def _mod_kernel(c_ref, w_ref, b_ref, o_ref):
    c = c_ref[...]
    o_ref[0] = _dot_hi(_silu(c), w_ref[0]) + b_ref[0]


def _modulation(cc, w_mod, b_mod):
    depth, d, dm = w_mod.shape
    r = cc.shape[0]
    tn = min(dm, 1536)
    return pl.pallas_call(
        _mod_kernel,
        grid=(depth, dm // tn),
        in_specs=[pl.BlockSpec((r, d), lambda l, j: (0, 0)),
                  pl.BlockSpec((1, d, tn), lambda l, j: (l, 0, j)),
                  pl.BlockSpec((1, 1, tn), lambda l, j: (l, 0, j))],
        out_specs=pl.BlockSpec((1, r, tn), lambda l, j: (l, 0, j)),
        out_shape=jax.ShapeDtypeStruct((depth, r, dm), F32),
        compiler_params=_cparams(("parallel", "parallel")),
        name="modulation",
    )(cc, w_mod, b_mod.reshape(depth, 1, dm))


def _rope_swap(y):
    w = y.shape[-1]
    lane = lax.broadcasted_iota(jnp.int32, y.shape, 1)
    up = pltpu.roll(y, w - 16, 1)
    down = pltpu.roll(y, 16, 1)
    return jnp.where((lane % 32) < 16, up, down)


def _inproj_kernel(x_ref, mod_ref, wa_ref, wb_ref, wc_ref, wab_ref, gq_ref, gk_ref, cos_ref, sin_ref,
                   gs_ref, qa_ref, ka_ref, va_ref, qb_ref, kb_ref, vb_ref, qkvc_ref, gatec_ref, ab_ref):
    x = x_ref[0]
    mod = mod_ref[0, 0]
    h = (_ln(x) * (1.0 + mod[1:2]) + mod[0:1]).astype(BF16)
    scale = HEAD_DIM ** -0.5 * LOG2E

    za = _dot(h, wa_ref[...])
    q = za[:, :A_Q_W]
    k = za[:, A_Q_W:A_Q_W + A_KV_W]
    v = za[:, A_Q_W + A_KV_W:]
    cos = cos_ref[...]
    sin = sin_ref[...]
    gs = gs_ref[...]
    qn = q * lax.rsqrt(_dot_hi(q * q, gs) + LN_EPS) * gq_ref[...]
    qn = qn * cos + _rope_swap(qn) * sin
    qa_ref[0] = (qn * scale).astype(BF16)
    kn = k * lax.rsqrt(_dot_hi(k * k, gs[:A_KV_W, :A_KV_W]) + LN_EPS) * gk_ref[...]
    kn = kn * cos[:, :A_KV_W] + _rope_swap(kn) * sin[:, :A_KV_W]
    for j in range(A_KV_HEADS):
        ka_ref[0, j] = kn[:, j * HEAD_DIM:(j + 1) * HEAD_DIM].astype(BF16)
        va_ref[0, j] = v[:, j * HEAD_DIM:(j + 1) * HEAD_DIM].astype(BF16)

    zb = _dot(h, wb_ref[...])
    qb_ref[0] = (zb[:, :B_W] * scale).astype(BF16)
    kb_ref[0] = zb[:, B_W:2 * B_W].astype(BF16)
    vb_ref[0] = zb[:, 2 * B_W:].astype(BF16)

    zc = _dot(h, wc_ref[...])
    qkvc_ref[0] = zc[:, :3 * C_W]
    gatec_ref[0] = zc[:, 3 * C_W:]
    ab_ref[0] = _dot(h, wab_ref[...])


def _in_projection(xa, mods, wa, wb, wc, wab, gq, gk, cos, sin, gs, n_lat, tm):
    b, t, d = xa.shape
    nl = n_lat // tm
    row = lambda bi, i: (bi, i, 0)
    const = lambda bi, i: (0, 0)
    outs = [
        (jax.ShapeDtypeStruct((b, t, A_Q_W), BF16), pl.BlockSpec((1, tm, A_Q_W), row)),
        (jax.ShapeDtypeStruct((b, A_KV_HEADS, t, HEAD_DIM), BF16),
         pl.BlockSpec((1, A_KV_HEADS, tm, HEAD_DIM), lambda bi, i: (bi, 0, i, 0))),
        (jax.ShapeDtypeStruct((b, A_KV_HEADS, t, HEAD_DIM), BF16),
         pl.BlockSpec((1, A_KV_HEADS, tm, HEAD_DIM), lambda bi, i: (bi, 0, i, 0))),
        (jax.ShapeDtypeStruct((b, t, B_W), BF16), pl.BlockSpec((1, tm, B_W), row)),
        (jax.ShapeDtypeStruct((b, t, B_W), BF16), pl.BlockSpec((1, tm, B_W), row)),
        (jax.ShapeDtypeStruct((b, t, B_W), BF16), pl.BlockSpec((1, tm, B_W), row)),
        (jax.ShapeDtypeStruct((b, t, 3 * C_W), F32), pl.BlockSpec((1, tm, 3 * C_W), row)),
        (jax.ShapeDtypeStruct((b, t, C_W), F32), pl.BlockSpec((1, tm, C_W), row)),
        (jax.ShapeDtypeStruct((b, t, LANE), F32), pl.BlockSpec((1, tm, LANE), row)),
    ]
    return pl.pallas_call(
        _inproj_kernel,
        grid=(b, t // tm),
        in_specs=[pl.BlockSpec((1, tm, d), row),
                  pl.BlockSpec((1, 1, N_MOD, d), lambda bi, i: (bi, (i >= nl).astype(jnp.int32), 0, 0)),
                  pl.BlockSpec(wa.shape, const), pl.BlockSpec(wb.shape, const),
                  pl.BlockSpec(wc.shape, const), pl.BlockSpec(wab.shape, const),
                  pl.BlockSpec(gq.shape, const), pl.BlockSpec(gk.shape, const),
                  pl.BlockSpec((tm, A_Q_W), lambda bi, i: (i, 0)),
                  pl.BlockSpec((tm, A_Q_W), lambda bi, i: (i, 0)),
                  pl.BlockSpec(gs.shape, const)],
        out_specs=[o[1] for o in outs],
        out_shape=[o[0] for o in outs],
        compiler_params=_cparams(("parallel", "parallel")),
        name="in_projection",
    )(xa, mods, wa, wb, wc, wab, gq, gk, cos, sin, gs)


def _gqa_kernel(q_ref, k_ref, v_ref, o_ref, *, tq, n_lat, n_tot):
    i = pl.program_id(2)
    g = A_HEADS // A_KV_HEADS
    q = jnp.concatenate([q_ref[0, :, j * HEAD_DIM:(j + 1) * HEAD_DIM] for j in range(g)], axis=0)

    def attend(k, v):
        s = _dot_nt(q, k)
        p = jnp.exp2(s - jnp.max(s, axis=-1, keepdims=True))
        l = jnp.sum(p, axis=-1, keepdims=True)
        o = (_dot(p.astype(BF16), v) / l).astype(BF16)
        for j in range(g):
            o_ref[0, :, j * HEAD_DIM:(j + 1) * HEAD_DIM] = o[j * tq:(j + 1) * tq]

    @pl.when(i < n_lat // tq)
    def _():
        attend(k_ref[0, 0], v_ref[0, 0])

    @pl.when(i >= n_lat // tq)
    def _():
        attend(k_ref[0, 0, n_lat:n_tot, :], v_ref[0, 0, n_lat:n_tot, :])


def _gqa(qa, ka, va, n_lat, tq):
    b, t, _ = qa.shape
    gw = A_Q_W // A_KV_HEADS
    return pl.pallas_call(
        functools.partial(_gqa_kernel, tq=tq, n_lat=n_lat, n_tot=t),
        grid=(b, A_KV_HEADS, t // tq),
        in_specs=[pl.BlockSpec((1, tq, gw), lambda bi, kh, i: (bi, i, kh)),
                  pl.BlockSpec((1, 1, t, HEAD_DIM), lambda bi, kh, i: (bi, kh, 0, 0)),
                  pl.BlockSpec((1, 1, t, HEAD_DIM), lambda bi, kh, i: (bi, kh, 0, 0))],
        out_specs=pl.BlockSpec((1, tq, gw), lambda bi, kh, i: (bi, i, kh)),
        out_shape=jax.ShapeDtypeStruct((b, t, A_Q_W), BF16),
        compiler_params=_cparams(("parallel", "parallel", "parallel")),
        name="gqa",
    )(qa, ka, va)


def _na_bias_slabs(rpb):
    cols = np.arange(GRID_W)
    col_start = np.clip(cols - NA_COLS // 2, 0, GRID_W - NA_COLS)
    cj = np.arange(GRID_W)
    valid = (cj[None, :] >= col_start[:, None]) & (cj[None, :] < col_start[:, None] + NA_COLS)
    off = np.clip(cj[None, :] - cols[:, None] + (NA_COLS - 1), 0, 2 * NA_COLS - 2)
    slabs = []
    for o in range(NA_ROWS):
        rows = rpb[:, o:o + NA_ROWS] * LOG2E
        bias = rows[:, :, off]
        bias = jnp.where(valid[None, None], bias, NEG_BIG)
        slabs.append(bias.transpose(0, 2, 1, 3).reshape(rpb.shape[0], GRID_W, NA_ROWS * GRID_W))
    slabs.append(jnp.full_like(slabs[0], NEG_BIG))
    return jnp.stack(slabs, axis=1).astype(F32)


def _na_kernel(q_ref, k_ref, v_ref, bias_ref, o_ref, *, n_lat, n_tot):
    r = pl.program_id(1)
    rows = n_lat // GRID_W
    win = NA_ROWS * GRID_W
    rs = jnp.clip(r - NA_ROWS // 2, 0, rows - NA_ROWS)
    nb = pl.ds(pl.multiple_of(rs * GRID_W, GRID_W), win)
    for h in range(B_HEADS):
        hs = slice(h * HEAD_DIM, (h + 1) * HEAD_DIM)
        q = q_ref[0, :, hs]
        s_nb = _dot_nt(q, k_ref[0, nb, hs]) + bias_ref[h, 0]
        s_cx = _dot_nt(q, k_ref[0, n_lat:n_tot, hs])
        m = jnp.maximum(jnp.max(s_nb, axis=-1, keepdims=True), jnp.max(s_cx, axis=-1, keepdims=True))
        p_nb = jnp.exp2(s_nb - m)
        p_cx = jnp.exp2(s_cx - m)
        l = jnp.sum(p_nb, axis=-1, keepdims=True) + jnp.sum(p_cx, axis=-1, keepdims=True)
        o = _dot(p_nb.astype(BF16), v_ref[0, nb, hs]) + _dot(p_cx.astype(BF16), v_ref[0, n_lat:n_tot, hs])
        o_ref[0, :, hs] = (o / l).astype(BF16)


def _neighborhood(qb, kb, vb, slabs, n_lat):
    b, t, _ = qb.shape
    rows = n_lat // GRID_W
    n_slab = slabs.shape[1]

    def slab_index(bi, r):
        rs = jnp.clip(r - NA_ROWS // 2, 0, rows - NA_ROWS)
        o = rs - r + (NA_ROWS - 1)
        return (0, jnp.where(r >= rows, n_slab - 1, o), 0, 0)

    return pl.pallas_call(
        functools.partial(_na_kernel, n_lat=n_lat, n_tot=t),
        grid=(b, t // GRID_W),
        in_specs=[pl.BlockSpec((1, GRID_W, B_W), lambda bi, r: (bi, r, 0)),
                  pl.BlockSpec((1, t, B_W), lambda bi, r: (bi, 0, 0)),
                  pl.BlockSpec((1, t, B_W), lambda bi, r: (bi, 0, 0)),
                  pl.BlockSpec((B_HEADS, 1, GRID_W, NA_ROWS * GRID_W), slab_index)],
        out_specs=pl.BlockSpec((1, GRID_W, B_W), lambda bi, r: (bi, r, 0)),
        out_shape=jax.ShapeDtypeStruct((b, t, B_W), BF16),
        compiler_params=_cparams(("parallel", "parallel")),
        name="neighborhood",
    )(qb, kb, vb, slabs)


GC_LANE, GL_LANE, BETA_LANE = 0, 2 * C_HEADS, 4 * C_HEADS


def _gdn_prep_kernel(x_ref, prev_ref, next_ref, ab_ref, cw_ref, alog_ref, dtb_ref, gs_ref,
                     q_ref, k_ref, v_ref, comp_ref, *, tm, n_lat, n_tot):
    i = pl.program_id(1)
    nl = n_lat // tm
    nt = n_tot // tm
    first = jnp.logical_or(i == 0, i == nl)
    last = jnp.logical_or(i == nl - 1, i == nt - 1)
    x = x_ref[0]
    pv = jnp.where(first, 0.0, prev_ref[0])
    nx = jnp.where(last, 0.0, next_ref[0])
    row = lax.broadcasted_iota(jnp.int32, (tm, 1), 0)
    xm1 = jnp.where(row == 0, pv[7:8], pltpu.roll(x, 1, 0))
    xm2 = jnp.where(row == 0, pv[6:7], jnp.where(row == 1, pv[7:8], pltpu.roll(x, 2, 0)))
    xp1 = jnp.where(row == tm - 1, nx[0:1], pltpu.roll(x, tm - 1, 0))
    cw = cw_ref[...]
    y = _silu(xm2 * cw[0:1] + xm1 * cw[1:2] + x * cw[2:3] + xp1 * cw[3:4])
    q = y[:, :C_W]
    k = y[:, C_W:2 * C_W]
    gs = gs_ref[...]
    q_ref[0] = q * lax.rsqrt(_dot_hi(q * q, gs) + LN_EPS) * (HEAD_DIM ** -0.5)
    k_ref[0] = k * lax.rsqrt(_dot_hi(k * k, gs) + LN_EPS)
    v_ref[0] = y[:, 2 * C_W:]
    ab = ab_ref[0]
    z = ab + dtb_ref[...]
    softplus = jnp.maximum(z, 0.0) + jnp.log(1.0 + jnp.exp(-jnp.abs(z)))
    lane = lax.broadcasted_iota(jnp.int32, ab.shape, 1)
    gb = jnp.where(lane < 2 * C_HEADS, -jnp.exp(alog_ref[...]) * softplus, _sigmoid(ab))
    ii = lax.broadcasted_iota(jnp.int32, (tm, tm), 0)
    jj = lax.broadcasted_iota(jnp.int32, (tm, tm), 1)
    same = (ii // CHUNK) == (jj // CHUNK)
    prefix = _dot_hi(jnp.logical_and(same, jj <= ii).astype(F32), gb)
    suffix = _dot_hi(jnp.logical_and(same, jj >= ii).astype(F32), gb)
    total = _dot_hi(same.astype(F32), gb)
    comp = jnp.where(lane < C_HEADS, prefix,
                     jnp.where(lane < GL_LANE, suffix,
                               jnp.where(lane < BETA_LANE, pltpu.roll(total, GL_LANE, 1),
                                         jnp.where(lane < BETA_LANE + 2 * C_HEADS, pltpu.roll(gb, 2 * C_HEADS, 1),
                                                   0.0))))
    comp_ref[0] = comp


def _gdn_prep(qkvc, ab, conv_w, a_log, dt_bias, gs, n_lat, tm):
    b, t, _ = qkvc.shape
    c3 = 3 * C_W
    alog_row = jnp.zeros((1, LANE), F32).at[0, :2 * C_HEADS].set(a_log.reshape(-1))
    dtb_row = jnp.zeros((1, LANE), F32).at[0, :2 * C_HEADS].set(dt_bias.reshape(-1))
    row = lambda bi, i: (bi, i, 0)
    const2 = lambda bi, i: (0, 0)
    hb = tm // 8
    last8 = t // 8 - 1
    tok = jax.ShapeDtypeStruct((b, t, C_W), F32)
    return pl.pallas_call(
        functools.partial(_gdn_prep_kernel, tm=tm, n_lat=n_lat, n_tot=t),
        grid=(b, t // tm),
        in_specs=[pl.BlockSpec((1, tm, c3), row),
                  pl.BlockSpec((1, 8, c3), lambda bi, i: (bi, jnp.maximum(i * hb - 1, 0), 0)),
                  pl.BlockSpec((1, 8, c3), lambda bi, i: (bi, jnp.minimum((i + 1) * hb, last8), 0)),
                  pl.BlockSpec((1, tm, LANE), row),
                  pl.BlockSpec((CONV_K, c3), const2),
                  pl.BlockSpec((1, LANE), const2), pl.BlockSpec((1, LANE), const2),
                  pl.BlockSpec(gs.shape, const2)],
        out_specs=[pl.BlockSpec((1, tm, C_W), row)] * 3 + [pl.BlockSpec((1, tm, LANE), row)],
        out_shape=[tok, tok, tok, jax.ShapeDtypeStruct((b, t, LANE), F32)],
        compiler_params=_cparams(("parallel", "parallel")),
        name="gdn_prep",
    )(qkvc, qkvc, qkvc, ab, conv_w, alog_row, dtb_row, gs)


def _unit_tri_inverse(lm, eye, bd16, bd32):
    def mm(a, b_):
        return _dot(a.astype(BF16), b_.astype(BF16))

    ld = jnp.where(bd16, lm, 0.0)
    x = eye - ld
    pw = mm(ld, ld)
    x = x + mm(x, pw)
    pw = mm(pw, pw)
    x = x + mm(x, pw)
    pw = mm(pw, pw)
    x = x + mm(x, pw)
    l1 = jnp.where(jnp.logical_and(bd32, jnp.logical_not(bd16)), lm, 0.0)
    x = x - mm(x, mm(l1, x))
    l2 = jnp.where(bd32, 0.0, lm)
    return x - mm(x, mm(l2, x))


def _gdn_local_kernel(q_ref, k_ref, v_ref, comp_ref, ex_ref, sel_ref, u_ref, w_ref, kd_ref, qd_ref, in_ref):
    n = q_ref.shape[1]
    hp = 2 * HEAD_DIM
    q = q_ref[0]
    k = k_ref[0]
    v = v_ref[0]
    e = _dot_hi(comp_ref[0], ex_ref[0])
    ii = lax.broadcasted_iota(jnp.int32, (n, n), 0)
    jj = lax.broadcasted_iota(jnp.int32, (n, n), 1)
    same = (ii // CHUNK) == (jj // CHUNK)
    eye = (ii == jj).astype(F32)
    bd16 = (ii // 16) == (jj // 16)
    bd32 = (ii // 32) == (jj // 32)
    lane = lax.broadcasted_iota(jnp.int32, (1, hp), 1)
    lane2 = lax.broadcasted_iota(jnp.int32, (1, 2 * hp), 1)
    for p in range(2):
        gc = e[:, p * hp:(p + 1) * hp]
        gl = e[:, (2 + p) * hp:(3 + p) * hp]
        beta = e[:, (4 + p) * hp:(5 + p) * hp]
        eg = jnp.exp(gc)
        kb = k * beta
        kd_ref[p, 0] = (k * jnp.exp(gl - gc)).astype(BF16)
        qd_ref[p, 0] = (q * eg).astype(BF16)
        rhs = jnp.concatenate([v * beta, kb * eg], axis=1)
        kbq = jnp.concatenate([kb, q], axis=0).astype(BF16)
        incl = jnp.logical_and(same, (ii >= jj) if p == 0 else (ii <= jj))
        strict = jnp.logical_and(same, (ii > jj) if p == 0 else (ii < jj))
        uw = jnp.zeros((n, 2 * hp), F32)
        intra2 = jnp.zeros((n, hp), F32)
        for hh in range(2):
            in_head = (lane // HEAD_DIM) == hh
            in_head2 = ((lane2 % hp) // HEAD_DIM) == hh
            gcb = jnp.broadcast_to(gc[:, hh * HEAD_DIM:hh * HEAD_DIM + 1], (n, n))
            decay = jnp.exp(jnp.where(incl, gcb - gcb.T, NEG_BIG))
            aq = _dot_nt(kbq, jnp.where(in_head, k, 0.0).astype(BF16))
            lm = jnp.where(strict, aq[:n] * decay, 0.0)
            intra = jnp.where(incl, aq[n:] * decay, 0.0)
            tmat = _unit_tri_inverse(lm, eye, bd16, bd32)
            uw = uw + _dot(tmat.astype(BF16), jnp.where(in_head2, rhs, 0.0).astype(BF16))
            intra2 = intra2 + _dot(intra.astype(BF16), sel_ref[hh])
        u_ref[p, 0] = uw[:, :hp].astype(BF16)
        w_ref[p, 0] = uw[:, hp:].astype(BF16)
        in_ref[p, 0] = intra2.astype(BF16)


def _gdn_local(q, k, v, comp, blk):
    b, t, _ = q.shape
    hp = 2 * HEAD_DIM
    n_pair = C_W // hp
    ex = np.zeros((n_pair, LANE, 6 * hp), np.float32)
    for pr in range(n_pair):
        for p in range(2):
            for hh in range(2):
                src = p * C_HEADS + pr * 2 + hh
                for kind, base in enumerate((GC_LANE, GL_LANE, BETA_LANE)):
                    col = (2 * kind + p) * hp + hh * HEAD_DIM
                    ex[pr, base + src, col:col + HEAD_DIM] = 1.0
    sel = np.zeros((2, blk, hp), np.float32)
    for hh in range(2):
        sel[hh, np.arange(blk), hh * HEAD_DIM + np.arange(blk) % CHUNK] = 1.0
    tok = lambda bi, pr, i: (bi, i, pr)
    dirs = lambda bi, pr, i: (0, bi, i, pr)
    out = jax.ShapeDtypeStruct((2, b, t, C_W), BF16)
    return pl.pallas_call(
        _gdn_local_kernel,
        grid=(b, n_pair, t // blk),
        in_specs=[pl.BlockSpec((1, blk, hp), tok)] * 3
                 + [pl.BlockSpec((1, blk, LANE), lambda bi, pr, i: (bi, i, 0)),
                    pl.BlockSpec((1, LANE, 6 * hp), lambda bi, pr, i: (pr, 0, 0)),
                    pl.BlockSpec((2, blk, hp), lambda bi, pr, i: (0, 0, 0))],
        out_specs=[pl.BlockSpec((2, 1, blk, hp), dirs)] * 5,
        out_shape=[out] * 5,
        compiler_params=_cparams(("parallel", "parallel", "parallel")),
        name="gdn_local",
    )(q, k, v, comp, jnp.asarray(ex), jnp.asarray(sel, dtype=BF16))


def _gdn_scan_kernel(*refs, ncb):
    ins = (refs[0:6], refs[6:12])
    ex_ref = refs[12]
    outs = refs[13:15]
    s_ref = refs[15]
    j = pl.program_id(1)

    @pl.when(j == 0)
    def _():
        s_ref[...] = jnp.zeros_like(s_ref)

    gw = 4 * HEAD_DIM
    n_grp = C_W // gw
    lane_head = lax.broadcasted_iota(jnp.int32, (1, gw), 1) // HEAD_DIM
    rr = lax.broadcasted_iota(jnp.int32, (gw, gw), 0) // HEAD_DIM
    cc_ = lax.broadcasted_iota(jnp.int32, (gw, gw), 1) // HEAD_DIM
    blockdiag = rr == cc_
    for cc in range(ncb):
        for p in range(2):
            u_ref, w_ref, kd_ref, qd_ref, in_ref, comp_ref = ins[p]
            c = cc if p == 0 else ncb - 1 - cc
            rows = slice(c * CHUNK, (c + 1) * CHUNK)
            sdec = jnp.exp(_dot_hi(comp_ref[0, c * CHUNK:c * CHUNK + 8, :], ex_ref[...]))
            for g in range(n_grp):
                ls = slice(g * gw, (g + 1) * gw)
                s = s_ref[p, g]
                sb = s.astype(BF16)
                v_new = u_ref[0, 0, rows, ls].astype(F32) - _dot(w_ref[0, 0, rows, ls], sb)
                vstack = jnp.concatenate(
                    [jnp.where(lane_head == h, v_new, 0.0).astype(BF16) for h in range(gw // HEAD_DIM)], axis=0)
                o = _dot(qd_ref[0, 0, rows, ls], sb) + _dot(in_ref[0, 0, rows, ls], vstack)
                upd = _dot_tn(kd_ref[0, 0, rows, ls], v_new.astype(BF16))
                s_ref[p, g] = s * sdec[0:1, p * C_W + g * gw:p * C_W + (g + 1) * gw] + jnp.where(blockdiag, upd, 0.0)
                outs[p][0, rows, ls] = o


def _gdn_scan(u, w, kd, qd, intra, comp, n_lat, blk):
    _, b, t, _ = u.shape
    nl = n_lat // blk
    nx = (t - n_lat) // blk
    ncb = blk // CHUNK
    ex = np.zeros((LANE, 2 * C_W), np.float32)
    for p in range(2):
        for h in range(C_HEADS):
            ex[GL_LANE + p * C_HEADS + h, p * C_W + h * HEAD_DIM:p * C_W + (h + 1) * HEAD_DIM] = 1.0

    def blk0(j):
        return jnp.where(j < nx, nl + j, j - nx)

    def blk1(j):
        return jnp.where(j < nx, nl + nx - 1 - j, nl - 1 - (j - nx))

    specs = []
    for p, bf in enumerate((blk0, blk1)):
        specs += [pl.BlockSpec((1, 1, blk, C_W), lambda bi, j, p=p, bf=bf: (p, bi, bf(j), 0))] * 5
        specs += [pl.BlockSpec((1, blk, LANE), lambda bi, j, bf=bf: (bi, bf(j), 0))]
    specs += [pl.BlockSpec((LANE, 2 * C_W), lambda bi, j: (0, 0))]
    out = jax.ShapeDtypeStruct((b, t, C_W), F32)
    gw = 4 * HEAD_DIM
    return pl.pallas_call(
        functools.partial(_gdn_scan_kernel, ncb=ncb),
        grid=(b, nl + nx),
        in_specs=specs,
        out_specs=[pl.BlockSpec((1, blk, C_W), lambda bi, j: (bi, blk0(j), 0)),
                   pl.BlockSpec((1, blk, C_W), lambda bi, j: (bi, blk1(j), 0))],
        out_shape=[out, out],
        scratch_shapes=[pltpu.VMEM((2, C_W // gw, gw, gw), F32)],
        compiler_params=_cparams(("parallel", "arbitrary")),
        name="gdn_scan",
    )(u, w, kd, qd, intra, comp, u, w, kd, qd, intra, comp, jnp.asarray(ex))


def _merge_kernel(x_ref, mod_ref, oa_ref, ob_ref, oc0_ref, oc1_ref, gate_ref, wmg_ref, wbr_ref, wout_ref, og_ref,
                  gs_ref, lng_ref, lnb_ref, o_ref, *, alpha):
    x = x_ref[0]
    d = x.shape[-1]
    mod = mod_ref[0, 0]
    h = (_ln(x) * (1.0 + mod[1:2]) + mod[0:1]).astype(BF16)
    gates = _sigmoid(_dot(h, wmg_ref[...]))
    oc = oc0_ref[0] + oc1_ref[0]
    ocn = oc * lax.rsqrt(_dot_hi(oc * oc, gs_ref[...]) + LN_EPS) * og_ref[...] * _silu(gate_ref[0])
    m = gates[:, :d] * _dot(oa_ref[0], wbr_ref[0])
    m = m + gates[:, d:2 * d] * _dot(ob_ref[0], wbr_ref[1])
    m = m + gates[:, 2 * d:] * _dot(ocn.astype(BF16), wbr_ref[2])
    mix = _dot(m.astype(BF16), wout_ref[...])
    y = alpha * x + mod[2:3] * mix
    o_ref[0] = _ln(y) * lng_ref[...] + lnb_ref[...]


def _merge(xa, mods, oa, ob, oc0, oc1, gate_c, wmg, wbr, wout, og, gs, lng, lnb, n_lat, tm, alpha):
    b, t, d = xa.shape
    nl = n_lat // tm
    row = lambda bi, i: (bi, i, 0)
    const2 = lambda bi, i: (0, 0)
    return pl.pallas_call(
        functools.partial(_merge_kernel, alpha=alpha),
        grid=(b, t // tm),
        in_specs=[pl.BlockSpec((1, tm, d), row),
                  pl.BlockSpec((1, 1, N_MOD, d), lambda bi, i: (bi, (i >= nl).astype(jnp.int32), 0, 0)),
                  pl.BlockSpec((1, tm, BRANCH_W), row), pl.BlockSpec((1, tm, BRANCH_W), row),
                  pl.BlockSpec((1, tm, BRANCH_W), row), pl.BlockSpec((1, tm, BRANCH_W), row),
                  pl.BlockSpec((1, tm, BRANCH_W), row),
                  pl.BlockSpec(wmg.shape, const2),
                  pl.BlockSpec(wbr.shape, lambda bi, i: (0, 0, 0)),
                  pl.BlockSpec(wout.shape, const2),
                  pl.BlockSpec(og.shape, const2), pl.BlockSpec(gs.shape, const2),
                  pl.BlockSpec(lng.shape, const2), pl.BlockSpec(lnb.shape, const2)],
        out_specs=pl.BlockSpec((1, tm, d), row),
        out_shape=jax.ShapeDtypeStruct((b, t, d), F32),
        compiler_params=_cparams(("parallel", "parallel")),
        name="merge",
    )(xa, mods, oa, ob, oc0, oc1, gate_c, wmg, wbr, wout, og, gs, lng, lnb)


def _route_kernel(x_ref, mod_ref, wr_ref, h_ref, aff_ref, *, n_exp):
    x = x_ref[0]
    mod = mod_ref[0, 0]
    h = _ln(x) * (1.0 + mod[4:5]) + mod[3:4]
    h_ref[0] = h.astype(BF16)
    logits = _dot_hi(h, wr_ref[...])
    lane = lax.broadcasted_iota(jnp.int32, logits.shape, 1)
    logits = jnp.where(lane < n_exp, logits, NEG_BIG)
    e = jnp.exp(logits - jnp.max(logits, axis=-1, keepdims=True))
    aff_ref[0] = e / jnp.sum(e, axis=-1, keepdims=True)


def _route(xa, mods, wr, n_lat, tm, n_exp):
    b, t, d = xa.shape
    nl = n_lat // tm
    row = lambda bi, i: (bi, i, 0)
    return pl.pallas_call(
        functools.partial(_route_kernel, n_exp=n_exp),
        grid=(b, t // tm),
        in_specs=[pl.BlockSpec((1, tm, d), row),
                  pl.BlockSpec((1, 1, N_MOD, d), lambda bi, i: (bi, (i >= nl).astype(jnp.int32), 0, 0)),
                  pl.BlockSpec(wr.shape, lambda bi, i: (0, 0))],
        out_specs=[pl.BlockSpec((1, tm, d), row), pl.BlockSpec((1, tm, LANE), row)],
        out_shape=[jax.ShapeDtypeStruct((b, t, d), BF16), jax.ShapeDtypeStruct((b, t, LANE), F32)],
        compiler_params=_cparams(("parallel", "parallel")),
        name="route",
    )(xa, mods, wr)


def _ffn_kernel(x_ref, wg_ref, wu_ref, wd_ref, gate_ref, o_ref):
    f = pl.program_id(2)

    @pl.when(f == 0)
    def _():
        o_ref[...] = jnp.zeros_like(o_ref)

    x = x_ref[0]
    g = _dot(x, wg_ref[0].astype(BF16))
    u = _dot(x, wu_ref[0].astype(BF16))
    a = (_silu(g) * u).astype(BF16)
    o_ref[0] += _dot(a, wd_ref[0].astype(BF16))

    @pl.when(f == pl.num_programs(2) - 1)
    def _():
        o_ref[0] = o_ref[0] * gate_ref[0]


def _expert_ffn(xs, gate, w_gate_up, w_down, tm, tf):
    e, r, d = xs.shape
    f = w_down.shape[1]
    nf = f // tf
    return pl.pallas_call(
        _ffn_kernel,
        grid=(e, r // tm, nf),
        in_specs=[pl.BlockSpec((1, tm, d), lambda ei, i, fi: (ei, i, 0)),
                  pl.BlockSpec((1, d, tf), lambda ei, i, fi: (ei, 0, fi)),
                  pl.BlockSpec((1, d, tf), lambda ei, i, fi: (ei, 0, nf + fi)),
                  pl.BlockSpec((1, tf, d), lambda ei, i, fi: (ei, fi, 0)),
                  pl.BlockSpec((1, tm, 1), lambda ei, i, fi: (ei, i, 0))],
        out_specs=pl.BlockSpec((1, tm, d), lambda ei, i, fi: (ei, i, 0)),
        out_shape=jax.ShapeDtypeStruct((e, r, d), F32),
        compiler_params=_cparams(("parallel", "parallel", "arbitrary")),
        name="expert_ffn",
    )(xs, w_gate_up, w_gate_up, w_down, gate)


def _post_kernel(x_ref, moe_ref, mod_ref, lng_ref, lnb_ref, o_ref, *, alpha):
    mod = mod_ref[0, 0]
    y = alpha * x_ref[0] + mod[5:6] * moe_ref[0]
    o_ref[0] = _ln(y) * lng_ref[...] + lnb_ref[...]


def _post(xa, moe, mods, lng, lnb, n_lat, tm, alpha):
    b, t, d = xa.shape
    nl = n_lat // tm
    row = lambda bi, i: (bi, i, 0)
    const2 = lambda bi, i: (0, 0)
    return pl.pallas_call(
        functools.partial(_post_kernel, alpha=alpha),
        grid=(b, t // tm),
        in_specs=[pl.BlockSpec((1, tm, d), row), pl.BlockSpec((1, tm, d), row),
                  pl.BlockSpec((1, 1, N_MOD, d), lambda bi, i: (bi, (i >= nl).astype(jnp.int32), 0, 0)),
                  pl.BlockSpec(lng.shape, const2), pl.BlockSpec(lnb.shape, const2)],
        out_specs=pl.BlockSpec((1, tm, d), row),
        out_shape=jax.ShapeDtypeStruct((b, t, d), F32),
        compiler_params=_cparams(("parallel", "parallel")),
        name="post_norm",
    )(xa, moe, mods, lng, lnb)


def _moe(xa, mods, w_router, w_gate_up, w_down, n_lat, tm):
    b, t, d = xa.shape
    n_exp = w_router.shape[-1]
    m_ctx = t - n_lat
    wr = jnp.zeros((d, LANE), F32).at[:, :n_exp].set(w_router)
    h, aff = _route(xa, mods, wr, n_lat, tm, n_exp)
    aff = jnp.swapaxes(aff[:, :, :n_exp], 1, 2)
    cap_l = CAPACITY * n_lat // n_exp
    cap_x = CAPACITY * m_ctx // n_exp
    gate_l, idx_l = lax.top_k(aff[:, :, :n_lat], cap_l)
    gate_x, idx_x = lax.top_k(aff[:, :, n_lat:], cap_x)
    idx = jnp.concatenate([idx_l, idx_x + n_lat], axis=-1)
    gate = jnp.concatenate([gate_l, gate_x], axis=-1)
    cap = cap_l + cap_x
    idx_e = jnp.swapaxes(idx, 0, 1)
    bidx = jnp.arange(b)[None, :, None]
    xs = h[bidx, idx_e].reshape(n_exp, b * cap, d)
    gate_e = jnp.swapaxes(gate, 0, 1).reshape(n_exp, b * cap, 1)
    rows = b * cap
    tmr = rows
    for cand in (1088, 1024, 512, 256, 128, 64, 32, 16):
        if rows % cand == 0:
            tmr = cand
            break
    tf = min(512, w_down.shape[1])
    y = _expert_ffn(xs, gate_e, w_gate_up, w_down, tmr, tf).reshape(n_exp, b, cap, d)
    return jnp.zeros((b, t, d), F32).at[bidx, idx_e].add(y)


def _rope_tables(n_lat, m_ctx):
    t = np.arange(n_lat)
    rows = (t // GRID_W).astype(np.float32)
    cols = (t % GRID_W).astype(np.float32)
    n_freq = HEAD_DIM // 4
    inv_freq = jnp.asarray(ROPE_THETA, F32) ** (-jnp.arange(n_freq, dtype=F32) / n_freq)
    ang_r = jnp.asarray(rows)[:, None] * inv_freq
    ang_c = jnp.asarray(cols)[:, None] * inv_freq
    cr, sr, cc, sc = jnp.cos(ang_r), jnp.sin(ang_r), jnp.cos(ang_c), jnp.sin(ang_c)
    cos = jnp.concatenate([cr, cr, cc, cc], axis=-1)
    sin = jnp.concatenate([-sr, sr, -sc, sc], axis=-1)
    cos = jnp.concatenate([cos, jnp.ones((m_ctx, HEAD_DIM), F32)], axis=0)
    sin = jnp.concatenate([sin, jnp.zeros((m_ctx, HEAD_DIM), F32)], axis=0)
    return jnp.tile(cos, (1, A_HEADS)), jnp.tile(sin, (1, A_HEADS))


def kernel(x, c, ctx, c_ctx, w_mod, b_mod, w_in, qk_gain, rpb, conv_w, a_log, dt_bias, o_gain, w_branch, w_out,
           ln1_g, ln1_b, w_router, w_gate_up, w_down, ln2_g, ln2_b):
    b, n_lat, d = x.shape
    m_ctx = ctx.shape[1]
    depth = w_mod.shape[0]
    alpha = (2 * depth) ** 0.25
    tm = math.gcd(256, math.gcd(n_lat, m_ctx))
    blk = tm

    xa = jnp.concatenate([x, ctx], axis=1)
    n_rows = -(-(b + 1) // 8) * 8
    cc = jnp.zeros((n_rows, d), F32).at[:b].set(c).at[b].set(c_ctx)
    mod_all = _modulation(cc, w_mod, b_mod).reshape(depth, n_rows, N_MOD, d)
    cos, sin = _rope_tables(n_lat, m_ctx)
    gs_mean = _group_sum_matrix(A_Q_W, 1.0 / HEAD_DIM)
    gs_sum = _group_sum_matrix(C_W, 1.0)

    offs = np.cumsum((0, A_Q_W, A_KV_W, A_KV_W, B_W, B_W, B_W, 3 * C_W, C_W, 2 * C_HEADS, 2 * C_HEADS, N_BRANCH * d))
    for layer in range(depth):
        ml = mod_all[layer]
        mods = jnp.stack([ml[:b], jnp.broadcast_to(ml[b][None], (b, N_MOD, d))], axis=1)
        wl = w_in[layer]
        wa = wl[:, offs[0]:offs[3]].astype(BF16)
        wb = wl[:, offs[3]:offs[6]].astype(BF16)
        wc = wl[:, offs[6]:offs[8]].astype(BF16)
        wab = jnp.zeros((d, LANE), F32).at[:, :4 * C_HEADS].set(wl[:, offs[8]:offs[10]]).astype(BF16)
        wmg = wl[:, offs[10]:offs[11]].astype(BF16)
        gq = jnp.tile(qk_gain[layer, 0], A_HEADS)[None]
        gk = jnp.tile(qk_gain[layer, 1], A_KV_HEADS)[None]
        qa, ka, va, qb, kb, vb, qkvc, gate_c, ab = _in_projection(
            xa, mods, wa, wb, wc, wab, gq, gk, cos, sin, gs_mean, n_lat, tm)
        oa = _gqa(qa, ka, va, n_lat, min(GQA_TQ, tm))
        ob = _neighborhood(qb, kb, vb, _na_bias_slabs(rpb[layer]), n_lat)
        qc, kc, vc, comp = _gdn_prep(qkvc, ab, conv_w[layer], a_log[layer], dt_bias[layer], gs_sum, n_lat, tm)
        u, w, kd, qd, intra = _gdn_local(qc, kc, vc, comp, blk)
        oc0, oc1 = _gdn_scan(u, w, kd, qd, intra, comp, n_lat, blk)
        og = jnp.tile(o_gain[layer], C_HEADS)[None]
        xa = _merge(xa, mods, oa, ob, oc0, oc1, gate_c, wmg, w_branch[layer].astype(BF16),
                    w_out[layer].astype(BF16), og, gs_mean, ln1_g[layer][None], ln1_b[layer][None], n_lat, tm, alpha)
        moe = _moe(xa, mods, w_router[layer], w_gate_up[layer], w_down[layer], n_lat, tm)
        xa = _post(xa, moe, mods, ln2_g[layer][None], ln2_b[layer][None], n_lat, tm, alpha)
    return xa[:, :n_lat]
```

```python
import functools
import math

import numpy as np
import jax
import jax.numpy as jnp
from jax import lax
from jax.experimental import pallas as pl
from jax.experimental.pallas import tpu as pltpu

F32 = jnp.float32
BF16 = jnp.bfloat16
HIGHEST = lax.Precision.HIGHEST

GRID_W = 64
HEAD_DIM = 64
A_HEADS = 8
A_KV_HEADS = 2
B_HEADS = 8
C_HEADS = 8
NA_ROWS = 8
NA_COLS = 16
CONV_K = 4
CHUNK = 64
N_BRANCH = 3
BRANCH_W = 512
CAPACITY = 2
N_MOD = 6
LN_EPS = 1e-6
ROPE_THETA = 10000.0
NEG_BIG = -1e30
LOG2E = 1.4426950408889634

A_Q_W = A_HEADS * HEAD_DIM
A_KV_W = A_KV_HEADS * HEAD_DIM
B_W = B_HEADS * HEAD_DIM
C_W = C_HEADS * HEAD_DIM
LANE = 128
VMEM_LIMIT = 56 * 1024 * 1024
GQA_TQ = 64
VT_ROWS = HEAD_DIM + 16
NA_WIN = NA_ROWS + 2


def _cparams(sem):
    return pltpu.CompilerParams(dimension_semantics=sem, vmem_limit_bytes=VMEM_LIMIT)


def _dot(a, b):
    return jnp.dot(a, b, preferred_element_type=F32)


def _dot_hi(a, b):
    return jnp.dot(a, b, precision=HIGHEST, preferred_element_type=F32)


def _dot_nt(a, b):
    return lax.dot_general(a, b, (((1,), (1,)), ((), ())), preferred_element_type=F32)


def _dot_tn(a, b):
    return lax.dot_general(a, b, (((0,), (0,)), ((), ())), preferred_element_type=F32)


def _split_bf16(x, pieces):
    out = []
    for _ in range(pieces - 1):
        h = x.astype(BF16)
        out.append(h)
        x = x - h.astype(F32)
    out.append(x.astype(BF16))
    return out


def _dot_exact_rhs(a, b, pieces=3):
    return sum(_dot(p, b) for p in _split_bf16(a, pieces))


def _dot_exact_lhs(a, b, pieces=3):
    return sum(_dot(a, p) for p in _split_bf16(b, pieces))


def _ln(x):
    mu = jnp.mean(x, axis=-1, keepdims=True)
    xc = x - mu
    var = jnp.mean(xc * xc, axis=-1, keepdims=True)
    return xc * lax.rsqrt(var + LN_EPS)


def _sigmoid(x):
    return 1.0 / (1.0 + jnp.exp(-x))


def _silu(x):
    return x * _sigmoid(x)


def _group_sum_matrix(width, scale):
    g = np.arange(width) // HEAD_DIM
    return jnp.asarray((g[:, None] == g[None, :]).astype(np.float32) * scale, dtype=BF16)


def _mod_kernel(c_ref, w_ref, b_ref, o_ref):
    c = c_ref[...]
    o_ref[0] = _dot_hi(_silu(c), w_ref[0]) + b_ref[0]


def _modulation(cc, w_mod, b_mod):
    depth, d, dm = w_mod.shape
    r = cc.shape[0]
    tn = min(dm, 1536)
    return pl.pallas_call(
        _mod_kernel,
        grid=(depth, dm // tn),
        in_specs=[pl.BlockSpec((r, d), lambda l, j: (0, 0)),
                  pl.BlockSpec((1, d, tn), lambda l, j: (l, 0, j)),
                  pl.BlockSpec((1, 1, tn), lambda l, j: (l, 0, j))],
        out_specs=pl.BlockSpec((1, r, tn), lambda l, j: (l, 0, j)),
        out_shape=jax.ShapeDtypeStruct((depth, r, dm), F32),
        compiler_params=_cparams(("parallel", "parallel")),
        name="modulation",
    )(cc, w_mod, b_mod.reshape(depth, 1, dm))


def _rope_swap(y):
    w = y.shape[-1]
    lane = lax.broadcasted_iota(jnp.int32, y.shape, 1)
    up = pltpu.roll(y, w - 16, 1)
    down = pltpu.roll(y, 16, 1)
    return jnp.where((lane % 32) < 16, up, down)


def _inproj_kernel(x_ref, mod_ref, wa_ref, wb_ref, wc_ref, wab_ref, gq_ref, gk_ref, cos_ref, sin_ref,
                   gs_ref, qa_ref, ka_ref, va_ref, qb_ref, kb_ref, vb_ref, qkvc_ref, gatec_ref, ab_ref):
    x = x_ref[0]
    mod = mod_ref[0, 0]
    h = (_ln(x) * (1.0 + mod[1:2]) + mod[0:1]).astype(BF16)
    scale = HEAD_DIM ** -0.5 * LOG2E

    za = _dot(h, wa_ref[...])
    q = za[:, :A_Q_W]
    k = za[:, A_Q_W:A_Q_W + A_KV_W]
    v = za[:, A_Q_W + A_KV_W:]
    cos = cos_ref[...]
    sin = sin_ref[...]
    gs = gs_ref[...]
    qn = q * lax.rsqrt(_dot_exact_rhs(q * q, gs, 2) + LN_EPS) * gq_ref[...]
    qn = qn * cos + _rope_swap(qn) * sin
    qa_ref[0] = (qn * scale).astype(BF16)
    kn = k * lax.rsqrt(_dot_exact_rhs(k * k, gs[:A_KV_W, :A_KV_W], 2) + LN_EPS) * gk_ref[...]
    kn = kn * cos[:, :A_KV_W] + _rope_swap(kn) * sin[:, :A_KV_W]
    ones = jnp.ones((VT_ROWS - HEAD_DIM, x.shape[0]), F32)
    vt = v.T
    for j in range(A_KV_HEADS):
        ka_ref[0, j] = kn[:, j * HEAD_DIM:(j + 1) * HEAD_DIM].astype(BF16)
        va_ref[0, j] = jnp.concatenate([vt[j * HEAD_DIM:(j + 1) * HEAD_DIM], ones], axis=0).astype(BF16)

    zb = _dot(h, wb_ref[...])
    qb_ref[0] = (zb[:, :B_W] * scale).astype(BF16)
    kbv = zb[:, B_W:2 * B_W]
    vt = zb[:, 2 * B_W:].T
    for j in range(B_HEADS):
        kb_ref[0, j] = kbv[:, j * HEAD_DIM:(j + 1) * HEAD_DIM].astype(BF16)
        vb_ref[0, j] = jnp.concatenate([vt[j * HEAD_DIM:(j + 1) * HEAD_DIM], ones], axis=0).astype(BF16)

    zc = _dot(h, wc_ref[...])
    qkvc_ref[0] = zc[:, :3 * C_W]
    gatec_ref[0] = zc[:, 3 * C_W:]
    ab_ref[0] = _dot(h, wab_ref[...])


def _in_projection(xa, mods, wa, wb, wc, wab, gq, gk, cos, sin, gs, n_lat, tm):
    b, t, d = xa.shape
    nl = n_lat // tm
    row = lambda bi, i: (bi, i, 0)
    const = lambda bi, i: (0, 0)
    outs = [
        (jax.ShapeDtypeStruct((b, t, A_Q_W), BF16), pl.BlockSpec((1, tm, A_Q_W), row)),
        (jax.ShapeDtypeStruct((b, A_KV_HEADS, t, HEAD_DIM), BF16),
         pl.BlockSpec((1, A_KV_HEADS, tm, HEAD_DIM), lambda bi, i: (bi, 0, i, 0))),
        (jax.ShapeDtypeStruct((b, A_KV_HEADS, VT_ROWS, t), BF16),
         pl.BlockSpec((1, A_KV_HEADS, VT_ROWS, tm), lambda bi, i: (bi, 0, 0, i))),
        (jax.ShapeDtypeStruct((b, t, B_W), BF16), pl.BlockSpec((1, tm, B_W), row)),
        (jax.ShapeDtypeStruct((b, B_HEADS, t, HEAD_DIM), BF16),
         pl.BlockSpec((1, B_HEADS, tm, HEAD_DIM), lambda bi, i: (bi, 0, i, 0))),
        (jax.ShapeDtypeStruct((b, B_HEADS, VT_ROWS, t), BF16),
         pl.BlockSpec((1, B_HEADS, VT_ROWS, tm), lambda bi, i: (bi, 0, 0, i))),
        (jax.ShapeDtypeStruct((b, t, 3 * C_W), F32), pl.BlockSpec((1, tm, 3 * C_W), row)),
        (jax.ShapeDtypeStruct((b, t, C_W), F32), pl.BlockSpec((1, tm, C_W), row)),
        (jax.ShapeDtypeStruct((b, t, LANE), F32), pl.BlockSpec((1, tm, LANE), row)),
    ]
    return pl.pallas_call(
        _inproj_kernel,
        grid=(b, t // tm),
        in_specs=[pl.BlockSpec((1, tm, d), row),
                  pl.BlockSpec((1, 1, N_MOD, d), lambda bi, i: (bi, (i >= nl).astype(jnp.int32), 0, 0)),
                  pl.BlockSpec(wa.shape, const), pl.BlockSpec(wb.shape, const),
                  pl.BlockSpec(wc.shape, const), pl.BlockSpec(wab.shape, const),
                  pl.BlockSpec(gq.shape, const), pl.BlockSpec(gk.shape, const),
                  pl.BlockSpec((tm, A_Q_W), lambda bi, i: (i, 0)),
                  pl.BlockSpec((tm, A_Q_W), lambda bi, i: (i, 0)),
                  pl.BlockSpec(gs.shape, const)],
        out_specs=[o[1] for o in outs],
        out_shape=[o[0] for o in outs],
        compiler_params=_cparams(("parallel", "parallel")),
        name="in_projection",
    )(xa, mods, wa, wb, wc, wab, gq, gk, cos, sin, gs)


def _gqa_kernel(q_ref, k_ref, v_ref, o_ref, *, tq, ck, n_lat, n_tot):
    i = pl.program_id(1)
    g = A_HEADS // A_KV_HEADS

    def attend(key_lo, key_hi):
        qs = [jnp.concatenate([q_ref[0, :, (kh * g + j) * HEAD_DIM:(kh * g + j + 1) * HEAD_DIM] for j in range(g)],
                              axis=0) for kh in range(A_KV_HEADS)]
        ss =[_dot_nt(k_ref[0, kh, key_lo:key_hi, :], qs[kh]) for kh in range(A_KV_HEADS)]
        ps = [jnp.exp2((s - jnp.max(s, axis=0, keepdims=True)).astype(BF16)) for s in ss]
        accs = [_dot(v_ref[0, kh, :, key_lo:key_hi], ps[kh]) for kh in range(A_KV_HEADS)]
        for kh in range(A_KV_HEADS):
            o = (accs[kh][:HEAD_DIM] / accs[kh][HEAD_DIM:HEAD_DIM + 1]).T.astype(BF16)
            for j in range(g):
                o_ref[0, :, (kh * g + j) * HEAD_DIM:(kh * g + j + 1) * HEAD_DIM] = o[j * tq:(j + 1) * tq]

    @pl.when(i < n_lat // tq)
    def _():
        attend(0, n_tot)

    @pl.when(i >= n_lat // tq)
    def _():
        attend(n_lat, n_tot)


def _gqa(qa, ka, va, n_lat, tq, ck):
    b, t, _ = qa.shape
    return pl.pallas_call(
        functools.partial(_gqa_kernel, tq=tq, ck=ck, n_lat=n_lat, n_tot=t),
        grid=(b, t // tq),
        in_specs=[pl.BlockSpec((1, tq, A_Q_W), lambda bi, i: (bi, i, 0)),
                  pl.BlockSpec((1, A_KV_HEADS, t, HEAD_DIM), lambda bi, i: (bi, 0, 0, 0)),
                  pl.BlockSpec((1, A_KV_HEADS, VT_ROWS, t), lambda bi, i: (bi, 0, 0, 0))],
        out_specs=pl.BlockSpec((1, tq, A_Q_W), lambda bi, i: (bi, i, 0)),
        out_shape=jax.ShapeDtypeStruct((b, t, A_Q_W), BF16),
        compiler_params=_cparams(("parallel", "parallel")),
        name="gqa",
    )(qa, ka, va)


def _na_window_start(r0, rows):
    return np.clip(r0 - NA_ROWS // 2, 0, rows - NA_ROWS - 1) // 2 * 2


def _na_classes(n_lat, n_tot):
    rows = n_lat // GRID_W
    assert rows >= NA_WIN and rows % 2 == 0
    geoms, table = [], []
    for r0 in range(0, rows, 2):
        geom = (int(_na_window_start(r0, rows)) - r0,) + tuple(
            int(np.clip(r0 + qr - NA_ROWS // 2, 0, rows - NA_ROWS)) - r0 for qr in range(2))
        if geom not in geoms:
            geoms.append(geom)
        table.append(geoms.index(geom))
    table += [len(geoms)] * ((n_tot - n_lat) // (2 * GRID_W))
    return geoms, np.asarray(table, np.int32)


def _na_bias_slabs(rpb, geoms):
    kk = np.arange(NA_WIN * GRID_W)[:, None]
    qq = np.arange(2 * GRID_W)[None, :]
    kr, cj = kk // GRID_W, kk % GRID_W
    qr, c = qq // GRID_W, qq % GRID_W
    col_start = np.clip(c - NA_COLS // 2, 0, GRID_W - NA_COLS)
    col_ok = (cj >= col_start) & (cj < col_start + NA_COLS)
    col_off = np.clip(cj - c + (NA_COLS - 1), 0, 2 * NA_COLS - 2)
    scaled = rpb * LOG2E
    slabs = []
    for a, b0, b1 in geoms:
        first = np.where(qr == 0, b0, b1)
        ok = col_ok & (a + kr >= first) & (a + kr < first + NA_ROWS)
        row_off = np.clip(a + kr - qr + (NA_ROWS - 1), 0, 2 * NA_ROWS - 2)
        slabs.append(jnp.where(ok[None], scaled[:, row_off, col_off], NEG_BIG))
    slabs.append(jnp.full_like(slabs[0], NEG_BIG))
    return jnp.stack(slabs, axis=0).astype(F32)


def _na_kernel(cls_ref, q_ref, k_ref, vt_ref, bias_ref, o_ref, *, n_lat, n_tot):
    del cls_ref
    rows = n_lat // GRID_W
    r0 = 2 * pl.program_id(1)
    start = jnp.clip(r0 - NA_ROWS // 2, 0, rows - NA_ROWS - 1) // 2 * 2
    nb = pl.ds(pl.multiple_of(start * GRID_W, 2 * GRID_W), NA_WIN * GRID_W)
    for h in range(B_HEADS):
        hs = slice(h * HEAD_DIM, (h + 1) * HEAD_DIM)
        q = q_ref[0, :, hs]
        s_nb = _dot_nt(k_ref[0, h, nb, :], q) + bias_ref[0, h]
        s_cx = _dot_nt(k_ref[0, h, n_lat:n_tot, :], q)
        m = jnp.maximum(jnp.max(s_nb, axis=0, keepdims=True), jnp.max(s_cx, axis=0, keepdims=True))
        p_nb = jnp.exp2((s_nb - m).astype(BF16))
        p_cx = jnp.exp2((s_cx - m).astype(BF16))
        oe = _dot(vt_ref[0, h, :, nb], p_nb) + _dot(vt_ref[0, h, :, n_lat:n_tot], p_cx)
        o_ref[0, :, hs] = (oe[:HEAD_DIM] / oe[HEAD_DIM:HEAD_DIM + 1]).T.astype(BF16)


def _neighborhood(qb, kb, vbt, slabs, table, n_lat):
    b, t, _ = qb.shape
    tq = 2 * GRID_W
    grid_spec = pltpu.PrefetchScalarGridSpec(
        num_scalar_prefetch=1,
        grid=(b, t // tq),
        in_specs=[pl.BlockSpec((1, tq, B_W), lambda bi, pi, cls: (bi, pi, 0)),
                  pl.BlockSpec((1, B_HEADS, t, HEAD_DIM), lambda bi, pi, cls: (bi, 0, 0, 0)),
                  pl.BlockSpec((1, B_HEADS, VT_ROWS, t), lambda bi, pi, cls: (bi, 0, 0, 0)),
                  pl.BlockSpec((1,) + slabs.shape[1:], lambda bi, pi, cls: (cls[pi], 0, 0, 0))],
        out_specs=pl.BlockSpec((1, tq, B_W), lambda bi, pi, cls: (bi, pi, 0)),
    )
    return pl.pallas_call(
        functools.partial(_na_kernel, n_lat=n_lat, n_tot=t),
        grid_spec=grid_spec,
        out_shape=jax.ShapeDtypeStruct((b, t, B_W), BF16),
        compiler_params=_cparams(("parallel", "parallel")),
        name="neighborhood",
    )(jnp.asarray(table), qb, kb, vbt, slabs)


GC_LANE, GL_LANE, BETA_LANE = 0, 2 * C_HEADS, 4 * C_HEADS


def _gdn_prep_kernel(x_ref, prev_ref, next_ref, ab_ref, cw_ref, alog_ref, dtb_ref, gs_ref,
                     q_ref, k_ref, v_ref, comp_ref, *, tm, n_lat, n_tot):
    i = pl.program_id(1)
    nl = n_lat // tm
    nt = n_tot // tm
    first = jnp.logical_or(i == 0, i == nl)
    last = jnp.logical_or(i == nl - 1, i == nt - 1)
    x = x_ref[0]
    pv = jnp.where(first, 0.0, prev_ref[0])
    nx = jnp.where(last, 0.0, next_ref[0])
    row = lax.broadcasted_iota(jnp.int32, (tm, 1), 0)
    xm1 = jnp.where(row == 0, pv[7:8], pltpu.roll(x, 1, 0))
    xm2 = jnp.where(row == 0, pv[6:7], jnp.where(row == 1, pv[7:8], pltpu.roll(x, 2, 0)))
    xp1 = jnp.where(row == tm - 1, nx[0:1], pltpu.roll(x, tm - 1, 0))
    cw = cw_ref[...]
    y = _silu(xm2 * cw[0:1] + xm1 * cw[1:2] + x * cw[2:3] + xp1 * cw[3:4])
    q = y[:, :C_W]
    k = y[:, C_W:2 * C_W]
    gs = gs_ref[...]
    q_ref[0] = q * lax.rsqrt(_dot_exact_rhs(q * q, gs, 2) + LN_EPS) * (HEAD_DIM ** -0.5)
    k_ref[0] = k * lax.rsqrt(_dot_exact_rhs(k * k, gs, 2) + LN_EPS)
    v_ref[0] = y[:, 2 * C_W:]
    ab = ab_ref[0]
    z = ab + dtb_ref[...]
    softplus = jnp.maximum(z, 0.0) + jnp.log(1.0 + jnp.exp(-jnp.abs(z)))
    lane = lax.broadcasted_iota(jnp.int32, ab.shape, 1)
    gb = jnp.where(lane < 2 * C_HEADS, -jnp.exp(alog_ref[...]) * softplus, _sigmoid(ab))
    ii = lax.broadcasted_iota(jnp.int32, (tm, tm), 0)
    jj = lax.broadcasted_iota(jnp.int32, (tm, tm), 1)
    same = (ii // CHUNK) == (jj // CHUNK)
    gb3 = _split_bf16(gb, 3)

    def masked_sum(mask):
        mb = jnp.where(mask, 1.0, 0.0).astype(BF16)
        return sum(_dot(mb, piece) for piece in gb3)

    prefix = masked_sum(jnp.logical_and(same, jj <= ii))
    suffix = masked_sum(jnp.logical_and(same, jj >= ii))
    total = masked_sum(same)
    comp = jnp.where(lane < C_HEADS, prefix,
                     jnp.where(lane < GL_LANE, suffix,
                               jnp.where(lane < BETA_LANE, pltpu.roll(total, GL_LANE, 1),
                                         jnp.where(lane < BETA_LANE + 2 * C_HEADS, pltpu.roll(gb, 2 * C_HEADS, 1),
                                                   0.0))))
    comp_ref[0] = comp


def _gdn_prep(qkvc, ab, conv_w, a_log, dt_bias, gs, n_lat, tm):
    b, t, _ = qkvc.shape
    c3 = 3 * C_W
    alog_row = jnp.zeros((1, LANE), F32).at[0, :2 * C_HEADS].set(a_log.reshape(-1))
    dtb_row = jnp.zeros((1, LANE), F32).at[0, :2 * C_HEADS].set(dt_bias.reshape(-1))
    row = lambda bi, i: (bi, i, 0)
    const2 = lambda bi, i: (0, 0)
    hb = tm // 8
    last8 = t // 8 - 1
    tok = jax.ShapeDtypeStruct((b, t, C_W), F32)
    return pl.pallas_call(
        functools.partial(_gdn_prep_kernel, tm=tm, n_lat=n_lat, n_tot=t),
        grid=(b, t // tm),
        in_specs=[pl.BlockSpec((1, tm, c3), row),
                  pl.BlockSpec((1, 8, c3), lambda bi, i: (bi, jnp.maximum(i * hb - 1, 0), 0)),
                  pl.BlockSpec((1, 8, c3), lambda bi, i: (bi, jnp.minimum((i + 1) * hb, last8), 0)),
                  pl.BlockSpec((1, tm, LANE), row),
                  pl.BlockSpec((CONV_K, c3), const2),
                  pl.BlockSpec((1, LANE), const2), pl.BlockSpec((1, LANE), const2),
                  pl.BlockSpec(gs.shape, const2)],
        out_specs=[pl.BlockSpec((1, tm, C_W), row)] * 3 + [pl.BlockSpec((1, tm, LANE), row)],
        out_shape=[tok, tok, tok, jax.ShapeDtypeStruct((b, t, LANE), F32)],
        compiler_params=_cparams(("parallel", "parallel")),
        name="gdn_prep",
    )(qkvc, qkvc, qkvc, ab, conv_w, alog_row, dtb_row, gs)


def _unit_tri_inverses(lms, eye, bd16, bd32):
    def mm(a, b_):
        return _dot(a.astype(BF16), b_.astype(BF16))

    lds = [jnp.where(bd16, lm, 0.0) for lm in lms]
    xs = [eye - ld for ld in lds]
    lds = [ld.astype(BF16) for ld in lds]
    pws = [_dot(ld, ld).astype(BF16) for ld in lds]
    for step in range(3):
        xs = [x + mm(x, pw) for x, pw in zip(xs, pws)]
        if step < 2:
            pws = [_dot(pw, pw).astype(BF16) for pw in pws]
    for level in (lambda lm: jnp.where(jnp.logical_and(bd32, jnp.logical_not(bd16)), lm, 0.0),
                  lambda lm: jnp.where(bd32, 0.0, lm)):
        xbs = [x.astype(BF16) for x in xs]
        ys = [_dot(level(lm).astype(BF16), xb) for lm, xb in zip(lms, xbs)]
        xs = [x - _dot(xb, y.astype(BF16)) for x, xb, y in zip(xs, xbs, ys)]
    return xs


def _gdn_local_kernel(q_ref, k_ref, v_ref, comp_ref, ex_ref, sel_ref, u_ref, w_ref, kd_ref, qd_ref, in_ref):
    n = q_ref.shape[1]
    hp = 2 * HEAD_DIM
    q = q_ref[0]
    k = k_ref[0]
    v = v_ref[0]
    e = _dot_exact_rhs(comp_ref[0], ex_ref[0])
    ii = lax.broadcasted_iota(jnp.int32, (n, n), 0)
    jj = lax.broadcasted_iota(jnp.int32, (n, n), 1)
    same = (ii // CHUNK) == (jj // CHUNK)
    eye = (ii == jj).astype(F32)
    bd16 = (ii // 16) == (jj // 16)
    bd32 = (ii // 32) == (jj // 32)
    lane = lax.broadcasted_iota(jnp.int32, (1, hp), 1)
    lane2 = lax.broadcasted_iota(jnp.int32, (1, 2 * hp), 1)
    kh = [jnp.where((lane // HEAD_DIM) == hh, k, 0.0).astype(BF16) for hh in range(2)]
    lms, intras, rhss = [], [], []
    for p in range(2):
        gc = e[:, p * hp:(p + 1) * hp]
        gl = e[:, (2 + p) * hp:(3 + p) * hp]
        beta = e[:, (4 + p) * hp:(5 + p) * hp]
        eg = jnp.exp(gc)
        kb = k * beta
        kd_ref[p, 0] = (k * jnp.exp(gl - gc)).astype(BF16)
        qd_ref[p, 0] = (q * eg).astype(BF16)
        rhs = jnp.concatenate([v * beta, kb * eg], axis=1)
        kbq = jnp.concatenate([kb, q], axis=0).astype(BF16)
        incl = jnp.logical_and(same, (ii >= jj) if p == 0 else (ii <= jj))
        strict = jnp.logical_and(same, (ii > jj) if p == 0 else (ii < jj))
        for hh in range(2):
            gcb = jnp.broadcast_to(gc[:, hh * HEAD_DIM:hh * HEAD_DIM + 1], (n, n))
            decay = jnp.exp(jnp.where(incl, gcb - gcb.T, NEG_BIG))
            aq = _dot_nt(kbq, kh[hh])
            lms.append(jnp.where(strict, aq[:n] * decay, 0.0))
            intras.append(jnp.where(incl, aq[n:] * decay, 0.0).astype(BF16))
            rhss.append(jnp.where(((lane2 % hp) // HEAD_DIM) == hh, rhs, 0.0).astype(BF16))
    tmats = _unit_tri_inverses(lms, eye, bd16, bd32)
    for p in range(2):
        uw = sum(_dot(tmats[2 * p + hh].astype(BF16), rhss[2 * p + hh]) for hh in range(2))
        intra2 = sum(_dot(intras[2 * p + hh], sel_ref[hh]) for hh in range(2))
        u_ref[p, 0] = uw[:, :hp].astype(BF16)
        w_ref[p, 0] = uw[:, hp:].astype(BF16)
        in_ref[p, 0] = intra2.astype(BF16)


def _gdn_local(q, k, v, comp, blk):
    b, t, _ = q.shape
    hp = 2 * HEAD_DIM
    n_pair = C_W // hp
    ex = np.zeros((n_pair, LANE, 6 * hp), np.float32)
    for pr in range(n_pair):
        for p in range(2):
            for hh in range(2):
                src = p * C_HEADS + pr * 2 + hh
                for kind, base in enumerate((GC_LANE, GL_LANE, BETA_LANE)):
                    col = (2 * kind + p) * hp + hh * HEAD_DIM
                    ex[pr, base + src, col:col + HEAD_DIM] = 1.0
    sel = np.zeros((2, blk, hp), np.float32)
    for hh in range(2):
        sel[hh, np.arange(blk), hh * HEAD_DIM + np.arange(blk) % CHUNK] = 1.0
    tok = lambda bi, pr, i: (bi, i, pr)
    dirs = lambda bi, pr, i: (0, bi, i, pr)
    out = jax.ShapeDtypeStruct((2, b, t, C_W), BF16)
    return pl.pallas_call(
        _gdn_local_kernel,
        grid=(b, n_pair, t // blk),
        in_specs=[pl.BlockSpec((1, blk, hp), tok)] * 3
                 + [pl.BlockSpec((1, blk, LANE), lambda bi, pr, i: (bi, i, 0)),
                    pl.BlockSpec((1, LANE, 6 * hp), lambda bi, pr, i: (pr, 0, 0)),
                    pl.BlockSpec((2, blk, hp), lambda bi, pr, i: (0, 0, 0))],
        out_specs=[pl.BlockSpec((2, 1, blk, hp), dirs)] * 5,
        out_shape=[out] * 5,
        compiler_params=_cparams(("parallel", "parallel", "parallel")),
        name="gdn_local",
    )(q, k, v, comp, jnp.asarray(ex, dtype=BF16), jnp.asarray(sel, dtype=BF16))


def _gdn_scan_kernel(*refs, ncb):
    ins = (refs[0:6], refs[6:12])
    ex_ref = refs[12]
    outs = refs[13:15]
    s_ref = refs[15]
    j = pl.program_id(1)

    @pl.when(j == 0)
    def _():
        s_ref[...] = jnp.zeros_like(s_ref)

    gw = 4 * HEAD_DIM
    n_grp = C_W // gw
    lane_head = lax.broadcasted_iota(jnp.int32, (1, gw), 1) // HEAD_DIM
    rr = lax.broadcasted_iota(jnp.int32, (gw, gw), 0) // HEAD_DIM
    cc_ = lax.broadcasted_iota(jnp.int32, (gw, gw), 1) // HEAD_DIM
    blockdiag = rr == cc_
    for cc in range(ncb):
        for p in range(2):
            u_ref, w_ref, kd_ref, qd_ref, in_ref, comp_ref = ins[p]
            c = cc if p == 0 else ncb - 1 - cc
            rows = slice(c * CHUNK, (c + 1) * CHUNK)
            sdec = jnp.exp(_dot_exact_rhs(comp_ref[0, c * CHUNK:c * CHUNK + 8, :], ex_ref[...]))
            for g in range(n_grp):
                ls = slice(g * gw, (g + 1) * gw)
                s = s_ref[p, g]
                sb = s.astype(BF16)
                v_new = u_ref[0, 0, rows, ls].astype(F32) - _dot(w_ref[0, 0, rows, ls], sb)
                vstack = jnp.concatenate(
                    [jnp.where(lane_head == h, v_new, 0.0).astype(BF16) for h in range(gw // HEAD_DIM)], axis=0)
                o = _dot(qd_ref[0, 0, rows, ls], sb) + _dot(in_ref[0, 0, rows, ls], vstack)
                upd = _dot_tn(kd_ref[0, 0, rows, ls], v_new.astype(BF16))
                s_ref[p, g] = s * sdec[0:1, p * C_W + g * gw:p * C_W + (g + 1) * gw] + jnp.where(blockdiag, upd, 0.0)
                outs[p][0, rows, ls] = o


def _gdn_scan(u, w, kd, qd, intra, comp, n_lat, blk):
    _, b, t, _ = u.shape
    nl = n_lat // blk
    nx = (t - n_lat) // blk
    ncb = blk // CHUNK
    ex = np.zeros((LANE, 2 * C_W), np.float32)
    for p in range(2):
        for h in range(C_HEADS):
            ex[GL_LANE + p * C_HEADS + h, p * C_W + h * HEAD_DIM:p * C_W + (h + 1) * HEAD_DIM] = 1.0

    def blk0(j):
        return jnp.where(j < nx, nl + j, j - nx)

    def blk1(j):
        return jnp.where(j < nx, nl + nx - 1 - j, nl - 1 - (j - nx))

    specs = []
    for p, bf in enumerate((blk0, blk1)):
        specs += [pl.BlockSpec((1, 1, blk, C_W), lambda bi, j, p=p, bf=bf: (p, bi, bf(j), 0))] * 5
        specs += [pl.BlockSpec((1, blk, LANE), lambda bi, j, bf=bf: (bi, bf(j), 0))]
    specs += [pl.BlockSpec((LANE, 2 * C_W), lambda bi, j: (0, 0))]
    out = jax.ShapeDtypeStruct((b, t, C_W), F32)
    gw = 4 * HEAD_DIM
    return pl.pallas_call(
        functools.partial(_gdn_scan_kernel, ncb=ncb),
        grid=(b, nl + nx),
        in_specs=specs,
        out_specs=[pl.BlockSpec((1, blk, C_W), lambda bi, j: (bi, blk0(j), 0)),
                   pl.BlockSpec((1, blk, C_W), lambda bi, j: (bi, blk1(j), 0))],
        out_shape=[out, out],
        scratch_shapes=[pltpu.VMEM((2, C_W // gw, gw, gw), F32)],
        compiler_params=_cparams(("parallel", "arbitrary")),
        name="gdn_scan",
    )(u, w, kd, qd, intra, comp, u, w, kd, qd, intra, comp, jnp.asarray(ex, dtype=BF16))


def _merge_kernel(x_ref, mod_ref, oa_ref, ob_ref, oc0_ref, oc1_ref, gate_ref, wmg_ref, wbr_ref, wout_ref, og_ref,
                  gs_ref, lng_ref, lnb_ref, o_ref, *, alpha):
    x = x_ref[0]
    d = x.shape[-1]
    mod = mod_ref[0, 0]
    h = (_ln(x) * (1.0 + mod[1:2]) + mod[0:1]).astype(BF16)
    gates = _sigmoid(_dot(h, wmg_ref[...]))
    oc = oc0_ref[0] + oc1_ref[0]
    ocn = oc * lax.rsqrt(_dot_exact_rhs(oc * oc, gs_ref[...], 2) + LN_EPS) * og_ref[...] * _silu(gate_ref[0])
    m = gates[:, :d] * _dot(oa_ref[0], wbr_ref[0])
    m = m + gates[:, d:2 * d] * _dot(ob_ref[0], wbr_ref[1])
    m = m + gates[:, 2 * d:] * _dot(ocn.astype(BF16), wbr_ref[2])
    mix = _dot(m.astype(BF16), wout_ref[...])
    y = alpha * x + mod[2:3] * mix
    o_ref[0] = _ln(y) * lng_ref[...] + lnb_ref[...]


def _merge(xa, mods, oa, ob, oc0, oc1, gate_c, wmg, wbr, wout, og, gs, lng, lnb, n_lat, tm, alpha):
    b, t, d = xa.shape
    nl = n_lat // tm
    row = lambda bi, i: (bi, i, 0)
    const2 = lambda bi, i: (0, 0)
    return pl.pallas_call(
        functools.partial(_merge_kernel, alpha=alpha),
        grid=(b, t // tm),
        in_specs=[pl.BlockSpec((1, tm, d), row),
                  pl.BlockSpec((1, 1, N_MOD, d), lambda bi, i: (bi, (i >= nl).astype(jnp.int32), 0, 0)),
                  pl.BlockSpec((1, tm, BRANCH_W), row), pl.BlockSpec((1, tm, BRANCH_W), row),
                  pl.BlockSpec((1, tm, BRANCH_W), row), pl.BlockSpec((1, tm, BRANCH_W), row),
                  pl.BlockSpec((1, tm, BRANCH_W), row),
                  pl.BlockSpec(wmg.shape, const2),
                  pl.BlockSpec(wbr.shape, lambda bi, i: (0, 0, 0)),
                  pl.BlockSpec(wout.shape, const2),
                  pl.BlockSpec(og.shape, const2), pl.BlockSpec(gs.shape, const2),
                  pl.BlockSpec(lng.shape, const2), pl.BlockSpec(lnb.shape, const2)],
        out_specs=pl.BlockSpec((1, tm, d), row),
        out_shape=jax.ShapeDtypeStruct((b, t, d), F32),
        compiler_params=_cparams(("parallel", "parallel")),
        name="merge",
    )(xa, mods, oa, ob, oc0, oc1, gate_c, wmg, wbr, wout, og, gs, lng, lnb)


def _route_kernel(x_ref, mod_ref, wr_ref, h_ref, aff_ref, *, n_exp):
    x = x_ref[0]
    mod = mod_ref[0, 0]
    h = _ln(x) * (1.0 + mod[4:5]) + mod[3:4]
    h_ref[0] = h.astype(BF16)
    logits = _dot_hi(h, wr_ref[...])
    lane = lax.broadcasted_iota(jnp.int32, logits.shape, 1)
    logits = jnp.where(lane < n_exp, logits, NEG_BIG)
    e = jnp.exp(logits - jnp.max(logits, axis=-1, keepdims=True))
    aff_ref[0] = e / jnp.sum(e, axis=-1, keepdims=True)


def _route(xa, mods, wr, n_lat, tm, n_exp):
    b, t, d = xa.shape
    nl = n_lat // tm
    row = lambda bi, i: (bi, i, 0)
    return pl.pallas_call(
        functools.partial(_route_kernel, n_exp=n_exp),
        grid=(b, t // tm),
        in_specs=[pl.BlockSpec((1, tm, d), row),
                  pl.BlockSpec((1, 1, N_MOD, d), lambda bi, i: (bi, (i >= nl).astype(jnp.int32), 0, 0)),
                  pl.BlockSpec(wr.shape, lambda bi, i: (0, 0))],
        out_specs=[pl.BlockSpec((1, tm, d), row), pl.BlockSpec((1, tm, LANE), row)],
        out_shape=[jax.ShapeDtypeStruct((b, t, d), BF16), jax.ShapeDtypeStruct((b, t, LANE), F32)],
        compiler_params=_cparams(("parallel", "parallel")),
        name="route",
    )(xa, mods, wr)


def _ffn_kernel(x_ref, wg_ref, wu_ref, wd_ref, gate_ref, o_ref):
    f = pl.program_id(2)

    @pl.when(f == 0)
    def _():
        o_ref[...] = jnp.zeros_like(o_ref)

    x = x_ref[0]
    g = _dot(x, wg_ref[0].astype(BF16))
    u = _dot(x, wu_ref[0].astype(BF16))
    a = (_silu(g) * u).astype(BF16)
    o_ref[0] += _dot(a, wd_ref[0].astype(BF16))

    @pl.when(f == pl.num_programs(2) - 1)
    def _():
        o_ref[0] = o_ref[0] * gate_ref[0]


def _expert_ffn(xs, gate, w_gate_up, w_down, tm, tf):
    e, r, d = xs.shape
    f = w_down.shape[1]
    nf = f // tf
    return pl.pallas_call(
        _ffn_kernel,
        grid=(e, r // tm, nf),
        in_specs=[pl.BlockSpec((1, tm, d), lambda ei, i, fi: (ei, i, 0)),
                  pl.BlockSpec((1, d, tf), lambda ei, i, fi: (ei, 0, fi)),
                  pl.BlockSpec((1, d, tf), lambda ei, i, fi: (ei, 0, nf + fi)),
                  pl.BlockSpec((1, tf, d), lambda ei, i, fi: (ei, fi, 0)),
                  pl.BlockSpec((1, tm, 1), lambda ei, i, fi: (ei, i, 0))],
        out_specs=pl.BlockSpec((1, tm, d), lambda ei, i, fi: (ei, i, 0)),
        out_shape=jax.ShapeDtypeStruct((e, r, d), F32),
        compiler_params=_cparams(("parallel", "parallel", "arbitrary")),
        name="expert_ffn",
    )(xs, w_gate_up, w_gate_up, w_down, gate)


def _post_kernel(x_ref, moe_ref, mod_ref, lng_ref, lnb_ref, o_ref, *, alpha):
    mod = mod_ref[0, 0]
    y = alpha * x_ref[0] + mod[5:6] * moe_ref[0]
    o_ref[0] = _ln(y) * lng_ref[...] + lnb_ref[...]


def _post(xa, moe, mods, lng, lnb, n_lat, tm, alpha):
    b, t, d = xa.shape
    nl = n_lat // tm
    row = lambda bi, i: (bi, i, 0)
    const2 = lambda bi, i: (0, 0)
    return pl.pallas_call(
        functools.partial(_post_kernel, alpha=alpha),
        grid=(b, t // tm),
        in_specs=[pl.BlockSpec((1, tm, d), row), pl.BlockSpec((1, tm, d), row),
                  pl.BlockSpec((1, 1, N_MOD, d), lambda bi, i: (bi, (i >= nl).astype(jnp.int32), 0, 0)),
                  pl.BlockSpec(lng.shape, const2), pl.BlockSpec(lnb.shape, const2)],
        out_specs=pl.BlockSpec((1, tm, d), row),
        out_shape=jax.ShapeDtypeStruct((b, t, d), F32),
        compiler_params=_cparams(("parallel", "parallel")),
        name="post_norm",
    )(xa, moe, mods, lng, lnb)


def _moe(xa, mods, w_router, w_gate_up, w_down, n_lat, tm):
    b, t, d = xa.shape
    n_exp = w_router.shape[-1]
    m_ctx = t - n_lat
    wr = jnp.zeros((d, LANE), F32).at[:, :n_exp].set(w_router)
    h, aff = _route(xa, mods, wr, n_lat, tm, n_exp)
    aff = jnp.swapaxes(aff[:, :, :n_exp], 1, 2)
    cap_l = CAPACITY * n_lat // n_exp
    cap_x = CAPACITY * m_ctx // n_exp
    gate_l, idx_l = lax.top_k(aff[:, :, :n_lat], cap_l)
    gate_x, idx_x = lax.top_k(aff[:, :, n_lat:], cap_x)
    idx = jnp.concatenate([idx_l, idx_x + n_lat], axis=-1)
    gate = jnp.concatenate([gate_l, gate_x], axis=-1)
    cap = cap_l + cap_x
    idx_e = jnp.swapaxes(idx, 0, 1)
    bidx = jnp.arange(b)[None, :, None]
    xs = h[bidx, idx_e].reshape(n_exp, b * cap, d)
    gate_e = jnp.swapaxes(gate, 0, 1).reshape(n_exp, b * cap, 1)
    rows = b * cap
    tmr = rows
    for cand in (1088, 1024, 512, 256, 128, 64, 32, 16):
        if rows % cand == 0:
            tmr = cand
            break
    tf = min(512, w_down.shape[1])
    y = _expert_ffn(xs, gate_e, w_gate_up, w_down, tmr, tf).reshape(n_exp, b, cap, d)
    return jnp.zeros((b, t, d), F32).at[bidx, idx_e].add(y)


def _rope_tables(n_lat, m_ctx):
    t = np.arange(n_lat)
    rows = (t // GRID_W).astype(np.float32)
    cols = (t % GRID_W).astype(np.float32)
    n_freq = HEAD_DIM // 4
    inv_freq = jnp.asarray(ROPE_THETA, F32) ** (-jnp.arange(n_freq, dtype=F32) / n_freq)
    ang_r = jnp.asarray(rows)[:, None] * inv_freq
    ang_c = jnp.asarray(cols)[:, None] * inv_freq
    cr, sr, cc, sc = jnp.cos(ang_r), jnp.sin(ang_r), jnp.cos(ang_c), jnp.sin(ang_c)
    cos = jnp.concatenate([cr, cr, cc, cc], axis=-1)
    sin = jnp.concatenate([-sr, sr, -sc, sc], axis=-1)
    cos = jnp.concatenate([cos, jnp.ones((m_ctx, HEAD_DIM), F32)], axis=0)
    sin = jnp.concatenate([sin, jnp.zeros((m_ctx, HEAD_DIM), F32)], axis=0)
    return jnp.tile(cos, (1, A_HEADS)), jnp.tile(sin, (1, A_HEADS))


def kernel(x, c, ctx, c_ctx, w_mod, b_mod, w_in, qk_gain, rpb, conv_w, a_log, dt_bias, o_gain, w_branch, w_out,
           ln1_g, ln1_b, w_router, w_gate_up, w_down, ln2_g, ln2_b):
    b, n_lat, d = x.shape
    m_ctx = ctx.shape[1]
    depth = w_mod.shape[0]
    alpha = (2 * depth) ** 0.25
    tm = math.gcd(256, math.gcd(n_lat, m_ctx))
    blk = tm

    xa = jnp.concatenate([x, ctx], axis=1)
    n_rows = -(-(b + 1) // 8) * 8
    cc = jnp.zeros((n_rows, d), F32).at[:b].set(c).at[b].set(c_ctx)
    mod_all = _modulation(cc, w_mod, b_mod).reshape(depth, n_rows, N_MOD, d)
    cos, sin = _rope_tables(n_lat, m_ctx)
    gs_mean = _group_sum_matrix(A_Q_W, 1.0 / HEAD_DIM)
    gs_sum = _group_sum_matrix(C_W, 1.0)
    na_geoms, na_table = _na_classes(n_lat, n_lat + m_ctx)

    offs = np.cumsum((0, A_Q_W, A_KV_W, A_KV_W, B_W, B_W, B_W, 3 * C_W, C_W, 2 * C_HEADS, 2 * C_HEADS, N_BRANCH * d))
    for layer in range(depth):
        ml = mod_all[layer]
        mods = jnp.stack([ml[:b], jnp.broadcast_to(ml[b][None], (b, N_MOD, d))], axis=1)
        wl = w_in[layer]
        wa = wl[:, offs[0]:offs[3]].astype(BF16)
        wb = wl[:, offs[3]:offs[6]].astype(BF16)
        wc = wl[:, offs[6]:offs[8]].astype(BF16)
        wab = jnp.zeros((d, LANE), F32).at[:, :4 * C_HEADS].set(wl[:, offs[8]:offs[10]]).astype(BF16)
        wmg = wl[:, offs[10]:offs[11]].astype(BF16)
        gq = jnp.tile(qk_gain[layer, 0], A_HEADS)[None]
        gk = jnp.tile(qk_gain[layer, 1], A_KV_HEADS)[None]
        qa, ka, va, qb, kb, vb, qkvc, gate_c, ab = _in_projection(
            xa, mods, wa, wb, wc, wab, gq, gk, cos, sin, gs_mean, n_lat, tm)
        oa = _gqa(qa, ka, va, n_lat, min(GQA_TQ, tm), tm)
        ob = _neighborhood(qb, kb, vb, _na_bias_slabs(rpb[layer], na_geoms), na_table, n_lat)
        qc, kc, vc, comp = _gdn_prep(qkvc, ab, conv_w[layer], a_log[layer], dt_bias[layer], gs_sum, n_lat, tm)
        u, w, kd, qd, intra = _gdn_local(qc, kc, vc, comp, blk)
        oc0, oc1 = _gdn_scan(u, w, kd, qd, intra, comp, n_lat, blk)
        og = jnp.tile(o_gain[layer], C_HEADS)[None]
        xa = _merge(xa, mods, oa, ob, oc0, oc1, gate_c, wmg, w_branch[layer].astype(BF16),
                    w_out[layer].astype(BF16), og, gs_mean, ln1_g[layer][None], ln1_b[layer][None], n_lat, tm, alpha)
        moe = _moe(xa, mods, w_router[layer], w_gate_up[layer], w_down[layer], n_lat, tm)
        xa = _post(xa, moe, mods, ln2_g[layer][None], ln2_b[layer][None], n_lat, tm, alpha)
    return xa[:, :n_lat]
```

```python
import functools
import math

import numpy as np
import jax
import jax.numpy as jnp
from jax import lax
from jax.experimental import pallas as pl
from jax.experimental.pallas import tpu as pltpu

F32 = jnp.float32
BF16 = jnp.bfloat16
HIGHEST = lax.Precision.HIGHEST

GRID_W = 64
HEAD_DIM = 64
A_HEADS = 8
A_KV_HEADS = 2
B_HEADS = 8
C_HEADS = 8
NA_ROWS = 8
NA_COLS = 16
CONV_K = 4
CHUNK = 64
N_BRANCH = 3
BRANCH_W = 512
CAPACITY = 2
N_MOD = 6
LN_EPS = 1e-6
ROPE_THETA = 10000.0
NEG_BIG = -1e30
LOG2E = 1.4426950408889634

A_Q_W = A_HEADS * HEAD_DIM
A_KV_W = A_KV_HEADS * HEAD_DIM
B_W = B_HEADS * HEAD_DIM
C_W = C_HEADS * HEAD_DIM
LANE = 128
VMEM_LIMIT = 56 * 1024 * 1024
GQA_TQ = 128
SCAN_BATCH = 2
VT_ROWS = HEAD_DIM + 16
NA_WIN = NA_ROWS + 2


def _cparams(sem):
    return pltpu.CompilerParams(dimension_semantics=sem, vmem_limit_bytes=VMEM_LIMIT)


def _dot(a, b):
    return jnp.dot(a, b, preferred_element_type=F32)


def _dot_hi(a, b):
    return jnp.dot(a, b, precision=HIGHEST, preferred_element_type=F32)


def _dot_nt(a, b):
    return lax.dot_general(a, b, (((1,), (1,)), ((), ())), preferred_element_type=F32)


def _dot_tn(a, b):
    return lax.dot_general(a, b, (((0,), (0,)), ((), ())), preferred_element_type=F32)


def _split_bf16(x, pieces):
    out = []
    for _ in range(pieces - 1):
        h = x.astype(BF16)
        out.append(h)
        x = x - h.astype(F32)
    out.append(x.astype(BF16))
    return out


def _dot_exact_rhs(a, b, pieces=3):
    return sum(_dot(p, b) for p in _split_bf16(a, pieces))


def _dot_exact_lhs(a, b, pieces=3):
    return sum(_dot(a, p) for p in _split_bf16(b, pieces))


def _ln(x):
    mu = jnp.mean(x, axis=-1, keepdims=True)
    xc = x - mu
    var = jnp.mean(xc * xc, axis=-1, keepdims=True)
    return xc * lax.rsqrt(var + LN_EPS)


def _sigmoid(x):
    return 1.0 / (1.0 + jnp.exp(-x))


def _silu(x):
    return x * _sigmoid(x)


def _group_sum_matrix(width, scale):
    g = np.arange(width) // HEAD_DIM
    return jnp.asarray((g[:, None] == g[None, :]).astype(np.float32) * scale, dtype=BF16)


def _mod_kernel(c_ref, w_ref, b_ref, o_ref):
    c = c_ref[...]
    o_ref[0] = _dot_hi(_silu(c), w_ref[0]) + b_ref[0]


def _modulation(cc, w_mod, b_mod):
    depth, d, dm = w_mod.shape
    r = cc.shape[0]
    tn = min(dm, 1536)
    return pl.pallas_call(
        _mod_kernel,
        grid=(depth, dm // tn),
        in_specs=[pl.BlockSpec((r, d), lambda l, j: (0, 0)),
                  pl.BlockSpec((1, d, tn), lambda l, j: (l, 0, j)),
                  pl.BlockSpec((1, 1, tn), lambda l, j: (l, 0, j))],
        out_specs=pl.BlockSpec((1, r, tn), lambda l, j: (l, 0, j)),
        out_shape=jax.ShapeDtypeStruct((depth, r, dm), F32),
        compiler_params=_cparams(("parallel", "parallel")),
        name="modulation",
    )(cc, w_mod, b_mod.reshape(depth, 1, dm))


def _rope_swap(y):
    w = y.shape[-1]
    lane = lax.broadcasted_iota(jnp.int32, y.shape, 1)
    up = pltpu.roll(y, w - 16, 1)
    down = pltpu.roll(y, 16, 1)
    return jnp.where((lane % 32) < 16, up, down)


def _inproj_kernel(x_ref, mod_ref, wa_ref, wb_ref, wc_ref, wab_ref, gq_ref, gk_ref, cos_ref, sin_ref,
                   gs_ref, qa_ref, ka_ref, va_ref, qb_ref, kb_ref, vb_ref, qkvc_ref, gatec_ref, ab_ref):
    x = x_ref[0]
    mod = mod_ref[0, 0]
    h = (_ln(x) * (1.0 + mod[1:2]) + mod[0:1]).astype(BF16)
    scale = HEAD_DIM ** -0.5 * LOG2E

    za = _dot(h, wa_ref[...])
    q = za[:, :A_Q_W]
    k = za[:, A_Q_W:A_Q_W + A_KV_W]
    v = za[:, A_Q_W + A_KV_W:]
    cos = cos_ref[...]
    sin = sin_ref[...]
    gs = gs_ref[...]
    qn = q * lax.rsqrt(_dot_exact_rhs(q * q, gs, 2) + LN_EPS) * gq_ref[...]
    qn = qn * cos + _rope_swap(qn) * sin
    qa_ref[0] = (qn * scale).astype(BF16)
    kn = k * lax.rsqrt(_dot_exact_rhs(k * k, gs[:A_KV_W, :A_KV_W], 2) + LN_EPS) * gk_ref[...]
    kn = kn * cos[:, :A_KV_W] + _rope_swap(kn) * sin[:, :A_KV_W]
    ones = jnp.ones((VT_ROWS - HEAD_DIM, x.shape[0]), F32)
    vt = v.T
    for j in range(A_KV_HEADS):
        ka_ref[0, j] = kn[:, j * HEAD_DIM:(j + 1) * HEAD_DIM].astype(BF16)
        va_ref[0, j] = jnp.concatenate([vt[j * HEAD_DIM:(j + 1) * HEAD_DIM], ones], axis=0).astype(BF16)

    zb = _dot(h, wb_ref[...])
    qb_ref[0] = (zb[:, :B_W] * scale).astype(BF16)
    kbv = zb[:, B_W:2 * B_W]
    vt = zb[:, 2 * B_W:].T
    for j in range(B_HEADS):
        kb_ref[0, j] = kbv[:, j * HEAD_DIM:(j + 1) * HEAD_DIM].astype(BF16)
        vb_ref[0, j] = jnp.concatenate([vt[j * HEAD_DIM:(j + 1) * HEAD_DIM], ones], axis=0).astype(BF16)

    zc = _dot(h, wc_ref[...])
    qkvc_ref[0] = zc[:, :3 * C_W]
    gatec_ref[0] = zc[:, 3 * C_W:]
    ab_ref[0] = _dot(h, wab_ref[...])


def _in_projection(xa, mods, wa, wb, wc, wab, gq, gk, cos, sin, gs, n_lat, tm):
    b, t, d = xa.shape
    nl = n_lat // tm
    row = lambda bi, i: (bi, i, 0)
    const = lambda bi, i: (0, 0)
    outs = [
        (jax.ShapeDtypeStruct((b, t, A_Q_W), BF16), pl.BlockSpec((1, tm, A_Q_W), row)),
        (jax.ShapeDtypeStruct((b, A_KV_HEADS, t, HEAD_DIM), BF16),
         pl.BlockSpec((1, A_KV_HEADS, tm, HEAD_DIM), lambda bi, i: (bi, 0, i, 0))),
        (jax.ShapeDtypeStruct((b, A_KV_HEADS, VT_ROWS, t), BF16),
         pl.BlockSpec((1, A_KV_HEADS, VT_ROWS, tm), lambda bi, i: (bi, 0, 0, i))),
        (jax.ShapeDtypeStruct((b, t, B_W), BF16), pl.BlockSpec((1, tm, B_W), row)),
        (jax.ShapeDtypeStruct((b, B_HEADS, t, HEAD_DIM), BF16),
         pl.BlockSpec((1, B_HEADS, tm, HEAD_DIM), lambda bi, i: (bi, 0, i, 0))),
        (jax.ShapeDtypeStruct((b, B_HEADS, VT_ROWS, t), BF16),
         pl.BlockSpec((1, B_HEADS, VT_ROWS, tm), lambda bi, i: (bi, 0, 0, i))),
        (jax.ShapeDtypeStruct((b, t, 3 * C_W), F32), pl.BlockSpec((1, tm, 3 * C_W), row)),
        (jax.ShapeDtypeStruct((b, t, C_W), F32), pl.BlockSpec((1, tm, C_W), row)),
        (jax.ShapeDtypeStruct((b, t, LANE), F32), pl.BlockSpec((1, tm, LANE), row)),
    ]
    return pl.pallas_call(
        _inproj_kernel,
        grid=(b, t // tm),
        in_specs=[pl.BlockSpec((1, tm, d), row),
                  pl.BlockSpec((1, 1, N_MOD, d), lambda bi, i: (bi, (i >= nl).astype(jnp.int32), 0, 0)),
                  pl.BlockSpec(wa.shape, const), pl.BlockSpec(wb.shape, const),
                  pl.BlockSpec(wc.shape, const), pl.BlockSpec(wab.shape, const),
                  pl.BlockSpec(gq.shape, const), pl.BlockSpec(gk.shape, const),
                  pl.BlockSpec((tm, A_Q_W), lambda bi, i: (i, 0)),
                  pl.BlockSpec((tm, A_Q_W), lambda bi, i: (i, 0)),
                  pl.BlockSpec(gs.shape, const)],
        out_specs=[o[1] for o in outs],
        out_shape=[o[0] for o in outs],
        compiler_params=_cparams(("parallel", "parallel")),
        name="in_projection",
    )(xa, mods, wa, wb, wc, wab, gq, gk, cos, sin, gs)


def _gqa_kernel(q_ref, k_ref, v_ref, o_ref, *, tq, ck, n_lat, n_tot):
    i = pl.program_id(1)
    g = A_HEADS // A_KV_HEADS

    def attend(key_lo, key_hi):
        qs = [jnp.concatenate([q_ref[0, :, (kh * g + j) * HEAD_DIM:(kh * g + j + 1) * HEAD_DIM] for j in range(g)],
                              axis=0) for kh in range(A_KV_HEADS)]
        ss =[_dot_nt(k_ref[0, kh, key_lo:key_hi, :], qs[kh]) for kh in range(A_KV_HEADS)]
        ps = [jnp.exp2((s - jnp.max(s, axis=0, keepdims=True)).astype(BF16)) for s in ss]
        accs = [_dot(v_ref[0, kh, :, key_lo:key_hi], ps[kh]) for kh in range(A_KV_HEADS)]
        for kh in range(A_KV_HEADS):
            o = (accs[kh][:HEAD_DIM] / accs[kh][HEAD_DIM:HEAD_DIM + 1]).T.astype(BF16)
            for j in range(g):
                o_ref[0, :, (kh * g + j) * HEAD_DIM:(kh * g + j + 1) * HEAD_DIM] = o[j * tq:(j + 1) * tq]

    @pl.when(i < n_lat // tq)
    def _():
        attend(0, n_tot)

    @pl.when(i >= n_lat // tq)
    def _():
        attend(n_lat, n_tot)


def _gqa(qa, ka, va, n_lat, tq, ck):
    b, t, _ = qa.shape
    return pl.pallas_call(
        functools.partial(_gqa_kernel, tq=tq, ck=ck, n_lat=n_lat, n_tot=t),
        grid=(b, t // tq),
        in_specs=[pl.BlockSpec((1, tq, A_Q_W), lambda bi, i: (bi, i, 0)),
                  pl.BlockSpec((1, A_KV_HEADS, t, HEAD_DIM), lambda bi, i: (bi, 0, 0, 0)),
                  pl.BlockSpec((1, A_KV_HEADS, VT_ROWS, t), lambda bi, i: (bi, 0, 0, 0))],
        out_specs=pl.BlockSpec((1, tq, A_Q_W), lambda bi, i: (bi, i, 0)),
        out_shape=jax.ShapeDtypeStruct((b, t, A_Q_W), BF16),
        compiler_params=_cparams(("parallel", "parallel")),
        name="gqa",
    )(qa, ka, va)


def _na_window_start(r0, rows):
    return np.clip(r0 - NA_ROWS // 2, 0, rows - NA_ROWS - 1) // 2 * 2


def _na_classes(n_lat, n_tot):
    rows = n_lat // GRID_W
    assert rows >= NA_WIN and rows % 2 == 0
    geoms, table = [], []
    for r0 in range(0, rows, 2):
        geom = (int(_na_window_start(r0, rows)) - r0,) + tuple(
            int(np.clip(r0 + qr - NA_ROWS // 2, 0, rows - NA_ROWS)) - r0 for qr in range(2))
        if geom not in geoms:
            geoms.append(geom)
        table.append(geoms.index(geom))
    table += [len(geoms)] * ((n_tot - n_lat) // (2 * GRID_W))
    return geoms, np.asarray(table, np.int32)


def _na_bias_slabs(rpb, geoms):
    n_head = rpb.shape[0]
    cj = np.arange(GRID_W)[:, None]
    c = np.arange(GRID_W)[None, :]
    col_start = np.clip(c - NA_COLS // 2, 0, GRID_W - NA_COLS)
    col_ok = (cj >= col_start) & (cj < col_start + NA_COLS)
    col_off = np.clip(cj - c + (NA_COLS - 1), 0, 2 * NA_COLS - 2)
    spread = np.zeros((2 * NA_COLS - 1, GRID_W * GRID_W), np.float32)
    spread[col_off.reshape(-1), np.arange(GRID_W * GRID_W)] = 1.0
    planes = jnp.dot(rpb * LOG2E, jnp.asarray(spread), precision=HIGHEST)
    planes = planes.reshape(n_head, 2 * NA_ROWS - 1, GRID_W, GRID_W)
    kr = np.arange(NA_WIN)[:, None]
    qr = np.arange(2)[None, :]
    slabs = []
    for a, b0, b1 in geoms:
        first = np.where(qr == 0, b0, b1)
        row_ok = (a + kr >= first) & (a + kr < first + NA_ROWS)
        row_off = np.clip(a + kr - qr + (NA_ROWS - 1), 0, 2 * NA_ROWS - 2)
        rows = jnp.stack([jnp.stack([planes[:, int(row_off[i, j])] for j in range(2)], axis=1)
                          for i in range(NA_WIN)], axis=1)
        ok = row_ok[:, :, None, None] & col_ok[None, None]
        slab = jnp.where(ok[None], rows, NEG_BIG).transpose(0, 1, 3, 2, 4)
        slabs.append(slab.reshape(n_head, NA_WIN * GRID_W, 2 * GRID_W))
    slabs.append(jnp.full_like(slabs[0], NEG_BIG))
    return jnp.stack(slabs, axis=0).astype(F32)


def _na_kernel(cls_ref, q_ref, k_ref, vt_ref, bias_ref, o_ref, *, n_lat, n_tot):
    del cls_ref
    rows = n_lat // GRID_W
    r0 = 2 * pl.program_id(1)
    start = jnp.clip(r0 - NA_ROWS // 2, 0, rows - NA_ROWS - 1) // 2 * 2
    nb = pl.ds(pl.multiple_of(start * GRID_W, 2 * GRID_W), NA_WIN * GRID_W)
    for h in range(B_HEADS):
        hs = slice(h * HEAD_DIM, (h + 1) * HEAD_DIM)
        q = q_ref[0, :, hs]
        s_nb = _dot_nt(k_ref[0, h, nb, :], q) + bias_ref[0, h]
        s_cx = _dot_nt(k_ref[0, h, n_lat:n_tot, :], q)
        m = jnp.maximum(jnp.max(s_nb, axis=0, keepdims=True), jnp.max(s_cx, axis=0, keepdims=True))
        p_nb = jnp.exp2((s_nb - m).astype(BF16))
        p_cx = jnp.exp2((s_cx - m).astype(BF16))
        oe = _dot(vt_ref[0, h, :, nb], p_nb) + _dot(vt_ref[0, h, :, n_lat:n_tot], p_cx)
        o_ref[0, :, hs] = (oe[:HEAD_DIM] / oe[HEAD_DIM:HEAD_DIM + 1]).T.astype(BF16)


def _neighborhood(qb, kb, vbt, slabs, table, n_lat):
    b, t, _ = qb.shape
    tq = 2 * GRID_W
    grid_spec = pltpu.PrefetchScalarGridSpec(
        num_scalar_prefetch=1,
        grid=(b, t // tq),
        in_specs=[pl.BlockSpec((1, tq, B_W), lambda bi, pi, cls: (bi, pi, 0)),
                  pl.BlockSpec((1, B_HEADS, t, HEAD_DIM), lambda bi, pi, cls: (bi, 0, 0, 0)),
                  pl.BlockSpec((1, B_HEADS, VT_ROWS, t), lambda bi, pi, cls: (bi, 0, 0, 0)),
                  pl.BlockSpec((1,) + slabs.shape[1:], lambda bi, pi, cls: (cls[pi], 0, 0, 0))],
        out_specs=pl.BlockSpec((1, tq, B_W), lambda bi, pi, cls: (bi, pi, 0)),
    )
    return pl.pallas_call(
        functools.partial(_na_kernel, n_lat=n_lat, n_tot=t),
        grid_spec=grid_spec,
        out_shape=jax.ShapeDtypeStruct((b, t, B_W), BF16),
        compiler_params=_cparams(("parallel", "parallel")),
        name="neighborhood",
    )(jnp.asarray(table), qb, kb, vbt, slabs)


GC_LANE, GL_LANE, BETA_LANE = 0, 2 * C_HEADS, 4 * C_HEADS


def _gdn_prep_kernel(x_ref, prev_ref, next_ref, ab_ref, cw_ref, alog_ref, dtb_ref, gs_ref,
                     q_ref, k_ref, v_ref, comp_ref, *, tm, n_lat, n_tot):
    i = pl.program_id(1)
    nl = n_lat // tm
    nt = n_tot // tm
    first = jnp.logical_or(i == 0, i == nl)
    last = jnp.logical_or(i == nl - 1, i == nt - 1)
    x = x_ref[0]
    pv = jnp.where(first, 0.0, prev_ref[0])
    nx = jnp.where(last, 0.0, next_ref[0])
    row = lax.broadcasted_iota(jnp.int32, (tm, 1), 0)
    xm1 = jnp.where(row == 0, pv[7:8], pltpu.roll(x, 1, 0))
    xm2 = jnp.where(row == 0, pv[6:7], jnp.where(row == 1, pv[7:8], pltpu.roll(x, 2, 0)))
    xp1 = jnp.where(row == tm - 1, nx[0:1], pltpu.roll(x, tm - 1, 0))
    cw = cw_ref[...]
    y = _silu(xm2 * cw[0:1] + xm1 * cw[1:2] + x * cw[2:3] + xp1 * cw[3:4])
    q = y[:, :C_W]
    k = y[:, C_W:2 * C_W]
    gs = gs_ref[...]
    q_ref[0] = q * lax.rsqrt(_dot_exact_rhs(q * q, gs, 2) + LN_EPS) * (HEAD_DIM ** -0.5)
    k_ref[0] = k * lax.rsqrt(_dot_exact_rhs(k * k, gs, 2) + LN_EPS)
    v_ref[0] = y[:, 2 * C_W:]
    ab = ab_ref[0]
    z = ab + dtb_ref[...]
    softplus = jnp.maximum(z, 0.0) + jnp.log(1.0 + jnp.exp(-jnp.abs(z)))
    lane = lax.broadcasted_iota(jnp.int32, ab.shape, 1)
    gb = jnp.where(lane < 2 * C_HEADS, -jnp.exp(alog_ref[...]) * softplus, _sigmoid(ab))
    ii = lax.broadcasted_iota(jnp.int32, (tm, tm), 0)
    jj = lax.broadcasted_iota(jnp.int32, (tm, tm), 1)
    same = (ii // CHUNK) == (jj // CHUNK)
    gb3 = _split_bf16(gb, 3)

    def masked_sum(mask):
        mb = jnp.where(mask, 1.0, 0.0).astype(BF16)
        return sum(_dot(mb, piece) for piece in gb3)

    prefix = masked_sum(jnp.logical_and(same, jj <= ii))
    suffix = masked_sum(jnp.logical_and(same, jj >= ii))
    total = masked_sum(same)
    comp = jnp.where(lane < C_HEADS, prefix,
                     jnp.where(lane < GL_LANE, suffix,
                               jnp.where(lane < BETA_LANE, pltpu.roll(total, GL_LANE, 1),
                                         jnp.where(lane < BETA_LANE + 2 * C_HEADS, pltpu.roll(gb, 2 * C_HEADS, 1),
                                                   0.0))))
    comp_ref[0] = comp


def _gdn_prep(qkvc, ab, conv_w, a_log, dt_bias, gs, n_lat, tm):
    b, t, _ = qkvc.shape
    c3 = 3 * C_W
    alog_row = jnp.zeros((1, LANE), F32).at[0, :2 * C_HEADS].set(a_log.reshape(-1))
    dtb_row = jnp.zeros((1, LANE), F32).at[0, :2 * C_HEADS].set(dt_bias.reshape(-1))
    row = lambda bi, i: (bi, i, 0)
    const2 = lambda bi, i: (0, 0)
    hb = tm // 8
    last8 = t // 8 - 1
    tok = jax.ShapeDtypeStruct((b, t, C_W), F32)
    return pl.pallas_call(
        functools.partial(_gdn_prep_kernel, tm=tm, n_lat=n_lat, n_tot=t),
        grid=(b, t // tm),
        in_specs=[pl.BlockSpec((1, tm, c3), row),
                  pl.BlockSpec((1, 8, c3), lambda bi, i: (bi, jnp.maximum(i * hb - 1, 0), 0)),
                  pl.BlockSpec((1, 8, c3), lambda bi, i: (bi, jnp.minimum((i + 1) * hb, last8), 0)),
                  pl.BlockSpec((1, tm, LANE), row),
                  pl.BlockSpec((CONV_K, c3), const2),
                  pl.BlockSpec((1, LANE), const2), pl.BlockSpec((1, LANE), const2),
                  pl.BlockSpec(gs.shape, const2)],
        out_specs=[pl.BlockSpec((1, tm, C_W), row)] * 3 + [pl.BlockSpec((1, tm, LANE), row)],
        out_shape=[tok, tok, tok, jax.ShapeDtypeStruct((b, t, LANE), F32)],
        compiler_params=_cparams(("parallel", "parallel")),
        name="gdn_prep",
    )(qkvc, qkvc, qkvc, ab, conv_w, alog_row, dtb_row, gs)


def _unit_tri_inverses(lms, eye, bd16, bd32):
    def mm(a, b_):
        return _dot(a.astype(BF16), b_.astype(BF16))

    lds = [jnp.where(bd16, lm, 0.0) for lm in lms]
    xs = [eye - ld for ld in lds]
    lds = [ld.astype(BF16) for ld in lds]
    pws = [_dot(ld, ld).astype(BF16) for ld in lds]
    for step in range(3):
        xs = [x + mm(x, pw) for x, pw in zip(xs, pws)]
        if step < 2:
            pws = [_dot(pw, pw).astype(BF16) for pw in pws]
    for level in (lambda lm: jnp.where(jnp.logical_and(bd32, jnp.logical_not(bd16)), lm, 0.0),
                  lambda lm: jnp.where(bd32, 0.0, lm)):
        xbs = [x.astype(BF16) for x in xs]
        ys = [_dot(level(lm).astype(BF16), xb) for lm, xb in zip(lms, xbs)]
        xs = [x - _dot(xb, y.astype(BF16)) for x, xb, y in zip(xs, xbs, ys)]
    return xs


def _gdn_local_kernel(q_ref, k_ref, v_ref, comp_ref, ex_ref, sel_ref, u_ref, w_ref, kd_ref, qd_ref, in_ref):
    n = q_ref.shape[1]
    hp = 2 * HEAD_DIM
    q = q_ref[0]
    k = k_ref[0]
    v = v_ref[0]
    e = _dot_exact_rhs(comp_ref[0], ex_ref[0])
    ii = lax.broadcasted_iota(jnp.int32, (n, n), 0)
    jj = lax.broadcasted_iota(jnp.int32, (n, n), 1)
    same = (ii // CHUNK) == (jj // CHUNK)
    eye = (ii == jj).astype(F32)
    bd16 = (ii // 16) == (jj // 16)
    bd32 = (ii // 32) == (jj // 32)
    lane = lax.broadcasted_iota(jnp.int32, (1, hp), 1)
    lane2 = lax.broadcasted_iota(jnp.int32, (1, 2 * hp), 1)
    kh = [jnp.where((lane // HEAD_DIM) == hh, k, 0.0).astype(BF16) for hh in range(2)]
    lms, intras, rhss = [], [], []
    for p in range(2):
        gc = e[:, p * hp:(p + 1) * hp]
        gl = e[:, (2 + p) * hp:(3 + p) * hp]
        beta = e[:, (4 + p) * hp:(5 + p) * hp]
        eg = jnp.exp(gc)
        kb = k * beta
        kd_ref[p, 0] = (k * jnp.exp(gl - gc)).astype(BF16)
        qd_ref[p, 0] = (q * eg).astype(BF16)
        rhs = jnp.concatenate([v * beta, kb * eg], axis=1)
        kbq = jnp.concatenate([kb, q], axis=0).astype(BF16)
        incl = jnp.logical_and(same, (ii >= jj) if p == 0 else (ii <= jj))
        strict = jnp.logical_and(same, (ii > jj) if p == 0 else (ii < jj))
        for hh in range(2):
            gcb = jnp.broadcast_to(gc[:, hh * HEAD_DIM:hh * HEAD_DIM + 1], (n, n))
            decay = jnp.exp(jnp.where(incl, gcb - gcb.T, NEG_BIG))
            aq = _dot_nt(kbq, kh[hh])
            lms.append(jnp.where(strict, aq[:n] * decay, 0.0))
            intras.append(jnp.where(incl, aq[n:] * decay, 0.0).astype(BF16))
            rhss.append(jnp.where(((lane2 % hp) // HEAD_DIM) == hh, rhs, 0.0).astype(BF16))
    tmats = _unit_tri_inverses(lms, eye, bd16, bd32)
    for p in range(2):
        uw = sum(_dot(tmats[2 * p + hh].astype(BF16), rhss[2 * p + hh]) for hh in range(2))
        intra2 = sum(_dot(intras[2 * p + hh], sel_ref[hh]) for hh in range(2))
        u_ref[p, 0] = uw[:, :hp].astype(BF16)
        w_ref[p, 0] = uw[:, hp:].astype(BF16)
        in_ref[p, 0] = intra2.astype(BF16)


def _gdn_local(q, k, v, comp, blk):
    b, t, _ = q.shape
    hp = 2 * HEAD_DIM
    n_pair = C_W // hp
    ex = np.zeros((n_pair, LANE, 6 * hp), np.float32)
    for pr in range(n_pair):
        for p in range(2):
            for hh in range(2):
                src = p * C_HEADS + pr * 2 + hh
                for kind, base in enumerate((GC_LANE, GL_LANE, BETA_LANE)):
                    col = (2 * kind + p) * hp + hh * HEAD_DIM
                    ex[pr, base + src, col:col + HEAD_DIM] = 1.0
    sel = np.zeros((2, blk, hp), np.float32)
    for hh in range(2):
        sel[hh, np.arange(blk), hh * HEAD_DIM + np.arange(blk) % CHUNK] = 1.0
    tok = lambda bi, pr, i: (bi, i, pr)
    dirs = lambda bi, pr, i: (0, bi, i, pr)
    out = jax.ShapeDtypeStruct((2, b, t, C_W), BF16)
    return pl.pallas_call(
        _gdn_local_kernel,
        grid=(b, n_pair, t // blk),
        in_specs=[pl.BlockSpec((1, blk, hp), tok)] * 3
                 + [pl.BlockSpec((1, blk, LANE), lambda bi, pr, i: (bi, i, 0)),
                    pl.BlockSpec((1, LANE, 6 * hp), lambda bi, pr, i: (pr, 0, 0)),
                    pl.BlockSpec((2, blk, hp), lambda bi, pr, i: (0, 0, 0))],
        out_specs=[pl.BlockSpec((2, 1, blk, hp), dirs)] * 5,
        out_shape=[out] * 5,
        compiler_params=_cparams(("parallel", "parallel", "parallel")),
        name="gdn_local",
    )(q, k, v, comp, jnp.asarray(ex, dtype=BF16), jnp.asarray(sel, dtype=BF16))


def _gdn_scan_kernel(*refs, ncb, nb):
    ins = (refs[0:6], refs[6:12])
    ex_ref = refs[12]
    outs = refs[13:15]
    s_ref = refs[15]
    j = pl.program_id(1)

    @pl.when(j == 0)
    def _():
        s_ref[...] = jnp.zeros_like(s_ref)

    gw = 4 * HEAD_DIM
    n_grp = C_W // gw
    lane_head = lax.broadcasted_iota(jnp.int32, (1, gw), 1) // HEAD_DIM
    rr = lax.broadcasted_iota(jnp.int32, (gw, gw), 0) // HEAD_DIM
    cc_ = lax.broadcasted_iota(jnp.int32, (gw, gw), 1) // HEAD_DIM
    blockdiag = rr == cc_
    chains = [(bi, p, g) for bi in range(nb) for p in range(2) for g in range(n_grp)]
    for cc in range(ncb):
        rows = [slice(cc * CHUNK, (cc + 1) * CHUNK), slice((ncb - 1 - cc) * CHUNK, (ncb - cc) * CHUNK)]
        sdec = {(bi, p): jnp.exp(_dot_exact_rhs(ins[p][5][bi, rows[p].start:rows[p].start + 8, :], ex_ref[...]))
                for bi in range(nb) for p in range(2)}
        lanes = [slice(g * gw, (g + 1) * gw) for g in range(n_grp)]
        ss = [s_ref[bi, p, g] for bi, p, g in chains]
        sbs = [s.astype(BF16) for s in ss]
        v_news = [ins[p][0][0, bi, rows[p], lanes[g]].astype(F32) - _dot(ins[p][1][0, bi, rows[p], lanes[g]], sb)
                  for (bi, p, g), sb in zip(chains, sbs)]
        vstacks = [jnp.concatenate([jnp.where(lane_head == h, v_new, 0.0).astype(BF16)
                                    for h in range(gw // HEAD_DIM)], axis=0) for v_new in v_news]
        for (bi, p, g), s, sb, v_new, vstack in zip(chains, ss, sbs, v_news, vstacks):
            o = _dot(ins[p][3][0, bi, rows[p], lanes[g]], sb) + _dot(ins[p][4][0, bi, rows[p], lanes[g]], vstack)
            upd = _dot_tn(ins[p][2][0, bi, rows[p], lanes[g]], v_new.astype(BF16))
            decay = sdec[bi, p][0:1, p * C_W + g * gw:p * C_W + (g + 1) * gw]
            s_ref[bi, p, g] = s * decay + jnp.where(blockdiag, upd, 0.0)
            outs[p][bi, rows[p], lanes[g]] = o


def _gdn_scan(u, w, kd, qd, intra, comp, n_lat, blk):
    _, b, t, _ = u.shape
    nl = n_lat // blk
    nx = (t - n_lat) // blk
    ncb = blk // CHUNK
    ex = np.zeros((LANE, 2 * C_W), np.float32)
    for p in range(2):
        for h in range(C_HEADS):
            ex[GL_LANE + p * C_HEADS + h, p * C_W + h * HEAD_DIM:p * C_W + (h + 1) * HEAD_DIM] = 1.0

    def blk0(j):
        return jnp.where(j < nx, nl + j, j - nx)

    def blk1(j):
        return jnp.where(j < nx, nl + nx - 1 - j, nl - 1 - (j - nx))

    nb = SCAN_BATCH if b % SCAN_BATCH == 0 else 1
    specs = []
    for p, bf in enumerate((blk0, blk1)):
        specs += [pl.BlockSpec((1, nb, blk, C_W), lambda bi, j, p=p, bf=bf: (p, bi, bf(j), 0))] * 5
        specs += [pl.BlockSpec((nb, blk, LANE), lambda bi, j, bf=bf: (bi, bf(j), 0))]
    specs += [pl.BlockSpec((LANE, 2 * C_W), lambda bi, j: (0, 0))]
    out = jax.ShapeDtypeStruct((b, t, C_W), F32)
    gw = 4 * HEAD_DIM
    return pl.pallas_call(
        functools.partial(_gdn_scan_kernel, ncb=ncb, nb=nb),
        grid=(b // nb, nl + nx),
        in_specs=specs,
        out_specs=[pl.BlockSpec((nb, blk, C_W), lambda bi, j: (bi, blk0(j), 0)),
                   pl.BlockSpec((nb, blk, C_W), lambda bi, j: (bi, blk1(j), 0))],
        out_shape=[out, out],
        scratch_shapes=[pltpu.VMEM((nb, 2, C_W // gw, gw, gw), F32)],
        compiler_params=_cparams(("parallel", "arbitrary")),
        name="gdn_scan",
    )(u, w, kd, qd, intra, comp, u, w, kd, qd, intra, comp, jnp.asarray(ex, dtype=BF16))


def _merge_kernel(x_ref, mod_ref, oa_ref, ob_ref, oc0_ref, oc1_ref, gate_ref, wmg_ref, wbr_ref, wout_ref, og_ref,
                  gs_ref, lng_ref, lnb_ref, o_ref, *, alpha):
    x = x_ref[0]
    d = x.shape[-1]
    mod = mod_ref[0, 0]
    h = (_ln(x) * (1.0 + mod[1:2]) + mod[0:1]).astype(BF16)
    gates = _sigmoid(_dot(h, wmg_ref[...]))
    oc = oc0_ref[0] + oc1_ref[0]
    ocn = oc * lax.rsqrt(_dot_exact_rhs(oc * oc, gs_ref[...], 2) + LN_EPS) * og_ref[...] * _silu(gate_ref[0])
    m = gates[:, :d] * _dot(oa_ref[0], wbr_ref[0])
    m = m + gates[:, d:2 * d] * _dot(ob_ref[0], wbr_ref[1])
    m = m + gates[:, 2 * d:] * _dot(ocn.astype(BF16), wbr_ref[2])
    mix = _dot(m.astype(BF16), wout_ref[...])
    y = alpha * x + mod[2:3] * mix
    o_ref[0] = _ln(y) * lng_ref[...] + lnb_ref[...]


def _merge(xa, mods, oa, ob, oc0, oc1, gate_c, wmg, wbr, wout, og, gs, lng, lnb, n_lat, tm, alpha):
    b, t, d = xa.shape
    nl = n_lat // tm
    row = lambda bi, i: (bi, i, 0)
    const2 = lambda bi, i: (0, 0)
    return pl.pallas_call(
        functools.partial(_merge_kernel, alpha=alpha),
        grid=(b, t // tm),
        in_specs=[pl.BlockSpec((1, tm, d), row),
                  pl.BlockSpec((1, 1, N_MOD, d), lambda bi, i: (bi, (i >= nl).astype(jnp.int32), 0, 0)),
                  pl.BlockSpec((1, tm, BRANCH_W), row), pl.BlockSpec((1, tm, BRANCH_W), row),
                  pl.BlockSpec((1, tm, BRANCH_W), row), pl.BlockSpec((1, tm, BRANCH_W), row),
                  pl.BlockSpec((1, tm, BRANCH_W), row),
                  pl.BlockSpec(wmg.shape, const2),
                  pl.BlockSpec(wbr.shape, lambda bi, i: (0, 0, 0)),
                  pl.BlockSpec(wout.shape, const2),
                  pl.BlockSpec(og.shape, const2), pl.BlockSpec(gs.shape, const2),
                  pl.BlockSpec(lng.shape, const2), pl.BlockSpec(lnb.shape, const2)],
        out_specs=pl.BlockSpec((1, tm, d), row),
        out_shape=jax.ShapeDtypeStruct((b, t, d), F32),
        compiler_params=_cparams(("parallel", "parallel")),
        name="merge",
    )(xa, mods, oa, ob, oc0, oc1, gate_c, wmg, wbr, wout, og, gs, lng, lnb)


def _route_kernel(x_ref, mod_ref, wr_ref, h_ref, aff_ref, *, n_exp):
    x = x_ref[0]
    mod = mod_ref[0, 0]
    h = _ln(x) * (1.0 + mod[4:5]) + mod[3:4]
    h_ref[0] = h.astype(BF16)
    logits = _dot_hi(h, wr_ref[...])
    lane = lax.broadcasted_iota(jnp.int32, logits.shape, 1)
    logits = jnp.where(lane < n_exp, logits, NEG_BIG)
    e = jnp.exp(logits - jnp.max(logits, axis=-1, keepdims=True))
    aff_ref[0] = e / jnp.sum(e, axis=-1, keepdims=True)


def _route(xa, mods, wr, n_lat, tm, n_exp):
    b, t, d = xa.shape
    nl = n_lat // tm
    row = lambda bi, i: (bi, i, 0)
    return pl.pallas_call(
        functools.partial(_route_kernel, n_exp=n_exp),
        grid=(b, t // tm),
        in_specs=[pl.BlockSpec((1, tm, d), row),
                  pl.BlockSpec((1, 1, N_MOD, d), lambda bi, i: (bi, (i >= nl).astype(jnp.int32), 0, 0)),
                  pl.BlockSpec(wr.shape, lambda bi, i: (0, 0))],
        out_specs=[pl.BlockSpec((1, tm, d), row), pl.BlockSpec((1, tm, LANE), row)],
        out_shape=[jax.ShapeDtypeStruct((b, t, d), BF16), jax.ShapeDtypeStruct((b, t, LANE), F32)],
        compiler_params=_cparams(("parallel", "parallel")),
        name="route",
    )(xa, mods, wr)


def _ffn_kernel(x_ref, wg_ref, wu_ref, wd_ref, gate_ref, o_ref):
    f = pl.program_id(2)

    @pl.when(f == 0)
    def _():
        o_ref[...] = jnp.zeros_like(o_ref)

    x = x_ref[0]
    g = _dot(x, wg_ref[0, 0].astype(BF16))
    u = _dot(x, wu_ref[0, 0].astype(BF16))
    a = (_silu(g) * u).astype(BF16)
    o_ref[0] += _dot(a, wd_ref[0, 0].astype(BF16))

    @pl.when(f == pl.num_programs(2) - 1)
    def _():
        o_ref[0] = o_ref[0] * gate_ref[0]


def _expert_ffn(xs, gate, w_gate_up, w_down, layer, tm, tf):
    e, r, d = xs.shape
    f = w_down.shape[2]
    nf = f // tf
    return pl.pallas_call(
        _ffn_kernel,
        grid=(e, r // tm, nf),
        in_specs=[pl.BlockSpec((1, tm, d), lambda ei, i, fi: (ei, i, 0)),
                  pl.BlockSpec((1, 1, d, tf), lambda ei, i, fi: (layer, ei, 0, fi)),
                  pl.BlockSpec((1, 1, d, tf), lambda ei, i, fi: (layer, ei, 0, nf + fi)),
                  pl.BlockSpec((1, 1, tf, d), lambda ei, i, fi: (layer, ei, fi, 0)),
                  pl.BlockSpec((1, tm, 1), lambda ei, i, fi: (ei, i, 0))],
        out_specs=pl.BlockSpec((1, tm, d), lambda ei, i, fi: (ei, i, 0)),
        out_shape=jax.ShapeDtypeStruct((e, r, d), F32),
        compiler_params=_cparams(("parallel", "parallel", "arbitrary")),
        name="expert_ffn",
    )(xs, w_gate_up, w_gate_up, w_down, gate)


def _post_kernel(x_ref, moe_ref, mod_ref, lng_ref, lnb_ref, o_ref, *, alpha):
    mod = mod_ref[0, 0]
    y = alpha * x_ref[0] + mod[5:6] * moe_ref[0]
    o_ref[0] = _ln(y) * lng_ref[...] + lnb_ref[...]


def _post(xa, moe, mods, lng, lnb, n_lat, tm, alpha):
    b, t, d = xa.shape
    nl = n_lat // tm
    row = lambda bi, i: (bi, i, 0)
    const2 = lambda bi, i: (0, 0)
    return pl.pallas_call(
        functools.partial(_post_kernel, alpha=alpha),
        grid=(b, t // tm),
        in_specs=[pl.BlockSpec((1, tm, d), row), pl.BlockSpec((1, tm, d), row),
                  pl.BlockSpec((1, 1, N_MOD, d), lambda bi, i: (bi, (i >= nl).astype(jnp.int32), 0, 0)),
                  pl.BlockSpec(lng.shape, const2), pl.BlockSpec(lnb.shape, const2)],
        out_specs=pl.BlockSpec((1, tm, d), row),
        out_shape=jax.ShapeDtypeStruct((b, t, d), F32),
        compiler_params=_cparams(("parallel", "parallel")),
        name="post_norm",
    )(xa, moe, mods, lng, lnb)


def _moe(xa, mods, w_router, w_gate_up, w_down, layer, n_lat, tm):
    b, t, d = xa.shape
    n_exp = w_router.shape[-1]
    m_ctx = t - n_lat
    wr = jnp.zeros((d, LANE), F32).at[:, :n_exp].set(w_router)
    h, aff = _route(xa, mods, wr, n_lat, tm, n_exp)
    aff = jnp.swapaxes(aff[:, :, :n_exp], 1, 2)
    cap_l = CAPACITY * n_lat // n_exp
    cap_x = CAPACITY * m_ctx // n_exp
    gate_l, idx_l = lax.top_k(aff[:, :, :n_lat], cap_l)
    gate_x, idx_x = lax.top_k(aff[:, :, n_lat:], cap_x)
    idx = jnp.concatenate([idx_l, idx_x + n_lat], axis=-1)
    gate = jnp.concatenate([gate_l, gate_x], axis=-1)
    cap = cap_l + cap_x
    idx_e = jnp.swapaxes(idx, 0, 1)
    bidx = jnp.arange(b)[None, :, None]
    xs = h[bidx, idx_e].reshape(n_exp, b * cap, d)
    gate_e = jnp.swapaxes(gate, 0, 1).reshape(n_exp, b * cap, 1)
    rows = b * cap
    tmr = rows
    for cand in (1088, 1024, 512, 256, 128, 64, 32, 16):
        if rows % cand == 0:
            tmr = cand
            break
    tf = min(512, w_down.shape[2])
    y = _expert_ffn(xs, gate_e, w_gate_up, w_down, layer, tmr, tf).reshape(n_exp, b, cap, d)
    return jnp.zeros((b, t, d), F32).at[bidx, idx_e].add(y)


def _rope_tables(n_lat, m_ctx):
    t = np.arange(n_lat)
    rows = (t // GRID_W).astype(np.float32)
    cols = (t % GRID_W).astype(np.float32)
    n_freq = HEAD_DIM // 4
    inv_freq = jnp.asarray(ROPE_THETA, F32) ** (-jnp.arange(n_freq, dtype=F32) / n_freq)
    ang_r = jnp.asarray(rows)[:, None] * inv_freq
    ang_c = jnp.asarray(cols)[:, None] * inv_freq
    cr, sr, cc, sc = jnp.cos(ang_r), jnp.sin(ang_r), jnp.cos(ang_c), jnp.sin(ang_c)
    cos = jnp.concatenate([cr, cr, cc, cc], axis=-1)
    sin = jnp.concatenate([-sr, sr, -sc, sc], axis=-1)
    cos = jnp.concatenate([cos, jnp.ones((m_ctx, HEAD_DIM), F32)], axis=0)
    sin = jnp.concatenate([sin, jnp.zeros((m_ctx, HEAD_DIM), F32)], axis=0)
    return jnp.tile(cos, (1, A_HEADS)), jnp.tile(sin, (1, A_HEADS))


def kernel(x, c, ctx, c_ctx, w_mod, b_mod, w_in, qk_gain, rpb, conv_w, a_log, dt_bias, o_gain, w_branch, w_out,
           ln1_g, ln1_b, w_router, w_gate_up, w_down, ln2_g, ln2_b):
    b, n_lat, d = x.shape
    m_ctx = ctx.shape[1]
    depth = w_mod.shape[0]
    alpha = (2 * depth) ** 0.25
    tm = math.gcd(256, math.gcd(n_lat, m_ctx))
    blk = tm

    xa = jnp.concatenate([x, ctx], axis=1)
    n_rows = -(-(b + 1) // 8) * 8
    cc = jnp.zeros((n_rows, d), F32).at[:b].set(c).at[b].set(c_ctx)
    mod_all = _modulation(cc, w_mod, b_mod).reshape(depth, n_rows, N_MOD, d)
    cos, sin = _rope_tables(n_lat, m_ctx)
    gs_mean = _group_sum_matrix(A_Q_W, 1.0 / HEAD_DIM)
    gs_sum = _group_sum_matrix(C_W, 1.0)
    na_geoms, na_table = _na_classes(n_lat, n_lat + m_ctx)

    offs = np.cumsum((0, A_Q_W, A_KV_W, A_KV_W, B_W, B_W, B_W, 3 * C_W, C_W, 2 * C_HEADS, 2 * C_HEADS, N_BRANCH * d))
    for layer in range(depth):
        ml = mod_all[layer]
        mods = jnp.stack([ml[:b], jnp.broadcast_to(ml[b][None], (b, N_MOD, d))], axis=1)
        wl = w_in[layer]
        wa = wl[:, offs[0]:offs[3]].astype(BF16)
        wb = wl[:, offs[3]:offs[6]].astype(BF16)
        wc = wl[:, offs[6]:offs[8]].astype(BF16)
        wab = jnp.zeros((d, LANE), F32).at[:, :4 * C_HEADS].set(wl[:, offs[8]:offs[10]]).astype(BF16)
        wmg = wl[:, offs[10]:offs[11]].astype(BF16)
        gq = jnp.tile(qk_gain[layer, 0], A_HEADS)[None]
        gk = jnp.tile(qk_gain[layer, 1], A_KV_HEADS)[None]
        qa, ka, va, qb, kb, vb, qkvc, gate_c, ab = _in_projection(
            xa, mods, wa, wb, wc, wab, gq, gk, cos, sin, gs_mean, n_lat, tm)
        oa = _gqa(qa, ka, va, n_lat, min(GQA_TQ, tm), tm)
        ob = _neighborhood(qb, kb, vb, _na_bias_slabs(rpb[layer], na_geoms), na_table, n_lat)
        qc, kc, vc, comp = _gdn_prep(qkvc, ab, conv_w[layer], a_log[layer], dt_bias[layer], gs_sum, n_lat, tm)
        u, w, kd, qd, intra = _gdn_local(qc, kc, vc, comp, blk)
        oc0, oc1 = _gdn_scan(u, w, kd, qd, intra, comp, n_lat, blk)
        og = jnp.tile(o_gain[layer], C_HEADS)[None]
        xa = _merge(xa, mods, oa, ob, oc0, oc1, gate_c, wmg, w_branch[layer].astype(BF16),
                    w_out[layer].astype(BF16), og, gs_mean, ln1_g[layer][None], ln1_b[layer][None], n_lat, tm, alpha)
        moe = _moe(xa, mods, w_router[layer], w_gate_up, w_down, layer, n_lat, tm)
        xa = _post(xa, moe, mods, ln2_g[layer][None], ln2_b[layer][None], n_lat, tm, alpha)
    return xa[:, :n_lat]
```

```python
import functools
import math

import numpy as np
import jax
import jax.numpy as jnp
from jax import lax
from jax.experimental import pallas as pl
from jax.experimental.pallas import tpu as pltpu

F32 = jnp.float32
BF16 = jnp.bfloat16
HIGHEST = lax.Precision.HIGHEST

GRID_W = 64
HEAD_DIM = 64
A_HEADS = 8
A_KV_HEADS = 2
B_HEADS = 8
C_HEADS = 8
NA_ROWS = 8
NA_COLS = 16
CONV_K = 4
CHUNK = 64
N_BRANCH = 3
BRANCH_W = 512
CAPACITY = 2
N_MOD = 6
LN_EPS = 1e-6
ROPE_THETA = 10000.0
NEG_BIG = -1e30
LOG2E = 1.4426950408889634

A_Q_W = A_HEADS * HEAD_DIM
A_KV_W = A_KV_HEADS * HEAD_DIM
B_W = B_HEADS * HEAD_DIM
C_W = C_HEADS * HEAD_DIM
LANE = 128
VMEM_LIMIT = 56 * 1024 * 1024
GQA_TQ = 128
SCAN_BATCH = 2
VT_ROWS = HEAD_DIM + 16
NA_WIN = NA_ROWS + 2


def _cparams(sem):
    return pltpu.CompilerParams(dimension_semantics=sem, vmem_limit_bytes=VMEM_LIMIT)


def _dot(a, b):
    return jnp.dot(a, b, preferred_element_type=F32)


def _dot_hi(a, b):
    return jnp.dot(a, b, precision=HIGHEST, preferred_element_type=F32)


def _dot_nt(a, b):
    return lax.dot_general(a, b, (((1,), (1,)), ((), ())), preferred_element_type=F32)


def _dot_tn(a, b):
    return lax.dot_general(a, b, (((0,), (0,)), ((), ())), preferred_element_type=F32)


def _split_bf16(x, pieces):
    out = []
    for _ in range(pieces - 1):
        h = x.astype(BF16)
        out.append(h)
        x = x - h.astype(F32)
    out.append(x.astype(BF16))
    return out


def _dot_exact_rhs(a, b, pieces=3):
    return sum(_dot(p, b) for p in _split_bf16(a, pieces))


def _dot_exact_lhs(a, b, pieces=3):
    return sum(_dot(a, p) for p in _split_bf16(b, pieces))


def _ln(x):
    mu = jnp.mean(x, axis=-1, keepdims=True)
    xc = x - mu
    var = jnp.mean(xc * xc, axis=-1, keepdims=True)
    return xc * lax.rsqrt(var + LN_EPS)


def _sigmoid(x):
    return 1.0 / (1.0 + jnp.exp(-x))


def _silu(x):
    return x * _sigmoid(x)


def _group_sum_matrix(width, scale):
    g = np.arange(width) // HEAD_DIM
    return jnp.asarray((g[:, None] == g[None, :]).astype(np.float32) * scale, dtype=BF16)


def _mod_kernel(c_ref, w_ref, b_ref, o_ref):
    c = c_ref[...]
    o_ref[0] = _dot_hi(_silu(c), w_ref[0]) + b_ref[0]


def _modulation(cc, w_mod, b_mod):
    depth, d, dm = w_mod.shape
    r = cc.shape[0]
    tn = min(dm, 1536)
    return pl.pallas_call(
        _mod_kernel,
        grid=(depth, dm // tn),
        in_specs=[pl.BlockSpec((r, d), lambda l, j: (0, 0)),
                  pl.BlockSpec((1, d, tn), lambda l, j: (l, 0, j)),
                  pl.BlockSpec((1, 1, tn), lambda l, j: (l, 0, j))],
        out_specs=pl.BlockSpec((1, r, tn), lambda l, j: (l, 0, j)),
        out_shape=jax.ShapeDtypeStruct((depth, r, dm), F32),
        compiler_params=_cparams(("parallel", "parallel")),
        name="modulation",
    )(cc, w_mod, b_mod.reshape(depth, 1, dm))


def _rope_swap(y):
    w = y.shape[-1]
    lane = lax.broadcasted_iota(jnp.int32, y.shape, 1)
    up = pltpu.roll(y, w - 16, 1)
    down = pltpu.roll(y, 16, 1)
    return jnp.where((lane % 32) < 16, up, down)


def _inproj_kernel(x_ref, mod_ref, wa_ref, wb_ref, wc_ref, wab_ref, gq_ref, gk_ref, cos_ref, sin_ref,
                   gs_ref, qa_ref, ka_ref, va_ref, qb_ref, kb_ref, vb_ref, qkvc_ref, gatec_ref, ab_ref):
    x = x_ref[0]
    mod = mod_ref[0, 0]
    h = (_ln(x) * (1.0 + mod[1:2]) + mod[0:1]).astype(BF16)
    scale = HEAD_DIM ** -0.5 * LOG2E

    za = _dot(h, wa_ref[...])
    q = za[:, :A_Q_W]
    k = za[:, A_Q_W:A_Q_W + A_KV_W]
    v = za[:, A_Q_W + A_KV_W:]
    cos = cos_ref[...]
    sin = sin_ref[...]
    gs = gs_ref[...]
    qn = q * lax.rsqrt(_dot_exact_rhs(q * q, gs, 2) + LN_EPS) * gq_ref[...]
    qn = qn * cos + _rope_swap(qn) * sin
    qa_ref[0] = (qn * scale).astype(BF16)
    kn = k * lax.rsqrt(_dot_exact_rhs(k * k, gs[:A_KV_W, :A_KV_W], 2) + LN_EPS) * gk_ref[...]
    kn = kn * cos[:, :A_KV_W] + _rope_swap(kn) * sin[:, :A_KV_W]
    ones = jnp.ones((VT_ROWS - HEAD_DIM, x.shape[0]), F32)
    vt = v.T
    for j in range(A_KV_HEADS):
        ka_ref[0, j] = kn[:, j * HEAD_DIM:(j + 1) * HEAD_DIM].astype(BF16)
        va_ref[0, j] = jnp.concatenate([vt[j * HEAD_DIM:(j + 1) * HEAD_DIM], ones], axis=0).astype(BF16)

    zb = _dot(h, wb_ref[...])
    qb_ref[0] = (zb[:, :B_W] * scale).astype(BF16)
    kbv = zb[:, B_W:2 * B_W]
    vt = zb[:, 2 * B_W:].T
    for j in range(B_HEADS):
        kb_ref[0, j] = kbv[:, j * HEAD_DIM:(j + 1) * HEAD_DIM].astype(BF16)
        vb_ref[0, j] = jnp.concatenate([vt[j * HEAD_DIM:(j + 1) * HEAD_DIM], ones], axis=0).astype(BF16)

    zc = _dot(h, wc_ref[...])
    qkvc_ref[0] = zc[:, :3 * C_W]
    gatec_ref[0] = zc[:, 3 * C_W:]
    ab_ref[0] = _dot(h, wab_ref[...])


def _in_projection(xa, mods, wa, wb, wc, wab, gq, gk, cos, sin, gs, n_lat, tm):
    b, t, d = xa.shape
    nl = n_lat // tm
    row = lambda bi, i: (bi, i, 0)
    const = lambda bi, i: (0, 0)
    outs = [
        (jax.ShapeDtypeStruct((b, t, A_Q_W), BF16), pl.BlockSpec((1, tm, A_Q_W), row)),
        (jax.ShapeDtypeStruct((b, A_KV_HEADS, t, HEAD_DIM), BF16),
         pl.BlockSpec((1, A_KV_HEADS, tm, HEAD_DIM), lambda bi, i: (bi, 0, i, 0))),
        (jax.ShapeDtypeStruct((b, A_KV_HEADS, VT_ROWS, t), BF16),
         pl.BlockSpec((1, A_KV_HEADS, VT_ROWS, tm), lambda bi, i: (bi, 0, 0, i))),
        (jax.ShapeDtypeStruct((b, t, B_W), BF16), pl.BlockSpec((1, tm, B_W), row)),
        (jax.ShapeDtypeStruct((b, B_HEADS, t, HEAD_DIM), BF16),
         pl.BlockSpec((1, B_HEADS, tm, HEAD_DIM), lambda bi, i: (bi, 0, i, 0))),
        (jax.ShapeDtypeStruct((b, B_HEADS, VT_ROWS, t), BF16),
         pl.BlockSpec((1, B_HEADS, VT_ROWS, tm), lambda bi, i: (bi, 0, 0, i))),
        (jax.ShapeDtypeStruct((b, t, 3 * C_W), F32), pl.BlockSpec((1, tm, 3 * C_W), row)),
        (jax.ShapeDtypeStruct((b, t, C_W), F32), pl.BlockSpec((1, tm, C_W), row)),
        (jax.ShapeDtypeStruct((b, t, LANE), F32), pl.BlockSpec((1, tm, LANE), row)),
    ]
    return pl.pallas_call(
        _inproj_kernel,
        grid=(b, t // tm),
        in_specs=[pl.BlockSpec((1, tm, d), row),
                  pl.BlockSpec((1, 1, N_MOD, d), lambda bi, i: (bi, (i >= nl).astype(jnp.int32), 0, 0)),
                  pl.BlockSpec(wa.shape, const), pl.BlockSpec(wb.shape, const),
                  pl.BlockSpec(wc.shape, const), pl.BlockSpec(wab.shape, const),
                  pl.BlockSpec(gq.shape, const), pl.BlockSpec(gk.shape, const),
                  pl.BlockSpec((tm, A_Q_W), lambda bi, i: (i, 0)),
                  pl.BlockSpec((tm, A_Q_W), lambda bi, i: (i, 0)),
                  pl.BlockSpec(gs.shape, const)],
        out_specs=[o[1] for o in outs],
        out_shape=[o[0] for o in outs],
        compiler_params=_cparams(("parallel", "parallel")),
        name="in_projection",
    )(xa, mods, wa, wb, wc, wab, gq, gk, cos, sin, gs)


def _gqa_kernel(q_ref, k_ref, v_ref, o_ref, *, tq, ck, n_lat, n_tot):
    i = pl.program_id(1)
    g = A_HEADS // A_KV_HEADS

    def attend(key_lo, key_hi, n_split):
        qs = [jnp.concatenate([q_ref[0, :, (kh * g + j) * HEAD_DIM:(kh * g + j + 1) * HEAD_DIM] for j in range(g)],
                              axis=0) for kh in range(A_KV_HEADS)]
        step = (key_hi - key_lo) // n_split
        units = [(kh, key_lo + r * step, key_lo + (r + 1) * step) for r in range(n_split) for kh in range(A_KV_HEADS)]
        ss, ms, ps, oes = {}, {}, {}, {}
        for n in range(len(units) + 2):
            if n < len(units):
                kh, lo, hi = units[n]
                ss[n] = _dot_nt(k_ref[0, kh, lo:hi, :], qs[kh])
            if 0 <= n - 1 < len(units):
                ms[n - 1] = jnp.max(ss[n - 1], axis=0, keepdims=True)
                ps[n - 1] = jnp.exp2((ss.pop(n - 1) - ms[n - 1]).astype(BF16))
            if 0 <= n - 2 < len(units):
                kh, lo, hi = units[n - 2]
                oes[n - 2] = _dot(v_ref[0, kh, :, lo:hi], ps.pop(n - 2))
        for kh in range(A_KV_HEADS):
            mine = [n for n, u in enumerate(units) if u[0] == kh]
            m = functools.reduce(jnp.maximum, [ms[n] for n in mine])
            oe = sum(oes[n] * jnp.exp2(ms[n] - m) for n in mine)
            o = (oe[:HEAD_DIM] / oe[HEAD_DIM:HEAD_DIM + 1]).T.astype(BF16)
            for j in range(g):
                o_ref[0, :, (kh * g + j) * HEAD_DIM:(kh * g + j + 1) * HEAD_DIM] = o[j * tq:(j + 1) * tq]

    @pl.when(i < n_lat // tq)
    def _():
        attend(0, n_tot, ck)

    @pl.when(i >= n_lat // tq)
    def _():
        attend(n_lat, n_tot, 1)


def _gqa(qa, ka, va, n_lat, tq, ck):
    b, t, _ = qa.shape
    return pl.pallas_call(
        functools.partial(_gqa_kernel, tq=tq, ck=ck, n_lat=n_lat, n_tot=t),
        grid=(b, t // tq),
        in_specs=[pl.BlockSpec((1, tq, A_Q_W), lambda bi, i: (bi, i, 0)),
                  pl.BlockSpec((1, A_KV_HEADS, t, HEAD_DIM), lambda bi, i: (bi, 0, 0, 0)),
                  pl.BlockSpec((1, A_KV_HEADS, VT_ROWS, t), lambda bi, i: (bi, 0, 0, 0))],
        out_specs=pl.BlockSpec((1, tq, A_Q_W), lambda bi, i: (bi, i, 0)),
        out_shape=jax.ShapeDtypeStruct((b, t, A_Q_W), BF16),
        compiler_params=_cparams(("parallel", "parallel")),
        name="gqa",
    )(qa, ka, va)


def _na_window_start(r0, rows):
    return np.clip(r0 - NA_ROWS // 2, 0, rows - NA_ROWS - 1) // 2 * 2


def _na_classes(n_lat, n_tot):
    rows = n_lat // GRID_W
    assert rows >= NA_WIN and rows % 2 == 0
    geoms, table = [], []
    for r0 in range(0, rows, 2):
        geom = (int(_na_window_start(r0, rows)) - r0,) + tuple(
            int(np.clip(r0 + qr - NA_ROWS // 2, 0, rows - NA_ROWS)) - r0 for qr in range(2))
        if geom not in geoms:
            geoms.append(geom)
        table.append(geoms.index(geom))
    table += [len(geoms)] * ((n_tot - n_lat) // (2 * GRID_W))
    return geoms, np.asarray(table, np.int32)


def _na_bias_slabs(rpb, geoms):
    n_head = rpb.shape[0]
    cj = np.arange(GRID_W)[:, None]
    c = np.arange(GRID_W)[None, :]
    col_start = np.clip(c - NA_COLS // 2, 0, GRID_W - NA_COLS)
    col_ok = (cj >= col_start) & (cj < col_start + NA_COLS)
    col_off = np.clip(cj - c + (NA_COLS - 1), 0, 2 * NA_COLS - 2)
    spread = np.zeros((2 * NA_COLS - 1, GRID_W * GRID_W), np.float32)
    spread[col_off.reshape(-1), np.arange(GRID_W * GRID_W)] = 1.0
    planes = jnp.dot(rpb * LOG2E, jnp.asarray(spread), precision=HIGHEST)
    planes = planes.reshape(n_head, 2 * NA_ROWS - 1, GRID_W, GRID_W)
    kr = np.arange(NA_WIN)[:, None]
    qr = np.arange(2)[None, :]
    slabs = []
    for a, b0, b1 in geoms:
        first = np.where(qr == 0, b0, b1)
        row_ok = (a + kr >= first) & (a + kr < first + NA_ROWS)
        row_off = np.clip(a + kr - qr + (NA_ROWS - 1), 0, 2 * NA_ROWS - 2)
        rows = jnp.stack([jnp.stack([planes[:, int(row_off[i, j])] for j in range(2)], axis=1)
                          for i in range(NA_WIN)], axis=1)
        ok = row_ok[:, :, None, None] & col_ok[None, None]
        slab = jnp.where(ok[None], rows, NEG_BIG).transpose(0, 1, 3, 2, 4)
        slabs.append(slab.reshape(n_head, NA_WIN * GRID_W, 2 * GRID_W))
    slabs.append(jnp.full_like(slabs[0], NEG_BIG))
    return jnp.stack(slabs, axis=0).astype(F32)


def _na_kernel(cls_ref, q_ref, k_ref, vt_ref, bias_ref, o_ref, *, n_lat, n_tot):
    del cls_ref
    rows = n_lat // GRID_W
    r0 = 2 * pl.program_id(1)
    start = jnp.clip(r0 - NA_ROWS // 2, 0, rows - NA_ROWS - 1) // 2 * 2
    nb = pl.ds(pl.multiple_of(start * GRID_W, 2 * GRID_W), NA_WIN * GRID_W)
    hs = [slice(h * HEAD_DIM, (h + 1) * HEAD_DIM) for h in range(B_HEADS)]
    qs = [q_ref[0, :, hs[h]] for h in range(B_HEADS)]
    s_nb = [_dot_nt(k_ref[0, h, nb, :], qs[h]) + bias_ref[0, h] for h in range(B_HEADS)]
    s_cx = [_dot_nt(k_ref[0, h, n_lat:n_tot, :], qs[h]) for h in range(B_HEADS)]
    ms = [jnp.maximum(jnp.max(a, axis=0, keepdims=True), jnp.max(c, axis=0, keepdims=True))
          for a, c in zip(s_nb, s_cx)]
    p_nb = [jnp.exp2((a - m).astype(BF16)) for a, m in zip(s_nb, ms)]
    p_cx = [jnp.exp2((c - m).astype(BF16)) for c, m in zip(s_cx, ms)]
    for h in range(B_HEADS):
        oe = _dot(vt_ref[0, h, :, nb], p_nb[h]) + _dot(vt_ref[0, h, :, n_lat:n_tot], p_cx[h])
        o_ref[0, :, hs[h]] = (oe[:HEAD_DIM] / oe[HEAD_DIM:HEAD_DIM + 1]).T.astype(BF16)


def _neighborhood(qb, kb, vbt, slabs, table, n_lat):
    b, t, _ = qb.shape
    tq = 2 * GRID_W
    grid_spec = pltpu.PrefetchScalarGridSpec(
        num_scalar_prefetch=1,
        grid=(b, t // tq),
        in_specs=[pl.BlockSpec((1, tq, B_W), lambda bi, pi, cls: (bi, pi, 0)),
                  pl.BlockSpec((1, B_HEADS, t, HEAD_DIM), lambda bi, pi, cls: (bi, 0, 0, 0)),
                  pl.BlockSpec((1, B_HEADS, VT_ROWS, t), lambda bi, pi, cls: (bi, 0, 0, 0)),
                  pl.BlockSpec((1,) + slabs.shape[1:], lambda bi, pi, cls: (cls[pi], 0, 0, 0))],
        out_specs=pl.BlockSpec((1, tq, B_W), lambda bi, pi, cls: (bi, pi, 0)),
    )
    return pl.pallas_call(
        functools.partial(_na_kernel, n_lat=n_lat, n_tot=t),
        grid_spec=grid_spec,
        out_shape=jax.ShapeDtypeStruct((b, t, B_W), BF16),
        compiler_params=_cparams(("parallel", "parallel")),
        name="neighborhood",
    )(jnp.asarray(table), qb, kb, vbt, slabs)


GC_LANE, GL_LANE, BETA_LANE = 0, 2 * C_HEADS, 4 * C_HEADS


def _gdn_prep_kernel(x_ref, prev_ref, next_ref, ab_ref, cw_ref, alog_ref, dtb_ref, gs_ref,
                     q_ref, k_ref, v_ref, comp_ref, *, tm, n_lat, n_tot):
    i = pl.program_id(1)
    nl = n_lat // tm
    nt = n_tot // tm
    first = jnp.logical_or(i == 0, i == nl)
    last = jnp.logical_or(i == nl - 1, i == nt - 1)
    x = x_ref[0]
    pv = jnp.where(first, 0.0, prev_ref[0])
    nx = jnp.where(last, 0.0, next_ref[0])
    row = lax.broadcasted_iota(jnp.int32, (tm, 1), 0)
    xm1 = jnp.where(row == 0, pv[7:8], pltpu.roll(x, 1, 0))
    xm2 = jnp.where(row == 0, pv[6:7], jnp.where(row == 1, pv[7:8], pltpu.roll(x, 2, 0)))
    xp1 = jnp.where(row == tm - 1, nx[0:1], pltpu.roll(x, tm - 1, 0))
    cw = cw_ref[...]
    y = _silu(xm2 * cw[0:1] + xm1 * cw[1:2] + x * cw[2:3] + xp1 * cw[3:4])
    q = y[:, :C_W]
    k = y[:, C_W:2 * C_W]
    gs = gs_ref[...]
    q_ref[0] = q * lax.rsqrt(_dot_exact_rhs(q * q, gs, 2) + LN_EPS) * (HEAD_DIM ** -0.5)
    k_ref[0] = k * lax.rsqrt(_dot_exact_rhs(k * k, gs, 2) + LN_EPS)
    v_ref[0] = y[:, 2 * C_W:]
    ab = ab_ref[0]
    z = ab + dtb_ref[...]
    softplus = jnp.maximum(z, 0.0) + jnp.log(1.0 + jnp.exp(-jnp.abs(z)))
    lane = lax.broadcasted_iota(jnp.int32, ab.shape, 1)
    gb = jnp.where(lane < 2 * C_HEADS, -jnp.exp(alog_ref[...]) * softplus, _sigmoid(ab))
    ii = lax.broadcasted_iota(jnp.int32, (tm, tm), 0)
    jj = lax.broadcasted_iota(jnp.int32, (tm, tm), 1)
    same = (ii // CHUNK) == (jj // CHUNK)
    gb3 = _split_bf16(gb, 3)

    def masked_sum(mask):
        mb = jnp.where(mask, 1.0, 0.0).astype(BF16)
        return sum(_dot(mb, piece) for piece in gb3)

    prefix = masked_sum(jnp.logical_and(same, jj <= ii))
    suffix = masked_sum(jnp.logical_and(same, jj >= ii))
    total = masked_sum(same)
    comp = jnp.where(lane < C_HEADS, prefix,
                     jnp.where(lane < GL_LANE, suffix,
                               jnp.where(lane < BETA_LANE, pltpu.roll(total, GL_LANE, 1),
                                         jnp.where(lane < BETA_LANE + 2 * C_HEADS, pltpu.roll(gb, 2 * C_HEADS, 1),
                                                   0.0))))
    comp_ref[0] = comp


def _gdn_prep(qkvc, ab, conv_w, a_log, dt_bias, gs, n_lat, tm):
    b, t, _ = qkvc.shape
    c3 = 3 * C_W
    alog_row = jnp.zeros((1, LANE), F32).at[0, :2 * C_HEADS].set(a_log.reshape(-1))
    dtb_row = jnp.zeros((1, LANE), F32).at[0, :2 * C_HEADS].set(dt_bias.reshape(-1))
    row = lambda bi, i: (bi, i, 0)
    const2 = lambda bi, i: (0, 0)
    hb = tm // 8
    last8 = t // 8 - 1
    tok = jax.ShapeDtypeStruct((b, t, C_W), F32)
    return pl.pallas_call(
        functools.partial(_gdn_prep_kernel, tm=tm, n_lat=n_lat, n_tot=t),
        grid=(b, t // tm),
        in_specs=[pl.BlockSpec((1, tm, c3), row),
                  pl.BlockSpec((1, 8, c3), lambda bi, i: (bi, jnp.maximum(i * hb - 1, 0), 0)),
                  pl.BlockSpec((1, 8, c3), lambda bi, i: (bi, jnp.minimum((i + 1) * hb, last8), 0)),
                  pl.BlockSpec((1, tm, LANE), row),
                  pl.BlockSpec((CONV_K, c3), const2),
                  pl.BlockSpec((1, LANE), const2), pl.BlockSpec((1, LANE), const2),
                  pl.BlockSpec(gs.shape, const2)],
        out_specs=[pl.BlockSpec((1, tm, C_W), row)] * 3 + [pl.BlockSpec((1, tm, LANE), row)],
        out_shape=[tok, tok, tok, jax.ShapeDtypeStruct((b, t, LANE), F32)],
        compiler_params=_cparams(("parallel", "parallel")),
        name="gdn_prep",
    )(qkvc, qkvc, qkvc, ab, conv_w, alog_row, dtb_row, gs)


def _unit_tri_inverses(lms, eye, bd16, bd32):
    def mm(a, b_):
        return _dot(a.astype(BF16), b_.astype(BF16))

    lds = [jnp.where(bd16, lm, 0.0) for lm in lms]
    xs = [eye - ld for ld in lds]
    lds = [ld.astype(BF16) for ld in lds]
    pws = [_dot(ld, ld).astype(BF16) for ld in lds]
    for step in range(3):
        xs = [x + mm(x, pw) for x, pw in zip(xs, pws)]
        if step < 2:
            pws = [_dot(pw, pw).astype(BF16) for pw in pws]
    for level in (lambda lm: jnp.where(jnp.logical_and(bd32, jnp.logical_not(bd16)), lm, 0.0),
                  lambda lm: jnp.where(bd32, 0.0, lm)):
        xbs = [x.astype(BF16) for x in xs]
        ys = [_dot(level(lm).astype(BF16), xb) for lm, xb in zip(lms, xbs)]
        xs = [x - _dot(xb, y.astype(BF16)) for x, xb, y in zip(xs, xbs, ys)]
    return xs


def _gdn_local_kernel(q_ref, k_ref, v_ref, comp_ref, ex_ref, sel_ref, u_ref, w_ref, kd_ref, qd_ref, in_ref):
    n = q_ref.shape[1]
    hp = 2 * HEAD_DIM
    q = q_ref[0]
    k = k_ref[0]
    v = v_ref[0]
    e = _dot_exact_rhs(comp_ref[0], ex_ref[0])
    ii = lax.broadcasted_iota(jnp.int32, (n, n), 0)
    jj = lax.broadcasted_iota(jnp.int32, (n, n), 1)
    same = (ii // CHUNK) == (jj // CHUNK)
    eye = (ii == jj).astype(F32)
    bd16 = (ii // 16) == (jj // 16)
    bd32 = (ii // 32) == (jj // 32)
    lane = lax.broadcasted_iota(jnp.int32, (1, hp), 1)
    lane2 = lax.broadcasted_iota(jnp.int32, (1, 2 * hp), 1)
    kh = [jnp.where((lane // HEAD_DIM) == hh, k, 0.0).astype(BF16) for hh in range(2)]
    lms, intras, rhss = [], [], []
    for p in range(2):
        gc = e[:, p * hp:(p + 1) * hp]
        gl = e[:, (2 + p) * hp:(3 + p) * hp]
        beta = e[:, (4 + p) * hp:(5 + p) * hp]
        eg = jnp.exp(gc)
        kb = k * beta
        kd_ref[p, 0] = (k * jnp.exp(gl - gc)).astype(BF16)
        qd_ref[p, 0] = (q * eg).astype(BF16)
        rhs = jnp.concatenate([v * beta, kb * eg], axis=1)
        kbq = jnp.concatenate([kb, q], axis=0).astype(BF16)
        incl = jnp.logical_and(same, (ii >= jj) if p == 0 else (ii <= jj))
        strict = jnp.logical_and(same, (ii > jj) if p == 0 else (ii < jj))
        for hh in range(2):
            gcb = jnp.broadcast_to(gc[:, hh * HEAD_DIM:hh * HEAD_DIM + 1], (n, n))
            decay = jnp.exp(jnp.where(incl, gcb - gcb.T, NEG_BIG))
            aq = _dot_nt(kbq, kh[hh])
            lms.append(jnp.where(strict, aq[:n] * decay, 0.0))
            intras.append(jnp.where(incl, aq[n:] * decay, 0.0).astype(BF16))
            rhss.append(jnp.where(((lane2 % hp) // HEAD_DIM) == hh, rhs, 0.0).astype(BF16))
    tmats = _unit_tri_inverses(lms, eye, bd16, bd32)
    for p in range(2):
        uw = sum(_dot(tmats[2 * p + hh].astype(BF16), rhss[2 * p + hh]) for hh in range(2))
        intra2 = sum(_dot(intras[2 * p + hh], sel_ref[hh]) for hh in range(2))
        u_ref[p, 0] = uw[:, :hp].astype(BF16)
        w_ref[p, 0] = uw[:, hp:].astype(BF16)
        in_ref[p, 0] = intra2.astype(BF16)


def _gdn_local(q, k, v, comp, blk):
    b, t, _ = q.shape
    hp = 2 * HEAD_DIM
    n_pair = C_W // hp
    ex = np.zeros((n_pair, LANE, 6 * hp), np.float32)
    for pr in range(n_pair):
        for p in range(2):
            for hh in range(2):
                src = p * C_HEADS + pr * 2 + hh
                for kind, base in enumerate((GC_LANE, GL_LANE, BETA_LANE)):
                    col = (2 * kind + p) * hp + hh * HEAD_DIM
                    ex[pr, base + src, col:col + HEAD_DIM] = 1.0
    sel = np.zeros((2, blk, hp), np.float32)
    for hh in range(2):
        sel[hh, np.arange(blk), hh * HEAD_DIM + np.arange(blk) % CHUNK] = 1.0
    tok = lambda bi, pr, i: (bi, i, pr)
    dirs = lambda bi, pr, i: (0, bi, i, pr)
    out = jax.ShapeDtypeStruct((2, b, t, C_W), BF16)
    return pl.pallas_call(
        _gdn_local_kernel,
        grid=(b, n_pair, t // blk),
        in_specs=[pl.BlockSpec((1, blk, hp), tok)] * 3
                 + [pl.BlockSpec((1, blk, LANE), lambda bi, pr, i: (bi, i, 0)),
                    pl.BlockSpec((1, LANE, 6 * hp), lambda bi, pr, i: (pr, 0, 0)),
                    pl.BlockSpec((2, blk, hp), lambda bi, pr, i: (0, 0, 0))],
        out_specs=[pl.BlockSpec((2, 1, blk, hp), dirs)] * 5,
        out_shape=[out] * 5,
        compiler_params=_cparams(("parallel", "parallel", "parallel")),
        name="gdn_local",
    )(q, k, v, comp, jnp.asarray(ex, dtype=BF16), jnp.asarray(sel, dtype=BF16))


def _gdn_scan_kernel(*refs, ncb, nb):
    ins = (refs[0:6], refs[6:12])
    ex_ref = refs[12]
    outs = refs[13:15]
    s_ref = refs[15]
    j = pl.program_id(1)

    @pl.when(j == 0)
    def _():
        s_ref[...] = jnp.zeros_like(s_ref)

    gw = 4 * HEAD_DIM
    n_grp = C_W // gw
    lane_head = lax.broadcasted_iota(jnp.int32, (1, gw), 1) // HEAD_DIM
    rr = lax.broadcasted_iota(jnp.int32, (gw, gw), 0) // HEAD_DIM
    cc_ = lax.broadcasted_iota(jnp.int32, (gw, gw), 1) // HEAD_DIM
    blockdiag = rr == cc_
    chains = [(bi, p, g) for bi in range(nb) for p in range(2) for g in range(n_grp)]
    for cc in range(ncb):
        rows = [slice(cc * CHUNK, (cc + 1) * CHUNK), slice((ncb - 1 - cc) * CHUNK, (ncb - cc) * CHUNK)]
        sdec = {(bi, p): jnp.exp(_dot_exact_rhs(ins[p][5][bi, rows[p].start:rows[p].start + 8, :], ex_ref[...]))
                for bi in range(nb) for p in range(2)}
        lanes = [slice(g * gw, (g + 1) * gw) for g in range(n_grp)]
        ss = [s_ref[bi, p, g] for bi, p, g in chains]
        sbs = [s.astype(BF16) for s in ss]
        v_news = [ins[p][0][0, bi, rows[p], lanes[g]].astype(F32) - _dot(ins[p][1][0, bi, rows[p], lanes[g]], sb)
                  for (bi, p, g), sb in zip(chains, sbs)]
        vstacks = [jnp.concatenate([jnp.where(lane_head == h, v_new, 0.0).astype(BF16)
                                    for h in range(gw // HEAD_DIM)], axis=0) for v_new in v_news]
        for (bi, p, g), s, sb, v_new, vstack in zip(chains, ss, sbs, v_news, vstacks):
            o = _dot(ins[p][3][0, bi, rows[p], lanes[g]], sb) + _dot(ins[p][4][0, bi, rows[p], lanes[g]], vstack)
            upd = _dot_tn(ins[p][2][0, bi, rows[p], lanes[g]], v_new.astype(BF16))
            decay = sdec[bi, p][0:1, p * C_W + g * gw:p * C_W + (g + 1) * gw]
            s_ref[bi, p, g] = s * decay + jnp.where(blockdiag, upd, 0.0)
            outs[p][bi, rows[p], lanes[g]] = o


def _gdn_scan(u, w, kd, qd, intra, comp, n_lat, blk):
    _, b, t, _ = u.shape
    nl = n_lat // blk
    nx = (t - n_lat) // blk
    ncb = blk // CHUNK
    ex = np.zeros((LANE, 2 * C_W), np.float32)
    for p in range(2):
        for h in range(C_HEADS):
            ex[GL_LANE + p * C_HEADS + h, p * C_W + h * HEAD_DIM:p * C_W + (h + 1) * HEAD_DIM] = 1.0

    def blk0(j):
        return jnp.where(j < nx, nl + j, j - nx)

    def blk1(j):
        return jnp.where(j < nx, nl + nx - 1 - j, nl - 1 - (j - nx))

    nb = SCAN_BATCH if b % SCAN_BATCH == 0 else 1
    specs = []
    for p, bf in enumerate((blk0, blk1)):
        specs += [pl.BlockSpec((1, nb, blk, C_W), lambda bi, j, p=p, bf=bf: (p, bi, bf(j), 0))] * 5
        specs += [pl.BlockSpec((nb, blk, LANE), lambda bi, j, bf=bf: (bi, bf(j), 0))]
    specs += [pl.BlockSpec((LANE, 2 * C_W), lambda bi, j: (0, 0))]
    out = jax.ShapeDtypeStruct((b, t, C_W), F32)
    gw = 4 * HEAD_DIM
    return pl.pallas_call(
        functools.partial(_gdn_scan_kernel, ncb=ncb, nb=nb),
        grid=(b // nb, nl + nx),
        in_specs=specs,
        out_specs=[pl.BlockSpec((nb, blk, C_W), lambda bi, j: (bi, blk0(j), 0)),
                   pl.BlockSpec((nb, blk, C_W), lambda bi, j: (bi, blk1(j), 0))],
        out_shape=[out, out],
        scratch_shapes=[pltpu.VMEM((nb, 2, C_W // gw, gw, gw), F32)],
        compiler_params=_cparams(("parallel", "arbitrary")),
        name="gdn_scan",
    )(u, w, kd, qd, intra, comp, u, w, kd, qd, intra, comp, jnp.asarray(ex, dtype=BF16))


def _merge_kernel(x_ref, mod_ref, oa_ref, ob_ref, oc0_ref, oc1_ref, gate_ref, wmg_ref, wbr_ref, wout_ref, og_ref,
                  gs_ref, lng_ref, lnb_ref, o_ref, *, alpha):
    x = x_ref[0]
    d = x.shape[-1]
    mod = mod_ref[0, 0]
    h = (_ln(x) * (1.0 + mod[1:2]) + mod[0:1]).astype(BF16)
    gates = _sigmoid(_dot(h, wmg_ref[...]))
    oc = oc0_ref[0] + oc1_ref[0]
    ocn = oc * lax.rsqrt(_dot_exact_rhs(oc * oc, gs_ref[...], 2) + LN_EPS) * og_ref[...] * _silu(gate_ref[0])
    m = gates[:, :d] * _dot(oa_ref[0], wbr_ref[0])
    m = m + gates[:, d:2 * d] * _dot(ob_ref[0], wbr_ref[1])
    m = m + gates[:, 2 * d:] * _dot(ocn.astype(BF16), wbr_ref[2])
    mix = _dot(m.astype(BF16), wout_ref[...])
    y = alpha * x + mod[2:3] * mix
    o_ref[0] = _ln(y) * lng_ref[...] + lnb_ref[...]


def _merge(xa, mods, oa, ob, oc0, oc1, gate_c, wmg, wbr, wout, og, gs, lng, lnb, n_lat, tm, alpha):
    b, t, d = xa.shape
    nl = n_lat // tm
    row = lambda bi, i: (bi, i, 0)
    const2 = lambda bi, i: (0, 0)
    return pl.pallas_call(
        functools.partial(_merge_kernel, alpha=alpha),
        grid=(b, t // tm),
        in_specs=[pl.BlockSpec((1, tm, d), row),
                  pl.BlockSpec((1, 1, N_MOD, d), lambda bi, i: (bi, (i >= nl).astype(jnp.int32), 0, 0)),
                  pl.BlockSpec((1, tm, BRANCH_W), row), pl.BlockSpec((1, tm, BRANCH_W), row),
                  pl.BlockSpec((1, tm, BRANCH_W), row), pl.BlockSpec((1, tm, BRANCH_W), row),
                  pl.BlockSpec((1, tm, BRANCH_W), row),
                  pl.BlockSpec(wmg.shape, const2),
                  pl.BlockSpec(wbr.shape, lambda bi, i: (0, 0, 0)),
                  pl.BlockSpec(wout.shape, const2),
                  pl.BlockSpec(og.shape, const2), pl.BlockSpec(gs.shape, const2),
                  pl.BlockSpec(lng.shape, const2), pl.BlockSpec(lnb.shape, const2)],
        out_specs=pl.BlockSpec((1, tm, d), row),
        out_shape=jax.ShapeDtypeStruct((b, t, d), F32),
        compiler_params=_cparams(("parallel", "parallel")),
        name="merge",
    )(xa, mods, oa, ob, oc0, oc1, gate_c, wmg, wbr, wout, og, gs, lng, lnb)


def _route_kernel(x_ref, mod_ref, wr_ref, h_ref, aff_ref, *, n_exp):
    x = x_ref[0]
    mod = mod_ref[0, 0]
    h = _ln(x) * (1.0 + mod[4:5]) + mod[3:4]
    h_ref[0] = h.astype(BF16)
    logits = _dot_hi(h, wr_ref[...])
    lane = lax.broadcasted_iota(jnp.int32, logits.shape, 1)
    logits = jnp.where(lane < n_exp, logits, NEG_BIG)
    e = jnp.exp(logits - jnp.max(logits, axis=-1, keepdims=True))
    aff_ref[0] = e / jnp.sum(e, axis=-1, keepdims=True)


def _route(xa, mods, wr, n_lat, tm, n_exp):
    b, t, d = xa.shape
    nl = n_lat // tm
    row = lambda bi, i: (bi, i, 0)
    return pl.pallas_call(
        functools.partial(_route_kernel, n_exp=n_exp),
        grid=(b, t // tm),
        in_specs=[pl.BlockSpec((1, tm, d), row),
                  pl.BlockSpec((1, 1, N_MOD, d), lambda bi, i: (bi, (i >= nl).astype(jnp.int32), 0, 0)),
                  pl.BlockSpec(wr.shape, lambda bi, i: (0, 0))],
        out_specs=[pl.BlockSpec((1, tm, d), row), pl.BlockSpec((1, tm, LANE), row)],
        out_shape=[jax.ShapeDtypeStruct((b, t, d), BF16), jax.ShapeDtypeStruct((b, t, LANE), F32)],
        compiler_params=_cparams(("parallel", "parallel")),
        name="route",
    )(xa, mods, wr)


def _ffn_kernel(x_ref, wg_ref, wu_ref, wd_ref, gate_ref, o_ref, acc_ref):
    f = pl.program_id(2)

    @pl.when(f == 0)
    def _():
        acc_ref[...] = jnp.zeros_like(acc_ref)

    x = x_ref[0]
    g = _dot(x, wg_ref[0, 0].astype(BF16))
    u = _dot(x, wu_ref[0, 0].astype(BF16))
    a = (_silu(g) * u).astype(BF16)
    acc_ref[...] += _dot(a, wd_ref[0, 0].astype(BF16))

    @pl.when(f == pl.num_programs(2) - 1)
    def _():
        o_ref[0] = (acc_ref[...] * gate_ref[0]).astype(BF16)


def _expert_ffn(xs, gate, w_gate_up, w_down, layer, tm, tf):
    e, r, d = xs.shape
    f = w_down.shape[2]
    nf = f // tf
    return pl.pallas_call(
        _ffn_kernel,
        grid=(e, r // tm, nf),
        in_specs=[pl.BlockSpec((1, tm, d), lambda ei, i, fi: (ei, i, 0)),
                  pl.BlockSpec((1, 1, d, tf), lambda ei, i, fi: (layer, ei, 0, fi)),
                  pl.BlockSpec((1, 1, d, tf), lambda ei, i, fi: (layer, ei, 0, nf + fi)),
                  pl.BlockSpec((1, 1, tf, d), lambda ei, i, fi: (layer, ei, fi, 0)),
                  pl.BlockSpec((1, tm, 1), lambda ei, i, fi: (ei, i, 0))],
        out_specs=pl.BlockSpec((1, tm, d), lambda ei, i, fi: (ei, i, 0)),
        out_shape=jax.ShapeDtypeStruct((e, r, d), BF16),
        scratch_shapes=[pltpu.VMEM((tm, d), F32)],
        compiler_params=_cparams(("parallel", "parallel", "arbitrary")),
        name="expert_ffn",
    )(xs, w_gate_up, w_gate_up, w_down, gate)


VALID, FIRST, LAST = 1, 2, 4


def _combine_plan(tok_of_pair, n_tok, tm, rwin):
    bsz, n_pair = tok_of_pair.shape
    nt = n_tok // tm
    n_win = n_pair // rwin
    n_item = nt + n_win
    order = jnp.argsort(tok_of_pair, axis=1)
    tok = jnp.take_along_axis(tok_of_pair, order, axis=1)
    bounds = jnp.arange(nt + 1, dtype=jnp.int32) * tm
    off = jax.vmap(lambda row: jnp.searchsorted(row, bounds, side="left"))(tok).astype(jnp.int32)
    first = jnp.minimum(off[:, :-1] // rwin, n_win - 1)
    last = jnp.minimum(jnp.maximum(off[:, 1:] - 1, off[:, :-1]) // rwin, n_win - 1)
    n_items = last - first + 1
    cum = jnp.cumsum(n_items, axis=1)
    k = jnp.arange(n_item, dtype=jnp.int32)
    tile = jax.vmap(lambda row: jnp.searchsorted(row, k, side="right"))(cum).astype(jnp.int32)
    valid = tile < nt
    tile = jnp.minimum(tile, nt - 1)
    end = jnp.take_along_axis(cum, tile, axis=1)
    start = end - jnp.take_along_axis(n_items, tile, axis=1)
    win = jnp.where(valid, jnp.take_along_axis(first, tile, axis=1) + (k[None] - start),
                    jnp.take_along_axis(last, tile, axis=1))
    flags = (valid * VALID + (valid & (k[None] == start)) * FIRST + (valid & (k[None] == end - 1)) * LAST)
    return order, tok, tile, win.astype(jnp.int32), flags.astype(jnp.int32)


def _combine_kernel(tile_ref, win_ref, flag_ref, y_ref, tok_ref, x_ref, mod_ref, lng_ref, lnb_ref, o_ref, acc_ref, *,
                    alpha, tm):
    b = pl.program_id(0)
    k = pl.program_id(1)
    flags = flag_ref[b, k]

    @pl.when((flags & FIRST) != 0)
    def _():
        acc_ref[...] = jnp.zeros_like(acc_ref)

    @pl.when((flags & VALID) != 0)
    def _():
        tok0 = tile_ref[b, k] * tm
        rows = lax.broadcasted_iota(jnp.int32, (tm, tok_ref.shape[-1]), 0) + tok0
        pick = jnp.where(rows == tok_ref[0, 0], 1.0, 0.0).astype(BF16)
        acc_ref[...] += _dot(pick, y_ref[0])

    @pl.when((flags & LAST) != 0)
    def _():
        mod = mod_ref[0, 0]
        y = alpha * x_ref[0] + mod[5:6] * acc_ref[...]
        o_ref[0] = _ln(y) * lng_ref[...] + lnb_ref[...]


def _combine(y_sorted, tok_sorted, tile, win, flags, xa, mods, lng, lnb, n_lat, tm, rwin, alpha):
    b, t, d = xa.shape
    nl = n_lat // tm
    n_item = tile.shape[1]
    grid_spec = pltpu.PrefetchScalarGridSpec(
        num_scalar_prefetch=3,
        grid=(b, n_item),
        in_specs=[pl.BlockSpec((1, rwin, d), lambda bi, k, tl, wn, fl: (bi, wn[bi, k], 0)),
                  pl.BlockSpec((1, 1, 1, rwin), lambda bi, k, tl, wn, fl: (bi, wn[bi, k], 0, 0)),
                  pl.BlockSpec((1, tm, d), lambda bi, k, tl, wn, fl: (bi, tl[bi, k], 0)),
                  pl.BlockSpec((1, 1, N_MOD, d),
                               lambda bi, k, tl, wn, fl: (bi, (tl[bi, k] >= nl).astype(jnp.int32), 0, 0)),
                  pl.BlockSpec(lng.shape, lambda bi, k, tl, wn, fl: (0, 0)),
                  pl.BlockSpec(lnb.shape, lambda bi, k, tl, wn, fl: (0, 0))],
        out_specs=pl.BlockSpec((1, tm, d), lambda bi, k, tl, wn, fl: (bi, tl[bi, k], 0)),
        scratch_shapes=[pltpu.VMEM((tm, d), F32)],
    )
    return pl.pallas_call(
        functools.partial(_combine_kernel, alpha=alpha, tm=tm),
        grid_spec=grid_spec,
        out_shape=jax.ShapeDtypeStruct((b, t, d), F32),
        compiler_params=_cparams(("parallel", "arbitrary")),
        name="moe_combine",
    )(tile, win, flags, y_sorted, tok_sorted.reshape(b, -1, 1, rwin), xa, mods, lng, lnb)


def _moe(xa, mods, w_router, w_gate_up, w_down, lng, lnb, layer, n_lat, tm, alpha):
    b, t, d = xa.shape
    n_exp = w_router.shape[-1]
    m_ctx = t - n_lat
    wr = jnp.zeros((d, LANE), F32).at[:, :n_exp].set(w_router)
    h, aff = _route(xa, mods, wr, n_lat, tm, n_exp)
    aff = jnp.swapaxes(aff[:, :, :n_exp], 1, 2)
    cap_l = CAPACITY * n_lat // n_exp
    cap_x = CAPACITY * m_ctx // n_exp
    gate_l, idx_l = lax.top_k(aff[:, :, :n_lat], cap_l)
    gate_x, idx_x = lax.top_k(aff[:, :, n_lat:], cap_x)
    idx = jnp.concatenate([idx_l, idx_x + n_lat], axis=-1)
    gate = jnp.concatenate([gate_l, gate_x], axis=-1)
    cap = cap_l + cap_x
    idx_e = jnp.swapaxes(idx, 0, 1)
    bidx = jnp.arange(b)[None, :, None]
    xs = h[bidx, idx_e].reshape(n_exp, b * cap, d)
    gate_e = jnp.swapaxes(gate, 0, 1).reshape(n_exp, b * cap, 1)
    rows = b * cap
    tmr = rows
    for cand in (1088, 1024, 512, 256, 128, 64, 32, 16):
        if rows % cand == 0:
            tmr = cand
            break
    tf = min(512, w_down.shape[2])
    y = _expert_ffn(xs, gate_e, w_gate_up, w_down, layer, tmr, tf)
    n_pair = n_exp * cap
    rwin = next(r for r in (512, 256, 128, 64, 32, 16, 8) if n_pair % r == 0)
    order, tok, tile, win, flags = _combine_plan(idx.reshape(b, n_pair), t, tm, rwin)
    y_sorted = y[order // cap, jnp.arange(b)[:, None] * cap + order % cap]
    return _combine(y_sorted, tok, tile, win, flags, xa, mods, lng, lnb, n_lat, tm, rwin, alpha)


def _rope_tables(n_lat, m_ctx):
    t = np.arange(n_lat)
    rows = (t // GRID_W).astype(np.float32)
    cols = (t % GRID_W).astype(np.float32)
    n_freq = HEAD_DIM // 4
    inv_freq = jnp.asarray(ROPE_THETA, F32) ** (-jnp.arange(n_freq, dtype=F32) / n_freq)
    ang_r = jnp.asarray(rows)[:, None] * inv_freq
    ang_c = jnp.asarray(cols)[:, None] * inv_freq
    cr, sr, cc, sc = jnp.cos(ang_r), jnp.sin(ang_r), jnp.cos(ang_c), jnp.sin(ang_c)
    cos = jnp.concatenate([cr, cr, cc, cc], axis=-1)
    sin = jnp.concatenate([-sr, sr, -sc, sc], axis=-1)
    cos = jnp.concatenate([cos, jnp.ones((m_ctx, HEAD_DIM), F32)], axis=0)
    sin = jnp.concatenate([sin, jnp.zeros((m_ctx, HEAD_DIM), F32)], axis=0)
    return jnp.tile(cos, (1, A_HEADS)), jnp.tile(sin, (1, A_HEADS))


def kernel(x, c, ctx, c_ctx, w_mod, b_mod, w_in, qk_gain, rpb, conv_w, a_log, dt_bias, o_gain, w_branch, w_out,
           ln1_g, ln1_b, w_router, w_gate_up, w_down, ln2_g, ln2_b):
    b, n_lat, d = x.shape
    m_ctx = ctx.shape[1]
    depth = w_mod.shape[0]
    alpha = (2 * depth) ** 0.25
    tm = math.gcd(256, math.gcd(n_lat, m_ctx))
    blk = tm

    xa = jnp.concatenate([x, ctx], axis=1)
    n_rows = -(-(b + 1) // 8) * 8
    cc = jnp.zeros((n_rows, d), F32).at[:b].set(c).at[b].set(c_ctx)
    mod_all = _modulation(cc, w_mod, b_mod).reshape(depth, n_rows, N_MOD, d)
    cos, sin = _rope_tables(n_lat, m_ctx)
    gs_mean = _group_sum_matrix(A_Q_W, 1.0 / HEAD_DIM)
    gs_sum = _group_sum_matrix(C_W, 1.0)
    na_geoms, na_table = _na_classes(n_lat, n_lat + m_ctx)

    offs = np.cumsum((0, A_Q_W, A_KV_W, A_KV_W, B_W, B_W, B_W, 3 * C_W, C_W, 2 * C_HEADS, 2 * C_HEADS, N_BRANCH * d))
    for layer in range(depth):
        ml = mod_all[layer]
        mods = jnp.stack([ml[:b], jnp.broadcast_to(ml[b][None], (b, N_MOD, d))], axis=1)
        wl = w_in[layer]
        wa = wl[:, offs[0]:offs[3]].astype(BF16)
        wb = wl[:, offs[3]:offs[6]].astype(BF16)
        wc = wl[:, offs[6]:offs[8]].astype(BF16)
        wab = jnp.zeros((d, LANE), F32).at[:, :4 * C_HEADS].set(wl[:, offs[8]:offs[10]]).astype(BF16)
        wmg = wl[:, offs[10]:offs[11]].astype(BF16)
        gq = jnp.tile(qk_gain[layer, 0], A_HEADS)[None]
        gk = jnp.tile(qk_gain[layer, 1], A_KV_HEADS)[None]
        qa, ka, va, qb, kb, vb, qkvc, gate_c, ab = _in_projection(
            xa, mods, wa, wb, wc, wab, gq, gk, cos, sin, gs_mean, n_lat, tm)
        oa = _gqa(qa, ka, va, n_lat, min(GQA_TQ, tm), 2 if (n_lat + m_ctx) % (2 * LANE) == 0 else 1)
        ob = _neighborhood(qb, kb, vb, _na_bias_slabs(rpb[layer], na_geoms), na_table, n_lat)
        qc, kc, vc, comp = _gdn_prep(qkvc, ab, conv_w[layer], a_log[layer], dt_bias[layer], gs_sum, n_lat, tm)
        u, w, kd, qd, intra = _gdn_local(qc, kc, vc, comp, blk)
        oc0, oc1 = _gdn_scan(u, w, kd, qd, intra, comp, n_lat, blk)
        og = jnp.tile(o_gain[layer], C_HEADS)[None]
        xa = _merge(xa, mods, oa, ob, oc0, oc1, gate_c, wmg, w_branch[layer].astype(BF16),
                    w_out[layer].astype(BF16), og, gs_mean, ln1_g[layer][None], ln1_b[layer][None], n_lat, tm, alpha)
        xa = _moe(xa, mods, w_router[layer], w_gate_up, w_down, ln2_g[layer][None], ln2_b[layer][None], layer,
                  n_lat, tm, alpha)
    return xa[:, :n_lat]
```

```python
import functools
import math

import numpy as np
import jax
import jax.numpy as jnp
from jax import lax
from jax.experimental import pallas as pl
from jax.experimental.pallas import tpu as pltpu

F32 = jnp.float32
BF16 = jnp.bfloat16
HIGHEST = lax.Precision.HIGHEST

GRID_W = 64
HEAD_DIM = 64
A_HEADS = 8
A_KV_HEADS = 2
B_HEADS = 8
C_HEADS = 8
NA_ROWS = 8
NA_COLS = 16
CONV_K = 4
CHUNK = 64
N_BRANCH = 3
BRANCH_W = 512
CAPACITY = 2
N_MOD = 6
LN_EPS = 1e-6
ROPE_THETA = 10000.0
NEG_BIG = -1e30
LOG2E = 1.4426950408889634

A_Q_W = A_HEADS * HEAD_DIM
A_KV_W = A_KV_HEADS * HEAD_DIM
B_W = B_HEADS * HEAD_DIM
C_W = C_HEADS * HEAD_DIM
LANE = 128
VMEM_LIMIT = 56 * 1024 * 1024
GQA_TQ = 128
LOCAL_PAIRS = 2
SCAN_BATCH = 2
VT_ROWS = HEAD_DIM + 16
NA_WIN = NA_ROWS + 2


def _cparams(sem):
    return pltpu.CompilerParams(dimension_semantics=sem, vmem_limit_bytes=VMEM_LIMIT)


def _dot(a, b):
    return jnp.dot(a, b, preferred_element_type=F32)


def _dot_hi(a, b):
    return jnp.dot(a, b, precision=HIGHEST, preferred_element_type=F32)


def _dot_nt(a, b):
    return lax.dot_general(a, b, (((1,), (1,)), ((), ())), preferred_element_type=F32)


def _dot_tn(a, b):
    return lax.dot_general(a, b, (((0,), (0,)), ((), ())), preferred_element_type=F32)


def _split_bf16(x, pieces):
    out = []
    for _ in range(pieces - 1):
        h = x.astype(BF16)
        out.append(h)
        x = x - h.astype(F32)
    out.append(x.astype(BF16))
    return out


def _dot_exact_rhs(a, b, pieces=3):
    return sum(_dot(p, b) for p in _split_bf16(a, pieces))


def _dot_exact_lhs(a, b, pieces=3):
    return sum(_dot(a, p) for p in _split_bf16(b, pieces))


def _ln(x):
    mu = jnp.mean(x, axis=-1, keepdims=True)
    xc = x - mu
    var = jnp.mean(xc * xc, axis=-1, keepdims=True)
    return xc * lax.rsqrt(var + LN_EPS)


def _sigmoid(x):
    return 1.0 / (1.0 + jnp.exp(-x))


def _silu(x):
    return x * _sigmoid(x)


def _group_sum_matrix(width, scale):
    g = np.arange(width) // HEAD_DIM
    return jnp.asarray((g[:, None] == g[None, :]).astype(np.float32) * scale, dtype=BF16)


def _mod_kernel(c_ref, w_ref, b_ref, o_ref):
    c = c_ref[...]
    o_ref[0] = _dot_hi(_silu(c), w_ref[0]) + b_ref[0]


def _modulation(cc, w_mod, b_mod):
    depth, d, dm = w_mod.shape
    r = cc.shape[0]
    tn = min(dm, 1536)
    return pl.pallas_call(
        _mod_kernel,
        grid=(depth, dm // tn),
        in_specs=[pl.BlockSpec((r, d), lambda l, j: (0, 0)),
                  pl.BlockSpec((1, d, tn), lambda l, j: (l, 0, j)),
                  pl.BlockSpec((1, 1, tn), lambda l, j: (l, 0, j))],
        out_specs=pl.BlockSpec((1, r, tn), lambda l, j: (l, 0, j)),
        out_shape=jax.ShapeDtypeStruct((depth, r, dm), F32),
        compiler_params=_cparams(("parallel", "parallel")),
        name="modulation",
    )(cc, w_mod, b_mod.reshape(depth, 1, dm))


def _rope_swap(y):
    w = y.shape[-1]
    lane = lax.broadcasted_iota(jnp.int32, y.shape, 1)
    up = pltpu.roll(y, w - 16, 1)
    down = pltpu.roll(y, 16, 1)
    return jnp.where((lane % 32) < 16, up, down)


def _inproj_kernel(x_ref, mod_ref, wa_ref, wb_ref, wc_ref, wab_ref, gq_ref, gk_ref, cos_ref, sin_ref,
                   gs_ref, qa_ref, ka_ref, va_ref, qb_ref, kb_ref, vb_ref, qkvc_ref, gatec_ref, ab_ref):
    x = x_ref[0]
    mod = mod_ref[0, 0]
    h = (_ln(x) * (1.0 + mod[1:2]) + mod[0:1]).astype(BF16)
    scale = HEAD_DIM ** -0.5 * LOG2E

    za = _dot(h, wa_ref[...])
    q = za[:, :A_Q_W]
    k = za[:, A_Q_W:A_Q_W + A_KV_W]
    v = za[:, A_Q_W + A_KV_W:]
    cos = cos_ref[...]
    sin = sin_ref[...]
    gs = gs_ref[...]
    qn = q * lax.rsqrt(_dot_exact_rhs(q * q, gs, 2) + LN_EPS) * gq_ref[...]
    qn = qn * cos + _rope_swap(qn) * sin
    qa_ref[0] = (qn * scale).astype(BF16)
    kn = k * lax.rsqrt(_dot_exact_rhs(k * k, gs[:A_KV_W, :A_KV_W], 2) + LN_EPS) * gk_ref[...]
    kn = kn * cos[:, :A_KV_W] + _rope_swap(kn) * sin[:, :A_KV_W]
    ones = jnp.ones((VT_ROWS - HEAD_DIM, x.shape[0]), F32)
    vt = v.T
    for j in range(A_KV_HEADS):
        ka_ref[0, j] = kn[:, j * HEAD_DIM:(j + 1) * HEAD_DIM].astype(BF16)
        va_ref[0, j] = jnp.concatenate([vt[j * HEAD_DIM:(j + 1) * HEAD_DIM], ones], axis=0).astype(BF16)

    zb = _dot(h, wb_ref[...])
    qb_ref[0] = (zb[:, :B_W] * scale).astype(BF16)
    kbv = zb[:, B_W:2 * B_W]
    vt = zb[:, 2 * B_W:].T
    for j in range(B_HEADS):
        kb_ref[0, j] = kbv[:, j * HEAD_DIM:(j + 1) * HEAD_DIM].astype(BF16)
        vb_ref[0, j] = jnp.concatenate([vt[j * HEAD_DIM:(j + 1) * HEAD_DIM], ones], axis=0).astype(BF16)

    zc = _dot(h, wc_ref[...])
    qkvc_ref[0] = zc[:, :3 * C_W]
    gatec_ref[0] = zc[:, 3 * C_W:]
    ab_ref[0] = _dot(h, wab_ref[...])


def _in_projection(xa, mods, wa, wb, wc, wab, gq, gk, cos, sin, gs, n_lat, tm):
    b, t, d = xa.shape
    nl = n_lat // tm
    row = lambda bi, i: (bi, i, 0)
    const = lambda bi, i: (0, 0)
    outs = [
        (jax.ShapeDtypeStruct((b, t, A_Q_W), BF16), pl.BlockSpec((1, tm, A_Q_W), row)),
        (jax.ShapeDtypeStruct((b, A_KV_HEADS, t, HEAD_DIM), BF16),
         pl.BlockSpec((1, A_KV_HEADS, tm, HEAD_DIM), lambda bi, i: (bi, 0, i, 0))),
        (jax.ShapeDtypeStruct((b, A_KV_HEADS, VT_ROWS, t), BF16),
         pl.BlockSpec((1, A_KV_HEADS, VT_ROWS, tm), lambda bi, i: (bi, 0, 0, i))),
        (jax.ShapeDtypeStruct((b, t, B_W), BF16), pl.BlockSpec((1, tm, B_W), row)),
        (jax.ShapeDtypeStruct((b, B_HEADS, t, HEAD_DIM), BF16),
         pl.BlockSpec((1, B_HEADS, tm, HEAD_DIM), lambda bi, i: (bi, 0, i, 0))),
        (jax.ShapeDtypeStruct((b, B_HEADS, VT_ROWS, t), BF16),
         pl.BlockSpec((1, B_HEADS, VT_ROWS, tm), lambda bi, i: (bi, 0, 0, i))),
        (jax.ShapeDtypeStruct((b, t, 3 * C_W), F32), pl.BlockSpec((1, tm, 3 * C_W), row)),
        (jax.ShapeDtypeStruct((b, t, C_W), F32), pl.BlockSpec((1, tm, C_W), row)),
        (jax.ShapeDtypeStruct((b, t, LANE), F32), pl.BlockSpec((1, tm, LANE), row)),
    ]
    return pl.pallas_call(
        _inproj_kernel,
        grid=(b, t // tm),
        in_specs=[pl.BlockSpec((1, tm, d), row),
                  pl.BlockSpec((1, 1, N_MOD, d), lambda bi, i: (bi, (i >= nl).astype(jnp.int32), 0, 0)),
                  pl.BlockSpec(wa.shape, const), pl.BlockSpec(wb.shape, const),
                  pl.BlockSpec(wc.shape, const), pl.BlockSpec(wab.shape, const),
                  pl.BlockSpec(gq.shape, const), pl.BlockSpec(gk.shape, const),
                  pl.BlockSpec((tm, A_Q_W), lambda bi, i: (i, 0)),
                  pl.BlockSpec((tm, A_Q_W), lambda bi, i: (i, 0)),
                  pl.BlockSpec(gs.shape, const)],
        out_specs=[o[1] for o in outs],
        out_shape=[o[0] for o in outs],
        compiler_params=_cparams(("parallel", "parallel")),
        name="in_projection",
    )(xa, mods, wa, wb, wc, wab, gq, gk, cos, sin, gs)


def _gqa_kernel(q_ref, k_ref, v_ref, o_ref, *, tq, ck, n_lat, n_tot):
    i = pl.program_id(1)
    g = A_HEADS // A_KV_HEADS

    def attend(key_lo, key_hi, n_split):
        qs = [jnp.concatenate([q_ref[0, :, (kh * g + j) * HEAD_DIM:(kh * g + j + 1) * HEAD_DIM] for j in range(g)],
                              axis=0) for kh in range(A_KV_HEADS)]
        step = (key_hi - key_lo) // n_split
        units = [(kh, key_lo + r * step, key_lo + (r + 1) * step) for r in range(n_split) for kh in range(A_KV_HEADS)]
        ss, ms, ps, oes = {}, {}, {}, {}
        for n in range(len(units) + 2):
            if n < len(units):
                kh, lo, hi = units[n]
                ss[n] = _dot_nt(k_ref[0, kh, lo:hi, :], qs[kh])
            if 0 <= n - 1 < len(units):
                ms[n - 1] = jnp.max(ss[n - 1], axis=0, keepdims=True)
                ps[n - 1] = jnp.exp2((ss.pop(n - 1) - ms[n - 1]).astype(BF16))
            if 0 <= n - 2 < len(units):
                kh, lo, hi = units[n - 2]
                oes[n - 2] = _dot(v_ref[0, kh, :, lo:hi], ps.pop(n - 2))
        for kh in range(A_KV_HEADS):
            mine = [n for n, u in enumerate(units) if u[0] == kh]
            m = functools.reduce(jnp.maximum, [ms[n] for n in mine])
            oe = sum(oes[n] * jnp.exp2(ms[n] - m) for n in mine)
            o = (oe[:HEAD_DIM] / oe[HEAD_DIM:HEAD_DIM + 1]).T.astype(BF16)
            for j in range(g):
                o_ref[0, :, (kh * g + j) * HEAD_DIM:(kh * g + j + 1) * HEAD_DIM] = o[j * tq:(j + 1) * tq]

    @pl.when(i < n_lat // tq)
    def _():
        attend(0, n_tot, ck)

    @pl.when(i >= n_lat // tq)
    def _():
        attend(n_lat, n_tot, 1)


def _gqa(qa, ka, va, n_lat, tq, ck):
    b, t, _ = qa.shape
    return pl.pallas_call(
        functools.partial(_gqa_kernel, tq=tq, ck=ck, n_lat=n_lat, n_tot=t),
        grid=(b, t // tq),
        in_specs=[pl.BlockSpec((1, tq, A_Q_W), lambda bi, i: (bi, i, 0)),
                  pl.BlockSpec((1, A_KV_HEADS, t, HEAD_DIM), lambda bi, i: (bi, 0, 0, 0)),
                  pl.BlockSpec((1, A_KV_HEADS, VT_ROWS, t), lambda bi, i: (bi, 0, 0, 0))],
        out_specs=pl.BlockSpec((1, tq, A_Q_W), lambda bi, i: (bi, i, 0)),
        out_shape=jax.ShapeDtypeStruct((b, t, A_Q_W), BF16),
        compiler_params=_cparams(("parallel", "parallel")),
        name="gqa",
    )(qa, ka, va)


def _na_window_start(r0, rows):
    return np.clip(r0 - NA_ROWS // 2, 0, rows - NA_ROWS - 1) // 2 * 2


def _na_classes(n_lat, n_tot):
    rows = n_lat // GRID_W
    assert rows >= NA_WIN and rows % 2 == 0
    geoms, table = [], []
    for r0 in range(0, rows, 2):
        geom = (int(_na_window_start(r0, rows)) - r0,) + tuple(
            int(np.clip(r0 + qr - NA_ROWS // 2, 0, rows - NA_ROWS)) - r0 for qr in range(2))
        if geom not in geoms:
            geoms.append(geom)
        table.append(geoms.index(geom))
    table += [len(geoms)] * ((n_tot - n_lat) // (2 * GRID_W))
    return geoms, np.asarray(table, np.int32)


def _na_bias_slabs(rpb, geoms):
    n_head = rpb.shape[0]
    cj = np.arange(GRID_W)[:, None]
    c = np.arange(GRID_W)[None, :]
    col_start = np.clip(c - NA_COLS // 2, 0, GRID_W - NA_COLS)
    col_ok = (cj >= col_start) & (cj < col_start + NA_COLS)
    col_off = np.clip(cj - c + (NA_COLS - 1), 0, 2 * NA_COLS - 2)
    spread = np.zeros((2 * NA_COLS - 1, GRID_W * GRID_W), np.float32)
    spread[col_off.reshape(-1), np.arange(GRID_W * GRID_W)] = 1.0
    planes = jnp.dot(rpb * LOG2E, jnp.asarray(spread), precision=HIGHEST)
    planes = planes.reshape(n_head, 2 * NA_ROWS - 1, GRID_W, GRID_W)
    kr = np.arange(NA_WIN)[:, None]
    qr = np.arange(2)[None, :]
    slabs = []
    for a, b0, b1 in geoms:
        first = np.where(qr == 0, b0, b1)
        row_ok = (a + kr >= first) & (a + kr < first + NA_ROWS)
        row_off = np.clip(a + kr - qr + (NA_ROWS - 1), 0, 2 * NA_ROWS - 2)
        rows = jnp.stack([jnp.stack([planes[:, int(row_off[i, j])] for j in range(2)], axis=1)
                          for i in range(NA_WIN)], axis=1)
        ok = row_ok[:, :, None, None] & col_ok[None, None]
        slab = jnp.where(ok[None], rows, NEG_BIG).transpose(0, 1, 3, 2, 4)
        slabs.append(slab.reshape(n_head, NA_WIN * GRID_W, 2 * GRID_W))
    slabs.append(jnp.full_like(slabs[0], NEG_BIG))
    return jnp.stack(slabs, axis=0).astype(F32)


def _na_kernel(cls_ref, q_ref, k_ref, vt_ref, bias_ref, o_ref, *, n_lat, n_tot):
    del cls_ref
    rows = n_lat // GRID_W
    r0 = 2 * pl.program_id(1)
    start = jnp.clip(r0 - NA_ROWS // 2, 0, rows - NA_ROWS - 1) // 2 * 2
    nb = pl.ds(pl.multiple_of(start * GRID_W, 2 * GRID_W), NA_WIN * GRID_W)
    hs = [slice(h * HEAD_DIM, (h + 1) * HEAD_DIM) for h in range(B_HEADS)]
    qs = [q_ref[0, :, hs[h]] for h in range(B_HEADS)]
    s_nb = [_dot_nt(k_ref[0, h, nb, :], qs[h]) + bias_ref[0, h] for h in range(B_HEADS)]
    s_cx = [_dot_nt(k_ref[0, h, n_lat:n_tot, :], qs[h]) for h in range(B_HEADS)]
    ms = [jnp.maximum(jnp.max(a, axis=0, keepdims=True), jnp.max(c, axis=0, keepdims=True))
          for a, c in zip(s_nb, s_cx)]
    p_nb = [jnp.exp2((a - m).astype(BF16)) for a, m in zip(s_nb, ms)]
    p_cx = [jnp.exp2((c - m).astype(BF16)) for c, m in zip(s_cx, ms)]
    for h in range(B_HEADS):
        oe = _dot(vt_ref[0, h, :, nb], p_nb[h]) + _dot(vt_ref[0, h, :, n_lat:n_tot], p_cx[h])
        o_ref[0, :, hs[h]] = (oe[:HEAD_DIM] / oe[HEAD_DIM:HEAD_DIM + 1]).T.astype(BF16)


def _neighborhood(qb, kb, vbt, slabs, table, n_lat):
    b, t, _ = qb.shape
    tq = 2 * GRID_W
    grid_spec = pltpu.PrefetchScalarGridSpec(
        num_scalar_prefetch=1,
        grid=(b, t // tq),
        in_specs=[pl.BlockSpec((1, tq, B_W), lambda bi, pi, cls: (bi, pi, 0)),
                  pl.BlockSpec((1, B_HEADS, t, HEAD_DIM), lambda bi, pi, cls: (bi, 0, 0, 0)),
                  pl.BlockSpec((1, B_HEADS, VT_ROWS, t), lambda bi, pi, cls: (bi, 0, 0, 0)),
                  pl.BlockSpec((1,) + slabs.shape[1:], lambda bi, pi, cls: (cls[pi], 0, 0, 0))],
        out_specs=pl.BlockSpec((1, tq, B_W), lambda bi, pi, cls: (bi, pi, 0)),
    )
    return pl.pallas_call(
        functools.partial(_na_kernel, n_lat=n_lat, n_tot=t),
        grid_spec=grid_spec,
        out_shape=jax.ShapeDtypeStruct((b, t, B_W), BF16),
        compiler_params=_cparams(("parallel", "parallel")),
        name="neighborhood",
    )(jnp.asarray(table), qb, kb, vbt, slabs)


GC_LANE, GL_LANE, BETA_LANE = 0, 2 * C_HEADS, 4 * C_HEADS


def _gdn_prep_kernel(x_ref, prev_ref, next_ref, ab_ref, cw_ref, alog_ref, dtb_ref, gs_ref,
                     q_ref, k_ref, v_ref, comp_ref, *, tm, n_lat, n_tot):
    i = pl.program_id(1)
    nl = n_lat // tm
    nt = n_tot // tm
    first = jnp.logical_or(i == 0, i == nl)
    last = jnp.logical_or(i == nl - 1, i == nt - 1)
    x = x_ref[0]
    pv = jnp.where(first, 0.0, prev_ref[0])
    nx = jnp.where(last, 0.0, next_ref[0])
    row = lax.broadcasted_iota(jnp.int32, (tm, 1), 0)
    xm1 = jnp.where(row == 0, pv[7:8], pltpu.roll(x, 1, 0))
    xm2 = jnp.where(row == 0, pv[6:7], jnp.where(row == 1, pv[7:8], pltpu.roll(x, 2, 0)))
    xp1 = jnp.where(row == tm - 1, nx[0:1], pltpu.roll(x, tm - 1, 0))
    cw = cw_ref[...]
    y = _silu(xm2 * cw[0:1] + xm1 * cw[1:2] + x * cw[2:3] + xp1 * cw[3:4])
    q = y[:, :C_W]
    k = y[:, C_W:2 * C_W]
    gs = gs_ref[...]
    q_ref[0] = q * lax.rsqrt(_dot_exact_rhs(q * q, gs, 2) + LN_EPS) * (HEAD_DIM ** -0.5)
    k_ref[0] = k * lax.rsqrt(_dot_exact_rhs(k * k, gs, 2) + LN_EPS)
    v_ref[0] = y[:, 2 * C_W:]
    ab = ab_ref[0]
    z = ab + dtb_ref[...]
    softplus = jnp.maximum(z, 0.0) + jnp.log(1.0 + jnp.exp(-jnp.abs(z)))
    lane = lax.broadcasted_iota(jnp.int32, ab.shape, 1)
    gb = jnp.where(lane < 2 * C_HEADS, -jnp.exp(alog_ref[...]) * softplus, _sigmoid(ab))
    ii = lax.broadcasted_iota(jnp.int32, (tm, tm), 0)
    jj = lax.broadcasted_iota(jnp.int32, (tm, tm), 1)
    same = (ii // CHUNK) == (jj // CHUNK)
    gb3 = _split_bf16(gb, 3)

    def masked_sum(mask):
        mb = jnp.where(mask, 1.0, 0.0).astype(BF16)
        return sum(_dot(mb, piece) for piece in gb3)

    prefix = masked_sum(jnp.logical_and(same, jj <= ii))
    suffix = masked_sum(jnp.logical_and(same, jj >= ii))
    total = masked_sum(same)
    comp = jnp.where(lane < C_HEADS, prefix,
                     jnp.where(lane < GL_LANE, suffix,
                               jnp.where(lane < BETA_LANE, pltpu.roll(total, GL_LANE, 1),
                                         jnp.where(lane < BETA_LANE + 2 * C_HEADS, pltpu.roll(gb, 2 * C_HEADS, 1),
                                                   0.0))))
    comp_ref[0] = comp


def _gdn_prep(qkvc, ab, conv_w, a_log, dt_bias, gs, n_lat, tm):
    b, t, _ = qkvc.shape
    c3 = 3 * C_W
    alog_row = jnp.zeros((1, LANE), F32).at[0, :2 * C_HEADS].set(a_log.reshape(-1))
    dtb_row = jnp.zeros((1, LANE), F32).at[0, :2 * C_HEADS].set(dt_bias.reshape(-1))
    row = lambda bi, i: (bi, i, 0)
    const2 = lambda bi, i: (0, 0)
    hb = tm // 8
    last8 = t // 8 - 1
    tok = jax.ShapeDtypeStruct((b, t, C_W), F32)
    return pl.pallas_call(
        functools.partial(_gdn_prep_kernel, tm=tm, n_lat=n_lat, n_tot=t),
        grid=(b, t // tm),
        in_specs=[pl.BlockSpec((1, tm, c3), row),
                  pl.BlockSpec((1, 8, c3), lambda bi, i: (bi, jnp.maximum(i * hb - 1, 0), 0)),
                  pl.BlockSpec((1, 8, c3), lambda bi, i: (bi, jnp.minimum((i + 1) * hb, last8), 0)),
                  pl.BlockSpec((1, tm, LANE), row),
                  pl.BlockSpec((CONV_K, c3), const2),
                  pl.BlockSpec((1, LANE), const2), pl.BlockSpec((1, LANE), const2),
                  pl.BlockSpec(gs.shape, const2)],
        out_specs=[pl.BlockSpec((1, tm, C_W), row)] * 3 + [pl.BlockSpec((1, tm, LANE), row)],
        out_shape=[tok, tok, tok, jax.ShapeDtypeStruct((b, t, LANE), F32)],
        compiler_params=_cparams(("parallel", "parallel")),
        name="gdn_prep",
    )(qkvc, qkvc, qkvc, ab, conv_w, alog_row, dtb_row, gs)


def _unit_tri_inverses(lms, eye, bd16, bd32):
    def mm(a, b_):
        return _dot(a.astype(BF16), b_.astype(BF16))

    lds = [jnp.where(bd16, lm, 0.0) for lm in lms]
    xs = [eye - ld for ld in lds]
    lds = [ld.astype(BF16) for ld in lds]
    pws = [_dot(ld, ld).astype(BF16) for ld in lds]
    for step in range(3):
        xs = [x + mm(x, pw) for x, pw in zip(xs, pws)]
        if step < 2:
            pws = [_dot(pw, pw).astype(BF16) for pw in pws]
    for level in (lambda lm: jnp.where(jnp.logical_and(bd32, jnp.logical_not(bd16)), lm, 0.0),
                  lambda lm: jnp.where(bd32, 0.0, lm)):
        xbs = [x.astype(BF16) for x in xs]
        ys = [_dot(level(lm).astype(BF16), xb) for lm, xb in zip(lms, xbs)]
        xs = [x - _dot(xb, y.astype(BF16)) for x, xb, y in zip(xs, xbs, ys)]
    return xs


def _gdn_local_kernel(q_ref, k_ref, v_ref, comp_ref, ex_ref, sel_ref, u_ref, w_ref, kd_ref, qd_ref, in_ref):
    n = q_ref.shape[1]
    hp = 2 * HEAD_DIM
    n_pr = q_ref.shape[2] // hp
    ii = lax.broadcasted_iota(jnp.int32, (n, n), 0)
    jj = lax.broadcasted_iota(jnp.int32, (n, n), 1)
    same = (ii // CHUNK) == (jj // CHUNK)
    eye = (ii == jj).astype(F32)
    bd16 = (ii // 16) == (jj // 16)
    bd32 = (ii // 32) == (jj // 32)
    lane = lax.broadcasted_iota(jnp.int32, (1, hp), 1)
    lane2 = lax.broadcasted_iota(jnp.int32, (1, 2 * hp), 1)
    incls = [jnp.logical_and(same, ii >= jj), jnp.logical_and(same, ii <= jj)]
    stricts = [jnp.logical_and(same, ii > jj), jnp.logical_and(same, ii < jj)]
    lms, intras, rhss = [], [], []
    for pr in range(n_pr):
        ls = slice(pr * hp, (pr + 1) * hp)
        q = q_ref[0, :, ls]
        k = k_ref[0, :, ls]
        v = v_ref[0, :, ls]
        e = _dot_exact_rhs(comp_ref[0], ex_ref[pr])
        kh = [jnp.where((lane // HEAD_DIM) == hh, k, 0.0).astype(BF16) for hh in range(2)]
        for p in range(2):
            gc = e[:, p * hp:(p + 1) * hp]
            gl = e[:, (2 + p) * hp:(3 + p) * hp]
            beta = e[:, (4 + p) * hp:(5 + p) * hp]
            eg = jnp.exp(gc)
            kb = k * beta
            kd_ref[p, 0, :, ls] = (k * jnp.exp(gl - gc)).astype(BF16)
            qd_ref[p, 0, :, ls] = (q * eg).astype(BF16)
            rhs = jnp.concatenate([v * beta, kb * eg], axis=1)
            kbq = jnp.concatenate([kb, q], axis=0).astype(BF16)
            for hh in range(2):
                gcb = jnp.broadcast_to(gc[:, hh * HEAD_DIM:hh * HEAD_DIM + 1], (n, n))
                decay = jnp.exp(jnp.where(incls[p], gcb - gcb.T, NEG_BIG))
                aq = _dot_nt(kbq, kh[hh])
                lms.append(jnp.where(stricts[p], aq[:n] * decay, 0.0))
                intras.append(jnp.where(incls[p], aq[n:] * decay, 0.0).astype(BF16))
                rhss.append(jnp.where(((lane2 % hp) // HEAD_DIM) == hh, rhs, 0.0).astype(BF16))
    tmats = _unit_tri_inverses(lms, eye, bd16, bd32)
    for pr in range(n_pr):
        ls = slice(pr * hp, (pr + 1) * hp)
        for p in range(2):
            at = 4 * pr + 2 * p
            uw = sum(_dot(tmats[at + hh].astype(BF16), rhss[at + hh]) for hh in range(2))
            intra2 = sum(_dot(intras[at + hh], sel_ref[hh]) for hh in range(2))
            u_ref[p, 0, :, ls] = uw[:, :hp].astype(BF16)
            w_ref[p, 0, :, ls] = uw[:, hp:].astype(BF16)
            in_ref[p, 0, :, ls] = intra2.astype(BF16)


def _gdn_local(q, k, v, comp, blk):
    b, t, _ = q.shape
    hp = 2 * HEAD_DIM
    n_pair = C_W // hp
    ex = np.zeros((n_pair, LANE, 6 * hp), np.float32)
    for pr in range(n_pair):
        for p in range(2):
            for hh in range(2):
                src = p * C_HEADS + pr * 2 + hh
                for kind, base in enumerate((GC_LANE, GL_LANE, BETA_LANE)):
                    col = (2 * kind + p) * hp + hh * HEAD_DIM
                    ex[pr, base + src, col:col + HEAD_DIM] = 1.0
    sel = np.zeros((2, blk, hp), np.float32)
    for hh in range(2):
        sel[hh, np.arange(blk), hh * HEAD_DIM + np.arange(blk) % CHUNK] = 1.0
    gp = LOCAL_PAIRS
    tok = lambda bi, pg, i: (bi, i, pg)
    dirs = lambda bi, pg, i: (0, bi, i, pg)
    out = jax.ShapeDtypeStruct((2, b, t, C_W), BF16)
    return pl.pallas_call(
        _gdn_local_kernel,
        grid=(b, n_pair // gp, t // blk),
        in_specs=[pl.BlockSpec((1, blk, gp * hp), tok)] * 3
                 + [pl.BlockSpec((1, blk, LANE), lambda bi, pg, i: (bi, i, 0)),
                    pl.BlockSpec((gp, LANE, 6 * hp), lambda bi, pg, i: (pg, 0, 0)),
                    pl.BlockSpec((2, blk, hp), lambda bi, pg, i: (0, 0, 0))],
        out_specs=[pl.BlockSpec((2, 1, blk, gp * hp), dirs)] * 5,
        out_shape=[out] * 5,
        compiler_params=_cparams(("parallel", "parallel", "parallel")),
        name="gdn_local",
    )(q, k, v, comp, jnp.asarray(ex, dtype=BF16), jnp.asarray(sel, dtype=BF16))


def _gdn_scan_kernel(*refs, ncb, nb):
    ins = (refs[0:6], refs[6:12])
    ex_ref = refs[12]
    outs = refs[13:15]
    s_ref = refs[15]
    j = pl.program_id(1)

    @pl.when(j == 0)
    def _():
        s_ref[...] = jnp.zeros_like(s_ref)

    gw = 4 * HEAD_DIM
    n_grp = C_W // gw
    lane_head = lax.broadcasted_iota(jnp.int32, (1, gw), 1) // HEAD_DIM
    rr = lax.broadcasted_iota(jnp.int32, (gw, gw), 0) // HEAD_DIM
    cc_ = lax.broadcasted_iota(jnp.int32, (gw, gw), 1) // HEAD_DIM
    blockdiag = rr == cc_
    chains = [(bi, p, g) for bi in range(nb) for p in range(2) for g in range(n_grp)]
    for cc in range(ncb):
        rows = [slice(cc * CHUNK, (cc + 1) * CHUNK), slice((ncb - 1 - cc) * CHUNK, (ncb - cc) * CHUNK)]
        sdec = {(bi, p): jnp.exp(_dot_exact_rhs(ins[p][5][bi, rows[p].start:rows[p].start + 8, :], ex_ref[...]))
                for bi in range(nb) for p in range(2)}
        lanes = [slice(g * gw, (g + 1) * gw) for g in range(n_grp)]
        ss = [s_ref[bi, p, g] for bi, p, g in chains]
        sbs = [s.astype(BF16) for s in ss]
        v_news = [ins[p][0][0, bi, rows[p], lanes[g]].astype(F32) - _dot(ins[p][1][0, bi, rows[p], lanes[g]], sb)
                  for (bi, p, g), sb in zip(chains, sbs)]
        vstacks = [jnp.concatenate([jnp.where(lane_head == h, v_new, 0.0).astype(BF16)
                                    for h in range(gw // HEAD_DIM)], axis=0) for v_new in v_news]
        for (bi, p, g), s, sb, v_new, vstack in zip(chains, ss, sbs, v_news, vstacks):
            o = _dot(ins[p][3][0, bi, rows[p], lanes[g]], sb) + _dot(ins[p][4][0, bi, rows[p], lanes[g]], vstack)
            upd = _dot_tn(ins[p][2][0, bi, rows[p], lanes[g]], v_new.astype(BF16))
            decay = sdec[bi, p][0:1, p * C_W + g * gw:p * C_W + (g + 1) * gw]
            s_ref[bi, p, g] = s * decay + jnp.where(blockdiag, upd, 0.0)
            outs[p][bi, rows[p], lanes[g]] = o


def _gdn_scan(u, w, kd, qd, intra, comp, n_lat, blk):
    _, b, t, _ = u.shape
    nl = n_lat // blk
    nx = (t - n_lat) // blk
    ncb = blk // CHUNK
    ex = np.zeros((LANE, 2 * C_W), np.float32)
    for p in range(2):
        for h in range(C_HEADS):
            ex[GL_LANE + p * C_HEADS + h, p * C_W + h * HEAD_DIM:p * C_W + (h + 1) * HEAD_DIM] = 1.0

    def blk0(j):
        return jnp.where(j < nx, nl + j, j - nx)

    def blk1(j):
        return jnp.where(j < nx, nl + nx - 1 - j, nl - 1 - (j - nx))

    nb = SCAN_BATCH if b % SCAN_BATCH == 0 else 1
    specs = []
    for p, bf in enumerate((blk0, blk1)):
        specs += [pl.BlockSpec((1, nb, blk, C_W), lambda bi, j, p=p, bf=bf: (p, bi, bf(j), 0))] * 5
        specs += [pl.BlockSpec((nb, blk, LANE), lambda bi, j, bf=bf: (bi, bf(j), 0))]
    specs += [pl.BlockSpec((LANE, 2 * C_W), lambda bi, j: (0, 0))]
    out = jax.ShapeDtypeStruct((b, t, C_W), F32)
    gw = 4 * HEAD_DIM
    return pl.pallas_call(
        functools.partial(_gdn_scan_kernel, ncb=ncb, nb=nb),
        grid=(b // nb, nl + nx),
        in_specs=specs,
        out_specs=[pl.BlockSpec((nb, blk, C_W), lambda bi, j: (bi, blk0(j), 0)),
                   pl.BlockSpec((nb, blk, C_W), lambda bi, j: (bi, blk1(j), 0))],
        out_shape=[out, out],
        scratch_shapes=[pltpu.VMEM((nb, 2, C_W // gw, gw, gw), F32)],
        compiler_params=_cparams(("parallel", "arbitrary")),
        name="gdn_scan",
    )(u, w, kd, qd, intra, comp, u, w, kd, qd, intra, comp, jnp.asarray(ex, dtype=BF16))


def _merge_kernel(x_ref, mod_ref, oa_ref, ob_ref, oc0_ref, oc1_ref, gate_ref, wmg_ref, wbr_ref, wout_ref, og_ref,
                  gs_ref, lng_ref, lnb_ref, o_ref, *, alpha):
    x = x_ref[0]
    d = x.shape[-1]
    mod = mod_ref[0, 0]
    h = (_ln(x) * (1.0 + mod[1:2]) + mod[0:1]).astype(BF16)
    gates = _sigmoid(_dot(h, wmg_ref[...]))
    oc = oc0_ref[0] + oc1_ref[0]
    ocn = oc * lax.rsqrt(_dot_exact_rhs(oc * oc, gs_ref[...], 2) + LN_EPS) * og_ref[...] * _silu(gate_ref[0])
    m = gates[:, :d] * _dot(oa_ref[0], wbr_ref[0])
    m = m + gates[:, d:2 * d] * _dot(ob_ref[0], wbr_ref[1])
    m = m + gates[:, 2 * d:] * _dot(ocn.astype(BF16), wbr_ref[2])
    mix = _dot(m.astype(BF16), wout_ref[...])
    y = alpha * x + mod[2:3] * mix
    o_ref[0] = _ln(y) * lng_ref[...] + lnb_ref[...]


def _merge(xa, mods, oa, ob, oc0, oc1, gate_c, wmg, wbr, wout, og, gs, lng, lnb, n_lat, tm, alpha):
    b, t, d = xa.shape
    nl = n_lat // tm
    row = lambda bi, i: (bi, i, 0)
    const2 = lambda bi, i: (0, 0)
    return pl.pallas_call(
        functools.partial(_merge_kernel, alpha=alpha),
        grid=(b, t // tm),
        in_specs=[pl.BlockSpec((1, tm, d), row),
                  pl.BlockSpec((1, 1, N_MOD, d), lambda bi, i: (bi, (i >= nl).astype(jnp.int32), 0, 0)),
                  pl.BlockSpec((1, tm, BRANCH_W), row), pl.BlockSpec((1, tm, BRANCH_W), row),
                  pl.BlockSpec((1, tm, BRANCH_W), row), pl.BlockSpec((1, tm, BRANCH_W), row),
                  pl.BlockSpec((1, tm, BRANCH_W), row),
                  pl.BlockSpec(wmg.shape, const2),
                  pl.BlockSpec(wbr.shape, lambda bi, i: (0, 0, 0)),
                  pl.BlockSpec(wout.shape, const2),
                  pl.BlockSpec(og.shape, const2), pl.BlockSpec(gs.shape, const2),
                  pl.BlockSpec(lng.shape, const2), pl.BlockSpec(lnb.shape, const2)],
        out_specs=pl.BlockSpec((1, tm, d), row),
        out_shape=jax.ShapeDtypeStruct((b, t, d), F32),
        compiler_params=_cparams(("parallel", "parallel")),
        name="merge",
    )(xa, mods, oa, ob, oc0, oc1, gate_c, wmg, wbr, wout, og, gs, lng, lnb)


def _route_kernel(x_ref, mod_ref, wr_ref, h_ref, aff_ref, *, n_exp):
    x = x_ref[0]
    mod = mod_ref[0, 0]
    h = _ln(x) * (1.0 + mod[4:5]) + mod[3:4]
    h_ref[0] = h.astype(BF16)
    logits = _dot_hi(h, wr_ref[...])
    lane = lax.broadcasted_iota(jnp.int32, logits.shape, 1)
    logits = jnp.where(lane < n_exp, logits, NEG_BIG)
    e = jnp.exp(logits - jnp.max(logits, axis=-1, keepdims=True))
    aff_ref[0] = e / jnp.sum(e, axis=-1, keepdims=True)


def _route(xa, mods, wr, n_lat, tm, n_exp):
    b, t, d = xa.shape
    nl = n_lat // tm
    row = lambda bi, i: (bi, i, 0)
    return pl.pallas_call(
        functools.partial(_route_kernel, n_exp=n_exp),
        grid=(b, t // tm),
        in_specs=[pl.BlockSpec((1, tm, d), row),
                  pl.BlockSpec((1, 1, N_MOD, d), lambda bi, i: (bi, (i >= nl).astype(jnp.int32), 0, 0)),
                  pl.BlockSpec(wr.shape, lambda bi, i: (0, 0))],
        out_specs=[pl.BlockSpec((1, tm, d), row), pl.BlockSpec((1, tm, LANE), row)],
        out_shape=[jax.ShapeDtypeStruct((b, t, d), BF16), jax.ShapeDtypeStruct((b, t, LANE), F32)],
        compiler_params=_cparams(("parallel", "parallel")),
        name="route",
    )(xa, mods, wr)


def _ffn_kernel(x_ref, wg_ref, wu_ref, wd_ref, gate_ref, o_ref, acc_ref):
    f = pl.program_id(2)

    @pl.when(f == 0)
    def _():
        acc_ref[...] = jnp.zeros_like(acc_ref)

    x = x_ref[0]
    g = _dot(x, wg_ref[0, 0].astype(BF16))
    u = _dot(x, wu_ref[0, 0].astype(BF16))
    a = (_silu(g) * u).astype(BF16)
    acc_ref[...] += _dot(a, wd_ref[0, 0].astype(BF16))

    @pl.when(f == pl.num_programs(2) - 1)
    def _():
        o_ref[0] = (acc_ref[...] * gate_ref[0]).astype(BF16)


def _expert_ffn(xs, gate, w_gate_up, w_down, layer, tm, tf):
    e, r, d = xs.shape
    f = w_down.shape[2]
    nf = f // tf
    return pl.pallas_call(
        _ffn_kernel,
        grid=(e, r // tm, nf),
        in_specs=[pl.BlockSpec((1, tm, d), lambda ei, i, fi: (ei, i, 0)),
                  pl.BlockSpec((1, 1, d, tf), lambda ei, i, fi: (layer, ei, 0, fi)),
                  pl.BlockSpec((1, 1, d, tf), lambda ei, i, fi: (layer, ei, 0, nf + fi)),
                  pl.BlockSpec((1, 1, tf, d), lambda ei, i, fi: (layer, ei, fi, 0)),
                  pl.BlockSpec((1, tm, 1), lambda ei, i, fi: (ei, i, 0))],
        out_specs=pl.BlockSpec((1, tm, d), lambda ei, i, fi: (ei, i, 0)),
        out_shape=jax.ShapeDtypeStruct((e, r, d), BF16),
        scratch_shapes=[pltpu.VMEM((tm, d), F32)],
        compiler_params=_cparams(("parallel", "parallel", "arbitrary")),
        name="expert_ffn",
    )(xs, w_gate_up, w_gate_up, w_down, gate)


VALID, FIRST, LAST = 1, 2, 4


def _combine_plan(tok_of_pair, n_tok, tm, rwin):
    bsz, n_pair = tok_of_pair.shape
    nt = n_tok // tm
    n_win = n_pair // rwin
    n_item = nt + n_win
    order = jnp.argsort(tok_of_pair, axis=1)
    tok = jnp.take_along_axis(tok_of_pair, order, axis=1)
    bounds = jnp.arange(nt + 1, dtype=jnp.int32) * tm
    off = jnp.sum(tok[:, None, :] < bounds[None, :, None], axis=-1, dtype=jnp.int32)
    first = jnp.minimum(off[:, :-1] // rwin, n_win - 1)
    last = jnp.minimum(jnp.maximum(off[:, 1:] - 1, off[:, :-1]) // rwin, n_win - 1)
    n_items = last - first + 1
    cum = jnp.cumsum(n_items, axis=1)
    k = jnp.arange(n_item, dtype=jnp.int32)
    tile = jnp.sum(cum[:, None, :] <= k[None, :, None], axis=-1, dtype=jnp.int32)
    valid = tile < nt
    tile = jnp.minimum(tile, nt - 1)
    end = jnp.take_along_axis(cum, tile, axis=1)
    start = end - jnp.take_along_axis(n_items, tile, axis=1)
    win = jnp.where(valid, jnp.take_along_axis(first, tile, axis=1) + (k[None] - start),
                    jnp.take_along_axis(last, tile, axis=1))
    flags = (valid * VALID + (valid & (k[None] == start)) * FIRST + (valid & (k[None] == end - 1)) * LAST)
    return order, tok, tile, win.astype(jnp.int32), flags.astype(jnp.int32)


def _combine_kernel(tile_ref, win_ref, flag_ref, y_ref, tok_ref, x_ref, mod_ref, lng_ref, lnb_ref, o_ref, acc_ref, *,
                    alpha, tm):
    b = pl.program_id(0)
    k = pl.program_id(1)
    flags = flag_ref[b, k]

    @pl.when((flags & FIRST) != 0)
    def _():
        acc_ref[...] = jnp.zeros_like(acc_ref)

    @pl.when((flags & VALID) != 0)
    def _():
        tok0 = tile_ref[b, k] * tm
        rows = lax.broadcasted_iota(jnp.int32, (tm, tok_ref.shape[-1]), 0) + tok0
        pick = jnp.where(rows == tok_ref[0, 0], 1.0, 0.0).astype(BF16)
        acc_ref[...] += _dot(pick, y_ref[0])

    @pl.when((flags & LAST) != 0)
    def _():
        mod = mod_ref[0, 0]
        y = alpha * x_ref[0] + mod[5:6] * acc_ref[...]
        o_ref[0] = _ln(y) * lng_ref[...] + lnb_ref[...]


def _combine(y_sorted, tok_sorted, tile, win, flags, xa, mods, lng, lnb, n_lat, tm, rwin, alpha):
    b, t, d = xa.shape
    nl = n_lat // tm
    n_item = tile.shape[1]
    grid_spec = pltpu.PrefetchScalarGridSpec(
        num_scalar_prefetch=3,
        grid=(b, n_item),
        in_specs=[pl.BlockSpec((1, rwin, d), lambda bi, k, tl, wn, fl: (bi, wn[bi, k], 0)),
                  pl.BlockSpec((1, 1, 1, rwin), lambda bi, k, tl, wn, fl: (bi, wn[bi, k], 0, 0)),
                  pl.BlockSpec((1, tm, d), lambda bi, k, tl, wn, fl: (bi, tl[bi, k], 0)),
                  pl.BlockSpec((1, 1, N_MOD, d),
                               lambda bi, k, tl, wn, fl: (bi, (tl[bi, k] >= nl).astype(jnp.int32), 0, 0)),
                  pl.BlockSpec(lng.shape, lambda bi, k, tl, wn, fl: (0, 0)),
                  pl.BlockSpec(lnb.shape, lambda bi, k, tl, wn, fl: (0, 0))],
        out_specs=pl.BlockSpec((1, tm, d), lambda bi, k, tl, wn, fl: (bi, tl[bi, k], 0)),
        scratch_shapes=[pltpu.VMEM((tm, d), F32)],
    )
    return pl.pallas_call(
        functools.partial(_combine_kernel, alpha=alpha, tm=tm),
        grid_spec=grid_spec,
        out_shape=jax.ShapeDtypeStruct((b, t, d), F32),
        compiler_params=_cparams(("parallel", "arbitrary")),
        name="moe_combine",
    )(tile, win, flags, y_sorted, tok_sorted.reshape(b, -1, 1, rwin), xa, mods, lng, lnb)


def _moe(xa, mods, w_router, w_gate_up, w_down, lng, lnb, layer, n_lat, tm, alpha):
    b, t, d = xa.shape
    n_exp = w_router.shape[-1]
    m_ctx = t - n_lat
    wr = jnp.zeros((d, LANE), F32).at[:, :n_exp].set(w_router)
    h, aff = _route(xa, mods, wr, n_lat, tm, n_exp)
    aff = jnp.swapaxes(aff[:, :, :n_exp], 1, 2)
    cap_l = CAPACITY * n_lat // n_exp
    cap_x = CAPACITY * m_ctx // n_exp
    gate_l, idx_l = lax.top_k(aff[:, :, :n_lat], cap_l)
    gate_x, idx_x = lax.top_k(aff[:, :, n_lat:], cap_x)
    idx = jnp.concatenate([idx_l, idx_x + n_lat], axis=-1)
    gate = jnp.concatenate([gate_l, gate_x], axis=-1)
    cap = cap_l + cap_x
    idx_e = jnp.swapaxes(idx, 0, 1)
    bidx = jnp.arange(b)[None, :, None]
    xs = h[bidx, idx_e].reshape(n_exp, b * cap, d)
    gate_e = jnp.swapaxes(gate, 0, 1).reshape(n_exp, b * cap, 1)
    rows = b * cap
    tmr = rows
    for cand in (1088, 1024, 512, 256, 128, 64, 32, 16):
        if rows % cand == 0:
            tmr = cand
            break
    tf = min(512, w_down.shape[2])
    y = _expert_ffn(xs, gate_e, w_gate_up, w_down, layer, tmr, tf)
    n_pair = n_exp * cap
    rwin = next(r for r in (512, 256, 128, 64, 32, 16, 8) if n_pair % r == 0)
    order, tok, tile, win, flags = _combine_plan(idx.reshape(b, n_pair), t, tm, rwin)
    y_sorted = y[order // cap, jnp.arange(b)[:, None] * cap + order % cap]
    return _combine(y_sorted, tok, tile, win, flags, xa, mods, lng, lnb, n_lat, tm, rwin, alpha)


def _rope_tables(n_lat, m_ctx):
    t = np.arange(n_lat)
    rows = (t // GRID_W).astype(np.float32)
    cols = (t % GRID_W).astype(np.float32)
    n_freq = HEAD_DIM // 4
    inv_freq = jnp.asarray(ROPE_THETA, F32) ** (-jnp.arange(n_freq, dtype=F32) / n_freq)
    ang_r = jnp.asarray(rows)[:, None] * inv_freq
    ang_c = jnp.asarray(cols)[:, None] * inv_freq
    cr, sr, cc, sc = jnp.cos(ang_r), jnp.sin(ang_r), jnp.cos(ang_c), jnp.sin(ang_c)
    cos = jnp.concatenate([cr, cr, cc, cc], axis=-1)
    sin = jnp.concatenate([-sr, sr, -sc, sc], axis=-1)
    cos = jnp.concatenate([cos, jnp.ones((m_ctx, HEAD_DIM), F32)], axis=0)
    sin = jnp.concatenate([sin, jnp.zeros((m_ctx, HEAD_DIM), F32)], axis=0)
    return jnp.tile(cos, (1, A_HEADS)), jnp.tile(sin, (1, A_HEADS))


def kernel(x, c, ctx, c_ctx, w_mod, b_mod, w_in, qk_gain, rpb, conv_w, a_log, dt_bias, o_gain, w_branch, w_out,
           ln1_g, ln1_b, w_router, w_gate_up, w_down, ln2_g, ln2_b):
    b, n_lat, d = x.shape
    m_ctx = ctx.shape[1]
    depth = w_mod.shape[0]
    alpha = (2 * depth) ** 0.25
    tm = math.gcd(256, math.gcd(n_lat, m_ctx))
    blk = tm

    xa = jnp.concatenate([x, ctx], axis=1)
    n_rows = -(-(b + 1) // 8) * 8
    cc = jnp.zeros((n_rows, d), F32).at[:b].set(c).at[b].set(c_ctx)
    mod_all = _modulation(cc, w_mod, b_mod).reshape(depth, n_rows, N_MOD, d)
    cos, sin = _rope_tables(n_lat, m_ctx)
    gs_mean = _group_sum_matrix(A_Q_W, 1.0 / HEAD_DIM)
    gs_sum = _group_sum_matrix(C_W, 1.0)
    na_geoms, na_table = _na_classes(n_lat, n_lat + m_ctx)

    offs = np.cumsum((0, A_Q_W, A_KV_W, A_KV_W, B_W, B_W, B_W, 3 * C_W, C_W, 2 * C_HEADS, 2 * C_HEADS, N_BRANCH * d))
    for layer in range(depth):
        ml = mod_all[layer]
        mods = jnp.stack([ml[:b], jnp.broadcast_to(ml[b][None], (b, N_MOD, d))], axis=1)
        wl = w_in[layer]
        wa = wl[:, offs[0]:offs[3]].astype(BF16)
        wb = wl[:, offs[3]:offs[6]].astype(BF16)
        wc = wl[:, offs[6]:offs[8]].astype(BF16)
        wab = jnp.zeros((d, LANE), F32).at[:, :4 * C_HEADS].set(wl[:, offs[8]:offs[10]]).astype(BF16)
        wmg = wl[:, offs[10]:offs[11]].astype(BF16)
        gq = jnp.tile(qk_gain[layer, 0], A_HEADS)[None]
        gk = jnp.tile(qk_gain[layer, 1], A_KV_HEADS)[None]
        qa, ka, va, qb, kb, vb, qkvc, gate_c, ab = _in_projection(
            xa, mods, wa, wb, wc, wab, gq, gk, cos, sin, gs_mean, n_lat, tm)
        oa = _gqa(qa, ka, va, n_lat, min(GQA_TQ, tm), 2 if (n_lat + m_ctx) % (2 * LANE) == 0 else 1)
        ob = _neighborhood(qb, kb, vb, _na_bias_slabs(rpb[layer], na_geoms), na_table, n_lat)
        qc, kc, vc, comp = _gdn_prep(qkvc, ab, conv_w[layer], a_log[layer], dt_bias[layer], gs_sum, n_lat, tm)
        u, w, kd, qd, intra = _gdn_local(qc, kc, vc, comp, blk)
        oc0, oc1 = _gdn_scan(u, w, kd, qd, intra, comp, n_lat, blk)
        og = jnp.tile(o_gain[layer], C_HEADS)[None]
        xa = _merge(xa, mods, oa, ob, oc0, oc1, gate_c, wmg, w_branch[layer].astype(BF16),
                    w_out[layer].astype(BF16), og, gs_mean, ln1_g[layer][None], ln1_b[layer][None], n_lat, tm, alpha)
        xa = _moe(xa, mods, w_router[layer], w_gate_up, w_down, ln2_g[layer][None], ln2_b[layer][None], layer,
                  n_lat, tm, alpha)
    return xa[:, :n_lat]
```

```python
import functools
import math

import numpy as np
import jax
import jax.numpy as jnp
from jax import lax
from jax.experimental import pallas as pl
from jax.experimental.pallas import tpu as pltpu

F32 = jnp.float32
BF16 = jnp.bfloat16
HIGHEST = lax.Precision.HIGHEST

GRID_W = 64
HEAD_DIM = 64
A_HEADS = 8
A_KV_HEADS = 2
B_HEADS = 8
C_HEADS = 8
NA_ROWS = 8
NA_COLS = 16
CONV_K = 4
CHUNK = 64
N_BRANCH = 3
BRANCH_W = 512
CAPACITY = 2
N_MOD = 6
LN_EPS = 1e-6
ROPE_THETA = 10000.0
NEG_BIG = -1e30
LOG2E = 1.4426950408889634

A_Q_W = A_HEADS * HEAD_DIM
A_KV_W = A_KV_HEADS * HEAD_DIM
B_W = B_HEADS * HEAD_DIM
C_W = C_HEADS * HEAD_DIM
LANE = 128
VMEM_LIMIT = 56 * 1024 * 1024
GQA_TQ = 128
GQA_KEY_CHUNK = 512
LOCAL_PAIRS = 2
SCAN_BATCH = 2
VT_ROWS = HEAD_DIM + 16
NA_WIN = NA_ROWS + 2


def _cparams(sem):
    return pltpu.CompilerParams(dimension_semantics=sem, vmem_limit_bytes=VMEM_LIMIT)


def _dot(a, b):
    return jnp.dot(a, b, preferred_element_type=F32)


def _dot_hi(a, b):
    return jnp.dot(a, b, precision=HIGHEST, preferred_element_type=F32)


def _dot_nt(a, b):
    return lax.dot_general(a, b, (((1,), (1,)), ((), ())), preferred_element_type=F32)


def _dot_tn(a, b):
    return lax.dot_general(a, b, (((0,), (0,)), ((), ())), preferred_element_type=F32)


def _split_bf16(x, pieces):
    out = []
    for _ in range(pieces - 1):
        h = x.astype(BF16)
        out.append(h)
        x = x - h.astype(F32)
    out.append(x.astype(BF16))
    return out


def _dot_exact_rhs(a, b, pieces=3):
    return sum(_dot(p, b) for p in _split_bf16(a, pieces))


def _dot_exact_lhs(a, b, pieces=3):
    return sum(_dot(a, p) for p in _split_bf16(b, pieces))


def _ln(x):
    mu = jnp.mean(x, axis=-1, keepdims=True)
    xc = x - mu
    var = jnp.mean(xc * xc, axis=-1, keepdims=True)
    return xc * lax.rsqrt(var + LN_EPS)


def _sigmoid(x):
    return 1.0 / (1.0 + jnp.exp(-x))


def _silu(x):
    return x * _sigmoid(x)


def _group_sum_matrix(width, scale):
    g = np.arange(width) // HEAD_DIM
    return jnp.asarray((g[:, None] == g[None, :]).astype(np.float32) * scale, dtype=BF16)


def _mod_kernel(c_ref, w_ref, b_ref, o_ref):
    c = c_ref[...]
    o_ref[0] = _dot_hi(_silu(c), w_ref[0]) + b_ref[0]


def _modulation(cc, w_mod, b_mod):
    depth, d, dm = w_mod.shape
    r = cc.shape[0]
    tn = min(dm, 1536)
    return pl.pallas_call(
        _mod_kernel,
        grid=(depth, dm // tn),
        in_specs=[pl.BlockSpec((r, d), lambda l, j: (0, 0)),
                  pl.BlockSpec((1, d, tn), lambda l, j: (l, 0, j)),
                  pl.BlockSpec((1, 1, tn), lambda l, j: (l, 0, j))],
        out_specs=pl.BlockSpec((1, r, tn), lambda l, j: (l, 0, j)),
        out_shape=jax.ShapeDtypeStruct((depth, r, dm), F32),
        compiler_params=_cparams(("parallel", "parallel")),
        name="modulation",
    )(cc, w_mod, b_mod.reshape(depth, 1, dm))


def _rope_swap(y):
    w = y.shape[-1]
    lane = lax.broadcasted_iota(jnp.int32, y.shape, 1)
    up = pltpu.roll(y, w - 16, 1)
    down = pltpu.roll(y, 16, 1)
    return jnp.where((lane % 32) < 16, up, down)


def _inproj_kernel(x_ref, mod_ref, wa_ref, wb_ref, wc_ref, wab_ref, gq_ref, gk_ref, cos_ref, sin_ref,
                   gs_ref, qa_ref, ka_ref, va_ref, qb_ref, kb_ref, vb_ref, qkvc_ref, gatec_ref, ab_ref):
    x = x_ref[0]
    mod = mod_ref[0, 0]
    h = (_ln(x) * (1.0 + mod[1:2]) + mod[0:1]).astype(BF16)
    scale = HEAD_DIM ** -0.5 * LOG2E

    za = _dot(h, wa_ref[...])
    q = za[:, :A_Q_W]
    k = za[:, A_Q_W:A_Q_W + A_KV_W]
    v = za[:, A_Q_W + A_KV_W:]
    cos = cos_ref[...]
    sin = sin_ref[...]
    gs = gs_ref[...]
    qn = q * lax.rsqrt(_dot_exact_rhs(q * q, gs, 2) + LN_EPS) * gq_ref[...]
    qn = qn * cos + _rope_swap(qn) * sin
    qa_ref[0] = (qn * scale).astype(BF16)
    kn = k * lax.rsqrt(_dot_exact_rhs(k * k, gs[:A_KV_W, :A_KV_W], 2) + LN_EPS) * gk_ref[...]
    kn = kn * cos[:, :A_KV_W] + _rope_swap(kn) * sin[:, :A_KV_W]
    ones = jnp.ones((VT_ROWS - HEAD_DIM, x.shape[0]), F32)
    vt = v.T
    for j in range(A_KV_HEADS):
        ka_ref[0, j] = kn[:, j * HEAD_DIM:(j + 1) * HEAD_DIM].astype(BF16)
        va_ref[0, j] = jnp.concatenate([vt[j * HEAD_DIM:(j + 1) * HEAD_DIM], ones], axis=0).astype(BF16)

    zb = _dot(h, wb_ref[...])
    qb_ref[0] = (zb[:, :B_W] * scale).astype(BF16)
    kbv = zb[:, B_W:2 * B_W]
    vt = zb[:, 2 * B_W:].T
    for j in range(B_HEADS):
        kb_ref[0, j] = kbv[:, j * HEAD_DIM:(j + 1) * HEAD_DIM].astype(BF16)
        vb_ref[0, j] = jnp.concatenate([vt[j * HEAD_DIM:(j + 1) * HEAD_DIM], ones], axis=0).astype(BF16)

    zc = _dot(h, wc_ref[...])
    qkvc_ref[0] = zc[:, :3 * C_W]
    gatec_ref[0] = zc[:, 3 * C_W:]
    ab_ref[0] = _dot(h, wab_ref[...])


def _in_projection(xa, mods, wa, wb, wc, wab, gq, gk, cos, sin, gs, n_lat, tm):
    b, t, d = xa.shape
    nl = n_lat // tm
    row = lambda bi, i: (bi, i, 0)
    const = lambda bi, i: (0, 0)
    outs = [
        (jax.ShapeDtypeStruct((b, t, A_Q_W), BF16), pl.BlockSpec((1, tm, A_Q_W), row)),
        (jax.ShapeDtypeStruct((b, A_KV_HEADS, t, HEAD_DIM), BF16),
         pl.BlockSpec((1, A_KV_HEADS, tm, HEAD_DIM), lambda bi, i: (bi, 0, i, 0))),
        (jax.ShapeDtypeStruct((b, A_KV_HEADS, VT_ROWS, t), BF16),
         pl.BlockSpec((1, A_KV_HEADS, VT_ROWS, tm), lambda bi, i: (bi, 0, 0, i))),
        (jax.ShapeDtypeStruct((b, t, B_W), BF16), pl.BlockSpec((1, tm, B_W), row)),
        (jax.ShapeDtypeStruct((b, B_HEADS, t, HEAD_DIM), BF16),
         pl.BlockSpec((1, B_HEADS, tm, HEAD_DIM), lambda bi, i: (bi, 0, i, 0))),
        (jax.ShapeDtypeStruct((b, B_HEADS, VT_ROWS, t), BF16),
         pl.BlockSpec((1, B_HEADS, VT_ROWS, tm), lambda bi, i: (bi, 0, 0, i))),
        (jax.ShapeDtypeStruct((b, t, 3 * C_W), F32), pl.BlockSpec((1, tm, 3 * C_W), row)),
        (jax.ShapeDtypeStruct((b, t, C_W), F32), pl.BlockSpec((1, tm, C_W), row)),
        (jax.ShapeDtypeStruct((b, t, LANE), F32), pl.BlockSpec((1, tm, LANE), row)),
    ]
    return pl.pallas_call(
        _inproj_kernel,
        grid=(b, t // tm),
        in_specs=[pl.BlockSpec((1, tm, d), row),
                  pl.BlockSpec((1, 1, N_MOD, d), lambda bi, i: (bi, (i >= nl).astype(jnp.int32), 0, 0)),
                  pl.BlockSpec(wa.shape, const), pl.BlockSpec(wb.shape, const),
                  pl.BlockSpec(wc.shape, const), pl.BlockSpec(wab.shape, const),
                  pl.BlockSpec(gq.shape, const), pl.BlockSpec(gk.shape, const),
                  pl.BlockSpec((tm, A_Q_W), lambda bi, i: (i, 0)),
                  pl.BlockSpec((tm, A_Q_W), lambda bi, i: (i, 0)),
                  pl.BlockSpec(gs.shape, const)],
        out_specs=[o[1] for o in outs],
        out_shape=[o[0] for o in outs],
        compiler_params=_cparams(("parallel", "parallel")),
        name="in_projection",
    )(xa, mods, wa, wb, wc, wab, gq, gk, cos, sin, gs)


def _gqa_kernel(q_ref, k_ref, v_ref, o_ref, s_ref, p_ref, acc_ref, *, tq, ck, n_lat, n_tot):
    i = pl.program_id(1)
    g = A_HEADS // A_KV_HEADS
    nq = g * tq

    def attend(key_lo, key_hi):
        qs = [jnp.concatenate([q_ref[0, :, (kh * g + j) * HEAD_DIM:(kh * g + j + 1) * HEAD_DIM] for j in range(g)],
                              axis=0) for kh in range(A_KV_HEADS)]
        chunks = [(lo, min(lo + ck, key_hi)) for lo in range(key_lo, key_hi, ck)]
        ms = {}
        for n in range(A_KV_HEADS + 2):
            if n < A_KV_HEADS:
                ms[n] = jnp.full((1, nq), NEG_BIG, F32)
            if 0 <= n - 2 < A_KV_HEADS:
                acc_ref[n - 2] = jnp.zeros((VT_ROWS, nq), F32)
            for lo, hi in chunks:
                if n < A_KV_HEADS:
                    s = _dot_nt(k_ref[0, n, lo:hi, :], qs[n])
                    s_ref[n, lo:hi, :] = s
                    ms[n] = jnp.maximum(ms[n], jnp.max(s, axis=0, keepdims=True))
                if 0 <= n - 1 < A_KV_HEADS:
                    p_ref[n - 1, lo:hi, :] = jnp.exp2((s_ref[n - 1, lo:hi, :] - ms[n - 1]).astype(BF16))
                if 0 <= n - 2 < A_KV_HEADS:
                    acc_ref[n - 2] += _dot(v_ref[0, n - 2, :, lo:hi], p_ref[n - 2, lo:hi, :])
        for kh in range(A_KV_HEADS):
            oe = acc_ref[kh]
            o = (oe[:HEAD_DIM] / oe[HEAD_DIM:HEAD_DIM + 1]).T.astype(BF16)
            for j in range(g):
                o_ref[0, :, (kh * g + j) * HEAD_DIM:(kh * g + j + 1) * HEAD_DIM] = o[j * tq:(j + 1) * tq]

    @pl.when(i < n_lat // tq)
    def _():
        attend(0, n_tot)

    @pl.when(i >= n_lat // tq)
    def _():
        attend(n_lat, n_tot)


def _gqa(qa, ka, va, n_lat, tq, ck):
    b, t, _ = qa.shape
    nq = (A_HEADS // A_KV_HEADS) * tq
    return pl.pallas_call(
        functools.partial(_gqa_kernel, tq=tq, ck=ck, n_lat=n_lat, n_tot=t),
        grid=(b, t // tq),
        in_specs=[pl.BlockSpec((1, tq, A_Q_W), lambda bi, i: (bi, i, 0)),
                  pl.BlockSpec((1, A_KV_HEADS, t, HEAD_DIM), lambda bi, i: (bi, 0, 0, 0)),
                  pl.BlockSpec((1, A_KV_HEADS, VT_ROWS, t), lambda bi, i: (bi, 0, 0, 0))],
        out_specs=pl.BlockSpec((1, tq, A_Q_W), lambda bi, i: (bi, i, 0)),
        out_shape=jax.ShapeDtypeStruct((b, t, A_Q_W), BF16),
        scratch_shapes=[pltpu.VMEM((A_KV_HEADS, t, nq), F32), pltpu.VMEM((A_KV_HEADS, t, nq), BF16),
                        pltpu.VMEM((A_KV_HEADS, VT_ROWS, nq), F32)],
        compiler_params=_cparams(("parallel", "parallel")),
        name="gqa",
    )(qa, ka, va)


def _na_window_start(r0, rows):
    return np.clip(r0 - NA_ROWS // 2, 0, rows - NA_ROWS - 1) // 2 * 2


def _na_classes(n_lat, n_tot):
    rows = n_lat // GRID_W
    assert rows >= NA_WIN and rows % 2 == 0
    geoms, table = [], []
    for r0 in range(0, rows, 2):
        geom = (int(_na_window_start(r0, rows)) - r0,) + tuple(
            int(np.clip(r0 + qr - NA_ROWS // 2, 0, rows - NA_ROWS)) - r0 for qr in range(2))
        if geom not in geoms:
            geoms.append(geom)
        table.append(geoms.index(geom))
    table += [len(geoms)] * ((n_tot - n_lat) // (2 * GRID_W))
    return geoms, np.asarray(table, np.int32)


def _na_bias_slabs(rpb, geoms):
    n_head = rpb.shape[0]
    cj = np.arange(GRID_W)[:, None]
    c = np.arange(GRID_W)[None, :]
    col_start = np.clip(c - NA_COLS // 2, 0, GRID_W - NA_COLS)
    col_ok = (cj >= col_start) & (cj < col_start + NA_COLS)
    col_off = np.clip(cj - c + (NA_COLS - 1), 0, 2 * NA_COLS - 2)
    spread = np.zeros((2 * NA_COLS - 1, GRID_W * GRID_W), np.float32)
    spread[col_off.reshape(-1), np.arange(GRID_W * GRID_W)] = 1.0
    planes = jnp.dot(rpb * LOG2E, jnp.asarray(spread), precision=HIGHEST)
    planes = planes.reshape(n_head, 2 * NA_ROWS - 1, GRID_W, GRID_W)
    kr = np.arange(NA_WIN)[:, None]
    qr = np.arange(2)[None, :]
    slabs = []
    for a, b0, b1 in geoms:
        first = np.where(qr == 0, b0, b1)
        row_ok = (a + kr >= first) & (a + kr < first + NA_ROWS)
        row_off = np.clip(a + kr - qr + (NA_ROWS - 1), 0, 2 * NA_ROWS - 2)
        rows = jnp.stack([jnp.stack([planes[:, int(row_off[i, j])] for j in range(2)], axis=1)
                          for i in range(NA_WIN)], axis=1)
        ok = row_ok[:, :, None, None] & col_ok[None, None]
        slab = jnp.where(ok[None], rows, NEG_BIG).transpose(0, 1, 3, 2, 4)
        slabs.append(slab.reshape(n_head, NA_WIN * GRID_W, 2 * GRID_W))
    slabs.append(jnp.full_like(slabs[0], NEG_BIG))
    return jnp.stack(slabs, axis=0).astype(F32)


def _na_kernel(cls_ref, q_ref, k_ref, vt_ref, bias_ref, o_ref, *, n_lat, n_tot):
    del cls_ref
    rows = n_lat // GRID_W
    r0 = 2 * pl.program_id(1)
    start = jnp.clip(r0 - NA_ROWS // 2, 0, rows - NA_ROWS - 1) // 2 * 2
    nb = pl.ds(pl.multiple_of(start * GRID_W, 2 * GRID_W), NA_WIN * GRID_W)
    hs = [slice(h * HEAD_DIM, (h + 1) * HEAD_DIM) for h in range(B_HEADS)]
    qs = [q_ref[0, :, hs[h]] for h in range(B_HEADS)]
    s_nb = [_dot_nt(k_ref[0, h, nb, :], qs[h]) + bias_ref[0, h] for h in range(B_HEADS)]
    s_cx = [_dot_nt(k_ref[0, h, n_lat:n_tot, :], qs[h]) for h in range(B_HEADS)]
    ms = [jnp.maximum(jnp.max(a, axis=0, keepdims=True), jnp.max(c, axis=0, keepdims=True))
          for a, c in zip(s_nb, s_cx)]
    p_nb = [jnp.exp2((a - m).astype(BF16)) for a, m in zip(s_nb, ms)]
    p_cx = [jnp.exp2((c - m).astype(BF16)) for c, m in zip(s_cx, ms)]
    for h in range(B_HEADS):
        oe = _dot(vt_ref[0, h, :, nb], p_nb[h]) + _dot(vt_ref[0, h, :, n_lat:n_tot], p_cx[h])
        o_ref[0, :, hs[h]] = (oe[:HEAD_DIM] / oe[HEAD_DIM:HEAD_DIM + 1]).T.astype(BF16)


def _neighborhood(qb, kb, vbt, slabs, table, n_lat):
    b, t, _ = qb.shape
    tq = 2 * GRID_W
    grid_spec = pltpu.PrefetchScalarGridSpec(
        num_scalar_prefetch=1,
        grid=(b, t // tq),
        in_specs=[pl.BlockSpec((1, tq, B_W), lambda bi, pi, cls: (bi, pi, 0)),
                  pl.BlockSpec((1, B_HEADS, t, HEAD_DIM), lambda bi, pi, cls: (bi, 0, 0, 0)),
                  pl.BlockSpec((1, B_HEADS, VT_ROWS, t), lambda bi, pi, cls: (bi, 0, 0, 0)),
                  pl.BlockSpec((1,) + slabs.shape[1:], lambda bi, pi, cls: (cls[pi], 0, 0, 0))],
        out_specs=pl.BlockSpec((1, tq, B_W), lambda bi, pi, cls: (bi, pi, 0)),
    )
    return pl.pallas_call(
        functools.partial(_na_kernel, n_lat=n_lat, n_tot=t),
        grid_spec=grid_spec,
        out_shape=jax.ShapeDtypeStruct((b, t, B_W), BF16),
        compiler_params=_cparams(("parallel", "parallel")),
        name="neighborhood",
    )(jnp.asarray(table), qb, kb, vbt, slabs)


GC_LANE, GL_LANE, BETA_LANE = 0, 2 * C_HEADS, 4 * C_HEADS


def _gdn_prep_kernel(x_ref, prev_ref, next_ref, ab_ref, cw_ref, alog_ref, dtb_ref, gs_ref,
                     q_ref, k_ref, v_ref, comp_ref, *, tm, n_lat, n_tot):
    i = pl.program_id(1)
    nl = n_lat // tm
    nt = n_tot // tm
    first = jnp.logical_or(i == 0, i == nl)
    last = jnp.logical_or(i == nl - 1, i == nt - 1)
    x = x_ref[0]
    pv = jnp.where(first, 0.0, prev_ref[0])
    nx = jnp.where(last, 0.0, next_ref[0])
    row = lax.broadcasted_iota(jnp.int32, (tm, 1), 0)
    xm1 = jnp.where(row == 0, pv[7:8], pltpu.roll(x, 1, 0))
    xm2 = jnp.where(row == 0, pv[6:7], jnp.where(row == 1, pv[7:8], pltpu.roll(x, 2, 0)))
    xp1 = jnp.where(row == tm - 1, nx[0:1], pltpu.roll(x, tm - 1, 0))
    cw = cw_ref[...]
    y = _silu(xm2 * cw[0:1] + xm1 * cw[1:2] + x * cw[2:3] + xp1 * cw[3:4])
    q = y[:, :C_W]
    k = y[:, C_W:2 * C_W]
    gs = gs_ref[...]
    q_ref[0] = q * lax.rsqrt(_dot_exact_rhs(q * q, gs, 2) + LN_EPS) * (HEAD_DIM ** -0.5)
    k_ref[0] = k * lax.rsqrt(_dot_exact_rhs(k * k, gs, 2) + LN_EPS)
    v_ref[0] = y[:, 2 * C_W:]
    ab = ab_ref[0]
    z = ab + dtb_ref[...]
    softplus = jnp.maximum(z, 0.0) + jnp.log(1.0 + jnp.exp(-jnp.abs(z)))
    lane = lax.broadcasted_iota(jnp.int32, ab.shape, 1)
    gb = jnp.where(lane < 2 * C_HEADS, -jnp.exp(alog_ref[...]) * softplus, _sigmoid(ab))
    ii = lax.broadcasted_iota(jnp.int32, (tm, tm), 0)
    jj = lax.broadcasted_iota(jnp.int32, (tm, tm), 1)
    same = (ii // CHUNK) == (jj // CHUNK)
    gb3 = _split_bf16(gb, 3)

    def masked_sum(mask):
        mb = jnp.where(mask, 1.0, 0.0).astype(BF16)
        return sum(_dot(mb, piece) for piece in gb3)

    prefix = masked_sum(jnp.logical_and(same, jj <= ii))
    suffix = masked_sum(jnp.logical_and(same, jj >= ii))
    total = masked_sum(same)
    comp = jnp.where(lane < C_HEADS, prefix,
                     jnp.where(lane < GL_LANE, suffix,
                               jnp.where(lane < BETA_LANE, pltpu.roll(total, GL_LANE, 1),
                                         jnp.where(lane < BETA_LANE + 2 * C_HEADS, pltpu.roll(gb, 2 * C_HEADS, 1),
                                                   0.0))))
    comp_ref[0] = comp


def _gdn_prep(qkvc, ab, conv_w, a_log, dt_bias, gs, n_lat, tm):
    b, t, _ = qkvc.shape
    c3 = 3 * C_W
    alog_row = jnp.zeros((1, LANE), F32).at[0, :2 * C_HEADS].set(a_log.reshape(-1))
    dtb_row = jnp.zeros((1, LANE), F32).at[0, :2 * C_HEADS].set(dt_bias.reshape(-1))
    row = lambda bi, i: (bi, i, 0)
    const2 = lambda bi, i: (0, 0)
    hb = tm // 8
    last8 = t // 8 - 1
    tok = jax.ShapeDtypeStruct((b, t, C_W), F32)
    return pl.pallas_call(
        functools.partial(_gdn_prep_kernel, tm=tm, n_lat=n_lat, n_tot=t),
        grid=(b, t // tm),
        in_specs=[pl.BlockSpec((1, tm, c3), row),
                  pl.BlockSpec((1, 8, c3), lambda bi, i: (bi, jnp.maximum(i * hb - 1, 0), 0)),
                  pl.BlockSpec((1, 8, c3), lambda bi, i: (bi, jnp.minimum((i + 1) * hb, last8), 0)),
                  pl.BlockSpec((1, tm, LANE), row),
                  pl.BlockSpec((CONV_K, c3), const2),
                  pl.BlockSpec((1, LANE), const2), pl.BlockSpec((1, LANE), const2),
                  pl.BlockSpec(gs.shape, const2)],
        out_specs=[pl.BlockSpec((1, tm, C_W), row)] * 3 + [pl.BlockSpec((1, tm, LANE), row)],
        out_shape=[tok, tok, tok, jax.ShapeDtypeStruct((b, t, LANE), F32)],
        compiler_params=_cparams(("parallel", "parallel")),
        name="gdn_prep",
    )(qkvc, qkvc, qkvc, ab, conv_w, alog_row, dtb_row, gs)


def _unit_tri_inverses(lms, eye, bd16, bd32):
    def mm(a, b_):
        return _dot(a.astype(BF16), b_.astype(BF16))

    lds = [jnp.where(bd16, lm, 0.0) for lm in lms]
    xs = [eye - ld for ld in lds]
    lds = [ld.astype(BF16) for ld in lds]
    pws = [_dot(ld, ld).astype(BF16) for ld in lds]
    for step in range(3):
        xs = [x + mm(x, pw) for x, pw in zip(xs, pws)]
        if step < 2:
            pws = [_dot(pw, pw).astype(BF16) for pw in pws]
    for level in (lambda lm: jnp.where(jnp.logical_and(bd32, jnp.logical_not(bd16)), lm, 0.0),
                  lambda lm: jnp.where(bd32, 0.0, lm)):
        xbs = [x.astype(BF16) for x in xs]
        ys = [_dot(level(lm).astype(BF16), xb) for lm, xb in zip(lms, xbs)]
        xs = [x - _dot(xb, y.astype(BF16)) for x, xb, y in zip(xs, xbs, ys)]
    return xs


def _gdn_local_kernel(q_ref, k_ref, v_ref, comp_ref, ex_ref, sel_ref, u_ref, w_ref, kd_ref, qd_ref, in_ref):
    n = q_ref.shape[1]
    hp = 2 * HEAD_DIM
    n_pr = q_ref.shape[2] // hp
    ii = lax.broadcasted_iota(jnp.int32, (n, n), 0)
    jj = lax.broadcasted_iota(jnp.int32, (n, n), 1)
    same = (ii // CHUNK) == (jj // CHUNK)
    eye = (ii == jj).astype(F32)
    bd16 = (ii // 16) == (jj // 16)
    bd32 = (ii // 32) == (jj // 32)
    lane = lax.broadcasted_iota(jnp.int32, (1, hp), 1)
    lane2 = lax.broadcasted_iota(jnp.int32, (1, 2 * hp), 1)
    incls = [jnp.logical_and(same, ii >= jj), jnp.logical_and(same, ii <= jj)]
    stricts = [jnp.logical_and(same, ii > jj), jnp.logical_and(same, ii < jj)]
    lms, intras, rhss = [], [], []
    for pr in range(n_pr):
        ls = slice(pr * hp, (pr + 1) * hp)
        q = q_ref[0, :, ls]
        k = k_ref[0, :, ls]
        v = v_ref[0, :, ls]
        e = _dot_exact_rhs(comp_ref[0], ex_ref[pr])
        kh = [jnp.where((lane // HEAD_DIM) == hh, k, 0.0).astype(BF16) for hh in range(2)]
        for p in range(2):
            gc = e[:, p * hp:(p + 1) * hp]
            gl = e[:, (2 + p) * hp:(3 + p) * hp]
            beta = e[:, (4 + p) * hp:(5 + p) * hp]
            eg = jnp.exp(gc)
            kb = k * beta
            kd_ref[p, 0, :, ls] = (k * jnp.exp(gl - gc)).astype(BF16)
            qd_ref[p, 0, :, ls] = (q * eg).astype(BF16)
            rhs = jnp.concatenate([v * beta, kb * eg], axis=1)
            kbq = jnp.concatenate([kb, q], axis=0).astype(BF16)
            for hh in range(2):
                gcb = jnp.broadcast_to(gc[:, hh * HEAD_DIM:hh * HEAD_DIM + 1], (n, n))
                decay = jnp.exp(jnp.where(incls[p], gcb - gcb.T, NEG_BIG))
                aq = _dot_nt(kbq, kh[hh])
                lms.append(jnp.where(stricts[p], aq[:n] * decay, 0.0))
                intras.append(jnp.where(incls[p], aq[n:] * decay, 0.0).astype(BF16))
                rhss.append(jnp.where(((lane2 % hp) // HEAD_DIM) == hh, rhs, 0.0).astype(BF16))
    tmats = _unit_tri_inverses(lms, eye, bd16, bd32)
    for pr in range(n_pr):
        ls = slice(pr * hp, (pr + 1) * hp)
        for p in range(2):
            at = 4 * pr + 2 * p
            uw = sum(_dot(tmats[at + hh].astype(BF16), rhss[at + hh]) for hh in range(2))
            intra2 = sum(_dot(intras[at + hh], sel_ref[hh]) for hh in range(2))
            u_ref[p, 0, :, ls] = uw[:, :hp].astype(BF16)
            w_ref[p, 0, :, ls] = uw[:, hp:].astype(BF16)
            in_ref[p, 0, :, ls] = intra2.astype(BF16)


def _gdn_local(q, k, v, comp, blk):
    b, t, _ = q.shape
    hp = 2 * HEAD_DIM
    n_pair = C_W // hp
    ex = np.zeros((n_pair, LANE, 6 * hp), np.float32)
    for pr in range(n_pair):
        for p in range(2):
            for hh in range(2):
                src = p * C_HEADS + pr * 2 + hh
                for kind, base in enumerate((GC_LANE, GL_LANE, BETA_LANE)):
                    col = (2 * kind + p) * hp + hh * HEAD_DIM
                    ex[pr, base + src, col:col + HEAD_DIM] = 1.0
    sel = np.zeros((2, blk, hp), np.float32)
    for hh in range(2):
        sel[hh, np.arange(blk), hh * HEAD_DIM + np.arange(blk) % CHUNK] = 1.0
    gp = LOCAL_PAIRS
    tok = lambda bi, pg, i: (bi, i, pg)
    dirs = lambda bi, pg, i: (0, bi, i, pg)
    out = jax.ShapeDtypeStruct((2, b, t, C_W), BF16)
    return pl.pallas_call(
        _gdn_local_kernel,
        grid=(b, n_pair // gp, t // blk),
        in_specs=[pl.BlockSpec((1, blk, gp * hp), tok)] * 3
                 + [pl.BlockSpec((1, blk, LANE), lambda bi, pg, i: (bi, i, 0)),
                    pl.BlockSpec((gp, LANE, 6 * hp), lambda bi, pg, i: (pg, 0, 0)),
                    pl.BlockSpec((2, blk, hp), lambda bi, pg, i: (0, 0, 0))],
        out_specs=[pl.BlockSpec((2, 1, blk, gp * hp), dirs)] * 5,
        out_shape=[out] * 5,
        compiler_params=_cparams(("parallel", "parallel", "parallel")),
        name="gdn_local",
    )(q, k, v, comp, jnp.asarray(ex, dtype=BF16), jnp.asarray(sel, dtype=BF16))


def _gdn_scan_kernel(*refs, ncb, nb):
    ins = (refs[0:6], refs[6:12])
    ex_ref = refs[12]
    outs = refs[13:15]
    s_ref = refs[15]
    j = pl.program_id(1)

    @pl.when(j == 0)
    def _():
        s_ref[...] = jnp.zeros_like(s_ref)

    gw = 4 * HEAD_DIM
    n_grp = C_W // gw
    lane_head = lax.broadcasted_iota(jnp.int32, (1, gw), 1) // HEAD_DIM
    rr = lax.broadcasted_iota(jnp.int32, (gw, gw), 0) // HEAD_DIM
    cc_ = lax.broadcasted_iota(jnp.int32, (gw, gw), 1) // HEAD_DIM
    blockdiag = rr == cc_
    chains = [(bi, p, g) for bi in range(nb) for p in range(2) for g in range(n_grp)]
    for cc in range(ncb):
        rows = [slice(cc * CHUNK, (cc + 1) * CHUNK), slice((ncb - 1 - cc) * CHUNK, (ncb - cc) * CHUNK)]
        sdec = {(bi, p): jnp.exp(_dot_exact_rhs(ins[p][5][bi, rows[p].start:rows[p].start + 8, :], ex_ref[...]))
                for bi in range(nb) for p in range(2)}
        lanes = [slice(g * gw, (g + 1) * gw) for g in range(n_grp)]
        ss = [s_ref[bi, p, g] for bi, p, g in chains]
        sbs = [s.astype(BF16) for s in ss]
        v_news = [ins[p][0][0, bi, rows[p], lanes[g]].astype(F32) - _dot(ins[p][1][0, bi, rows[p], lanes[g]], sb)
                  for (bi, p, g), sb in zip(chains, sbs)]
        vstacks = [jnp.concatenate([jnp.where(lane_head == h, v_new, 0.0).astype(BF16)
                                    for h in range(gw // HEAD_DIM)], axis=0) for v_new in v_news]
        for (bi, p, g), s, sb, v_new, vstack in zip(chains, ss, sbs, v_news, vstacks):
            o = _dot(ins[p][3][0, bi, rows[p], lanes[g]], sb) + _dot(ins[p][4][0, bi, rows[p], lanes[g]], vstack)
            upd = _dot_tn(ins[p][2][0, bi, rows[p], lanes[g]], v_new.astype(BF16))
            decay = sdec[bi, p][0:1, p * C_W + g * gw:p * C_W + (g + 1) * gw]
            s_ref[bi, p, g] = s * decay + jnp.where(blockdiag, upd, 0.0)
            outs[p][bi, rows[p], lanes[g]] = o


def _gdn_scan(u, w, kd, qd, intra, comp, n_lat, blk):
    _, b, t, _ = u.shape
    nl = n_lat // blk
    nx = (t - n_lat) // blk
    ncb = blk // CHUNK
    ex = np.zeros((LANE, 2 * C_W), np.float32)
    for p in range(2):
        for h in range(C_HEADS):
            ex[GL_LANE + p * C_HEADS + h, p * C_W + h * HEAD_DIM:p * C_W + (h + 1) * HEAD_DIM] = 1.0

    def blk0(j):
        return jnp.where(j < nx, nl + j, j - nx)

    def blk1(j):
        return jnp.where(j < nx, nl + nx - 1 - j, nl - 1 - (j - nx))

    nb = SCAN_BATCH if b % SCAN_BATCH == 0 else 1
    specs = []
    for p, bf in enumerate((blk0, blk1)):
        specs += [pl.BlockSpec((1, nb, blk, C_W), lambda bi, j, p=p, bf=bf: (p, bi, bf(j), 0))] * 5
        specs += [pl.BlockSpec((nb, blk, LANE), lambda bi, j, bf=bf: (bi, bf(j), 0))]
    specs += [pl.BlockSpec((LANE, 2 * C_W), lambda bi, j: (0, 0))]
    out = jax.ShapeDtypeStruct((b, t, C_W), F32)
    gw = 4 * HEAD_DIM
    return pl.pallas_call(
        functools.partial(_gdn_scan_kernel, ncb=ncb, nb=nb),
        grid=(b // nb, nl + nx),
        in_specs=specs,
        out_specs=[pl.BlockSpec((nb, blk, C_W), lambda bi, j: (bi, blk0(j), 0)),
                   pl.BlockSpec((nb, blk, C_W), lambda bi, j: (bi, blk1(j), 0))],
        out_shape=[out, out],
        scratch_shapes=[pltpu.VMEM((nb, 2, C_W // gw, gw, gw), F32)],
        compiler_params=_cparams(("parallel", "arbitrary")),
        name="gdn_scan",
    )(u, w, kd, qd, intra, comp, u, w, kd, qd, intra, comp, jnp.asarray(ex, dtype=BF16))


def _merge_kernel(x_ref, mod_ref, oa_ref, ob_ref, oc0_ref, oc1_ref, gate_ref, wmg_ref, wbr_ref, wout_ref, og_ref,
                  gs_ref, lng_ref, lnb_ref, o_ref, *, alpha):
    x = x_ref[0]
    d = x.shape[-1]
    mod = mod_ref[0, 0]
    h = (_ln(x) * (1.0 + mod[1:2]) + mod[0:1]).astype(BF16)
    gates = _sigmoid(_dot(h, wmg_ref[...]))
    oc = oc0_ref[0] + oc1_ref[0]
    ocn = oc * lax.rsqrt(_dot_exact_rhs(oc * oc, gs_ref[...], 2) + LN_EPS) * og_ref[...] * _silu(gate_ref[0])
    m = gates[:, :d] * _dot(oa_ref[0], wbr_ref[0])
    m = m + gates[:, d:2 * d] * _dot(ob_ref[0], wbr_ref[1])
    m = m + gates[:, 2 * d:] * _dot(ocn.astype(BF16), wbr_ref[2])
    mix = _dot(m.astype(BF16), wout_ref[...])
    y = alpha * x + mod[2:3] * mix
    o_ref[0] = _ln(y) * lng_ref[...] + lnb_ref[...]


def _merge(xa, mods, oa, ob, oc0, oc1, gate_c, wmg, wbr, wout, og, gs, lng, lnb, n_lat, tm, alpha):
    b, t, d = xa.shape
    nl = n_lat // tm
    row = lambda bi, i: (bi, i, 0)
    const2 = lambda bi, i: (0, 0)
    return pl.pallas_call(
        functools.partial(_merge_kernel, alpha=alpha),
        grid=(b, t // tm),
        in_specs=[pl.BlockSpec((1, tm, d), row),
                  pl.BlockSpec((1, 1, N_MOD, d), lambda bi, i: (bi, (i >= nl).astype(jnp.int32), 0, 0)),
                  pl.BlockSpec((1, tm, BRANCH_W), row), pl.BlockSpec((1, tm, BRANCH_W), row),
                  pl.BlockSpec((1, tm, BRANCH_W), row), pl.BlockSpec((1, tm, BRANCH_W), row),
                  pl.BlockSpec((1, tm, BRANCH_W), row),
                  pl.BlockSpec(wmg.shape, const2),
                  pl.BlockSpec(wbr.shape, lambda bi, i: (0, 0, 0)),
                  pl.BlockSpec(wout.shape, const2),
                  pl.BlockSpec(og.shape, const2), pl.BlockSpec(gs.shape, const2),
                  pl.BlockSpec(lng.shape, const2), pl.BlockSpec(lnb.shape, const2)],
        out_specs=pl.BlockSpec((1, tm, d), row),
        out_shape=jax.ShapeDtypeStruct((b, t, d), F32),
        compiler_params=_cparams(("parallel", "parallel")),
        name="merge",
    )(xa, mods, oa, ob, oc0, oc1, gate_c, wmg, wbr, wout, og, gs, lng, lnb)


def _route_kernel(x_ref, mod_ref, wr_ref, h_ref, aff_ref, *, n_exp):
    x = x_ref[0]
    mod = mod_ref[0, 0]
    h = _ln(x) * (1.0 + mod[4:5]) + mod[3:4]
    h_ref[0] = h.astype(BF16)
    logits = _dot_hi(h, wr_ref[...])
    lane = lax.broadcasted_iota(jnp.int32, logits.shape, 1)
    logits = jnp.where(lane < n_exp, logits, NEG_BIG)
    e = jnp.exp(logits - jnp.max(logits, axis=-1, keepdims=True))
    aff_ref[0] = e / jnp.sum(e, axis=-1, keepdims=True)


def _route(xa, mods, wr, n_lat, tm, n_exp):
    b, t, d = xa.shape
    nl = n_lat // tm
    row = lambda bi, i: (bi, i, 0)
    return pl.pallas_call(
        functools.partial(_route_kernel, n_exp=n_exp),
        grid=(b, t // tm),
        in_specs=[pl.BlockSpec((1, tm, d), row),
                  pl.BlockSpec((1, 1, N_MOD, d), lambda bi, i: (bi, (i >= nl).astype(jnp.int32), 0, 0)),
                  pl.BlockSpec(wr.shape, lambda bi, i: (0, 0))],
        out_specs=[pl.BlockSpec((1, tm, d), row), pl.BlockSpec((1, tm, LANE), row)],
        out_shape=[jax.ShapeDtypeStruct((b, t, d), BF16), jax.ShapeDtypeStruct((b, t, LANE), F32)],
        compiler_params=_cparams(("parallel", "parallel")),
        name="route",
    )(xa, mods, wr)


def _ffn_kernel(x_ref, wg_ref, wu_ref, wd_ref, gate_ref, o_ref, acc_ref):
    f = pl.program_id(2)

    @pl.when(f == 0)
    def _():
        acc_ref[...] = jnp.zeros_like(acc_ref)

    x = x_ref[0]
    g = _dot(x, wg_ref[0, 0].astype(BF16))
    u = _dot(x, wu_ref[0, 0].astype(BF16))
    a = (_silu(g) * u).astype(BF16)
    acc_ref[...] += _dot(a, wd_ref[0, 0].astype(BF16))

    @pl.when(f == pl.num_programs(2) - 1)
    def _():
        o_ref[0] = (acc_ref[...] * gate_ref[0]).astype(BF16)


def _expert_ffn(xs, gate, w_gate_up, w_down, layer, tm, tf):
    e, r, d = xs.shape
    f = w_down.shape[2]
    nf = f // tf
    return pl.pallas_call(
        _ffn_kernel,
        grid=(e, r // tm, nf),
        in_specs=[pl.BlockSpec((1, tm, d), lambda ei, i, fi: (ei, i, 0)),
                  pl.BlockSpec((1, 1, d, tf), lambda ei, i, fi: (layer, ei, 0, fi)),
                  pl.BlockSpec((1, 1, d, tf), lambda ei, i, fi: (layer, ei, 0, nf + fi)),
                  pl.BlockSpec((1, 1, tf, d), lambda ei, i, fi: (layer, ei, fi, 0)),
                  pl.BlockSpec((1, tm, 1), lambda ei, i, fi: (ei, i, 0))],
        out_specs=pl.BlockSpec((1, tm, d), lambda ei, i, fi: (ei, i, 0)),
        out_shape=jax.ShapeDtypeStruct((e, r, d), BF16),
        scratch_shapes=[pltpu.VMEM((tm, d), F32)],
        compiler_params=_cparams(("parallel", "parallel", "arbitrary")),
        name="expert_ffn",
    )(xs, w_gate_up, w_gate_up, w_down, gate)


VALID, FIRST, LAST = 1, 2, 4


def _combine_plan(tok_of_pair, n_tok, tm, rwin):
    bsz, n_pair = tok_of_pair.shape
    nt = n_tok // tm
    n_win = n_pair // rwin
    n_item = nt + n_win
    order = jnp.argsort(tok_of_pair, axis=1)
    tok = jnp.take_along_axis(tok_of_pair, order, axis=1)
    bounds = jnp.arange(nt + 1, dtype=jnp.int32) * tm
    off = jnp.sum(tok[:, None, :] < bounds[None, :, None], axis=-1, dtype=jnp.int32)
    first = jnp.minimum(off[:, :-1] // rwin, n_win - 1)
    last = jnp.minimum(jnp.maximum(off[:, 1:] - 1, off[:, :-1]) // rwin, n_win - 1)
    n_items = last - first + 1
    cum = jnp.cumsum(n_items, axis=1)
    k = jnp.arange(n_item, dtype=jnp.int32)
    tile = jnp.sum(cum[:, None, :] <= k[None, :, None], axis=-1, dtype=jnp.int32)
    valid = tile < nt
    tile = jnp.minimum(tile, nt - 1)
    end = jnp.take_along_axis(cum, tile, axis=1)
    start = end - jnp.take_along_axis(n_items, tile, axis=1)
    win = jnp.where(valid, jnp.take_along_axis(first, tile, axis=1) + (k[None] - start),
                    jnp.take_along_axis(last, tile, axis=1))
    flags = (valid * VALID + (valid & (k[None] == start)) * FIRST + (valid & (k[None] == end - 1)) * LAST)
    return order, tok, tile, win.astype(jnp.int32), flags.astype(jnp.int32)


def _combine_kernel(tile_ref, win_ref, flag_ref, y_ref, tok_ref, x_ref, mod_ref, lng_ref, lnb_ref, o_ref, acc_ref, *,
                    alpha, tm):
    b = pl.program_id(0)
    k = pl.program_id(1)
    flags = flag_ref[b, k]

    @pl.when((flags & FIRST) != 0)
    def _():
        acc_ref[...] = jnp.zeros_like(acc_ref)

    @pl.when((flags & VALID) != 0)
    def _():
        tok0 = tile_ref[b, k] * tm
        rows = lax.broadcasted_iota(jnp.int32, (tm, tok_ref.shape[-1]), 0) + tok0
        pick = jnp.where(rows == tok_ref[0, 0], 1.0, 0.0).astype(BF16)
        acc_ref[...] += _dot(pick, y_ref[0])

    @pl.when((flags & LAST) != 0)
    def _():
        mod = mod_ref[0, 0]
        y = alpha * x_ref[0] + mod[5:6] * acc_ref[...]
        o_ref[0] = _ln(y) * lng_ref[...] + lnb_ref[...]


def _combine(y_sorted, tok_sorted, tile, win, flags, xa, mods, lng, lnb, n_lat, tm, rwin, alpha):
    b, t, d = xa.shape
    nl = n_lat // tm
    n_item = tile.shape[1]
    grid_spec = pltpu.PrefetchScalarGridSpec(
        num_scalar_prefetch=3,
        grid=(b, n_item),
        in_specs=[pl.BlockSpec((1, rwin, d), lambda bi, k, tl, wn, fl: (bi, wn[bi, k], 0)),
                  pl.BlockSpec((1, 1, 1, rwin), lambda bi, k, tl, wn, fl: (bi, wn[bi, k], 0, 0)),
                  pl.BlockSpec((1, tm, d), lambda bi, k, tl, wn, fl: (bi, tl[bi, k], 0)),
                  pl.BlockSpec((1, 1, N_MOD, d),
                               lambda bi, k, tl, wn, fl: (bi, (tl[bi, k] >= nl).astype(jnp.int32), 0, 0)),
                  pl.BlockSpec(lng.shape, lambda bi, k, tl, wn, fl: (0, 0)),
                  pl.BlockSpec(lnb.shape, lambda bi, k, tl, wn, fl: (0, 0))],
        out_specs=pl.BlockSpec((1, tm, d), lambda bi, k, tl, wn, fl: (bi, tl[bi, k], 0)),
        scratch_shapes=[pltpu.VMEM((tm, d), F32)],
    )
    return pl.pallas_call(
        functools.partial(_combine_kernel, alpha=alpha, tm=tm),
        grid_spec=grid_spec,
        out_shape=jax.ShapeDtypeStruct((b, t, d), F32),
        compiler_params=_cparams(("parallel", "arbitrary")),
        name="moe_combine",
    )(tile, win, flags, y_sorted, tok_sorted.reshape(b, -1, 1, rwin), xa, mods, lng, lnb)


def _moe(xa, mods, w_router, w_gate_up, w_down, lng, lnb, layer, n_lat, tm, alpha):
    b, t, d = xa.shape
    n_exp = w_router.shape[-1]
    m_ctx = t - n_lat
    wr = jnp.zeros((d, LANE), F32).at[:, :n_exp].set(w_router)
    h, aff = _route(xa, mods, wr, n_lat, tm, n_exp)
    aff = jnp.swapaxes(aff[:, :, :n_exp], 1, 2)
    cap_l = CAPACITY * n_lat // n_exp
    cap_x = CAPACITY * m_ctx // n_exp
    gate_l, idx_l = lax.top_k(aff[:, :, :n_lat], cap_l)
    gate_x, idx_x = lax.top_k(aff[:, :, n_lat:], cap_x)
    idx = jnp.concatenate([idx_l, idx_x + n_lat], axis=-1)
    gate = jnp.concatenate([gate_l, gate_x], axis=-1)
    cap = cap_l + cap_x
    idx_e = jnp.swapaxes(idx, 0, 1)
    bidx = jnp.arange(b)[None, :, None]
    xs = h[bidx, idx_e].reshape(n_exp, b * cap, d)
    gate_e = jnp.swapaxes(gate, 0, 1).reshape(n_exp, b * cap, 1)
    rows = b * cap
    tmr = rows
    for cand in (1088, 1024, 512, 256, 128, 64, 32, 16):
        if rows % cand == 0:
            tmr = cand
            break
    tf = min(512, w_down.shape[2])
    y = _expert_ffn(xs, gate_e, w_gate_up, w_down, layer, tmr, tf)
    n_pair = n_exp * cap
    rwin = next(r for r in (512, 256, 128, 64, 32, 16, 8) if n_pair % r == 0)
    order, tok, tile, win, flags = _combine_plan(idx.reshape(b, n_pair), t, tm, rwin)
    y_sorted = y[order // cap, jnp.arange(b)[:, None] * cap + order % cap]
    return _combine(y_sorted, tok, tile, win, flags, xa, mods, lng, lnb, n_lat, tm, rwin, alpha)


def _rope_tables(n_lat, m_ctx):
    t = np.arange(n_lat)
    rows = (t // GRID_W).astype(np.float32)
    cols = (t % GRID_W).astype(np.float32)
    n_freq = HEAD_DIM // 4
    inv_freq = jnp.asarray(ROPE_THETA, F32) ** (-jnp.arange(n_freq, dtype=F32) / n_freq)
    ang_r = jnp.asarray(rows)[:, None] * inv_freq
    ang_c = jnp.asarray(cols)[:, None] * inv_freq
    cr, sr, cc, sc = jnp.cos(ang_r), jnp.sin(ang_r), jnp.cos(ang_c), jnp.sin(ang_c)
    cos = jnp.concatenate([cr, cr, cc, cc], axis=-1)
    sin = jnp.concatenate([-sr, sr, -sc, sc], axis=-1)
    cos = jnp.concatenate([cos, jnp.ones((m_ctx, HEAD_DIM), F32)], axis=0)
    sin = jnp.concatenate([sin, jnp.zeros((m_ctx, HEAD_DIM), F32)], axis=0)
    return jnp.tile(cos, (1, A_HEADS)), jnp.tile(sin, (1, A_HEADS))


def kernel(x, c, ctx, c_ctx, w_mod, b_mod, w_in, qk_gain, rpb, conv_w, a_log, dt_bias, o_gain, w_branch, w_out,
           ln1_g, ln1_b, w_router, w_gate_up, w_down, ln2_g, ln2_b):
    b, n_lat, d = x.shape
    m_ctx = ctx.shape[1]
    depth = w_mod.shape[0]
    alpha = (2 * depth) ** 0.25
    tm = math.gcd(256, math.gcd(n_lat, m_ctx))
    blk = tm

    xa = jnp.concatenate([x, ctx], axis=1)
    n_rows = -(-(b + 1) // 8) * 8
    cc = jnp.zeros((n_rows, d), F32).at[:b].set(c).at[b].set(c_ctx)
    mod_all = _modulation(cc, w_mod, b_mod).reshape(depth, n_rows, N_MOD, d)
    cos, sin = _rope_tables(n_lat, m_ctx)
    gs_mean = _group_sum_matrix(A_Q_W, 1.0 / HEAD_DIM)
    gs_sum = _group_sum_matrix(C_W, 1.0)
    na_geoms, na_table = _na_classes(n_lat, n_lat + m_ctx)

    offs = np.cumsum((0, A_Q_W, A_KV_W, A_KV_W, B_W, B_W, B_W, 3 * C_W, C_W, 2 * C_HEADS, 2 * C_HEADS, N_BRANCH * d))
    for layer in range(depth):
        ml = mod_all[layer]
        mods = jnp.stack([ml[:b], jnp.broadcast_to(ml[b][None], (b, N_MOD, d))], axis=1)
        wl = w_in[layer]
        wa = wl[:, offs[0]:offs[3]].astype(BF16)
        wb = wl[:, offs[3]:offs[6]].astype(BF16)
        wc = wl[:, offs[6]:offs[8]].astype(BF16)
        wab = jnp.zeros((d, LANE), F32).at[:, :4 * C_HEADS].set(wl[:, offs[8]:offs[10]]).astype(BF16)
        wmg = wl[:, offs[10]:offs[11]].astype(BF16)
        gq = jnp.tile(qk_gain[layer, 0], A_HEADS)[None]
        gk = jnp.tile(qk_gain[layer, 1], A_KV_HEADS)[None]
        qa, ka, va, qb, kb, vb, qkvc, gate_c, ab = _in_projection(
            xa, mods, wa, wb, wc, wab, gq, gk, cos, sin, gs_mean, n_lat, tm)
        oa = _gqa(qa, ka, va, n_lat, min(GQA_TQ, tm), GQA_KEY_CHUNK)
        ob = _neighborhood(qb, kb, vb, _na_bias_slabs(rpb[layer], na_geoms), na_table, n_lat)
        qc, kc, vc, comp = _gdn_prep(qkvc, ab, conv_w[layer], a_log[layer], dt_bias[layer], gs_sum, n_lat, tm)
        u, w, kd, qd, intra = _gdn_local(qc, kc, vc, comp, blk)
        oc0, oc1 = _gdn_scan(u, w, kd, qd, intra, comp, n_lat, blk)
        og = jnp.tile(o_gain[layer], C_HEADS)[None]
        xa = _merge(xa, mods, oa, ob, oc0, oc1, gate_c, wmg, w_branch[layer].astype(BF16),
                    w_out[layer].astype(BF16), og, gs_mean, ln1_g[layer][None], ln1_b[layer][None], n_lat, tm, alpha)
        xa = _moe(xa, mods, w_router[layer], w_gate_up, w_down, ln2_g[layer][None], ln2_b[layer][None], layer,
                  n_lat, tm, alpha)
    return xa[:, :n_lat]
```

```python
import functools
import math

import numpy as np
import jax
import jax.numpy as jnp
from jax import lax
from jax.experimental import pallas as pl
from jax.experimental.pallas import tpu as pltpu

F32 = jnp.float32
BF16 = jnp.bfloat16
HIGHEST = lax.Precision.HIGHEST

GRID_W = 64
HEAD_DIM = 64
A_HEADS = 8
A_KV_HEADS = 2
B_HEADS = 8
C_HEADS = 8
NA_ROWS = 8
NA_COLS = 16
CONV_K = 4
CHUNK = 64
N_BRANCH = 3
BRANCH_W = 512
CAPACITY = 2
N_MOD = 6
LN_EPS = 1e-6
ROPE_THETA = 10000.0
NEG_BIG = -1e30
LOG2E = 1.4426950408889634

A_Q_W = A_HEADS * HEAD_DIM
A_KV_W = A_KV_HEADS * HEAD_DIM
B_W = B_HEADS * HEAD_DIM
C_W = C_HEADS * HEAD_DIM
LANE = 128
VMEM_LIMIT = 56 * 1024 * 1024
GQA_TQ = 128
LOCAL_PAIRS = 2
SCAN_BATCH = 2
VT_ROWS = HEAD_DIM + 16
NA_WIN = NA_ROWS + 2


def _cparams(sem):
    return pltpu.CompilerParams(dimension_semantics=sem, vmem_limit_bytes=VMEM_LIMIT)


def _dot(a, b):
    return jnp.dot(a, b, preferred_element_type=F32)


def _dot_hi(a, b):
    return jnp.dot(a, b, precision=HIGHEST, preferred_element_type=F32)


def _dot_nt(a, b):
    return lax.dot_general(a, b, (((1,), (1,)), ((), ())), preferred_element_type=F32)


def _dot_tn(a, b):
    return lax.dot_general(a, b, (((0,), (0,)), ((), ())), preferred_element_type=F32)


def _split_bf16(x, pieces):
    out = []
    for _ in range(pieces - 1):
        h = x.astype(BF16)
        out.append(h)
        x = x - h.astype(F32)
    out.append(x.astype(BF16))
    return out


def _dot_exact_rhs(a, b, pieces=3):
    return sum(_dot(p, b) for p in _split_bf16(a, pieces))


def _dot_exact_lhs(a, b, pieces=3):
    return sum(_dot(a, p) for p in _split_bf16(b, pieces))


def _ln(x):
    mu = jnp.mean(x, axis=-1, keepdims=True)
    xc = x - mu
    var = jnp.mean(xc * xc, axis=-1, keepdims=True)
    return xc * lax.rsqrt(var + LN_EPS)


def _sigmoid(x):
    return 1.0 / (1.0 + jnp.exp(-x))


def _silu(x):
    return x * _sigmoid(x)


def _group_sum_matrix(width, scale):
    g = np.arange(width) // HEAD_DIM
    return jnp.asarray((g[:, None] == g[None, :]).astype(np.float32) * scale, dtype=BF16)


def _mod_kernel(c_ref, w_ref, b_ref, o_ref):
    c = c_ref[...]
    o_ref[0] = _dot_hi(_silu(c), w_ref[0]) + b_ref[0]


def _modulation(cc, w_mod, b_mod):
    depth, d, dm = w_mod.shape
    r = cc.shape[0]
    tn = min(dm, 1536)
    return pl.pallas_call(
        _mod_kernel,
        grid=(depth, dm // tn),
        in_specs=[pl.BlockSpec((r, d), lambda l, j: (0, 0)),
                  pl.BlockSpec((1, d, tn), lambda l, j: (l, 0, j)),
                  pl.BlockSpec((1, 1, tn), lambda l, j: (l, 0, j))],
        out_specs=pl.BlockSpec((1, r, tn), lambda l, j: (l, 0, j)),
        out_shape=jax.ShapeDtypeStruct((depth, r, dm), F32),
        compiler_params=_cparams(("parallel", "parallel")),
        name="modulation",
    )(cc, w_mod, b_mod.reshape(depth, 1, dm))


def _rope_swap(y):
    w = y.shape[-1]
    lane = lax.broadcasted_iota(jnp.int32, y.shape, 1)
    up = pltpu.roll(y, w - 16, 1)
    down = pltpu.roll(y, 16, 1)
    return jnp.where((lane % 32) < 16, up, down)


def _inproj_kernel(x_ref, mod_ref, wa_ref, wb_ref, wc_ref, wab_ref, gq_ref, gk_ref, cos_ref, sin_ref,
                   gs_ref, qa_ref, ka_ref, va_ref, qb_ref, kb_ref, vb_ref, qkvc_ref, gatec_ref, ab_ref):
    x = x_ref[0]
    mod = mod_ref[0, 0]
    h = (_ln(x) * (1.0 + mod[1:2]) + mod[0:1]).astype(BF16)
    scale = HEAD_DIM ** -0.5 * LOG2E

    za = _dot(h, wa_ref[...])
    q = za[:, :A_Q_W]
    k = za[:, A_Q_W:A_Q_W + A_KV_W]
    v = za[:, A_Q_W + A_KV_W:]
    cos = cos_ref[...]
    sin = sin_ref[...]
    gs = gs_ref[...]
    qn = q * lax.rsqrt(_dot_exact_rhs(q * q, gs, 2) + LN_EPS) * gq_ref[...]
    qn = qn * cos + _rope_swap(qn) * sin
    qa_ref[0] = (qn * scale).astype(BF16)
    kn = k * lax.rsqrt(_dot_exact_rhs(k * k, gs[:A_KV_W, :A_KV_W], 2) + LN_EPS) * gk_ref[...]
    kn = kn * cos[:, :A_KV_W] + _rope_swap(kn) * sin[:, :A_KV_W]
    ones = jnp.ones((VT_ROWS - HEAD_DIM, x.shape[0]), F32)
    vt = v.T
    for j in range(A_KV_HEADS):
        ka_ref[0, j] = kn[:, j * HEAD_DIM:(j + 1) * HEAD_DIM].astype(BF16)
        va_ref[0, j] = jnp.concatenate([vt[j * HEAD_DIM:(j + 1) * HEAD_DIM], ones], axis=0).astype(BF16)

    zb = _dot(h, wb_ref[...])
    qb_ref[0] = (zb[:, :B_W] * scale).astype(BF16)
    kbv = zb[:, B_W:2 * B_W]
    vt = zb[:, 2 * B_W:].T
    for j in range(B_HEADS):
        kb_ref[0, j] = kbv[:, j * HEAD_DIM:(j + 1) * HEAD_DIM].astype(BF16)
        vb_ref[0, j] = jnp.concatenate([vt[j * HEAD_DIM:(j + 1) * HEAD_DIM], ones], axis=0).astype(BF16)

    zc = _dot(h, wc_ref[...])
    qkvc_ref[0] = zc[:, :3 * C_W]
    gatec_ref[0] = zc[:, 3 * C_W:]
    ab_ref[0] = _dot(h, wab_ref[...])


def _in_projection(xa, mods, wa, wb, wc, wab, gq, gk, cos, sin, gs, n_lat, tm):
    b, t, d = xa.shape
    nl = n_lat // tm
    row = lambda bi, i: (bi, i, 0)
    const = lambda bi, i: (0, 0)
    outs = [
        (jax.ShapeDtypeStruct((b, t, A_Q_W), BF16), pl.BlockSpec((1, tm, A_Q_W), row)),
        (jax.ShapeDtypeStruct((b, A_KV_HEADS, t, HEAD_DIM), BF16),
         pl.BlockSpec((1, A_KV_HEADS, tm, HEAD_DIM), lambda bi, i: (bi, 0, i, 0))),
        (jax.ShapeDtypeStruct((b, A_KV_HEADS, VT_ROWS, t), BF16),
         pl.BlockSpec((1, A_KV_HEADS, VT_ROWS, tm), lambda bi, i: (bi, 0, 0, i))),
        (jax.ShapeDtypeStruct((b, t, B_W), BF16), pl.BlockSpec((1, tm, B_W), row)),
        (jax.ShapeDtypeStruct((b, B_HEADS, t, HEAD_DIM), BF16),
         pl.BlockSpec((1, B_HEADS, tm, HEAD_DIM), lambda bi, i: (bi, 0, i, 0))),
        (jax.ShapeDtypeStruct((b, B_HEADS, VT_ROWS, t), BF16),
         pl.BlockSpec((1, B_HEADS, VT_ROWS, tm), lambda bi, i: (bi, 0, 0, i))),
        (jax.ShapeDtypeStruct((b, t, 3 * C_W), F32), pl.BlockSpec((1, tm, 3 * C_W), row)),
        (jax.ShapeDtypeStruct((b, t, C_W), F32), pl.BlockSpec((1, tm, C_W), row)),
        (jax.ShapeDtypeStruct((b, t, LANE), F32), pl.BlockSpec((1, tm, LANE), row)),
    ]
    return pl.pallas_call(
        _inproj_kernel,
        grid=(b, t // tm),
        in_specs=[pl.BlockSpec((1, tm, d), row),
                  pl.BlockSpec((1, 1, N_MOD, d), lambda bi, i: (bi, (i >= nl).astype(jnp.int32), 0, 0)),
                  pl.BlockSpec(wa.shape, const), pl.BlockSpec(wb.shape, const),
                  pl.BlockSpec(wc.shape, const), pl.BlockSpec(wab.shape, const),
                  pl.BlockSpec(gq.shape, const), pl.BlockSpec(gk.shape, const),
                  pl.BlockSpec((tm, A_Q_W), lambda bi, i: (i, 0)),
                  pl.BlockSpec((tm, A_Q_W), lambda bi, i: (i, 0)),
                  pl.BlockSpec(gs.shape, const)],
        out_specs=[o[1] for o in outs],
        out_shape=[o[0] for o in outs],
        compiler_params=_cparams(("parallel", "parallel")),
        name="in_projection",
    )(xa, mods, wa, wb, wc, wab, gq, gk, cos, sin, gs)


def _gqa_kernel(q_ref, k_ref, v_ref, o_ref, *, tq, ck, n_lat, n_tot):
    i = pl.program_id(1)
    g = A_HEADS // A_KV_HEADS

    def attend(key_lo, key_hi, n_split):
        qs = [jnp.concatenate([q_ref[0, :, (kh * g + j) * HEAD_DIM:(kh * g + j + 1) * HEAD_DIM] for j in range(g)],
                              axis=0) for kh in range(A_KV_HEADS)]
        step = (key_hi - key_lo) // n_split
        units = [(kh, key_lo + r * step, key_lo + (r + 1) * step) for r in range(n_split) for kh in range(A_KV_HEADS)]
        ss, ms, ps, oes = {}, {}, {}, {}
        for n in range(len(units) + 2):
            if n < len(units):
                kh, lo, hi = units[n]
                ss[n] = _dot_nt(k_ref[0, kh, lo:hi, :], qs[kh])
            if 0 <= n - 1 < len(units):
                ms[n - 1] = jnp.max(ss[n - 1], axis=0, keepdims=True)
                ps[n - 1] = jnp.exp2((ss.pop(n - 1) - ms[n - 1]).astype(BF16))
            if 0 <= n - 2 < len(units):
                kh, lo, hi = units[n - 2]
                oes[n - 2] = _dot(v_ref[0, kh, :, lo:hi], ps.pop(n - 2))
        for kh in range(A_KV_HEADS):
            mine = [n for n, u in enumerate(units) if u[0] == kh]
            m = functools.reduce(jnp.maximum, [ms[n] for n in mine])
            oe = sum(oes[n] * jnp.exp2(ms[n] - m) for n in mine)
            o = (oe[:HEAD_DIM] / oe[HEAD_DIM:HEAD_DIM + 1]).T.astype(BF16)
            for j in range(g):
                o_ref[0, :, (kh * g + j) * HEAD_DIM:(kh * g + j + 1) * HEAD_DIM] = o[j * tq:(j + 1) * tq]

    @pl.when(i < n_lat // tq)
    def _():
        attend(0, n_tot, ck)

    @pl.when(i >= n_lat // tq)
    def _():
        attend(n_lat, n_tot, 1)


def _gqa(qa, ka, va, n_lat, tq, ck):
    b, t, _ = qa.shape
    return pl.pallas_call(
        functools.partial(_gqa_kernel, tq=tq, ck=ck, n_lat=n_lat, n_tot=t),
        grid=(b, t // tq),
        in_specs=[pl.BlockSpec((1, tq, A_Q_W), lambda bi, i: (bi, i, 0)),
                  pl.BlockSpec((1, A_KV_HEADS, t, HEAD_DIM), lambda bi, i: (bi, 0, 0, 0)),
                  pl.BlockSpec((1, A_KV_HEADS, VT_ROWS, t), lambda bi, i: (bi, 0, 0, 0))],
        out_specs=pl.BlockSpec((1, tq, A_Q_W), lambda bi, i: (bi, i, 0)),
        out_shape=jax.ShapeDtypeStruct((b, t, A_Q_W), BF16),
        compiler_params=_cparams(("parallel", "parallel")),
        name="gqa",
    )(qa, ka, va)


def _na_window_start(r0, rows):
    return np.clip(r0 - NA_ROWS // 2, 0, rows - NA_ROWS - 1) // 2 * 2


def _na_classes(n_lat, n_tot):
    rows = n_lat // GRID_W
    assert rows >= NA_WIN and rows % 2 == 0
    geoms, table = [], []
    for r0 in range(0, rows, 2):
        geom = (int(_na_window_start(r0, rows)) - r0,) + tuple(
            int(np.clip(r0 + qr - NA_ROWS // 2, 0, rows - NA_ROWS)) - r0 for qr in range(2))
        if geom not in geoms:
            geoms.append(geom)
        table.append(geoms.index(geom))
    table += [len(geoms)] * ((n_tot - n_lat) // (2 * GRID_W))
    return geoms, np.asarray(table, np.int32)


def _na_bias_slabs(rpb, geoms):
    n_head = rpb.shape[0]
    cj = np.arange(GRID_W)[:, None]
    c = np.arange(GRID_W)[None, :]
    col_start = np.clip(c - NA_COLS // 2, 0, GRID_W - NA_COLS)
    col_ok = (cj >= col_start) & (cj < col_start + NA_COLS)
    col_off = np.clip(cj - c + (NA_COLS - 1), 0, 2 * NA_COLS - 2)
    spread = np.zeros((2 * NA_COLS - 1, GRID_W * GRID_W), np.float32)
    spread[col_off.reshape(-1), np.arange(GRID_W * GRID_W)] = 1.0
    planes = jnp.dot(rpb * LOG2E, jnp.asarray(spread), precision=HIGHEST)
    planes = planes.reshape(n_head, 2 * NA_ROWS - 1, GRID_W, GRID_W)
    kr = np.arange(NA_WIN)[:, None]
    qr = np.arange(2)[None, :]
    slabs = []
    for a, b0, b1 in geoms:
        first = np.where(qr == 0, b0, b1)
        row_ok = (a + kr >= first) & (a + kr < first + NA_ROWS)
        row_off = np.clip(a + kr - qr + (NA_ROWS - 1), 0, 2 * NA_ROWS - 2)
        rows = jnp.stack([jnp.stack([planes[:, int(row_off[i, j])] for j in range(2)], axis=1)
                          for i in range(NA_WIN)], axis=1)
        ok = row_ok[:, :, None, None] & col_ok[None, None]
        slab = jnp.where(ok[None], rows, NEG_BIG).transpose(0, 1, 3, 2, 4)
        slabs.append(slab.reshape(n_head, NA_WIN * GRID_W, 2 * GRID_W))
    slabs.append(jnp.full_like(slabs[0], NEG_BIG))
    return jnp.stack(slabs, axis=0).astype(F32)


def _na_kernel(cls_ref, q_ref, k_ref, vt_ref, bias_ref, o_ref, *, n_lat, n_tot):
    del cls_ref
    rows = n_lat // GRID_W
    r0 = 2 * pl.program_id(1)
    start = jnp.clip(r0 - NA_ROWS // 2, 0, rows - NA_ROWS - 1) // 2 * 2
    nb = pl.ds(pl.multiple_of(start * GRID_W, 2 * GRID_W), NA_WIN * GRID_W)
    hs = [slice(h * HEAD_DIM, (h + 1) * HEAD_DIM) for h in range(B_HEADS)]
    qs = [q_ref[0, :, hs[h]] for h in range(B_HEADS)]
    s_nb = [_dot_nt(k_ref[0, h, nb, :], qs[h]) + bias_ref[0, h] for h in range(B_HEADS)]
    s_cx = [_dot_nt(k_ref[0, h, n_lat:n_tot, :], qs[h]) for h in range(B_HEADS)]
    ms = [jnp.maximum(jnp.max(a, axis=0, keepdims=True), jnp.max(c, axis=0, keepdims=True))
          for a, c in zip(s_nb, s_cx)]
    p_nb = [jnp.exp2((a - m).astype(BF16)) for a, m in zip(s_nb, ms)]
    p_cx = [jnp.exp2((c - m).astype(BF16)) for c, m in zip(s_cx, ms)]
    for h in range(B_HEADS):
        oe = _dot(vt_ref[0, h, :, nb], p_nb[h]) + _dot(vt_ref[0, h, :, n_lat:n_tot], p_cx[h])
        o_ref[0, :, hs[h]] = (oe[:HEAD_DIM] / oe[HEAD_DIM:HEAD_DIM + 1]).T.astype(BF16)


def _neighborhood(qb, kb, vbt, slabs, table, n_lat):
    b, t, _ = qb.shape
    tq = 2 * GRID_W
    grid_spec = pltpu.PrefetchScalarGridSpec(
        num_scalar_prefetch=1,
        grid=(b, t // tq),
        in_specs=[pl.BlockSpec((1, tq, B_W), lambda bi, pi, cls: (bi, pi, 0)),
                  pl.BlockSpec((1, B_HEADS, t, HEAD_DIM), lambda bi, pi, cls: (bi, 0, 0, 0)),
                  pl.BlockSpec((1, B_HEADS, VT_ROWS, t), lambda bi, pi, cls: (bi, 0, 0, 0)),
                  pl.BlockSpec((1,) + slabs.shape[1:], lambda bi, pi, cls: (cls[pi], 0, 0, 0))],
        out_specs=pl.BlockSpec((1, tq, B_W), lambda bi, pi, cls: (bi, pi, 0)),
    )
    return pl.pallas_call(
        functools.partial(_na_kernel, n_lat=n_lat, n_tot=t),
        grid_spec=grid_spec,
        out_shape=jax.ShapeDtypeStruct((b, t, B_W), BF16),
        compiler_params=_cparams(("parallel", "parallel")),
        name="neighborhood",
    )(jnp.asarray(table), qb, kb, vbt, slabs)


GC_LANE, GL_LANE, BETA_LANE = 0, 2 * C_HEADS, 4 * C_HEADS


def _gdn_prep_kernel(x_ref, prev_ref, next_ref, ab_ref, cw_ref, alog_ref, dtb_ref, gs_ref,
                     q_ref, k_ref, v_ref, comp_ref, *, tm, n_lat, n_tot):
    i = pl.program_id(1)
    nl = n_lat // tm
    nt = n_tot // tm
    first = jnp.logical_or(i == 0, i == nl)
    last = jnp.logical_or(i == nl - 1, i == nt - 1)
    x = x_ref[0]
    pv = jnp.where(first, 0.0, prev_ref[0])
    nx = jnp.where(last, 0.0, next_ref[0])
    row = lax.broadcasted_iota(jnp.int32, (tm, 1), 0)
    xm1 = jnp.where(row == 0, pv[7:8], pltpu.roll(x, 1, 0))
    xm2 = jnp.where(row == 0, pv[6:7], jnp.where(row == 1, pv[7:8], pltpu.roll(x, 2, 0)))
    xp1 = jnp.where(row == tm - 1, nx[0:1], pltpu.roll(x, tm - 1, 0))
    cw = cw_ref[...]
    y = _silu(xm2 * cw[0:1] + xm1 * cw[1:2] + x * cw[2:3] + xp1 * cw[3:4])
    q = y[:, :C_W]
    k = y[:, C_W:2 * C_W]
    gs = gs_ref[...]
    q_ref[0] = q * lax.rsqrt(_dot_exact_rhs(q * q, gs, 2) + LN_EPS) * (HEAD_DIM ** -0.5)
    k_ref[0] = k * lax.rsqrt(_dot_exact_rhs(k * k, gs, 2) + LN_EPS)
    v_ref[0] = y[:, 2 * C_W:]
    ab = ab_ref[0]
    z = ab + dtb_ref[...]
    softplus = jnp.maximum(z, 0.0) + jnp.log(1.0 + jnp.exp(-jnp.abs(z)))
    lane = lax.broadcasted_iota(jnp.int32, ab.shape, 1)
    gb = jnp.where(lane < 2 * C_HEADS, -jnp.exp(alog_ref[...]) * softplus, _sigmoid(ab))
    ii = lax.broadcasted_iota(jnp.int32, (tm, tm), 0)
    jj = lax.broadcasted_iota(jnp.int32, (tm, tm), 1)
    same = (ii // CHUNK) == (jj // CHUNK)
    gb3 = _split_bf16(gb, 3)

    def masked_sum(mask):
        mb = jnp.where(mask, 1.0, 0.0).astype(BF16)
        return sum(_dot(mb, piece) for piece in gb3)

    prefix = masked_sum(jnp.logical_and(same, jj <= ii))
    suffix = masked_sum(jnp.logical_and(same, jj >= ii))
    total = masked_sum(same)
    comp = jnp.where(lane < C_HEADS, prefix,
                     jnp.where(lane < GL_LANE, suffix,
                               jnp.where(lane < BETA_LANE, pltpu.roll(total, GL_LANE, 1),
                                         jnp.where(lane < BETA_LANE + 2 * C_HEADS, pltpu.roll(gb, 2 * C_HEADS, 1),
                                                   0.0))))
    comp_ref[0] = comp


def _gdn_prep(qkvc, ab, conv_w, a_log, dt_bias, gs, n_lat, tm):
    b, t, _ = qkvc.shape
    c3 = 3 * C_W
    alog_row = jnp.zeros((1, LANE), F32).at[0, :2 * C_HEADS].set(a_log.reshape(-1))
    dtb_row = jnp.zeros((1, LANE), F32).at[0, :2 * C_HEADS].set(dt_bias.reshape(-1))
    row = lambda bi, i: (bi, i, 0)
    const2 = lambda bi, i: (0, 0)
    hb = tm // 8
    last8 = t // 8 - 1
    tok = jax.ShapeDtypeStruct((b, t, C_W), F32)
    return pl.pallas_call(
        functools.partial(_gdn_prep_kernel, tm=tm, n_lat=n_lat, n_tot=t),
        grid=(b, t // tm),
        in_specs=[pl.BlockSpec((1, tm, c3), row),
                  pl.BlockSpec((1, 8, c3), lambda bi, i: (bi, jnp.maximum(i * hb - 1, 0), 0)),
                  pl.BlockSpec((1, 8, c3), lambda bi, i: (bi, jnp.minimum((i + 1) * hb, last8), 0)),
                  pl.BlockSpec((1, tm, LANE), row),
                  pl.BlockSpec((CONV_K, c3), const2),
                  pl.BlockSpec((1, LANE), const2), pl.BlockSpec((1, LANE), const2),
                  pl.BlockSpec(gs.shape, const2)],
        out_specs=[pl.BlockSpec((1, tm, C_W), row)] * 3 + [pl.BlockSpec((1, tm, LANE), row)],
        out_shape=[tok, tok, tok, jax.ShapeDtypeStruct((b, t, LANE), F32)],
        compiler_params=_cparams(("parallel", "parallel")),
        name="gdn_prep",
    )(qkvc, qkvc, qkvc, ab, conv_w, alog_row, dtb_row, gs)


def _unit_tri_inverses(lms, eye, bd16, bd32):
    def mm(a, b_):
        return _dot(a.astype(BF16), b_.astype(BF16))

    lds = [jnp.where(bd16, lm, 0.0) for lm in lms]
    xs = [eye - ld for ld in lds]
    lds = [ld.astype(BF16) for ld in lds]
    pws = [_dot(ld, ld).astype(BF16) for ld in lds]
    for step in range(3):
        xs = [x + mm(x, pw) for x, pw in zip(xs, pws)]
        if step < 2:
            pws = [_dot(pw, pw).astype(BF16) for pw in pws]
    for level in (lambda lm: jnp.where(jnp.logical_and(bd32, jnp.logical_not(bd16)), lm, 0.0),
                  lambda lm: jnp.where(bd32, 0.0, lm)):
        xbs = [x.astype(BF16) for x in xs]
        ys = [_dot(level(lm).astype(BF16), xb) for lm, xb in zip(lms, xbs)]
        xs = [x - _dot(xb, y.astype(BF16)) for x, xb, y in zip(xs, xbs, ys)]
    return xs


def _gdn_local_kernel(q_ref, k_ref, v_ref, comp_ref, ex_ref, sel_ref, u_ref, w_ref, kd_ref, qd_ref, in_ref):
    n = q_ref.shape[1]
    hp = 2 * HEAD_DIM
    n_pr = q_ref.shape[2] // hp
    ii = lax.broadcasted_iota(jnp.int32, (n, n), 0)
    jj = lax.broadcasted_iota(jnp.int32, (n, n), 1)
    same = (ii // CHUNK) == (jj // CHUNK)
    eye = (ii == jj).astype(F32)
    bd16 = (ii // 16) == (jj // 16)
    bd32 = (ii // 32) == (jj // 32)
    lane = lax.broadcasted_iota(jnp.int32, (1, hp), 1)
    lane2 = lax.broadcasted_iota(jnp.int32, (1, 2 * hp), 1)
    incls = [jnp.logical_and(same, ii >= jj), jnp.logical_and(same, ii <= jj)]
    stricts = [jnp.logical_and(same, ii > jj), jnp.logical_and(same, ii < jj)]
    lms, intras, rhss = [], [], []
    for pr in range(n_pr):
        ls = slice(pr * hp, (pr + 1) * hp)
        q = q_ref[0, :, ls]
        k = k_ref[0, :, ls]
        v = v_ref[0, :, ls]
        e = _dot_exact_rhs(comp_ref[0], ex_ref[pr])
        kh = [jnp.where((lane // HEAD_DIM) == hh, k, 0.0).astype(BF16) for hh in range(2)]
        for p in range(2):
            gc = e[:, p * hp:(p + 1) * hp]
            gl = e[:, (2 + p) * hp:(3 + p) * hp]
            beta = e[:, (4 + p) * hp:(5 + p) * hp]
            eg = jnp.exp(gc)
            kb = k * beta
            kd_ref[p, 0, :, ls] = (k * jnp.exp(gl - gc)).astype(BF16)
            qd_ref[p, 0, :, ls] = (q * eg).astype(BF16)
            rhs = jnp.concatenate([v * beta, kb * eg], axis=1)
            kbq = jnp.concatenate([kb, q], axis=0).astype(BF16)
            for hh in range(2):
                gcb = jnp.broadcast_to(gc[:, hh * HEAD_DIM:hh * HEAD_DIM + 1], (n, n))
                decay = jnp.exp(jnp.where(incls[p], gcb - gcb.T, NEG_BIG))
                aq = _dot_nt(kbq, kh[hh])
                lms.append(jnp.where(stricts[p], aq[:n] * decay, 0.0))
                intras.append(jnp.where(incls[p], aq[n:] * decay, 0.0).astype(BF16))
                rhss.append(jnp.where(((lane2 % hp) // HEAD_DIM) == hh, rhs, 0.0).astype(BF16))
    tmats = _unit_tri_inverses(lms, eye, bd16, bd32)
    for pr in range(n_pr):
        ls = slice(pr * hp, (pr + 1) * hp)
        for p in range(2):
            at = 4 * pr + 2 * p
            uw = sum(_dot(tmats[at + hh].astype(BF16), rhss[at + hh]) for hh in range(2))
            intra2 = sum(_dot(intras[at + hh], sel_ref[hh]) for hh in range(2))
            u_ref[p, 0, :, ls] = uw[:, :hp].astype(BF16)
            w_ref[p, 0, :, ls] = uw[:, hp:].astype(BF16)
            in_ref[p, 0, :, ls] = intra2.astype(BF16)


def _gdn_local(q, k, v, comp, blk):
    b, t, _ = q.shape
    hp = 2 * HEAD_DIM
    n_pair = C_W // hp
    ex = np.zeros((n_pair, LANE, 6 * hp), np.float32)
    for pr in range(n_pair):
        for p in range(2):
            for hh in range(2):
                src = p * C_HEADS + pr * 2 + hh
                for kind, base in enumerate((GC_LANE, GL_LANE, BETA_LANE)):
                    col = (2 * kind + p) * hp + hh * HEAD_DIM
                    ex[pr, base + src, col:col + HEAD_DIM] = 1.0
    sel = np.zeros((2, blk, hp), np.float32)
    for hh in range(2):
        sel[hh, np.arange(blk), hh * HEAD_DIM + np.arange(blk) % CHUNK] = 1.0
    gp = LOCAL_PAIRS
    tok = lambda bi, pg, i: (bi, i, pg)
    dirs = lambda bi, pg, i: (0, bi, i, pg)
    out = jax.ShapeDtypeStruct((2, b, t, C_W), BF16)
    return pl.pallas_call(
        _gdn_local_kernel,
        grid=(b, n_pair // gp, t // blk),
        in_specs=[pl.BlockSpec((1, blk, gp * hp), tok)] * 3
                 + [pl.BlockSpec((1, blk, LANE), lambda bi, pg, i: (bi, i, 0)),
                    pl.BlockSpec((gp, LANE, 6 * hp), lambda bi, pg, i: (pg, 0, 0)),
                    pl.BlockSpec((2, blk, hp), lambda bi, pg, i: (0, 0, 0))],
        out_specs=[pl.BlockSpec((2, 1, blk, gp * hp), dirs)] * 5,
        out_shape=[out] * 5,
        compiler_params=_cparams(("parallel", "parallel", "parallel")),
        name="gdn_local",
    )(q, k, v, comp, jnp.asarray(ex, dtype=BF16), jnp.asarray(sel, dtype=BF16))


def _gdn_scan_kernel(*refs, ncb, nb):
    ins = (refs[0:6], refs[6:12])
    ex_ref = refs[12]
    outs = refs[13:15]
    s_ref = refs[15]
    j = pl.program_id(1)

    @pl.when(j == 0)
    def _():
        s_ref[...] = jnp.zeros_like(s_ref)

    gw = 4 * HEAD_DIM
    n_grp = C_W // gw
    lane_head = lax.broadcasted_iota(jnp.int32, (1, gw), 1) // HEAD_DIM
    rr = lax.broadcasted_iota(jnp.int32, (gw, gw), 0) // HEAD_DIM
    cc_ = lax.broadcasted_iota(jnp.int32, (gw, gw), 1) // HEAD_DIM
    blockdiag = rr == cc_
    chains = [(bi, p, g) for bi in range(nb) for p in range(2) for g in range(n_grp)]
    for cc in range(ncb):
        rows = [slice(cc * CHUNK, (cc + 1) * CHUNK), slice((ncb - 1 - cc) * CHUNK, (ncb - cc) * CHUNK)]
        sdec = {(bi, p): jnp.exp(_dot_exact_rhs(ins[p][5][bi, rows[p].start:rows[p].start + 8, :], ex_ref[...]))
                for bi in range(nb) for p in range(2)}
        lanes = [slice(g * gw, (g + 1) * gw) for g in range(n_grp)]
        ss = [s_ref[bi, p, g] for bi, p, g in chains]
        sbs = [s.astype(BF16) for s in ss]
        v_news = [ins[p][0][0, bi, rows[p], lanes[g]].astype(F32) - _dot(ins[p][1][0, bi, rows[p], lanes[g]], sb)
                  for (bi, p, g), sb in zip(chains, sbs)]
        vstacks = [jnp.concatenate([jnp.where(lane_head == h, v_new, 0.0).astype(BF16)
                                    for h in range(gw // HEAD_DIM)], axis=0) for v_new in v_news]
        for (bi, p, g), s, sb, v_new, vstack in zip(chains, ss, sbs, v_news, vstacks):
            o = _dot(ins[p][3][0, bi, rows[p], lanes[g]], sb) + _dot(ins[p][4][0, bi, rows[p], lanes[g]], vstack)
            upd = _dot_tn(ins[p][2][0, bi, rows[p], lanes[g]], v_new.astype(BF16))
            decay = sdec[bi, p][0:1, p * C_W + g * gw:p * C_W + (g + 1) * gw]
            s_ref[bi, p, g] = s * decay + jnp.where(blockdiag, upd, 0.0)
            outs[p][bi, rows[p], lanes[g]] = o


def _gdn_scan(u, w, kd, qd, intra, comp, n_lat, blk):
    _, b, t, _ = u.shape
    nl = n_lat // blk
    nx = (t - n_lat) // blk
    ncb = blk // CHUNK
    ex = np.zeros((LANE, 2 * C_W), np.float32)
    for p in range(2):
        for h in range(C_HEADS):
            ex[GL_LANE + p * C_HEADS + h, p * C_W + h * HEAD_DIM:p * C_W + (h + 1) * HEAD_DIM] = 1.0

    def blk0(j):
        return jnp.where(j < nx, nl + j, j - nx)

    def blk1(j):
        return jnp.where(j < nx, nl + nx - 1 - j, nl - 1 - (j - nx))

    nb = SCAN_BATCH if b % SCAN_BATCH == 0 else 1
    specs = []
    for p, bf in enumerate((blk0, blk1)):
        specs += [pl.BlockSpec((1, nb, blk, C_W), lambda bi, j, p=p, bf=bf: (p, bi, bf(j), 0))] * 5
        specs += [pl.BlockSpec((nb, blk, LANE), lambda bi, j, bf=bf: (bi, bf(j), 0))]
    specs += [pl.BlockSpec((LANE, 2 * C_W), lambda bi, j: (0, 0))]
    out = jax.ShapeDtypeStruct((b, t, C_W), F32)
    gw = 4 * HEAD_DIM
    return pl.pallas_call(
        functools.partial(_gdn_scan_kernel, ncb=ncb, nb=nb),
        grid=(b // nb, nl + nx),
        in_specs=specs,
        out_specs=[pl.BlockSpec((nb, blk, C_W), lambda bi, j: (bi, blk0(j), 0)),
                   pl.BlockSpec((nb, blk, C_W), lambda bi, j: (bi, blk1(j), 0))],
        out_shape=[out, out],
        scratch_shapes=[pltpu.VMEM((nb, 2, C_W // gw, gw, gw), F32)],
        compiler_params=_cparams(("parallel", "arbitrary")),
        name="gdn_scan",
    )(u, w, kd, qd, intra, comp, u, w, kd, qd, intra, comp, jnp.asarray(ex, dtype=BF16))


def _merge_kernel(x_ref, mod_ref, oa_ref, ob_ref, oc0_ref, oc1_ref, gate_ref, wmg_ref, wbr_ref, wout_ref, og_ref,
                  gs_ref, lng_ref, lnb_ref, wr_ref, o_ref, h_ref, aff_ref, *, alpha, n_exp):
    x = x_ref[0]
    d = x.shape[-1]
    mod = mod_ref[0, 0]
    h = (_ln(x) * (1.0 + mod[1:2]) + mod[0:1]).astype(BF16)
    gates = _sigmoid(_dot(h, wmg_ref[...]))
    oc = oc0_ref[0] + oc1_ref[0]
    ocn = oc * lax.rsqrt(_dot_exact_rhs(oc * oc, gs_ref[...], 2) + LN_EPS) * og_ref[...] * _silu(gate_ref[0])
    m = gates[:, :d] * _dot(oa_ref[0], wbr_ref[0])
    m = m + gates[:, d:2 * d] * _dot(ob_ref[0], wbr_ref[1])
    m = m + gates[:, 2 * d:] * _dot(ocn.astype(BF16), wbr_ref[2])
    mix = _dot(m.astype(BF16), wout_ref[...])
    y = alpha * x + mod[2:3] * mix
    x1 = _ln(y) * lng_ref[...] + lnb_ref[...]
    o_ref[0] = x1
    h2 = _ln(x1) * (1.0 + mod[4:5]) + mod[3:4]
    h_ref[0] = h2.astype(BF16)
    hp = _split_bf16(h2, 3)
    logits = sum(_dot(hp[i], wr_ref[j]) for i, j in ((2, 0), (1, 1), (0, 2), (1, 0), (0, 1), (0, 0)))
    lane = lax.broadcasted_iota(jnp.int32, logits.shape, 1)
    logits = jnp.where(lane < n_exp, logits, NEG_BIG)
    ex = jnp.exp(logits - jnp.max(logits, axis=-1, keepdims=True))
    aff_ref[0] = ex / jnp.sum(ex, axis=-1, keepdims=True)


def _merge(xa, mods, oa, ob, oc0, oc1, gate_c, wmg, wbr, wout, og, gs, lng, lnb, w_router, n_lat, tm, alpha):
    b, t, d = xa.shape
    nl = n_lat // tm
    n_exp = w_router.shape[-1]
    wr_pad = jnp.pad(w_router, ((0, 0), (0, LANE - n_exp)))
    pieces, rest = [], wr_pad
    for _ in range(3):
        piece = lax.reduce_precision(rest, exponent_bits=8, mantissa_bits=7)
        pieces.append(piece.astype(BF16))
        rest = rest - piece
    wr = jnp.stack(pieces)
    row = lambda bi, i: (bi, i, 0)
    const2 = lambda bi, i: (0, 0)
    return pl.pallas_call(
        functools.partial(_merge_kernel, alpha=alpha, n_exp=n_exp),
        grid=(b, t // tm),
        in_specs=[pl.BlockSpec((1, tm, d), row),
                  pl.BlockSpec((1, 1, N_MOD, d), lambda bi, i: (bi, (i >= nl).astype(jnp.int32), 0, 0)),
                  pl.BlockSpec((1, tm, BRANCH_W), row), pl.BlockSpec((1, tm, BRANCH_W), row),
                  pl.BlockSpec((1, tm, BRANCH_W), row), pl.BlockSpec((1, tm, BRANCH_W), row),
                  pl.BlockSpec((1, tm, BRANCH_W), row),
                  pl.BlockSpec(wmg.shape, const2),
                  pl.BlockSpec(wbr.shape, lambda bi, i: (0, 0, 0)),
                  pl.BlockSpec(wout.shape, const2),
                  pl.BlockSpec(og.shape, const2), pl.BlockSpec(gs.shape, const2),
                  pl.BlockSpec(lng.shape, const2), pl.BlockSpec(lnb.shape, const2),
                  pl.BlockSpec(wr.shape, lambda bi, i: (0, 0, 0))],
        out_specs=[pl.BlockSpec((1, tm, d), row), pl.BlockSpec((1, tm, d), row), pl.BlockSpec((1, tm, LANE), row)],
        out_shape=[jax.ShapeDtypeStruct((b, t, d), F32), jax.ShapeDtypeStruct((b, t, d), BF16),
                   jax.ShapeDtypeStruct((b, t, LANE), F32)],
        compiler_params=_cparams(("parallel", "parallel")),
        name="merge",
    )(xa, mods, oa, ob, oc0, oc1, gate_c, wmg, wbr, wout, og, gs, lng, lnb, wr)


def _ffn_kernel(x_ref, wg_ref, wu_ref, wd_ref, gate_ref, o_ref, acc_ref):
    f = pl.program_id(2)

    @pl.when(f == 0)
    def _():
        acc_ref[...] = jnp.zeros_like(acc_ref)

    x = x_ref[0]
    g = _dot(x, wg_ref[0, 0].astype(BF16))
    u = _dot(x, wu_ref[0, 0].astype(BF16))
    a = (_silu(g) * u).astype(BF16)
    acc_ref[...] += _dot(a, wd_ref[0, 0].astype(BF16))

    @pl.when(f == pl.num_programs(2) - 1)
    def _():
        o_ref[0] = (acc_ref[...] * gate_ref[0]).astype(BF16)


def _expert_ffn(xs, gate, w_gate_up, w_down, layer, tm, tf):
    e, r, d = xs.shape
    f = w_down.shape[2]
    nf = f // tf
    return pl.pallas_call(
        _ffn_kernel,
        grid=(e, r // tm, nf),
        in_specs=[pl.BlockSpec((1, tm, d), lambda ei, i, fi: (ei, i, 0)),
                  pl.BlockSpec((1, 1, d, tf), lambda ei, i, fi: (layer, ei, 0, fi)),
                  pl.BlockSpec((1, 1, d, tf), lambda ei, i, fi: (layer, ei, 0, nf + fi)),
                  pl.BlockSpec((1, 1, tf, d), lambda ei, i, fi: (layer, ei, fi, 0)),
                  pl.BlockSpec((1, tm, 1), lambda ei, i, fi: (ei, i, 0))],
        out_specs=pl.BlockSpec((1, tm, d), lambda ei, i, fi: (ei, i, 0)),
        out_shape=jax.ShapeDtypeStruct((e, r, d), BF16),
        scratch_shapes=[pltpu.VMEM((tm, d), F32)],
        compiler_params=_cparams(("parallel", "parallel", "arbitrary")),
        name="expert_ffn",
    )(xs, w_gate_up, w_gate_up, w_down, gate)


VALID, FIRST, LAST = 1, 2, 4


def _combine_plan(tok_of_pair, n_tok, tm, rwin):
    bsz, n_pair = tok_of_pair.shape
    nt = n_tok // tm
    n_win = n_pair // rwin
    n_item = nt + n_win
    order = jnp.argsort(tok_of_pair, axis=1)
    tok = jnp.take_along_axis(tok_of_pair, order, axis=1)
    bounds = jnp.arange(nt + 1, dtype=jnp.int32) * tm
    off = jnp.sum(tok[:, None, :] < bounds[None, :, None], axis=-1, dtype=jnp.int32)
    first = jnp.minimum(off[:, :-1] // rwin, n_win - 1)
    last = jnp.minimum(jnp.maximum(off[:, 1:] - 1, off[:, :-1]) // rwin, n_win - 1)
    n_items = last - first + 1
    cum = jnp.cumsum(n_items, axis=1)
    k = jnp.arange(n_item, dtype=jnp.int32)
    tile = jnp.sum(cum[:, None, :] <= k[None, :, None], axis=-1, dtype=jnp.int32)
    valid = tile < nt
    tile = jnp.minimum(tile, nt - 1)
    end = jnp.take_along_axis(cum, tile, axis=1)
    start = end - jnp.take_along_axis(n_items, tile, axis=1)
    win = jnp.where(valid, jnp.take_along_axis(first, tile, axis=1) + (k[None] - start),
                    jnp.take_along_axis(last, tile, axis=1))
    flags = (valid * VALID + (valid & (k[None] == start)) * FIRST + (valid & (k[None] == end - 1)) * LAST)
    return order, tok, tile, win.astype(jnp.int32), flags.astype(jnp.int32)


def _combine_kernel(tile_ref, win_ref, flag_ref, y_ref, tok_ref, x_ref, mod_ref, lng_ref, lnb_ref, o_ref, acc_ref, *,
                    alpha, tm):
    b = pl.program_id(0)
    k = pl.program_id(1)
    flags = flag_ref[b, k]

    @pl.when((flags & FIRST) != 0)
    def _():
        acc_ref[...] = jnp.zeros_like(acc_ref)

    @pl.when((flags & VALID) != 0)
    def _():
        tok0 = tile_ref[b, k] * tm
        rows = lax.broadcasted_iota(jnp.int32, (tm, tok_ref.shape[-1]), 0) + tok0
        pick = jnp.where(rows == tok_ref[0, 0], 1.0, 0.0).astype(BF16)
        acc_ref[...] += _dot(pick, y_ref[0])

    @pl.when((flags & LAST) != 0)
    def _():
        mod = mod_ref[0, 0]
        y = alpha * x_ref[0] + mod[5:6] * acc_ref[...]
        o_ref[0] = _ln(y) * lng_ref[...] + lnb_ref[...]


def _combine(y_sorted, tok_sorted, tile, win, flags, xa, mods, lng, lnb, n_lat, tm, rwin, alpha):
    b, t, d = xa.shape
    nl = n_lat // tm
    n_item = tile.shape[1]
    grid_spec = pltpu.PrefetchScalarGridSpec(
        num_scalar_prefetch=3,
        grid=(b, n_item),
        in_specs=[pl.BlockSpec((1, rwin, d), lambda bi, k, tl, wn, fl: (bi, wn[bi, k], 0)),
                  pl.BlockSpec((1, 1, 1, rwin), lambda bi, k, tl, wn, fl: (bi, wn[bi, k], 0, 0)),
                  pl.BlockSpec((1, tm, d), lambda bi, k, tl, wn, fl: (bi, tl[bi, k], 0)),
                  pl.BlockSpec((1, 1, N_MOD, d),
                               lambda bi, k, tl, wn, fl: (bi, (tl[bi, k] >= nl).astype(jnp.int32), 0, 0)),
                  pl.BlockSpec(lng.shape, lambda bi, k, tl, wn, fl: (0, 0)),
                  pl.BlockSpec(lnb.shape, lambda bi, k, tl, wn, fl: (0, 0))],
        out_specs=pl.BlockSpec((1, tm, d), lambda bi, k, tl, wn, fl: (bi, tl[bi, k], 0)),
        scratch_shapes=[pltpu.VMEM((tm, d), F32)],
    )
    return pl.pallas_call(
        functools.partial(_combine_kernel, alpha=alpha, tm=tm),
        grid_spec=grid_spec,
        out_shape=jax.ShapeDtypeStruct((b, t, d), F32),
        compiler_params=_cparams(("parallel", "arbitrary")),
        name="moe_combine",
    )(tile, win, flags, y_sorted, tok_sorted.reshape(b, -1, 1, rwin), xa, mods, lng, lnb)


def _moe(xa, h, aff, mods, n_exp, w_gate_up, w_down, lng, lnb, layer, n_lat, tm, alpha):
    b, t, d = xa.shape
    m_ctx = t - n_lat
    aff = jnp.swapaxes(aff[:, :, :n_exp], 1, 2)
    cap_l = CAPACITY * n_lat // n_exp
    cap_x = CAPACITY * m_ctx // n_exp
    gate_l, idx_l = lax.top_k(aff[:, :, :n_lat], cap_l)
    gate_x, idx_x = lax.top_k(aff[:, :, n_lat:], cap_x)
    idx = jnp.concatenate([idx_l, idx_x + n_lat], axis=-1)
    gate = jnp.concatenate([gate_l, gate_x], axis=-1)
    cap = cap_l + cap_x
    idx_e = jnp.swapaxes(idx, 0, 1)
    bidx = jnp.arange(b)[None, :, None]
    xs = h[bidx, idx_e].reshape(n_exp, b * cap, d)
    gate_e = jnp.swapaxes(gate, 0, 1).reshape(n_exp, b * cap, 1)
    rows = b * cap
    tmr = rows
    for cand in (1088, 1024, 512, 256, 128, 64, 32, 16):
        if rows % cand == 0:
            tmr = cand
            break
    tf = min(512, w_down.shape[2])
    y = _expert_ffn(xs, gate_e, w_gate_up, w_down, layer, tmr, tf)
    n_pair = n_exp * cap
    rwin = next(r for r in (512, 256, 128, 64, 32, 16, 8) if n_pair % r == 0)
    order, tok, tile, win, flags = _combine_plan(idx.reshape(b, n_pair), t, tm, rwin)
    y_sorted = y[order // cap, jnp.arange(b)[:, None] * cap + order % cap]
    return _combine(y_sorted, tok, tile, win, flags, xa, mods, lng, lnb, n_lat, tm, rwin, alpha)


def _rope_tables(n_lat, m_ctx):
    t = np.arange(n_lat)
    rows = (t // GRID_W).astype(np.float32)
    cols = (t % GRID_W).astype(np.float32)
    n_freq = HEAD_DIM // 4
    inv_freq = jnp.asarray(ROPE_THETA, F32) ** (-jnp.arange(n_freq, dtype=F32) / n_freq)
    ang_r = jnp.asarray(rows)[:, None] * inv_freq
    ang_c = jnp.asarray(cols)[:, None] * inv_freq
    cr, sr, cc, sc = jnp.cos(ang_r), jnp.sin(ang_r), jnp.cos(ang_c), jnp.sin(ang_c)
    cos = jnp.concatenate([cr, cr, cc, cc], axis=-1)
    sin = jnp.concatenate([-sr, sr, -sc, sc], axis=-1)
    cos = jnp.concatenate([cos, jnp.ones((m_ctx, HEAD_DIM), F32)], axis=0)
    sin = jnp.concatenate([sin, jnp.zeros((m_ctx, HEAD_DIM), F32)], axis=0)
    return jnp.tile(cos, (1, A_HEADS)), jnp.tile(sin, (1, A_HEADS))


def kernel(x, c, ctx, c_ctx, w_mod, b_mod, w_in, qk_gain, rpb, conv_w, a_log, dt_bias, o_gain, w_branch, w_out,
           ln1_g, ln1_b, w_router, w_gate_up, w_down, ln2_g, ln2_b):
    b, n_lat, d = x.shape
    m_ctx = ctx.shape[1]
    depth = w_mod.shape[0]
    alpha = (2 * depth) ** 0.25
    tm = math.gcd(256, math.gcd(n_lat, m_ctx))
    blk = tm

    xa = jnp.concatenate([x, ctx], axis=1)
    n_rows = -(-(b + 1) // 8) * 8
    cc = jnp.zeros((n_rows, d), F32).at[:b].set(c).at[b].set(c_ctx)
    mod_all = _modulation(cc, w_mod, b_mod).reshape(depth, n_rows, N_MOD, d)
    cos, sin = _rope_tables(n_lat, m_ctx)
    gs_mean = _group_sum_matrix(A_Q_W, 1.0 / HEAD_DIM)
    gs_sum = _group_sum_matrix(C_W, 1.0)
    na_geoms, na_table = _na_classes(n_lat, n_lat + m_ctx)

    offs = np.cumsum((0, A_Q_W, A_KV_W, A_KV_W, B_W, B_W, B_W, 3 * C_W, C_W, 2 * C_HEADS, 2 * C_HEADS, N_BRANCH * d))
    for layer in range(depth):
        ml = mod_all[layer]
        mods = jnp.stack([ml[:b], jnp.broadcast_to(ml[b][None], (b, N_MOD, d))], axis=1)
        wl = w_in[layer]
        wa = wl[:, offs[0]:offs[3]].astype(BF16)
        wb = wl[:, offs[3]:offs[6]].astype(BF16)
        wc = wl[:, offs[6]:offs[8]].astype(BF16)
        wab = jnp.pad(wl[:, offs[8]:offs[10]], ((0, 0), (0, LANE - 4 * C_HEADS))).astype(BF16)
        wmg = wl[:, offs[10]:offs[11]].astype(BF16)
        gq = jnp.tile(qk_gain[layer, 0], A_HEADS)[None]
        gk = jnp.tile(qk_gain[layer, 1], A_KV_HEADS)[None]
        qa, ka, va, qb, kb, vb, qkvc, gate_c, ab = _in_projection(
            xa, mods, wa, wb, wc, wab, gq, gk, cos, sin, gs_mean, n_lat, tm)
        oa = _gqa(qa, ka, va, n_lat, min(GQA_TQ, tm), 2 if (n_lat + m_ctx) % (2 * LANE) == 0 else 1)
        ob = _neighborhood(qb, kb, vb, _na_bias_slabs(rpb[layer], na_geoms), na_table, n_lat)
        qc, kc, vc, comp = _gdn_prep(qkvc, ab, conv_w[layer], a_log[layer], dt_bias[layer], gs_sum, n_lat, tm)
        u, w, kd, qd, intra = _gdn_local(qc, kc, vc, comp, blk)
        oc0, oc1 = _gdn_scan(u, w, kd, qd, intra, comp, n_lat, blk)
        og = jnp.tile(o_gain[layer], C_HEADS)[None]
        xa, h, aff = _merge(xa, mods, oa, ob, oc0, oc1, gate_c, wmg, w_branch[layer].astype(BF16),
                            w_out[layer].astype(BF16), og, gs_mean, ln1_g[layer][None], ln1_b[layer][None],
                            w_router[layer], n_lat, tm, alpha)
        xa = _moe(xa, h, aff, mods, w_router.shape[-1], w_gate_up, w_down, ln2_g[layer][None], ln2_b[layer][None],
                  layer, n_lat, tm, alpha)
    return xa[:, :n_lat]
```

```python
import functools
import math

import numpy as np
import jax
import jax.numpy as jnp
from jax import lax
from jax.experimental import pallas as pl
from jax.experimental.pallas import tpu as pltpu

F32 = jnp.float32
BF16 = jnp.bfloat16
HIGHEST = lax.Precision.HIGHEST

GRID_W = 64
HEAD_DIM = 64
A_HEADS = 8
A_KV_HEADS = 2
B_HEADS = 8
C_HEADS = 8
NA_ROWS = 8
NA_COLS = 16
CONV_K = 4
CHUNK = 64
N_BRANCH = 3
BRANCH_W = 512
CAPACITY = 2
N_MOD = 6
LN_EPS = 1e-6
ROPE_THETA = 10000.0
NEG_BIG = -1e30
LOG2E = 1.4426950408889634

A_Q_W = A_HEADS * HEAD_DIM
A_KV_W = A_KV_HEADS * HEAD_DIM
B_W = B_HEADS * HEAD_DIM
C_W = C_HEADS * HEAD_DIM
LANE = 128
VMEM_LIMIT = 56 * 1024 * 1024
GQA_TQ = 128
LOCAL_PAIRS = 2
SCAN_BATCH = 2
VT_ROWS = HEAD_DIM + 16
NA_WIN = NA_ROWS + 2


def _cparams(sem):
    return pltpu.CompilerParams(dimension_semantics=sem, vmem_limit_bytes=VMEM_LIMIT)


def _dot(a, b):
    return jnp.dot(a, b, preferred_element_type=F32)


def _dot_hi(a, b):
    return jnp.dot(a, b, precision=HIGHEST, preferred_element_type=F32)


def _dot_nt(a, b):
    return lax.dot_general(a, b, (((1,), (1,)), ((), ())), preferred_element_type=F32)


def _dot_tn(a, b):
    return lax.dot_general(a, b, (((0,), (0,)), ((), ())), preferred_element_type=F32)


def _split_bf16(x, pieces):
    out = []
    for _ in range(pieces - 1):
        h = x.astype(BF16)
        out.append(h)
        x = x - h.astype(F32)
    out.append(x.astype(BF16))
    return out


def _dot_exact_rhs(a, b, pieces=3):
    return sum(_dot(p, b) for p in _split_bf16(a, pieces))


def _dot_exact_lhs(a, b, pieces=3):
    return sum(_dot(a, p) for p in _split_bf16(b, pieces))


def _ln(x):
    mu = jnp.mean(x, axis=-1, keepdims=True)
    xc = x - mu
    var = jnp.mean(xc * xc, axis=-1, keepdims=True)
    return xc * lax.rsqrt(var + LN_EPS)


def _sigmoid(x):
    return 1.0 / (1.0 + jnp.exp(-x))


def _silu(x):
    return x * _sigmoid(x)


def _group_sum_matrix(width, scale):
    g = np.arange(width) // HEAD_DIM
    return jnp.asarray((g[:, None] == g[None, :]).astype(np.float32) * scale, dtype=BF16)


def _mod_kernel(c_ref, w_ref, b_ref, o_ref):
    c = c_ref[...]
    o_ref[0] = _dot_hi(_silu(c), w_ref[0]) + b_ref[0]


def _modulation(cc, w_mod, b_mod):
    depth, d, dm = w_mod.shape
    r = cc.shape[0]
    tn = min(dm, 1536)
    return pl.pallas_call(
        _mod_kernel,
        grid=(depth, dm // tn),
        in_specs=[pl.BlockSpec((r, d), lambda l, j: (0, 0)),
                  pl.BlockSpec((1, d, tn), lambda l, j: (l, 0, j)),
                  pl.BlockSpec((1, 1, tn), lambda l, j: (l, 0, j))],
        out_specs=pl.BlockSpec((1, r, tn), lambda l, j: (l, 0, j)),
        out_shape=jax.ShapeDtypeStruct((depth, r, dm), F32),
        compiler_params=_cparams(("parallel", "parallel")),
        name="modulation",
    )(cc, w_mod, b_mod.reshape(depth, 1, dm))


def _rope_swap(y):
    w = y.shape[-1]
    lane = lax.broadcasted_iota(jnp.int32, y.shape, 1)
    up = pltpu.roll(y, w - 16, 1)
    down = pltpu.roll(y, 16, 1)
    return jnp.where((lane % 32) < 16, up, down)


def _inproj_kernel(x_ref, mod_ref, wa_ref, wb_ref, wc_ref, wab_ref, gq_ref, gk_ref, cos_ref, sin_ref,
                   gs_ref, qa_ref, ka_ref, va_ref, qb_ref, kb_ref, vb_ref, qkvc_ref, gatec_ref, ab_ref):
    x = x_ref[0]
    mod = mod_ref[0, 0]
    h = (_ln(x) * (1.0 + mod[1:2]) + mod[0:1]).astype(BF16)
    scale = HEAD_DIM ** -0.5 * LOG2E

    za = _dot(h, wa_ref[...])
    q = za[:, :A_Q_W]
    k = za[:, A_Q_W:A_Q_W + A_KV_W]
    v = za[:, A_Q_W + A_KV_W:]
    cos = cos_ref[...]
    sin = sin_ref[...]
    gs = gs_ref[...]
    qn = q * lax.rsqrt(_dot_exact_rhs(q * q, gs, 2) + LN_EPS) * gq_ref[...]
    qn = qn * cos + _rope_swap(qn) * sin
    qa_ref[0] = (qn * scale).astype(BF16)
    kn = k * lax.rsqrt(_dot_exact_rhs(k * k, gs[:A_KV_W, :A_KV_W], 2) + LN_EPS) * gk_ref[...]
    kn = kn * cos[:, :A_KV_W] + _rope_swap(kn) * sin[:, :A_KV_W]
    ones = jnp.ones((VT_ROWS - HEAD_DIM, x.shape[0]), F32)
    vt = v.T
    for j in range(A_KV_HEADS):
        ka_ref[0, j] = kn[:, j * HEAD_DIM:(j + 1) * HEAD_DIM].astype(BF16)
        va_ref[0, j] = jnp.concatenate([vt[j * HEAD_DIM:(j + 1) * HEAD_DIM], ones], axis=0).astype(BF16)

    zb = _dot(h, wb_ref[...])
    qb_ref[0] = (zb[:, :B_W] * scale).astype(BF16)
    kbv = zb[:, B_W:2 * B_W]
    vt = zb[:, 2 * B_W:].T
    for j in range(B_HEADS):
        kb_ref[0, j] = kbv[:, j * HEAD_DIM:(j + 1) * HEAD_DIM].astype(BF16)
        vb_ref[0, j] = jnp.concatenate([vt[j * HEAD_DIM:(j + 1) * HEAD_DIM], ones], axis=0).astype(BF16)

    zc = _dot(h, wc_ref[...])
    qkvc_ref[0] = zc[:, :3 * C_W]
    gatec_ref[0] = zc[:, 3 * C_W:]
    ab_ref[0] = _dot(h, wab_ref[...])


def _in_projection(xa, mods, wa, wb, wc, wab, gq, gk, cos, sin, gs, n_lat, tm):
    b, t, d = xa.shape
    nl = n_lat // tm
    row = lambda bi, i: (bi, i, 0)
    const = lambda bi, i: (0, 0)
    outs = [
        (jax.ShapeDtypeStruct((b, t, A_Q_W), BF16), pl.BlockSpec((1, tm, A_Q_W), row)),
        (jax.ShapeDtypeStruct((b, A_KV_HEADS, t, HEAD_DIM), BF16),
         pl.BlockSpec((1, A_KV_HEADS, tm, HEAD_DIM), lambda bi, i: (bi, 0, i, 0))),
        (jax.ShapeDtypeStruct((b, A_KV_HEADS, VT_ROWS, t), BF16),
         pl.BlockSpec((1, A_KV_HEADS, VT_ROWS, tm), lambda bi, i: (bi, 0, 0, i))),
        (jax.ShapeDtypeStruct((b, t, B_W), BF16), pl.BlockSpec((1, tm, B_W), row)),
        (jax.ShapeDtypeStruct((b, B_HEADS, t, HEAD_DIM), BF16),
         pl.BlockSpec((1, B_HEADS, tm, HEAD_DIM), lambda bi, i: (bi, 0, i, 0))),
        (jax.ShapeDtypeStruct((b, B_HEADS, VT_ROWS, t), BF16),
         pl.BlockSpec((1, B_HEADS, VT_ROWS, tm), lambda bi, i: (bi, 0, 0, i))),
        (jax.ShapeDtypeStruct((b, t, 3 * C_W), F32), pl.BlockSpec((1, tm, 3 * C_W), row)),
        (jax.ShapeDtypeStruct((b, t, C_W), F32), pl.BlockSpec((1, tm, C_W), row)),
        (jax.ShapeDtypeStruct((b, t, LANE), F32), pl.BlockSpec((1, tm, LANE), row)),
    ]
    return pl.pallas_call(
        _inproj_kernel,
        grid=(b, t // tm),
        in_specs=[pl.BlockSpec((1, tm, d), row),
                  pl.BlockSpec((1, 1, N_MOD, d), lambda bi, i: (bi, (i >= nl).astype(jnp.int32), 0, 0)),
                  pl.BlockSpec(wa.shape, const), pl.BlockSpec(wb.shape, const),
                  pl.BlockSpec(wc.shape, const), pl.BlockSpec(wab.shape, const),
                  pl.BlockSpec(gq.shape, const), pl.BlockSpec(gk.shape, const),
                  pl.BlockSpec((tm, A_Q_W), lambda bi, i: (i, 0)),
                  pl.BlockSpec((tm, A_Q_W), lambda bi, i: (i, 0)),
                  pl.BlockSpec(gs.shape, const)],
        out_specs=[o[1] for o in outs],
        out_shape=[o[0] for o in outs],
        compiler_params=_cparams(("parallel", "parallel")),
        name="in_projection",
    )(xa, mods, wa, wb, wc, wab, gq, gk, cos, sin, gs)


def _gqa_kernel(q_ref, k_ref, v_ref, o_ref, *, tq, ck, n_lat, n_tot):
    i = pl.program_id(1)
    g = A_HEADS // A_KV_HEADS

    def attend(key_lo, key_hi, n_split):
        qs = [jnp.concatenate([q_ref[0, :, (kh * g + j) * HEAD_DIM:(kh * g + j + 1) * HEAD_DIM] for j in range(g)],
                              axis=0) for kh in range(A_KV_HEADS)]
        step = (key_hi - key_lo) // n_split
        units = [(kh, key_lo + r * step, key_lo + (r + 1) * step) for r in range(n_split) for kh in range(A_KV_HEADS)]
        ss, ms, ps, oes = {}, {}, {}, {}
        for n in range(len(units) + 2):
            if n < len(units):
                kh, lo, hi = units[n]
                ss[n] = _dot_nt(k_ref[0, kh, lo:hi, :], qs[kh])
            if 0 <= n - 1 < len(units):
                ms[n - 1] = jnp.max(ss[n - 1], axis=0, keepdims=True)
                ps[n - 1] = jnp.exp2((ss.pop(n - 1) - ms[n - 1]).astype(BF16))
            if 0 <= n - 2 < len(units):
                kh, lo, hi = units[n - 2]
                oes[n - 2] = _dot(v_ref[0, kh, :, lo:hi], ps.pop(n - 2))
        for kh in range(A_KV_HEADS):
            mine = [n for n, u in enumerate(units) if u[0] == kh]
            m = functools.reduce(jnp.maximum, [ms[n] for n in mine])
            oe = sum(oes[n] * jnp.exp2(ms[n] - m) for n in mine)
            o = (oe[:HEAD_DIM] / oe[HEAD_DIM:HEAD_DIM + 1]).T.astype(BF16)
            for j in range(g):
                o_ref[0, :, (kh * g + j) * HEAD_DIM:(kh * g + j + 1) * HEAD_DIM] = o[j * tq:(j + 1) * tq]

    @pl.when(i < n_lat // tq)
    def _():
        attend(0, n_tot, ck)

    @pl.when(i >= n_lat // tq)
    def _():
        attend(n_lat, n_tot, 1)


def _gqa(qa, ka, va, n_lat, tq, ck):
    b, t, _ = qa.shape
    return pl.pallas_call(
        functools.partial(_gqa_kernel, tq=tq, ck=ck, n_lat=n_lat, n_tot=t),
        grid=(b, t // tq),
        in_specs=[pl.BlockSpec((1, tq, A_Q_W), lambda bi, i: (bi, i, 0)),
                  pl.BlockSpec((1, A_KV_HEADS, t, HEAD_DIM), lambda bi, i: (bi, 0, 0, 0)),
                  pl.BlockSpec((1, A_KV_HEADS, VT_ROWS, t), lambda bi, i: (bi, 0, 0, 0))],
        out_specs=pl.BlockSpec((1, tq, A_Q_W), lambda bi, i: (bi, i, 0)),
        out_shape=jax.ShapeDtypeStruct((b, t, A_Q_W), BF16),
        compiler_params=_cparams(("parallel", "parallel")),
        name="gqa",
    )(qa, ka, va)


def _na_window_start(r0, rows):
    return np.clip(r0 - NA_ROWS // 2, 0, rows - NA_ROWS - 1) // 2 * 2


def _na_classes(n_lat, n_tot):
    rows = n_lat // GRID_W
    assert rows >= NA_WIN and rows % 2 == 0
    geoms, table = [], []
    for r0 in range(0, rows, 2):
        geom = (int(_na_window_start(r0, rows)) - r0,) + tuple(
            int(np.clip(r0 + qr - NA_ROWS // 2, 0, rows - NA_ROWS)) - r0 for qr in range(2))
        if geom not in geoms:
            geoms.append(geom)
        table.append(geoms.index(geom))
    table += [len(geoms)] * ((n_tot - n_lat) // (2 * GRID_W))
    return geoms, np.asarray(table, np.int32)


def _na_bias_slabs(rpb, geoms):
    n_head = rpb.shape[0]
    cj = np.arange(GRID_W)[:, None]
    c = np.arange(GRID_W)[None, :]
    col_start = np.clip(c - NA_COLS // 2, 0, GRID_W - NA_COLS)
    col_ok = (cj >= col_start) & (cj < col_start + NA_COLS)
    col_off = np.clip(cj - c + (NA_COLS - 1), 0, 2 * NA_COLS - 2)
    spread = np.zeros((2 * NA_COLS - 1, GRID_W * GRID_W), np.float32)
    spread[col_off.reshape(-1), np.arange(GRID_W * GRID_W)] = 1.0
    kr = np.arange(NA_WIN)[:, None]
    qr = np.arange(2)[None, :]
    pick = np.zeros((len(geoms), NA_WIN, 2, 2 * NA_ROWS - 1), np.float32)
    ok = np.zeros((len(geoms), 1, NA_WIN, 2, GRID_W, GRID_W), bool)
    for n, (a, b0, b1) in enumerate(geoms):
        first = np.where(qr == 0, b0, b1)
        row_ok = (a + kr >= first) & (a + kr < first + NA_ROWS)
        row_off = np.clip(a + kr - qr + (NA_ROWS - 1), 0, 2 * NA_ROWS - 2)
        pick[n, kr, qr, row_off] = 1.0
        ok[n, 0] = row_ok[:, :, None, None] & col_ok[None, None]
    rows = jnp.einsum("nkqo,hoc->nhkqc", jnp.asarray(pick), rpb * LOG2E, precision=HIGHEST)
    full = jnp.dot(rows, jnp.asarray(spread), precision=HIGHEST)
    full = full.reshape(len(geoms), n_head, NA_WIN, 2, GRID_W, GRID_W)
    slabs = jnp.where(jnp.asarray(ok), full, NEG_BIG).transpose(0, 1, 2, 4, 3, 5)
    slabs = slabs.reshape(len(geoms), n_head, NA_WIN * GRID_W, 2 * GRID_W)
    return jnp.concatenate([slabs, jnp.full_like(slabs[:1], NEG_BIG)], axis=0).astype(F32)


def _na_kernel(cls_ref, q_ref, k_ref, vt_ref, bias_ref, o_ref, *, n_lat, n_tot):
    del cls_ref
    rows = n_lat // GRID_W
    r0 = 2 * pl.program_id(1)
    start = jnp.clip(r0 - NA_ROWS // 2, 0, rows - NA_ROWS - 1) // 2 * 2
    nb = pl.ds(pl.multiple_of(start * GRID_W, 2 * GRID_W), NA_WIN * GRID_W)
    hs = [slice(h * HEAD_DIM, (h + 1) * HEAD_DIM) for h in range(B_HEADS)]
    qs = [q_ref[0, :, hs[h]] for h in range(B_HEADS)]
    s_nb = [_dot_nt(k_ref[0, h, nb, :], qs[h]) + bias_ref[0, h] for h in range(B_HEADS)]
    s_cx = [_dot_nt(k_ref[0, h, n_lat:n_tot, :], qs[h]) for h in range(B_HEADS)]
    ms = [jnp.maximum(jnp.max(a, axis=0, keepdims=True), jnp.max(c, axis=0, keepdims=True))
          for a, c in zip(s_nb, s_cx)]
    p_nb = [jnp.exp2((a - m).astype(BF16)) for a, m in zip(s_nb, ms)]
    p_cx = [jnp.exp2((c - m).astype(BF16)) for c, m in zip(s_cx, ms)]
    for h in range(B_HEADS):
        oe = _dot(vt_ref[0, h, :, nb], p_nb[h]) + _dot(vt_ref[0, h, :, n_lat:n_tot], p_cx[h])
        o_ref[0, :, hs[h]] = (oe[:HEAD_DIM] / oe[HEAD_DIM:HEAD_DIM + 1]).T.astype(BF16)


def _neighborhood(qb, kb, vbt, slabs, table, n_lat):
    b, t, _ = qb.shape
    tq = 2 * GRID_W
    grid_spec = pltpu.PrefetchScalarGridSpec(
        num_scalar_prefetch=1,
        grid=(b, t // tq),
        in_specs=[pl.BlockSpec((1, tq, B_W), lambda bi, pi, cls: (bi, pi, 0)),
                  pl.BlockSpec((1, B_HEADS, t, HEAD_DIM), lambda bi, pi, cls: (bi, 0, 0, 0)),
                  pl.BlockSpec((1, B_HEADS, VT_ROWS, t), lambda bi, pi, cls: (bi, 0, 0, 0)),
                  pl.BlockSpec((1,) + slabs.shape[1:], lambda bi, pi, cls: (cls[pi], 0, 0, 0))],
        out_specs=pl.BlockSpec((1, tq, B_W), lambda bi, pi, cls: (bi, pi, 0)),
    )
    return pl.pallas_call(
        functools.partial(_na_kernel, n_lat=n_lat, n_tot=t),
        grid_spec=grid_spec,
        out_shape=jax.ShapeDtypeStruct((b, t, B_W), BF16),
        compiler_params=_cparams(("parallel", "parallel")),
        name="neighborhood",
    )(jnp.asarray(table), qb, kb, vbt, slabs)


GC_LANE, GL_LANE, BETA_LANE = 0, 2 * C_HEADS, 4 * C_HEADS


def _gdn_prep_kernel(x_ref, prev_ref, next_ref, ab_ref, cw_ref, alog_ref, dtb_ref, gs_ref,
                     q_ref, k_ref, v_ref, comp_ref, *, tm, n_lat, n_tot):
    i = pl.program_id(1)
    nl = n_lat // tm
    nt = n_tot // tm
    first = jnp.logical_or(i == 0, i == nl)
    last = jnp.logical_or(i == nl - 1, i == nt - 1)
    x = x_ref[0]
    pv = jnp.where(first, 0.0, prev_ref[0])
    nx = jnp.where(last, 0.0, next_ref[0])
    row = lax.broadcasted_iota(jnp.int32, (tm, 1), 0)
    xm1 = jnp.where(row == 0, pv[7:8], pltpu.roll(x, 1, 0))
    xm2 = jnp.where(row == 0, pv[6:7], jnp.where(row == 1, pv[7:8], pltpu.roll(x, 2, 0)))
    xp1 = jnp.where(row == tm - 1, nx[0:1], pltpu.roll(x, tm - 1, 0))
    cw = cw_ref[...]
    y = _silu(xm2 * cw[0:1] + xm1 * cw[1:2] + x * cw[2:3] + xp1 * cw[3:4])
    q = y[:, :C_W]
    k = y[:, C_W:2 * C_W]
    gs = gs_ref[...]
    q_ref[0] = q * lax.rsqrt(_dot_exact_rhs(q * q, gs, 2) + LN_EPS) * (HEAD_DIM ** -0.5)
    k_ref[0] = k * lax.rsqrt(_dot_exact_rhs(k * k, gs, 2) + LN_EPS)
    v_ref[0] = y[:, 2 * C_W:]
    ab = ab_ref[0]
    z = ab + dtb_ref[...]
    softplus = jnp.maximum(z, 0.0) + jnp.log(1.0 + jnp.exp(-jnp.abs(z)))
    lane = lax.broadcasted_iota(jnp.int32, ab.shape, 1)
    gb = jnp.where(lane < 2 * C_HEADS, -jnp.exp(alog_ref[...]) * softplus, _sigmoid(ab))
    ii = lax.broadcasted_iota(jnp.int32, (tm, tm), 0)
    jj = lax.broadcasted_iota(jnp.int32, (tm, tm), 1)
    same = (ii // CHUNK) == (jj // CHUNK)
    gb3 = _split_bf16(gb, 3)

    def masked_sum(mask):
        mb = jnp.where(mask, 1.0, 0.0).astype(BF16)
        return sum(_dot(mb, piece) for piece in gb3)

    prefix = masked_sum(jnp.logical_and(same, jj <= ii))
    suffix = masked_sum(jnp.logical_and(same, jj >= ii))
    total = masked_sum(same)
    comp = jnp.where(lane < C_HEADS, prefix,
                     jnp.where(lane < GL_LANE, suffix,
                               jnp.where(lane < BETA_LANE, pltpu.roll(total, GL_LANE, 1),
                                         jnp.where(lane < BETA_LANE + 2 * C_HEADS, pltpu.roll(gb, 2 * C_HEADS, 1),
                                                   0.0))))
    comp_ref[0] = comp


def _gdn_prep(qkvc, ab, conv_w, a_log, dt_bias, gs, n_lat, tm):
    b, t, _ = qkvc.shape
    c3 = 3 * C_W
    alog_row = jnp.zeros((1, LANE), F32).at[0, :2 * C_HEADS].set(a_log.reshape(-1))
    dtb_row = jnp.zeros((1, LANE), F32).at[0, :2 * C_HEADS].set(dt_bias.reshape(-1))
    row = lambda bi, i: (bi, i, 0)
    const2 = lambda bi, i: (0, 0)
    hb = tm // 8
    last8 = t // 8 - 1
    tok = jax.ShapeDtypeStruct((b, t, C_W), F32)
    return pl.pallas_call(
        functools.partial(_gdn_prep_kernel, tm=tm, n_lat=n_lat, n_tot=t),
        grid=(b, t // tm),
        in_specs=[pl.BlockSpec((1, tm, c3), row),
                  pl.BlockSpec((1, 8, c3), lambda bi, i: (bi, jnp.maximum(i * hb - 1, 0), 0)),
                  pl.BlockSpec((1, 8, c3), lambda bi, i: (bi, jnp.minimum((i + 1) * hb, last8), 0)),
                  pl.BlockSpec((1, tm, LANE), row),
                  pl.BlockSpec((CONV_K, c3), const2),
                  pl.BlockSpec((1, LANE), const2), pl.BlockSpec((1, LANE), const2),
                  pl.BlockSpec(gs.shape, const2)],
        out_specs=[pl.BlockSpec((1, tm, C_W), row)] * 3 + [pl.BlockSpec((1, tm, LANE), row)],
        out_shape=[tok, tok, tok, jax.ShapeDtypeStruct((b, t, LANE), F32)],
        compiler_params=_cparams(("parallel", "parallel")),
        name="gdn_prep",
    )(qkvc, qkvc, qkvc, ab, conv_w, alog_row, dtb_row, gs)


def _unit_tri_inverses(lms, eye, bd16, bd32):
    def mm(a, b_):
        return _dot(a.astype(BF16), b_.astype(BF16))

    lds = [jnp.where(bd16, lm, 0.0) for lm in lms]
    xs = [eye - ld for ld in lds]
    lds = [ld.astype(BF16) for ld in lds]
    pws = [_dot(ld, ld).astype(BF16) for ld in lds]
    for step in range(3):
        xs = [x + mm(x, pw) for x, pw in zip(xs, pws)]
        if step < 2:
            pws = [_dot(pw, pw).astype(BF16) for pw in pws]
    for level in (lambda lm: jnp.where(jnp.logical_and(bd32, jnp.logical_not(bd16)), lm, 0.0),
                  lambda lm: jnp.where(bd32, 0.0, lm)):
        xbs = [x.astype(BF16) for x in xs]
        ys = [_dot(level(lm).astype(BF16), xb) for lm, xb in zip(lms, xbs)]
        xs = [x - _dot(xb, y.astype(BF16)) for x, xb, y in zip(xs, xbs, ys)]
    return xs


def _gdn_local_kernel(q_ref, k_ref, v_ref, comp_ref, ex_ref, sel_ref, u_ref, w_ref, kd_ref, qd_ref, in_ref):
    n = q_ref.shape[1]
    hp = 2 * HEAD_DIM
    n_pr = q_ref.shape[2] // hp
    ii = lax.broadcasted_iota(jnp.int32, (n, n), 0)
    jj = lax.broadcasted_iota(jnp.int32, (n, n), 1)
    same = (ii // CHUNK) == (jj // CHUNK)
    eye = (ii == jj).astype(F32)
    bd16 = (ii // 16) == (jj // 16)
    bd32 = (ii // 32) == (jj // 32)
    lane = lax.broadcasted_iota(jnp.int32, (1, hp), 1)
    lane2 = lax.broadcasted_iota(jnp.int32, (1, 2 * hp), 1)
    incls = [jnp.logical_and(same, ii >= jj), jnp.logical_and(same, ii <= jj)]
    stricts = [jnp.logical_and(same, ii > jj), jnp.logical_and(same, ii < jj)]
    lms, intras, rhss = [], [], []
    for pr in range(n_pr):
        ls = slice(pr * hp, (pr + 1) * hp)
        q = q_ref[0, :, ls]
        k = k_ref[0, :, ls]
        v = v_ref[0, :, ls]
        e = _dot_exact_rhs(comp_ref[0], ex_ref[pr])
        kh = [jnp.where((lane // HEAD_DIM) == hh, k, 0.0).astype(BF16) for hh in range(2)]
        for p in range(2):
            gc = e[:, p * hp:(p + 1) * hp]
            gl = e[:, (2 + p) * hp:(3 + p) * hp]
            beta = e[:, (4 + p) * hp:(5 + p) * hp]
            eg = jnp.exp(gc)
            kb = k * beta
            kd_ref[p, 0, :, ls] = (k * jnp.exp(gl - gc)).astype(BF16)
            qd_ref[p, 0, :, ls] = (q * eg).astype(BF16)
            rhs = jnp.concatenate([v * beta, kb * eg], axis=1)
            kbq = jnp.concatenate([kb, q], axis=0).astype(BF16)
            for hh in range(2):
                gcb = jnp.broadcast_to(gc[:, hh * HEAD_DIM:hh * HEAD_DIM + 1], (n, n))
                decay = jnp.exp(jnp.where(incls[p], gcb - gcb.T, NEG_BIG))
                aq = _dot_nt(kbq, kh[hh])
                lms.append(jnp.where(stricts[p], aq[:n] * decay, 0.0))
                intras.append(jnp.where(incls[p], aq[n:] * decay, 0.0).astype(BF16))
                rhss.append(jnp.where(((lane2 % hp) // HEAD_DIM) == hh, rhs, 0.0).astype(BF16))
    tmats = _unit_tri_inverses(lms, eye, bd16, bd32)
    for pr in range(n_pr):
        ls = slice(pr * hp, (pr + 1) * hp)
        for p in range(2):
            at = 4 * pr + 2 * p
            uw = sum(_dot(tmats[at + hh].astype(BF16), rhss[at + hh]) for hh in range(2))
            intra2 = sum(_dot(intras[at + hh], sel_ref[hh]) for hh in range(2))
            u_ref[p, 0, :, ls] = uw[:, :hp].astype(BF16)
            w_ref[p, 0, :, ls] = uw[:, hp:].astype(BF16)
            in_ref[p, 0, :, ls] = intra2.astype(BF16)


def _gdn_local(q, k, v, comp, blk):
    b, t, _ = q.shape
    hp = 2 * HEAD_DIM
    n_pair = C_W // hp
    ex = np.zeros((n_pair, LANE, 6 * hp), np.float32)
    for pr in range(n_pair):
        for p in range(2):
            for hh in range(2):
                src = p * C_HEADS + pr * 2 + hh
                for kind, base in enumerate((GC_LANE, GL_LANE, BETA_LANE)):
                    col = (2 * kind + p) * hp + hh * HEAD_DIM
                    ex[pr, base + src, col:col + HEAD_DIM] = 1.0
    sel = np.zeros((2, blk, hp), np.float32)
    for hh in range(2):
        sel[hh, np.arange(blk), hh * HEAD_DIM + np.arange(blk) % CHUNK] = 1.0
    gp = LOCAL_PAIRS
    tok = lambda bi, pg, i: (bi, i, pg)
    dirs = lambda bi, pg, i: (0, bi, i, pg)
    out = jax.ShapeDtypeStruct((2, b, t, C_W), BF16)
    return pl.pallas_call(
        _gdn_local_kernel,
        grid=(b, n_pair // gp, t // blk),
        in_specs=[pl.BlockSpec((1, blk, gp * hp), tok)] * 3
                 + [pl.BlockSpec((1, blk, LANE), lambda bi, pg, i: (bi, i, 0)),
                    pl.BlockSpec((gp, LANE, 6 * hp), lambda bi, pg, i: (pg, 0, 0)),
                    pl.BlockSpec((2, blk, hp), lambda bi, pg, i: (0, 0, 0))],
        out_specs=[pl.BlockSpec((2, 1, blk, gp * hp), dirs)] * 5,
        out_shape=[out] * 5,
        compiler_params=_cparams(("parallel", "parallel", "parallel")),
        name="gdn_local",
    )(q, k, v, comp, jnp.asarray(ex, dtype=BF16), jnp.asarray(sel, dtype=BF16))


def _gdn_scan_kernel(*refs, ncb, nb):
    ins = (refs[0:6], refs[6:12])
    ex_ref = refs[12]
    outs = refs[13:15]
    s_ref = refs[15]
    j = pl.program_id(1)

    @pl.when(j == 0)
    def _():
        s_ref[...] = jnp.zeros_like(s_ref)

    gw = 4 * HEAD_DIM
    n_grp = C_W // gw
    lane_head = lax.broadcasted_iota(jnp.int32, (1, gw), 1) // HEAD_DIM
    rr = lax.broadcasted_iota(jnp.int32, (gw, gw), 0) // HEAD_DIM
    cc_ = lax.broadcasted_iota(jnp.int32, (gw, gw), 1) // HEAD_DIM
    blockdiag = rr == cc_
    chains = [(bi, p, g) for bi in range(nb) for p in range(2) for g in range(n_grp)]
    for cc in range(ncb):
        rows = [slice(cc * CHUNK, (cc + 1) * CHUNK), slice((ncb - 1 - cc) * CHUNK, (ncb - cc) * CHUNK)]
        sdec = {(bi, p): jnp.exp(_dot_exact_rhs(ins[p][5][bi, rows[p].start:rows[p].start + 8, :], ex_ref[...]))
                for bi in range(nb) for p in range(2)}
        lanes = [slice(g * gw, (g + 1) * gw) for g in range(n_grp)]
        ss = [s_ref[bi, p, g] for bi, p, g in chains]
        sbs = [s.astype(BF16) for s in ss]
        v_news = [ins[p][0][0, bi, rows[p], lanes[g]].astype(F32) - _dot(ins[p][1][0, bi, rows[p], lanes[g]], sb)
                  for (bi, p, g), sb in zip(chains, sbs)]
        vstacks = [jnp.concatenate([jnp.where(lane_head == h, v_new, 0.0).astype(BF16)
                                    for h in range(gw // HEAD_DIM)], axis=0) for v_new in v_news]
        for (bi, p, g), s, sb, v_new, vstack in zip(chains, ss, sbs, v_news, vstacks):
            o = _dot(ins[p][3][0, bi, rows[p], lanes[g]], sb) + _dot(ins[p][4][0, bi, rows[p], lanes[g]], vstack)
            upd = _dot_tn(ins[p][2][0, bi, rows[p], lanes[g]], v_new.astype(BF16))
            decay = sdec[bi, p][0:1, p * C_W + g * gw:p * C_W + (g + 1) * gw]
            s_ref[bi, p, g] = s * decay + jnp.where(blockdiag, upd, 0.0)
            outs[p][bi, rows[p], lanes[g]] = o


def _gdn_scan(u, w, kd, qd, intra, comp, n_lat, blk):
    _, b, t, _ = u.shape
    nl = n_lat // blk
    nx = (t - n_lat) // blk
    ncb = blk // CHUNK
    ex = np.zeros((LANE, 2 * C_W), np.float32)
    for p in range(2):
        for h in range(C_HEADS):
            ex[GL_LANE + p * C_HEADS + h, p * C_W + h * HEAD_DIM:p * C_W + (h + 1) * HEAD_DIM] = 1.0

    def blk0(j):
        return jnp.where(j < nx, nl + j, j - nx)

    def blk1(j):
        return jnp.where(j < nx, nl + nx - 1 - j, nl - 1 - (j - nx))

    nb = SCAN_BATCH if b % SCAN_BATCH == 0 else 1
    specs = []
    for p, bf in enumerate((blk0, blk1)):
        specs += [pl.BlockSpec((1, nb, blk, C_W), lambda bi, j, p=p, bf=bf: (p, bi, bf(j), 0))] * 5
        specs += [pl.BlockSpec((nb, blk, LANE), lambda bi, j, bf=bf: (bi, bf(j), 0))]
    specs += [pl.BlockSpec((LANE, 2 * C_W), lambda bi, j: (0, 0))]
    out = jax.ShapeDtypeStruct((b, t, C_W), F32)
    gw = 4 * HEAD_DIM
    return pl.pallas_call(
        functools.partial(_gdn_scan_kernel, ncb=ncb, nb=nb),
        grid=(b // nb, nl + nx),
        in_specs=specs,
        out_specs=[pl.BlockSpec((nb, blk, C_W), lambda bi, j: (bi, blk0(j), 0)),
                   pl.BlockSpec((nb, blk, C_W), lambda bi, j: (bi, blk1(j), 0))],
        out_shape=[out, out],
        scratch_shapes=[pltpu.VMEM((nb, 2, C_W // gw, gw, gw), F32)],
        compiler_params=_cparams(("parallel", "arbitrary")),
        name="gdn_scan",
    )(u, w, kd, qd, intra, comp, u, w, kd, qd, intra, comp, jnp.asarray(ex, dtype=BF16))


def _merge_kernel(x_ref, mod_ref, oa_ref, ob_ref, oc0_ref, oc1_ref, gate_ref, wmg_ref, wbr_ref, wout_ref, og_ref,
                  gs_ref, lng_ref, lnb_ref, wr_ref, o_ref, h_ref, aff_ref, *, alpha, n_exp):
    x = x_ref[0]
    d = x.shape[-1]
    mod = mod_ref[0, 0]
    h = (_ln(x) * (1.0 + mod[1:2]) + mod[0:1]).astype(BF16)
    gates = _sigmoid(_dot(h, wmg_ref[...]))
    oc = oc0_ref[0] + oc1_ref[0]
    ocn = oc * lax.rsqrt(_dot_exact_rhs(oc * oc, gs_ref[...], 2) + LN_EPS) * og_ref[...] * _silu(gate_ref[0])
    m = gates[:, :d] * _dot(oa_ref[0], wbr_ref[0])
    m = m + gates[:, d:2 * d] * _dot(ob_ref[0], wbr_ref[1])
    m = m + gates[:, 2 * d:] * _dot(ocn.astype(BF16), wbr_ref[2])
    mix = _dot(m.astype(BF16), wout_ref[...])
    y = alpha * x + mod[2:3] * mix
    x1 = _ln(y) * lng_ref[...] + lnb_ref[...]
    o_ref[0] = x1
    h2 = _ln(x1) * (1.0 + mod[4:5]) + mod[3:4]
    h_ref[0] = h2.astype(BF16)
    hp = _split_bf16(h2, 3)
    logits = sum(_dot(hp[i], wr_ref[j]) for i, j in ((2, 0), (1, 1), (0, 2), (1, 0), (0, 1), (0, 0)))
    lane = lax.broadcasted_iota(jnp.int32, logits.shape, 1)
    logits = jnp.where(lane < n_exp, logits, NEG_BIG)
    ex = jnp.exp(logits - jnp.max(logits, axis=-1, keepdims=True))
    aff_ref[0] = ex / jnp.sum(ex, axis=-1, keepdims=True)


def _merge(xa, mods, oa, ob, oc0, oc1, gate_c, wmg, wbr, wout, og, gs, lng, lnb, w_router, n_lat, tm, alpha):
    b, t, d = xa.shape
    nl = n_lat // tm
    n_exp = w_router.shape[-1]
    wr_pad = jnp.pad(w_router, ((0, 0), (0, LANE - n_exp)))
    pieces, rest = [], wr_pad
    for _ in range(3):
        piece = lax.reduce_precision(rest, exponent_bits=8, mantissa_bits=7)
        pieces.append(piece.astype(BF16))
        rest = rest - piece
    wr = jnp.stack(pieces)
    row = lambda bi, i: (bi, i, 0)
    const2 = lambda bi, i: (0, 0)
    return pl.pallas_call(
        functools.partial(_merge_kernel, alpha=alpha, n_exp=n_exp),
        grid=(b, t // tm),
        in_specs=[pl.BlockSpec((1, tm, d), row),
                  pl.BlockSpec((1, 1, N_MOD, d), lambda bi, i: (bi, (i >= nl).astype(jnp.int32), 0, 0)),
                  pl.BlockSpec((1, tm, BRANCH_W), row), pl.BlockSpec((1, tm, BRANCH_W), row),
                  pl.BlockSpec((1, tm, BRANCH_W), row), pl.BlockSpec((1, tm, BRANCH_W), row),
                  pl.BlockSpec((1, tm, BRANCH_W), row),
                  pl.BlockSpec(wmg.shape, const2),
                  pl.BlockSpec(wbr.shape, lambda bi, i: (0, 0, 0)),
                  pl.BlockSpec(wout.shape, const2),
                  pl.BlockSpec(og.shape, const2), pl.BlockSpec(gs.shape, const2),
                  pl.BlockSpec(lng.shape, const2), pl.BlockSpec(lnb.shape, const2),
                  pl.BlockSpec(wr.shape, lambda bi, i: (0, 0, 0))],
        out_specs=[pl.BlockSpec((1, tm, d), row), pl.BlockSpec((1, tm, d), row), pl.BlockSpec((1, tm, LANE), row)],
        out_shape=[jax.ShapeDtypeStruct((b, t, d), F32), jax.ShapeDtypeStruct((b, t, d), BF16),
                   jax.ShapeDtypeStruct((b, t, LANE), F32)],
        compiler_params=_cparams(("parallel", "parallel")),
        name="merge",
    )(xa, mods, oa, ob, oc0, oc1, gate_c, wmg, wbr, wout, og, gs, lng, lnb, wr)


def _ffn_kernel(x_ref, wg_ref, wu_ref, wd_ref, gate_ref, o_ref, acc_ref):
    f = pl.program_id(2)

    @pl.when(f == 0)
    def _():
        acc_ref[...] = jnp.zeros_like(acc_ref)

    x = x_ref[0]
    g = _dot(x, wg_ref[0, 0].astype(BF16))
    u = _dot(x, wu_ref[0, 0].astype(BF16))
    a = (_silu(g) * u).astype(BF16)
    acc_ref[...] += _dot(a, wd_ref[0, 0].astype(BF16))

    @pl.when(f == pl.num_programs(2) - 1)
    def _():
        o_ref[0] = (acc_ref[...] * gate_ref[0]).astype(BF16)


def _expert_ffn(xs, gate, w_gate_up, w_down, layer, tm, tf):
    e, r, d = xs.shape
    f = w_down.shape[2]
    nf = f // tf
    return pl.pallas_call(
        _ffn_kernel,
        grid=(e, r // tm, nf),
        in_specs=[pl.BlockSpec((1, tm, d), lambda ei, i, fi: (ei, i, 0)),
                  pl.BlockSpec((1, 1, d, tf), lambda ei, i, fi: (layer, ei, 0, fi)),
                  pl.BlockSpec((1, 1, d, tf), lambda ei, i, fi: (layer, ei, 0, nf + fi)),
                  pl.BlockSpec((1, 1, tf, d), lambda ei, i, fi: (layer, ei, fi, 0)),
                  pl.BlockSpec((1, tm, 1), lambda ei, i, fi: (ei, i, 0))],
        out_specs=pl.BlockSpec((1, tm, d), lambda ei, i, fi: (ei, i, 0)),
        out_shape=jax.ShapeDtypeStruct((e, r, d), BF16),
        scratch_shapes=[pltpu.VMEM((tm, d), F32)],
        compiler_params=_cparams(("parallel", "parallel", "arbitrary")),
        name="expert_ffn",
    )(xs, w_gate_up, w_gate_up, w_down, gate)


VALID, FIRST, LAST = 1, 2, 4


def _combine_plan(tok_of_pair, n_tok, tm, rwin):
    bsz, n_pair = tok_of_pair.shape
    nt = n_tok // tm
    n_win = n_pair // rwin
    n_item = nt + n_win
    order = jnp.argsort(tok_of_pair, axis=1)
    tok = jnp.take_along_axis(tok_of_pair, order, axis=1)
    bounds = jnp.arange(nt + 1, dtype=jnp.int32) * tm
    off = jnp.sum(tok[:, None, :] < bounds[None, :, None], axis=-1, dtype=jnp.int32)
    first = jnp.minimum(off[:, :-1] // rwin, n_win - 1)
    last = jnp.minimum(jnp.maximum(off[:, 1:] - 1, off[:, :-1]) // rwin, n_win - 1)
    n_items = last - first + 1
    cum = jnp.cumsum(n_items, axis=1)
    k = jnp.arange(n_item, dtype=jnp.int32)
    tile = jnp.sum(cum[:, None, :] <= k[None, :, None], axis=-1, dtype=jnp.int32)
    valid = tile < nt
    tile = jnp.minimum(tile, nt - 1)
    end = jnp.take_along_axis(cum, tile, axis=1)
    start = end - jnp.take_along_axis(n_items, tile, axis=1)
    win = jnp.where(valid, jnp.take_along_axis(first, tile, axis=1) + (k[None] - start),
                    jnp.take_along_axis(last, tile, axis=1))
    flags = (valid * VALID + (valid & (k[None] == start)) * FIRST + (valid & (k[None] == end - 1)) * LAST)
    return order, tok, tile, win.astype(jnp.int32), flags.astype(jnp.int32)


def _combine_kernel(tile_ref, win_ref, flag_ref, y_ref, tok_ref, x_ref, mod_ref, lng_ref, lnb_ref, o_ref, acc_ref, *,
                    alpha, tm):
    b = pl.program_id(0)
    k = pl.program_id(1)
    flags = flag_ref[b, k]

    @pl.when((flags & FIRST) != 0)
    def _():
        acc_ref[...] = jnp.zeros_like(acc_ref)

    @pl.when((flags & VALID) != 0)
    def _():
        tok0 = tile_ref[b, k] * tm
        rows = lax.broadcasted_iota(jnp.int32, (tm, tok_ref.shape[-1]), 0) + tok0
        pick = jnp.where(rows == tok_ref[0, 0], 1.0, 0.0).astype(BF16)
        acc_ref[...] += _dot(pick, y_ref[0])

    @pl.when((flags & LAST) != 0)
    def _():
        mod = mod_ref[0, 0]
        y = alpha * x_ref[0] + mod[5:6] * acc_ref[...]
        o_ref[0] = _ln(y) * lng_ref[...] + lnb_ref[...]


def _combine(y_sorted, tok_sorted, tile, win, flags, xa, mods, lng, lnb, n_lat, tm, rwin, alpha):
    b, t, d = xa.shape
    nl = n_lat // tm
    n_item = tile.shape[1]
    grid_spec = pltpu.PrefetchScalarGridSpec(
        num_scalar_prefetch=3,
        grid=(b, n_item),
        in_specs=[pl.BlockSpec((1, rwin, d), lambda bi, k, tl, wn, fl: (bi, wn[bi, k], 0)),
                  pl.BlockSpec((1, 1, 1, rwin), lambda bi, k, tl, wn, fl: (bi, wn[bi, k], 0, 0)),
                  pl.BlockSpec((1, tm, d), lambda bi, k, tl, wn, fl: (bi, tl[bi, k], 0)),
                  pl.BlockSpec((1, 1, N_MOD, d),
                               lambda bi, k, tl, wn, fl: (bi, (tl[bi, k] >= nl).astype(jnp.int32), 0, 0)),
                  pl.BlockSpec(lng.shape, lambda bi, k, tl, wn, fl: (0, 0)),
                  pl.BlockSpec(lnb.shape, lambda bi, k, tl, wn, fl: (0, 0))],
        out_specs=pl.BlockSpec((1, tm, d), lambda bi, k, tl, wn, fl: (bi, tl[bi, k], 0)),
        scratch_shapes=[pltpu.VMEM((tm, d), F32)],
    )
    return pl.pallas_call(
        functools.partial(_combine_kernel, alpha=alpha, tm=tm),
        grid_spec=grid_spec,
        out_shape=jax.ShapeDtypeStruct((b, t, d), F32),
        compiler_params=_cparams(("parallel", "arbitrary")),
        name="moe_combine",
    )(tile, win, flags, y_sorted, tok_sorted.reshape(b, -1, 1, rwin), xa, mods, lng, lnb)


def _moe(xa, h, aff, mods, n_exp, w_gate_up, w_down, lng, lnb, layer, n_lat, tm, alpha):
    b, t, d = xa.shape
    m_ctx = t - n_lat
    aff = jnp.swapaxes(aff[:, :, :n_exp], 1, 2)
    cap_l = CAPACITY * n_lat // n_exp
    cap_x = CAPACITY * m_ctx // n_exp
    gate_l, idx_l = lax.top_k(aff[:, :, :n_lat], cap_l)
    gate_x, idx_x = lax.top_k(aff[:, :, n_lat:], cap_x)
    idx = jnp.concatenate([idx_l, idx_x + n_lat], axis=-1)
    gate = jnp.concatenate([gate_l, gate_x], axis=-1)
    cap = cap_l + cap_x
    idx_e = jnp.swapaxes(idx, 0, 1)
    bidx = jnp.arange(b)[None, :, None]
    xs = h[bidx, idx_e].reshape(n_exp, b * cap, d)
    gate_e = jnp.swapaxes(gate, 0, 1).reshape(n_exp, b * cap, 1)
    rows = b * cap
    tmr = rows
    for cand in (1088, 1024, 512, 256, 128, 64, 32, 16):
        if rows % cand == 0:
            tmr = cand
            break
    tf = min(512, w_down.shape[2])
    y = _expert_ffn(xs, gate_e, w_gate_up, w_down, layer, tmr, tf)
    n_pair = n_exp * cap
    rwin = next(r for r in (512, 256, 128, 64, 32, 16, 8) if n_pair % r == 0)
    order, tok, tile, win, flags = _combine_plan(idx.reshape(b, n_pair), t, tm, rwin)
    y_sorted = y[order // cap, jnp.arange(b)[:, None] * cap + order % cap]
    return _combine(y_sorted, tok, tile, win, flags, xa, mods, lng, lnb, n_lat, tm, rwin, alpha)


def _rope_tables(n_lat, m_ctx):
    t = np.arange(n_lat)
    rows = (t // GRID_W).astype(np.float32)
    cols = (t % GRID_W).astype(np.float32)
    n_freq = HEAD_DIM // 4
    inv_freq = jnp.asarray(ROPE_THETA, F32) ** (-jnp.arange(n_freq, dtype=F32) / n_freq)
    ang_r = jnp.asarray(rows)[:, None] * inv_freq
    ang_c = jnp.asarray(cols)[:, None] * inv_freq
    cr, sr, cc, sc = jnp.cos(ang_r), jnp.sin(ang_r), jnp.cos(ang_c), jnp.sin(ang_c)
    cos = jnp.concatenate([cr, cr, cc, cc], axis=-1)
    sin = jnp.concatenate([-sr, sr, -sc, sc], axis=-1)
    cos = jnp.concatenate([cos, jnp.ones((m_ctx, HEAD_DIM), F32)], axis=0)
    sin = jnp.concatenate([sin, jnp.zeros((m_ctx, HEAD_DIM), F32)], axis=0)
    return jnp.tile(cos, (1, A_HEADS)), jnp.tile(sin, (1, A_HEADS))


def kernel(x, c, ctx, c_ctx, w_mod, b_mod, w_in, qk_gain, rpb, conv_w, a_log, dt_bias, o_gain, w_branch, w_out,
           ln1_g, ln1_b, w_router, w_gate_up, w_down, ln2_g, ln2_b):
    b, n_lat, d = x.shape
    m_ctx = ctx.shape[1]
    depth = w_mod.shape[0]
    alpha = (2 * depth) ** 0.25
    tm = math.gcd(256, math.gcd(n_lat, m_ctx))
    blk = tm

    xa = jnp.concatenate([x, ctx], axis=1)
    n_rows = -(-(b + 1) // 8) * 8
    cc = jnp.zeros((n_rows, d), F32).at[:b].set(c).at[b].set(c_ctx)
    mod_all = _modulation(cc, w_mod, b_mod).reshape(depth, n_rows, N_MOD, d)
    cos, sin = _rope_tables(n_lat, m_ctx)
    gs_mean = _group_sum_matrix(A_Q_W, 1.0 / HEAD_DIM)
    gs_sum = _group_sum_matrix(C_W, 1.0)
    na_geoms, na_table = _na_classes(n_lat, n_lat + m_ctx)

    offs = np.cumsum((0, A_Q_W, A_KV_W, A_KV_W, B_W, B_W, B_W, 3 * C_W, C_W, 2 * C_HEADS, 2 * C_HEADS, N_BRANCH * d))
    for layer in range(depth):
        ml = mod_all[layer]
        mods = jnp.stack([ml[:b], jnp.broadcast_to(ml[b][None], (b, N_MOD, d))], axis=1)
        wl = w_in[layer]
        wa = wl[:, offs[0]:offs[3]].astype(BF16)
        wb = wl[:, offs[3]:offs[6]].astype(BF16)
        wc = wl[:, offs[6]:offs[8]].astype(BF16)
        wab = jnp.pad(wl[:, offs[8]:offs[10]], ((0, 0), (0, LANE - 4 * C_HEADS))).astype(BF16)
        wmg = wl[:, offs[10]:offs[11]].astype(BF16)
        gq = jnp.tile(qk_gain[layer, 0], A_HEADS)[None]
        gk = jnp.tile(qk_gain[layer, 1], A_KV_HEADS)[None]
        qa, ka, va, qb, kb, vb, qkvc, gate_c, ab = _in_projection(
            xa, mods, wa, wb, wc, wab, gq, gk, cos, sin, gs_mean, n_lat, tm)
        oa = _gqa(qa, ka, va, n_lat, min(GQA_TQ, tm), 2 if (n_lat + m_ctx) % (2 * LANE) == 0 else 1)
        ob = _neighborhood(qb, kb, vb, _na_bias_slabs(rpb[layer], na_geoms), na_table, n_lat)
        qc, kc, vc, comp = _gdn_prep(qkvc, ab, conv_w[layer], a_log[layer], dt_bias[layer], gs_sum, n_lat, tm)
        u, w, kd, qd, intra = _gdn_local(qc, kc, vc, comp, blk)
        oc0, oc1 = _gdn_scan(u, w, kd, qd, intra, comp, n_lat, blk)
        og = jnp.tile(o_gain[layer], C_HEADS)[None]
        xa, h, aff = _merge(xa, mods, oa, ob, oc0, oc1, gate_c, wmg, w_branch[layer].astype(BF16),
                            w_out[layer].astype(BF16), og, gs_mean, ln1_g[layer][None], ln1_b[layer][None],
                            w_router[layer], n_lat, tm, alpha)
        xa = _moe(xa, h, aff, mods, w_router.shape[-1], w_gate_up, w_down, ln2_g[layer][None], ln2_b[layer][None],
                  layer, n_lat, tm, alpha)
    return xa[:, :n_lat]
```

```python
import functools
import math

import numpy as np
import jax
import jax.numpy as jnp
from jax import lax
from jax.experimental import pallas as pl
from jax.experimental.pallas import tpu as pltpu

F32 = jnp.float32
BF16 = jnp.bfloat16
HIGHEST = lax.Precision.HIGHEST

GRID_W = 64
HEAD_DIM = 64
A_HEADS = 8
A_KV_HEADS = 2
B_HEADS = 8
C_HEADS = 8
NA_ROWS = 8
NA_COLS = 16
CONV_K = 4
CHUNK = 64
N_BRANCH = 3
BRANCH_W = 512
CAPACITY = 2
N_MOD = 6
LN_EPS = 1e-6
ROPE_THETA = 10000.0
NEG_BIG = -1e30
LOG2E = 1.4426950408889634

A_Q_W = A_HEADS * HEAD_DIM
A_KV_W = A_KV_HEADS * HEAD_DIM
B_W = B_HEADS * HEAD_DIM
C_W = C_HEADS * HEAD_DIM
LANE = 128
SUBLANE = 8
V7X_VMEM_BYTES = 64 * 1024 * 1024
VMEM_LIMIT = V7X_VMEM_BYTES * 7 // 8
GQA_TQ = 128
LOCAL_PAIRS = 2
SCAN_BATCH = 2
VT_ROWS = HEAD_DIM + 16
NA_WIN = NA_ROWS + 2


def _cparams(sem):
    return pltpu.CompilerParams(dimension_semantics=sem, vmem_limit_bytes=VMEM_LIMIT)


def _dot(a, b):
    return jnp.dot(a, b, preferred_element_type=F32)


def _dot_hi(a, b):
    return jnp.dot(a, b, precision=HIGHEST, preferred_element_type=F32)


def _dot_nt(a, b):
    return lax.dot_general(a, b, (((1,), (1,)), ((), ())), preferred_element_type=F32)


def _dot_tn(a, b):
    return lax.dot_general(a, b, (((0,), (0,)), ((), ())), preferred_element_type=F32)


def _split_bf16(x, pieces):
    out = []
    for _ in range(pieces - 1):
        h = x.astype(BF16)
        out.append(h)
        x = x - h.astype(F32)
    out.append(x.astype(BF16))
    return out


def _dot_exact_rhs(a, b, pieces=3):
    return sum(_dot(p, b) for p in _split_bf16(a, pieces))


def _dot_exact_lhs(a, b, pieces=3):
    return sum(_dot(a, p) for p in _split_bf16(b, pieces))


def _ln(x):
    mu = jnp.mean(x, axis=-1, keepdims=True)
    xc = x - mu
    var = jnp.mean(xc * xc, axis=-1, keepdims=True)
    return xc * lax.rsqrt(var + LN_EPS)


def _sigmoid(x):
    return 1.0 / (1.0 + jnp.exp(-x))


def _silu(x):
    return x * _sigmoid(x)


def _group_sum_matrix(width, scale):
    g = np.arange(width) // HEAD_DIM
    return jnp.asarray((g[:, None] == g[None, :]).astype(np.float32) * scale, dtype=BF16)


def _mod_kernel(c_ref, w_ref, b_ref, o_ref):
    c = c_ref[...]
    o_ref[0] = _dot_hi(_silu(c), w_ref[0]) + b_ref[0]


def _modulation(cc, w_mod, b_mod):
    depth, d, dm = w_mod.shape
    r = cc.shape[0]
    tn = min(dm, 1536)
    return pl.pallas_call(
        _mod_kernel,
        grid=(depth, dm // tn),
        in_specs=[pl.BlockSpec((r, d), lambda l, j: (0, 0)),
                  pl.BlockSpec((1, d, tn), lambda l, j: (l, 0, j)),
                  pl.BlockSpec((1, 1, tn), lambda l, j: (l, 0, j))],
        out_specs=pl.BlockSpec((1, r, tn), lambda l, j: (l, 0, j)),
        out_shape=jax.ShapeDtypeStruct((depth, r, dm), F32),
        compiler_params=_cparams(("parallel", "parallel")),
        name="modulation",
    )(cc, w_mod, b_mod.reshape(depth, 1, dm))


def _rope_swap(y):
    w = y.shape[-1]
    lane = lax.broadcasted_iota(jnp.int32, y.shape, 1)
    up = pltpu.roll(y, w - 16, 1)
    down = pltpu.roll(y, 16, 1)
    return jnp.where((lane % 32) < 16, up, down)


def _inproj_kernel(x_ref, mod_ref, wa_ref, wb_ref, wc_ref, wab_ref, gq_ref, gk_ref, cos_ref, sin_ref,
                   gs_ref, qa_ref, ka_ref, va_ref, qb_ref, kb_ref, vb_ref, qkvc_ref, gatec_ref, ab_ref):
    x = x_ref[0]
    mod = mod_ref[0, 0]
    h = (_ln(x) * (1.0 + mod[1:2]) + mod[0:1]).astype(BF16)
    scale = HEAD_DIM ** -0.5 * LOG2E

    za = _dot(h, wa_ref[...])
    q = za[:, :A_Q_W]
    k = za[:, A_Q_W:A_Q_W + A_KV_W]
    v = za[:, A_Q_W + A_KV_W:]
    cos = cos_ref[...]
    sin = sin_ref[...]
    gs = gs_ref[...]
    qn = q * lax.rsqrt(_dot_exact_rhs(q * q, gs, 2) + LN_EPS) * gq_ref[...]
    qn = qn * cos + _rope_swap(qn) * sin
    qa_ref[0] = (qn * scale).astype(BF16)
    kn = k * lax.rsqrt(_dot_exact_rhs(k * k, gs[:A_KV_W, :A_KV_W], 2) + LN_EPS) * gk_ref[...]
    kn = kn * cos[:, :A_KV_W] + _rope_swap(kn) * sin[:, :A_KV_W]
    ones = jnp.ones((VT_ROWS - HEAD_DIM, x.shape[0]), F32)
    vt = v.T
    for j in range(A_KV_HEADS):
        ka_ref[0, j] = kn[:, j * HEAD_DIM:(j + 1) * HEAD_DIM].astype(BF16)
        va_ref[0, j] = jnp.concatenate([vt[j * HEAD_DIM:(j + 1) * HEAD_DIM], ones], axis=0).astype(BF16)

    zb = _dot(h, wb_ref[...])
    qb_ref[0] = (zb[:, :B_W] * scale).astype(BF16)
    kbv = zb[:, B_W:2 * B_W]
    vt = zb[:, 2 * B_W:].T
    for j in range(B_HEADS):
        kb_ref[0, j] = kbv[:, j * HEAD_DIM:(j + 1) * HEAD_DIM].astype(BF16)
        vb_ref[0, j] = jnp.concatenate([vt[j * HEAD_DIM:(j + 1) * HEAD_DIM], ones], axis=0).astype(BF16)

    zc = _dot(h, wc_ref[...])
    qkvc_ref[0] = zc[:, :3 * C_W]
    gatec_ref[0] = zc[:, 3 * C_W:]
    ab_ref[0] = _dot(h, wab_ref[...])


def _in_projection(xa, mods, wa, wb, wc, wab, gq, gk, cos, sin, gs, n_lat, tm):
    b, t, d = xa.shape
    nl = n_lat // tm
    row = lambda bi, i: (bi, i, 0)
    const = lambda bi, i: (0, 0)
    outs = [
        (jax.ShapeDtypeStruct((b, t, A_Q_W), BF16), pl.BlockSpec((1, tm, A_Q_W), row)),
        (jax.ShapeDtypeStruct((b, A_KV_HEADS, t, HEAD_DIM), BF16),
         pl.BlockSpec((1, A_KV_HEADS, tm, HEAD_DIM), lambda bi, i: (bi, 0, i, 0))),
        (jax.ShapeDtypeStruct((b, A_KV_HEADS, VT_ROWS, t), BF16),
         pl.BlockSpec((1, A_KV_HEADS, VT_ROWS, tm), lambda bi, i: (bi, 0, 0, i))),
        (jax.ShapeDtypeStruct((b, t, B_W), BF16), pl.BlockSpec((1, tm, B_W), row)),
        (jax.ShapeDtypeStruct((b, B_HEADS, t, HEAD_DIM), BF16),
         pl.BlockSpec((1, B_HEADS, tm, HEAD_DIM), lambda bi, i: (bi, 0, i, 0))),
        (jax.ShapeDtypeStruct((b, B_HEADS, VT_ROWS, t), BF16),
         pl.BlockSpec((1, B_HEADS, VT_ROWS, tm), lambda bi, i: (bi, 0, 0, i))),
        (jax.ShapeDtypeStruct((b, t, 3 * C_W), F32), pl.BlockSpec((1, tm, 3 * C_W), row)),
        (jax.ShapeDtypeStruct((b, t, C_W), F32), pl.BlockSpec((1, tm, C_W), row)),
        (jax.ShapeDtypeStruct((b, t, LANE), F32), pl.BlockSpec((1, tm, LANE), row)),
    ]
    return pl.pallas_call(
        _inproj_kernel,
        grid=(b, t // tm),
        in_specs=[pl.BlockSpec((1, tm, d), row),
                  pl.BlockSpec((1, 1, N_MOD, d), lambda bi, i: (bi, (i >= nl).astype(jnp.int32), 0, 0)),
                  pl.BlockSpec(wa.shape, const), pl.BlockSpec(wb.shape, const),
                  pl.BlockSpec(wc.shape, const), pl.BlockSpec(wab.shape, const),
                  pl.BlockSpec(gq.shape, const), pl.BlockSpec(gk.shape, const),
                  pl.BlockSpec((tm, A_Q_W), lambda bi, i: (i, 0)),
                  pl.BlockSpec((tm, A_Q_W), lambda bi, i: (i, 0)),
                  pl.BlockSpec(gs.shape, const)],
        out_specs=[o[1] for o in outs],
        out_shape=[o[0] for o in outs],
        compiler_params=_cparams(("parallel", "parallel")),
        name="in_projection",
    )(xa, mods, wa, wb, wc, wab, gq, gk, cos, sin, gs)


def _gqa_kernel(q_ref, k_ref, v_ref, o_ref, *, tq, ck, n_lat, n_tot):
    i = pl.program_id(1)
    g = A_HEADS // A_KV_HEADS

    def attend(key_lo, key_hi, n_split):
        qs = [jnp.concatenate([q_ref[0, :, (kh * g + j) * HEAD_DIM:(kh * g + j + 1) * HEAD_DIM] for j in range(g)],
                              axis=0) for kh in range(A_KV_HEADS)]
        step = (key_hi - key_lo) // n_split
        units = [(kh, key_lo + r * step, key_lo + (r + 1) * step) for r in range(n_split) for kh in range(A_KV_HEADS)]
        ss, ms, ps, oes = {}, {}, {}, {}
        for n in range(len(units) + 2):
            if n < len(units):
                kh, lo, hi = units[n]
                ss[n] = _dot_nt(k_ref[0, kh, lo:hi, :], qs[kh])
            if 0 <= n - 1 < len(units):
                ms[n - 1] = jnp.max(ss[n - 1], axis=0, keepdims=True)
                ps[n - 1] = jnp.exp2((ss.pop(n - 1) - ms[n - 1]).astype(BF16))
            if 0 <= n - 2 < len(units):
                kh, lo, hi = units[n - 2]
                oes[n - 2] = _dot(v_ref[0, kh, :, lo:hi], ps.pop(n - 2))
        for kh in range(A_KV_HEADS):
            mine = [n for n, u in enumerate(units) if u[0] == kh]
            m = functools.reduce(jnp.maximum, [ms[n] for n in mine])
            oe = sum(oes[n] * jnp.exp2(ms[n] - m) for n in mine)
            o = (oe[:HEAD_DIM] / oe[HEAD_DIM:HEAD_DIM + 1]).T.astype(BF16)
            for j in range(g):
                o_ref[0, :, (kh * g + j) * HEAD_DIM:(kh * g + j + 1) * HEAD_DIM] = o[j * tq:(j + 1) * tq]

    @pl.when(i < n_lat // tq)
    def _():
        attend(0, n_tot, ck)

    @pl.when(i >= n_lat // tq)
    def _():
        attend(n_lat, n_tot, 1)


def _gqa(qa, ka, va, n_lat, tq, ck):
    b, t, _ = qa.shape
    return pl.pallas_call(
        functools.partial(_gqa_kernel, tq=tq, ck=ck, n_lat=n_lat, n_tot=t),
        grid=(b, t // tq),
        in_specs=[pl.BlockSpec((1, tq, A_Q_W), lambda bi, i: (bi, i, 0)),
                  pl.BlockSpec((1, A_KV_HEADS, t, HEAD_DIM), lambda bi, i: (bi, 0, 0, 0)),
                  pl.BlockSpec((1, A_KV_HEADS, VT_ROWS, t), lambda bi, i: (bi, 0, 0, 0))],
        out_specs=pl.BlockSpec((1, tq, A_Q_W), lambda bi, i: (bi, i, 0)),
        out_shape=jax.ShapeDtypeStruct((b, t, A_Q_W), BF16),
        compiler_params=_cparams(("parallel", "parallel")),
        name="gqa",
    )(qa, ka, va)


def _na_window_start(r0, rows):
    return np.clip(r0 - NA_ROWS // 2, 0, rows - NA_ROWS - 1) // 2 * 2


def _na_classes(n_lat, n_tot):
    rows = n_lat // GRID_W
    assert rows >= NA_WIN and rows % 2 == 0
    geoms, table = [], []
    for r0 in range(0, rows, 2):
        geom = (int(_na_window_start(r0, rows)) - r0,) + tuple(
            int(np.clip(r0 + qr - NA_ROWS // 2, 0, rows - NA_ROWS)) - r0 for qr in range(2))
        if geom not in geoms:
            geoms.append(geom)
        table.append(geoms.index(geom))
    table += [len(geoms)] * ((n_tot - n_lat) // (2 * GRID_W))
    return geoms, np.asarray(table, np.int32)


def _na_bias_slabs(rpb, geoms):
    n_head = rpb.shape[0]
    cj = np.arange(GRID_W)[:, None]
    c = np.arange(GRID_W)[None, :]
    col_start = np.clip(c - NA_COLS // 2, 0, GRID_W - NA_COLS)
    col_ok = (cj >= col_start) & (cj < col_start + NA_COLS)
    col_off = np.clip(cj - c + (NA_COLS - 1), 0, 2 * NA_COLS - 2)
    spread = np.zeros((2 * NA_COLS - 1, GRID_W * GRID_W), np.float32)
    spread[col_off.reshape(-1), np.arange(GRID_W * GRID_W)] = 1.0
    kr = np.arange(NA_WIN)[:, None]
    qr = np.arange(2)[None, :]
    pick = np.zeros((len(geoms), NA_WIN, 2, 2 * NA_ROWS - 1), np.float32)
    ok = np.zeros((len(geoms), 1, NA_WIN, 2, GRID_W, GRID_W), bool)
    for n, (a, b0, b1) in enumerate(geoms):
        first = np.where(qr == 0, b0, b1)
        row_ok = (a + kr >= first) & (a + kr < first + NA_ROWS)
        row_off = np.clip(a + kr - qr + (NA_ROWS - 1), 0, 2 * NA_ROWS - 2)
        pick[n, kr, qr, row_off] = 1.0
        ok[n, 0] = row_ok[:, :, None, None] & col_ok[None, None]
    rows = jnp.einsum("nkqo,hoc->nhkqc", jnp.asarray(pick), rpb * LOG2E, precision=HIGHEST)
    full = jnp.dot(rows, jnp.asarray(spread), precision=HIGHEST)
    full = full.reshape(len(geoms), n_head, NA_WIN, 2, GRID_W, GRID_W)
    masked = jnp.where(jnp.asarray(ok), full, NEG_BIG)
    slabs = jnp.concatenate([masked[:, :, :, 0], masked[:, :, :, 1]], axis=-1)
    slabs = slabs.reshape(len(geoms), n_head, NA_WIN * GRID_W, 2 * GRID_W)
    return jnp.concatenate([slabs, jnp.full_like(slabs[:1], NEG_BIG)], axis=0).astype(F32)


def _na_kernel(cls_ref, q_ref, k_ref, vt_ref, bias_ref, o_ref, *, n_lat, n_tot):
    del cls_ref
    rows = n_lat // GRID_W
    r0 = 2 * pl.program_id(1)
    start = jnp.clip(r0 - NA_ROWS // 2, 0, rows - NA_ROWS - 1) // 2 * 2
    nb = pl.ds(pl.multiple_of(start * GRID_W, 2 * GRID_W), NA_WIN * GRID_W)
    hs = [slice(h * HEAD_DIM, (h + 1) * HEAD_DIM) for h in range(B_HEADS)]
    qs = [q_ref[0, :, hs[h]] for h in range(B_HEADS)]
    s_nb = [_dot_nt(k_ref[0, h, nb, :], qs[h]) + bias_ref[0, h] for h in range(B_HEADS)]
    s_cx = [_dot_nt(k_ref[0, h, n_lat:n_tot, :], qs[h]) for h in range(B_HEADS)]
    ms = [jnp.maximum(jnp.max(a, axis=0, keepdims=True), jnp.max(c, axis=0, keepdims=True))
          for a, c in zip(s_nb, s_cx)]
    p_nb = [jnp.exp2((a - m).astype(BF16)) for a, m in zip(s_nb, ms)]
    p_cx = [jnp.exp2((c - m).astype(BF16)) for c, m in zip(s_cx, ms)]
    for h in range(B_HEADS):
        oe = _dot(vt_ref[0, h, :, nb], p_nb[h]) + _dot(vt_ref[0, h, :, n_lat:n_tot], p_cx[h])
        o_ref[0, :, hs[h]] = (oe[:HEAD_DIM] / oe[HEAD_DIM:HEAD_DIM + 1]).T.astype(BF16)


def _neighborhood(qb, kb, vbt, slabs, table, n_lat):
    b, t, _ = qb.shape
    tq = 2 * GRID_W
    grid_spec = pltpu.PrefetchScalarGridSpec(
        num_scalar_prefetch=1,
        grid=(b, t // tq),
        in_specs=[pl.BlockSpec((1, tq, B_W), lambda bi, pi, cls: (bi, pi, 0)),
                  pl.BlockSpec((1, B_HEADS, t, HEAD_DIM), lambda bi, pi, cls: (bi, 0, 0, 0)),
                  pl.BlockSpec((1, B_HEADS, VT_ROWS, t), lambda bi, pi, cls: (bi, 0, 0, 0)),
                  pl.BlockSpec((1,) + slabs.shape[1:], lambda bi, pi, cls: (cls[pi], 0, 0, 0))],
        out_specs=pl.BlockSpec((1, tq, B_W), lambda bi, pi, cls: (bi, pi, 0)),
    )
    return pl.pallas_call(
        functools.partial(_na_kernel, n_lat=n_lat, n_tot=t),
        grid_spec=grid_spec,
        out_shape=jax.ShapeDtypeStruct((b, t, B_W), BF16),
        compiler_params=_cparams(("parallel", "parallel")),
        name="neighborhood",
    )(jnp.asarray(table), qb, kb, vbt, slabs)


GC_LANE, GL_LANE, BETA_LANE = 0, 2 * C_HEADS, 4 * C_HEADS


def _gdn_prep_kernel(x_ref, prev_ref, next_ref, ab_ref, cw_ref, alog_ref, dtb_ref, gs_ref,
                     q_ref, k_ref, v_ref, comp_ref, *, tm, n_lat, n_tot):
    i = pl.program_id(1)
    nl = n_lat // tm
    nt = n_tot // tm
    first = jnp.logical_or(i == 0, i == nl)
    last = jnp.logical_or(i == nl - 1, i == nt - 1)
    x = x_ref[0]
    pv = jnp.where(first, 0.0, prev_ref[0])
    nx = jnp.where(last, 0.0, next_ref[0])
    row = lax.broadcasted_iota(jnp.int32, (tm, 1), 0)
    prev1 = pv[SUBLANE - 1:SUBLANE]
    prev2 = pv[SUBLANE - 2:SUBLANE - 1]
    xm1 = jnp.where(row == 0, prev1, pltpu.roll(x, 1, 0))
    xm2 = jnp.where(row == 0, prev2, jnp.where(row == 1, prev1, pltpu.roll(x, 2, 0)))
    xp1 = jnp.where(row == tm - 1, nx[0:1], pltpu.roll(x, tm - 1, 0))
    cw = cw_ref[...]
    y = _silu(xm2 * cw[0:1] + xm1 * cw[1:2] + x * cw[2:3] + xp1 * cw[3:4])
    q = y[:, :C_W]
    k = y[:, C_W:2 * C_W]
    gs = gs_ref[...]
    q_ref[0] = q * lax.rsqrt(_dot_exact_rhs(q * q, gs, 2) + LN_EPS) * (HEAD_DIM ** -0.5)
    k_ref[0] = k * lax.rsqrt(_dot_exact_rhs(k * k, gs, 2) + LN_EPS)
    v_ref[0] = y[:, 2 * C_W:]
    ab = ab_ref[0]
    z = ab + dtb_ref[...]
    softplus = jnp.maximum(z, 0.0) + jnp.log(1.0 + jnp.exp(-jnp.abs(z)))
    lane = lax.broadcasted_iota(jnp.int32, ab.shape, 1)
    gb = jnp.where(lane < 2 * C_HEADS, -jnp.exp(alog_ref[...]) * softplus, _sigmoid(ab))
    ii = lax.broadcasted_iota(jnp.int32, (tm, tm), 0)
    jj = lax.broadcasted_iota(jnp.int32, (tm, tm), 1)
    same = (ii // CHUNK) == (jj // CHUNK)
    gb3 = _split_bf16(gb, 3)

    def masked_sum(mask):
        mb = jnp.where(mask, 1.0, 0.0).astype(BF16)
        return sum(_dot(mb, piece) for piece in gb3)

    prefix = masked_sum(jnp.logical_and(same, jj <= ii))
    suffix = masked_sum(jnp.logical_and(same, jj >= ii))
    total = masked_sum(same)
    comp = jnp.where(lane < C_HEADS, prefix,
                     jnp.where(lane < GL_LANE, suffix,
                               jnp.where(lane < BETA_LANE, pltpu.roll(total, GL_LANE, 1),
                                         jnp.where(lane < BETA_LANE + 2 * C_HEADS, pltpu.roll(gb, 2 * C_HEADS, 1),
                                                   0.0))))
    comp_ref[0] = comp


def _gdn_prep(qkvc, ab, conv_w, a_log, dt_bias, gs, n_lat, tm):
    b, t, _ = qkvc.shape
    c3 = 3 * C_W
    alog_row = jnp.zeros((1, LANE), F32).at[0, :2 * C_HEADS].set(a_log.reshape(-1))
    dtb_row = jnp.zeros((1, LANE), F32).at[0, :2 * C_HEADS].set(dt_bias.reshape(-1))
    row = lambda bi, i: (bi, i, 0)
    const2 = lambda bi, i: (0, 0)
    hb = tm // SUBLANE
    last8 = t // SUBLANE - 1
    tok = jax.ShapeDtypeStruct((b, t, C_W), F32)
    return pl.pallas_call(
        functools.partial(_gdn_prep_kernel, tm=tm, n_lat=n_lat, n_tot=t),
        grid=(b, t // tm),
        in_specs=[pl.BlockSpec((1, tm, c3), row),
                  pl.BlockSpec((1, SUBLANE, c3), lambda bi, i: (bi, jnp.maximum(i * hb - 1, 0), 0)),
                  pl.BlockSpec((1, SUBLANE, c3), lambda bi, i: (bi, jnp.minimum((i + 1) * hb, last8), 0)),
                  pl.BlockSpec((1, tm, LANE), row),
                  pl.BlockSpec((CONV_K, c3), const2),
                  pl.BlockSpec((1, LANE), const2), pl.BlockSpec((1, LANE), const2),
                  pl.BlockSpec(gs.shape, const2)],
        out_specs=[pl.BlockSpec((1, tm, C_W), row)] * 3 + [pl.BlockSpec((1, tm, LANE), row)],
        out_shape=[tok, tok, tok, jax.ShapeDtypeStruct((b, t, LANE), F32)],
        compiler_params=_cparams(("parallel", "parallel")),
        name="gdn_prep",
    )(qkvc, qkvc, qkvc, ab, conv_w, alog_row, dtb_row, gs)


def _unit_tri_inverses(lms, eye, bd16, bd32):
    def mm(a, b_):
        return _dot(a.astype(BF16), b_.astype(BF16))

    lds = [jnp.where(bd16, lm, 0.0) for lm in lms]
    xs = [eye - ld for ld in lds]
    lds = [ld.astype(BF16) for ld in lds]
    pws = [_dot(ld, ld).astype(BF16) for ld in lds]
    for step in range(3):
        xs = [x + mm(x, pw) for x, pw in zip(xs, pws)]
        if step < 2:
            pws = [_dot(pw, pw).astype(BF16) for pw in pws]
    for level in (lambda lm: jnp.where(jnp.logical_and(bd32, jnp.logical_not(bd16)), lm, 0.0),
                  lambda lm: jnp.where(bd32, 0.0, lm)):
        xbs = [x.astype(BF16) for x in xs]
        ys = [_dot(level(lm).astype(BF16), xb) for lm, xb in zip(lms, xbs)]
        xs = [x - _dot(xb, y.astype(BF16)) for x, xb, y in zip(xs, xbs, ys)]
    return xs


def _gdn_local_kernel(q_ref, k_ref, v_ref, comp_ref, ex_ref, sel_ref, u_ref, w_ref, kd_ref, qd_ref, in_ref):
    n = q_ref.shape[1]
    hp = 2 * HEAD_DIM
    n_pr = q_ref.shape[2] // hp
    ii = lax.broadcasted_iota(jnp.int32, (n, n), 0)
    jj = lax.broadcasted_iota(jnp.int32, (n, n), 1)
    same = (ii // CHUNK) == (jj // CHUNK)
    eye = (ii == jj).astype(F32)
    bd16 = (ii // 16) == (jj // 16)
    bd32 = (ii // 32) == (jj // 32)
    lane = lax.broadcasted_iota(jnp.int32, (1, hp), 1)
    lane2 = lax.broadcasted_iota(jnp.int32, (1, 2 * hp), 1)
    incls = [jnp.logical_and(same, ii >= jj), jnp.logical_and(same, ii <= jj)]
    stricts = [jnp.logical_and(same, ii > jj), jnp.logical_and(same, ii < jj)]
    lms, intras, rhss = [], [], []
    for pr in range(n_pr):
        ls = slice(pr * hp, (pr + 1) * hp)
        q = q_ref[0, :, ls]
        k = k_ref[0, :, ls]
        v = v_ref[0, :, ls]
        e = _dot_exact_rhs(comp_ref[0], ex_ref[pr])
        kh = [jnp.where((lane // HEAD_DIM) == hh, k, 0.0).astype(BF16) for hh in range(2)]
        for p in range(2):
            gc = e[:, p * hp:(p + 1) * hp]
            gl = e[:, (2 + p) * hp:(3 + p) * hp]
            beta = e[:, (4 + p) * hp:(5 + p) * hp]
            eg = jnp.exp(gc)
            kb = k * beta
            kd_ref[p, 0, :, ls] = (k * jnp.exp(gl - gc)).astype(BF16)
            qd_ref[p, 0, :, ls] = (q * eg).astype(BF16)
            rhs = jnp.concatenate([v * beta, kb * eg], axis=1)
            kbq = jnp.concatenate([kb, q], axis=0).astype(BF16)
            for hh in range(2):
                gcb = jnp.broadcast_to(gc[:, hh * HEAD_DIM:hh * HEAD_DIM + 1], (n, n))
                decay = jnp.exp(jnp.where(incls[p], gcb - gcb.T, NEG_BIG))
                aq = _dot_nt(kbq, kh[hh])
                lms.append(jnp.where(stricts[p], aq[:n] * decay, 0.0))
                intras.append(jnp.where(incls[p], aq[n:] * decay, 0.0).astype(BF16))
                rhss.append(jnp.where(((lane2 % hp) // HEAD_DIM) == hh, rhs, 0.0).astype(BF16))
    tmats = _unit_tri_inverses(lms, eye, bd16, bd32)
    for pr in range(n_pr):
        ls = slice(pr * hp, (pr + 1) * hp)
        for p in range(2):
            at = 4 * pr + 2 * p
            uw = sum(_dot(tmats[at + hh].astype(BF16), rhss[at + hh]) for hh in range(2))
            intra2 = sum(_dot(intras[at + hh], sel_ref[hh]) for hh in range(2))
            u_ref[p, 0, :, ls] = uw[:, :hp].astype(BF16)
            w_ref[p, 0, :, ls] = uw[:, hp:].astype(BF16)
            in_ref[p, 0, :, ls] = intra2.astype(BF16)


def _gdn_local(q, k, v, comp, blk):
    b, t, _ = q.shape
    hp = 2 * HEAD_DIM
    n_pair = C_W // hp
    ex = np.zeros((n_pair, LANE, 6 * hp), np.float32)
    for pr in range(n_pair):
        for p in range(2):
            for hh in range(2):
                src = p * C_HEADS + pr * 2 + hh
                for kind, base in enumerate((GC_LANE, GL_LANE, BETA_LANE)):
                    col = (2 * kind + p) * hp + hh * HEAD_DIM
                    ex[pr, base + src, col:col + HEAD_DIM] = 1.0
    sel = np.zeros((2, blk, hp), np.float32)
    for hh in range(2):
        sel[hh, np.arange(blk), hh * HEAD_DIM + np.arange(blk) % CHUNK] = 1.0
    gp = LOCAL_PAIRS
    tok = lambda bi, pg, i: (bi, i, pg)
    dirs = lambda bi, pg, i: (0, bi, i, pg)
    out = jax.ShapeDtypeStruct((2, b, t, C_W), BF16)
    return pl.pallas_call(
        _gdn_local_kernel,
        grid=(b, n_pair // gp, t // blk),
        in_specs=[pl.BlockSpec((1, blk, gp * hp), tok)] * 3
                 + [pl.BlockSpec((1, blk, LANE), lambda bi, pg, i: (bi, i, 0)),
                    pl.BlockSpec((gp, LANE, 6 * hp), lambda bi, pg, i: (pg, 0, 0)),
                    pl.BlockSpec((2, blk, hp), lambda bi, pg, i: (0, 0, 0))],
        out_specs=[pl.BlockSpec((2, 1, blk, gp * hp), dirs)] * 5,
        out_shape=[out] * 5,
        compiler_params=_cparams(("parallel", "parallel", "parallel")),
        name="gdn_local",
    )(q, k, v, comp, jnp.asarray(ex, dtype=BF16), jnp.asarray(sel, dtype=BF16))


def _gdn_scan_kernel(*refs, ncb, nb):
    ins = (refs[0:6], refs[6:12])
    ex_ref = refs[12]
    outs = refs[13:15]
    s_ref = refs[15]
    j = pl.program_id(1)

    @pl.when(j == 0)
    def _():
        s_ref[...] = jnp.zeros_like(s_ref)

    gw = 4 * HEAD_DIM
    n_grp = C_W // gw
    lane_head = lax.broadcasted_iota(jnp.int32, (1, gw), 1) // HEAD_DIM
    rr = lax.broadcasted_iota(jnp.int32, (gw, gw), 0) // HEAD_DIM
    cc_ = lax.broadcasted_iota(jnp.int32, (gw, gw), 1) // HEAD_DIM
    blockdiag = rr == cc_
    chains = [(bi, p, g) for bi in range(nb) for p in range(2) for g in range(n_grp)]
    for cc in range(ncb):
        rows = [slice(cc * CHUNK, (cc + 1) * CHUNK), slice((ncb - 1 - cc) * CHUNK, (ncb - cc) * CHUNK)]
        sdec = {(bi, p): jnp.exp(_dot_exact_rhs(ins[p][5][bi, rows[p].start:rows[p].start + SUBLANE, :], ex_ref[...]))
                for bi in range(nb) for p in range(2)}
        lanes = [slice(g * gw, (g + 1) * gw) for g in range(n_grp)]
        ss = [s_ref[bi, p, g] for bi, p, g in chains]
        sbs = [s.astype(BF16) for s in ss]
        v_news = [ins[p][0][0, bi, rows[p], lanes[g]].astype(F32) - _dot(ins[p][1][0, bi, rows[p], lanes[g]], sb)
                  for (bi, p, g), sb in zip(chains, sbs)]
        vstacks = [jnp.concatenate([jnp.where(lane_head == h, v_new, 0.0).astype(BF16)
                                    for h in range(gw // HEAD_DIM)], axis=0) for v_new in v_news]
        for (bi, p, g), s, sb, v_new, vstack in zip(chains, ss, sbs, v_news, vstacks):
            o = _dot(ins[p][3][0, bi, rows[p], lanes[g]], sb) + _dot(ins[p][4][0, bi, rows[p], lanes[g]], vstack)
            upd = _dot_tn(ins[p][2][0, bi, rows[p], lanes[g]], v_new.astype(BF16))
            decay = sdec[bi, p][0:1, p * C_W + g * gw:p * C_W + (g + 1) * gw]
            s_ref[bi, p, g] = s * decay + jnp.where(blockdiag, upd, 0.0)
            outs[p][bi, rows[p], lanes[g]] = o


def _gdn_scan(u, w, kd, qd, intra, comp, n_lat, blk):
    _, b, t, _ = u.shape
    nl = n_lat // blk
    nx = (t - n_lat) // blk
    ncb = blk // CHUNK
    ex = np.zeros((LANE, 2 * C_W), np.float32)
    for p in range(2):
        for h in range(C_HEADS):
            ex[GL_LANE + p * C_HEADS + h, p * C_W + h * HEAD_DIM:p * C_W + (h + 1) * HEAD_DIM] = 1.0

    def blk0(j):
        return jnp.where(j < nx, nl + j, j - nx)

    def blk1(j):
        return jnp.where(j < nx, nl + nx - 1 - j, nl - 1 - (j - nx))

    nb = SCAN_BATCH if b % SCAN_BATCH == 0 else 1
    specs = []
    for p, bf in enumerate((blk0, blk1)):
        specs += [pl.BlockSpec((1, nb, blk, C_W), lambda bi, j, p=p, bf=bf: (p, bi, bf(j), 0))] * 5
        specs += [pl.BlockSpec((nb, blk, LANE), lambda bi, j, bf=bf: (bi, bf(j), 0))]
    specs += [pl.BlockSpec((LANE, 2 * C_W), lambda bi, j: (0, 0))]
    out = jax.ShapeDtypeStruct((b, t, C_W), F32)
    gw = 4 * HEAD_DIM
    return pl.pallas_call(
        functools.partial(_gdn_scan_kernel, ncb=ncb, nb=nb),
        grid=(b // nb, nl + nx),
        in_specs=specs,
        out_specs=[pl.BlockSpec((nb, blk, C_W), lambda bi, j: (bi, blk0(j), 0)),
                   pl.BlockSpec((nb, blk, C_W), lambda bi, j: (bi, blk1(j), 0))],
        out_shape=[out, out],
        scratch_shapes=[pltpu.VMEM((nb, 2, C_W // gw, gw, gw), F32)],
        compiler_params=_cparams(("parallel", "arbitrary")),
        name="gdn_scan",
    )(u, w, kd, qd, intra, comp, u, w, kd, qd, intra, comp, jnp.asarray(ex, dtype=BF16))


def _merge_kernel(x_ref, mod_ref, oa_ref, ob_ref, oc0_ref, oc1_ref, gate_ref, wmg_ref, wbr_ref, wout_ref, og_ref,
                  gs_ref, lng_ref, lnb_ref, wr_ref, o_ref, h_ref, aff_ref, *, alpha, n_exp):
    x = x_ref[0]
    d = x.shape[-1]
    mod = mod_ref[0, 0]
    h = (_ln(x) * (1.0 + mod[1:2]) + mod[0:1]).astype(BF16)
    gates = _sigmoid(_dot(h, wmg_ref[...]))
    oc = oc0_ref[0] + oc1_ref[0]
    ocn = oc * lax.rsqrt(_dot_exact_rhs(oc * oc, gs_ref[...], 2) + LN_EPS) * og_ref[...] * _silu(gate_ref[0])
    m = gates[:, :d] * _dot(oa_ref[0], wbr_ref[0])
    m = m + gates[:, d:2 * d] * _dot(ob_ref[0], wbr_ref[1])
    m = m + gates[:, 2 * d:] * _dot(ocn.astype(BF16), wbr_ref[2])
    mix = _dot(m.astype(BF16), wout_ref[...])
    y = alpha * x + mod[2:3] * mix
    x1 = _ln(y) * lng_ref[...] + lnb_ref[...]
    o_ref[0] = x1
    h2 = _ln(x1) * (1.0 + mod[4:5]) + mod[3:4]
    h_ref[0] = h2.astype(BF16)
    hp = _split_bf16(h2, 3)
    logits = sum(_dot(hp[i], wr_ref[j]) for i, j in ((2, 0), (1, 1), (0, 2), (1, 0), (0, 1), (0, 0)))
    lane = lax.broadcasted_iota(jnp.int32, logits.shape, 1)
    logits = jnp.where(lane < n_exp, logits, NEG_BIG)
    ex = jnp.exp(logits - jnp.max(logits, axis=-1, keepdims=True))
    aff_ref[0] = ex / jnp.sum(ex, axis=-1, keepdims=True)


def _merge(xa, mods, oa, ob, oc0, oc1, gate_c, wmg, wbr, wout, og, gs, lng, lnb, w_router, n_lat, tm, alpha):
    b, t, d = xa.shape
    nl = n_lat // tm
    n_exp = w_router.shape[-1]
    wr_pad = jnp.pad(w_router, ((0, 0), (0, LANE - n_exp)))
    pieces, rest = [], wr_pad
    for _ in range(3):
        piece = lax.reduce_precision(rest, exponent_bits=8, mantissa_bits=7)
        pieces.append(piece.astype(BF16))
        rest = rest - piece
    wr = jnp.stack(pieces)
    row = lambda bi, i: (bi, i, 0)
    const2 = lambda bi, i: (0, 0)
    return pl.pallas_call(
        functools.partial(_merge_kernel, alpha=alpha, n_exp=n_exp),
        grid=(b, t // tm),
        in_specs=[pl.BlockSpec((1, tm, d), row),
                  pl.BlockSpec((1, 1, N_MOD, d), lambda bi, i: (bi, (i >= nl).astype(jnp.int32), 0, 0)),
                  pl.BlockSpec((1, tm, BRANCH_W), row), pl.BlockSpec((1, tm, BRANCH_W), row),
                  pl.BlockSpec((1, tm, BRANCH_W), row), pl.BlockSpec((1, tm, BRANCH_W), row),
                  pl.BlockSpec((1, tm, BRANCH_W), row),
                  pl.BlockSpec(wmg.shape, const2),
                  pl.BlockSpec(wbr.shape, lambda bi, i: (0, 0, 0)),
                  pl.BlockSpec(wout.shape, const2),
                  pl.BlockSpec(og.shape, const2), pl.BlockSpec(gs.shape, const2),
                  pl.BlockSpec(lng.shape, const2), pl.BlockSpec(lnb.shape, const2),
                  pl.BlockSpec(wr.shape, lambda bi, i: (0, 0, 0))],
        out_specs=[pl.BlockSpec((1, tm, d), row), pl.BlockSpec((1, tm, d), row), pl.BlockSpec((1, tm, LANE), row)],
        out_shape=[jax.ShapeDtypeStruct((b, t, d), F32), jax.ShapeDtypeStruct((b, t, d), BF16),
                   jax.ShapeDtypeStruct((b, t, LANE), F32)],
        compiler_params=_cparams(("parallel", "parallel")),
        name="merge",
    )(xa, mods, oa, ob, oc0, oc1, gate_c, wmg, wbr, wout, og, gs, lng, lnb, wr)


def _ffn_kernel(x_ref, wg_ref, wu_ref, wd_ref, gate_ref, o_ref, acc_ref):
    f = pl.program_id(2)

    @pl.when(f == 0)
    def _():
        acc_ref[...] = jnp.zeros_like(acc_ref)

    x = x_ref[0]
    g = _dot(x, wg_ref[0, 0].astype(BF16))
    u = _dot(x, wu_ref[0, 0].astype(BF16))
    a = (_silu(g) * u).astype(BF16)
    acc_ref[...] += _dot(a, wd_ref[0, 0].astype(BF16))

    @pl.when(f == pl.num_programs(2) - 1)
    def _():
        o_ref[0] = (acc_ref[...] * gate_ref[0]).astype(BF16)


def _expert_ffn(xs, gate, w_gate_up, w_down, layer, tm, tf):
    e, r, d = xs.shape
    f = w_down.shape[2]
    nf = f // tf
    return pl.pallas_call(
        _ffn_kernel,
        grid=(e, r // tm, nf),
        in_specs=[pl.BlockSpec((1, tm, d), lambda ei, i, fi: (ei, i, 0)),
                  pl.BlockSpec((1, 1, d, tf), lambda ei, i, fi: (layer, ei, 0, fi)),
                  pl.BlockSpec((1, 1, d, tf), lambda ei, i, fi: (layer, ei, 0, nf + fi)),
                  pl.BlockSpec((1, 1, tf, d), lambda ei, i, fi: (layer, ei, fi, 0)),
                  pl.BlockSpec((1, tm, 1), lambda ei, i, fi: (ei, i, 0))],
        out_specs=pl.BlockSpec((1, tm, d), lambda ei, i, fi: (ei, i, 0)),
        out_shape=jax.ShapeDtypeStruct((e, r, d), BF16),
        scratch_shapes=[pltpu.VMEM((tm, d), F32)],
        compiler_params=_cparams(("parallel", "parallel", "arbitrary")),
        name="expert_ffn",
    )(xs, w_gate_up, w_gate_up, w_down, gate)


VALID, FIRST, LAST = 1, 2, 4


def _combine_plan(tok_of_pair, n_tok, tm, rwin):
    bsz, n_pair = tok_of_pair.shape
    nt = n_tok // tm
    n_win = n_pair // rwin
    n_item = nt + n_win
    order = jnp.argsort(tok_of_pair, axis=1)
    tok = jnp.take_along_axis(tok_of_pair, order, axis=1)
    bounds = jnp.arange(nt + 1, dtype=jnp.int32) * tm
    off = jnp.sum(tok[:, None, :] < bounds[None, :, None], axis=-1, dtype=jnp.int32)
    first = jnp.minimum(off[:, :-1] // rwin, n_win - 1)
    last = jnp.minimum(jnp.maximum(off[:, 1:] - 1, off[:, :-1]) // rwin, n_win - 1)
    n_items = last - first + 1
    cum = jnp.cumsum(n_items, axis=1)
    k = jnp.arange(n_item, dtype=jnp.int32)
    tile = jnp.sum(cum[:, None, :] <= k[None, :, None], axis=-1, dtype=jnp.int32)
    valid = tile < nt
    tile = jnp.minimum(tile, nt - 1)
    end = jnp.take_along_axis(cum, tile, axis=1)
    start = end - jnp.take_along_axis(n_items, tile, axis=1)
    win = jnp.where(valid, jnp.take_along_axis(first, tile, axis=1) + (k[None] - start),
                    jnp.take_along_axis(last, tile, axis=1))
    flags = (valid * VALID + (valid & (k[None] == start)) * FIRST + (valid & (k[None] == end - 1)) * LAST)
    return order, tok, tile, win.astype(jnp.int32), flags.astype(jnp.int32)


def _combine_kernel(tile_ref, win_ref, flag_ref, y_ref, tok_ref, x_ref, mod_ref, lng_ref, lnb_ref, o_ref, acc_ref, *,
                    alpha, tm):
    b = pl.program_id(0)
    k = pl.program_id(1)
    flags = flag_ref[b, k]

    @pl.when((flags & FIRST) != 0)
    def _():
        acc_ref[...] = jnp.zeros_like(acc_ref)

    @pl.when((flags & VALID) != 0)
    def _():
        tok0 = tile_ref[b, k] * tm
        rows = lax.broadcasted_iota(jnp.int32, (tm, tok_ref.shape[-1]), 0) + tok0
        pick = jnp.where(rows == tok_ref[0, 0], 1.0, 0.0).astype(BF16)
        acc_ref[...] += _dot(pick, y_ref[0])

    @pl.when((flags & LAST) != 0)
    def _():
        mod = mod_ref[0, 0]
        y = alpha * x_ref[0] + mod[5:6] * acc_ref[...]
        o_ref[0] = _ln(y) * lng_ref[...] + lnb_ref[...]


def _combine(y_sorted, tok_sorted, tile, win, flags, xa, mods, lng, lnb, n_lat, tm, rwin, alpha):
    b, t, d = xa.shape
    nl = n_lat // tm
    n_item = tile.shape[1]
    grid_spec = pltpu.PrefetchScalarGridSpec(
        num_scalar_prefetch=3,
        grid=(b, n_item),
        in_specs=[pl.BlockSpec((1, rwin, d), lambda bi, k, tl, wn, fl: (bi, wn[bi, k], 0)),
                  pl.BlockSpec((1, 1, 1, rwin), lambda bi, k, tl, wn, fl: (bi, wn[bi, k], 0, 0)),
                  pl.BlockSpec((1, tm, d), lambda bi, k, tl, wn, fl: (bi, tl[bi, k], 0)),
                  pl.BlockSpec((1, 1, N_MOD, d),
                               lambda bi, k, tl, wn, fl: (bi, (tl[bi, k] >= nl).astype(jnp.int32), 0, 0)),
                  pl.BlockSpec(lng.shape, lambda bi, k, tl, wn, fl: (0, 0)),
                  pl.BlockSpec(lnb.shape, lambda bi, k, tl, wn, fl: (0, 0))],
        out_specs=pl.BlockSpec((1, tm, d), lambda bi, k, tl, wn, fl: (bi, tl[bi, k], 0)),
        scratch_shapes=[pltpu.VMEM((tm, d), F32)],
    )
    return pl.pallas_call(
        functools.partial(_combine_kernel, alpha=alpha, tm=tm),
        grid_spec=grid_spec,
        out_shape=jax.ShapeDtypeStruct((b, t, d), F32),
        compiler_params=_cparams(("parallel", "arbitrary")),
        name="moe_combine",
    )(tile, win, flags, y_sorted, tok_sorted.reshape(b, -1, 1, rwin), xa, mods, lng, lnb)


def _moe(xa, h, aff, mods, n_exp, w_gate_up, w_down, lng, lnb, layer, n_lat, tm, alpha):
    b, t, d = xa.shape
    m_ctx = t - n_lat
    aff = jnp.swapaxes(aff[:, :, :n_exp], 1, 2)
    cap_l = CAPACITY * n_lat // n_exp
    cap_x = CAPACITY * m_ctx // n_exp
    gate_l, idx_l = lax.top_k(aff[:, :, :n_lat], cap_l)
    gate_x, idx_x = lax.top_k(aff[:, :, n_lat:], cap_x)
    idx = jnp.concatenate([idx_l, idx_x + n_lat], axis=-1)
    gate = jnp.concatenate([gate_l, gate_x], axis=-1)
    cap = cap_l + cap_x
    idx_e = jnp.swapaxes(idx, 0, 1)
    bidx = jnp.arange(b)[None, :, None]
    xs = h[bidx, idx_e].reshape(n_exp, b * cap, d)
    gate_e = jnp.swapaxes(gate, 0, 1).reshape(n_exp, b * cap, 1)
    rows = b * cap
    tmr = rows
    for cand in (1088, 1024, 512, 256, 128, 64, 32, 16):
        if rows % cand == 0:
            tmr = cand
            break
    tf = min(512, w_down.shape[2])
    y = _expert_ffn(xs, gate_e, w_gate_up, w_down, layer, tmr, tf)
    n_pair = n_exp * cap
    rwin = next(r for r in (512, 256, 128, 64, 32, 16, 8) if n_pair % r == 0)
    order, tok, tile, win, flags = _combine_plan(idx.reshape(b, n_pair), t, tm, rwin)
    y_sorted = y[order // cap, jnp.arange(b)[:, None] * cap + order % cap]
    return _combine(y_sorted, tok, tile, win, flags, xa, mods, lng, lnb, n_lat, tm, rwin, alpha)


def _rope_tables(n_lat, m_ctx):
    t = np.arange(n_lat)
    rows = (t // GRID_W).astype(np.float32)
    cols = (t % GRID_W).astype(np.float32)
    n_freq = HEAD_DIM // 4
    inv_freq = jnp.asarray(ROPE_THETA, F32) ** (-jnp.arange(n_freq, dtype=F32) / n_freq)
    ang_r = jnp.asarray(rows)[:, None] * inv_freq
    ang_c = jnp.asarray(cols)[:, None] * inv_freq
    cr, sr, cc, sc = jnp.cos(ang_r), jnp.sin(ang_r), jnp.cos(ang_c), jnp.sin(ang_c)
    cos = jnp.concatenate([cr, cr, cc, cc], axis=-1)
    sin = jnp.concatenate([-sr, sr, -sc, sc], axis=-1)
    cos = jnp.concatenate([cos, jnp.ones((m_ctx, HEAD_DIM), F32)], axis=0)
    sin = jnp.concatenate([sin, jnp.zeros((m_ctx, HEAD_DIM), F32)], axis=0)
    return jnp.tile(cos, (1, A_HEADS)), jnp.tile(sin, (1, A_HEADS))


def kernel(x, c, ctx, c_ctx, w_mod, b_mod, w_in, qk_gain, rpb, conv_w, a_log, dt_bias, o_gain, w_branch, w_out,
           ln1_g, ln1_b, w_router, w_gate_up, w_down, ln2_g, ln2_b):
    b, n_lat, d = x.shape
    m_ctx = ctx.shape[1]
    depth = w_mod.shape[0]
    alpha = (2 * depth) ** 0.25
    tm = math.gcd(256, math.gcd(n_lat, m_ctx))
    blk = tm

    xa = jnp.concatenate([x, ctx], axis=1)
    n_rows = -(-(b + 1) // SUBLANE) * SUBLANE
    cc = jnp.zeros((n_rows, d), F32).at[:b].set(c).at[b].set(c_ctx)
    mod_all = _modulation(cc, w_mod, b_mod).reshape(depth, n_rows, N_MOD, d)
    cos, sin = _rope_tables(n_lat, m_ctx)
    gs_mean = _group_sum_matrix(A_Q_W, 1.0 / HEAD_DIM)
    gs_sum = _group_sum_matrix(C_W, 1.0)
    na_geoms, na_table = _na_classes(n_lat, n_lat + m_ctx)

    offs = np.cumsum((0, A_Q_W, A_KV_W, A_KV_W, B_W, B_W, B_W, 3 * C_W, C_W, 2 * C_HEADS, 2 * C_HEADS, N_BRANCH * d))
    for layer in range(depth):
        ml = mod_all[layer]
        mods = jnp.stack([ml[:b], jnp.broadcast_to(ml[b][None], (b, N_MOD, d))], axis=1)
        wl = w_in[layer]
        wa = wl[:, offs[0]:offs[3]].astype(BF16)
        wb = wl[:, offs[3]:offs[6]].astype(BF16)
        wc = wl[:, offs[6]:offs[8]].astype(BF16)
        wab = jnp.pad(wl[:, offs[8]:offs[10]], ((0, 0), (0, LANE - 4 * C_HEADS))).astype(BF16)
        wmg = wl[:, offs[10]:offs[11]].astype(BF16)
        gq = jnp.tile(qk_gain[layer, 0], A_HEADS)[None]
        gk = jnp.tile(qk_gain[layer, 1], A_KV_HEADS)[None]
        qa, ka, va, qb, kb, vb, qkvc, gate_c, ab = _in_projection(
            xa, mods, wa, wb, wc, wab, gq, gk, cos, sin, gs_mean, n_lat, tm)
        oa = _gqa(qa, ka, va, n_lat, min(GQA_TQ, tm), 2 if (n_lat + m_ctx) % (2 * LANE) == 0 else 1)
        ob = _neighborhood(qb, kb, vb, _na_bias_slabs(rpb[layer], na_geoms), na_table, n_lat)
        qc, kc, vc, comp = _gdn_prep(qkvc, ab, conv_w[layer], a_log[layer], dt_bias[layer], gs_sum, n_lat, tm)
        u, w, kd, qd, intra = _gdn_local(qc, kc, vc, comp, blk)
        oc0, oc1 = _gdn_scan(u, w, kd, qd, intra, comp, n_lat, blk)
        og = jnp.tile(o_gain[layer], C_HEADS)[None]
        xa, h, aff = _merge(xa, mods, oa, ob, oc0, oc1, gate_c, wmg, w_branch[layer].astype(BF16),
                            w_out[layer].astype(BF16), og, gs_mean, ln1_g[layer][None], ln1_b[layer][None],
                            w_router[layer], n_lat, tm, alpha)
        xa = _moe(xa, h, aff, mods, w_router.shape[-1], w_gate_up, w_down, ln2_g[layer][None], ln2_b[layer][None],
                  layer, n_lat, tm, alpha)
    return xa[:, :n_lat]
```

```python
import functools
import math

import numpy as np
import jax
import jax.numpy as jnp
from jax import lax
from jax.experimental import pallas as pl
from jax.experimental.pallas import tpu as pltpu

F32 = jnp.float32
BF16 = jnp.bfloat16
HIGHEST = lax.Precision.HIGHEST

GRID_W = 64
HEAD_DIM = 64
A_HEADS = 8
A_KV_HEADS = 2
B_HEADS = 8
C_HEADS = 8
NA_ROWS = 8
NA_COLS = 16
CONV_K = 4
CHUNK = 64
N_BRANCH = 3
BRANCH_W = 512
CAPACITY = 2
N_MOD = 6
LN_EPS = 1e-6
ROPE_THETA = 10000.0
NEG_BIG = -1e30
LOG2E = 1.4426950408889634

A_Q_W = A_HEADS * HEAD_DIM
A_KV_W = A_KV_HEADS * HEAD_DIM
B_W = B_HEADS * HEAD_DIM
C_W = C_HEADS * HEAD_DIM
LANE = 128
SUBLANE = 8
V7X_VMEM_BYTES = 64 * 1024 * 1024
VMEM_LIMIT = V7X_VMEM_BYTES * 7 // 8
GQA_TQ = 128
LOCAL_PAIRS = 2
SCAN_BATCH = 2
VT_ROWS = HEAD_DIM + 16
NA_WIN = NA_ROWS + 2


def _cparams(sem):
    return pltpu.CompilerParams(dimension_semantics=sem, vmem_limit_bytes=VMEM_LIMIT)


def _dot(a, b):
    return jnp.dot(a, b, preferred_element_type=F32)


def _dot_hi(a, b):
    return jnp.dot(a, b, precision=HIGHEST, preferred_element_type=F32)


def _dot_nt(a, b):
    return lax.dot_general(a, b, (((1,), (1,)), ((), ())), preferred_element_type=F32)


def _dot_tn(a, b):
    return lax.dot_general(a, b, (((0,), (0,)), ((), ())), preferred_element_type=F32)


def _split_bf16(x, pieces):
    out = []
    for _ in range(pieces - 1):
        h = x.astype(BF16)
        out.append(h)
        x = x - h.astype(F32)
    out.append(x.astype(BF16))
    return out


def _dot_exact_rhs(a, b, pieces=3):
    return sum(_dot(p, b) for p in _split_bf16(a, pieces))


def _dot_exact_lhs(a, b, pieces=3):
    return sum(_dot(a, p) for p in _split_bf16(b, pieces))


def _ln(x):
    mu = jnp.mean(x, axis=-1, keepdims=True)
    xc = x - mu
    var = jnp.mean(xc * xc, axis=-1, keepdims=True)
    return xc * lax.rsqrt(var + LN_EPS)


def _sigmoid(x):
    return 1.0 / (1.0 + jnp.exp(-x))


def _silu(x):
    return x * _sigmoid(x)


def _group_sum_matrix(width, scale):
    g = np.arange(width) // HEAD_DIM
    return jnp.asarray((g[:, None] == g[None, :]).astype(np.float32) * scale, dtype=BF16)


def _mod_kernel(c_ref, w_ref, b_ref, o_ref):
    c = c_ref[...]
    o_ref[0] = _dot_hi(_silu(c), w_ref[0]) + b_ref[0]


def _modulation(cc, w_mod, b_mod):
    depth, d, dm = w_mod.shape
    r = cc.shape[0]
    tn = min(dm, 1536)
    return pl.pallas_call(
        _mod_kernel,
        grid=(depth, dm // tn),
        in_specs=[pl.BlockSpec((r, d), lambda l, j: (0, 0)),
                  pl.BlockSpec((1, d, tn), lambda l, j: (l, 0, j)),
                  pl.BlockSpec((1, 1, tn), lambda l, j: (l, 0, j))],
        out_specs=pl.BlockSpec((1, r, tn), lambda l, j: (l, 0, j)),
        out_shape=jax.ShapeDtypeStruct((depth, r, dm), F32),
        compiler_params=_cparams(("parallel", "parallel")),
        name="modulation",
    )(cc, w_mod, b_mod.reshape(depth, 1, dm))


def _rope_swap(y):
    w = y.shape[-1]
    lane = lax.broadcasted_iota(jnp.int32, y.shape, 1)
    up = pltpu.roll(y, w - 16, 1)
    down = pltpu.roll(y, 16, 1)
    return jnp.where((lane % 32) < 16, up, down)


def _inproj_kernel(x_ref, mod_ref, wa_ref, wb_ref, wc_ref, wab_ref, gq_ref, gk_ref, cos_ref, sin_ref,
                   gs_ref, qa_ref, ka_ref, va_ref, qb_ref, kb_ref, vb_ref, qkvc_ref, gatec_ref, ab_ref):
    x = x_ref[0]
    mod = mod_ref[0, 0]
    h = (_ln(x) * (1.0 + mod[1:2]) + mod[0:1]).astype(BF16)
    scale = HEAD_DIM ** -0.5 * LOG2E

    za = _dot(h, wa_ref[...])
    q = za[:, :A_Q_W]
    k = za[:, A_Q_W:A_Q_W + A_KV_W]
    v = za[:, A_Q_W + A_KV_W:]
    cos = cos_ref[...]
    sin = sin_ref[...]
    gs = gs_ref[...]
    qn = q * lax.rsqrt(_dot_exact_rhs(q * q, gs, 2) + LN_EPS) * gq_ref[...]
    qn = qn * cos + _rope_swap(qn) * sin
    qa_ref[0] = (qn * scale).astype(BF16)
    kn = k * lax.rsqrt(_dot_exact_rhs(k * k, gs[:A_KV_W, :A_KV_W], 2) + LN_EPS) * gk_ref[...]
    kn = kn * cos[:, :A_KV_W] + _rope_swap(kn) * sin[:, :A_KV_W]
    ones = jnp.ones((VT_ROWS - HEAD_DIM, x.shape[0]), F32)
    vt = v.T
    for j in range(A_KV_HEADS):
        ka_ref[0, j] = kn[:, j * HEAD_DIM:(j + 1) * HEAD_DIM].astype(BF16)
        va_ref[0, j] = jnp.concatenate([vt[j * HEAD_DIM:(j + 1) * HEAD_DIM], ones], axis=0).astype(BF16)

    zb = _dot(h, wb_ref[...])
    qb_ref[0] = (zb[:, :B_W] * scale).astype(BF16)
    kbv = zb[:, B_W:2 * B_W]
    vt = zb[:, 2 * B_W:].T
    for j in range(B_HEADS):
        kb_ref[0, j] = kbv[:, j * HEAD_DIM:(j + 1) * HEAD_DIM].astype(BF16)
        vb_ref[0, j] = jnp.concatenate([vt[j * HEAD_DIM:(j + 1) * HEAD_DIM], ones], axis=0).astype(BF16)

    zc = _dot(h, wc_ref[...])
    qkvc_ref[0] = zc[:, :3 * C_W]
    gatec_ref[0] = zc[:, 3 * C_W:]
    ab_ref[0] = _dot(h, wab_ref[...])


def _in_projection(xa, mods, wa, wb, wc, wab, gq, gk, cos, sin, gs, n_lat, tm):
    b, t, d = xa.shape
    nl = n_lat // tm
    row = lambda bi, i: (bi, i, 0)
    const = lambda bi, i: (0, 0)
    outs = [
        (jax.ShapeDtypeStruct((b, t, A_Q_W), BF16), pl.BlockSpec((1, tm, A_Q_W), row)),
        (jax.ShapeDtypeStruct((b, A_KV_HEADS, t, HEAD_DIM), BF16),
         pl.BlockSpec((1, A_KV_HEADS, tm, HEAD_DIM), lambda bi, i: (bi, 0, i, 0))),
        (jax.ShapeDtypeStruct((b, A_KV_HEADS, VT_ROWS, t), BF16),
         pl.BlockSpec((1, A_KV_HEADS, VT_ROWS, tm), lambda bi, i: (bi, 0, 0, i))),
        (jax.ShapeDtypeStruct((b, t, B_W), BF16), pl.BlockSpec((1, tm, B_W), row)),
        (jax.ShapeDtypeStruct((b, B_HEADS, t, HEAD_DIM), BF16),
         pl.BlockSpec((1, B_HEADS, tm, HEAD_DIM), lambda bi, i: (bi, 0, i, 0))),
        (jax.ShapeDtypeStruct((b, B_HEADS, VT_ROWS, t), BF16),
         pl.BlockSpec((1, B_HEADS, VT_ROWS, tm), lambda bi, i: (bi, 0, 0, i))),
        (jax.ShapeDtypeStruct((b, t, 3 * C_W), F32), pl.BlockSpec((1, tm, 3 * C_W), row)),
        (jax.ShapeDtypeStruct((b, t, C_W), F32), pl.BlockSpec((1, tm, C_W), row)),
        (jax.ShapeDtypeStruct((b, t, LANE), F32), pl.BlockSpec((1, tm, LANE), row)),
    ]
    return pl.pallas_call(
        _inproj_kernel,
        grid=(b, t // tm),
        in_specs=[pl.BlockSpec((1, tm, d), row),
                  pl.BlockSpec((1, 1, N_MOD, d), lambda bi, i: (bi, (i >= nl).astype(jnp.int32), 0, 0)),
                  pl.BlockSpec(wa.shape, const), pl.BlockSpec(wb.shape, const),
                  pl.BlockSpec(wc.shape, const), pl.BlockSpec(wab.shape, const),
                  pl.BlockSpec(gq.shape, const), pl.BlockSpec(gk.shape, const),
                  pl.BlockSpec((tm, A_Q_W), lambda bi, i: (i, 0)),
                  pl.BlockSpec((tm, A_Q_W), lambda bi, i: (i, 0)),
                  pl.BlockSpec(gs.shape, const)],
        out_specs=[o[1] for o in outs],
        out_shape=[o[0] for o in outs],
        compiler_params=_cparams(("parallel", "parallel")),
        name="in_projection",
    )(xa, mods, wa, wb, wc, wab, gq, gk, cos, sin, gs)


def _gqa_kernel(q_ref, k_ref, v_ref, o_ref, *, tq, ck, n_lat, n_tot):
    i = pl.program_id(1)
    g = A_HEADS // A_KV_HEADS

    def attend(key_lo, key_hi, n_split):
        qs = [jnp.concatenate([q_ref[0, :, (kh * g + j) * HEAD_DIM:(kh * g + j + 1) * HEAD_DIM] for j in range(g)],
                              axis=0) for kh in range(A_KV_HEADS)]
        step = (key_hi - key_lo) // n_split
        units = [(kh, key_lo + r * step, key_lo + (r + 1) * step) for r in range(n_split) for kh in range(A_KV_HEADS)]
        ss, ms, ps, oes = {}, {}, {}, {}
        for n in range(len(units) + 2):
            if n < len(units):
                kh, lo, hi = units[n]
                ss[n] = _dot_nt(k_ref[0, kh, lo:hi, :], qs[kh])
            if 0 <= n - 1 < len(units):
                ms[n - 1] = jnp.max(ss[n - 1], axis=0, keepdims=True).astype(BF16).astype(F32)
                ps[n - 1] = jnp.exp2(ss.pop(n - 1).astype(BF16) - ms[n - 1].astype(BF16))
            if 0 <= n - 2 < len(units):
                kh, lo, hi = units[n - 2]
                oes[n - 2] = _dot(v_ref[0, kh, :, lo:hi], ps.pop(n - 2))
        for kh in range(A_KV_HEADS):
            mine = [n for n, u in enumerate(units) if u[0] == kh]
            m = functools.reduce(jnp.maximum, [ms[n] for n in mine])
            oe = sum(oes[n] * jnp.exp2(ms[n] - m) for n in mine)
            o = (oe[:HEAD_DIM] / oe[HEAD_DIM:HEAD_DIM + 1]).T.astype(BF16)
            for j in range(g):
                o_ref[0, :, (kh * g + j) * HEAD_DIM:(kh * g + j + 1) * HEAD_DIM] = o[j * tq:(j + 1) * tq]

    @pl.when(i < n_lat // tq)
    def _():
        attend(0, n_tot, ck)

    @pl.when(i >= n_lat // tq)
    def _():
        attend(n_lat, n_tot, 1)


def _gqa(qa, ka, va, n_lat, tq, ck):
    b, t, _ = qa.shape
    return pl.pallas_call(
        functools.partial(_gqa_kernel, tq=tq, ck=ck, n_lat=n_lat, n_tot=t),
        grid=(b, t // tq),
        in_specs=[pl.BlockSpec((1, tq, A_Q_W), lambda bi, i: (bi, i, 0)),
                  pl.BlockSpec((1, A_KV_HEADS, t, HEAD_DIM), lambda bi, i: (bi, 0, 0, 0)),
                  pl.BlockSpec((1, A_KV_HEADS, VT_ROWS, t), lambda bi, i: (bi, 0, 0, 0))],
        out_specs=pl.BlockSpec((1, tq, A_Q_W), lambda bi, i: (bi, i, 0)),
        out_shape=jax.ShapeDtypeStruct((b, t, A_Q_W), BF16),
        compiler_params=_cparams(("parallel", "parallel")),
        name="gqa",
    )(qa, ka, va)


def _na_window_start(r0, rows):
    return np.clip(r0 - NA_ROWS // 2, 0, rows - NA_ROWS - 1) // 2 * 2


def _na_classes(n_lat, n_tot):
    rows = n_lat // GRID_W
    assert rows >= NA_WIN and rows % 2 == 0
    geoms, table = [], []
    for r0 in range(0, rows, 2):
        geom = (int(_na_window_start(r0, rows)) - r0,) + tuple(
            int(np.clip(r0 + qr - NA_ROWS // 2, 0, rows - NA_ROWS)) - r0 for qr in range(2))
        if geom not in geoms:
            geoms.append(geom)
        table.append(geoms.index(geom))
    table += [len(geoms)] * ((n_tot - n_lat) // (2 * GRID_W))
    return geoms, np.asarray(table, np.int32)


def _na_bias_slabs(rpb, geoms):
    n_head = rpb.shape[0]
    cj = np.arange(GRID_W)[:, None]
    c = np.arange(GRID_W)[None, :]
    col_start = np.clip(c - NA_COLS // 2, 0, GRID_W - NA_COLS)
    col_ok = (cj >= col_start) & (cj < col_start + NA_COLS)
    col_off = np.clip(cj - c + (NA_COLS - 1), 0, 2 * NA_COLS - 2)
    spread = np.zeros((2 * NA_COLS - 1, GRID_W * GRID_W), np.float32)
    spread[col_off.reshape(-1), np.arange(GRID_W * GRID_W)] = 1.0
    kr = np.arange(NA_WIN)[:, None]
    qr = np.arange(2)[None, :]
    pick = np.zeros((len(geoms), NA_WIN, 2, 2 * NA_ROWS - 1), np.float32)
    ok = np.zeros((len(geoms), 1, NA_WIN, 2, GRID_W, GRID_W), bool)
    for n, (a, b0, b1) in enumerate(geoms):
        first = np.where(qr == 0, b0, b1)
        row_ok = (a + kr >= first) & (a + kr < first + NA_ROWS)
        row_off = np.clip(a + kr - qr + (NA_ROWS - 1), 0, 2 * NA_ROWS - 2)
        pick[n, kr, qr, row_off] = 1.0
        ok[n, 0] = row_ok[:, :, None, None] & col_ok[None, None]
    rows = jnp.einsum("nkqo,hoc->nhkqc", jnp.asarray(pick), rpb * LOG2E, precision=HIGHEST)
    full = jnp.dot(rows, jnp.asarray(spread), precision=HIGHEST)
    full = full.reshape(len(geoms), n_head, NA_WIN, 2, GRID_W, GRID_W)
    masked = jnp.where(jnp.asarray(ok), full, NEG_BIG)
    slabs = jnp.concatenate([masked[:, :, :, 0], masked[:, :, :, 1]], axis=-1)
    slabs = slabs.reshape(len(geoms), n_head, NA_WIN * GRID_W, 2 * GRID_W)
    return jnp.concatenate([slabs, jnp.full_like(slabs[:1], NEG_BIG)], axis=0).astype(F32)


def _na_kernel(cls_ref, q_ref, k_ref, vt_ref, bias_ref, o_ref, *, n_lat, n_tot):
    del cls_ref
    rows = n_lat // GRID_W
    r0 = 2 * pl.program_id(1)
    start = jnp.clip(r0 - NA_ROWS // 2, 0, rows - NA_ROWS - 1) // 2 * 2
    nb = pl.ds(pl.multiple_of(start * GRID_W, 2 * GRID_W), NA_WIN * GRID_W)
    hs = [slice(h * HEAD_DIM, (h + 1) * HEAD_DIM) for h in range(B_HEADS)]
    qs = [q_ref[0, :, hs[h]] for h in range(B_HEADS)]
    s_nb = [_dot_nt(k_ref[0, h, nb, :], qs[h]) + bias_ref[0, h] for h in range(B_HEADS)]
    s_cx = [_dot_nt(k_ref[0, h, n_lat:n_tot, :], qs[h]) for h in range(B_HEADS)]
    ms = [jnp.maximum(jnp.max(a, axis=0, keepdims=True), jnp.max(c, axis=0, keepdims=True))
          for a, c in zip(s_nb, s_cx)]
    p_nb = [jnp.exp2((a - m).astype(BF16)) for a, m in zip(s_nb, ms)]
    p_cx = [jnp.exp2((c - m).astype(BF16)) for c, m in zip(s_cx, ms)]
    for h in range(B_HEADS):
        oe = _dot(vt_ref[0, h, :, nb], p_nb[h]) + _dot(vt_ref[0, h, :, n_lat:n_tot], p_cx[h])
        o_ref[0, :, hs[h]] = (oe[:HEAD_DIM] / oe[HEAD_DIM:HEAD_DIM + 1]).T.astype(BF16)


def _neighborhood(qb, kb, vbt, slabs, table, n_lat):
    b, t, _ = qb.shape
    tq = 2 * GRID_W
    grid_spec = pltpu.PrefetchScalarGridSpec(
        num_scalar_prefetch=1,
        grid=(b, t // tq),
        in_specs=[pl.BlockSpec((1, tq, B_W), lambda bi, pi, cls: (bi, pi, 0)),
                  pl.BlockSpec((1, B_HEADS, t, HEAD_DIM), lambda bi, pi, cls: (bi, 0, 0, 0)),
                  pl.BlockSpec((1, B_HEADS, VT_ROWS, t), lambda bi, pi, cls: (bi, 0, 0, 0)),
                  pl.BlockSpec((1,) + slabs.shape[1:], lambda bi, pi, cls: (cls[pi], 0, 0, 0))],
        out_specs=pl.BlockSpec((1, tq, B_W), lambda bi, pi, cls: (bi, pi, 0)),
    )
    return pl.pallas_call(
        functools.partial(_na_kernel, n_lat=n_lat, n_tot=t),
        grid_spec=grid_spec,
        out_shape=jax.ShapeDtypeStruct((b, t, B_W), BF16),
        compiler_params=_cparams(("parallel", "parallel")),
        name="neighborhood",
    )(jnp.asarray(table), qb, kb, vbt, slabs)


GC_LANE, GL_LANE, BETA_LANE = 0, 2 * C_HEADS, 4 * C_HEADS


def _gdn_prep_kernel(x_ref, prev_ref, next_ref, ab_ref, cw_ref, alog_ref, dtb_ref, gs_ref,
                     q_ref, k_ref, v_ref, comp_ref, *, tm, n_lat, n_tot):
    i = pl.program_id(1)
    nl = n_lat // tm
    nt = n_tot // tm
    first = jnp.logical_or(i == 0, i == nl)
    last = jnp.logical_or(i == nl - 1, i == nt - 1)
    x = x_ref[0]
    pv = jnp.where(first, 0.0, prev_ref[0])
    nx = jnp.where(last, 0.0, next_ref[0])
    row = lax.broadcasted_iota(jnp.int32, (tm, 1), 0)
    prev1 = pv[SUBLANE - 1:SUBLANE]
    prev2 = pv[SUBLANE - 2:SUBLANE - 1]
    xm1 = jnp.where(row == 0, prev1, pltpu.roll(x, 1, 0))
    xm2 = jnp.where(row == 0, prev2, jnp.where(row == 1, prev1, pltpu.roll(x, 2, 0)))
    xp1 = jnp.where(row == tm - 1, nx[0:1], pltpu.roll(x, tm - 1, 0))
    cw = cw_ref[...]
    y = _silu(xm2 * cw[0:1] + xm1 * cw[1:2] + x * cw[2:3] + xp1 * cw[3:4])
    q = y[:, :C_W]
    k = y[:, C_W:2 * C_W]
    gs = gs_ref[...]
    q_ref[0] = q * lax.rsqrt(_dot_exact_rhs(q * q, gs, 2) + LN_EPS) * (HEAD_DIM ** -0.5)
    k_ref[0] = k * lax.rsqrt(_dot_exact_rhs(k * k, gs, 2) + LN_EPS)
    v_ref[0] = y[:, 2 * C_W:]
    ab = ab_ref[0]
    z = ab + dtb_ref[...]
    softplus = jnp.maximum(z, 0.0) + jnp.log(1.0 + jnp.exp(-jnp.abs(z)))
    lane = lax.broadcasted_iota(jnp.int32, ab.shape, 1)
    gb = jnp.where(lane < 2 * C_HEADS, -jnp.exp(alog_ref[...]) * softplus, _sigmoid(ab))
    ii = lax.broadcasted_iota(jnp.int32, (tm, tm), 0)
    jj = lax.broadcasted_iota(jnp.int32, (tm, tm), 1)
    same = (ii // CHUNK) == (jj // CHUNK)
    gb3 = _split_bf16(gb, 3)

    def masked_sum(mask):
        mb = jnp.where(mask, 1.0, 0.0).astype(BF16)
        return sum(_dot(mb, piece) for piece in gb3)

    prefix = masked_sum(jnp.logical_and(same, jj <= ii))
    suffix = masked_sum(jnp.logical_and(same, jj >= ii))
    total = masked_sum(same)
    comp = jnp.where(lane < C_HEADS, prefix,
                     jnp.where(lane < GL_LANE, suffix,
                               jnp.where(lane < BETA_LANE, pltpu.roll(total, GL_LANE, 1),
                                         jnp.where(lane < BETA_LANE + 2 * C_HEADS, pltpu.roll(gb, 2 * C_HEADS, 1),
                                                   0.0))))
    comp_ref[0] = comp


def _gdn_prep(qkvc, ab, conv_w, a_log, dt_bias, gs, n_lat, tm):
    b, t, _ = qkvc.shape
    c3 = 3 * C_W
    alog_row = jnp.zeros((1, LANE), F32).at[0, :2 * C_HEADS].set(a_log.reshape(-1))
    dtb_row = jnp.zeros((1, LANE), F32).at[0, :2 * C_HEADS].set(dt_bias.reshape(-1))
    row = lambda bi, i: (bi, i, 0)
    const2 = lambda bi, i: (0, 0)
    hb = tm // SUBLANE
    last8 = t // SUBLANE - 1
    tok = jax.ShapeDtypeStruct((b, t, C_W), F32)
    return pl.pallas_call(
        functools.partial(_gdn_prep_kernel, tm=tm, n_lat=n_lat, n_tot=t),
        grid=(b, t // tm),
        in_specs=[pl.BlockSpec((1, tm, c3), row),
                  pl.BlockSpec((1, SUBLANE, c3), lambda bi, i: (bi, jnp.maximum(i * hb - 1, 0), 0)),
                  pl.BlockSpec((1, SUBLANE, c3), lambda bi, i: (bi, jnp.minimum((i + 1) * hb, last8), 0)),
                  pl.BlockSpec((1, tm, LANE), row),
                  pl.BlockSpec((CONV_K, c3), const2),
                  pl.BlockSpec((1, LANE), const2), pl.BlockSpec((1, LANE), const2),
                  pl.BlockSpec(gs.shape, const2)],
        out_specs=[pl.BlockSpec((1, tm, C_W), row)] * 3 + [pl.BlockSpec((1, tm, LANE), row)],
        out_shape=[tok, tok, tok, jax.ShapeDtypeStruct((b, t, LANE), F32)],
        compiler_params=_cparams(("parallel", "parallel")),
        name="gdn_prep",
    )(qkvc, qkvc, qkvc, ab, conv_w, alog_row, dtb_row, gs)


def _unit_tri_inverses(lms, eye, bd16, bd32):
    def mm(a, b_):
        return _dot(a.astype(BF16), b_.astype(BF16))

    lds = [jnp.where(bd16, lm, 0.0) for lm in lms]
    xs = [eye - ld for ld in lds]
    lds = [ld.astype(BF16) for ld in lds]
    pws = [_dot(ld, ld).astype(BF16) for ld in lds]
    for step in range(3):
        xs = [x + mm(x, pw) for x, pw in zip(xs, pws)]
        if step < 2:
            pws = [_dot(pw, pw).astype(BF16) for pw in pws]
    for level in (lambda lm: jnp.where(jnp.logical_and(bd32, jnp.logical_not(bd16)), lm, 0.0),
                  lambda lm: jnp.where(bd32, 0.0, lm)):
        xbs = [x.astype(BF16) for x in xs]
        ys = [_dot(level(lm).astype(BF16), xb) for lm, xb in zip(lms, xbs)]
        xs = [x - _dot(xb, y.astype(BF16)) for x, xb, y in zip(xs, xbs, ys)]
    return xs


def _gdn_local_kernel(q_ref, k_ref, v_ref, comp_ref, ex_ref, sel_ref, u_ref, w_ref, kd_ref, qd_ref, in_ref):
    n = q_ref.shape[1]
    hp = 2 * HEAD_DIM
    n_pr = q_ref.shape[2] // hp
    ii = lax.broadcasted_iota(jnp.int32, (n, n), 0)
    jj = lax.broadcasted_iota(jnp.int32, (n, n), 1)
    same = (ii // CHUNK) == (jj // CHUNK)
    eye = (ii == jj).astype(F32)
    bd16 = (ii // 16) == (jj // 16)
    bd32 = (ii // 32) == (jj // 32)
    lane = lax.broadcasted_iota(jnp.int32, (1, hp), 1)
    lane2 = lax.broadcasted_iota(jnp.int32, (1, 2 * hp), 1)
    incls = [jnp.logical_and(same, ii >= jj), jnp.logical_and(same, ii <= jj)]
    stricts = [jnp.logical_and(same, ii > jj), jnp.logical_and(same, ii < jj)]
    lms, intras, rhss = [], [], []
    for pr in range(n_pr):
        ls = slice(pr * hp, (pr + 1) * hp)
        q = q_ref[0, :, ls]
        k = k_ref[0, :, ls]
        v = v_ref[0, :, ls]
        e = _dot_exact_rhs(comp_ref[0], ex_ref[pr])
        kh = [jnp.where((lane // HEAD_DIM) == hh, k, 0.0).astype(BF16) for hh in range(2)]
        for p in range(2):
            gc = e[:, p * hp:(p + 1) * hp]
            gl = e[:, (2 + p) * hp:(3 + p) * hp]
            beta = e[:, (4 + p) * hp:(5 + p) * hp]
            eg = jnp.exp(gc)
            kb = k * beta
            kd_ref[p, 0, :, ls] = (k * jnp.exp(gl - gc)).astype(BF16)
            qd_ref[p, 0, :, ls] = (q * eg).astype(BF16)
            rhs = jnp.concatenate([v * beta, kb * eg], axis=1)
            kbq = jnp.concatenate([kb, q], axis=0).astype(BF16)
            for hh in range(2):
                gcb = jnp.broadcast_to(gc[:, hh * HEAD_DIM:hh * HEAD_DIM + 1], (n, n))
                decay = jnp.exp(jnp.where(incls[p], gcb - gcb.T, NEG_BIG))
                aq = _dot_nt(kbq, kh[hh])
                lms.append(jnp.where(stricts[p], aq[:n] * decay, 0.0))
                intras.append(jnp.where(incls[p], aq[n:] * decay, 0.0).astype(BF16))
                rhss.append(jnp.where(((lane2 % hp) // HEAD_DIM) == hh, rhs, 0.0).astype(BF16))
    tmats = _unit_tri_inverses(lms, eye, bd16, bd32)
    for pr in range(n_pr):
        ls = slice(pr * hp, (pr + 1) * hp)
        for p in range(2):
            at = 4 * pr + 2 * p
            uw = sum(_dot(tmats[at + hh].astype(BF16), rhss[at + hh]) for hh in range(2))
            intra2 = sum(_dot(intras[at + hh], sel_ref[hh]) for hh in range(2))
            u_ref[p, 0, :, ls] = uw[:, :hp].astype(BF16)
            w_ref[p, 0, :, ls] = uw[:, hp:].astype(BF16)
            in_ref[p, 0, :, ls] = intra2.astype(BF16)


def _gdn_local(q, k, v, comp, blk):
    b, t, _ = q.shape
    hp = 2 * HEAD_DIM
    n_pair = C_W // hp
    ex = np.zeros((n_pair, LANE, 6 * hp), np.float32)
    for pr in range(n_pair):
        for p in range(2):
            for hh in range(2):
                src = p * C_HEADS + pr * 2 + hh
                for kind, base in enumerate((GC_LANE, GL_LANE, BETA_LANE)):
                    col = (2 * kind + p) * hp + hh * HEAD_DIM
                    ex[pr, base + src, col:col + HEAD_DIM] = 1.0
    sel = np.zeros((2, blk, hp), np.float32)
    for hh in range(2):
        sel[hh, np.arange(blk), hh * HEAD_DIM + np.arange(blk) % CHUNK] = 1.0
    gp = LOCAL_PAIRS
    tok = lambda bi, pg, i: (bi, i, pg)
    dirs = lambda bi, pg, i: (0, bi, i, pg)
    out = jax.ShapeDtypeStruct((2, b, t, C_W), BF16)
    return pl.pallas_call(
        _gdn_local_kernel,
        grid=(b, n_pair // gp, t // blk),
        in_specs=[pl.BlockSpec((1, blk, gp * hp), tok)] * 3
                 + [pl.BlockSpec((1, blk, LANE), lambda bi, pg, i: (bi, i, 0)),
                    pl.BlockSpec((gp, LANE, 6 * hp), lambda bi, pg, i: (pg, 0, 0)),
                    pl.BlockSpec((2, blk, hp), lambda bi, pg, i: (0, 0, 0))],
        out_specs=[pl.BlockSpec((2, 1, blk, gp * hp), dirs)] * 5,
        out_shape=[out] * 5,
        compiler_params=_cparams(("parallel", "parallel", "parallel")),
        name="gdn_local",
    )(q, k, v, comp, jnp.asarray(ex, dtype=BF16), jnp.asarray(sel, dtype=BF16))


def _gdn_scan_kernel(*refs, ncb, nb):
    ins = (refs[0:6], refs[6:12])
    ex_ref = refs[12]
    outs = refs[13:15]
    s_ref = refs[15]
    j = pl.program_id(1)

    @pl.when(j == 0)
    def _():
        s_ref[...] = jnp.zeros_like(s_ref)

    gw = 4 * HEAD_DIM
    n_grp = C_W // gw
    lane_head = lax.broadcasted_iota(jnp.int32, (1, gw), 1) // HEAD_DIM
    rr = lax.broadcasted_iota(jnp.int32, (gw, gw), 0) // HEAD_DIM
    cc_ = lax.broadcasted_iota(jnp.int32, (gw, gw), 1) // HEAD_DIM
    blockdiag = rr == cc_
    chains = [(bi, p, g) for bi in range(nb) for p in range(2) for g in range(n_grp)]
    for cc in range(ncb):
        rows = [slice(cc * CHUNK, (cc + 1) * CHUNK), slice((ncb - 1 - cc) * CHUNK, (ncb - cc) * CHUNK)]
        sdec = {(bi, p): jnp.exp(_dot_exact_rhs(ins[p][5][bi, rows[p].start:rows[p].start + SUBLANE, :], ex_ref[...]))
                for bi in range(nb) for p in range(2)}
        lanes = [slice(g * gw, (g + 1) * gw) for g in range(n_grp)]
        ss = [s_ref[bi, p, g] for bi, p, g in chains]
        sbs = [s.astype(BF16) for s in ss]
        v_news = [ins[p][0][0, bi, rows[p], lanes[g]].astype(F32) - _dot(ins[p][1][0, bi, rows[p], lanes[g]], sb)
                  for (bi, p, g), sb in zip(chains, sbs)]
        vstacks = [jnp.concatenate([jnp.where(lane_head == h, v_new, 0.0).astype(BF16)
                                    for h in range(gw // HEAD_DIM)], axis=0) for v_new in v_news]
        for (bi, p, g), s, sb, v_new, vstack in zip(chains, ss, sbs, v_news, vstacks):
            o = _dot(ins[p][3][0, bi, rows[p], lanes[g]], sb) + _dot(ins[p][4][0, bi, rows[p], lanes[g]], vstack)
            upd = _dot_tn(ins[p][2][0, bi, rows[p], lanes[g]], v_new.astype(BF16))
            decay = sdec[bi, p][0:1, p * C_W + g * gw:p * C_W + (g + 1) * gw]
            s_ref[bi, p, g] = s * decay + jnp.where(blockdiag, upd, 0.0)
            outs[p][bi, rows[p], lanes[g]] = o


def _gdn_scan(u, w, kd, qd, intra, comp, n_lat, blk):
    _, b, t, _ = u.shape
    nl = n_lat // blk
    nx = (t - n_lat) // blk
    ncb = blk // CHUNK
    ex = np.zeros((LANE, 2 * C_W), np.float32)
    for p in range(2):
        for h in range(C_HEADS):
            ex[GL_LANE + p * C_HEADS + h, p * C_W + h * HEAD_DIM:p * C_W + (h + 1) * HEAD_DIM] = 1.0

    def blk0(j):
        return jnp.where(j < nx, nl + j, j - nx)

    def blk1(j):
        return jnp.where(j < nx, nl + nx - 1 - j, nl - 1 - (j - nx))

    nb = SCAN_BATCH if b % SCAN_BATCH == 0 else 1
    specs = []
    for p, bf in enumerate((blk0, blk1)):
        specs += [pl.BlockSpec((1, nb, blk, C_W), lambda bi, j, p=p, bf=bf: (p, bi, bf(j), 0))] * 5
        specs += [pl.BlockSpec((nb, blk, LANE), lambda bi, j, bf=bf: (bi, bf(j), 0))]
    specs += [pl.BlockSpec((LANE, 2 * C_W), lambda bi, j: (0, 0))]
    out = jax.ShapeDtypeStruct((b, t, C_W), F32)
    gw = 4 * HEAD_DIM
    return pl.pallas_call(
        functools.partial(_gdn_scan_kernel, ncb=ncb, nb=nb),
        grid=(b // nb, nl + nx),
        in_specs=specs,
        out_specs=[pl.BlockSpec((nb, blk, C_W), lambda bi, j: (bi, blk0(j), 0)),
                   pl.BlockSpec((nb, blk, C_W), lambda bi, j: (bi, blk1(j), 0))],
        out_shape=[out, out],
        scratch_shapes=[pltpu.VMEM((nb, 2, C_W // gw, gw, gw), F32)],
        compiler_params=_cparams(("parallel", "arbitrary")),
        name="gdn_scan",
    )(u, w, kd, qd, intra, comp, u, w, kd, qd, intra, comp, jnp.asarray(ex, dtype=BF16))


def _merge_kernel(x_ref, mod_ref, oa_ref, ob_ref, oc0_ref, oc1_ref, gate_ref, wmg_ref, wbr_ref, wout_ref, og_ref,
                  gs_ref, lng_ref, lnb_ref, wr_ref, o_ref, h_ref, aff_ref, *, alpha, n_exp):
    x = x_ref[0]
    d = x.shape[-1]
    mod = mod_ref[0, 0]
    h = (_ln(x) * (1.0 + mod[1:2]) + mod[0:1]).astype(BF16)
    gates = _sigmoid(_dot(h, wmg_ref[...]))
    oc = oc0_ref[0] + oc1_ref[0]
    ocn = oc * lax.rsqrt(_dot_exact_rhs(oc * oc, gs_ref[...], 2) + LN_EPS) * og_ref[...] * _silu(gate_ref[0])
    m = gates[:, :d] * _dot(oa_ref[0], wbr_ref[0])
    m = m + gates[:, d:2 * d] * _dot(ob_ref[0], wbr_ref[1])
    m = m + gates[:, 2 * d:] * _dot(ocn.astype(BF16), wbr_ref[2])
    mix = _dot(m.astype(BF16), wout_ref[...])
    y = alpha * x + mod[2:3] * mix
    x1 = _ln(y) * lng_ref[...] + lnb_ref[...]
    o_ref[0] = x1
    h2 = _ln(x1) * (1.0 + mod[4:5]) + mod[3:4]
    h_ref[0] = h2.astype(BF16)
    hp = _split_bf16(h2, 3)
    logits = sum(_dot(hp[i], wr_ref[j]) for i, j in ((2, 0), (1, 1), (0, 2), (1, 0), (0, 1), (0, 0)))
    lane = lax.broadcasted_iota(jnp.int32, logits.shape, 1)
    logits = jnp.where(lane < n_exp, logits, NEG_BIG)
    ex = jnp.exp(logits - jnp.max(logits, axis=-1, keepdims=True))
    aff_ref[0] = ex / jnp.sum(ex, axis=-1, keepdims=True)


def _merge(xa, mods, oa, ob, oc0, oc1, gate_c, wmg, wbr, wout, og, gs, lng, lnb, w_router, n_lat, tm, alpha):
    b, t, d = xa.shape
    nl = n_lat // tm
    n_exp = w_router.shape[-1]
    wr_pad = jnp.pad(w_router, ((0, 0), (0, LANE - n_exp)))
    pieces, rest = [], wr_pad
    for _ in range(3):
        piece = lax.reduce_precision(rest, exponent_bits=8, mantissa_bits=7)
        pieces.append(piece.astype(BF16))
        rest = rest - piece
    wr = jnp.stack(pieces)
    row = lambda bi, i: (bi, i, 0)
    const2 = lambda bi, i: (0, 0)
    return pl.pallas_call(
        functools.partial(_merge_kernel, alpha=alpha, n_exp=n_exp),
        grid=(b, t // tm),
        in_specs=[pl.BlockSpec((1, tm, d), row),
                  pl.BlockSpec((1, 1, N_MOD, d), lambda bi, i: (bi, (i >= nl).astype(jnp.int32), 0, 0)),
                  pl.BlockSpec((1, tm, BRANCH_W), row), pl.BlockSpec((1, tm, BRANCH_W), row),
                  pl.BlockSpec((1, tm, BRANCH_W), row), pl.BlockSpec((1, tm, BRANCH_W), row),
                  pl.BlockSpec((1, tm, BRANCH_W), row),
                  pl.BlockSpec(wmg.shape, const2),
                  pl.BlockSpec(wbr.shape, lambda bi, i: (0, 0, 0)),
                  pl.BlockSpec(wout.shape, const2),
                  pl.BlockSpec(og.shape, const2), pl.BlockSpec(gs.shape, const2),
                  pl.BlockSpec(lng.shape, const2), pl.BlockSpec(lnb.shape, const2),
                  pl.BlockSpec(wr.shape, lambda bi, i: (0, 0, 0))],
        out_specs=[pl.BlockSpec((1, tm, d), row), pl.BlockSpec((1, tm, d), row), pl.BlockSpec((1, tm, LANE), row)],
        out_shape=[jax.ShapeDtypeStruct((b, t, d), F32), jax.ShapeDtypeStruct((b, t, d), BF16),
                   jax.ShapeDtypeStruct((b, t, LANE), F32)],
        compiler_params=_cparams(("parallel", "parallel")),
        name="merge",
    )(xa, mods, oa, ob, oc0, oc1, gate_c, wmg, wbr, wout, og, gs, lng, lnb, wr)


def _ffn_kernel(x_ref, wg_ref, wu_ref, wd_ref, gate_ref, o_ref, acc_ref):
    f = pl.program_id(2)

    @pl.when(f == 0)
    def _():
        acc_ref[...] = jnp.zeros_like(acc_ref)

    x = x_ref[0]
    g = _dot(x, wg_ref[0, 0].astype(BF16))
    u = _dot(x, wu_ref[0, 0].astype(BF16))
    a = (_silu(g) * u).astype(BF16)
    acc_ref[...] += _dot(a, wd_ref[0, 0].astype(BF16))

    @pl.when(f == pl.num_programs(2) - 1)
    def _():
        o_ref[0] = (acc_ref[...] * gate_ref[0]).astype(BF16)


def _expert_ffn(xs, gate, w_gate_up, w_down, layer, tm, tf):
    e, r, d = xs.shape
    f = w_down.shape[2]
    nf = f // tf
    return pl.pallas_call(
        _ffn_kernel,
        grid=(e, r // tm, nf),
        in_specs=[pl.BlockSpec((1, tm, d), lambda ei, i, fi: (ei, i, 0)),
                  pl.BlockSpec((1, 1, d, tf), lambda ei, i, fi: (layer, ei, 0, fi)),
                  pl.BlockSpec((1, 1, d, tf), lambda ei, i, fi: (layer, ei, 0, nf + fi)),
                  pl.BlockSpec((1, 1, tf, d), lambda ei, i, fi: (layer, ei, fi, 0)),
                  pl.BlockSpec((1, tm, 1), lambda ei, i, fi: (ei, i, 0))],
        out_specs=pl.BlockSpec((1, tm, d), lambda ei, i, fi: (ei, i, 0)),
        out_shape=jax.ShapeDtypeStruct((e, r, d), BF16),
        scratch_shapes=[pltpu.VMEM((tm, d), F32)],
        compiler_params=_cparams(("parallel", "parallel", "arbitrary")),
        name="expert_ffn",
    )(xs, w_gate_up, w_gate_up, w_down, gate)


VALID, FIRST, LAST = 1, 2, 4


def _combine_plan(tok_of_pair, n_tok, tm, rwin):
    bsz, n_pair = tok_of_pair.shape
    nt = n_tok // tm
    n_win = n_pair // rwin
    n_item = nt + n_win
    order = jnp.argsort(tok_of_pair, axis=1)
    tok = jnp.take_along_axis(tok_of_pair, order, axis=1)
    bounds = jnp.arange(nt + 1, dtype=jnp.int32) * tm
    off = jnp.sum(tok[:, None, :] < bounds[None, :, None], axis=-1, dtype=jnp.int32)
    first = jnp.minimum(off[:, :-1] // rwin, n_win - 1)
    last = jnp.minimum(jnp.maximum(off[:, 1:] - 1, off[:, :-1]) // rwin, n_win - 1)
    n_items = last - first + 1
    cum = jnp.cumsum(n_items, axis=1)
    k = jnp.arange(n_item, dtype=jnp.int32)
    tile = jnp.sum(cum[:, None, :] <= k[None, :, None], axis=-1, dtype=jnp.int32)
    valid = tile < nt
    tile = jnp.minimum(tile, nt - 1)
    end = jnp.take_along_axis(cum, tile, axis=1)
    start = end - jnp.take_along_axis(n_items, tile, axis=1)
    win = jnp.where(valid, jnp.take_along_axis(first, tile, axis=1) + (k[None] - start),
                    jnp.take_along_axis(last, tile, axis=1))
    flags = (valid * VALID + (valid & (k[None] == start)) * FIRST + (valid & (k[None] == end - 1)) * LAST)
    return order, tok, tile, win.astype(jnp.int32), flags.astype(jnp.int32)


def _combine_kernel(tile_ref, win_ref, flag_ref, y_ref, tok_ref, x_ref, mod_ref, lng_ref, lnb_ref, o_ref, acc_ref, *,
                    alpha, tm):
    b = pl.program_id(0)
    k = pl.program_id(1)
    flags = flag_ref[b, k]

    @pl.when((flags & FIRST) != 0)
    def _():
        acc_ref[...] = jnp.zeros_like(acc_ref)

    @pl.when((flags & VALID) != 0)
    def _():
        tok0 = tile_ref[b, k] * tm
        rows = lax.broadcasted_iota(jnp.int32, (tm, tok_ref.shape[-1]), 0) + tok0
        pick = jnp.where(rows == tok_ref[0, 0], 1.0, 0.0).astype(BF16)
        acc_ref[...] += _dot(pick, y_ref[0])

    @pl.when((flags & LAST) != 0)
    def _():
        mod = mod_ref[0, 0]
        y = alpha * x_ref[0] + mod[5:6] * acc_ref[...]
        o_ref[0] = _ln(y) * lng_ref[...] + lnb_ref[...]


def _combine(y_sorted, tok_sorted, tile, win, flags, xa, mods, lng, lnb, n_lat, tm, rwin, alpha):
    b, t, d = xa.shape
    nl = n_lat // tm
    n_item = tile.shape[1]
    grid_spec = pltpu.PrefetchScalarGridSpec(
        num_scalar_prefetch=3,
        grid=(b, n_item),
        in_specs=[pl.BlockSpec((1, rwin, d), lambda bi, k, tl, wn, fl: (bi, wn[bi, k], 0)),
                  pl.BlockSpec((1, 1, 1, rwin), lambda bi, k, tl, wn, fl: (bi, wn[bi, k], 0, 0)),
                  pl.BlockSpec((1, tm, d), lambda bi, k, tl, wn, fl: (bi, tl[bi, k], 0)),
                  pl.BlockSpec((1, 1, N_MOD, d),
                               lambda bi, k, tl, wn, fl: (bi, (tl[bi, k] >= nl).astype(jnp.int32), 0, 0)),
                  pl.BlockSpec(lng.shape, lambda bi, k, tl, wn, fl: (0, 0)),
                  pl.BlockSpec(lnb.shape, lambda bi, k, tl, wn, fl: (0, 0))],
        out_specs=pl.BlockSpec((1, tm, d), lambda bi, k, tl, wn, fl: (bi, tl[bi, k], 0)),
        scratch_shapes=[pltpu.VMEM((tm, d), F32)],
    )
    return pl.pallas_call(
        functools.partial(_combine_kernel, alpha=alpha, tm=tm),
        grid_spec=grid_spec,
        out_shape=jax.ShapeDtypeStruct((b, t, d), F32),
        compiler_params=_cparams(("parallel", "arbitrary")),
        name="moe_combine",
    )(tile, win, flags, y_sorted, tok_sorted.reshape(b, -1, 1, rwin), xa, mods, lng, lnb)


def _moe(xa, h, aff, mods, n_exp, w_gate_up, w_down, lng, lnb, layer, n_lat, tm, alpha):
    b, t, d = xa.shape
    m_ctx = t - n_lat
    aff = jnp.swapaxes(aff[:, :, :n_exp], 1, 2)
    cap_l = CAPACITY * n_lat // n_exp
    cap_x = CAPACITY * m_ctx // n_exp
    gate_l, idx_l = lax.top_k(aff[:, :, :n_lat], cap_l)
    gate_x, idx_x = lax.top_k(aff[:, :, n_lat:], cap_x)
    idx = jnp.concatenate([idx_l, idx_x + n_lat], axis=-1)
    gate = jnp.concatenate([gate_l, gate_x], axis=-1)
    cap = cap_l + cap_x
    idx_e = jnp.swapaxes(idx, 0, 1)
    bidx = jnp.arange(b)[None, :, None]
    xs = h[bidx, idx_e].reshape(n_exp, b * cap, d)
    gate_e = jnp.swapaxes(gate, 0, 1).reshape(n_exp, b * cap, 1)
    rows = b * cap
    tmr = rows
    for cand in (1088, 1024, 512, 256, 128, 64, 32, 16):
        if rows % cand == 0:
            tmr = cand
            break
    tf = min(512, w_down.shape[2])
    y = _expert_ffn(xs, gate_e, w_gate_up, w_down, layer, tmr, tf)
    n_pair = n_exp * cap
    rwin = next(r for r in (512, 256, 128, 64, 32, 16, 8) if n_pair % r == 0)
    order, tok, tile, win, flags = _combine_plan(idx.reshape(b, n_pair), t, tm, rwin)
    y_sorted = y[order // cap, jnp.arange(b)[:, None] * cap + order % cap]
    return _combine(y_sorted, tok, tile, win, flags, xa, mods, lng, lnb, n_lat, tm, rwin, alpha)


def _rope_tables(n_lat, m_ctx):
    t = np.arange(n_lat)
    rows = (t // GRID_W).astype(np.float32)
    cols = (t % GRID_W).astype(np.float32)
    n_freq = HEAD_DIM // 4
    inv_freq = jnp.asarray(ROPE_THETA, F32) ** (-jnp.arange(n_freq, dtype=F32) / n_freq)
    ang_r = jnp.asarray(rows)[:, None] * inv_freq
    ang_c = jnp.asarray(cols)[:, None] * inv_freq
    cr, sr, cc, sc = jnp.cos(ang_r), jnp.sin(ang_r), jnp.cos(ang_c), jnp.sin(ang_c)
    cos = jnp.concatenate([cr, cr, cc, cc], axis=-1)
    sin = jnp.concatenate([-sr, sr, -sc, sc], axis=-1)
    cos = jnp.concatenate([cos, jnp.ones((m_ctx, HEAD_DIM), F32)], axis=0)
    sin = jnp.concatenate([sin, jnp.zeros((m_ctx, HEAD_DIM), F32)], axis=0)
    return jnp.tile(cos, (1, A_HEADS)), jnp.tile(sin, (1, A_HEADS))


def kernel(x, c, ctx, c_ctx, w_mod, b_mod, w_in, qk_gain, rpb, conv_w, a_log, dt_bias, o_gain, w_branch, w_out,
           ln1_g, ln1_b, w_router, w_gate_up, w_down, ln2_g, ln2_b):
    b, n_lat, d = x.shape
    m_ctx = ctx.shape[1]
    depth = w_mod.shape[0]
    alpha = (2 * depth) ** 0.25
    tm = math.gcd(256, math.gcd(n_lat, m_ctx))
    blk = tm

    xa = jnp.concatenate([x, ctx], axis=1)
    n_rows = -(-(b + 1) // SUBLANE) * SUBLANE
    cc = jnp.zeros((n_rows, d), F32).at[:b].set(c).at[b].set(c_ctx)
    mod_all = _modulation(cc, w_mod, b_mod).reshape(depth, n_rows, N_MOD, d)
    cos, sin = _rope_tables(n_lat, m_ctx)
    gs_mean = _group_sum_matrix(A_Q_W, 1.0 / HEAD_DIM)
    gs_sum = _group_sum_matrix(C_W, 1.0)
    na_geoms, na_table = _na_classes(n_lat, n_lat + m_ctx)

    offs = np.cumsum((0, A_Q_W, A_KV_W, A_KV_W, B_W, B_W, B_W, 3 * C_W, C_W, 2 * C_HEADS, 2 * C_HEADS, N_BRANCH * d))
    for layer in range(depth):
        ml = mod_all[layer]
        mods = jnp.stack([ml[:b], jnp.broadcast_to(ml[b][None], (b, N_MOD, d))], axis=1)
        wl = w_in[layer]
        wa = wl[:, offs[0]:offs[3]].astype(BF16)
        wb = wl[:, offs[3]:offs[6]].astype(BF16)
        wc = wl[:, offs[6]:offs[8]].astype(BF16)
        wab = jnp.pad(wl[:, offs[8]:offs[10]], ((0, 0), (0, LANE - 4 * C_HEADS))).astype(BF16)
        wmg = wl[:, offs[10]:offs[11]].astype(BF16)
        gq = jnp.tile(qk_gain[layer, 0], A_HEADS)[None]
        gk = jnp.tile(qk_gain[layer, 1], A_KV_HEADS)[None]
        qa, ka, va, qb, kb, vb, qkvc, gate_c, ab = _in_projection(
            xa, mods, wa, wb, wc, wab, gq, gk, cos, sin, gs_mean, n_lat, tm)
        oa = _gqa(qa, ka, va, n_lat, min(GQA_TQ, tm), 2 if (n_lat + m_ctx) % (2 * LANE) == 0 else 1)
        ob = _neighborhood(qb, kb, vb, _na_bias_slabs(rpb[layer], na_geoms), na_table, n_lat)
        qc, kc, vc, comp = _gdn_prep(qkvc, ab, conv_w[layer], a_log[layer], dt_bias[layer], gs_sum, n_lat, tm)
        u, w, kd, qd, intra = _gdn_local(qc, kc, vc, comp, blk)
        oc0, oc1 = _gdn_scan(u, w, kd, qd, intra, comp, n_lat, blk)
        og = jnp.tile(o_gain[layer], C_HEADS)[None]
        xa, h, aff = _merge(xa, mods, oa, ob, oc0, oc1, gate_c, wmg, w_branch[layer].astype(BF16),
                            w_out[layer].astype(BF16), og, gs_mean, ln1_g[layer][None], ln1_b[layer][None],
                            w_router[layer], n_lat, tm, alpha)
        xa = _moe(xa, h, aff, mods, w_router.shape[-1], w_gate_up, w_down, ln2_g[layer][None], ln2_b[layer][None],
                  layer, n_lat, tm, alpha)
    return xa[:, :n_lat]
```

```python
import functools
import math

import numpy as np
import jax
import jax.numpy as jnp
from jax import lax
from jax.experimental import pallas as pl
from jax.experimental.pallas import tpu as pltpu

F32 = jnp.float32
BF16 = jnp.bfloat16
HIGHEST = lax.Precision.HIGHEST

GRID_W = 64
HEAD_DIM = 64
A_HEADS = 8
A_KV_HEADS = 2
B_HEADS = 8
C_HEADS = 8
NA_ROWS = 8
NA_COLS = 16
CONV_K = 4
CHUNK = 64
N_BRANCH = 3
BRANCH_W = 512
CAPACITY = 2
N_MOD = 6
LN_EPS = 1e-6
ROPE_THETA = 10000.0
NEG_BIG = -1e30
LOG2E = 1.4426950408889634

A_Q_W = A_HEADS * HEAD_DIM
A_KV_W = A_KV_HEADS * HEAD_DIM
B_W = B_HEADS * HEAD_DIM
C_W = C_HEADS * HEAD_DIM
LANE = 128
SUBLANE = 8
V7X_VMEM_BYTES = 64 * 1024 * 1024
VMEM_LIMIT = V7X_VMEM_BYTES * 7 // 8
GQA_TQ = 128
LOCAL_PAIRS = 2
SCAN_BATCH = 2
VT_ROWS = HEAD_DIM + 16
NA_WIN = NA_ROWS + 2


def _cparams(sem):
    return pltpu.CompilerParams(dimension_semantics=sem, vmem_limit_bytes=VMEM_LIMIT)


def _dot(a, b):
    return jnp.dot(a, b, preferred_element_type=F32)


def _dot_hi(a, b):
    return jnp.dot(a, b, precision=HIGHEST, preferred_element_type=F32)


def _dot_nt(a, b):
    return lax.dot_general(a, b, (((1,), (1,)), ((), ())), preferred_element_type=F32)


def _dot_tn(a, b):
    return lax.dot_general(a, b, (((0,), (0,)), ((), ())), preferred_element_type=F32)


def _split_bf16(x, pieces):
    out = []
    for _ in range(pieces - 1):
        h = x.astype(BF16)
        out.append(h)
        x = x - h.astype(F32)
    out.append(x.astype(BF16))
    return out


def _dot_exact_rhs(a, b, pieces=3):
    return sum(_dot(p, b) for p in _split_bf16(a, pieces))


def _dot_exact_lhs(a, b, pieces=3):
    return sum(_dot(a, p) for p in _split_bf16(b, pieces))


def _ln(x):
    mu = jnp.mean(x, axis=-1, keepdims=True)
    xc = x - mu
    var = jnp.mean(xc * xc, axis=-1, keepdims=True)
    return xc * lax.rsqrt(var + LN_EPS)


def _sigmoid(x):
    return 1.0 / (1.0 + jnp.exp(-x))


def _silu(x):
    return x * _sigmoid(x)


def _group_sum_matrix(width, scale):
    g = np.arange(width) // HEAD_DIM
    return jnp.asarray((g[:, None] == g[None, :]).astype(np.float32) * scale, dtype=BF16)


def _mod_kernel(c_ref, w_ref, b_ref, o_ref):
    c = c_ref[...]
    o_ref[0] = _dot_hi(_silu(c), w_ref[0]) + b_ref[0]


def _modulation(cc, w_mod, b_mod):
    depth, d, dm = w_mod.shape
    r = cc.shape[0]
    tn = min(dm, 1536)
    return pl.pallas_call(
        _mod_kernel,
        grid=(depth, dm // tn),
        in_specs=[pl.BlockSpec((r, d), lambda l, j: (0, 0)),
                  pl.BlockSpec((1, d, tn), lambda l, j: (l, 0, j)),
                  pl.BlockSpec((1, 1, tn), lambda l, j: (l, 0, j))],
        out_specs=pl.BlockSpec((1, r, tn), lambda l, j: (l, 0, j)),
        out_shape=jax.ShapeDtypeStruct((depth, r, dm), F32),
        compiler_params=_cparams(("parallel", "parallel")),
        name="modulation",
    )(cc, w_mod, b_mod.reshape(depth, 1, dm))


def _rope_swap(y):
    w = y.shape[-1]
    lane = lax.broadcasted_iota(jnp.int32, y.shape, 1)
    up = pltpu.roll(y, w - 16, 1)
    down = pltpu.roll(y, 16, 1)
    return jnp.where((lane % 32) < 16, up, down)


def _inproj_kernel(x_ref, mod_ref, wa_ref, wb_ref, wc_ref, wab_ref, gq_ref, gk_ref, cos_ref, sin_ref,
                   gs_ref, qa_ref, ka_ref, va_ref, qb_ref, kb_ref, vb_ref, qkvc_ref, gatec_ref, ab_ref):
    x = x_ref[0]
    mod = mod_ref[0, 0]
    h = (_ln(x) * (1.0 + mod[1:2]) + mod[0:1]).astype(BF16)
    scale = HEAD_DIM ** -0.5 * LOG2E

    za = _dot(h, wa_ref[...])
    q = za[:, :A_Q_W]
    k = za[:, A_Q_W:A_Q_W + A_KV_W]
    v = za[:, A_Q_W + A_KV_W:]
    cos = cos_ref[...]
    sin = sin_ref[...]
    gs = gs_ref[...]
    qn = q * lax.rsqrt(_dot_exact_rhs(q * q, gs, 2) + LN_EPS) * gq_ref[...]
    qn = qn * cos + _rope_swap(qn) * sin
    qa_ref[0] = (qn * scale).astype(BF16)
    kn = k * lax.rsqrt(_dot_exact_rhs(k * k, gs[:A_KV_W, :A_KV_W], 2) + LN_EPS) * gk_ref[...]
    kn = kn * cos[:, :A_KV_W] + _rope_swap(kn) * sin[:, :A_KV_W]
    ones = jnp.ones((VT_ROWS - HEAD_DIM, x.shape[0]), F32)
    vt = v.T
    for j in range(A_KV_HEADS):
        ka_ref[0, j] = kn[:, j * HEAD_DIM:(j + 1) * HEAD_DIM].astype(BF16)
        va_ref[0, j] = jnp.concatenate([vt[j * HEAD_DIM:(j + 1) * HEAD_DIM], ones], axis=0).astype(BF16)

    zb = _dot(h, wb_ref[...])
    qb_ref[0] = (zb[:, :B_W] * scale).astype(BF16)
    kbv = zb[:, B_W:2 * B_W]
    vt = zb[:, 2 * B_W:].T
    for j in range(B_HEADS):
        kb_ref[0, j] = kbv[:, j * HEAD_DIM:(j + 1) * HEAD_DIM].astype(BF16)
        vb_ref[0, j] = jnp.concatenate([vt[j * HEAD_DIM:(j + 1) * HEAD_DIM], ones], axis=0).astype(BF16)

    zc = _dot(h, wc_ref[...])
    qkvc_ref[0] = zc[:, :3 * C_W]
    gatec_ref[0] = zc[:, 3 * C_W:]
    ab_ref[0] = _dot(h, wab_ref[...])


def _in_projection(xa, mods, wa, wb, wc, wab, gq, gk, cos, sin, gs, n_lat, tm):
    b, t, d = xa.shape
    nl = n_lat // tm
    row = lambda bi, i: (bi, i, 0)
    const = lambda bi, i: (0, 0)
    outs = [
        (jax.ShapeDtypeStruct((b, t, A_Q_W), BF16), pl.BlockSpec((1, tm, A_Q_W), row)),
        (jax.ShapeDtypeStruct((b, A_KV_HEADS, t, HEAD_DIM), BF16),
         pl.BlockSpec((1, A_KV_HEADS, tm, HEAD_DIM), lambda bi, i: (bi, 0, i, 0))),
        (jax.ShapeDtypeStruct((b, A_KV_HEADS, VT_ROWS, t), BF16),
         pl.BlockSpec((1, A_KV_HEADS, VT_ROWS, tm), lambda bi, i: (bi, 0, 0, i))),
        (jax.ShapeDtypeStruct((b, t, B_W), BF16), pl.BlockSpec((1, tm, B_W), row)),
        (jax.ShapeDtypeStruct((b, B_HEADS, t, HEAD_DIM), BF16),
         pl.BlockSpec((1, B_HEADS, tm, HEAD_DIM), lambda bi, i: (bi, 0, i, 0))),
        (jax.ShapeDtypeStruct((b, B_HEADS, VT_ROWS, t), BF16),
         pl.BlockSpec((1, B_HEADS, VT_ROWS, tm), lambda bi, i: (bi, 0, 0, i))),
        (jax.ShapeDtypeStruct((b, t, 3 * C_W), F32), pl.BlockSpec((1, tm, 3 * C_W), row)),
        (jax.ShapeDtypeStruct((b, t, C_W), F32), pl.BlockSpec((1, tm, C_W), row)),
        (jax.ShapeDtypeStruct((b, t, LANE), F32), pl.BlockSpec((1, tm, LANE), row)),
    ]
    return pl.pallas_call(
        _inproj_kernel,
        grid=(b, t // tm),
        in_specs=[pl.BlockSpec((1, tm, d), row),
                  pl.BlockSpec((1, 1, N_MOD, d), lambda bi, i: (bi, (i >= nl).astype(jnp.int32), 0, 0)),
                  pl.BlockSpec(wa.shape, const), pl.BlockSpec(wb.shape, const),
                  pl.BlockSpec(wc.shape, const), pl.BlockSpec(wab.shape, const),
                  pl.BlockSpec(gq.shape, const), pl.BlockSpec(gk.shape, const),
                  pl.BlockSpec((tm, A_Q_W), lambda bi, i: (i, 0)),
                  pl.BlockSpec((tm, A_Q_W), lambda bi, i: (i, 0)),
                  pl.BlockSpec(gs.shape, const)],
        out_specs=[o[1] for o in outs],
        out_shape=[o[0] for o in outs],
        compiler_params=_cparams(("parallel", "parallel")),
        name="in_projection",
    )(xa, mods, wa, wb, wc, wab, gq, gk, cos, sin, gs)


def _gqa_kernel(q_ref, k_ref, v_ref, o_ref, *, tq, ck, n_lat, n_tot):
    i = pl.program_id(1)
    g = A_HEADS // A_KV_HEADS

    def attend(key_lo, key_hi, n_split):
        qs = [jnp.concatenate([q_ref[0, :, (kh * g + j) * HEAD_DIM:(kh * g + j + 1) * HEAD_DIM] for j in range(g)],
                              axis=0) for kh in range(A_KV_HEADS)]
        step = (key_hi - key_lo) // n_split
        units = [(kh, key_lo + r * step, key_lo + (r + 1) * step) for r in range(n_split) for kh in range(A_KV_HEADS)]
        ss, ms, ps, oes = {}, {}, {}, {}
        for n in range(len(units) + 2):
            if n < len(units):
                kh, lo, hi = units[n]
                ss[n] = _dot_nt(k_ref[0, kh, lo:hi, :], qs[kh])
            if 0 <= n - 1 < len(units):
                ms[n - 1] = jnp.max(ss[n - 1], axis=0, keepdims=True).astype(BF16).astype(F32)
                ps[n - 1] = jnp.exp2(ss.pop(n - 1).astype(BF16) - ms[n - 1].astype(BF16))
            if 0 <= n - 2 < len(units):
                kh, lo, hi = units[n - 2]
                oes[n - 2] = _dot(v_ref[0, kh, :, lo:hi], ps.pop(n - 2))
        for kh in range(A_KV_HEADS):
            mine = [n for n, u in enumerate(units) if u[0] == kh]
            m = functools.reduce(jnp.maximum, [ms[n] for n in mine])
            oe = sum(oes[n] * jnp.exp2(ms[n] - m) for n in mine)
            o = (oe[:HEAD_DIM] / oe[HEAD_DIM:HEAD_DIM + 1]).T.astype(BF16)
            for j in range(g):
                o_ref[0, :, (kh * g + j) * HEAD_DIM:(kh * g + j + 1) * HEAD_DIM] = o[j * tq:(j + 1) * tq]

    @pl.when(i < n_lat // tq)
    def _():
        attend(0, n_tot, ck)

    @pl.when(i >= n_lat // tq)
    def _():
        attend(n_lat, n_tot, 1)


def _gqa(qa, ka, va, n_lat, tq, ck):
    b, t, _ = qa.shape
    return pl.pallas_call(
        functools.partial(_gqa_kernel, tq=tq, ck=ck, n_lat=n_lat, n_tot=t),
        grid=(b, t // tq),
        in_specs=[pl.BlockSpec((1, tq, A_Q_W), lambda bi, i: (bi, i, 0)),
                  pl.BlockSpec((1, A_KV_HEADS, t, HEAD_DIM), lambda bi, i: (bi, 0, 0, 0)),
                  pl.BlockSpec((1, A_KV_HEADS, VT_ROWS, t), lambda bi, i: (bi, 0, 0, 0))],
        out_specs=pl.BlockSpec((1, tq, A_Q_W), lambda bi, i: (bi, i, 0)),
        out_shape=jax.ShapeDtypeStruct((b, t, A_Q_W), BF16),
        compiler_params=_cparams(("parallel", "parallel")),
        name="gqa",
    )(qa, ka, va)


def _na_window_start(r0, rows):
    return np.clip(r0 - NA_ROWS // 2, 0, rows - NA_ROWS - 1) // 2 * 2


def _na_classes(n_lat, n_tot):
    rows = n_lat // GRID_W
    assert rows >= NA_WIN and rows % 2 == 0
    geoms, table = [], []
    for r0 in range(0, rows, 2):
        geom = (int(_na_window_start(r0, rows)) - r0,) + tuple(
            int(np.clip(r0 + qr - NA_ROWS // 2, 0, rows - NA_ROWS)) - r0 for qr in range(2))
        if geom not in geoms:
            geoms.append(geom)
        table.append(geoms.index(geom))
    table += [len(geoms)] * ((n_tot - n_lat) // (2 * GRID_W))
    return geoms, np.asarray(table, np.int32)


def _na_bias_slabs(rpb, geoms):
    n_head = rpb.shape[0]
    cj = np.arange(GRID_W)[:, None]
    c = np.arange(GRID_W)[None, :]
    col_start = np.clip(c - NA_COLS // 2, 0, GRID_W - NA_COLS)
    col_ok = (cj >= col_start) & (cj < col_start + NA_COLS)
    col_off = np.clip(cj - c + (NA_COLS - 1), 0, 2 * NA_COLS - 2)
    spread = np.zeros((2 * NA_COLS - 1, GRID_W * GRID_W), np.float32)
    spread[col_off.reshape(-1), np.arange(GRID_W * GRID_W)] = 1.0
    kr = np.arange(NA_WIN)[:, None]
    qr = np.arange(2)[None, :]
    pick = np.zeros((len(geoms), NA_WIN, 2, 2 * NA_ROWS - 1), np.float32)
    ok = np.zeros((len(geoms), 1, NA_WIN, 2, GRID_W, GRID_W), bool)
    for n, (a, b0, b1) in enumerate(geoms):
        first = np.where(qr == 0, b0, b1)
        row_ok = (a + kr >= first) & (a + kr < first + NA_ROWS)
        row_off = np.clip(a + kr - qr + (NA_ROWS - 1), 0, 2 * NA_ROWS - 2)
        pick[n, kr, qr, row_off] = 1.0
        ok[n, 0] = row_ok[:, :, None, None] & col_ok[None, None]
    rows = jnp.einsum("nkqo,hoc->nhkqc", jnp.asarray(pick), rpb * LOG2E, precision=HIGHEST)
    full = jnp.dot(rows, jnp.asarray(spread), precision=HIGHEST)
    full = full.reshape(len(geoms), n_head, NA_WIN, 2, GRID_W, GRID_W)
    masked = jnp.where(jnp.asarray(ok), full, NEG_BIG)
    slabs = jnp.concatenate([masked[:, :, :, 0], masked[:, :, :, 1]], axis=-1)
    slabs = slabs.reshape(len(geoms), n_head, NA_WIN * GRID_W, 2 * GRID_W)
    return jnp.concatenate([slabs, jnp.full_like(slabs[:1], NEG_BIG)], axis=0).astype(F32)


def _na_kernel(cls_ref, q_ref, k_ref, vt_ref, bias_ref, o_ref, *, n_lat, n_tot):
    del cls_ref
    rows = n_lat // GRID_W
    r0 = 2 * pl.program_id(1)
    start = jnp.clip(r0 - NA_ROWS // 2, 0, rows - NA_ROWS - 1) // 2 * 2
    nb = pl.ds(pl.multiple_of(start * GRID_W, 2 * GRID_W), NA_WIN * GRID_W)
    hs = [slice(h * HEAD_DIM, (h + 1) * HEAD_DIM) for h in range(B_HEADS)]
    qs = [q_ref[0, :, hs[h]] for h in range(B_HEADS)]
    s_nb = [_dot_nt(k_ref[0, h, nb, :], qs[h]) + bias_ref[0, h] for h in range(B_HEADS)]
    s_cx = [_dot_nt(k_ref[0, h, n_lat:n_tot, :], qs[h]) for h in range(B_HEADS)]
    ms = [jnp.maximum(jnp.max(a, axis=0, keepdims=True), jnp.max(c, axis=0, keepdims=True))
          for a, c in zip(s_nb, s_cx)]
    p_nb = [jnp.exp2((a - m).astype(BF16)) for a, m in zip(s_nb, ms)]
    p_cx = [jnp.exp2((c - m).astype(BF16)) for c, m in zip(s_cx, ms)]
    for h in range(B_HEADS):
        oe = _dot(vt_ref[0, h, :, nb], p_nb[h]) + _dot(vt_ref[0, h, :, n_lat:n_tot], p_cx[h])
        o_ref[0, :, hs[h]] = (oe[:HEAD_DIM] / oe[HEAD_DIM:HEAD_DIM + 1]).T.astype(BF16)


def _neighborhood(qb, kb, vbt, slabs, table, n_lat):
    b, t, _ = qb.shape
    tq = 2 * GRID_W
    grid_spec = pltpu.PrefetchScalarGridSpec(
        num_scalar_prefetch=1,
        grid=(b, t // tq),
        in_specs=[pl.BlockSpec((1, tq, B_W), lambda bi, pi, cls: (bi, pi, 0)),
                  pl.BlockSpec((1, B_HEADS, t, HEAD_DIM), lambda bi, pi, cls: (bi, 0, 0, 0)),
                  pl.BlockSpec((1, B_HEADS, VT_ROWS, t), lambda bi, pi, cls: (bi, 0, 0, 0)),
                  pl.BlockSpec((1,) + slabs.shape[1:], lambda bi, pi, cls: (cls[pi], 0, 0, 0))],
        out_specs=pl.BlockSpec((1, tq, B_W), lambda bi, pi, cls: (bi, pi, 0)),
    )
    return pl.pallas_call(
        functools.partial(_na_kernel, n_lat=n_lat, n_tot=t),
        grid_spec=grid_spec,
        out_shape=jax.ShapeDtypeStruct((b, t, B_W), BF16),
        compiler_params=_cparams(("parallel", "parallel")),
        name="neighborhood",
    )(jnp.asarray(table), qb, kb, vbt, slabs)


GC_LANE, GL_LANE, BETA_LANE = 0, 2 * C_HEADS, 4 * C_HEADS


def _gdn_prep_kernel(x_ref, prev_ref, next_ref, ab_ref, cw_ref, alog_ref, dtb_ref, gs_ref,
                     q_ref, k_ref, v_ref, comp_ref, *, tm, n_lat, n_tot):
    i = pl.program_id(1)
    nl = n_lat // tm
    nt = n_tot // tm
    first = jnp.logical_or(i == 0, i == nl)
    last = jnp.logical_or(i == nl - 1, i == nt - 1)
    x = x_ref[0]
    pv = jnp.where(first, 0.0, prev_ref[0])
    nx = jnp.where(last, 0.0, next_ref[0])
    row = lax.broadcasted_iota(jnp.int32, (tm, 1), 0)
    prev1 = pv[SUBLANE - 1:SUBLANE]
    prev2 = pv[SUBLANE - 2:SUBLANE - 1]
    xm1 = jnp.where(row == 0, prev1, pltpu.roll(x, 1, 0))
    xm2 = jnp.where(row == 0, prev2, jnp.where(row == 1, prev1, pltpu.roll(x, 2, 0)))
    xp1 = jnp.where(row == tm - 1, nx[0:1], pltpu.roll(x, tm - 1, 0))
    cw = cw_ref[...]
    y = _silu(xm2 * cw[0:1] + xm1 * cw[1:2] + x * cw[2:3] + xp1 * cw[3:4])
    q = y[:, :C_W]
    k = y[:, C_W:2 * C_W]
    gs = gs_ref[...]
    q_ref[0] = q * lax.rsqrt(_dot_exact_rhs(q * q, gs, 2) + LN_EPS) * (HEAD_DIM ** -0.5)
    k_ref[0] = k * lax.rsqrt(_dot_exact_rhs(k * k, gs, 2) + LN_EPS)
    v_ref[0] = y[:, 2 * C_W:]
    ab = ab_ref[0]
    z = ab + dtb_ref[...]
    softplus = jnp.maximum(z, 0.0) + jnp.log(1.0 + jnp.exp(-jnp.abs(z)))
    lane = lax.broadcasted_iota(jnp.int32, ab.shape, 1)
    gb = jnp.where(lane < 2 * C_HEADS, -jnp.exp(alog_ref[...]) * softplus, _sigmoid(ab))
    ii = lax.broadcasted_iota(jnp.int32, (tm, tm), 0)
    jj = lax.broadcasted_iota(jnp.int32, (tm, tm), 1)
    same = (ii // CHUNK) == (jj // CHUNK)
    gb3 = _split_bf16(gb, 3)

    def masked_sum(mask):
        mb = jnp.where(mask, 1.0, 0.0).astype(BF16)
        return sum(_dot(mb, piece) for piece in gb3)

    prefix = masked_sum(jnp.logical_and(same, jj <= ii))
    suffix = masked_sum(jnp.logical_and(same, jj >= ii))
    total = masked_sum(same)
    comp = jnp.where(lane < C_HEADS, prefix,
                     jnp.where(lane < GL_LANE, suffix,
                               jnp.where(lane < BETA_LANE, pltpu.roll(total, GL_LANE, 1),
                                         jnp.where(lane < BETA_LANE + 2 * C_HEADS, pltpu.roll(gb, 2 * C_HEADS, 1),
                                                   0.0))))
    comp_ref[0] = comp


def _gdn_prep(qkvc, ab, conv_w, a_log, dt_bias, gs, n_lat, tm):
    b, t, _ = qkvc.shape
    c3 = 3 * C_W
    alog_row = jnp.zeros((1, LANE), F32).at[0, :2 * C_HEADS].set(a_log.reshape(-1))
    dtb_row = jnp.zeros((1, LANE), F32).at[0, :2 * C_HEADS].set(dt_bias.reshape(-1))
    row = lambda bi, i: (bi, i, 0)
    const2 = lambda bi, i: (0, 0)
    hb = tm // SUBLANE
    last8 = t // SUBLANE - 1
    tok = jax.ShapeDtypeStruct((b, t, C_W), F32)
    return pl.pallas_call(
        functools.partial(_gdn_prep_kernel, tm=tm, n_lat=n_lat, n_tot=t),
        grid=(b, t // tm),
        in_specs=[pl.BlockSpec((1, tm, c3), row),
                  pl.BlockSpec((1, SUBLANE, c3), lambda bi, i: (bi, jnp.maximum(i * hb - 1, 0), 0)),
                  pl.BlockSpec((1, SUBLANE, c3), lambda bi, i: (bi, jnp.minimum((i + 1) * hb, last8), 0)),
                  pl.BlockSpec((1, tm, LANE), row),
                  pl.BlockSpec((CONV_K, c3), const2),
                  pl.BlockSpec((1, LANE), const2), pl.BlockSpec((1, LANE), const2),
                  pl.BlockSpec(gs.shape, const2)],
        out_specs=[pl.BlockSpec((1, tm, C_W), row)] * 3 + [pl.BlockSpec((1, tm, LANE), row)],
        out_shape=[tok, tok, tok, jax.ShapeDtypeStruct((b, t, LANE), F32)],
        compiler_params=_cparams(("parallel", "parallel")),
        name="gdn_prep",
    )(qkvc, qkvc, qkvc, ab, conv_w, alog_row, dtb_row, gs)


def _half_block_rows(a, h, upper):
    return jnp.concatenate([a[s:s + h] for s in range(0 if upper else h, a.shape[0], 2 * h)], axis=0)


def _with_half_block_rows(full, part, h, upper):
    pieces = []
    for blk in range(full.shape[0] // h):
        if (blk % 2 == 0) == upper:
            pieces.append(part[(blk // 2) * h:(blk // 2 + 1) * h])
        else:
            pieces.append(full[blk * h:(blk + 1) * h])
    return jnp.concatenate(pieces, axis=0)


def _unit_tri_inverses(lms, uppers, eye, same_block):
    def mm(a, b_):
        return _dot(a.astype(BF16), b_.astype(BF16))

    lds = [jnp.where(same_block[SUBLANE], lm, 0.0) for lm in lms]
    xs = [eye - ld for ld in lds]
    lds = [ld.astype(BF16) for ld in lds]
    pws = [_dot(ld, ld).astype(BF16) for ld in lds]
    xs = [x + mm(x, pw) for x, pw in zip(xs, pws)]
    pws = [_dot(pw, pw).astype(BF16) for pw in pws]
    xs = [x + mm(x, pw) for x, pw in zip(xs, pws)]
    h = SUBLANE
    while h < CHUNK:
        level = jnp.logical_not(same_block[h]) if 2 * h == CHUNK else jnp.logical_and(
            same_block[2 * h], jnp.logical_not(same_block[h]))
        xbs = [x.astype(BF16) for x in xs]
        ls = [_half_block_rows(jnp.where(level, lm, 0.0), h, up).astype(BF16) for lm, up in zip(lms, uppers)]
        ys = [_dot(l, xb) for l, xb in zip(ls, xbs)]
        zs = [_with_half_block_rows(jnp.zeros_like(x), y, h, up).astype(BF16) for x, y, up in zip(xs, ys, uppers)]
        xrs = [_half_block_rows(x, h, up) for x, up in zip(xs, uppers)]
        upd = [_dot(xr.astype(BF16), z) for xr, z in zip(xrs, zs)]
        xs = [_with_half_block_rows(x, xr - u, h, up) for x, xr, u, up in zip(xs, xrs, upd, uppers)]
        h *= 2
    return xs


def _gdn_local_kernel(q_ref, k_ref, v_ref, comp_ref, ex_ref, sel_ref, u_ref, w_ref, kd_ref, qd_ref, in_ref):
    n = q_ref.shape[1]
    hp = 2 * HEAD_DIM
    n_pr = q_ref.shape[2] // hp
    ii = lax.broadcasted_iota(jnp.int32, (n, n), 0)
    jj = lax.broadcasted_iota(jnp.int32, (n, n), 1)
    same = (ii // CHUNK) == (jj // CHUNK)
    eye = (ii == jj).astype(F32)
    same_block = {h: (ii // h) == (jj // h) for h in (8, 16, 32)}
    lane = lax.broadcasted_iota(jnp.int32, (1, hp), 1)
    lane2 = lax.broadcasted_iota(jnp.int32, (1, 2 * hp), 1)
    incls = [jnp.logical_and(same, ii >= jj), jnp.logical_and(same, ii <= jj)]
    stricts = [jnp.logical_and(same, ii > jj), jnp.logical_and(same, ii < jj)]
    lms, intras, rhss = [], [], []
    for pr in range(n_pr):
        ls = slice(pr * hp, (pr + 1) * hp)
        q = q_ref[0, :, ls]
        k = k_ref[0, :, ls]
        v = v_ref[0, :, ls]
        e = _dot_exact_rhs(comp_ref[0], ex_ref[pr])
        kh = [jnp.where((lane // HEAD_DIM) == hh, k, 0.0).astype(BF16) for hh in range(2)]
        for p in range(2):
            gc = e[:, p * hp:(p + 1) * hp]
            gl = e[:, (2 + p) * hp:(3 + p) * hp]
            beta = e[:, (4 + p) * hp:(5 + p) * hp]
            eg = jnp.exp(gc)
            kb = k * beta
            kd_ref[p, 0, :, ls] = (k * jnp.exp(gl - gc)).astype(BF16)
            qd_ref[p, 0, :, ls] = (q * eg).astype(BF16)
            rhs = jnp.concatenate([v * beta, kb * eg], axis=1)
            kbq = jnp.concatenate([kb, q], axis=0).astype(BF16)
            for hh in range(2):
                gcb = jnp.broadcast_to(gc[:, hh * HEAD_DIM:hh * HEAD_DIM + 1], (n, n))
                decay = jnp.exp(jnp.where(incls[p], gcb - gcb.T, NEG_BIG))
                aq = _dot_nt(kbq, kh[hh])
                lms.append(jnp.where(stricts[p], aq[:n] * decay, 0.0))
                intras.append(jnp.where(incls[p], aq[n:] * decay, 0.0).astype(BF16))
                rhss.append(jnp.where(((lane2 % hp) // HEAD_DIM) == hh, rhs, 0.0).astype(BF16))
    uppers = [p == 1 for _ in range(n_pr) for p in range(2) for _ in range(2)]
    tmats = _unit_tri_inverses(lms, uppers, eye, same_block)
    for pr in range(n_pr):
        ls = slice(pr * hp, (pr + 1) * hp)
        for p in range(2):
            at = 4 * pr + 2 * p
            uw = sum(_dot(tmats[at + hh].astype(BF16), rhss[at + hh]) for hh in range(2))
            intra2 = sum(_dot(intras[at + hh], sel_ref[hh]) for hh in range(2))
            u_ref[p, 0, :, ls] = uw[:, :hp].astype(BF16)
            w_ref[p, 0, :, ls] = uw[:, hp:].astype(BF16)
            in_ref[p, 0, :, ls] = intra2.astype(BF16)


def _gdn_local(q, k, v, comp, blk):
    b, t, _ = q.shape
    hp = 2 * HEAD_DIM
    n_pair = C_W // hp
    ex = np.zeros((n_pair, LANE, 6 * hp), np.float32)
    for pr in range(n_pair):
        for p in range(2):
            for hh in range(2):
                src = p * C_HEADS + pr * 2 + hh
                for kind, base in enumerate((GC_LANE, GL_LANE, BETA_LANE)):
                    col = (2 * kind + p) * hp + hh * HEAD_DIM
                    ex[pr, base + src, col:col + HEAD_DIM] = 1.0
    sel = np.zeros((2, blk, hp), np.float32)
    for hh in range(2):
        sel[hh, np.arange(blk), hh * HEAD_DIM + np.arange(blk) % CHUNK] = 1.0
    gp = LOCAL_PAIRS
    tok = lambda bi, pg, i: (bi, i, pg)
    dirs = lambda bi, pg, i: (0, bi, i, pg)
    out = jax.ShapeDtypeStruct((2, b, t, C_W), BF16)
    return pl.pallas_call(
        _gdn_local_kernel,
        grid=(b, n_pair // gp, t // blk),
        in_specs=[pl.BlockSpec((1, blk, gp * hp), tok)] * 3
                 + [pl.BlockSpec((1, blk, LANE), lambda bi, pg, i: (bi, i, 0)),
                    pl.BlockSpec((gp, LANE, 6 * hp), lambda bi, pg, i: (pg, 0, 0)),
                    pl.BlockSpec((2, blk, hp), lambda bi, pg, i: (0, 0, 0))],
        out_specs=[pl.BlockSpec((2, 1, blk, gp * hp), dirs)] * 5,
        out_shape=[out] * 5,
        compiler_params=_cparams(("parallel", "parallel", "parallel")),
        name="gdn_local",
    )(q, k, v, comp, jnp.asarray(ex, dtype=BF16), jnp.asarray(sel, dtype=BF16))


def _gdn_scan_kernel(*refs, ncb, nb):
    ins = (refs[0:6], refs[6:12])
    ex_ref = refs[12]
    outs = refs[13:15]
    s_ref = refs[15]
    j = pl.program_id(1)

    @pl.when(j == 0)
    def _():
        s_ref[...] = jnp.zeros_like(s_ref)

    gw = 4 * HEAD_DIM
    n_grp = C_W // gw
    lane_head = lax.broadcasted_iota(jnp.int32, (1, gw), 1) // HEAD_DIM
    rr = lax.broadcasted_iota(jnp.int32, (gw, gw), 0) // HEAD_DIM
    cc_ = lax.broadcasted_iota(jnp.int32, (gw, gw), 1) // HEAD_DIM
    blockdiag = rr == cc_
    chains = [(bi, p, g) for bi in range(nb) for p in range(2) for g in range(n_grp)]
    for cc in range(ncb):
        rows = [slice(cc * CHUNK, (cc + 1) * CHUNK), slice((ncb - 1 - cc) * CHUNK, (ncb - cc) * CHUNK)]
        sdec = {(bi, p): jnp.exp(_dot_exact_rhs(ins[p][5][bi, rows[p].start:rows[p].start + SUBLANE, :], ex_ref[...]))
                for bi in range(nb) for p in range(2)}
        lanes = [slice(g * gw, (g + 1) * gw) for g in range(n_grp)]
        ss = [s_ref[bi, p, g] for bi, p, g in chains]
        sbs = [s.astype(BF16) for s in ss]
        v_news = [ins[p][0][0, bi, rows[p], lanes[g]].astype(F32) - _dot(ins[p][1][0, bi, rows[p], lanes[g]], sb)
                  for (bi, p, g), sb in zip(chains, sbs)]
        vstacks = [jnp.concatenate([jnp.where(lane_head == h, v_new, 0.0).astype(BF16)
                                    for h in range(gw // HEAD_DIM)], axis=0) for v_new in v_news]
        for (bi, p, g), s, sb, v_new, vstack in zip(chains, ss, sbs, v_news, vstacks):
            o = _dot(ins[p][3][0, bi, rows[p], lanes[g]], sb) + _dot(ins[p][4][0, bi, rows[p], lanes[g]], vstack)
            upd = _dot_tn(ins[p][2][0, bi, rows[p], lanes[g]], v_new.astype(BF16))
            decay = sdec[bi, p][0:1, p * C_W + g * gw:p * C_W + (g + 1) * gw]
            s_ref[bi, p, g] = s * decay + jnp.where(blockdiag, upd, 0.0)
            outs[p][bi, rows[p], lanes[g]] = o


def _gdn_scan(u, w, kd, qd, intra, comp, n_lat, blk):
    _, b, t, _ = u.shape
    nl = n_lat // blk
    nx = (t - n_lat) // blk
    ncb = blk // CHUNK
    ex = np.zeros((LANE, 2 * C_W), np.float32)
    for p in range(2):
        for h in range(C_HEADS):
            ex[GL_LANE + p * C_HEADS + h, p * C_W + h * HEAD_DIM:p * C_W + (h + 1) * HEAD_DIM] = 1.0

    def blk0(j):
        return jnp.where(j < nx, nl + j, j - nx)

    def blk1(j):
        return jnp.where(j < nx, nl + nx - 1 - j, nl - 1 - (j - nx))

    nb = SCAN_BATCH if b % SCAN_BATCH == 0 else 1
    specs = []
    for p, bf in enumerate((blk0, blk1)):
        specs += [pl.BlockSpec((1, nb, blk, C_W), lambda bi, j, p=p, bf=bf: (p, bi, bf(j), 0))] * 5
        specs += [pl.BlockSpec((nb, blk, LANE), lambda bi, j, bf=bf: (bi, bf(j), 0))]
    specs += [pl.BlockSpec((LANE, 2 * C_W), lambda bi, j: (0, 0))]
    out = jax.ShapeDtypeStruct((b, t, C_W), F32)
    gw = 4 * HEAD_DIM
    return pl.pallas_call(
        functools.partial(_gdn_scan_kernel, ncb=ncb, nb=nb),
        grid=(b // nb, nl + nx),
        in_specs=specs,
        out_specs=[pl.BlockSpec((nb, blk, C_W), lambda bi, j: (bi, blk0(j), 0)),
                   pl.BlockSpec((nb, blk, C_W), lambda bi, j: (bi, blk1(j), 0))],
        out_shape=[out, out],
        scratch_shapes=[pltpu.VMEM((nb, 2, C_W // gw, gw, gw), F32)],
        compiler_params=_cparams(("parallel", "arbitrary")),
        name="gdn_scan",
    )(u, w, kd, qd, intra, comp, u, w, kd, qd, intra, comp, jnp.asarray(ex, dtype=BF16))


def _merge_kernel(x_ref, mod_ref, oa_ref, ob_ref, oc0_ref, oc1_ref, gate_ref, wmg_ref, wbr_ref, wout_ref, og_ref,
                  gs_ref, lng_ref, lnb_ref, wr_ref, o_ref, h_ref, aff_ref, *, alpha, n_exp):
    x = x_ref[0]
    d = x.shape[-1]
    mod = mod_ref[0, 0]
    h = (_ln(x) * (1.0 + mod[1:2]) + mod[0:1]).astype(BF16)
    gates = _sigmoid(_dot(h, wmg_ref[...]))
    oc = oc0_ref[0] + oc1_ref[0]
    ocn = oc * lax.rsqrt(_dot_exact_rhs(oc * oc, gs_ref[...], 2) + LN_EPS) * og_ref[...] * _silu(gate_ref[0])
    m = gates[:, :d] * _dot(oa_ref[0], wbr_ref[0])
    m = m + gates[:, d:2 * d] * _dot(ob_ref[0], wbr_ref[1])
    m = m + gates[:, 2 * d:] * _dot(ocn.astype(BF16), wbr_ref[2])
    mix = _dot(m.astype(BF16), wout_ref[...])
    y = alpha * x + mod[2:3] * mix
    x1 = _ln(y) * lng_ref[...] + lnb_ref[...]
    o_ref[0] = x1
    h2 = _ln(x1) * (1.0 + mod[4:5]) + mod[3:4]
    h_ref[0] = h2.astype(BF16)
    hp = _split_bf16(h2, 3)
    logits = sum(_dot(hp[i], wr_ref[j]) for i, j in ((2, 0), (1, 1), (0, 2), (1, 0), (0, 1), (0, 0)))
    lane = lax.broadcasted_iota(jnp.int32, logits.shape, 1)
    logits = jnp.where(lane < n_exp, logits, NEG_BIG)
    ex = jnp.exp(logits - jnp.max(logits, axis=-1, keepdims=True))
    aff_ref[0] = ex / jnp.sum(ex, axis=-1, keepdims=True)


def _merge(xa, mods, oa, ob, oc0, oc1, gate_c, wmg, wbr, wout, og, gs, lng, lnb, w_router, n_lat, tm, alpha):
    b, t, d = xa.shape
    nl = n_lat // tm
    n_exp = w_router.shape[-1]
    wr_pad = jnp.pad(w_router, ((0, 0), (0, LANE - n_exp)))
    pieces, rest = [], wr_pad
    for _ in range(3):
        piece = lax.reduce_precision(rest, exponent_bits=8, mantissa_bits=7)
        pieces.append(piece.astype(BF16))
        rest = rest - piece
    wr = jnp.stack(pieces)
    row = lambda bi, i: (bi, i, 0)
    const2 = lambda bi, i: (0, 0)
    return pl.pallas_call(
        functools.partial(_merge_kernel, alpha=alpha, n_exp=n_exp),
        grid=(b, t // tm),
        in_specs=[pl.BlockSpec((1, tm, d), row),
                  pl.BlockSpec((1, 1, N_MOD, d), lambda bi, i: (bi, (i >= nl).astype(jnp.int32), 0, 0)),
                  pl.BlockSpec((1, tm, BRANCH_W), row), pl.BlockSpec((1, tm, BRANCH_W), row),
                  pl.BlockSpec((1, tm, BRANCH_W), row), pl.BlockSpec((1, tm, BRANCH_W), row),
                  pl.BlockSpec((1, tm, BRANCH_W), row),
                  pl.BlockSpec(wmg.shape, const2),
                  pl.BlockSpec(wbr.shape, lambda bi, i: (0, 0, 0)),
                  pl.BlockSpec(wout.shape, const2),
                  pl.BlockSpec(og.shape, const2), pl.BlockSpec(gs.shape, const2),
                  pl.BlockSpec(lng.shape, const2), pl.BlockSpec(lnb.shape, const2),
                  pl.BlockSpec(wr.shape, lambda bi, i: (0, 0, 0))],
        out_specs=[pl.BlockSpec((1, tm, d), row), pl.BlockSpec((1, tm, d), row), pl.BlockSpec((1, tm, LANE), row)],
        out_shape=[jax.ShapeDtypeStruct((b, t, d), F32), jax.ShapeDtypeStruct((b, t, d), BF16),
                   jax.ShapeDtypeStruct((b, t, LANE), F32)],
        compiler_params=_cparams(("parallel", "parallel")),
        name="merge",
    )(xa, mods, oa, ob, oc0, oc1, gate_c, wmg, wbr, wout, og, gs, lng, lnb, wr)


def _ffn_kernel(x_ref, wg_ref, wu_ref, wd_ref, gate_ref, o_ref, acc_ref):
    f = pl.program_id(2)

    @pl.when(f == 0)
    def _():
        acc_ref[...] = jnp.zeros_like(acc_ref)

    x = x_ref[0]
    g = _dot(x, wg_ref[0, 0].astype(BF16))
    u = _dot(x, wu_ref[0, 0].astype(BF16))
    a = (_silu(g) * u).astype(BF16)
    acc_ref[...] += _dot(a, wd_ref[0, 0].astype(BF16))

    @pl.when(f == pl.num_programs(2) - 1)
    def _():
        o_ref[0] = (acc_ref[...] * gate_ref[0]).astype(BF16)


def _expert_ffn(xs, gate, w_gate_up, w_down, layer, tm, tf):
    e, r, d = xs.shape
    f = w_down.shape[2]
    nf = f // tf
    return pl.pallas_call(
        _ffn_kernel,
        grid=(e, r // tm, nf),
        in_specs=[pl.BlockSpec((1, tm, d), lambda ei, i, fi: (ei, i, 0)),
                  pl.BlockSpec((1, 1, d, tf), lambda ei, i, fi: (layer, ei, 0, fi)),
                  pl.BlockSpec((1, 1, d, tf), lambda ei, i, fi: (layer, ei, 0, nf + fi)),
                  pl.BlockSpec((1, 1, tf, d), lambda ei, i, fi: (layer, ei, fi, 0)),
                  pl.BlockSpec((1, tm, 1), lambda ei, i, fi: (ei, i, 0))],
        out_specs=pl.BlockSpec((1, tm, d), lambda ei, i, fi: (ei, i, 0)),
        out_shape=jax.ShapeDtypeStruct((e, r, d), BF16),
        scratch_shapes=[pltpu.VMEM((tm, d), F32)],
        compiler_params=_cparams(("parallel", "parallel", "arbitrary")),
        name="expert_ffn",
    )(xs, w_gate_up, w_gate_up, w_down, gate)


VALID, FIRST, LAST = 1, 2, 4


def _combine_plan(tok_of_pair, n_tok, tm, rwin):
    bsz, n_pair = tok_of_pair.shape
    nt = n_tok // tm
    n_win = n_pair // rwin
    n_item = nt + n_win
    order = jnp.argsort(tok_of_pair, axis=1)
    tok = jnp.take_along_axis(tok_of_pair, order, axis=1)
    bounds = jnp.arange(nt + 1, dtype=jnp.int32) * tm
    off = jnp.sum(tok[:, None, :] < bounds[None, :, None], axis=-1, dtype=jnp.int32)
    first = jnp.minimum(off[:, :-1] // rwin, n_win - 1)
    last = jnp.minimum(jnp.maximum(off[:, 1:] - 1, off[:, :-1]) // rwin, n_win - 1)
    n_items = last - first + 1
    cum = jnp.cumsum(n_items, axis=1)
    k = jnp.arange(n_item, dtype=jnp.int32)
    tile = jnp.sum(cum[:, None, :] <= k[None, :, None], axis=-1, dtype=jnp.int32)
    valid = tile < nt
    tile = jnp.minimum(tile, nt - 1)
    end = jnp.take_along_axis(cum, tile, axis=1)
    start = end - jnp.take_along_axis(n_items, tile, axis=1)
    win = jnp.where(valid, jnp.take_along_axis(first, tile, axis=1) + (k[None] - start),
                    jnp.take_along_axis(last, tile, axis=1))
    flags = (valid * VALID + (valid & (k[None] == start)) * FIRST + (valid & (k[None] == end - 1)) * LAST)
    return order, tok, tile, win.astype(jnp.int32), flags.astype(jnp.int32)


def _combine_kernel(tile_ref, win_ref, flag_ref, y_ref, tok_ref, x_ref, mod_ref, lng_ref, lnb_ref, o_ref, acc_ref, *,
                    alpha, tm):
    b = pl.program_id(0)
    k = pl.program_id(1)
    flags = flag_ref[b, k]

    @pl.when((flags & FIRST) != 0)
    def _():
        acc_ref[...] = jnp.zeros_like(acc_ref)

    @pl.when((flags & VALID) != 0)
    def _():
        tok0 = tile_ref[b, k] * tm
        rows = lax.broadcasted_iota(jnp.int32, (tm, tok_ref.shape[-1]), 0) + tok0
        pick = jnp.where(rows == tok_ref[0, 0], 1.0, 0.0).astype(BF16)
        acc_ref[...] += _dot(pick, y_ref[0])

    @pl.when((flags & LAST) != 0)
    def _():
        mod = mod_ref[0, 0]
        y = alpha * x_ref[0] + mod[5:6] * acc_ref[...]
        o_ref[0] = _ln(y) * lng_ref[...] + lnb_ref[...]


def _combine(y_sorted, tok_sorted, tile, win, flags, xa, mods, lng, lnb, n_lat, tm, rwin, alpha):
    b, t, d = xa.shape
    nl = n_lat // tm
    n_item = tile.shape[1]
    grid_spec = pltpu.PrefetchScalarGridSpec(
        num_scalar_prefetch=3,
        grid=(b, n_item),
        in_specs=[pl.BlockSpec((1, rwin, d), lambda bi, k, tl, wn, fl: (bi, wn[bi, k], 0)),
                  pl.BlockSpec((1, 1, 1, rwin), lambda bi, k, tl, wn, fl: (bi, wn[bi, k], 0, 0)),
                  pl.BlockSpec((1, tm, d), lambda bi, k, tl, wn, fl: (bi, tl[bi, k], 0)),
                  pl.BlockSpec((1, 1, N_MOD, d),
                               lambda bi, k, tl, wn, fl: (bi, (tl[bi, k] >= nl).astype(jnp.int32), 0, 0)),
                  pl.BlockSpec(lng.shape, lambda bi, k, tl, wn, fl: (0, 0)),
                  pl.BlockSpec(lnb.shape, lambda bi, k, tl, wn, fl: (0, 0))],
        out_specs=pl.BlockSpec((1, tm, d), lambda bi, k, tl, wn, fl: (bi, tl[bi, k], 0)),
        scratch_shapes=[pltpu.VMEM((tm, d), F32)],
    )
    return pl.pallas_call(
        functools.partial(_combine_kernel, alpha=alpha, tm=tm),
        grid_spec=grid_spec,
        out_shape=jax.ShapeDtypeStruct((b, t, d), F32),
        compiler_params=_cparams(("parallel", "arbitrary")),
        name="moe_combine",
    )(tile, win, flags, y_sorted, tok_sorted.reshape(b, -1, 1, rwin), xa, mods, lng, lnb)


def _moe(xa, h, aff, mods, n_exp, w_gate_up, w_down, lng, lnb, layer, n_lat, tm, alpha):
    b, t, d = xa.shape
    m_ctx = t - n_lat
    aff = jnp.swapaxes(aff[:, :, :n_exp], 1, 2)
    cap_l = CAPACITY * n_lat // n_exp
    cap_x = CAPACITY * m_ctx // n_exp
    gate_l, idx_l = lax.top_k(aff[:, :, :n_lat], cap_l)
    gate_x, idx_x = lax.top_k(aff[:, :, n_lat:], cap_x)
    idx = jnp.concatenate([idx_l, idx_x + n_lat], axis=-1)
    gate = jnp.concatenate([gate_l, gate_x], axis=-1)
    cap = cap_l + cap_x
    idx_e = jnp.swapaxes(idx, 0, 1)
    bidx = jnp.arange(b)[None, :, None]
    xs = h[bidx, idx_e].reshape(n_exp, b * cap, d)
    gate_e = jnp.swapaxes(gate, 0, 1).reshape(n_exp, b * cap, 1)
    rows = b * cap
    tmr = rows
    for cand in (1088, 1024, 512, 256, 128, 64, 32, 16):
        if rows % cand == 0:
            tmr = cand
            break
    tf = min(512, w_down.shape[2])
    y = _expert_ffn(xs, gate_e, w_gate_up, w_down, layer, tmr, tf)
    n_pair = n_exp * cap
    rwin = next(r for r in (512, 256, 128, 64, 32, 16, 8) if n_pair % r == 0)
    order, tok, tile, win, flags = _combine_plan(idx.reshape(b, n_pair), t, tm, rwin)
    y_sorted = y[order // cap, jnp.arange(b)[:, None] * cap + order % cap]
    return _combine(y_sorted, tok, tile, win, flags, xa, mods, lng, lnb, n_lat, tm, rwin, alpha)


def _rope_tables(n_lat, m_ctx):
    t = np.arange(n_lat)
    rows = (t // GRID_W).astype(np.float32)
    cols = (t % GRID_W).astype(np.float32)
    n_freq = HEAD_DIM // 4
    inv_freq = jnp.asarray(ROPE_THETA, F32) ** (-jnp.arange(n_freq, dtype=F32) / n_freq)
    ang_r = jnp.asarray(rows)[:, None] * inv_freq
    ang_c = jnp.asarray(cols)[:, None] * inv_freq
    cr, sr, cc, sc = jnp.cos(ang_r), jnp.sin(ang_r), jnp.cos(ang_c), jnp.sin(ang_c)
    cos = jnp.concatenate([cr, cr, cc, cc], axis=-1)
    sin = jnp.concatenate([-sr, sr, -sc, sc], axis=-1)
    cos = jnp.concatenate([cos, jnp.ones((m_ctx, HEAD_DIM), F32)], axis=0)
    sin = jnp.concatenate([sin, jnp.zeros((m_ctx, HEAD_DIM), F32)], axis=0)
    return jnp.tile(cos, (1, A_HEADS)), jnp.tile(sin, (1, A_HEADS))


def kernel(x, c, ctx, c_ctx, w_mod, b_mod, w_in, qk_gain, rpb, conv_w, a_log, dt_bias, o_gain, w_branch, w_out,
           ln1_g, ln1_b, w_router, w_gate_up, w_down, ln2_g, ln2_b):
    b, n_lat, d = x.shape
    m_ctx = ctx.shape[1]
    depth = w_mod.shape[0]
    alpha = (2 * depth) ** 0.25
    tm = math.gcd(256, math.gcd(n_lat, m_ctx))
    blk = tm

    xa = jnp.concatenate([x, ctx], axis=1)
    n_rows = -(-(b + 1) // SUBLANE) * SUBLANE
    cc = jnp.zeros((n_rows, d), F32).at[:b].set(c).at[b].set(c_ctx)
    mod_all = _modulation(cc, w_mod, b_mod).reshape(depth, n_rows, N_MOD, d)
    cos, sin = _rope_tables(n_lat, m_ctx)
    gs_mean = _group_sum_matrix(A_Q_W, 1.0 / HEAD_DIM)
    gs_sum = _group_sum_matrix(C_W, 1.0)
    na_geoms, na_table = _na_classes(n_lat, n_lat + m_ctx)

    offs = np.cumsum((0, A_Q_W, A_KV_W, A_KV_W, B_W, B_W, B_W, 3 * C_W, C_W, 2 * C_HEADS, 2 * C_HEADS, N_BRANCH * d))
    for layer in range(depth):
        ml = mod_all[layer]
        mods = jnp.stack([ml[:b], jnp.broadcast_to(ml[b][None], (b, N_MOD, d))], axis=1)
        wl = w_in[layer]
        wa = wl[:, offs[0]:offs[3]].astype(BF16)
        wb = wl[:, offs[3]:offs[6]].astype(BF16)
        wc = wl[:, offs[6]:offs[8]].astype(BF16)
        wab = jnp.pad(wl[:, offs[8]:offs[10]], ((0, 0), (0, LANE - 4 * C_HEADS))).astype(BF16)
        wmg = wl[:, offs[10]:offs[11]].astype(BF16)
        gq = jnp.tile(qk_gain[layer, 0], A_HEADS)[None]
        gk = jnp.tile(qk_gain[layer, 1], A_KV_HEADS)[None]
        qa, ka, va, qb, kb, vb, qkvc, gate_c, ab = _in_projection(
            xa, mods, wa, wb, wc, wab, gq, gk, cos, sin, gs_mean, n_lat, tm)
        oa = _gqa(qa, ka, va, n_lat, min(GQA_TQ, tm), 2 if (n_lat + m_ctx) % (2 * LANE) == 0 else 1)
        ob = _neighborhood(qb, kb, vb, _na_bias_slabs(rpb[layer], na_geoms), na_table, n_lat)
        qc, kc, vc, comp = _gdn_prep(qkvc, ab, conv_w[layer], a_log[layer], dt_bias[layer], gs_sum, n_lat, tm)
        u, w, kd, qd, intra = _gdn_local(qc, kc, vc, comp, blk)
        oc0, oc1 = _gdn_scan(u, w, kd, qd, intra, comp, n_lat, blk)
        og = jnp.tile(o_gain[layer], C_HEADS)[None]
        xa, h, aff = _merge(xa, mods, oa, ob, oc0, oc1, gate_c, wmg, w_branch[layer].astype(BF16),
                            w_out[layer].astype(BF16), og, gs_mean, ln1_g[layer][None], ln1_b[layer][None],
                            w_router[layer], n_lat, tm, alpha)
        xa = _moe(xa, h, aff, mods, w_router.shape[-1], w_gate_up, w_down, ln2_g[layer][None], ln2_b[layer][None],
                  layer, n_lat, tm, alpha)
    return xa[:, :n_lat]
```

```python
import functools
import math

import numpy as np
import jax
import jax.numpy as jnp
from jax import lax
from jax.experimental import pallas as pl
from jax.experimental.pallas import tpu as pltpu

F32 = jnp.float32
BF16 = jnp.bfloat16
HIGHEST = lax.Precision.HIGHEST

GRID_W = 64
HEAD_DIM = 64
A_HEADS = 8
A_KV_HEADS = 2
B_HEADS = 8
C_HEADS = 8
NA_ROWS = 8
NA_COLS = 16
CONV_K = 4
CHUNK = 64
N_BRANCH = 3
BRANCH_W = 512
CAPACITY = 2
N_MOD = 6
LN_EPS = 1e-6
ROPE_THETA = 10000.0
NEG_BIG = -1e30
LOG2E = 1.4426950408889634

A_Q_W = A_HEADS * HEAD_DIM
A_KV_W = A_KV_HEADS * HEAD_DIM
B_W = B_HEADS * HEAD_DIM
C_W = C_HEADS * HEAD_DIM
LANE = 128
SUBLANE = 8
V7X_VMEM_BYTES = 64 * 1024 * 1024
VMEM_LIMIT = V7X_VMEM_BYTES * 7 // 8
GQA_TQ = 128
LOCAL_PAIRS = 2
SCAN_BATCH = 2
VT_ROWS = HEAD_DIM + 16
NA_WIN = NA_ROWS + 2


def _cparams(sem):
    return pltpu.CompilerParams(dimension_semantics=sem, vmem_limit_bytes=VMEM_LIMIT)


def _dot(a, b):
    return jnp.dot(a, b, preferred_element_type=F32)


def _dot_hi(a, b):
    return jnp.dot(a, b, precision=HIGHEST, preferred_element_type=F32)


def _dot_nt(a, b):
    return lax.dot_general(a, b, (((1,), (1,)), ((), ())), preferred_element_type=F32)


def _dot_tn(a, b):
    return lax.dot_general(a, b, (((0,), (0,)), ((), ())), preferred_element_type=F32)


def _split_bf16(x, pieces):
    out = []
    for _ in range(pieces - 1):
        h = x.astype(BF16)
        out.append(h)
        x = x - h.astype(F32)
    out.append(x.astype(BF16))
    return out


def _dot_exact_rhs(a, b, pieces=3):
    return sum(_dot(p, b) for p in _split_bf16(a, pieces))


def _dot_exact_lhs(a, b, pieces=3):
    return sum(_dot(a, p) for p in _split_bf16(b, pieces))


def _ln(x):
    mu = jnp.mean(x, axis=-1, keepdims=True)
    xc = x - mu
    var = jnp.mean(xc * xc, axis=-1, keepdims=True)
    return xc * lax.rsqrt(var + LN_EPS)


def _sigmoid(x):
    return 1.0 / (1.0 + jnp.exp(-x))


def _silu(x):
    return x * _sigmoid(x)


def _group_sum_matrix(width, scale):
    g = np.arange(width) // HEAD_DIM
    return jnp.asarray((g[:, None] == g[None, :]).astype(np.float32) * scale, dtype=BF16)


def _mod_kernel(c_ref, w_ref, b_ref, o_ref):
    c = c_ref[...]
    o_ref[0] = _dot_hi(_silu(c), w_ref[0]) + b_ref[0]


def _modulation(cc, w_mod, b_mod):
    depth, d, dm = w_mod.shape
    r = cc.shape[0]
    tn = min(dm, 1536)
    return pl.pallas_call(
        _mod_kernel,
        grid=(depth, dm // tn),
        in_specs=[pl.BlockSpec((r, d), lambda l, j: (0, 0)),
                  pl.BlockSpec((1, d, tn), lambda l, j: (l, 0, j)),
                  pl.BlockSpec((1, 1, tn), lambda l, j: (l, 0, j))],
        out_specs=pl.BlockSpec((1, r, tn), lambda l, j: (l, 0, j)),
        out_shape=jax.ShapeDtypeStruct((depth, r, dm), F32),
        compiler_params=_cparams(("parallel", "parallel")),
        name="modulation",
    )(cc, w_mod, b_mod.reshape(depth, 1, dm))


def _rope_swap(y):
    w = y.shape[-1]
    lane = lax.broadcasted_iota(jnp.int32, y.shape, 1)
    up = pltpu.roll(y, w - 16, 1)
    down = pltpu.roll(y, 16, 1)
    return jnp.where((lane % 32) < 16, up, down)


def _inproj_kernel(x_ref, mod_ref, wa_ref, wb_ref, wc_ref, wab_ref, gq_ref, gk_ref, cos_ref, sin_ref,
                   gs_ref, qa_ref, ka_ref, va_ref, qb_ref, kb_ref, vb_ref, qkvc_ref, gatec_ref, ab_ref):
    x = x_ref[0]
    mod = mod_ref[0, 0]
    h = (_ln(x) * (1.0 + mod[1:2]) + mod[0:1]).astype(BF16)
    scale = HEAD_DIM ** -0.5 * LOG2E

    za = _dot(h, wa_ref[...])
    q = za[:, :A_Q_W]
    k = za[:, A_Q_W:A_Q_W + A_KV_W]
    v = za[:, A_Q_W + A_KV_W:]
    cos = cos_ref[...]
    sin = sin_ref[...]
    gs = gs_ref[...]
    qn = q * lax.rsqrt(_dot_exact_rhs(q * q, gs, 2) + LN_EPS) * gq_ref[...]
    qn = qn * cos + _rope_swap(qn) * sin
    qa_ref[0] = (qn * scale).astype(BF16)
    kn = k * lax.rsqrt(_dot_exact_rhs(k * k, gs[:A_KV_W, :A_KV_W], 2) + LN_EPS) * gk_ref[...]
    kn = kn * cos[:, :A_KV_W] + _rope_swap(kn) * sin[:, :A_KV_W]
    ones = jnp.ones((VT_ROWS - HEAD_DIM, x.shape[0]), F32)
    vt = v.T
    for j in range(A_KV_HEADS):
        ka_ref[0, j] = kn[:, j * HEAD_DIM:(j + 1) * HEAD_DIM].astype(BF16)
        va_ref[0, j] = jnp.concatenate([vt[j * HEAD_DIM:(j + 1) * HEAD_DIM], ones], axis=0).astype(BF16)

    zb = _dot(h, wb_ref[...])
    qb_ref[0] = (zb[:, :B_W] * scale).astype(BF16)
    kbv = zb[:, B_W:2 * B_W]
    vt = zb[:, 2 * B_W:].T
    for j in range(B_HEADS):
        kb_ref[0, j] = kbv[:, j * HEAD_DIM:(j + 1) * HEAD_DIM].astype(BF16)
        vb_ref[0, j] = jnp.concatenate([vt[j * HEAD_DIM:(j + 1) * HEAD_DIM], ones], axis=0).astype(BF16)

    zc = _dot(h, wc_ref[...])
    qkvc_ref[0] = zc[:, :3 * C_W]
    gatec_ref[0] = zc[:, 3 * C_W:]
    ab_ref[0] = _dot(h, wab_ref[...])


def _in_projection(xa, mods, wa, wb, wc, wab, gq, gk, cos, sin, gs, n_lat, tm):
    b, t, d = xa.shape
    nl = n_lat // tm
    row = lambda bi, i: (bi, i, 0)
    const = lambda bi, i: (0, 0)
    outs = [
        (jax.ShapeDtypeStruct((b, t, A_Q_W), BF16), pl.BlockSpec((1, tm, A_Q_W), row)),
        (jax.ShapeDtypeStruct((b, A_KV_HEADS, t, HEAD_DIM), BF16),
         pl.BlockSpec((1, A_KV_HEADS, tm, HEAD_DIM), lambda bi, i: (bi, 0, i, 0))),
        (jax.ShapeDtypeStruct((b, A_KV_HEADS, VT_ROWS, t), BF16),
         pl.BlockSpec((1, A_KV_HEADS, VT_ROWS, tm), lambda bi, i: (bi, 0, 0, i))),
        (jax.ShapeDtypeStruct((b, t, B_W), BF16), pl.BlockSpec((1, tm, B_W), row)),
        (jax.ShapeDtypeStruct((b, B_HEADS, t, HEAD_DIM), BF16),
         pl.BlockSpec((1, B_HEADS, tm, HEAD_DIM), lambda bi, i: (bi, 0, i, 0))),
        (jax.ShapeDtypeStruct((b, B_HEADS, VT_ROWS, t), BF16),
         pl.BlockSpec((1, B_HEADS, VT_ROWS, tm), lambda bi, i: (bi, 0, 0, i))),
        (jax.ShapeDtypeStruct((b, t, 3 * C_W), F32), pl.BlockSpec((1, tm, 3 * C_W), row)),
        (jax.ShapeDtypeStruct((b, t, C_W), F32), pl.BlockSpec((1, tm, C_W), row)),
        (jax.ShapeDtypeStruct((b, t, LANE), F32), pl.BlockSpec((1, tm, LANE), row)),
    ]
    return pl.pallas_call(
        _inproj_kernel,
        grid=(b, t // tm),
        in_specs=[pl.BlockSpec((1, tm, d), row),
                  pl.BlockSpec((1, 1, N_MOD, d), lambda bi, i: (bi, (i >= nl).astype(jnp.int32), 0, 0)),
                  pl.BlockSpec(wa.shape, const), pl.BlockSpec(wb.shape, const),
                  pl.BlockSpec(wc.shape, const), pl.BlockSpec(wab.shape, const),
                  pl.BlockSpec(gq.shape, const), pl.BlockSpec(gk.shape, const),
                  pl.BlockSpec((tm, A_Q_W), lambda bi, i: (i, 0)),
                  pl.BlockSpec((tm, A_Q_W), lambda bi, i: (i, 0)),
                  pl.BlockSpec(gs.shape, const)],
        out_specs=[o[1] for o in outs],
        out_shape=[o[0] for o in outs],
        compiler_params=_cparams(("parallel", "parallel")),
        name="in_projection",
    )(xa, mods, wa, wb, wc, wab, gq, gk, cos, sin, gs)


def _gqa_kernel(q_ref, k_ref, v_ref, o_ref, *, tq, ck, n_lat, n_tot):
    i = pl.program_id(1)
    g = A_HEADS // A_KV_HEADS

    def attend(key_lo, key_hi, n_split):
        qs = [jnp.concatenate([q_ref[0, :, (kh * g + j) * HEAD_DIM:(kh * g + j + 1) * HEAD_DIM] for j in range(g)],
                              axis=0) for kh in range(A_KV_HEADS)]
        step = (key_hi - key_lo) // n_split
        units = [(kh, key_lo + r * step, key_lo + (r + 1) * step) for r in range(n_split) for kh in range(A_KV_HEADS)]
        ss, ms, ps, oes = {}, {}, {}, {}
        for n in range(len(units) + 2):
            if n < len(units):
                kh, lo, hi = units[n]
                ss[n] = _dot_nt(k_ref[0, kh, lo:hi, :], qs[kh])
            if 0 <= n - 1 < len(units):
                ms[n - 1] = jnp.max(ss[n - 1], axis=0, keepdims=True).astype(BF16).astype(F32)
                ps[n - 1] = jnp.exp2(ss.pop(n - 1).astype(BF16) - ms[n - 1].astype(BF16))
            if 0 <= n - 2 < len(units):
                kh, lo, hi = units[n - 2]
                oes[n - 2] = _dot(v_ref[0, kh, :, lo:hi], ps.pop(n - 2))
        for kh in range(A_KV_HEADS):
            mine = [n for n, u in enumerate(units) if u[0] == kh]
            m = functools.reduce(jnp.maximum, [ms[n] for n in mine])
            oe = sum(oes[n] * jnp.exp2(ms[n] - m) for n in mine)
            o = (oe[:HEAD_DIM] / oe[HEAD_DIM:HEAD_DIM + 1]).T.astype(BF16)
            for j in range(g):
                o_ref[0, :, (kh * g + j) * HEAD_DIM:(kh * g + j + 1) * HEAD_DIM] = o[j * tq:(j + 1) * tq]

    @pl.when(i < n_lat // tq)
    def _():
        attend(0, n_tot, ck)

    @pl.when(i >= n_lat // tq)
    def _():
        attend(n_lat, n_tot, 1)


def _gqa(qa, ka, va, n_lat, tq, ck):
    b, t, _ = qa.shape
    return pl.pallas_call(
        functools.partial(_gqa_kernel, tq=tq, ck=ck, n_lat=n_lat, n_tot=t),
        grid=(b, t // tq),
        in_specs=[pl.BlockSpec((1, tq, A_Q_W), lambda bi, i: (bi, i, 0)),
                  pl.BlockSpec((1, A_KV_HEADS, t, HEAD_DIM), lambda bi, i: (bi, 0, 0, 0)),
                  pl.BlockSpec((1, A_KV_HEADS, VT_ROWS, t), lambda bi, i: (bi, 0, 0, 0))],
        out_specs=pl.BlockSpec((1, tq, A_Q_W), lambda bi, i: (bi, i, 0)),
        out_shape=jax.ShapeDtypeStruct((b, t, A_Q_W), BF16),
        compiler_params=_cparams(("parallel", "parallel")),
        name="gqa",
    )(qa, ka, va)


def _na_window_start(r0, rows):
    return np.clip(r0 - NA_ROWS // 2, 0, rows - NA_ROWS - 1) // 2 * 2


def _na_classes(n_lat, n_tot):
    rows = n_lat // GRID_W
    assert rows >= NA_WIN and rows % 2 == 0
    geoms, table = [], []
    for r0 in range(0, rows, 2):
        geom = (int(_na_window_start(r0, rows)) - r0,) + tuple(
            int(np.clip(r0 + qr - NA_ROWS // 2, 0, rows - NA_ROWS)) - r0 for qr in range(2))
        if geom not in geoms:
            geoms.append(geom)
        table.append(geoms.index(geom))
    table += [len(geoms)] * ((n_tot - n_lat) // (2 * GRID_W))
    return geoms, np.asarray(table, np.int32)


def _na_bias_slabs(rpb, geoms):
    n_head = rpb.shape[0]
    cj = np.arange(GRID_W)[:, None]
    c = np.arange(GRID_W)[None, :]
    col_start = np.clip(c - NA_COLS // 2, 0, GRID_W - NA_COLS)
    col_ok = (cj >= col_start) & (cj < col_start + NA_COLS)
    col_off = np.clip(cj - c + (NA_COLS - 1), 0, 2 * NA_COLS - 2)
    spread = np.zeros((2 * NA_COLS - 1, GRID_W * GRID_W), np.float32)
    spread[col_off.reshape(-1), np.arange(GRID_W * GRID_W)] = 1.0
    kr = np.arange(NA_WIN)[:, None]
    qr = np.arange(2)[None, :]
    pick = np.zeros((len(geoms), NA_WIN, 2, 2 * NA_ROWS - 1), np.float32)
    ok = np.zeros((len(geoms), 1, NA_WIN, 2, GRID_W, GRID_W), bool)
    for n, (a, b0, b1) in enumerate(geoms):
        first = np.where(qr == 0, b0, b1)
        row_ok = (a + kr >= first) & (a + kr < first + NA_ROWS)
        row_off = np.clip(a + kr - qr + (NA_ROWS - 1), 0, 2 * NA_ROWS - 2)
        pick[n, kr, qr, row_off] = 1.0
        ok[n, 0] = row_ok[:, :, None, None] & col_ok[None, None]
    rows = jnp.einsum("nkqo,hoc->nhkqc", jnp.asarray(pick), rpb * LOG2E, precision=HIGHEST)
    full = jnp.dot(rows, jnp.asarray(spread), precision=HIGHEST)
    full = full.reshape(len(geoms), n_head, NA_WIN, 2, GRID_W, GRID_W)
    masked = jnp.where(jnp.asarray(ok), full, NEG_BIG)
    slabs = jnp.concatenate([masked[:, :, :, 0], masked[:, :, :, 1]], axis=-1)
    slabs = slabs.reshape(len(geoms), n_head, NA_WIN * GRID_W, 2 * GRID_W)
    return jnp.concatenate([slabs, jnp.full_like(slabs[:1], NEG_BIG)], axis=0).astype(F32)


def _na_kernel(cls_ref, q_ref, k_ref, vt_ref, bias_ref, o_ref, *, n_lat, n_tot):
    del cls_ref
    rows = n_lat // GRID_W
    r0 = 2 * pl.program_id(1)
    start = jnp.clip(r0 - NA_ROWS // 2, 0, rows - NA_ROWS - 1) // 2 * 2
    nb = pl.ds(pl.multiple_of(start * GRID_W, 2 * GRID_W), NA_WIN * GRID_W)
    hs = [slice(h * HEAD_DIM, (h + 1) * HEAD_DIM) for h in range(B_HEADS)]
    qs = [q_ref[0, :, hs[h]] for h in range(B_HEADS)]
    s_nb = [_dot_nt(k_ref[0, h, nb, :], qs[h]) + bias_ref[0, h] for h in range(B_HEADS)]
    s_cx = [_dot_nt(k_ref[0, h, n_lat:n_tot, :], qs[h]) for h in range(B_HEADS)]
    ms = [jnp.maximum(jnp.max(a, axis=0, keepdims=True), jnp.max(c, axis=0, keepdims=True))
          for a, c in zip(s_nb, s_cx)]
    p_nb = [jnp.exp2((a - m).astype(BF16)) for a, m in zip(s_nb, ms)]
    p_cx = [jnp.exp2((c - m).astype(BF16)) for c, m in zip(s_cx, ms)]
    for h in range(B_HEADS):
        oe = _dot(vt_ref[0, h, :, nb], p_nb[h]) + _dot(vt_ref[0, h, :, n_lat:n_tot], p_cx[h])
        o_ref[0, :, hs[h]] = (oe[:HEAD_DIM] / oe[HEAD_DIM:HEAD_DIM + 1]).T.astype(BF16)


def _neighborhood(qb, kb, vbt, slabs, table, n_lat):
    b, t, _ = qb.shape
    tq = 2 * GRID_W
    grid_spec = pltpu.PrefetchScalarGridSpec(
        num_scalar_prefetch=1,
        grid=(b, t // tq),
        in_specs=[pl.BlockSpec((1, tq, B_W), lambda bi, pi, cls: (bi, pi, 0)),
                  pl.BlockSpec((1, B_HEADS, t, HEAD_DIM), lambda bi, pi, cls: (bi, 0, 0, 0)),
                  pl.BlockSpec((1, B_HEADS, VT_ROWS, t), lambda bi, pi, cls: (bi, 0, 0, 0)),
                  pl.BlockSpec((1,) + slabs.shape[1:], lambda bi, pi, cls: (cls[pi], 0, 0, 0))],
        out_specs=pl.BlockSpec((1, tq, B_W), lambda bi, pi, cls: (bi, pi, 0)),
    )
    return pl.pallas_call(
        functools.partial(_na_kernel, n_lat=n_lat, n_tot=t),
        grid_spec=grid_spec,
        out_shape=jax.ShapeDtypeStruct((b, t, B_W), BF16),
        compiler_params=_cparams(("parallel", "parallel")),
        name="neighborhood",
    )(jnp.asarray(table), qb, kb, vbt, slabs)


GC_LANE, GL_LANE, BETA_LANE = 0, 2 * C_HEADS, 4 * C_HEADS


def _gdn_prep_kernel(x_ref, prev_ref, next_ref, ab_ref, cw_ref, alog_ref, dtb_ref, gs_ref,
                     q_ref, k_ref, v_ref, comp_ref, *, tm, n_lat, n_tot):
    i = pl.program_id(1)
    nl = n_lat // tm
    nt = n_tot // tm
    first = jnp.logical_or(i == 0, i == nl)
    last = jnp.logical_or(i == nl - 1, i == nt - 1)
    x = x_ref[0]
    pv = jnp.where(first, 0.0, prev_ref[0])
    nx = jnp.where(last, 0.0, next_ref[0])
    row = lax.broadcasted_iota(jnp.int32, (tm, 1), 0)
    prev1 = pv[SUBLANE - 1:SUBLANE]
    prev2 = pv[SUBLANE - 2:SUBLANE - 1]
    xm1 = jnp.where(row == 0, prev1, pltpu.roll(x, 1, 0))
    xm2 = jnp.where(row == 0, prev2, jnp.where(row == 1, prev1, pltpu.roll(x, 2, 0)))
    xp1 = jnp.where(row == tm - 1, nx[0:1], pltpu.roll(x, tm - 1, 0))
    cw = cw_ref[...]
    y = _silu(xm2 * cw[0:1] + xm1 * cw[1:2] + x * cw[2:3] + xp1 * cw[3:4])
    q = y[:, :C_W]
    k = y[:, C_W:2 * C_W]
    gs = gs_ref[...]
    q_ref[0] = q * lax.rsqrt(_dot_exact_rhs(q * q, gs, 2) + LN_EPS) * (HEAD_DIM ** -0.5)
    k_ref[0] = k * lax.rsqrt(_dot_exact_rhs(k * k, gs, 2) + LN_EPS)
    v_ref[0] = y[:, 2 * C_W:]
    ab = ab_ref[0]
    z = ab + dtb_ref[...]
    softplus = jnp.maximum(z, 0.0) + jnp.log(1.0 + jnp.exp(-jnp.abs(z)))
    lane = lax.broadcasted_iota(jnp.int32, ab.shape, 1)
    gb = jnp.where(lane < 2 * C_HEADS, -jnp.exp(alog_ref[...]) * softplus, _sigmoid(ab))
    ii = lax.broadcasted_iota(jnp.int32, (tm, tm), 0)
    jj = lax.broadcasted_iota(jnp.int32, (tm, tm), 1)
    same = (ii // CHUNK) == (jj // CHUNK)
    gb3 = _split_bf16(gb, 3)

    def masked_sum(mask):
        mb = jnp.where(mask, 1.0, 0.0).astype(BF16)
        return sum(_dot(mb, piece) for piece in gb3)

    prefix = masked_sum(jnp.logical_and(same, jj <= ii))
    suffix = masked_sum(jnp.logical_and(same, jj >= ii))
    total = masked_sum(same)
    comp = jnp.where(lane < C_HEADS, prefix,
                     jnp.where(lane < GL_LANE, suffix,
                               jnp.where(lane < BETA_LANE, pltpu.roll(total, GL_LANE, 1),
                                         jnp.where(lane < BETA_LANE + 2 * C_HEADS, pltpu.roll(gb, 2 * C_HEADS, 1),
                                                   0.0))))
    comp_ref[0] = comp


def _gdn_prep(qkvc, ab, conv_w, a_log, dt_bias, gs, n_lat, tm):
    b, t, _ = qkvc.shape
    c3 = 3 * C_W
    alog_row = jnp.zeros((1, LANE), F32).at[0, :2 * C_HEADS].set(a_log.reshape(-1))
    dtb_row = jnp.zeros((1, LANE), F32).at[0, :2 * C_HEADS].set(dt_bias.reshape(-1))
    row = lambda bi, i: (bi, i, 0)
    const2 = lambda bi, i: (0, 0)
    hb = tm // SUBLANE
    last8 = t // SUBLANE - 1
    tok = jax.ShapeDtypeStruct((b, t, C_W), F32)
    return pl.pallas_call(
        functools.partial(_gdn_prep_kernel, tm=tm, n_lat=n_lat, n_tot=t),
        grid=(b, t // tm),
        in_specs=[pl.BlockSpec((1, tm, c3), row),
                  pl.BlockSpec((1, SUBLANE, c3), lambda bi, i: (bi, jnp.maximum(i * hb - 1, 0), 0)),
                  pl.BlockSpec((1, SUBLANE, c3), lambda bi, i: (bi, jnp.minimum((i + 1) * hb, last8), 0)),
                  pl.BlockSpec((1, tm, LANE), row),
                  pl.BlockSpec((CONV_K, c3), const2),
                  pl.BlockSpec((1, LANE), const2), pl.BlockSpec((1, LANE), const2),
                  pl.BlockSpec(gs.shape, const2)],
        out_specs=[pl.BlockSpec((1, tm, C_W), row)] * 3 + [pl.BlockSpec((1, tm, LANE), row)],
        out_shape=[tok, tok, tok, jax.ShapeDtypeStruct((b, t, LANE), F32)],
        compiler_params=_cparams(("parallel", "parallel")),
        name="gdn_prep",
    )(qkvc, qkvc, qkvc, ab, conv_w, alog_row, dtb_row, gs)


def _half_block_rows(a, h, upper):
    return jnp.concatenate([a[s:s + h] for s in range(0 if upper else h, a.shape[0], 2 * h)], axis=0)


def _with_half_block_rows(full, part, h, upper):
    pieces = []
    for blk in range(full.shape[0] // h):
        if (blk % 2 == 0) == upper:
            pieces.append(part[(blk // 2) * h:(blk // 2 + 1) * h])
        else:
            pieces.append(full[blk * h:(blk + 1) * h])
    return jnp.concatenate(pieces, axis=0)


def _unit_tri_inverses(lms, uppers, eye, same_block):
    def mm(a, b_):
        return _dot(a.astype(BF16), b_.astype(BF16))

    lds = [jnp.where(same_block[SUBLANE], lm, 0.0) for lm in lms]
    xs = [eye - ld for ld in lds]
    lds = [ld.astype(BF16) for ld in lds]
    pws = [_dot(ld, ld).astype(BF16) for ld in lds]
    xs = [x + mm(x, pw) for x, pw in zip(xs, pws)]
    pws = [_dot(pw, pw).astype(BF16) for pw in pws]
    xs = [x + mm(x, pw) for x, pw in zip(xs, pws)]
    h = SUBLANE
    while h < CHUNK:
        level = jnp.logical_not(same_block[h]) if 2 * h == CHUNK else jnp.logical_and(
            same_block[2 * h], jnp.logical_not(same_block[h]))
        xbs = [x.astype(BF16) for x in xs]
        ls = [_half_block_rows(jnp.where(level, lm, 0.0), h, up).astype(BF16) for lm, up in zip(lms, uppers)]
        ys = [_dot(l, xb) for l, xb in zip(ls, xbs)]
        zs = [_with_half_block_rows(jnp.zeros_like(x), y, h, up).astype(BF16) for x, y, up in zip(xs, ys, uppers)]
        xrs = [_half_block_rows(x, h, up) for x, up in zip(xs, uppers)]
        upd = [_dot(xr.astype(BF16), z) for xr, z in zip(xrs, zs)]
        xs = [_with_half_block_rows(x, xr - u, h, up) for x, xr, u, up in zip(xs, xrs, upd, uppers)]
        h *= 2
    return xs


def _gdn_local_kernel(q_ref, k_ref, v_ref, comp_ref, ex_ref, u_ref, w_ref, kd_ref, qd_ref, in_ref):
    n = q_ref.shape[1]
    hp = 2 * HEAD_DIM
    n_pr = q_ref.shape[2] // hp
    ii = lax.broadcasted_iota(jnp.int32, (n, n), 0)
    jj = lax.broadcasted_iota(jnp.int32, (n, n), 1)
    same = (ii // CHUNK) == (jj // CHUNK)
    eye = (ii == jj).astype(F32)
    same_block = {h: (ii // h) == (jj // h) for h in (8, 16, 32)}
    lane = lax.broadcasted_iota(jnp.int32, (1, hp), 1)
    lane2 = lax.broadcasted_iota(jnp.int32, (1, 2 * hp), 1)
    incls = [jnp.logical_and(same, ii >= jj), jnp.logical_and(same, ii <= jj)]
    stricts = [jnp.logical_and(same, ii > jj), jnp.logical_and(same, ii < jj)]
    lms, intras, rhss = [], [], []
    for pr in range(n_pr):
        ls = slice(pr * hp, (pr + 1) * hp)
        q = q_ref[0, :, ls]
        k = k_ref[0, :, ls]
        v = v_ref[0, :, ls]
        e = _dot_exact_rhs(comp_ref[0], ex_ref[pr])
        kh = [jnp.where((lane // HEAD_DIM) == hh, k, 0.0).astype(BF16) for hh in range(2)]
        for p in range(2):
            gc = e[:, p * hp:(p + 1) * hp]
            gl = e[:, (2 + p) * hp:(3 + p) * hp]
            beta = e[:, (4 + p) * hp:(5 + p) * hp]
            eg = jnp.exp(gc)
            kb = k * beta
            kd_ref[p, 0, :, ls] = (k * jnp.exp(gl - gc)).astype(BF16)
            qd_ref[p, 0, :, ls] = (q * eg).astype(BF16)
            rhs = jnp.concatenate([v * beta, kb * eg], axis=1)
            kbq = jnp.concatenate([kb, q], axis=0).astype(BF16)
            for hh in range(2):
                gcb = jnp.broadcast_to(gc[:, hh * HEAD_DIM:hh * HEAD_DIM + 1], (n, n))
                decay = jnp.exp(jnp.where(incls[p], gcb - gcb.T, NEG_BIG))
                aq = _dot_nt(kbq, kh[hh])
                lms.append(jnp.where(stricts[p], aq[:n] * decay, 0.0))
                intras.append(jnp.where(incls[p], aq[n:] * decay, 0.0))
                rhss.append(jnp.where(((lane2 % hp) // HEAD_DIM) == hh, rhs, 0.0).astype(BF16))
    uppers = [p == 1 for _ in range(n_pr) for p in range(2) for _ in range(2)]
    tmats = _unit_tri_inverses(lms, uppers, eye, same_block)
    for pr in range(n_pr):
        ls = slice(pr * hp, (pr + 1) * hp)
        for p in range(2):
            at = 4 * pr + 2 * p
            uw = sum(_dot(tmats[at + hh].astype(BF16), rhss[at + hh]) for hh in range(2))
            tiles = []
            for c in range(n // CHUNK):
                halves = []
                for hh in range(2):
                    lo = c * CHUNK // hp * hp
                    piece = intras[at + hh][c * CHUNK:(c + 1) * CHUNK, lo:lo + hp]
                    halves.append(piece if c % 2 == hh else pltpu.roll(piece, HEAD_DIM, 1))
                tiles.append(jnp.where(lane < HEAD_DIM, halves[0], halves[1]))
            intra2 = jnp.concatenate(tiles, axis=0)
            u_ref[p, 0, :, ls] = uw[:, :hp].astype(BF16)
            w_ref[p, 0, :, ls] = uw[:, hp:].astype(BF16)
            in_ref[p, 0, :, ls] = intra2.astype(BF16)


def _gdn_local(q, k, v, comp, blk):
    b, t, _ = q.shape
    hp = 2 * HEAD_DIM
    n_pair = C_W // hp
    ex = np.zeros((n_pair, LANE, 6 * hp), np.float32)
    for pr in range(n_pair):
        for p in range(2):
            for hh in range(2):
                src = p * C_HEADS + pr * 2 + hh
                for kind, base in enumerate((GC_LANE, GL_LANE, BETA_LANE)):
                    col = (2 * kind + p) * hp + hh * HEAD_DIM
                    ex[pr, base + src, col:col + HEAD_DIM] = 1.0
    gp = LOCAL_PAIRS
    tok = lambda bi, pg, i: (bi, i, pg)
    dirs = lambda bi, pg, i: (0, bi, i, pg)
    out = jax.ShapeDtypeStruct((2, b, t, C_W), BF16)
    return pl.pallas_call(
        _gdn_local_kernel,
        grid=(b, n_pair // gp, t // blk),
        in_specs=[pl.BlockSpec((1, blk, gp * hp), tok)] * 3
                 + [pl.BlockSpec((1, blk, LANE), lambda bi, pg, i: (bi, i, 0)),
                    pl.BlockSpec((gp, LANE, 6 * hp), lambda bi, pg, i: (pg, 0, 0))],
        out_specs=[pl.BlockSpec((2, 1, blk, gp * hp), dirs)] * 5,
        out_shape=[out] * 5,
        compiler_params=_cparams(("parallel", "parallel", "parallel")),
        name="gdn_local",
    )(q, k, v, comp, jnp.asarray(ex, dtype=BF16))


def _gdn_scan_kernel(*refs, ncb, nb):
    ins = (refs[0:6], refs[6:12])
    ex_ref = refs[12]
    outs = refs[13:15]
    s_ref = refs[15]
    j = pl.program_id(1)

    @pl.when(j == 0)
    def _():
        s_ref[...] = jnp.zeros_like(s_ref)

    gw = 4 * HEAD_DIM
    n_grp = C_W // gw
    lane_head = lax.broadcasted_iota(jnp.int32, (1, gw), 1) // HEAD_DIM
    rr = lax.broadcasted_iota(jnp.int32, (gw, gw), 0) // HEAD_DIM
    cc_ = lax.broadcasted_iota(jnp.int32, (gw, gw), 1) // HEAD_DIM
    blockdiag = rr == cc_
    chains = [(bi, p, g) for bi in range(nb) for p in range(2) for g in range(n_grp)]
    for cc in range(ncb):
        rows = [slice(cc * CHUNK, (cc + 1) * CHUNK), slice((ncb - 1 - cc) * CHUNK, (ncb - cc) * CHUNK)]
        sdec = {(bi, p): jnp.exp(_dot_exact_rhs(ins[p][5][bi, rows[p].start:rows[p].start + SUBLANE, :], ex_ref[...]))
                for bi in range(nb) for p in range(2)}
        lanes = [slice(g * gw, (g + 1) * gw) for g in range(n_grp)]
        ss = [s_ref[bi, p, g] for bi, p, g in chains]
        sbs = [s.astype(BF16) for s in ss]
        v_news = [ins[p][0][0, bi, rows[p], lanes[g]].astype(F32) - _dot(ins[p][1][0, bi, rows[p], lanes[g]], sb)
                  for (bi, p, g), sb in zip(chains, sbs)]
        vstacks = [jnp.concatenate([jnp.where(lane_head == h, v_new, 0.0).astype(BF16)
                                    for h in range(gw // HEAD_DIM)], axis=0) for v_new in v_news]
        for (bi, p, g), s, sb, v_new, vstack in zip(chains, ss, sbs, v_news, vstacks):
            o = _dot(ins[p][3][0, bi, rows[p], lanes[g]], sb) + _dot(ins[p][4][0, bi, rows[p], lanes[g]], vstack)
            upd = _dot_tn(ins[p][2][0, bi, rows[p], lanes[g]], v_new.astype(BF16))
            decay = sdec[bi, p][0:1, p * C_W + g * gw:p * C_W + (g + 1) * gw]
            s_ref[bi, p, g] = s * decay + jnp.where(blockdiag, upd, 0.0)
            outs[p][bi, rows[p], lanes[g]] = o


def _gdn_scan(u, w, kd, qd, intra, comp, n_lat, blk):
    _, b, t, _ = u.shape
    nl = n_lat // blk
    nx = (t - n_lat) // blk
    ncb = blk // CHUNK
    ex = np.zeros((LANE, 2 * C_W), np.float32)
    for p in range(2):
        for h in range(C_HEADS):
            ex[GL_LANE + p * C_HEADS + h, p * C_W + h * HEAD_DIM:p * C_W + (h + 1) * HEAD_DIM] = 1.0

    def blk0(j):
        return jnp.where(j < nx, nl + j, j - nx)

    def blk1(j):
        return jnp.where(j < nx, nl + nx - 1 - j, nl - 1 - (j - nx))

    nb = SCAN_BATCH if b % SCAN_BATCH == 0 else 1
    specs = []
    for p, bf in enumerate((blk0, blk1)):
        specs += [pl.BlockSpec((1, nb, blk, C_W), lambda bi, j, p=p, bf=bf: (p, bi, bf(j), 0))] * 5
        specs += [pl.BlockSpec((nb, blk, LANE), lambda bi, j, bf=bf: (bi, bf(j), 0))]
    specs += [pl.BlockSpec((LANE, 2 * C_W), lambda bi, j: (0, 0))]
    out = jax.ShapeDtypeStruct((b, t, C_W), F32)
    gw = 4 * HEAD_DIM
    return pl.pallas_call(
        functools.partial(_gdn_scan_kernel, ncb=ncb, nb=nb),
        grid=(b // nb, nl + nx),
        in_specs=specs,
        out_specs=[pl.BlockSpec((nb, blk, C_W), lambda bi, j: (bi, blk0(j), 0)),
                   pl.BlockSpec((nb, blk, C_W), lambda bi, j: (bi, blk1(j), 0))],
        out_shape=[out, out],
        scratch_shapes=[pltpu.VMEM((nb, 2, C_W // gw, gw, gw), F32)],
        compiler_params=_cparams(("parallel", "arbitrary")),
        name="gdn_scan",
    )(u, w, kd, qd, intra, comp, u, w, kd, qd, intra, comp, jnp.asarray(ex, dtype=BF16))


def _merge_kernel(x_ref, mod_ref, oa_ref, ob_ref, oc0_ref, oc1_ref, gate_ref, wmg_ref, wbr_ref, wout_ref, og_ref,
                  gs_ref, lng_ref, lnb_ref, wr_ref, o_ref, h_ref, aff_ref, *, alpha, n_exp):
    x = x_ref[0]
    d = x.shape[-1]
    mod = mod_ref[0, 0]
    h = (_ln(x) * (1.0 + mod[1:2]) + mod[0:1]).astype(BF16)
    gates = _sigmoid(_dot(h, wmg_ref[...]))
    oc = oc0_ref[0] + oc1_ref[0]
    ocn = oc * lax.rsqrt(_dot_exact_rhs(oc * oc, gs_ref[...], 2) + LN_EPS) * og_ref[...] * _silu(gate_ref[0])
    m = gates[:, :d] * _dot(oa_ref[0], wbr_ref[0])
    m = m + gates[:, d:2 * d] * _dot(ob_ref[0], wbr_ref[1])
    m = m + gates[:, 2 * d:] * _dot(ocn.astype(BF16), wbr_ref[2])
    mix = _dot(m.astype(BF16), wout_ref[...])
    y = alpha * x + mod[2:3] * mix
    x1 = _ln(y) * lng_ref[...] + lnb_ref[...]
    o_ref[0] = x1
    h2 = _ln(x1) * (1.0 + mod[4:5]) + mod[3:4]
    h_ref[0] = h2.astype(BF16)
    hp = _split_bf16(h2, 3)
    logits = sum(_dot(hp[i], wr_ref[j]) for i, j in ((2, 0), (1, 1), (0, 2), (1, 0), (0, 1), (0, 0)))
    lane = lax.broadcasted_iota(jnp.int32, logits.shape, 1)
    logits = jnp.where(lane < n_exp, logits, NEG_BIG)
    ex = jnp.exp(logits - jnp.max(logits, axis=-1, keepdims=True))
    aff_ref[0] = ex / jnp.sum(ex, axis=-1, keepdims=True)


def _merge(xa, mods, oa, ob, oc0, oc1, gate_c, wmg, wbr, wout, og, gs, lng, lnb, w_router, n_lat, tm, alpha):
    b, t, d = xa.shape
    nl = n_lat // tm
    n_exp = w_router.shape[-1]
    wr_pad = jnp.pad(w_router, ((0, 0), (0, LANE - n_exp)))
    pieces, rest = [], wr_pad
    for _ in range(3):
        piece = lax.reduce_precision(rest, exponent_bits=8, mantissa_bits=7)
        pieces.append(piece.astype(BF16))
        rest = rest - piece
    wr = jnp.stack(pieces)
    row = lambda bi, i: (bi, i, 0)
    const2 = lambda bi, i: (0, 0)
    return pl.pallas_call(
        functools.partial(_merge_kernel, alpha=alpha, n_exp=n_exp),
        grid=(b, t // tm),
        in_specs=[pl.BlockSpec((1, tm, d), row),
                  pl.BlockSpec((1, 1, N_MOD, d), lambda bi, i: (bi, (i >= nl).astype(jnp.int32), 0, 0)),
                  pl.BlockSpec((1, tm, BRANCH_W), row), pl.BlockSpec((1, tm, BRANCH_W), row),
                  pl.BlockSpec((1, tm, BRANCH_W), row), pl.BlockSpec((1, tm, BRANCH_W), row),
                  pl.BlockSpec((1, tm, BRANCH_W), row),
                  pl.BlockSpec(wmg.shape, const2),
                  pl.BlockSpec(wbr.shape, lambda bi, i: (0, 0, 0)),
                  pl.BlockSpec(wout.shape, const2),
                  pl.BlockSpec(og.shape, const2), pl.BlockSpec(gs.shape, const2),
                  pl.BlockSpec(lng.shape, const2), pl.BlockSpec(lnb.shape, const2),
                  pl.BlockSpec(wr.shape, lambda bi, i: (0, 0, 0))],
        out_specs=[pl.BlockSpec((1, tm, d), row), pl.BlockSpec((1, tm, d), row), pl.BlockSpec((1, tm, LANE), row)],
        out_shape=[jax.ShapeDtypeStruct((b, t, d), F32), jax.ShapeDtypeStruct((b, t, d), BF16),
                   jax.ShapeDtypeStruct((b, t, LANE), F32)],
        compiler_params=_cparams(("parallel", "parallel")),
        name="merge",
    )(xa, mods, oa, ob, oc0, oc1, gate_c, wmg, wbr, wout, og, gs, lng, lnb, wr)


def _ffn_kernel(x_ref, wg_ref, wu_ref, wd_ref, gate_ref, o_ref, acc_ref):
    f = pl.program_id(2)

    @pl.when(f == 0)
    def _():
        acc_ref[...] = jnp.zeros_like(acc_ref)

    x = x_ref[0]
    g = _dot(x, wg_ref[0, 0].astype(BF16))
    u = _dot(x, wu_ref[0, 0].astype(BF16))
    a = (_silu(g) * u).astype(BF16)
    acc_ref[...] += _dot(a, wd_ref[0, 0].astype(BF16))

    @pl.when(f == pl.num_programs(2) - 1)
    def _():
        o_ref[0] = (acc_ref[...] * gate_ref[0]).astype(BF16)


def _expert_ffn(xs, gate, w_gate_up, w_down, layer, tm, tf):
    e, r, d = xs.shape
    f = w_down.shape[2]
    nf = f // tf
    return pl.pallas_call(
        _ffn_kernel,
        grid=(e, r // tm, nf),
        in_specs=[pl.BlockSpec((1, tm, d), lambda ei, i, fi: (ei, i, 0)),
                  pl.BlockSpec((1, 1, d, tf), lambda ei, i, fi: (layer, ei, 0, fi)),
                  pl.BlockSpec((1, 1, d, tf), lambda ei, i, fi: (layer, ei, 0, nf + fi)),
                  pl.BlockSpec((1, 1, tf, d), lambda ei, i, fi: (layer, ei, fi, 0)),
                  pl.BlockSpec((1, tm, 1), lambda ei, i, fi: (ei, i, 0))],
        out_specs=pl.BlockSpec((1, tm, d), lambda ei, i, fi: (ei, i, 0)),
        out_shape=jax.ShapeDtypeStruct((e, r, d), BF16),
        scratch_shapes=[pltpu.VMEM((tm, d), F32)],
        compiler_params=_cparams(("parallel", "parallel", "arbitrary")),
        name="expert_ffn",
    )(xs, w_gate_up, w_gate_up, w_down, gate)


VALID, FIRST, LAST = 1, 2, 4


def _combine_plan(tok_of_pair, n_tok, tm, rwin):
    bsz, n_pair = tok_of_pair.shape
    nt = n_tok // tm
    n_win = n_pair // rwin
    n_item = nt + n_win
    order = jnp.argsort(tok_of_pair, axis=1)
    tok = jnp.take_along_axis(tok_of_pair, order, axis=1)
    bounds = jnp.arange(nt + 1, dtype=jnp.int32) * tm
    off = jnp.sum(tok[:, None, :] < bounds[None, :, None], axis=-1, dtype=jnp.int32)
    first = jnp.minimum(off[:, :-1] // rwin, n_win - 1)
    last = jnp.minimum(jnp.maximum(off[:, 1:] - 1, off[:, :-1]) // rwin, n_win - 1)
    n_items = last - first + 1
    cum = jnp.cumsum(n_items, axis=1)
    k = jnp.arange(n_item, dtype=jnp.int32)
    tile = jnp.sum(cum[:, None, :] <= k[None, :, None], axis=-1, dtype=jnp.int32)
    valid = tile < nt
    tile = jnp.minimum(tile, nt - 1)
    end = jnp.take_along_axis(cum, tile, axis=1)
    start = end - jnp.take_along_axis(n_items, tile, axis=1)
    win = jnp.where(valid, jnp.take_along_axis(first, tile, axis=1) + (k[None] - start),
                    jnp.take_along_axis(last, tile, axis=1))
    flags = (valid * VALID + (valid & (k[None] == start)) * FIRST + (valid & (k[None] == end - 1)) * LAST)
    return order, tok, tile, win.astype(jnp.int32), flags.astype(jnp.int32)


def _combine_kernel(tile_ref, win_ref, flag_ref, y_ref, tok_ref, x_ref, mod_ref, lng_ref, lnb_ref, o_ref, acc_ref, *,
                    alpha, tm):
    b = pl.program_id(0)
    k = pl.program_id(1)
    flags = flag_ref[b, k]

    @pl.when((flags & FIRST) != 0)
    def _():
        acc_ref[...] = jnp.zeros_like(acc_ref)

    @pl.when((flags & VALID) != 0)
    def _():
        tok0 = tile_ref[b, k] * tm
        rows = lax.broadcasted_iota(jnp.int32, (tm, tok_ref.shape[-1]), 0) + tok0
        pick = jnp.where(rows == tok_ref[0, 0], 1.0, 0.0).astype(BF16)
        acc_ref[...] += _dot(pick, y_ref[0])

    @pl.when((flags & LAST) != 0)
    def _():
        mod = mod_ref[0, 0]
        y = alpha * x_ref[0] + mod[5:6] * acc_ref[...]
        o_ref[0] = _ln(y) * lng_ref[...] + lnb_ref[...]


def _combine(y_sorted, tok_sorted, tile, win, flags, xa, mods, lng, lnb, n_lat, tm, rwin, alpha):
    b, t, d = xa.shape
    nl = n_lat // tm
    n_item = tile.shape[1]
    grid_spec = pltpu.PrefetchScalarGridSpec(
        num_scalar_prefetch=3,
        grid=(b, n_item),
        in_specs=[pl.BlockSpec((1, rwin, d), lambda bi, k, tl, wn, fl: (bi, wn[bi, k], 0)),
                  pl.BlockSpec((1, 1, 1, rwin), lambda bi, k, tl, wn, fl: (bi, wn[bi, k], 0, 0)),
                  pl.BlockSpec((1, tm, d), lambda bi, k, tl, wn, fl: (bi, tl[bi, k], 0)),
                  pl.BlockSpec((1, 1, N_MOD, d),
                               lambda bi, k, tl, wn, fl: (bi, (tl[bi, k] >= nl).astype(jnp.int32), 0, 0)),
                  pl.BlockSpec(lng.shape, lambda bi, k, tl, wn, fl: (0, 0)),
                  pl.BlockSpec(lnb.shape, lambda bi, k, tl, wn, fl: (0, 0))],
        out_specs=pl.BlockSpec((1, tm, d), lambda bi, k, tl, wn, fl: (bi, tl[bi, k], 0)),
        scratch_shapes=[pltpu.VMEM((tm, d), F32)],
    )
    return pl.pallas_call(
        functools.partial(_combine_kernel, alpha=alpha, tm=tm),
        grid_spec=grid_spec,
        out_shape=jax.ShapeDtypeStruct((b, t, d), F32),
        compiler_params=_cparams(("parallel", "arbitrary")),
        name="moe_combine",
    )(tile, win, flags, y_sorted, tok_sorted.reshape(b, -1, 1, rwin), xa, mods, lng, lnb)


def _moe(xa, h, aff, mods, n_exp, w_gate_up, w_down, lng, lnb, layer, n_lat, tm, alpha):
    b, t, d = xa.shape
    m_ctx = t - n_lat
    aff = jnp.swapaxes(aff[:, :, :n_exp], 1, 2)
    cap_l = CAPACITY * n_lat // n_exp
    cap_x = CAPACITY * m_ctx // n_exp
    gate_l, idx_l = lax.top_k(aff[:, :, :n_lat], cap_l)
    gate_x, idx_x = lax.top_k(aff[:, :, n_lat:], cap_x)
    idx = jnp.concatenate([idx_l, idx_x + n_lat], axis=-1)
    gate = jnp.concatenate([gate_l, gate_x], axis=-1)
    cap = cap_l + cap_x
    idx_e = jnp.swapaxes(idx, 0, 1)
    bidx = jnp.arange(b)[None, :, None]
    xs = h[bidx, idx_e].reshape(n_exp, b * cap, d)
    gate_e = jnp.swapaxes(gate, 0, 1).reshape(n_exp, b * cap, 1)
    rows = b * cap
    tmr = rows
    for cand in (1088, 1024, 512, 256, 128, 64, 32, 16):
        if rows % cand == 0:
            tmr = cand
            break
    tf = min(512, w_down.shape[2])
    y = _expert_ffn(xs, gate_e, w_gate_up, w_down, layer, tmr, tf)
    n_pair = n_exp * cap
    rwin = next(r for r in (512, 256, 128, 64, 32, 16, 8) if n_pair % r == 0)
    order, tok, tile, win, flags = _combine_plan(idx.reshape(b, n_pair), t, tm, rwin)
    y_sorted = y[order // cap, jnp.arange(b)[:, None] * cap + order % cap]
    return _combine(y_sorted, tok, tile, win, flags, xa, mods, lng, lnb, n_lat, tm, rwin, alpha)


def _rope_tables(n_lat, m_ctx):
    t = np.arange(n_lat)
    rows = (t // GRID_W).astype(np.float32)
    cols = (t % GRID_W).astype(np.float32)
    n_freq = HEAD_DIM // 4
    inv_freq = jnp.asarray(ROPE_THETA, F32) ** (-jnp.arange(n_freq, dtype=F32) / n_freq)
    ang_r = jnp.asarray(rows)[:, None] * inv_freq
    ang_c = jnp.asarray(cols)[:, None] * inv_freq
    cr, sr, cc, sc = jnp.cos(ang_r), jnp.sin(ang_r), jnp.cos(ang_c), jnp.sin(ang_c)
    cos = jnp.concatenate([cr, cr, cc, cc], axis=-1)
    sin = jnp.concatenate([-sr, sr, -sc, sc], axis=-1)
    cos = jnp.concatenate([cos, jnp.ones((m_ctx, HEAD_DIM), F32)], axis=0)
    sin = jnp.concatenate([sin, jnp.zeros((m_ctx, HEAD_DIM), F32)], axis=0)
    return jnp.tile(cos, (1, A_HEADS)), jnp.tile(sin, (1, A_HEADS))


def kernel(x, c, ctx, c_ctx, w_mod, b_mod, w_in, qk_gain, rpb, conv_w, a_log, dt_bias, o_gain, w_branch, w_out,
           ln1_g, ln1_b, w_router, w_gate_up, w_down, ln2_g, ln2_b):
    b, n_lat, d = x.shape
    m_ctx = ctx.shape[1]
    depth = w_mod.shape[0]
    alpha = (2 * depth) ** 0.25
    tm = math.gcd(256, math.gcd(n_lat, m_ctx))
    blk = tm

    xa = jnp.concatenate([x, ctx], axis=1)
    n_rows = -(-(b + 1) // SUBLANE) * SUBLANE
    cc = jnp.zeros((n_rows, d), F32).at[:b].set(c).at[b].set(c_ctx)
    mod_all = _modulation(cc, w_mod, b_mod).reshape(depth, n_rows, N_MOD, d)
    cos, sin = _rope_tables(n_lat, m_ctx)
    gs_mean = _group_sum_matrix(A_Q_W, 1.0 / HEAD_DIM)
    gs_sum = _group_sum_matrix(C_W, 1.0)
    na_geoms, na_table = _na_classes(n_lat, n_lat + m_ctx)

    offs = np.cumsum((0, A_Q_W, A_KV_W, A_KV_W, B_W, B_W, B_W, 3 * C_W, C_W, 2 * C_HEADS, 2 * C_HEADS, N_BRANCH * d))
    for layer in range(depth):
        ml = mod_all[layer]
        mods = jnp.stack([ml[:b], jnp.broadcast_to(ml[b][None], (b, N_MOD, d))], axis=1)
        wl = w_in[layer]
        wa = wl[:, offs[0]:offs[3]].astype(BF16)
        wb = wl[:, offs[3]:offs[6]].astype(BF16)
        wc = wl[:, offs[6]:offs[8]].astype(BF16)
        wab = jnp.pad(wl[:, offs[8]:offs[10]], ((0, 0), (0, LANE - 4 * C_HEADS))).astype(BF16)
        wmg = wl[:, offs[10]:offs[11]].astype(BF16)
        gq = jnp.tile(qk_gain[layer, 0], A_HEADS)[None]
        gk = jnp.tile(qk_gain[layer, 1], A_KV_HEADS)[None]
        qa, ka, va, qb, kb, vb, qkvc, gate_c, ab = _in_projection(
            xa, mods, wa, wb, wc, wab, gq, gk, cos, sin, gs_mean, n_lat, tm)
        oa = _gqa(qa, ka, va, n_lat, min(GQA_TQ, tm), 2 if (n_lat + m_ctx) % (2 * LANE) == 0 else 1)
        ob = _neighborhood(qb, kb, vb, _na_bias_slabs(rpb[layer], na_geoms), na_table, n_lat)
        qc, kc, vc, comp = _gdn_prep(qkvc, ab, conv_w[layer], a_log[layer], dt_bias[layer], gs_sum, n_lat, tm)
        u, w, kd, qd, intra = _gdn_local(qc, kc, vc, comp, blk)
        oc0, oc1 = _gdn_scan(u, w, kd, qd, intra, comp, n_lat, blk)
        og = jnp.tile(o_gain[layer], C_HEADS)[None]
        xa, h, aff = _merge(xa, mods, oa, ob, oc0, oc1, gate_c, wmg, w_branch[layer].astype(BF16),
                            w_out[layer].astype(BF16), og, gs_mean, ln1_g[layer][None], ln1_b[layer][None],
                            w_router[layer], n_lat, tm, alpha)
        xa = _moe(xa, h, aff, mods, w_router.shape[-1], w_gate_up, w_down, ln2_g[layer][None], ln2_b[layer][None],
                  layer, n_lat, tm, alpha)
    return xa[:, :n_lat]
```

```python
import functools
import math

import numpy as np
import jax
import jax.numpy as jnp
from jax import lax
from jax.experimental import pallas as pl
from jax.experimental.pallas import tpu as pltpu

F32 = jnp.float32
BF16 = jnp.bfloat16
HIGHEST = lax.Precision.HIGHEST

GRID_W = 64
HEAD_DIM = 64
A_HEADS = 8
A_KV_HEADS = 2
B_HEADS = 8
C_HEADS = 8
NA_ROWS = 8
NA_COLS = 16
CONV_K = 4
CHUNK = 64
N_BRANCH = 3
BRANCH_W = 512
CAPACITY = 2
N_MOD = 6
LN_EPS = 1e-6
ROPE_THETA = 10000.0
NEG_BIG = -1e30
LOG2E = 1.4426950408889634

A_Q_W = A_HEADS * HEAD_DIM
A_KV_W = A_KV_HEADS * HEAD_DIM
B_W = B_HEADS * HEAD_DIM
C_W = C_HEADS * HEAD_DIM
LANE = 128
SUBLANE = 8
V7X_VMEM_BYTES = 64 * 1024 * 1024
VMEM_LIMIT = V7X_VMEM_BYTES * 7 // 8
GQA_TQ = 128
LOCAL_PAIRS = 2
SCAN_BATCH = 2
VT_ROWS = HEAD_DIM + 16
NA_WIN = NA_ROWS + 2


def _cparams(sem):
    return pltpu.CompilerParams(dimension_semantics=sem, vmem_limit_bytes=VMEM_LIMIT)


def _dot(a, b):
    return jnp.dot(a, b, preferred_element_type=F32)


def _dot_hi(a, b):
    return jnp.dot(a, b, precision=HIGHEST, preferred_element_type=F32)


def _dot_nt(a, b):
    return lax.dot_general(a, b, (((1,), (1,)), ((), ())), preferred_element_type=F32)


def _dot_tn(a, b):
    return lax.dot_general(a, b, (((0,), (0,)), ((), ())), preferred_element_type=F32)


def _split_bf16(x, pieces):
    out = []
    for _ in range(pieces - 1):
        h = x.astype(BF16)
        out.append(h)
        x = x - h.astype(F32)
    out.append(x.astype(BF16))
    return out


def _dot_exact_rhs(a, b, pieces=3):
    return sum(_dot(p, b) for p in _split_bf16(a, pieces))


def _dot_exact_lhs(a, b, pieces=3):
    return sum(_dot(a, p) for p in _split_bf16(b, pieces))


def _ln(x):
    mu = jnp.mean(x, axis=-1, keepdims=True)
    xc = x - mu
    var = jnp.mean(xc * xc, axis=-1, keepdims=True)
    return xc * lax.rsqrt(var + LN_EPS)


def _sigmoid(x):
    return 1.0 / (1.0 + jnp.exp(-x))


def _silu(x):
    return x * _sigmoid(x)


def _group_sum_matrix(width, scale):
    g = np.arange(width) // HEAD_DIM
    return jnp.asarray((g[:, None] == g[None, :]).astype(np.float32) * scale, dtype=BF16)


def _mod_kernel(c_ref, w_ref, b_ref, o_ref):
    c = c_ref[...]
    o_ref[0] = _dot_hi(_silu(c), w_ref[0]) + b_ref[0]


def _modulation(cc, w_mod, b_mod):
    depth, d, dm = w_mod.shape
    r = cc.shape[0]
    tn = min(dm, 1536)
    return pl.pallas_call(
        _mod_kernel,
        grid=(depth, dm // tn),
        in_specs=[pl.BlockSpec((r, d), lambda l, j: (0, 0)),
                  pl.BlockSpec((1, d, tn), lambda l, j: (l, 0, j)),
                  pl.BlockSpec((1, 1, tn), lambda l, j: (l, 0, j))],
        out_specs=pl.BlockSpec((1, r, tn), lambda l, j: (l, 0, j)),
        out_shape=jax.ShapeDtypeStruct((depth, r, dm), F32),
        compiler_params=_cparams(("parallel", "parallel")),
        name="modulation",
    )(cc, w_mod, b_mod.reshape(depth, 1, dm))


def _rope_swap(y):
    w = y.shape[-1]
    lane = lax.broadcasted_iota(jnp.int32, y.shape, 1)
    up = pltpu.roll(y, w - 16, 1)
    down = pltpu.roll(y, 16, 1)
    return jnp.where((lane % 32) < 16, up, down)


def _inproj_kernel(x_ref, mod_ref, wa_ref, wb_ref, wc_ref, wab_ref, gq_ref, gk_ref, cos_ref, sin_ref,
                   gs_ref, qa_ref, ka_ref, va_ref, qb_ref, kb_ref, vb_ref, qkvc_ref, gatec_ref, ab_ref):
    x = x_ref[0]
    mod = mod_ref[0, 0]
    h = (_ln(x) * (1.0 + mod[1:2]) + mod[0:1]).astype(BF16)
    scale = HEAD_DIM ** -0.5 * LOG2E

    za = _dot(h, wa_ref[...])
    q = za[:, :A_Q_W]
    k = za[:, A_Q_W:A_Q_W + A_KV_W]
    v = za[:, A_Q_W + A_KV_W:]
    cos = cos_ref[...]
    sin = sin_ref[...]
    gs = gs_ref[...]
    qn = q * lax.rsqrt(_dot_exact_rhs(q * q, gs, 2) + LN_EPS) * gq_ref[...]
    qn = qn * cos + _rope_swap(qn) * sin
    qa_ref[0] = (qn * scale).astype(BF16)
    kn = k * lax.rsqrt(_dot_exact_rhs(k * k, gs[:A_KV_W, :A_KV_W], 2) + LN_EPS) * gk_ref[...]
    kn = kn * cos[:, :A_KV_W] + _rope_swap(kn) * sin[:, :A_KV_W]
    ones = jnp.ones((VT_ROWS - HEAD_DIM, x.shape[0]), F32)
    vt = v.T
    for j in range(A_KV_HEADS):
        ka_ref[0, j] = kn[:, j * HEAD_DIM:(j + 1) * HEAD_DIM].astype(BF16)
        va_ref[0, j] = jnp.concatenate([vt[j * HEAD_DIM:(j + 1) * HEAD_DIM], ones], axis=0).astype(BF16)

    zb = _dot(h, wb_ref[...])
    qb_ref[0] = (zb[:, :B_W] * scale).astype(BF16)
    kbv = zb[:, B_W:2 * B_W]
    vt = zb[:, 2 * B_W:].T
    for j in range(B_HEADS):
        kb_ref[0, j] = kbv[:, j * HEAD_DIM:(j + 1) * HEAD_DIM].astype(BF16)
        vb_ref[0, j] = jnp.concatenate([vt[j * HEAD_DIM:(j + 1) * HEAD_DIM], ones], axis=0).astype(BF16)

    zc = _dot(h, wc_ref[...])
    qkvc_ref[0] = zc[:, :3 * C_W]
    gatec_ref[0] = zc[:, 3 * C_W:]
    ab_ref[0] = _dot(h, wab_ref[...])


def _in_projection(xa, mods, wa, wb, wc, wab, gq, gk, cos, sin, gs, n_lat, tm):
    b, t, d = xa.shape
    nl = n_lat // tm
    row = lambda bi, i: (bi, i, 0)
    const = lambda bi, i: (0, 0)
    outs = [
        (jax.ShapeDtypeStruct((b, t, A_Q_W), BF16), pl.BlockSpec((1, tm, A_Q_W), row)),
        (jax.ShapeDtypeStruct((b, A_KV_HEADS, t, HEAD_DIM), BF16),
         pl.BlockSpec((1, A_KV_HEADS, tm, HEAD_DIM), lambda bi, i: (bi, 0, i, 0))),
        (jax.ShapeDtypeStruct((b, A_KV_HEADS, VT_ROWS, t), BF16),
         pl.BlockSpec((1, A_KV_HEADS, VT_ROWS, tm), lambda bi, i: (bi, 0, 0, i))),
        (jax.ShapeDtypeStruct((b, t, B_W), BF16), pl.BlockSpec((1, tm, B_W), row)),
        (jax.ShapeDtypeStruct((b, B_HEADS, t, HEAD_DIM), BF16),
         pl.BlockSpec((1, B_HEADS, tm, HEAD_DIM), lambda bi, i: (bi, 0, i, 0))),
        (jax.ShapeDtypeStruct((b, B_HEADS, VT_ROWS, t), BF16),
         pl.BlockSpec((1, B_HEADS, VT_ROWS, tm), lambda bi, i: (bi, 0, 0, i))),
        (jax.ShapeDtypeStruct((b, t, 3 * C_W), F32), pl.BlockSpec((1, tm, 3 * C_W), row)),
        (jax.ShapeDtypeStruct((b, t, C_W), F32), pl.BlockSpec((1, tm, C_W), row)),
        (jax.ShapeDtypeStruct((b, t, LANE), F32), pl.BlockSpec((1, tm, LANE), row)),
    ]
    return pl.pallas_call(
        _inproj_kernel,
        grid=(b, t // tm),
        in_specs=[pl.BlockSpec((1, tm, d), row),
                  pl.BlockSpec((1, 1, N_MOD, d), lambda bi, i: (bi, (i >= nl).astype(jnp.int32), 0, 0)),
                  pl.BlockSpec(wa.shape, const), pl.BlockSpec(wb.shape, const),
                  pl.BlockSpec(wc.shape, const), pl.BlockSpec(wab.shape, const),
                  pl.BlockSpec(gq.shape, const), pl.BlockSpec(gk.shape, const),
                  pl.BlockSpec((tm, A_Q_W), lambda bi, i: (i, 0)),
                  pl.BlockSpec((tm, A_Q_W), lambda bi, i: (i, 0)),
                  pl.BlockSpec(gs.shape, const)],
        out_specs=[o[1] for o in outs],
        out_shape=[o[0] for o in outs],
        compiler_params=_cparams(("parallel", "parallel")),
        name="in_projection",
    )(xa, mods, wa, wb, wc, wab, gq, gk, cos, sin, gs)


def _gqa_kernel(q_ref, k_ref, v_ref, o_ref, *, tq, ck, n_lat, n_tot):
    i = pl.program_id(1)
    g = A_HEADS // A_KV_HEADS

    def attend(key_lo, key_hi, n_split):
        qs = [jnp.concatenate([q_ref[0, :, (kh * g + j) * HEAD_DIM:(kh * g + j + 1) * HEAD_DIM] for j in range(g)],
                              axis=0) for kh in range(A_KV_HEADS)]
        step = (key_hi - key_lo) // n_split
        units = [(kh, key_lo + r * step, key_lo + (r + 1) * step) for r in range(n_split) for kh in range(A_KV_HEADS)]
        ss, ms, ps, oes = {}, {}, {}, {}
        for n in range(len(units) + 2):
            if n < len(units):
                kh, lo, hi = units[n]
                ss[n] = _dot_nt(k_ref[0, kh, lo:hi, :], qs[kh])
            if 0 <= n - 1 < len(units):
                ms[n - 1] = jnp.max(ss[n - 1], axis=0, keepdims=True).astype(BF16).astype(F32)
                ps[n - 1] = jnp.exp2(ss.pop(n - 1).astype(BF16) - ms[n - 1].astype(BF16))
            if 0 <= n - 2 < len(units):
                kh, lo, hi = units[n - 2]
                oes[n - 2] = _dot(v_ref[0, kh, :, lo:hi], ps.pop(n - 2))
        for kh in range(A_KV_HEADS):
            mine = [n for n, u in enumerate(units) if u[0] == kh]
            m = functools.reduce(jnp.maximum, [ms[n] for n in mine])
            oe = sum(oes[n] * jnp.exp2(ms[n] - m) for n in mine)
            o = (oe[:HEAD_DIM] / oe[HEAD_DIM:HEAD_DIM + 1]).T.astype(BF16)
            for j in range(g):
                o_ref[0, :, (kh * g + j) * HEAD_DIM:(kh * g + j + 1) * HEAD_DIM] = o[j * tq:(j + 1) * tq]

    @pl.when(i < n_lat // tq)
    def _():
        attend(0, n_tot, ck)

    @pl.when(i >= n_lat // tq)
    def _():
        attend(n_lat, n_tot, 1)


def _gqa(qa, ka, va, n_lat, tq, ck):
    b, t, _ = qa.shape
    return pl.pallas_call(
        functools.partial(_gqa_kernel, tq=tq, ck=ck, n_lat=n_lat, n_tot=t),
        grid=(b, t // tq),
        in_specs=[pl.BlockSpec((1, tq, A_Q_W), lambda bi, i: (bi, i, 0)),
                  pl.BlockSpec((1, A_KV_HEADS, t, HEAD_DIM), lambda bi, i: (bi, 0, 0, 0)),
                  pl.BlockSpec((1, A_KV_HEADS, VT_ROWS, t), lambda bi, i: (bi, 0, 0, 0))],
        out_specs=pl.BlockSpec((1, tq, A_Q_W), lambda bi, i: (bi, i, 0)),
        out_shape=jax.ShapeDtypeStruct((b, t, A_Q_W), BF16),
        compiler_params=_cparams(("parallel", "parallel")),
        name="gqa",
    )(qa, ka, va)


def _na_window_start(r0, rows):
    return np.clip(r0 - NA_ROWS // 2, 0, rows - NA_ROWS - 1) // 2 * 2


def _na_classes(n_lat, n_tot):
    rows = n_lat // GRID_W
    assert rows >= NA_WIN and rows % 2 == 0
    geoms, table = [], []
    for r0 in range(0, rows, 2):
        geom = (int(_na_window_start(r0, rows)) - r0,) + tuple(
            int(np.clip(r0 + qr - NA_ROWS // 2, 0, rows - NA_ROWS)) - r0 for qr in range(2))
        if geom not in geoms:
            geoms.append(geom)
        table.append(geoms.index(geom))
    table += [len(geoms)] * ((n_tot - n_lat) // (2 * GRID_W))
    return geoms, np.asarray(table, np.int32)


def _na_bias_slabs(rpb, geoms):
    n_head = rpb.shape[0]
    cj = np.arange(GRID_W)[:, None]
    c = np.arange(GRID_W)[None, :]
    col_start = np.clip(c - NA_COLS // 2, 0, GRID_W - NA_COLS)
    col_ok = (cj >= col_start) & (cj < col_start + NA_COLS)
    col_off = np.clip(cj - c + (NA_COLS - 1), 0, 2 * NA_COLS - 2)
    spread = np.zeros((2 * NA_COLS - 1, GRID_W * GRID_W), np.float32)
    spread[col_off.reshape(-1), np.arange(GRID_W * GRID_W)] = 1.0
    kr = np.arange(NA_WIN)[:, None]
    qr = np.arange(2)[None, :]
    pick = np.zeros((len(geoms), NA_WIN, 2, 2 * NA_ROWS - 1), np.float32)
    ok = np.zeros((len(geoms), 1, NA_WIN, 2, GRID_W, GRID_W), bool)
    for n, (a, b0, b1) in enumerate(geoms):
        first = np.where(qr == 0, b0, b1)
        row_ok = (a + kr >= first) & (a + kr < first + NA_ROWS)
        row_off = np.clip(a + kr - qr + (NA_ROWS - 1), 0, 2 * NA_ROWS - 2)
        pick[n, kr, qr, row_off] = 1.0
        ok[n, 0] = row_ok[:, :, None, None] & col_ok[None, None]
    rows = jnp.einsum("nkqo,hoc->nhkqc", jnp.asarray(pick), rpb * LOG2E, precision=HIGHEST)
    full = jnp.dot(rows, jnp.asarray(spread), precision=HIGHEST)
    full = full.reshape(len(geoms), n_head, NA_WIN, 2, GRID_W, GRID_W)
    masked = jnp.where(jnp.asarray(ok), full, NEG_BIG)
    slabs = jnp.concatenate([masked[:, :, :, 0], masked[:, :, :, 1]], axis=-1)
    slabs = slabs.reshape(len(geoms), n_head, NA_WIN * GRID_W, 2 * GRID_W)
    return jnp.concatenate([slabs, jnp.full_like(slabs[:1], NEG_BIG)], axis=0).astype(F32)


def _na_kernel(cls_ref, q_ref, k_ref, vt_ref, bias_ref, o_ref, *, n_lat, n_tot):
    del cls_ref
    rows = n_lat // GRID_W
    r0 = 2 * pl.program_id(1)
    start = jnp.clip(r0 - NA_ROWS // 2, 0, rows - NA_ROWS - 1) // 2 * 2
    nb = pl.ds(pl.multiple_of(start * GRID_W, 2 * GRID_W), NA_WIN * GRID_W)
    hs = [slice(h * HEAD_DIM, (h + 1) * HEAD_DIM) for h in range(B_HEADS)]
    qs = [q_ref[0, :, hs[h]] for h in range(B_HEADS)]
    s_nb = [_dot_nt(k_ref[0, h, nb, :], qs[h]) + bias_ref[0, h] for h in range(B_HEADS)]
    s_cx = [_dot_nt(k_ref[0, h, n_lat:n_tot, :], qs[h]) for h in range(B_HEADS)]
    ms = [jnp.maximum(jnp.max(a, axis=0, keepdims=True), jnp.max(c, axis=0, keepdims=True))
          for a, c in zip(s_nb, s_cx)]
    p_nb = [jnp.exp2((a - m).astype(BF16)) for a, m in zip(s_nb, ms)]
    p_cx = [jnp.exp2((c - m).astype(BF16)) for c, m in zip(s_cx, ms)]
    for h in range(B_HEADS):
        oe = _dot(vt_ref[0, h, :, nb], p_nb[h]) + _dot(vt_ref[0, h, :, n_lat:n_tot], p_cx[h])
        o_ref[0, :, hs[h]] = (oe[:HEAD_DIM] / oe[HEAD_DIM:HEAD_DIM + 1]).T.astype(BF16)


def _neighborhood(qb, kb, vbt, slabs, table, n_lat):
    b, t, _ = qb.shape
    tq = 2 * GRID_W
    grid_spec = pltpu.PrefetchScalarGridSpec(
        num_scalar_prefetch=1,
        grid=(b, t // tq),
        in_specs=[pl.BlockSpec((1, tq, B_W), lambda bi, pi, cls: (bi, pi, 0)),
                  pl.BlockSpec((1, B_HEADS, t, HEAD_DIM), lambda bi, pi, cls: (bi, 0, 0, 0)),
                  pl.BlockSpec((1, B_HEADS, VT_ROWS, t), lambda bi, pi, cls: (bi, 0, 0, 0)),
                  pl.BlockSpec((1,) + slabs.shape[1:], lambda bi, pi, cls: (cls[pi], 0, 0, 0))],
        out_specs=pl.BlockSpec((1, tq, B_W), lambda bi, pi, cls: (bi, pi, 0)),
    )
    return pl.pallas_call(
        functools.partial(_na_kernel, n_lat=n_lat, n_tot=t),
        grid_spec=grid_spec,
        out_shape=jax.ShapeDtypeStruct((b, t, B_W), BF16),
        compiler_params=_cparams(("parallel", "parallel")),
        name="neighborhood",
    )(jnp.asarray(table), qb, kb, vbt, slabs)


GC_LANE, GL_LANE, BETA_LANE = 0, 2 * C_HEADS, 4 * C_HEADS


def _gdn_prep_kernel(x_ref, prev_ref, next_ref, ab_ref, cw_ref, alog_ref, dtb_ref, gs_ref,
                     q_ref, k_ref, v_ref, comp_ref, *, tm, n_lat, n_tot):
    i = pl.program_id(1)
    nl = n_lat // tm
    nt = n_tot // tm
    first = jnp.logical_or(i == 0, i == nl)
    last = jnp.logical_or(i == nl - 1, i == nt - 1)
    x = x_ref[0]
    pv = jnp.where(first, 0.0, prev_ref[0])
    nx = jnp.where(last, 0.0, next_ref[0])
    row = lax.broadcasted_iota(jnp.int32, (tm, 1), 0)
    prev1 = pv[SUBLANE - 1:SUBLANE]
    prev2 = pv[SUBLANE - 2:SUBLANE - 1]
    xm1 = jnp.where(row == 0, prev1, pltpu.roll(x, 1, 0))
    xm2 = jnp.where(row == 0, prev2, jnp.where(row == 1, prev1, pltpu.roll(x, 2, 0)))
    xp1 = jnp.where(row == tm - 1, nx[0:1], pltpu.roll(x, tm - 1, 0))
    cw = cw_ref[...]
    y = _silu(xm2 * cw[0:1] + xm1 * cw[1:2] + x * cw[2:3] + xp1 * cw[3:4])
    q = y[:, :C_W]
    k = y[:, C_W:2 * C_W]
    gs = gs_ref[...]
    q_ref[0] = q * lax.rsqrt(_dot_exact_rhs(q * q, gs, 2) + LN_EPS) * (HEAD_DIM ** -0.5)
    k_ref[0] = k * lax.rsqrt(_dot_exact_rhs(k * k, gs, 2) + LN_EPS)
    v_ref[0] = y[:, 2 * C_W:]
    ab = ab_ref[0]
    z = ab + dtb_ref[...]
    softplus = jnp.maximum(z, 0.0) + jnp.log(1.0 + jnp.exp(-jnp.abs(z)))
    lane = lax.broadcasted_iota(jnp.int32, ab.shape, 1)
    gb = jnp.where(lane < 2 * C_HEADS, -jnp.exp(alog_ref[...]) * softplus, _sigmoid(ab))
    ii = lax.broadcasted_iota(jnp.int32, (tm, tm), 0)
    jj = lax.broadcasted_iota(jnp.int32, (tm, tm), 1)
    same = (ii // CHUNK) == (jj // CHUNK)
    gb3 = _split_bf16(gb, 3)

    def masked_sum(mask):
        mb = jnp.where(mask, 1.0, 0.0).astype(BF16)
        return sum(_dot(mb, piece) for piece in gb3)

    prefix = masked_sum(jnp.logical_and(same, jj <= ii))
    suffix = masked_sum(jnp.logical_and(same, jj >= ii))
    total = masked_sum(same)
    comp = jnp.where(lane < C_HEADS, prefix,
                     jnp.where(lane < GL_LANE, suffix,
                               jnp.where(lane < BETA_LANE, pltpu.roll(total, GL_LANE, 1),
                                         jnp.where(lane < BETA_LANE + 2 * C_HEADS, pltpu.roll(gb, 2 * C_HEADS, 1),
                                                   0.0))))
    comp_ref[0] = comp


def _gdn_prep(qkvc, ab, conv_w, a_log, dt_bias, gs, n_lat, tm):
    b, t, _ = qkvc.shape
    c3 = 3 * C_W
    alog_row = jnp.zeros((1, LANE), F32).at[0, :2 * C_HEADS].set(a_log.reshape(-1))
    dtb_row = jnp.zeros((1, LANE), F32).at[0, :2 * C_HEADS].set(dt_bias.reshape(-1))
    row = lambda bi, i: (bi, i, 0)
    const2 = lambda bi, i: (0, 0)
    hb = tm // SUBLANE
    last8 = t // SUBLANE - 1
    tok = jax.ShapeDtypeStruct((b, t, C_W), F32)
    return pl.pallas_call(
        functools.partial(_gdn_prep_kernel, tm=tm, n_lat=n_lat, n_tot=t),
        grid=(b, t // tm),
        in_specs=[pl.BlockSpec((1, tm, c3), row),
                  pl.BlockSpec((1, SUBLANE, c3), lambda bi, i: (bi, jnp.maximum(i * hb - 1, 0), 0)),
                  pl.BlockSpec((1, SUBLANE, c3), lambda bi, i: (bi, jnp.minimum((i + 1) * hb, last8), 0)),
                  pl.BlockSpec((1, tm, LANE), row),
                  pl.BlockSpec((CONV_K, c3), const2),
                  pl.BlockSpec((1, LANE), const2), pl.BlockSpec((1, LANE), const2),
                  pl.BlockSpec(gs.shape, const2)],
        out_specs=[pl.BlockSpec((1, tm, C_W), row)] * 3 + [pl.BlockSpec((1, tm, LANE), row)],
        out_shape=[tok, tok, tok, jax.ShapeDtypeStruct((b, t, LANE), F32)],
        compiler_params=_cparams(("parallel", "parallel")),
        name="gdn_prep",
    )(qkvc, qkvc, qkvc, ab, conv_w, alog_row, dtb_row, gs)


def _half_block_rows(a, h, upper):
    return jnp.concatenate([a[s:s + h] for s in range(0 if upper else h, a.shape[0], 2 * h)], axis=0)


def _with_half_block_rows(full, part, h, upper):
    pieces = []
    for blk in range(full.shape[0] // h):
        if (blk % 2 == 0) == upper:
            pieces.append(part[(blk // 2) * h:(blk // 2 + 1) * h])
        else:
            pieces.append(full[blk * h:(blk + 1) * h])
    return jnp.concatenate(pieces, axis=0)


def _unit_tri_inverses(lms, uppers, eye, same_block):
    def mm(a, b_):
        return _dot(a.astype(BF16), b_.astype(BF16))

    lds = [jnp.where(same_block[SUBLANE], lm, 0.0) for lm in lms]
    xs = [eye - ld for ld in lds]
    lds = [ld.astype(BF16) for ld in lds]
    pws = [_dot(ld, ld).astype(BF16) for ld in lds]
    xs = [x + mm(x, pw) for x, pw in zip(xs, pws)]
    pws = [_dot(pw, pw).astype(BF16) for pw in pws]
    xs = [x + mm(x, pw) for x, pw in zip(xs, pws)]
    h = SUBLANE
    while h < CHUNK:
        level = jnp.logical_not(same_block[h]) if 2 * h == CHUNK else jnp.logical_and(
            same_block[2 * h], jnp.logical_not(same_block[h]))
        xbs = [x.astype(BF16) for x in xs]
        ls = [_half_block_rows(jnp.where(level, lm, 0.0), h, up).astype(BF16) for lm, up in zip(lms, uppers)]
        ys = [_dot(l, xb) for l, xb in zip(ls, xbs)]
        zs = [_with_half_block_rows(jnp.zeros_like(x), y, h, up).astype(BF16) for x, y, up in zip(xs, ys, uppers)]
        xrs = [_half_block_rows(x, h, up) for x, up in zip(xs, uppers)]
        upd = [_dot(xr.astype(BF16), z) for xr, z in zip(xrs, zs)]
        xs = [_with_half_block_rows(x, xr - u, h, up) for x, xr, u, up in zip(xs, xrs, upd, uppers)]
        h *= 2
    return xs


def _gdn_local_kernel(q_ref, k_ref, v_ref, comp_ref, ex_ref, u_ref, w_ref, kd_ref, qd_ref, in_ref):
    n = q_ref.shape[1]
    hp = 2 * HEAD_DIM
    n_pr = q_ref.shape[2] // hp
    ii = lax.broadcasted_iota(jnp.int32, (n, n), 0)
    jj = lax.broadcasted_iota(jnp.int32, (n, n), 1)
    same = (ii // CHUNK) == (jj // CHUNK)
    eye = (ii == jj).astype(F32)
    same_block = {h: (ii // h) == (jj // h) for h in (8, 16, 32)}
    lane = lax.broadcasted_iota(jnp.int32, (1, hp), 1)
    lane2 = lax.broadcasted_iota(jnp.int32, (1, 2 * hp), 1)
    incls = [jnp.logical_and(same, ii >= jj), jnp.logical_and(same, ii <= jj)]
    stricts = [jnp.logical_and(same, ii > jj), jnp.logical_and(same, ii < jj)]
    lms, intras, rhss = [], [], []
    for pr in range(n_pr):
        ls = slice(pr * hp, (pr + 1) * hp)
        q = q_ref[0, :, ls]
        k = k_ref[0, :, ls]
        v = v_ref[0, :, ls]
        e = _dot_exact_rhs(comp_ref[0], ex_ref[pr])
        kh = [jnp.where((lane // HEAD_DIM) == hh, k, 0.0).astype(BF16) for hh in range(2)]
        for p in range(2):
            gc = e[:, p * hp:(p + 1) * hp]
            gl = e[:, (2 + p) * hp:(3 + p) * hp]
            beta = e[:, (4 + p) * hp:(5 + p) * hp]
            eg = jnp.exp(gc)
            kb = k * beta
            kd_ref[p, 0, :, ls] = (k * jnp.exp(gl - gc)).astype(BF16)
            qd_ref[p, 0, :, ls] = (q * eg).astype(BF16)
            rhs = jnp.concatenate([v * beta, kb * eg], axis=1)
            kbq = jnp.concatenate([kb, q], axis=0).astype(BF16)
            for hh in range(2):
                gcb = jnp.broadcast_to(gc[:, hh * HEAD_DIM:hh * HEAD_DIM + 1], (n, n))
                decay = jnp.exp(jnp.where(incls[p], gcb - gcb.T, NEG_BIG))
                aq = _dot_nt(kbq, kh[hh])
                lms.append(jnp.where(stricts[p], aq[:n] * decay, 0.0))
                intras.append(jnp.where(incls[p], aq[n:] * decay, 0.0))
                rhss.append(jnp.where(((lane2 % hp) // HEAD_DIM) == hh, rhs, 0.0).astype(BF16))
    uppers = [p == 1 for _ in range(n_pr) for p in range(2) for _ in range(2)]
    tmats = _unit_tri_inverses(lms, uppers, eye, same_block)
    for pr in range(n_pr):
        ls = slice(pr * hp, (pr + 1) * hp)
        for p in range(2):
            at = 4 * pr + 2 * p
            uw = sum(_dot(tmats[at + hh].astype(BF16), rhss[at + hh]) for hh in range(2))
            tiles = []
            for c in range(n // CHUNK):
                halves = []
                for hh in range(2):
                    lo = c * CHUNK // hp * hp
                    piece = intras[at + hh][c * CHUNK:(c + 1) * CHUNK, lo:lo + hp]
                    halves.append(piece if c % 2 == hh else pltpu.roll(piece, HEAD_DIM, 1))
                tiles.append(jnp.where(lane < HEAD_DIM, halves[0], halves[1]))
            intra2 = jnp.concatenate(tiles, axis=0)
            u_ref[p, 0, :, ls] = uw[:, :hp].astype(BF16)
            w_ref[p, 0, :, ls] = uw[:, hp:].astype(BF16)
            in_ref[p, 0, :, ls] = intra2.astype(BF16)


def _gdn_local(q, k, v, comp, blk):
    b, t, _ = q.shape
    hp = 2 * HEAD_DIM
    n_pair = C_W // hp
    ex = np.zeros((n_pair, LANE, 6 * hp), np.float32)
    for pr in range(n_pair):
        for p in range(2):
            for hh in range(2):
                src = p * C_HEADS + pr * 2 + hh
                for kind, base in enumerate((GC_LANE, GL_LANE, BETA_LANE)):
                    col = (2 * kind + p) * hp + hh * HEAD_DIM
                    ex[pr, base + src, col:col + HEAD_DIM] = 1.0
    gp = LOCAL_PAIRS
    tok = lambda bi, pg, i: (bi, i, pg)
    dirs = lambda bi, pg, i: (0, bi, i, pg)
    out = jax.ShapeDtypeStruct((2, b, t, C_W), BF16)
    return pl.pallas_call(
        _gdn_local_kernel,
        grid=(b, n_pair // gp, t // blk),
        in_specs=[pl.BlockSpec((1, blk, gp * hp), tok)] * 3
                 + [pl.BlockSpec((1, blk, LANE), lambda bi, pg, i: (bi, i, 0)),
                    pl.BlockSpec((gp, LANE, 6 * hp), lambda bi, pg, i: (pg, 0, 0))],
        out_specs=[pl.BlockSpec((2, 1, blk, gp * hp), dirs)] * 5,
        out_shape=[out] * 5,
        compiler_params=_cparams(("parallel", "parallel", "parallel")),
        name="gdn_local",
    )(q, k, v, comp, jnp.asarray(ex, dtype=BF16))


def _gdn_scan_kernel(*refs, ncb, nb):
    ins = (refs[0:6], refs[6:12])
    ex_ref = refs[12]
    outs = refs[13:15]
    s_ref = refs[15]
    j = pl.program_id(1)

    @pl.when(j == 0)
    def _():
        s_ref[...] = jnp.zeros_like(s_ref)

    gw = 4 * HEAD_DIM
    n_grp = C_W // gw
    lane_head = lax.broadcasted_iota(jnp.int32, (1, gw), 1) // HEAD_DIM
    rr = lax.broadcasted_iota(jnp.int32, (gw, gw), 0) // HEAD_DIM
    cc_ = lax.broadcasted_iota(jnp.int32, (gw, gw), 1) // HEAD_DIM
    blockdiag = rr == cc_
    chains = [(bi, p, g) for bi in range(nb) for p in range(2) for g in range(n_grp)]
    for cc in range(ncb):
        rows = [slice(cc * CHUNK, (cc + 1) * CHUNK), slice((ncb - 1 - cc) * CHUNK, (ncb - cc) * CHUNK)]
        sdec = {(bi, p): jnp.exp(_dot_exact_rhs(ins[p][5][bi, rows[p].start:rows[p].start + SUBLANE, :], ex_ref[...]))
                for bi in range(nb) for p in range(2)}
        lanes = [slice(g * gw, (g + 1) * gw) for g in range(n_grp)]
        ss = [s_ref[bi, p, g] for bi, p, g in chains]
        sbs = [s.astype(BF16) for s in ss]
        wqs = [_dot(jnp.concatenate([ins[p][1][0, bi, rows[p], lanes[g]], ins[p][3][0, bi, rows[p], lanes[g]]], axis=0),
                    sb) for (bi, p, g), sb in zip(chains, sbs)]
        v_news = [ins[p][0][0, bi, rows[p], lanes[g]].astype(F32) - wq[:CHUNK] for (bi, p, g), wq in zip(chains, wqs)]
        vstacks = [jnp.concatenate([jnp.where(lane_head == h, v_new, 0.0).astype(BF16)
                                    for h in range(gw // HEAD_DIM)], axis=0) for v_new in v_news]
        for (bi, p, g), s, wq, v_new, vstack in zip(chains, ss, wqs, v_news, vstacks):
            o = wq[CHUNK:] + _dot(ins[p][4][0, bi, rows[p], lanes[g]], vstack)
            upd = _dot_tn(ins[p][2][0, bi, rows[p], lanes[g]], v_new.astype(BF16))
            decay = sdec[bi, p][0:1, p * C_W + g * gw:p * C_W + (g + 1) * gw]
            s_ref[bi, p, g] = s * decay + jnp.where(blockdiag, upd, 0.0)
            outs[p][bi, rows[p], lanes[g]] = o


def _gdn_scan(u, w, kd, qd, intra, comp, n_lat, blk):
    _, b, t, _ = u.shape
    nl = n_lat // blk
    nx = (t - n_lat) // blk
    ncb = blk // CHUNK
    ex = np.zeros((LANE, 2 * C_W), np.float32)
    for p in range(2):
        for h in range(C_HEADS):
            ex[GL_LANE + p * C_HEADS + h, p * C_W + h * HEAD_DIM:p * C_W + (h + 1) * HEAD_DIM] = 1.0

    def blk0(j):
        return jnp.where(j < nx, nl + j, j - nx)

    def blk1(j):
        return jnp.where(j < nx, nl + nx - 1 - j, nl - 1 - (j - nx))

    nb = SCAN_BATCH if b % SCAN_BATCH == 0 else 1
    specs = []
    for p, bf in enumerate((blk0, blk1)):
        specs += [pl.BlockSpec((1, nb, blk, C_W), lambda bi, j, p=p, bf=bf: (p, bi, bf(j), 0))] * 5
        specs += [pl.BlockSpec((nb, blk, LANE), lambda bi, j, bf=bf: (bi, bf(j), 0))]
    specs += [pl.BlockSpec((LANE, 2 * C_W), lambda bi, j: (0, 0))]
    out = jax.ShapeDtypeStruct((b, t, C_W), F32)
    gw = 4 * HEAD_DIM
    return pl.pallas_call(
        functools.partial(_gdn_scan_kernel, ncb=ncb, nb=nb),
        grid=(b // nb, nl + nx),
        in_specs=specs,
        out_specs=[pl.BlockSpec((nb, blk, C_W), lambda bi, j: (bi, blk0(j), 0)),
                   pl.BlockSpec((nb, blk, C_W), lambda bi, j: (bi, blk1(j), 0))],
        out_shape=[out, out],
        scratch_shapes=[pltpu.VMEM((nb, 2, C_W // gw, gw, gw), F32)],
        compiler_params=_cparams(("parallel", "arbitrary")),
        name="gdn_scan",
    )(u, w, kd, qd, intra, comp, u, w, kd, qd, intra, comp, jnp.asarray(ex, dtype=BF16))


def _merge_kernel(x_ref, mod_ref, oa_ref, ob_ref, oc0_ref, oc1_ref, gate_ref, wmg_ref, wbr_ref, wout_ref, og_ref,
                  gs_ref, lng_ref, lnb_ref, wr_ref, o_ref, h_ref, aff_ref, *, alpha, n_exp):
    x = x_ref[0]
    d = x.shape[-1]
    mod = mod_ref[0, 0]
    h = (_ln(x) * (1.0 + mod[1:2]) + mod[0:1]).astype(BF16)
    gates = _sigmoid(_dot(h, wmg_ref[...]))
    oc = oc0_ref[0] + oc1_ref[0]
    ocn = oc * lax.rsqrt(_dot_exact_rhs(oc * oc, gs_ref[...], 2) + LN_EPS) * og_ref[...] * _silu(gate_ref[0])
    m = gates[:, :d] * _dot(oa_ref[0], wbr_ref[0])
    m = m + gates[:, d:2 * d] * _dot(ob_ref[0], wbr_ref[1])
    m = m + gates[:, 2 * d:] * _dot(ocn.astype(BF16), wbr_ref[2])
    mix = _dot(m.astype(BF16), wout_ref[...])
    y = alpha * x + mod[2:3] * mix
    x1 = _ln(y) * lng_ref[...] + lnb_ref[...]
    o_ref[0] = x1
    h2 = _ln(x1) * (1.0 + mod[4:5]) + mod[3:4]
    h_ref[0] = h2.astype(BF16)
    hp = _split_bf16(h2, 3)
    logits = sum(_dot(hp[i], wr_ref[j]) for i, j in ((2, 0), (1, 1), (0, 2), (1, 0), (0, 1), (0, 0)))
    lane = lax.broadcasted_iota(jnp.int32, logits.shape, 1)
    logits = jnp.where(lane < n_exp, logits, NEG_BIG)
    ex = jnp.exp(logits - jnp.max(logits, axis=-1, keepdims=True))
    aff_ref[0] = ex / jnp.sum(ex, axis=-1, keepdims=True)


def _merge(xa, mods, oa, ob, oc0, oc1, gate_c, wmg, wbr, wout, og, gs, lng, lnb, w_router, n_lat, tm, alpha):
    b, t, d = xa.shape
    nl = n_lat // tm
    n_exp = w_router.shape[-1]
    wr_pad = jnp.pad(w_router, ((0, 0), (0, LANE - n_exp)))
    pieces, rest = [], wr_pad
    for _ in range(3):
        piece = lax.reduce_precision(rest, exponent_bits=8, mantissa_bits=7)
        pieces.append(piece.astype(BF16))
        rest = rest - piece
    wr = jnp.stack(pieces)
    row = lambda bi, i: (bi, i, 0)
    const2 = lambda bi, i: (0, 0)
    return pl.pallas_call(
        functools.partial(_merge_kernel, alpha=alpha, n_exp=n_exp),
        grid=(b, t // tm),
        in_specs=[pl.BlockSpec((1, tm, d), row),
                  pl.BlockSpec((1, 1, N_MOD, d), lambda bi, i: (bi, (i >= nl).astype(jnp.int32), 0, 0)),
                  pl.BlockSpec((1, tm, BRANCH_W), row), pl.BlockSpec((1, tm, BRANCH_W), row),
                  pl.BlockSpec((1, tm, BRANCH_W), row), pl.BlockSpec((1, tm, BRANCH_W), row),
                  pl.BlockSpec((1, tm, BRANCH_W), row),
                  pl.BlockSpec(wmg.shape, const2),
                  pl.BlockSpec(wbr.shape, lambda bi, i: (0, 0, 0)),
                  pl.BlockSpec(wout.shape, const2),
                  pl.BlockSpec(og.shape, const2), pl.BlockSpec(gs.shape, const2),
                  pl.BlockSpec(lng.shape, const2), pl.BlockSpec(lnb.shape, const2),
                  pl.BlockSpec(wr.shape, lambda bi, i: (0, 0, 0))],
        out_specs=[pl.BlockSpec((1, tm, d), row), pl.BlockSpec((1, tm, d), row), pl.BlockSpec((1, tm, LANE), row)],
        out_shape=[jax.ShapeDtypeStruct((b, t, d), F32), jax.ShapeDtypeStruct((b, t, d), BF16),
                   jax.ShapeDtypeStruct((b, t, LANE), F32)],
        compiler_params=_cparams(("parallel", "parallel")),
        name="merge",
    )(xa, mods, oa, ob, oc0, oc1, gate_c, wmg, wbr, wout, og, gs, lng, lnb, wr)


def _ffn_kernel(x_ref, wg_ref, wu_ref, wd_ref, gate_ref, o_ref, acc_ref):
    f = pl.program_id(2)

    @pl.when(f == 0)
    def _():
        acc_ref[...] = jnp.zeros_like(acc_ref)

    x = x_ref[0]
    g = _dot(x, wg_ref[0, 0].astype(BF16))
    u = _dot(x, wu_ref[0, 0].astype(BF16))
    a = (_silu(g) * u).astype(BF16)
    acc_ref[...] += _dot(a, wd_ref[0, 0].astype(BF16))

    @pl.when(f == pl.num_programs(2) - 1)
    def _():
        o_ref[0] = (acc_ref[...] * gate_ref[0]).astype(BF16)


def _expert_ffn(xs, gate, w_gate_up, w_down, layer, tm, tf):
    e, r, d = xs.shape
    f = w_down.shape[2]
    nf = f // tf
    return pl.pallas_call(
        _ffn_kernel,
        grid=(e, r // tm, nf),
        in_specs=[pl.BlockSpec((1, tm, d), lambda ei, i, fi: (ei, i, 0)),
                  pl.BlockSpec((1, 1, d, tf), lambda ei, i, fi: (layer, ei, 0, fi)),
                  pl.BlockSpec((1, 1, d, tf), lambda ei, i, fi: (layer, ei, 0, nf + fi)),
                  pl.BlockSpec((1, 1, tf, d), lambda ei, i, fi: (layer, ei, fi, 0)),
                  pl.BlockSpec((1, tm, 1), lambda ei, i, fi: (ei, i, 0))],
        out_specs=pl.BlockSpec((1, tm, d), lambda ei, i, fi: (ei, i, 0)),
        out_shape=jax.ShapeDtypeStruct((e, r, d), BF16),
        scratch_shapes=[pltpu.VMEM((tm, d), F32)],
        compiler_params=_cparams(("parallel", "parallel", "arbitrary")),
        name="expert_ffn",
    )(xs, w_gate_up, w_gate_up, w_down, gate)


VALID, FIRST, LAST = 1, 2, 4


def _combine_plan(tok_of_pair, n_tok, tm, rwin):
    bsz, n_pair = tok_of_pair.shape
    nt = n_tok // tm
    n_win = n_pair // rwin
    n_item = nt + n_win
    order = jnp.argsort(tok_of_pair, axis=1)
    tok = jnp.take_along_axis(tok_of_pair, order, axis=1)
    bounds = jnp.arange(nt + 1, dtype=jnp.int32) * tm
    off = jnp.sum(tok[:, None, :] < bounds[None, :, None], axis=-1, dtype=jnp.int32)
    first = jnp.minimum(off[:, :-1] // rwin, n_win - 1)
    last = jnp.minimum(jnp.maximum(off[:, 1:] - 1, off[:, :-1]) // rwin, n_win - 1)
    n_items = last - first + 1
    cum = jnp.cumsum(n_items, axis=1)
    k = jnp.arange(n_item, dtype=jnp.int32)
    tile = jnp.sum(cum[:, None, :] <= k[None, :, None], axis=-1, dtype=jnp.int32)
    valid = tile < nt
    tile = jnp.minimum(tile, nt - 1)
    end = jnp.take_along_axis(cum, tile, axis=1)
    start = end - jnp.take_along_axis(n_items, tile, axis=1)
    win = jnp.where(valid, jnp.take_along_axis(first, tile, axis=1) + (k[None] - start),
                    jnp.take_along_axis(last, tile, axis=1))
    flags = (valid * VALID + (valid & (k[None] == start)) * FIRST + (valid & (k[None] == end - 1)) * LAST)
    return order, tok, tile, win.astype(jnp.int32), flags.astype(jnp.int32)


def _combine_kernel(tile_ref, win_ref, flag_ref, y_ref, tok_ref, x_ref, mod_ref, lng_ref, lnb_ref, o_ref, acc_ref, *,
                    alpha, tm):
    b = pl.program_id(0)
    k = pl.program_id(1)
    flags = flag_ref[b, k]

    @pl.when((flags & FIRST) != 0)
    def _():
        acc_ref[...] = jnp.zeros_like(acc_ref)

    @pl.when((flags & VALID) != 0)
    def _():
        tok0 = tile_ref[b, k] * tm
        rows = lax.broadcasted_iota(jnp.int32, (tm, tok_ref.shape[-1]), 0) + tok0
        pick = jnp.where(rows == tok_ref[0, 0], 1.0, 0.0).astype(BF16)
        acc_ref[...] += _dot(pick, y_ref[0])

    @pl.when((flags & LAST) != 0)
    def _():
        mod = mod_ref[0, 0]
        y = alpha * x_ref[0] + mod[5:6] * acc_ref[...]
        o_ref[0] = _ln(y) * lng_ref[...] + lnb_ref[...]


def _combine(y_sorted, tok_sorted, tile, win, flags, xa, mods, lng, lnb, n_lat, tm, rwin, alpha):
    b, t, d = xa.shape
    nl = n_lat // tm
    n_item = tile.shape[1]
    grid_spec = pltpu.PrefetchScalarGridSpec(
        num_scalar_prefetch=3,
        grid=(b, n_item),
        in_specs=[pl.BlockSpec((1, rwin, d), lambda bi, k, tl, wn, fl: (bi, wn[bi, k], 0)),
                  pl.BlockSpec((1, 1, 1, rwin), lambda bi, k, tl, wn, fl: (bi, wn[bi, k], 0, 0)),
                  pl.BlockSpec((1, tm, d), lambda bi, k, tl, wn, fl: (bi, tl[bi, k], 0)),
                  pl.BlockSpec((1, 1, N_MOD, d),
                               lambda bi, k, tl, wn, fl: (bi, (tl[bi, k] >= nl).astype(jnp.int32), 0, 0)),
                  pl.BlockSpec(lng.shape, lambda bi, k, tl, wn, fl: (0, 0)),
                  pl.BlockSpec(lnb.shape, lambda bi, k, tl, wn, fl: (0, 0))],
        out_specs=pl.BlockSpec((1, tm, d), lambda bi, k, tl, wn, fl: (bi, tl[bi, k], 0)),
        scratch_shapes=[pltpu.VMEM((tm, d), F32)],
    )
    return pl.pallas_call(
        functools.partial(_combine_kernel, alpha=alpha, tm=tm),
        grid_spec=grid_spec,
        out_shape=jax.ShapeDtypeStruct((b, t, d), F32),
        compiler_params=_cparams(("parallel", "arbitrary")),
        name="moe_combine",
    )(tile, win, flags, y_sorted, tok_sorted.reshape(b, -1, 1, rwin), xa, mods, lng, lnb)


def _moe(xa, h, aff, mods, n_exp, w_gate_up, w_down, lng, lnb, layer, n_lat, tm, alpha):
    b, t, d = xa.shape
    m_ctx = t - n_lat
    aff = jnp.swapaxes(aff[:, :, :n_exp], 1, 2)
    cap_l = CAPACITY * n_lat // n_exp
    cap_x = CAPACITY * m_ctx // n_exp
    gate_l, idx_l = lax.top_k(aff[:, :, :n_lat], cap_l)
    gate_x, idx_x = lax.top_k(aff[:, :, n_lat:], cap_x)
    idx = jnp.concatenate([idx_l, idx_x + n_lat], axis=-1)
    gate = jnp.concatenate([gate_l, gate_x], axis=-1)
    cap = cap_l + cap_x
    idx_e = jnp.swapaxes(idx, 0, 1)
    bidx = jnp.arange(b)[None, :, None]
    xs = h[bidx, idx_e].reshape(n_exp, b * cap, d)
    gate_e = jnp.swapaxes(gate, 0, 1).reshape(n_exp, b * cap, 1)
    rows = b * cap
    tmr = rows
    for cand in (1088, 1024, 512, 256, 128, 64, 32, 16):
        if rows % cand == 0:
            tmr = cand
            break
    tf = min(512, w_down.shape[2])
    y = _expert_ffn(xs, gate_e, w_gate_up, w_down, layer, tmr, tf)
    n_pair = n_exp * cap
    rwin = next(r for r in (512, 256, 128, 64, 32, 16, 8) if n_pair % r == 0)
    order, tok, tile, win, flags = _combine_plan(idx.reshape(b, n_pair), t, tm, rwin)
    y_sorted = y[order // cap, jnp.arange(b)[:, None] * cap + order % cap]
    return _combine(y_sorted, tok, tile, win, flags, xa, mods, lng, lnb, n_lat, tm, rwin, alpha)


def _rope_tables(n_lat, m_ctx):
    t = np.arange(n_lat)
    rows = (t // GRID_W).astype(np.float32)
    cols = (t % GRID_W).astype(np.float32)
    n_freq = HEAD_DIM // 4
    inv_freq = jnp.asarray(ROPE_THETA, F32) ** (-jnp.arange(n_freq, dtype=F32) / n_freq)
    ang_r = jnp.asarray(rows)[:, None] * inv_freq
    ang_c = jnp.asarray(cols)[:, None] * inv_freq
    cr, sr, cc, sc = jnp.cos(ang_r), jnp.sin(ang_r), jnp.cos(ang_c), jnp.sin(ang_c)
    cos = jnp.concatenate([cr, cr, cc, cc], axis=-1)
    sin = jnp.concatenate([-sr, sr, -sc, sc], axis=-1)
    cos = jnp.concatenate([cos, jnp.ones((m_ctx, HEAD_DIM), F32)], axis=0)
    sin = jnp.concatenate([sin, jnp.zeros((m_ctx, HEAD_DIM), F32)], axis=0)
    return jnp.tile(cos, (1, A_HEADS)), jnp.tile(sin, (1, A_HEADS))


def kernel(x, c, ctx, c_ctx, w_mod, b_mod, w_in, qk_gain, rpb, conv_w, a_log, dt_bias, o_gain, w_branch, w_out,
           ln1_g, ln1_b, w_router, w_gate_up, w_down, ln2_g, ln2_b):
    b, n_lat, d = x.shape
    m_ctx = ctx.shape[1]
    depth = w_mod.shape[0]
    alpha = (2 * depth) ** 0.25
    tm = math.gcd(256, math.gcd(n_lat, m_ctx))
    blk = tm

    xa = jnp.concatenate([x, ctx], axis=1)
    n_rows = -(-(b + 1) // SUBLANE) * SUBLANE
    cc = jnp.zeros((n_rows, d), F32).at[:b].set(c).at[b].set(c_ctx)
    mod_all = _modulation(cc, w_mod, b_mod).reshape(depth, n_rows, N_MOD, d)
    cos, sin = _rope_tables(n_lat, m_ctx)
    gs_mean = _group_sum_matrix(A_Q_W, 1.0 / HEAD_DIM)
    gs_sum = _group_sum_matrix(C_W, 1.0)
    na_geoms, na_table = _na_classes(n_lat, n_lat + m_ctx)

    offs = np.cumsum((0, A_Q_W, A_KV_W, A_KV_W, B_W, B_W, B_W, 3 * C_W, C_W, 2 * C_HEADS, 2 * C_HEADS, N_BRANCH * d))
    for layer in range(depth):
        ml = mod_all[layer]
        mods = jnp.stack([ml[:b], jnp.broadcast_to(ml[b][None], (b, N_MOD, d))], axis=1)
        wl = w_in[layer]
        wa = wl[:, offs[0]:offs[3]].astype(BF16)
        wb = wl[:, offs[3]:offs[6]].astype(BF16)
        wc = wl[:, offs[6]:offs[8]].astype(BF16)
        wab = jnp.pad(wl[:, offs[8]:offs[10]], ((0, 0), (0, LANE - 4 * C_HEADS))).astype(BF16)
        wmg = wl[:, offs[10]:offs[11]].astype(BF16)
        gq = jnp.tile(qk_gain[layer, 0], A_HEADS)[None]
        gk = jnp.tile(qk_gain[layer, 1], A_KV_HEADS)[None]
        qa, ka, va, qb, kb, vb, qkvc, gate_c, ab = _in_projection(
            xa, mods, wa, wb, wc, wab, gq, gk, cos, sin, gs_mean, n_lat, tm)
        oa = _gqa(qa, ka, va, n_lat, min(GQA_TQ, tm), 2 if (n_lat + m_ctx) % (2 * LANE) == 0 else 1)
        ob = _neighborhood(qb, kb, vb, _na_bias_slabs(rpb[layer], na_geoms), na_table, n_lat)
        qc, kc, vc, comp = _gdn_prep(qkvc, ab, conv_w[layer], a_log[layer], dt_bias[layer], gs_sum, n_lat, tm)
        u, w, kd, qd, intra = _gdn_local(qc, kc, vc, comp, blk)
        oc0, oc1 = _gdn_scan(u, w, kd, qd, intra, comp, n_lat, blk)
        og = jnp.tile(o_gain[layer], C_HEADS)[None]
        xa, h, aff = _merge(xa, mods, oa, ob, oc0, oc1, gate_c, wmg, w_branch[layer].astype(BF16),
                            w_out[layer].astype(BF16), og, gs_mean, ln1_g[layer][None], ln1_b[layer][None],
                            w_router[layer], n_lat, tm, alpha)
        xa = _moe(xa, h, aff, mods, w_router.shape[-1], w_gate_up, w_down, ln2_g[layer][None], ln2_b[layer][None],
                  layer, n_lat, tm, alpha)
    return xa[:, :n_lat]
```

```python
import functools
import math

import numpy as np
import jax
import jax.numpy as jnp
from jax import lax
from jax.experimental import pallas as pl
from jax.experimental.pallas import tpu as pltpu

F32 = jnp.float32
BF16 = jnp.bfloat16
HIGHEST = lax.Precision.HIGHEST

GRID_W = 64
HEAD_DIM = 64
A_HEADS = 8
A_KV_HEADS = 2
B_HEADS = 8
C_HEADS = 8
NA_ROWS = 8
NA_COLS = 16
CONV_K = 4
CHUNK = 64
N_BRANCH = 3
BRANCH_W = 512
CAPACITY = 2
N_MOD = 6
LN_EPS = 1e-6
ROPE_THETA = 10000.0
NEG_BIG = -1e30
LOG2E = 1.4426950408889634

A_Q_W = A_HEADS * HEAD_DIM
A_KV_W = A_KV_HEADS * HEAD_DIM
B_W = B_HEADS * HEAD_DIM
C_W = C_HEADS * HEAD_DIM
LANE = 128
SUBLANE = 8
V7X_VMEM_BYTES = 64 * 1024 * 1024
VMEM_LIMIT = V7X_VMEM_BYTES * 7 // 8
GQA_TQ = 128
LOCAL_PAIRS = 2
SCAN_BATCH = 4
VT_ROWS = HEAD_DIM + 16
NA_WIN = NA_ROWS + 2


def _cparams(sem):
    return pltpu.CompilerParams(dimension_semantics=sem, vmem_limit_bytes=VMEM_LIMIT)


def _dot(a, b):
    return jnp.dot(a, b, preferred_element_type=F32)


def _dot_hi(a, b):
    return jnp.dot(a, b, precision=HIGHEST, preferred_element_type=F32)


def _dot_nt(a, b):
    return lax.dot_general(a, b, (((1,), (1,)), ((), ())), preferred_element_type=F32)


def _dot_tn(a, b):
    return lax.dot_general(a, b, (((0,), (0,)), ((), ())), preferred_element_type=F32)


def _split_bf16(x, pieces):
    out = []
    for _ in range(pieces - 1):
        h = x.astype(BF16)
        out.append(h)
        x = x - h.astype(F32)
    out.append(x.astype(BF16))
    return out


def _dot_exact_rhs(a, b, pieces=3):
    return sum(_dot(p, b) for p in _split_bf16(a, pieces))


def _dot_exact_lhs(a, b, pieces=3):
    return sum(_dot(a, p) for p in _split_bf16(b, pieces))


def _ln(x):
    mu = jnp.mean(x, axis=-1, keepdims=True)
    xc = x - mu
    var = jnp.mean(xc * xc, axis=-1, keepdims=True)
    return xc * lax.rsqrt(var + LN_EPS)


def _sigmoid(x):
    return 1.0 / (1.0 + jnp.exp(-x))


def _silu(x):
    return x * _sigmoid(x)


def _group_sum_matrix(width, scale):
    g = np.arange(width) // HEAD_DIM
    return jnp.asarray((g[:, None] == g[None, :]).astype(np.float32) * scale, dtype=BF16)


def _mod_kernel(c_ref, w_ref, b_ref, o_ref):
    c = c_ref[...]
    o_ref[0] = _dot_hi(_silu(c), w_ref[0]) + b_ref[0]


def _modulation(cc, w_mod, b_mod):
    depth, d, dm = w_mod.shape
    r = cc.shape[0]
    tn = min(dm, 1536)
    return pl.pallas_call(
        _mod_kernel,
        grid=(depth, dm // tn),
        in_specs=[pl.BlockSpec((r, d), lambda l, j: (0, 0)),
                  pl.BlockSpec((1, d, tn), lambda l, j: (l, 0, j)),
                  pl.BlockSpec((1, 1, tn), lambda l, j: (l, 0, j))],
        out_specs=pl.BlockSpec((1, r, tn), lambda l, j: (l, 0, j)),
        out_shape=jax.ShapeDtypeStruct((depth, r, dm), F32),
        compiler_params=_cparams(("parallel", "parallel")),
        name="modulation",
    )(cc, w_mod, b_mod.reshape(depth, 1, dm))


def _rope_swap(y):
    w = y.shape[-1]
    lane = lax.broadcasted_iota(jnp.int32, y.shape, 1)
    up = pltpu.roll(y, w - 16, 1)
    down = pltpu.roll(y, 16, 1)
    return jnp.where((lane % 32) < 16, up, down)


def _inproj_kernel(x_ref, mod_ref, wa_ref, wb_ref, wc_ref, wab_ref, gq_ref, gk_ref, cos_ref, sin_ref,
                   gs_ref, qa_ref, ka_ref, va_ref, qb_ref, kb_ref, vb_ref, qkvc_ref, gatec_ref, ab_ref):
    x = x_ref[0]
    mod = mod_ref[0, 0]
    h = (_ln(x) * (1.0 + mod[1:2]) + mod[0:1]).astype(BF16)
    scale = HEAD_DIM ** -0.5 * LOG2E

    za = _dot(h, wa_ref[...])
    q = za[:, :A_Q_W]
    k = za[:, A_Q_W:A_Q_W + A_KV_W]
    v = za[:, A_Q_W + A_KV_W:]
    cos = cos_ref[...]
    sin = sin_ref[...]
    gs = gs_ref[...]
    qn = q * lax.rsqrt(_dot_exact_rhs(q * q, gs, 2) + LN_EPS) * gq_ref[...]
    qn = qn * cos + _rope_swap(qn) * sin
    qa_ref[0] = (qn * scale).astype(BF16)
    kn = k * lax.rsqrt(_dot_exact_rhs(k * k, gs[:A_KV_W, :A_KV_W], 2) + LN_EPS) * gk_ref[...]
    kn = kn * cos[:, :A_KV_W] + _rope_swap(kn) * sin[:, :A_KV_W]
    ones = jnp.ones((VT_ROWS - HEAD_DIM, x.shape[0]), F32)
    vt = v.T
    for j in range(A_KV_HEADS):
        ka_ref[0, j] = kn[:, j * HEAD_DIM:(j + 1) * HEAD_DIM].astype(BF16)
        va_ref[0, j] = jnp.concatenate([vt[j * HEAD_DIM:(j + 1) * HEAD_DIM], ones], axis=0).astype(BF16)

    zb = _dot(h, wb_ref[...])
    qb_ref[0] = (zb[:, :B_W] * scale).astype(BF16)
    kbv = zb[:, B_W:2 * B_W]
    vt = zb[:, 2 * B_W:].T
    for j in range(B_HEADS):
        kb_ref[0, j] = kbv[:, j * HEAD_DIM:(j + 1) * HEAD_DIM].astype(BF16)
        vb_ref[0, j] = jnp.concatenate([vt[j * HEAD_DIM:(j + 1) * HEAD_DIM], ones], axis=0).astype(BF16)

    zc = _dot(h, wc_ref[...])
    qkvc_ref[0] = zc[:, :3 * C_W]
    gatec_ref[0] = zc[:, 3 * C_W:]
    ab_ref[0] = _dot(h, wab_ref[...])


def _in_projection(xa, mods, wa, wb, wc, wab, gq, gk, cos, sin, gs, n_lat, tm):
    b, t, d = xa.shape
    nl = n_lat // tm
    row = lambda bi, i: (bi, i, 0)
    const = lambda bi, i: (0, 0)
    outs = [
        (jax.ShapeDtypeStruct((b, t, A_Q_W), BF16), pl.BlockSpec((1, tm, A_Q_W), row)),
        (jax.ShapeDtypeStruct((b, A_KV_HEADS, t, HEAD_DIM), BF16),
         pl.BlockSpec((1, A_KV_HEADS, tm, HEAD_DIM), lambda bi, i: (bi, 0, i, 0))),
        (jax.ShapeDtypeStruct((b, A_KV_HEADS, VT_ROWS, t), BF16),
         pl.BlockSpec((1, A_KV_HEADS, VT_ROWS, tm), lambda bi, i: (bi, 0, 0, i))),
        (jax.ShapeDtypeStruct((b, t, B_W), BF16), pl.BlockSpec((1, tm, B_W), row)),
        (jax.ShapeDtypeStruct((b, B_HEADS, t, HEAD_DIM), BF16),
         pl.BlockSpec((1, B_HEADS, tm, HEAD_DIM), lambda bi, i: (bi, 0, i, 0))),
        (jax.ShapeDtypeStruct((b, B_HEADS, VT_ROWS, t), BF16),
         pl.BlockSpec((1, B_HEADS, VT_ROWS, tm), lambda bi, i: (bi, 0, 0, i))),
        (jax.ShapeDtypeStruct((b, t, 3 * C_W), F32), pl.BlockSpec((1, tm, 3 * C_W), row)),
        (jax.ShapeDtypeStruct((b, t, C_W), F32), pl.BlockSpec((1, tm, C_W), row)),
        (jax.ShapeDtypeStruct((b, t, LANE), F32), pl.BlockSpec((1, tm, LANE), row)),
    ]
    return pl.pallas_call(
        _inproj_kernel,
        grid=(b, t // tm),
        in_specs=[pl.BlockSpec((1, tm, d), row),
                  pl.BlockSpec((1, 1, N_MOD, d), lambda bi, i: (bi, (i >= nl).astype(jnp.int32), 0, 0)),
                  pl.BlockSpec(wa.shape, const), pl.BlockSpec(wb.shape, const),
                  pl.BlockSpec(wc.shape, const), pl.BlockSpec(wab.shape, const),
                  pl.BlockSpec(gq.shape, const), pl.BlockSpec(gk.shape, const),
                  pl.BlockSpec((tm, A_Q_W), lambda bi, i: (i, 0)),
                  pl.BlockSpec((tm, A_Q_W), lambda bi, i: (i, 0)),
                  pl.BlockSpec(gs.shape, const)],
        out_specs=[o[1] for o in outs],
        out_shape=[o[0] for o in outs],
        compiler_params=_cparams(("parallel", "parallel")),
        name="in_projection",
    )(xa, mods, wa, wb, wc, wab, gq, gk, cos, sin, gs)


def _gqa_kernel(q_ref, k_ref, v_ref, o_ref, *, tq, ck, n_lat, n_tot):
    i = pl.program_id(1)
    g = A_HEADS // A_KV_HEADS

    def attend(key_lo, key_hi, n_split):
        qs = [jnp.concatenate([q_ref[0, :, (kh * g + j) * HEAD_DIM:(kh * g + j + 1) * HEAD_DIM] for j in range(g)],
                              axis=0) for kh in range(A_KV_HEADS)]
        step = (key_hi - key_lo) // n_split
        units = [(kh, key_lo + r * step, key_lo + (r + 1) * step) for r in range(n_split) for kh in range(A_KV_HEADS)]
        ss, ms, ps, oes = {}, {}, {}, {}
        for n in range(len(units) + 2):
            if n < len(units):
                kh, lo, hi = units[n]
                ss[n] = _dot_nt(k_ref[0, kh, lo:hi, :], qs[kh])
            if 0 <= n - 1 < len(units):
                ms[n - 1] = jnp.max(ss[n - 1], axis=0, keepdims=True).astype(BF16).astype(F32)
                ps[n - 1] = jnp.exp2(ss.pop(n - 1).astype(BF16) - ms[n - 1].astype(BF16))
            if 0 <= n - 2 < len(units):
                kh, lo, hi = units[n - 2]
                oes[n - 2] = _dot(v_ref[0, kh, :, lo:hi], ps.pop(n - 2))
        for kh in range(A_KV_HEADS):
            mine = [n for n, u in enumerate(units) if u[0] == kh]
            m = functools.reduce(jnp.maximum, [ms[n] for n in mine])
            oe = sum(oes[n] * jnp.exp2(ms[n] - m) for n in mine)
            o = (oe[:HEAD_DIM] / oe[HEAD_DIM:HEAD_DIM + 1]).T.astype(BF16)
            for j in range(g):
                o_ref[0, :, (kh * g + j) * HEAD_DIM:(kh * g + j + 1) * HEAD_DIM] = o[j * tq:(j + 1) * tq]

    @pl.when(i < n_lat // tq)
    def _():
        attend(0, n_tot, ck)

    @pl.when(i >= n_lat // tq)
    def _():
        attend(n_lat, n_tot, 1)


def _gqa(qa, ka, va, n_lat, tq, ck):
    b, t, _ = qa.shape
    return pl.pallas_call(
        functools.partial(_gqa_kernel, tq=tq, ck=ck, n_lat=n_lat, n_tot=t),
        grid=(b, t // tq),
        in_specs=[pl.BlockSpec((1, tq, A_Q_W), lambda bi, i: (bi, i, 0)),
                  pl.BlockSpec((1, A_KV_HEADS, t, HEAD_DIM), lambda bi, i: (bi, 0, 0, 0)),
                  pl.BlockSpec((1, A_KV_HEADS, VT_ROWS, t), lambda bi, i: (bi, 0, 0, 0))],
        out_specs=pl.BlockSpec((1, tq, A_Q_W), lambda bi, i: (bi, i, 0)),
        out_shape=jax.ShapeDtypeStruct((b, t, A_Q_W), BF16),
        compiler_params=_cparams(("parallel", "parallel")),
        name="gqa",
    )(qa, ka, va)


def _na_window_start(r0, rows):
    return np.clip(r0 - NA_ROWS // 2, 0, rows - NA_ROWS - 1) // 2 * 2


def _na_classes(n_lat, n_tot):
    rows = n_lat // GRID_W
    assert rows >= NA_WIN and rows % 2 == 0
    geoms, table = [], []
    for r0 in range(0, rows, 2):
        geom = (int(_na_window_start(r0, rows)) - r0,) + tuple(
            int(np.clip(r0 + qr - NA_ROWS // 2, 0, rows - NA_ROWS)) - r0 for qr in range(2))
        if geom not in geoms:
            geoms.append(geom)
        table.append(geoms.index(geom))
    table += [len(geoms)] * ((n_tot - n_lat) // (2 * GRID_W))
    return geoms, np.asarray(table, np.int32)


def _na_bias_slabs(rpb, geoms):
    n_head = rpb.shape[0]
    cj = np.arange(GRID_W)[:, None]
    c = np.arange(GRID_W)[None, :]
    col_start = np.clip(c - NA_COLS // 2, 0, GRID_W - NA_COLS)
    col_ok = (cj >= col_start) & (cj < col_start + NA_COLS)
    col_off = np.clip(cj - c + (NA_COLS - 1), 0, 2 * NA_COLS - 2)
    spread = np.zeros((2 * NA_COLS - 1, GRID_W * GRID_W), np.float32)
    spread[col_off.reshape(-1), np.arange(GRID_W * GRID_W)] = 1.0
    kr = np.arange(NA_WIN)[:, None]
    qr = np.arange(2)[None, :]
    pick = np.zeros((len(geoms), NA_WIN, 2, 2 * NA_ROWS - 1), np.float32)
    ok = np.zeros((len(geoms), 1, NA_WIN, 2, GRID_W, GRID_W), bool)
    for n, (a, b0, b1) in enumerate(geoms):
        first = np.where(qr == 0, b0, b1)
        row_ok = (a + kr >= first) & (a + kr < first + NA_ROWS)
        row_off = np.clip(a + kr - qr + (NA_ROWS - 1), 0, 2 * NA_ROWS - 2)
        pick[n, kr, qr, row_off] = 1.0
        ok[n, 0] = row_ok[:, :, None, None] & col_ok[None, None]
    rows = jnp.einsum("nkqo,hoc->nhkqc", jnp.asarray(pick), rpb * LOG2E, precision=HIGHEST)
    full = jnp.dot(rows, jnp.asarray(spread), precision=HIGHEST)
    full = full.reshape(len(geoms), n_head, NA_WIN, 2, GRID_W, GRID_W)
    masked = jnp.where(jnp.asarray(ok), full, NEG_BIG)
    slabs = jnp.concatenate([masked[:, :, :, 0], masked[:, :, :, 1]], axis=-1)
    slabs = slabs.reshape(len(geoms), n_head, NA_WIN * GRID_W, 2 * GRID_W)
    return jnp.concatenate([slabs, jnp.full_like(slabs[:1], NEG_BIG)], axis=0).astype(F32)


def _na_kernel(cls_ref, q_ref, k_ref, vt_ref, *rest, n_lat, n_tot, pps):
    del cls_ref
    bias_refs, o_ref = rest[:pps], rest[pps]
    rows = n_lat // GRID_W
    tq = 2 * GRID_W
    hs = [slice(h * HEAD_DIM, (h + 1) * HEAD_DIM) for h in range(B_HEADS)]
    units = []
    for j in range(pps):
        r0 = 2 * (pps * pl.program_id(1) + j)
        start = jnp.clip(r0 - NA_ROWS // 2, 0, rows - NA_ROWS - 1) // 2 * 2
        nb = pl.ds(pl.multiple_of(start * GRID_W, 2 * GRID_W), NA_WIN * GRID_W)
        units += [(j, h, nb) for h in range(B_HEADS)]
    qs = [q_ref[0, j * tq:(j + 1) * tq, hs[h]] for j, h, _ in units]
    s_nb = [_dot_nt(k_ref[0, h, nb, :], q) + bias_refs[j][0, h] for (j, h, nb), q in zip(units, qs)]
    s_cx = [_dot_nt(k_ref[0, h, n_lat:n_tot, :], q) for (j, h, nb), q in zip(units, qs)]
    ms = [jnp.maximum(jnp.max(a, axis=0, keepdims=True), jnp.max(c, axis=0, keepdims=True))
          for a, c in zip(s_nb, s_cx)]
    p_nb = [jnp.exp2((a - m).astype(BF16)) for a, m in zip(s_nb, ms)]
    p_cx = [jnp.exp2((c - m).astype(BF16)) for c, m in zip(s_cx, ms)]
    for (j, h, nb), pn, pc in zip(units, p_nb, p_cx):
        oe = _dot(vt_ref[0, h, :, nb], pn) + _dot(vt_ref[0, h, :, n_lat:n_tot], pc)
        o_ref[0, j * tq:(j + 1) * tq, hs[h]] = (oe[:HEAD_DIM] / oe[HEAD_DIM:HEAD_DIM + 1]).T.astype(BF16)


def _neighborhood(qb, kb, vbt, slabs, table, n_lat):
    b, t, _ = qb.shape
    tq = 2 * GRID_W
    pps = 2 if (t // tq) % 2 == 0 else 1
    slab_specs = [pl.BlockSpec((1,) + slabs.shape[1:], lambda bi, pi, cls, j=j: (cls[pps * pi + j], 0, 0, 0))
                  for j in range(pps)]
    grid_spec = pltpu.PrefetchScalarGridSpec(
        num_scalar_prefetch=1,
        grid=(b, t // (pps * tq)),
        in_specs=[pl.BlockSpec((1, pps * tq, B_W), lambda bi, pi, cls: (bi, pi, 0)),
                  pl.BlockSpec((1, B_HEADS, t, HEAD_DIM), lambda bi, pi, cls: (bi, 0, 0, 0)),
                  pl.BlockSpec((1, B_HEADS, VT_ROWS, t), lambda bi, pi, cls: (bi, 0, 0, 0))] + slab_specs,
        out_specs=pl.BlockSpec((1, pps * tq, B_W), lambda bi, pi, cls: (bi, pi, 0)),
    )
    return pl.pallas_call(
        functools.partial(_na_kernel, n_lat=n_lat, n_tot=t, pps=pps),
        grid_spec=grid_spec,
        out_shape=jax.ShapeDtypeStruct((b, t, B_W), BF16),
        compiler_params=_cparams(("parallel", "parallel")),
        name="neighborhood",
    )(jnp.asarray(table), qb, kb, vbt, *([slabs] * pps))


GC_LANE, GL_LANE, BETA_LANE = 0, 2 * C_HEADS, 4 * C_HEADS


def _gdn_prep_kernel(x_ref, prev_ref, next_ref, ab_ref, cw_ref, alog_ref, dtb_ref, gs_ref,
                     q_ref, k_ref, v_ref, comp_ref, *, tm, n_lat, n_tot):
    i = pl.program_id(1)
    nl = n_lat // tm
    nt = n_tot // tm
    first = jnp.logical_or(i == 0, i == nl)
    last = jnp.logical_or(i == nl - 1, i == nt - 1)
    x = x_ref[0]
    pv = jnp.where(first, 0.0, prev_ref[0])
    nx = jnp.where(last, 0.0, next_ref[0])
    row = lax.broadcasted_iota(jnp.int32, (tm, 1), 0)
    prev1 = pv[SUBLANE - 1:SUBLANE]
    prev2 = pv[SUBLANE - 2:SUBLANE - 1]
    xm1 = jnp.where(row == 0, prev1, pltpu.roll(x, 1, 0))
    xm2 = jnp.where(row == 0, prev2, jnp.where(row == 1, prev1, pltpu.roll(x, 2, 0)))
    xp1 = jnp.where(row == tm - 1, nx[0:1], pltpu.roll(x, tm - 1, 0))
    cw = cw_ref[...]
    y = _silu(xm2 * cw[0:1] + xm1 * cw[1:2] + x * cw[2:3] + xp1 * cw[3:4])
    q = y[:, :C_W]
    k = y[:, C_W:2 * C_W]
    gs = gs_ref[...]
    q_ref[0] = q * lax.rsqrt(_dot_exact_rhs(q * q, gs, 2) + LN_EPS) * (HEAD_DIM ** -0.5)
    k_ref[0] = k * lax.rsqrt(_dot_exact_rhs(k * k, gs, 2) + LN_EPS)
    v_ref[0] = y[:, 2 * C_W:]
    ab = ab_ref[0]
    z = ab + dtb_ref[...]
    softplus = jnp.maximum(z, 0.0) + jnp.log(1.0 + jnp.exp(-jnp.abs(z)))
    lane = lax.broadcasted_iota(jnp.int32, ab.shape, 1)
    gb = jnp.where(lane < 2 * C_HEADS, -jnp.exp(alog_ref[...]) * softplus, _sigmoid(ab))
    ii = lax.broadcasted_iota(jnp.int32, (tm, tm), 0)
    jj = lax.broadcasted_iota(jnp.int32, (tm, tm), 1)
    same = (ii // CHUNK) == (jj // CHUNK)
    gb3 = _split_bf16(gb, 3)

    def masked_sum(mask):
        mb = jnp.where(mask, 1.0, 0.0).astype(BF16)
        return sum(_dot(mb, piece) for piece in gb3)

    prefix = masked_sum(jnp.logical_and(same, jj <= ii))
    suffix = masked_sum(jnp.logical_and(same, jj >= ii))
    total = masked_sum(same)
    comp = jnp.where(lane < C_HEADS, prefix,
                     jnp.where(lane < GL_LANE, suffix,
                               jnp.where(lane < BETA_LANE, pltpu.roll(total, GL_LANE, 1),
                                         jnp.where(lane < BETA_LANE + 2 * C_HEADS, pltpu.roll(gb, 2 * C_HEADS, 1),
                                                   0.0))))
    comp_ref[0] = comp


def _gdn_prep(qkvc, ab, conv_w, a_log, dt_bias, gs, n_lat, tm):
    b, t, _ = qkvc.shape
    c3 = 3 * C_W
    alog_row = jnp.zeros((1, LANE), F32).at[0, :2 * C_HEADS].set(a_log.reshape(-1))
    dtb_row = jnp.zeros((1, LANE), F32).at[0, :2 * C_HEADS].set(dt_bias.reshape(-1))
    row = lambda bi, i: (bi, i, 0)
    const2 = lambda bi, i: (0, 0)
    hb = tm // SUBLANE
    last8 = t // SUBLANE - 1
    tok = jax.ShapeDtypeStruct((b, t, C_W), F32)
    return pl.pallas_call(
        functools.partial(_gdn_prep_kernel, tm=tm, n_lat=n_lat, n_tot=t),
        grid=(b, t // tm),
        in_specs=[pl.BlockSpec((1, tm, c3), row),
                  pl.BlockSpec((1, SUBLANE, c3), lambda bi, i: (bi, jnp.maximum(i * hb - 1, 0), 0)),
                  pl.BlockSpec((1, SUBLANE, c3), lambda bi, i: (bi, jnp.minimum((i + 1) * hb, last8), 0)),
                  pl.BlockSpec((1, tm, LANE), row),
                  pl.BlockSpec((CONV_K, c3), const2),
                  pl.BlockSpec((1, LANE), const2), pl.BlockSpec((1, LANE), const2),
                  pl.BlockSpec(gs.shape, const2)],
        out_specs=[pl.BlockSpec((1, tm, C_W), row)] * 3 + [pl.BlockSpec((1, tm, LANE), row)],
        out_shape=[tok, tok, tok, jax.ShapeDtypeStruct((b, t, LANE), F32)],
        compiler_params=_cparams(("parallel", "parallel")),
        name="gdn_prep",
    )(qkvc, qkvc, qkvc, ab, conv_w, alog_row, dtb_row, gs)


def _half_block_rows(a, h, upper):
    return jnp.concatenate([a[s:s + h] for s in range(0 if upper else h, a.shape[0], 2 * h)], axis=0)


def _with_half_block_rows(full, part, h, upper):
    pieces = []
    for blk in range(full.shape[0] // h):
        if (blk % 2 == 0) == upper:
            pieces.append(part[(blk // 2) * h:(blk // 2 + 1) * h])
        else:
            pieces.append(full[blk * h:(blk + 1) * h])
    return jnp.concatenate(pieces, axis=0)


def _unit_tri_inverses(lms, uppers, eye, same_block):
    def mm(a, b_):
        return _dot(a.astype(BF16), b_.astype(BF16))

    lds = [jnp.where(same_block[SUBLANE], lm, 0.0) for lm in lms]
    xs = [eye - ld for ld in lds]
    lds = [ld.astype(BF16) for ld in lds]
    pws = [_dot(ld, ld).astype(BF16) for ld in lds]
    xs = [x + mm(x, pw) for x, pw in zip(xs, pws)]
    pws = [_dot(pw, pw).astype(BF16) for pw in pws]
    xs = [x + mm(x, pw) for x, pw in zip(xs, pws)]
    h = SUBLANE
    while h < CHUNK:
        level = jnp.logical_not(same_block[h]) if 2 * h == CHUNK else jnp.logical_and(
            same_block[2 * h], jnp.logical_not(same_block[h]))
        xbs = [x.astype(BF16) for x in xs]
        ls = [_half_block_rows(jnp.where(level, lm, 0.0), h, up).astype(BF16) for lm, up in zip(lms, uppers)]
        ys = [_dot(l, xb) for l, xb in zip(ls, xbs)]
        zs = [_with_half_block_rows(jnp.zeros_like(x), y, h, up).astype(BF16) for x, y, up in zip(xs, ys, uppers)]
        xrs = [_half_block_rows(x, h, up) for x, up in zip(xs, uppers)]
        upd = [_dot(xr.astype(BF16), z) for xr, z in zip(xrs, zs)]
        xs = [_with_half_block_rows(x, xr - u, h, up) for x, xr, u, up in zip(xs, xrs, upd, uppers)]
        h *= 2
    return xs


def _gdn_local_kernel(q_ref, k_ref, v_ref, comp_ref, ex_ref, u_ref, w_ref, kd_ref, qd_ref, in_ref):
    n = q_ref.shape[1]
    hp = 2 * HEAD_DIM
    n_pr = q_ref.shape[2] // hp
    ii = lax.broadcasted_iota(jnp.int32, (n, n), 0)
    jj = lax.broadcasted_iota(jnp.int32, (n, n), 1)
    same = (ii // CHUNK) == (jj // CHUNK)
    eye = (ii == jj).astype(F32)
    same_block = {h: (ii // h) == (jj // h) for h in (8, 16, 32)}
    lane = lax.broadcasted_iota(jnp.int32, (1, hp), 1)
    lane2 = lax.broadcasted_iota(jnp.int32, (1, 2 * hp), 1)
    incls = [jnp.logical_and(same, ii >= jj), jnp.logical_and(same, ii <= jj)]
    stricts = [jnp.logical_and(same, ii > jj), jnp.logical_and(same, ii < jj)]
    lms, intras, rhss = [], [], []
    for pr in range(n_pr):
        ls = slice(pr * hp, (pr + 1) * hp)
        q = q_ref[0, :, ls]
        k = k_ref[0, :, ls]
        v = v_ref[0, :, ls]
        e = _dot_exact_rhs(comp_ref[0], ex_ref[pr])
        kh = [jnp.where((lane // HEAD_DIM) == hh, k, 0.0).astype(BF16) for hh in range(2)]
        for p in range(2):
            gc = e[:, p * hp:(p + 1) * hp]
            gl = e[:, (2 + p) * hp:(3 + p) * hp]
            beta = e[:, (4 + p) * hp:(5 + p) * hp]
            eg = jnp.exp(gc)
            kb = k * beta
            kd_ref[p, 0, :, ls] = (k * jnp.exp(gl - gc)).astype(BF16)
            qd_ref[p, 0, :, ls] = (q * eg).astype(BF16)
            rhs = jnp.concatenate([v * beta, kb * eg], axis=1)
            kbq = jnp.concatenate([kb, q], axis=0).astype(BF16)
            for hh in range(2):
                gcb = jnp.broadcast_to(gc[:, hh * HEAD_DIM:hh * HEAD_DIM + 1], (n, n))
                decay = jnp.exp(jnp.where(incls[p], gcb - gcb.T, NEG_BIG))
                aq = _dot_nt(kbq, kh[hh])
                lms.append(jnp.where(stricts[p], aq[:n] * decay, 0.0))
                intras.append(jnp.where(incls[p], aq[n:] * decay, 0.0))
                rhss.append(jnp.where(((lane2 % hp) // HEAD_DIM) == hh, rhs, 0.0).astype(BF16))
    uppers = [p == 1 for _ in range(n_pr) for p in range(2) for _ in range(2)]
    tmats = _unit_tri_inverses(lms, uppers, eye, same_block)
    for pr in range(n_pr):
        ls = slice(pr * hp, (pr + 1) * hp)
        for p in range(2):
            at = 4 * pr + 2 * p
            uw = sum(_dot(tmats[at + hh].astype(BF16), rhss[at + hh]) for hh in range(2))
            tiles = []
            for c in range(n // CHUNK):
                halves = []
                for hh in range(2):
                    lo = c * CHUNK // hp * hp
                    piece = intras[at + hh][c * CHUNK:(c + 1) * CHUNK, lo:lo + hp]
                    halves.append(piece if c % 2 == hh else pltpu.roll(piece, HEAD_DIM, 1))
                tiles.append(jnp.where(lane < HEAD_DIM, halves[0], halves[1]))
            intra2 = jnp.concatenate(tiles, axis=0)
            u_ref[p, 0, :, ls] = uw[:, :hp].astype(BF16)
            w_ref[p, 0, :, ls] = uw[:, hp:].astype(BF16)
            in_ref[p, 0, :, ls] = intra2.astype(BF16)


def _gdn_local(q, k, v, comp, blk):
    b, t, _ = q.shape
    hp = 2 * HEAD_DIM
    n_pair = C_W // hp
    ex = np.zeros((n_pair, LANE, 6 * hp), np.float32)
    for pr in range(n_pair):
        for p in range(2):
            for hh in range(2):
                src = p * C_HEADS + pr * 2 + hh
                for kind, base in enumerate((GC_LANE, GL_LANE, BETA_LANE)):
                    col = (2 * kind + p) * hp + hh * HEAD_DIM
                    ex[pr, base + src, col:col + HEAD_DIM] = 1.0
    gp = LOCAL_PAIRS
    tok = lambda bi, pg, i: (bi, i, pg)
    dirs = lambda bi, pg, i: (0, bi, i, pg)
    out = jax.ShapeDtypeStruct((2, b, t, C_W), BF16)
    return pl.pallas_call(
        _gdn_local_kernel,
        grid=(b, n_pair // gp, t // blk),
        in_specs=[pl.BlockSpec((1, blk, gp * hp), tok)] * 3
                 + [pl.BlockSpec((1, blk, LANE), lambda bi, pg, i: (bi, i, 0)),
                    pl.BlockSpec((gp, LANE, 6 * hp), lambda bi, pg, i: (pg, 0, 0))],
        out_specs=[pl.BlockSpec((2, 1, blk, gp * hp), dirs)] * 5,
        out_shape=[out] * 5,
        compiler_params=_cparams(("parallel", "parallel", "parallel")),
        name="gdn_local",
    )(q, k, v, comp, jnp.asarray(ex, dtype=BF16))


def _gdn_scan_kernel(*refs, ncb, nb):
    ins = (refs[0:6], refs[6:12])
    ex_ref = refs[12]
    outs = refs[13:15]
    s_ref = refs[15]
    j = pl.program_id(1)

    @pl.when(j == 0)
    def _():
        s_ref[...] = jnp.zeros_like(s_ref)

    gw = 4 * HEAD_DIM
    n_grp = C_W // gw
    lane_head = lax.broadcasted_iota(jnp.int32, (1, gw), 1) // HEAD_DIM
    rr = lax.broadcasted_iota(jnp.int32, (gw, gw), 0) // HEAD_DIM
    cc_ = lax.broadcasted_iota(jnp.int32, (gw, gw), 1) // HEAD_DIM
    blockdiag = rr == cc_
    chains = [(bi, p, g) for bi in range(nb) for p in range(2) for g in range(n_grp)]
    for cc in range(ncb):
        rows = [slice(cc * CHUNK, (cc + 1) * CHUNK), slice((ncb - 1 - cc) * CHUNK, (ncb - cc) * CHUNK)]
        sdec = {(bi, p): jnp.exp(_dot_exact_rhs(ins[p][5][bi, rows[p].start:rows[p].start + SUBLANE, :], ex_ref[...]))
                for bi in range(nb) for p in range(2)}
        lanes = [slice(g * gw, (g + 1) * gw) for g in range(n_grp)]
        ss = [s_ref[bi, p, g] for bi, p, g in chains]
        sbs = [s.astype(BF16) for s in ss]
        wqs = [_dot(jnp.concatenate([ins[p][1][0, bi, rows[p], lanes[g]], ins[p][3][0, bi, rows[p], lanes[g]]], axis=0),
                    sb) for (bi, p, g), sb in zip(chains, sbs)]
        v_news = [ins[p][0][0, bi, rows[p], lanes[g]].astype(F32) - wq[:CHUNK] for (bi, p, g), wq in zip(chains, wqs)]
        vstacks = [jnp.concatenate([jnp.where(lane_head == h, v_new, 0.0).astype(BF16)
                                    for h in range(gw // HEAD_DIM)], axis=0) for v_new in v_news]
        for (bi, p, g), s, wq, v_new, vstack in zip(chains, ss, wqs, v_news, vstacks):
            o = wq[CHUNK:] + _dot(ins[p][4][0, bi, rows[p], lanes[g]], vstack)
            upd = _dot_tn(ins[p][2][0, bi, rows[p], lanes[g]], v_new.astype(BF16))
            decay = sdec[bi, p][0:1, p * C_W + g * gw:p * C_W + (g + 1) * gw]
            s_ref[bi, p, g] = s * decay + jnp.where(blockdiag, upd, 0.0)
            outs[p][bi, rows[p], lanes[g]] = o


def _gdn_scan(u, w, kd, qd, intra, comp, n_lat, blk):
    _, b, t, _ = u.shape
    nl = n_lat // blk
    nx = (t - n_lat) // blk
    ncb = blk // CHUNK
    ex = np.zeros((LANE, 2 * C_W), np.float32)
    for p in range(2):
        for h in range(C_HEADS):
            ex[GL_LANE + p * C_HEADS + h, p * C_W + h * HEAD_DIM:p * C_W + (h + 1) * HEAD_DIM] = 1.0

    def blk0(j):
        return jnp.where(j < nx, nl + j, j - nx)

    def blk1(j):
        return jnp.where(j < nx, nl + nx - 1 - j, nl - 1 - (j - nx))

    nb = SCAN_BATCH if b % SCAN_BATCH == 0 else 1
    specs = []
    for p, bf in enumerate((blk0, blk1)):
        specs += [pl.BlockSpec((1, nb, blk, C_W), lambda bi, j, p=p, bf=bf: (p, bi, bf(j), 0))] * 5
        specs += [pl.BlockSpec((nb, blk, LANE), lambda bi, j, bf=bf: (bi, bf(j), 0))]
    specs += [pl.BlockSpec((LANE, 2 * C_W), lambda bi, j: (0, 0))]
    out = jax.ShapeDtypeStruct((b, t, C_W), F32)
    gw = 4 * HEAD_DIM
    return pl.pallas_call(
        functools.partial(_gdn_scan_kernel, ncb=ncb, nb=nb),
        grid=(b // nb, nl + nx),
        in_specs=specs,
        out_specs=[pl.BlockSpec((nb, blk, C_W), lambda bi, j: (bi, blk0(j), 0)),
                   pl.BlockSpec((nb, blk, C_W), lambda bi, j: (bi, blk1(j), 0))],
        out_shape=[out, out],
        scratch_shapes=[pltpu.VMEM((nb, 2, C_W // gw, gw, gw), F32)],
        compiler_params=_cparams(("parallel", "arbitrary")),
        name="gdn_scan",
    )(u, w, kd, qd, intra, comp, u, w, kd, qd, intra, comp, jnp.asarray(ex, dtype=BF16))


def _merge_kernel(x_ref, mod_ref, oa_ref, ob_ref, oc0_ref, oc1_ref, gate_ref, wmg_ref, wbr_ref, wout_ref, og_ref,
                  gs_ref, lng_ref, lnb_ref, wr_ref, o_ref, h_ref, aff_ref, *, alpha, n_exp):
    mod = mod_ref[0, 0]
    tm = x_ref.shape[1]
    d = x_ref.shape[2]
    halves = [slice(0, tm // 2), slice(tm // 2, tm)]
    xs = [x_ref[0, r, :] for r in halves]
    hs = [(_ln(x) * (1.0 + mod[1:2]) + mod[0:1]).astype(BF16) for x in xs]
    gates = [_sigmoid(_dot(h, wmg_ref[...])) for h in hs]
    ocs = [oc0_ref[0, r, :] + oc1_ref[0, r, :] for r in halves]
    ocns = [oc * lax.rsqrt(_dot_exact_rhs(oc * oc, gs_ref[...], 2) + LN_EPS) * og_ref[...] * _silu(gate_ref[0, r, :])
            for oc, r in zip(ocs, halves)]
    ms = [g[:, :d] * _dot(oa_ref[0, r, :], wbr_ref[0]) + g[:, d:2 * d] * _dot(ob_ref[0, r, :], wbr_ref[1])
          + g[:, 2 * d:] * _dot(ocn.astype(BF16), wbr_ref[2]) for g, ocn, r in zip(gates, ocns, halves)]
    mixes = [_dot(m.astype(BF16), wout_ref[...]) for m in ms]
    for x, mix, r in zip(xs, mixes, halves):
        y = alpha * x + mod[2:3] * mix
        x1 = _ln(y) * lng_ref[...] + lnb_ref[...]
        o_ref[0, r, :] = x1
        h2 = _ln(x1) * (1.0 + mod[4:5]) + mod[3:4]
        h_ref[0, r, :] = h2.astype(BF16)
        hp = _split_bf16(h2, 3)
        logits = sum(_dot(hp[i], wr_ref[j]) for i, j in ((2, 0), (1, 1), (0, 2), (1, 0), (0, 1), (0, 0)))
        lane = lax.broadcasted_iota(jnp.int32, logits.shape, 1)
        logits = jnp.where(lane < n_exp, logits, NEG_BIG)
        ex = jnp.exp(logits - jnp.max(logits, axis=-1, keepdims=True))
        aff_ref[0, r, :] = ex / jnp.sum(ex, axis=-1, keepdims=True)


def _merge(xa, mods, oa, ob, oc0, oc1, gate_c, wmg, wbr, wout, og, gs, lng, lnb, w_router, n_lat, tm, alpha):
    b, t, d = xa.shape
    nl = n_lat // tm
    n_exp = w_router.shape[-1]
    wr_pad = jnp.pad(w_router, ((0, 0), (0, LANE - n_exp)))
    pieces, rest = [], wr_pad
    for _ in range(3):
        piece = lax.reduce_precision(rest, exponent_bits=8, mantissa_bits=7)
        pieces.append(piece.astype(BF16))
        rest = rest - piece
    wr = jnp.stack(pieces)
    row = lambda bi, i: (bi, i, 0)
    const2 = lambda bi, i: (0, 0)
    return pl.pallas_call(
        functools.partial(_merge_kernel, alpha=alpha, n_exp=n_exp),
        grid=(b, t // tm),
        in_specs=[pl.BlockSpec((1, tm, d), row),
                  pl.BlockSpec((1, 1, N_MOD, d), lambda bi, i: (bi, (i >= nl).astype(jnp.int32), 0, 0)),
                  pl.BlockSpec((1, tm, BRANCH_W), row), pl.BlockSpec((1, tm, BRANCH_W), row),
                  pl.BlockSpec((1, tm, BRANCH_W), row), pl.BlockSpec((1, tm, BRANCH_W), row),
                  pl.BlockSpec((1, tm, BRANCH_W), row),
                  pl.BlockSpec(wmg.shape, const2),
                  pl.BlockSpec(wbr.shape, lambda bi, i: (0, 0, 0)),
                  pl.BlockSpec(wout.shape, const2),
                  pl.BlockSpec(og.shape, const2), pl.BlockSpec(gs.shape, const2),
                  pl.BlockSpec(lng.shape, const2), pl.BlockSpec(lnb.shape, const2),
                  pl.BlockSpec(wr.shape, lambda bi, i: (0, 0, 0))],
        out_specs=[pl.BlockSpec((1, tm, d), row), pl.BlockSpec((1, tm, d), row), pl.BlockSpec((1, tm, LANE), row)],
        out_shape=[jax.ShapeDtypeStruct((b, t, d), F32), jax.ShapeDtypeStruct((b, t, d), BF16),
                   jax.ShapeDtypeStruct((b, t, LANE), F32)],
        compiler_params=_cparams(("parallel", "parallel")),
        name="merge",
    )(xa, mods, oa, ob, oc0, oc1, gate_c, wmg, wbr, wout, og, gs, lng, lnb, wr)


def _ffn_kernel(x_ref, wg_ref, wu_ref, wd_ref, gate_ref, o_ref, acc_ref):
    f = pl.program_id(2)

    @pl.when(f == 0)
    def _():
        acc_ref[...] = jnp.zeros_like(acc_ref)

    x = x_ref[0]
    g = _dot(x, wg_ref[0, 0].astype(BF16))
    u = _dot(x, wu_ref[0, 0].astype(BF16))
    a = (_silu(g) * u).astype(BF16)
    acc_ref[...] += _dot(a, wd_ref[0, 0].astype(BF16))

    @pl.when(f == pl.num_programs(2) - 1)
    def _():
        o_ref[0] = (acc_ref[...] * gate_ref[0]).astype(BF16)


def _expert_ffn(xs, gate, w_gate_up, w_down, layer, tm, tf):
    e, r, d = xs.shape
    f = w_down.shape[2]
    nf = f // tf
    return pl.pallas_call(
        _ffn_kernel,
        grid=(e, r // tm, nf),
        in_specs=[pl.BlockSpec((1, tm, d), lambda ei, i, fi: (ei, i, 0)),
                  pl.BlockSpec((1, 1, d, tf), lambda ei, i, fi: (layer, ei, 0, fi)),
                  pl.BlockSpec((1, 1, d, tf), lambda ei, i, fi: (layer, ei, 0, nf + fi)),
                  pl.BlockSpec((1, 1, tf, d), lambda ei, i, fi: (layer, ei, fi, 0)),
                  pl.BlockSpec((1, tm, 1), lambda ei, i, fi: (ei, i, 0))],
        out_specs=pl.BlockSpec((1, tm, d), lambda ei, i, fi: (ei, i, 0)),
        out_shape=jax.ShapeDtypeStruct((e, r, d), BF16),
        scratch_shapes=[pltpu.VMEM((tm, d), F32)],
        compiler_params=_cparams(("parallel", "parallel", "arbitrary")),
        name="expert_ffn",
    )(xs, w_gate_up, w_gate_up, w_down, gate)


VALID, FIRST, LAST = 1, 2, 4


def _combine_plan(tok_of_pair, n_tok, tm, rwin):
    bsz, n_pair = tok_of_pair.shape
    nt = n_tok // tm
    n_win = n_pair // rwin
    n_item = nt + n_win
    order = jnp.argsort(tok_of_pair, axis=1)
    tok = jnp.take_along_axis(tok_of_pair, order, axis=1)
    bounds = jnp.arange(nt + 1, dtype=jnp.int32) * tm
    off = jnp.sum(tok[:, None, :] < bounds[None, :, None], axis=-1, dtype=jnp.int32)
    first = jnp.minimum(off[:, :-1] // rwin, n_win - 1)
    last = jnp.minimum(jnp.maximum(off[:, 1:] - 1, off[:, :-1]) // rwin, n_win - 1)
    n_items = last - first + 1
    cum = jnp.cumsum(n_items, axis=1)
    k = jnp.arange(n_item, dtype=jnp.int32)
    tile = jnp.sum(cum[:, None, :] <= k[None, :, None], axis=-1, dtype=jnp.int32)
    valid = tile < nt
    tile = jnp.minimum(tile, nt - 1)
    end = jnp.take_along_axis(cum, tile, axis=1)
    start = end - jnp.take_along_axis(n_items, tile, axis=1)
    win = jnp.where(valid, jnp.take_along_axis(first, tile, axis=1) + (k[None] - start),
                    jnp.take_along_axis(last, tile, axis=1))
    flags = (valid * VALID + (valid & (k[None] == start)) * FIRST + (valid & (k[None] == end - 1)) * LAST)
    return order, tok, tile, win.astype(jnp.int32), flags.astype(jnp.int32)


def _combine_kernel(tile_ref, win_ref, flag_ref, y_ref, tok_ref, x_ref, mod_ref, lng_ref, lnb_ref, o_ref, acc_ref, *,
                    alpha, tm):
    b = pl.program_id(0)
    k = pl.program_id(1)
    flags = flag_ref[b, k]

    @pl.when((flags & FIRST) != 0)
    def _():
        acc_ref[...] = jnp.zeros_like(acc_ref)

    @pl.when((flags & VALID) != 0)
    def _():
        tok0 = tile_ref[b, k] * tm
        rows = lax.broadcasted_iota(jnp.int32, (tm, tok_ref.shape[-1]), 0) + tok0
        pick = jnp.where(rows == tok_ref[0, 0], 1.0, 0.0).astype(BF16)
        acc_ref[...] += _dot(pick, y_ref[0])

    @pl.when((flags & LAST) != 0)
    def _():
        mod = mod_ref[0, 0]
        y = alpha * x_ref[0] + mod[5:6] * acc_ref[...]
        o_ref[0] = _ln(y) * lng_ref[...] + lnb_ref[...]


def _combine(y_sorted, tok_sorted, tile, win, flags, xa, mods, lng, lnb, n_lat, tm, rwin, alpha):
    b, t, d = xa.shape
    nl = n_lat // tm
    n_item = tile.shape[1]
    grid_spec = pltpu.PrefetchScalarGridSpec(
        num_scalar_prefetch=3,
        grid=(b, n_item),
        in_specs=[pl.BlockSpec((1, rwin, d), lambda bi, k, tl, wn, fl: (bi, wn[bi, k], 0)),
                  pl.BlockSpec((1, 1, 1, rwin), lambda bi, k, tl, wn, fl: (bi, wn[bi, k], 0, 0)),
                  pl.BlockSpec((1, tm, d), lambda bi, k, tl, wn, fl: (bi, tl[bi, k], 0)),
                  pl.BlockSpec((1, 1, N_MOD, d),
                               lambda bi, k, tl, wn, fl: (bi, (tl[bi, k] >= nl).astype(jnp.int32), 0, 0)),
                  pl.BlockSpec(lng.shape, lambda bi, k, tl, wn, fl: (0, 0)),
                  pl.BlockSpec(lnb.shape, lambda bi, k, tl, wn, fl: (0, 0))],
        out_specs=pl.BlockSpec((1, tm, d), lambda bi, k, tl, wn, fl: (bi, tl[bi, k], 0)),
        scratch_shapes=[pltpu.VMEM((tm, d), F32)],
    )
    return pl.pallas_call(
        functools.partial(_combine_kernel, alpha=alpha, tm=tm),
        grid_spec=grid_spec,
        out_shape=jax.ShapeDtypeStruct((b, t, d), F32),
        compiler_params=_cparams(("parallel", "arbitrary")),
        name="moe_combine",
    )(tile, win, flags, y_sorted, tok_sorted.reshape(b, -1, 1, rwin), xa, mods, lng, lnb)


def _moe(xa, h, aff, mods, n_exp, w_gate_up, w_down, lng, lnb, layer, n_lat, tm, alpha):
    b, t, d = xa.shape
    m_ctx = t - n_lat
    aff = jnp.swapaxes(aff[:, :, :n_exp], 1, 2)
    cap_l = CAPACITY * n_lat // n_exp
    cap_x = CAPACITY * m_ctx // n_exp
    gate_l, idx_l = lax.top_k(aff[:, :, :n_lat], cap_l)
    gate_x, idx_x = lax.top_k(aff[:, :, n_lat:], cap_x)
    idx = jnp.concatenate([idx_l, idx_x + n_lat], axis=-1)
    gate = jnp.concatenate([gate_l, gate_x], axis=-1)
    cap = cap_l + cap_x
    idx_e = jnp.swapaxes(idx, 0, 1)
    bidx = jnp.arange(b)[None, :, None]
    xs = h[bidx, idx_e].reshape(n_exp, b * cap, d)
    gate_e = jnp.swapaxes(gate, 0, 1).reshape(n_exp, b * cap, 1)
    rows = b * cap
    tmr = rows
    for cand in (1088, 1024, 512, 256, 128, 64, 32, 16):
        if rows % cand == 0:
            tmr = cand
            break
    tf = min(512, w_down.shape[2])
    y = _expert_ffn(xs, gate_e, w_gate_up, w_down, layer, tmr, tf)
    n_pair = n_exp * cap
    rwin = next(r for r in (512, 256, 128, 64, 32, 16, 8) if n_pair % r == 0)
    order, tok, tile, win, flags = _combine_plan(idx.reshape(b, n_pair), t, tm, rwin)
    y_sorted = y[order // cap, jnp.arange(b)[:, None] * cap + order % cap]
    return _combine(y_sorted, tok, tile, win, flags, xa, mods, lng, lnb, n_lat, tm, rwin, alpha)


def _rope_tables(n_lat, m_ctx):
    t = np.arange(n_lat)
    rows = (t // GRID_W).astype(np.float32)
    cols = (t % GRID_W).astype(np.float32)
    n_freq = HEAD_DIM // 4
    inv_freq = jnp.asarray(ROPE_THETA, F32) ** (-jnp.arange(n_freq, dtype=F32) / n_freq)
    ang_r = jnp.asarray(rows)[:, None] * inv_freq
    ang_c = jnp.asarray(cols)[:, None] * inv_freq
    cr, sr, cc, sc = jnp.cos(ang_r), jnp.sin(ang_r), jnp.cos(ang_c), jnp.sin(ang_c)
    cos = jnp.concatenate([cr, cr, cc, cc], axis=-1)
    sin = jnp.concatenate([-sr, sr, -sc, sc], axis=-1)
    cos = jnp.concatenate([cos, jnp.ones((m_ctx, HEAD_DIM), F32)], axis=0)
    sin = jnp.concatenate([sin, jnp.zeros((m_ctx, HEAD_DIM), F32)], axis=0)
    return jnp.tile(cos, (1, A_HEADS)), jnp.tile(sin, (1, A_HEADS))


def kernel(x, c, ctx, c_ctx, w_mod, b_mod, w_in, qk_gain, rpb, conv_w, a_log, dt_bias, o_gain, w_branch, w_out,
           ln1_g, ln1_b, w_router, w_gate_up, w_down, ln2_g, ln2_b):
    b, n_lat, d = x.shape
    m_ctx = ctx.shape[1]
    depth = w_mod.shape[0]
    alpha = (2 * depth) ** 0.25
    tm = math.gcd(256, math.gcd(n_lat, m_ctx))
    blk = tm

    xa = jnp.concatenate([x, ctx], axis=1)
    n_rows = -(-(b + 1) // SUBLANE) * SUBLANE
    cc = jnp.zeros((n_rows, d), F32).at[:b].set(c).at[b].set(c_ctx)
    mod_all = _modulation(cc, w_mod, b_mod).reshape(depth, n_rows, N_MOD, d)
    cos, sin = _rope_tables(n_lat, m_ctx)
    gs_mean = _group_sum_matrix(A_Q_W, 1.0 / HEAD_DIM)
    gs_sum = _group_sum_matrix(C_W, 1.0)
    na_geoms, na_table = _na_classes(n_lat, n_lat + m_ctx)

    offs = np.cumsum((0, A_Q_W, A_KV_W, A_KV_W, B_W, B_W, B_W, 3 * C_W, C_W, 2 * C_HEADS, 2 * C_HEADS, N_BRANCH * d))
    for layer in range(depth):
        ml = mod_all[layer]
        mods = jnp.stack([ml[:b], jnp.broadcast_to(ml[b][None], (b, N_MOD, d))], axis=1)
        wl = w_in[layer]
        wa = wl[:, offs[0]:offs[3]].astype(BF16)
        wb = wl[:, offs[3]:offs[6]].astype(BF16)
        wc = wl[:, offs[6]:offs[8]].astype(BF16)
        wab = jnp.pad(wl[:, offs[8]:offs[10]], ((0, 0), (0, LANE - 4 * C_HEADS))).astype(BF16)
        wmg = wl[:, offs[10]:offs[11]].astype(BF16)
        gq = jnp.tile(qk_gain[layer, 0], A_HEADS)[None]
        gk = jnp.tile(qk_gain[layer, 1], A_KV_HEADS)[None]
        qa, ka, va, qb, kb, vb, qkvc, gate_c, ab = _in_projection(
            xa, mods, wa, wb, wc, wab, gq, gk, cos, sin, gs_mean, n_lat, tm)
        oa = _gqa(qa, ka, va, n_lat, min(GQA_TQ, tm), 2 if (n_lat + m_ctx) % (2 * LANE) == 0 else 1)
        ob = _neighborhood(qb, kb, vb, _na_bias_slabs(rpb[layer], na_geoms), na_table, n_lat)
        qc, kc, vc, comp = _gdn_prep(qkvc, ab, conv_w[layer], a_log[layer], dt_bias[layer], gs_sum, n_lat, tm)
        u, w, kd, qd, intra = _gdn_local(qc, kc, vc, comp, blk)
        oc0, oc1 = _gdn_scan(u, w, kd, qd, intra, comp, n_lat, blk)
        og = jnp.tile(o_gain[layer], C_HEADS)[None]
        xa, h, aff = _merge(xa, mods, oa, ob, oc0, oc1, gate_c, wmg, w_branch[layer].astype(BF16),
                            w_out[layer].astype(BF16), og, gs_mean, ln1_g[layer][None], ln1_b[layer][None],
                            w_router[layer], n_lat, tm, alpha)
        xa = _moe(xa, h, aff, mods, w_router.shape[-1], w_gate_up, w_down, ln2_g[layer][None], ln2_b[layer][None],
                  layer, n_lat, tm, alpha)
    return xa[:, :n_lat]
```

```python
import functools
import math

import numpy as np
import jax
import jax.numpy as jnp
from jax import lax
from jax.experimental import pallas as pl
from jax.experimental.pallas import tpu as pltpu

F32 = jnp.float32
BF16 = jnp.bfloat16
HIGHEST = lax.Precision.HIGHEST

GRID_W = 64
HEAD_DIM = 64
A_HEADS = 8
A_KV_HEADS = 2
B_HEADS = 8
C_HEADS = 8
NA_ROWS = 8
NA_COLS = 16
CONV_K = 4
CHUNK = 64
N_BRANCH = 3
BRANCH_W = 512
CAPACITY = 2
N_MOD = 6
LN_EPS = 1e-6
ROPE_THETA = 10000.0
NEG_BIG = -1e30
LOG2E = 1.4426950408889634

A_Q_W = A_HEADS * HEAD_DIM
A_KV_W = A_KV_HEADS * HEAD_DIM
B_W = B_HEADS * HEAD_DIM
C_W = C_HEADS * HEAD_DIM
LANE = 128
SUBLANE = 8
V7X_VMEM_BYTES = 64 * 1024 * 1024
VMEM_LIMIT = V7X_VMEM_BYTES * 7 // 8
GQA_TQ = 128
LOCAL_PAIRS = 4
SCAN_BATCH = 4
VT_ROWS = HEAD_DIM + 16
NA_WIN = NA_ROWS + 2


def _cparams(sem):
    return pltpu.CompilerParams(dimension_semantics=sem, vmem_limit_bytes=VMEM_LIMIT)


def _dot(a, b):
    return jnp.dot(a, b, preferred_element_type=F32)


def _dot_hi(a, b):
    return jnp.dot(a, b, precision=HIGHEST, preferred_element_type=F32)


def _dot_nt(a, b):
    return lax.dot_general(a, b, (((1,), (1,)), ((), ())), preferred_element_type=F32)


def _dot_tn(a, b):
    return lax.dot_general(a, b, (((0,), (0,)), ((), ())), preferred_element_type=F32)


def _split_bf16(x, pieces):
    out = []
    for _ in range(pieces - 1):
        h = x.astype(BF16)
        out.append(h)
        x = x - h.astype(F32)
    out.append(x.astype(BF16))
    return out


def _dot_exact_rhs(a, b, pieces=3):
    return sum(_dot(p, b) for p in _split_bf16(a, pieces))


def _dot_exact_lhs(a, b, pieces=3):
    return sum(_dot(a, p) for p in _split_bf16(b, pieces))


def _ln(x):
    mu = jnp.mean(x, axis=-1, keepdims=True)
    xc = x - mu
    var = jnp.mean(xc * xc, axis=-1, keepdims=True)
    return xc * lax.rsqrt(var + LN_EPS)


def _sigmoid(x):
    return 1.0 / (1.0 + jnp.exp(-x))


def _silu(x):
    return x * _sigmoid(x)


def _group_sum_matrix(width, scale):
    g = np.arange(width) // HEAD_DIM
    return jnp.asarray((g[:, None] == g[None, :]).astype(np.float32) * scale, dtype=BF16)


def _mod_kernel(c_ref, w_ref, b_ref, o_ref):
    c = c_ref[...]
    o_ref[0] = _dot_hi(_silu(c), w_ref[0]) + b_ref[0]


def _modulation(cc, w_mod, b_mod):
    depth, d, dm = w_mod.shape
    r = cc.shape[0]
    tn = min(dm, 1536)
    return pl.pallas_call(
        _mod_kernel,
        grid=(depth, dm // tn),
        in_specs=[pl.BlockSpec((r, d), lambda l, j: (0, 0)),
                  pl.BlockSpec((1, d, tn), lambda l, j: (l, 0, j)),
                  pl.BlockSpec((1, 1, tn), lambda l, j: (l, 0, j))],
        out_specs=pl.BlockSpec((1, r, tn), lambda l, j: (l, 0, j)),
        out_shape=jax.ShapeDtypeStruct((depth, r, dm), F32),
        compiler_params=_cparams(("parallel", "parallel")),
        name="modulation",
    )(cc, w_mod, b_mod.reshape(depth, 1, dm))


def _rope_swap(y):
    w = y.shape[-1]
    lane = lax.broadcasted_iota(jnp.int32, y.shape, 1)
    up = pltpu.roll(y, w - 16, 1)
    down = pltpu.roll(y, 16, 1)
    return jnp.where((lane % 32) < 16, up, down)


def _inproj_kernel(x_ref, mod_ref, wa_ref, wb_ref, wc_ref, wab_ref, gq_ref, gk_ref, cos_ref, sin_ref,
                   gs_ref, qa_ref, ka_ref, va_ref, qb_ref, kb_ref, vb_ref, qkvc_ref, gatec_ref, ab_ref):
    x = x_ref[0]
    mod = mod_ref[0, 0]
    h = (_ln(x) * (1.0 + mod[1:2]) + mod[0:1]).astype(BF16)
    scale = HEAD_DIM ** -0.5 * LOG2E

    za = _dot(h, wa_ref[...])
    q = za[:, :A_Q_W]
    k = za[:, A_Q_W:A_Q_W + A_KV_W]
    v = za[:, A_Q_W + A_KV_W:]
    cos = cos_ref[...]
    sin = sin_ref[...]
    gs = gs_ref[...]
    qn = q * lax.rsqrt(_dot_exact_rhs(q * q, gs, 2) + LN_EPS) * gq_ref[...]
    qn = qn * cos + _rope_swap(qn) * sin
    qa_ref[0] = (qn * scale).astype(BF16)
    kn = k * lax.rsqrt(_dot_exact_rhs(k * k, gs[:A_KV_W, :A_KV_W], 2) + LN_EPS) * gk_ref[...]
    kn = kn * cos[:, :A_KV_W] + _rope_swap(kn) * sin[:, :A_KV_W]
    ones = jnp.ones((VT_ROWS - HEAD_DIM, x.shape[0]), F32)
    vt = v.T
    for j in range(A_KV_HEADS):
        ka_ref[0, j] = kn[:, j * HEAD_DIM:(j + 1) * HEAD_DIM].astype(BF16)
        va_ref[0, j] = jnp.concatenate([vt[j * HEAD_DIM:(j + 1) * HEAD_DIM], ones], axis=0).astype(BF16)

    zb = _dot(h, wb_ref[...])
    qb_ref[0] = (zb[:, :B_W] * scale).astype(BF16)
    kbv = zb[:, B_W:2 * B_W]
    vt = zb[:, 2 * B_W:].T
    for j in range(B_HEADS):
        kb_ref[0, j] = kbv[:, j * HEAD_DIM:(j + 1) * HEAD_DIM].astype(BF16)
        vb_ref[0, j] = jnp.concatenate([vt[j * HEAD_DIM:(j + 1) * HEAD_DIM], ones], axis=0).astype(BF16)

    zc = _dot(h, wc_ref[...])
    qkvc_ref[0] = zc[:, :3 * C_W]
    gatec_ref[0] = zc[:, 3 * C_W:]
    ab_ref[0] = _dot(h, wab_ref[...])


def _in_projection(xa, mods, wa, wb, wc, wab, gq, gk, cos, sin, gs, n_lat, tm):
    b, t, d = xa.shape
    nl = n_lat // tm
    row = lambda bi, i: (bi, i, 0)
    const = lambda bi, i: (0, 0)
    outs = [
        (jax.ShapeDtypeStruct((b, t, A_Q_W), BF16), pl.BlockSpec((1, tm, A_Q_W), row)),
        (jax.ShapeDtypeStruct((b, A_KV_HEADS, t, HEAD_DIM), BF16),
         pl.BlockSpec((1, A_KV_HEADS, tm, HEAD_DIM), lambda bi, i: (bi, 0, i, 0))),
        (jax.ShapeDtypeStruct((b, A_KV_HEADS, VT_ROWS, t), BF16),
         pl.BlockSpec((1, A_KV_HEADS, VT_ROWS, tm), lambda bi, i: (bi, 0, 0, i))),
        (jax.ShapeDtypeStruct((b, t, B_W), BF16), pl.BlockSpec((1, tm, B_W), row)),
        (jax.ShapeDtypeStruct((b, B_HEADS, t, HEAD_DIM), BF16),
         pl.BlockSpec((1, B_HEADS, tm, HEAD_DIM), lambda bi, i: (bi, 0, i, 0))),
        (jax.ShapeDtypeStruct((b, B_HEADS, VT_ROWS, t), BF16),
         pl.BlockSpec((1, B_HEADS, VT_ROWS, tm), lambda bi, i: (bi, 0, 0, i))),
        (jax.ShapeDtypeStruct((b, t, 3 * C_W), F32), pl.BlockSpec((1, tm, 3 * C_W), row)),
        (jax.ShapeDtypeStruct((b, t, C_W), F32), pl.BlockSpec((1, tm, C_W), row)),
        (jax.ShapeDtypeStruct((b, t, LANE), F32), pl.BlockSpec((1, tm, LANE), row)),
    ]
    return pl.pallas_call(
        _inproj_kernel,
        grid=(b, t // tm),
        in_specs=[pl.BlockSpec((1, tm, d), row),
                  pl.BlockSpec((1, 1, N_MOD, d), lambda bi, i: (bi, (i >= nl).astype(jnp.int32), 0, 0)),
                  pl.BlockSpec(wa.shape, const), pl.BlockSpec(wb.shape, const),
                  pl.BlockSpec(wc.shape, const), pl.BlockSpec(wab.shape, const),
                  pl.BlockSpec(gq.shape, const), pl.BlockSpec(gk.shape, const),
                  pl.BlockSpec((tm, A_Q_W), lambda bi, i: (i, 0)),
                  pl.BlockSpec((tm, A_Q_W), lambda bi, i: (i, 0)),
                  pl.BlockSpec(gs.shape, const)],
        out_specs=[o[1] for o in outs],
        out_shape=[o[0] for o in outs],
        compiler_params=_cparams(("parallel", "parallel")),
        name="in_projection",
    )(xa, mods, wa, wb, wc, wab, gq, gk, cos, sin, gs)


def _gqa_kernel(q_ref, k_ref, v_ref, o_ref, *, tq, ck, n_lat, n_tot):
    i = pl.program_id(1)
    g = A_HEADS // A_KV_HEADS

    def attend(key_lo, key_hi, n_split):
        qs = [jnp.concatenate([q_ref[0, :, (kh * g + j) * HEAD_DIM:(kh * g + j + 1) * HEAD_DIM] for j in range(g)],
                              axis=0) for kh in range(A_KV_HEADS)]
        step = (key_hi - key_lo) // n_split
        units = [(kh, key_lo + r * step, key_lo + (r + 1) * step) for r in range(n_split) for kh in range(A_KV_HEADS)]
        ss, ms, ps, oes = {}, {}, {}, {}
        for n in range(len(units) + 2):
            if n < len(units):
                kh, lo, hi = units[n]
                ss[n] = _dot_nt(k_ref[0, kh, lo:hi, :], qs[kh])
            if 0 <= n - 1 < len(units):
                ms[n - 1] = jnp.max(ss[n - 1], axis=0, keepdims=True).astype(BF16).astype(F32)
                ps[n - 1] = jnp.exp2(ss.pop(n - 1).astype(BF16) - ms[n - 1].astype(BF16))
            if 0 <= n - 2 < len(units):
                kh, lo, hi = units[n - 2]
                oes[n - 2] = _dot(v_ref[0, kh, :, lo:hi], ps.pop(n - 2))
        for kh in range(A_KV_HEADS):
            mine = [n for n, u in enumerate(units) if u[0] == kh]
            m = functools.reduce(jnp.maximum, [ms[n] for n in mine])
            oe = sum(oes[n] * jnp.exp2(ms[n] - m) for n in mine)
            o = (oe[:HEAD_DIM] / oe[HEAD_DIM:HEAD_DIM + 1]).T.astype(BF16)
            for j in range(g):
                o_ref[0, :, (kh * g + j) * HEAD_DIM:(kh * g + j + 1) * HEAD_DIM] = o[j * tq:(j + 1) * tq]

    @pl.when(i < n_lat // tq)
    def _():
        attend(0, n_tot, ck)

    @pl.when(i >= n_lat // tq)
    def _():
        attend(n_lat, n_tot, 1)


def _gqa(qa, ka, va, n_lat, tq, ck):
    b, t, _ = qa.shape
    return pl.pallas_call(
        functools.partial(_gqa_kernel, tq=tq, ck=ck, n_lat=n_lat, n_tot=t),
        grid=(b, t // tq),
        in_specs=[pl.BlockSpec((1, tq, A_Q_W), lambda bi, i: (bi, i, 0)),
                  pl.BlockSpec((1, A_KV_HEADS, t, HEAD_DIM), lambda bi, i: (bi, 0, 0, 0)),
                  pl.BlockSpec((1, A_KV_HEADS, VT_ROWS, t), lambda bi, i: (bi, 0, 0, 0))],
        out_specs=pl.BlockSpec((1, tq, A_Q_W), lambda bi, i: (bi, i, 0)),
        out_shape=jax.ShapeDtypeStruct((b, t, A_Q_W), BF16),
        compiler_params=_cparams(("parallel", "parallel")),
        name="gqa",
    )(qa, ka, va)


def _na_window_start(r0, rows):
    return np.clip(r0 - NA_ROWS // 2, 0, rows - NA_ROWS - 1) // 2 * 2


def _na_classes(n_lat, n_tot):
    rows = n_lat // GRID_W
    assert rows >= NA_WIN and rows % 2 == 0
    geoms, table = [], []
    for r0 in range(0, rows, 2):
        geom = (int(_na_window_start(r0, rows)) - r0,) + tuple(
            int(np.clip(r0 + qr - NA_ROWS // 2, 0, rows - NA_ROWS)) - r0 for qr in range(2))
        if geom not in geoms:
            geoms.append(geom)
        table.append(geoms.index(geom))
    table += [len(geoms)] * ((n_tot - n_lat) // (2 * GRID_W))
    return geoms, np.asarray(table, np.int32)


def _na_bias_slabs(rpb, geoms):
    n_head = rpb.shape[0]
    cj = np.arange(GRID_W)[:, None]
    c = np.arange(GRID_W)[None, :]
    col_start = np.clip(c - NA_COLS // 2, 0, GRID_W - NA_COLS)
    col_ok = (cj >= col_start) & (cj < col_start + NA_COLS)
    col_off = np.clip(cj - c + (NA_COLS - 1), 0, 2 * NA_COLS - 2)
    spread = np.zeros((2 * NA_COLS - 1, GRID_W * GRID_W), np.float32)
    spread[col_off.reshape(-1), np.arange(GRID_W * GRID_W)] = 1.0
    kr = np.arange(NA_WIN)[:, None]
    qr = np.arange(2)[None, :]
    pick = np.zeros((len(geoms), NA_WIN, 2, 2 * NA_ROWS - 1), np.float32)
    ok = np.zeros((len(geoms), 1, NA_WIN, 2, GRID_W, GRID_W), bool)
    for n, (a, b0, b1) in enumerate(geoms):
        first = np.where(qr == 0, b0, b1)
        row_ok = (a + kr >= first) & (a + kr < first + NA_ROWS)
        row_off = np.clip(a + kr - qr + (NA_ROWS - 1), 0, 2 * NA_ROWS - 2)
        pick[n, kr, qr, row_off] = 1.0
        ok[n, 0] = row_ok[:, :, None, None] & col_ok[None, None]
    rows = jnp.einsum("nkqo,hoc->nhkqc", jnp.asarray(pick), rpb * LOG2E, precision=HIGHEST)
    full = jnp.dot(rows, jnp.asarray(spread), precision=HIGHEST)
    full = full.reshape(len(geoms), n_head, NA_WIN, 2, GRID_W, GRID_W)
    masked = jnp.where(jnp.asarray(ok), full, NEG_BIG)
    slabs = jnp.concatenate([masked[:, :, :, 0], masked[:, :, :, 1]], axis=-1)
    slabs = slabs.reshape(len(geoms), n_head, NA_WIN * GRID_W, 2 * GRID_W)
    return jnp.concatenate([slabs, jnp.full_like(slabs[:1], NEG_BIG)], axis=0).astype(F32)


def _na_kernel(cls_ref, q_ref, k_ref, vt_ref, *rest, n_lat, n_tot, pps):
    del cls_ref
    bias_refs, o_ref = rest[:pps], rest[pps]
    rows = n_lat // GRID_W
    tq = 2 * GRID_W
    hs = [slice(h * HEAD_DIM, (h + 1) * HEAD_DIM) for h in range(B_HEADS)]
    units = []
    for j in range(pps):
        r0 = 2 * (pps * pl.program_id(1) + j)
        start = jnp.clip(r0 - NA_ROWS // 2, 0, rows - NA_ROWS - 1) // 2 * 2
        nb = pl.ds(pl.multiple_of(start * GRID_W, 2 * GRID_W), NA_WIN * GRID_W)
        units += [(j, h, nb) for h in range(B_HEADS)]
    qs = [q_ref[0, j * tq:(j + 1) * tq, hs[h]] for j, h, _ in units]
    s_nb = [_dot_nt(k_ref[0, h, nb, :], q) + bias_refs[j][0, h] for (j, h, nb), q in zip(units, qs)]
    s_cx = [_dot_nt(k_ref[0, h, n_lat:n_tot, :], q) for (j, h, nb), q in zip(units, qs)]
    ms = [jnp.maximum(jnp.max(a, axis=0, keepdims=True), jnp.max(c, axis=0, keepdims=True))
          for a, c in zip(s_nb, s_cx)]
    p_nb = [jnp.exp2((a - m).astype(BF16)) for a, m in zip(s_nb, ms)]
    p_cx = [jnp.exp2((c - m).astype(BF16)) for c, m in zip(s_cx, ms)]
    for (j, h, nb), pn, pc in zip(units, p_nb, p_cx):
        oe = _dot(vt_ref[0, h, :, nb], pn) + _dot(vt_ref[0, h, :, n_lat:n_tot], pc)
        o_ref[0, j * tq:(j + 1) * tq, hs[h]] = (oe[:HEAD_DIM] / oe[HEAD_DIM:HEAD_DIM + 1]).T.astype(BF16)


def _neighborhood(qb, kb, vbt, slabs, table, n_lat):
    b, t, _ = qb.shape
    tq = 2 * GRID_W
    pps = 2 if (t // tq) % 2 == 0 else 1
    slab_specs = [pl.BlockSpec((1,) + slabs.shape[1:], lambda bi, pi, cls, j=j: (cls[pps * pi + j], 0, 0, 0))
                  for j in range(pps)]
    grid_spec = pltpu.PrefetchScalarGridSpec(
        num_scalar_prefetch=1,
        grid=(b, t // (pps * tq)),
        in_specs=[pl.BlockSpec((1, pps * tq, B_W), lambda bi, pi, cls: (bi, pi, 0)),
                  pl.BlockSpec((1, B_HEADS, t, HEAD_DIM), lambda bi, pi, cls: (bi, 0, 0, 0)),
                  pl.BlockSpec((1, B_HEADS, VT_ROWS, t), lambda bi, pi, cls: (bi, 0, 0, 0))] + slab_specs,
        out_specs=pl.BlockSpec((1, pps * tq, B_W), lambda bi, pi, cls: (bi, pi, 0)),
    )
    return pl.pallas_call(
        functools.partial(_na_kernel, n_lat=n_lat, n_tot=t, pps=pps),
        grid_spec=grid_spec,
        out_shape=jax.ShapeDtypeStruct((b, t, B_W), BF16),
        compiler_params=_cparams(("parallel", "parallel")),
        name="neighborhood",
    )(jnp.asarray(table), qb, kb, vbt, *([slabs] * pps))


GC_LANE, GL_LANE, BETA_LANE = 0, 2 * C_HEADS, 4 * C_HEADS


def _gdn_prep_kernel(x_ref, prev_ref, next_ref, ab_ref, cw_ref, alog_ref, dtb_ref, gs_ref,
                     q_ref, k_ref, v_ref, comp_ref, *, tm, n_lat, n_tot):
    i = pl.program_id(1)
    nl = n_lat // tm
    nt = n_tot // tm
    first = jnp.logical_or(i == 0, i == nl)
    last = jnp.logical_or(i == nl - 1, i == nt - 1)
    x = x_ref[0]
    pv = jnp.where(first, 0.0, prev_ref[0])
    nx = jnp.where(last, 0.0, next_ref[0])
    row = lax.broadcasted_iota(jnp.int32, (tm, 1), 0)
    prev1 = pv[SUBLANE - 1:SUBLANE]
    prev2 = pv[SUBLANE - 2:SUBLANE - 1]
    xm1 = jnp.where(row == 0, prev1, pltpu.roll(x, 1, 0))
    xm2 = jnp.where(row == 0, prev2, jnp.where(row == 1, prev1, pltpu.roll(x, 2, 0)))
    xp1 = jnp.where(row == tm - 1, nx[0:1], pltpu.roll(x, tm - 1, 0))
    cw = cw_ref[...]
    y = _silu(xm2 * cw[0:1] + xm1 * cw[1:2] + x * cw[2:3] + xp1 * cw[3:4])
    q = y[:, :C_W]
    k = y[:, C_W:2 * C_W]
    gs = gs_ref[...]
    q_ref[0] = q * lax.rsqrt(_dot_exact_rhs(q * q, gs, 2) + LN_EPS) * (HEAD_DIM ** -0.5)
    k_ref[0] = k * lax.rsqrt(_dot_exact_rhs(k * k, gs, 2) + LN_EPS)
    v_ref[0] = y[:, 2 * C_W:]
    ab = ab_ref[0]
    z = ab + dtb_ref[...]
    softplus = jnp.maximum(z, 0.0) + jnp.log(1.0 + jnp.exp(-jnp.abs(z)))
    lane = lax.broadcasted_iota(jnp.int32, ab.shape, 1)
    gb = jnp.where(lane < 2 * C_HEADS, -jnp.exp(alog_ref[...]) * softplus, _sigmoid(ab))
    ii = lax.broadcasted_iota(jnp.int32, (tm, tm), 0)
    jj = lax.broadcasted_iota(jnp.int32, (tm, tm), 1)
    same = (ii // CHUNK) == (jj // CHUNK)
    gb3 = _split_bf16(gb, 3)

    def masked_sum(mask):
        mb = jnp.where(mask, 1.0, 0.0).astype(BF16)
        return sum(_dot(mb, piece) for piece in gb3)

    prefix = masked_sum(jnp.logical_and(same, jj <= ii))
    suffix = masked_sum(jnp.logical_and(same, jj >= ii))
    total = masked_sum(same)
    comp = jnp.where(lane < C_HEADS, prefix,
                     jnp.where(lane < GL_LANE, suffix,
                               jnp.where(lane < BETA_LANE, pltpu.roll(total, GL_LANE, 1),
                                         jnp.where(lane < BETA_LANE + 2 * C_HEADS, pltpu.roll(gb, 2 * C_HEADS, 1),
                                                   0.0))))
    comp_ref[0] = comp


def _gdn_prep(qkvc, ab, conv_w, a_log, dt_bias, gs, n_lat, tm):
    b, t, _ = qkvc.shape
    c3 = 3 * C_W
    alog_row = jnp.zeros((1, LANE), F32).at[0, :2 * C_HEADS].set(a_log.reshape(-1))
    dtb_row = jnp.zeros((1, LANE), F32).at[0, :2 * C_HEADS].set(dt_bias.reshape(-1))
    row = lambda bi, i: (bi, i, 0)
    const2 = lambda bi, i: (0, 0)
    hb = tm // SUBLANE
    last8 = t // SUBLANE - 1
    tok = jax.ShapeDtypeStruct((b, t, C_W), F32)
    return pl.pallas_call(
        functools.partial(_gdn_prep_kernel, tm=tm, n_lat=n_lat, n_tot=t),
        grid=(b, t // tm),
        in_specs=[pl.BlockSpec((1, tm, c3), row),
                  pl.BlockSpec((1, SUBLANE, c3), lambda bi, i: (bi, jnp.maximum(i * hb - 1, 0), 0)),
                  pl.BlockSpec((1, SUBLANE, c3), lambda bi, i: (bi, jnp.minimum((i + 1) * hb, last8), 0)),
                  pl.BlockSpec((1, tm, LANE), row),
                  pl.BlockSpec((CONV_K, c3), const2),
                  pl.BlockSpec((1, LANE), const2), pl.BlockSpec((1, LANE), const2),
                  pl.BlockSpec(gs.shape, const2)],
        out_specs=[pl.BlockSpec((1, tm, C_W), row)] * 3 + [pl.BlockSpec((1, tm, LANE), row)],
        out_shape=[tok, tok, tok, jax.ShapeDtypeStruct((b, t, LANE), F32)],
        compiler_params=_cparams(("parallel", "parallel")),
        name="gdn_prep",
    )(qkvc, qkvc, qkvc, ab, conv_w, alog_row, dtb_row, gs)


def _half_block_rows(a, h, upper):
    return jnp.concatenate([a[s:s + h] for s in range(0 if upper else h, a.shape[0], 2 * h)], axis=0)


def _with_half_block_rows(full, part, h, upper):
    pieces = []
    for blk in range(full.shape[0] // h):
        if (blk % 2 == 0) == upper:
            pieces.append(part[(blk // 2) * h:(blk // 2 + 1) * h])
        else:
            pieces.append(full[blk * h:(blk + 1) * h])
    return jnp.concatenate(pieces, axis=0)


def _unit_tri_inverses(lms, uppers, eye, same_block):
    def mm(a, b_):
        return _dot(a.astype(BF16), b_.astype(BF16))

    lds = [jnp.where(same_block[SUBLANE], lm, 0.0) for lm in lms]
    xs = [eye - ld for ld in lds]
    lds = [ld.astype(BF16) for ld in lds]
    pws = [_dot(ld, ld).astype(BF16) for ld in lds]
    xs = [x + mm(x, pw) for x, pw in zip(xs, pws)]
    pws = [_dot(pw, pw).astype(BF16) for pw in pws]
    xs = [x + mm(x, pw) for x, pw in zip(xs, pws)]
    h = SUBLANE
    while h < CHUNK:
        level = jnp.logical_not(same_block[h]) if 2 * h == CHUNK else jnp.logical_and(
            same_block[2 * h], jnp.logical_not(same_block[h]))
        xbs = [x.astype(BF16) for x in xs]
        ls = [_half_block_rows(jnp.where(level, lm, 0.0), h, up).astype(BF16) for lm, up in zip(lms, uppers)]
        ys = [_dot(l, xb) for l, xb in zip(ls, xbs)]
        zs = [_with_half_block_rows(jnp.zeros_like(x), y, h, up).astype(BF16) for x, y, up in zip(xs, ys, uppers)]
        xrs = [_half_block_rows(x, h, up) for x, up in zip(xs, uppers)]
        upd = [_dot(xr.astype(BF16), z) for xr, z in zip(xrs, zs)]
        xs = [_with_half_block_rows(x, xr - u, h, up) for x, xr, u, up in zip(xs, xrs, upd, uppers)]
        h *= 2
    return xs


def _gdn_local_kernel(q_ref, k_ref, v_ref, comp_ref, ex_ref, u_ref, w_ref, kd_ref, qd_ref, in_ref):
    n = q_ref.shape[1]
    hp = 2 * HEAD_DIM
    n_pr = q_ref.shape[2] // hp
    ii = lax.broadcasted_iota(jnp.int32, (n, n), 0)
    jj = lax.broadcasted_iota(jnp.int32, (n, n), 1)
    same = (ii // CHUNK) == (jj // CHUNK)
    eye = (ii == jj).astype(F32)
    same_block = {h: (ii // h) == (jj // h) for h in (8, 16, 32)}
    lane = lax.broadcasted_iota(jnp.int32, (1, hp), 1)
    lane2 = lax.broadcasted_iota(jnp.int32, (1, 2 * hp), 1)
    incls = [jnp.logical_and(same, ii >= jj), jnp.logical_and(same, ii <= jj)]
    stricts = [jnp.logical_and(same, ii > jj), jnp.logical_and(same, ii < jj)]
    lms, intras, rhss = [], [], []
    for pr in range(n_pr):
        ls = slice(pr * hp, (pr + 1) * hp)
        q = q_ref[0, :, ls]
        k = k_ref[0, :, ls]
        v = v_ref[0, :, ls]
        e = _dot_exact_rhs(comp_ref[0], ex_ref[pr])
        kh = [jnp.where((lane // HEAD_DIM) == hh, k, 0.0).astype(BF16) for hh in range(2)]
        for p in range(2):
            gc = e[:, p * hp:(p + 1) * hp]
            gl = e[:, (2 + p) * hp:(3 + p) * hp]
            beta = e[:, (4 + p) * hp:(5 + p) * hp]
            eg = jnp.exp(gc)
            kb = k * beta
            kd_ref[p, 0, :, ls] = (k * jnp.exp(gl - gc)).astype(BF16)
            qd_ref[p, 0, :, ls] = (q * eg).astype(BF16)
            rhs = jnp.concatenate([v * beta, kb * eg], axis=1)
            kbq = jnp.concatenate([kb, q], axis=0).astype(BF16)
            for hh in range(2):
                gcb = jnp.broadcast_to(gc[:, hh * HEAD_DIM:hh * HEAD_DIM + 1], (n, n))
                decay = jnp.exp(jnp.where(incls[p], gcb - gcb.T, NEG_BIG))
                aq = _dot_nt(kbq, kh[hh])
                lms.append(jnp.where(stricts[p], aq[:n] * decay, 0.0))
                intras.append(jnp.where(incls[p], aq[n:] * decay, 0.0))
                rhss.append(jnp.where(((lane2 % hp) // HEAD_DIM) == hh, rhs, 0.0).astype(BF16))
    uppers = [p == 1 for _ in range(n_pr) for p in range(2) for _ in range(2)]
    tmats = _unit_tri_inverses(lms, uppers, eye, same_block)
    for pr in range(n_pr):
        ls = slice(pr * hp, (pr + 1) * hp)
        for p in range(2):
            at = 4 * pr + 2 * p
            uw = sum(_dot(tmats[at + hh].astype(BF16), rhss[at + hh]) for hh in range(2))
            tiles = []
            for c in range(n // CHUNK):
                halves = []
                for hh in range(2):
                    lo = c * CHUNK // hp * hp
                    piece = intras[at + hh][c * CHUNK:(c + 1) * CHUNK, lo:lo + hp]
                    halves.append(piece if c % 2 == hh else pltpu.roll(piece, HEAD_DIM, 1))
                tiles.append(jnp.where(lane < HEAD_DIM, halves[0], halves[1]))
            intra2 = jnp.concatenate(tiles, axis=0)
            u_ref[p, 0, :, ls] = uw[:, :hp].astype(BF16)
            w_ref[p, 0, :, ls] = uw[:, hp:].astype(BF16)
            in_ref[p, 0, :, ls] = intra2.astype(BF16)


def _gdn_local(q, k, v, comp, blk):
    b, t, _ = q.shape
    hp = 2 * HEAD_DIM
    n_pair = C_W // hp
    ex = np.zeros((n_pair, LANE, 6 * hp), np.float32)
    for pr in range(n_pair):
        for p in range(2):
            for hh in range(2):
                src = p * C_HEADS + pr * 2 + hh
                for kind, base in enumerate((GC_LANE, GL_LANE, BETA_LANE)):
                    col = (2 * kind + p) * hp + hh * HEAD_DIM
                    ex[pr, base + src, col:col + HEAD_DIM] = 1.0
    gp = LOCAL_PAIRS
    tok = lambda bi, pg, i: (bi, i, pg)
    dirs = lambda bi, pg, i: (0, bi, i, pg)
    out = jax.ShapeDtypeStruct((2, b, t, C_W), BF16)
    return pl.pallas_call(
        _gdn_local_kernel,
        grid=(b, n_pair // gp, t // blk),
        in_specs=[pl.BlockSpec((1, blk, gp * hp), tok)] * 3
                 + [pl.BlockSpec((1, blk, LANE), lambda bi, pg, i: (bi, i, 0)),
                    pl.BlockSpec((gp, LANE, 6 * hp), lambda bi, pg, i: (pg, 0, 0))],
        out_specs=[pl.BlockSpec((2, 1, blk, gp * hp), dirs)] * 5,
        out_shape=[out] * 5,
        compiler_params=_cparams(("parallel", "parallel", "parallel")),
        name="gdn_local",
    )(q, k, v, comp, jnp.asarray(ex, dtype=BF16))


def _gdn_scan_kernel(*refs, ncb, nb):
    ins = (refs[0:6], refs[6:12])
    ex_ref = refs[12]
    outs = refs[13:15]
    s_ref = refs[15]
    j = pl.program_id(1)

    @pl.when(j == 0)
    def _():
        s_ref[...] = jnp.zeros_like(s_ref)

    gw = 4 * HEAD_DIM
    n_grp = C_W // gw
    lane_head = lax.broadcasted_iota(jnp.int32, (1, gw), 1) // HEAD_DIM
    rr = lax.broadcasted_iota(jnp.int32, (gw, gw), 0) // HEAD_DIM
    cc_ = lax.broadcasted_iota(jnp.int32, (gw, gw), 1) // HEAD_DIM
    blockdiag = rr == cc_
    chains = [(bi, p, g) for bi in range(nb) for p in range(2) for g in range(n_grp)]
    for cc in range(ncb):
        rows = [slice(cc * CHUNK, (cc + 1) * CHUNK), slice((ncb - 1 - cc) * CHUNK, (ncb - cc) * CHUNK)]
        sdec = {(bi, p): jnp.exp(_dot_exact_rhs(ins[p][5][bi, rows[p].start:rows[p].start + SUBLANE, :], ex_ref[...]))
                for bi in range(nb) for p in range(2)}
        lanes = [slice(g * gw, (g + 1) * gw) for g in range(n_grp)]
        ss = [s_ref[bi, p, g] for bi, p, g in chains]
        sbs = [s.astype(BF16) for s in ss]
        wqs = [_dot(jnp.concatenate([ins[p][1][0, bi, rows[p], lanes[g]], ins[p][3][0, bi, rows[p], lanes[g]]], axis=0),
                    sb) for (bi, p, g), sb in zip(chains, sbs)]
        v_news = [ins[p][0][0, bi, rows[p], lanes[g]].astype(F32) - wq[:CHUNK] for (bi, p, g), wq in zip(chains, wqs)]
        vstacks = [jnp.concatenate([jnp.where(lane_head == h, v_new, 0.0).astype(BF16)
                                    for h in range(gw // HEAD_DIM)], axis=0) for v_new in v_news]
        for (bi, p, g), s, wq, v_new, vstack in zip(chains, ss, wqs, v_news, vstacks):
            o = wq[CHUNK:] + _dot(ins[p][4][0, bi, rows[p], lanes[g]], vstack)
            upd = _dot_tn(ins[p][2][0, bi, rows[p], lanes[g]], v_new.astype(BF16))
            decay = sdec[bi, p][0:1, p * C_W + g * gw:p * C_W + (g + 1) * gw]
            s_ref[bi, p, g] = s * decay + jnp.where(blockdiag, upd, 0.0)
            outs[p][bi, rows[p], lanes[g]] = o


def _gdn_scan(u, w, kd, qd, intra, comp, n_lat, blk):
    _, b, t, _ = u.shape
    nl = n_lat // blk
    nx = (t - n_lat) // blk
    ncb = blk // CHUNK
    ex = np.zeros((LANE, 2 * C_W), np.float32)
    for p in range(2):
        for h in range(C_HEADS):
            ex[GL_LANE + p * C_HEADS + h, p * C_W + h * HEAD_DIM:p * C_W + (h + 1) * HEAD_DIM] = 1.0

    def blk0(j):
        return jnp.where(j < nx, nl + j, j - nx)

    def blk1(j):
        return jnp.where(j < nx, nl + nx - 1 - j, nl - 1 - (j - nx))

    nb = SCAN_BATCH if b % SCAN_BATCH == 0 else 1
    specs = []
    for p, bf in enumerate((blk0, blk1)):
        specs += [pl.BlockSpec((1, nb, blk, C_W), lambda bi, j, p=p, bf=bf: (p, bi, bf(j), 0))] * 5
        specs += [pl.BlockSpec((nb, blk, LANE), lambda bi, j, bf=bf: (bi, bf(j), 0))]
    specs += [pl.BlockSpec((LANE, 2 * C_W), lambda bi, j: (0, 0))]
    out = jax.ShapeDtypeStruct((b, t, C_W), F32)
    gw = 4 * HEAD_DIM
    return pl.pallas_call(
        functools.partial(_gdn_scan_kernel, ncb=ncb, nb=nb),
        grid=(b // nb, nl + nx),
        in_specs=specs,
        out_specs=[pl.BlockSpec((nb, blk, C_W), lambda bi, j: (bi, blk0(j), 0)),
                   pl.BlockSpec((nb, blk, C_W), lambda bi, j: (bi, blk1(j), 0))],
        out_shape=[out, out],
        scratch_shapes=[pltpu.VMEM((nb, 2, C_W // gw, gw, gw), F32)],
        compiler_params=_cparams(("parallel", "arbitrary")),
        name="gdn_scan",
    )(u, w, kd, qd, intra, comp, u, w, kd, qd, intra, comp, jnp.asarray(ex, dtype=BF16))


def _merge_kernel(x_ref, mod_ref, oa_ref, ob_ref, oc0_ref, oc1_ref, gate_ref, wmg_ref, wbr_ref, wout_ref, og_ref,
                  gs_ref, lng_ref, lnb_ref, wr_ref, o_ref, h_ref, aff_ref, *, alpha, n_exp):
    mod = mod_ref[0, 0]
    tm = x_ref.shape[1]
    d = x_ref.shape[2]
    halves = [slice(0, tm // 2), slice(tm // 2, tm)]
    xs = [x_ref[0, r, :] for r in halves]
    hs = [(_ln(x) * (1.0 + mod[1:2]) + mod[0:1]).astype(BF16) for x in xs]
    gates = [_sigmoid(_dot(h, wmg_ref[...])) for h in hs]
    ocs = [oc0_ref[0, r, :] + oc1_ref[0, r, :] for r in halves]
    ocns = [oc * lax.rsqrt(_dot_exact_rhs(oc * oc, gs_ref[...], 2) + LN_EPS) * og_ref[...] * _silu(gate_ref[0, r, :])
            for oc, r in zip(ocs, halves)]
    ms = [g[:, :d] * _dot(oa_ref[0, r, :], wbr_ref[0]) + g[:, d:2 * d] * _dot(ob_ref[0, r, :], wbr_ref[1])
          + g[:, 2 * d:] * _dot(ocn.astype(BF16), wbr_ref[2]) for g, ocn, r in zip(gates, ocns, halves)]
    mixes = [_dot(m.astype(BF16), wout_ref[...]) for m in ms]
    for x, mix, r in zip(xs, mixes, halves):
        y = alpha * x + mod[2:3] * mix
        x1 = _ln(y) * lng_ref[...] + lnb_ref[...]
        o_ref[0, r, :] = x1
        h2 = _ln(x1) * (1.0 + mod[4:5]) + mod[3:4]
        h_ref[0, r, :] = h2.astype(BF16)
        hp = _split_bf16(h2, 3)
        logits = sum(_dot(hp[i], wr_ref[j]) for i, j in ((2, 0), (1, 1), (0, 2), (1, 0), (0, 1), (0, 0)))
        lane = lax.broadcasted_iota(jnp.int32, logits.shape, 1)
        logits = jnp.where(lane < n_exp, logits, NEG_BIG)
        ex = jnp.exp(logits - jnp.max(logits, axis=-1, keepdims=True))
        aff_ref[0, r, :] = ex / jnp.sum(ex, axis=-1, keepdims=True)


def _merge(xa, mods, oa, ob, oc0, oc1, gate_c, wmg, wbr, wout, og, gs, lng, lnb, w_router, n_lat, tm, alpha):
    b, t, d = xa.shape
    nl = n_lat // tm
    n_exp = w_router.shape[-1]
    wr_pad = jnp.pad(w_router, ((0, 0), (0, LANE - n_exp)))
    pieces, rest = [], wr_pad
    for _ in range(3):
        piece = lax.reduce_precision(rest, exponent_bits=8, mantissa_bits=7)
        pieces.append(piece.astype(BF16))
        rest = rest - piece
    wr = jnp.stack(pieces)
    row = lambda bi, i: (bi, i, 0)
    const2 = lambda bi, i: (0, 0)
    return pl.pallas_call(
        functools.partial(_merge_kernel, alpha=alpha, n_exp=n_exp),
        grid=(b, t // tm),
        in_specs=[pl.BlockSpec((1, tm, d), row),
                  pl.BlockSpec((1, 1, N_MOD, d), lambda bi, i: (bi, (i >= nl).astype(jnp.int32), 0, 0)),
                  pl.BlockSpec((1, tm, BRANCH_W), row), pl.BlockSpec((1, tm, BRANCH_W), row),
                  pl.BlockSpec((1, tm, BRANCH_W), row), pl.BlockSpec((1, tm, BRANCH_W), row),
                  pl.BlockSpec((1, tm, BRANCH_W), row),
                  pl.BlockSpec(wmg.shape, const2),
                  pl.BlockSpec(wbr.shape, lambda bi, i: (0, 0, 0)),
                  pl.BlockSpec(wout.shape, const2),
                  pl.BlockSpec(og.shape, const2), pl.BlockSpec(gs.shape, const2),
                  pl.BlockSpec(lng.shape, const2), pl.BlockSpec(lnb.shape, const2),
                  pl.BlockSpec(wr.shape, lambda bi, i: (0, 0, 0))],
        out_specs=[pl.BlockSpec((1, tm, d), row), pl.BlockSpec((1, tm, d), row), pl.BlockSpec((1, tm, LANE), row)],
        out_shape=[jax.ShapeDtypeStruct((b, t, d), F32), jax.ShapeDtypeStruct((b, t, d), BF16),
                   jax.ShapeDtypeStruct((b, t, LANE), F32)],
        compiler_params=_cparams(("parallel", "parallel")),
        name="merge",
    )(xa, mods, oa, ob, oc0, oc1, gate_c, wmg, wbr, wout, og, gs, lng, lnb, wr)


def _ffn_kernel(x_ref, wg_ref, wu_ref, wd_ref, gate_ref, o_ref, acc_ref):
    f = pl.program_id(2)

    @pl.when(f == 0)
    def _():
        acc_ref[...] = jnp.zeros_like(acc_ref)

    x = x_ref[0]
    g = _dot(x, wg_ref[0, 0].astype(BF16))
    u = _dot(x, wu_ref[0, 0].astype(BF16))
    a = (_silu(g) * u).astype(BF16)
    acc_ref[...] += _dot(a, wd_ref[0, 0].astype(BF16))

    @pl.when(f == pl.num_programs(2) - 1)
    def _():
        o_ref[0] = (acc_ref[...] * gate_ref[0]).astype(BF16)


def _expert_ffn(xs, gate, w_gate_up, w_down, layer, tm, tf):
    e, r, d = xs.shape
    f = w_down.shape[2]
    nf = f // tf
    return pl.pallas_call(
        _ffn_kernel,
        grid=(e, r // tm, nf),
        in_specs=[pl.BlockSpec((1, tm, d), lambda ei, i, fi: (ei, i, 0)),
                  pl.BlockSpec((1, 1, d, tf), lambda ei, i, fi: (layer, ei, 0, fi)),
                  pl.BlockSpec((1, 1, d, tf), lambda ei, i, fi: (layer, ei, 0, nf + fi)),
                  pl.BlockSpec((1, 1, tf, d), lambda ei, i, fi: (layer, ei, fi, 0)),
                  pl.BlockSpec((1, tm, 1), lambda ei, i, fi: (ei, i, 0))],
        out_specs=pl.BlockSpec((1, tm, d), lambda ei, i, fi: (ei, i, 0)),
        out_shape=jax.ShapeDtypeStruct((e, r, d), BF16),
        scratch_shapes=[pltpu.VMEM((tm, d), F32)],
        compiler_params=_cparams(("parallel", "parallel", "arbitrary")),
        name="expert_ffn",
    )(xs, w_gate_up, w_gate_up, w_down, gate)


VALID, FIRST, LAST = 1, 2, 4


def _combine_plan(tok_of_pair, n_tok, tm, rwin):
    bsz, n_pair = tok_of_pair.shape
    nt = n_tok // tm
    n_win = n_pair // rwin
    n_item = nt + n_win
    order = jnp.argsort(tok_of_pair, axis=1)
    tok = jnp.take_along_axis(tok_of_pair, order, axis=1)
    bounds = jnp.arange(nt + 1, dtype=jnp.int32) * tm
    off = jnp.sum(tok[:, None, :] < bounds[None, :, None], axis=-1, dtype=jnp.int32)
    first = jnp.minimum(off[:, :-1] // rwin, n_win - 1)
    last = jnp.minimum(jnp.maximum(off[:, 1:] - 1, off[:, :-1]) // rwin, n_win - 1)
    n_items = last - first + 1
    cum = jnp.cumsum(n_items, axis=1)
    k = jnp.arange(n_item, dtype=jnp.int32)
    tile = jnp.sum(cum[:, None, :] <= k[None, :, None], axis=-1, dtype=jnp.int32)
    valid = tile < nt
    tile = jnp.minimum(tile, nt - 1)
    end = jnp.take_along_axis(cum, tile, axis=1)
    start = end - jnp.take_along_axis(n_items, tile, axis=1)
    win = jnp.where(valid, jnp.take_along_axis(first, tile, axis=1) + (k[None] - start),
                    jnp.take_along_axis(last, tile, axis=1))
    flags = (valid * VALID + (valid & (k[None] == start)) * FIRST + (valid & (k[None] == end - 1)) * LAST)
    return order, tok, tile, win.astype(jnp.int32), flags.astype(jnp.int32)


def _combine_kernel(tile_ref, win_ref, flag_ref, y_ref, tok_ref, x_ref, mod_ref, lng_ref, lnb_ref, o_ref, acc_ref, *,
                    alpha, tm):
    b = pl.program_id(0)
    k = pl.program_id(1)
    flags = flag_ref[b, k]

    @pl.when((flags & FIRST) != 0)
    def _():
        acc_ref[...] = jnp.zeros_like(acc_ref)

    @pl.when((flags & VALID) != 0)
    def _():
        tok0 = tile_ref[b, k] * tm
        rows = lax.broadcasted_iota(jnp.int32, (tm, tok_ref.shape[-1]), 0) + tok0
        pick = jnp.where(rows == tok_ref[0, 0], 1.0, 0.0).astype(BF16)
        acc_ref[...] += _dot(pick, y_ref[0])

    @pl.when((flags & LAST) != 0)
    def _():
        mod = mod_ref[0, 0]
        y = alpha * x_ref[0] + mod[5:6] * acc_ref[...]
        o_ref[0] = _ln(y) * lng_ref[...] + lnb_ref[...]


def _combine(y_sorted, tok_sorted, tile, win, flags, xa, mods, lng, lnb, n_lat, tm, rwin, alpha):
    b, t, d = xa.shape
    nl = n_lat // tm
    n_item = tile.shape[1]
    grid_spec = pltpu.PrefetchScalarGridSpec(
        num_scalar_prefetch=3,
        grid=(b, n_item),
        in_specs=[pl.BlockSpec((1, rwin, d), lambda bi, k, tl, wn, fl: (bi, wn[bi, k], 0)),
                  pl.BlockSpec((1, 1, 1, rwin), lambda bi, k, tl, wn, fl: (bi, wn[bi, k], 0, 0)),
                  pl.BlockSpec((1, tm, d), lambda bi, k, tl, wn, fl: (bi, tl[bi, k], 0)),
                  pl.BlockSpec((1, 1, N_MOD, d),
                               lambda bi, k, tl, wn, fl: (bi, (tl[bi, k] >= nl).astype(jnp.int32), 0, 0)),
                  pl.BlockSpec(lng.shape, lambda bi, k, tl, wn, fl: (0, 0)),
                  pl.BlockSpec(lnb.shape, lambda bi, k, tl, wn, fl: (0, 0))],
        out_specs=pl.BlockSpec((1, tm, d), lambda bi, k, tl, wn, fl: (bi, tl[bi, k], 0)),
        scratch_shapes=[pltpu.VMEM((tm, d), F32)],
    )
    return pl.pallas_call(
        functools.partial(_combine_kernel, alpha=alpha, tm=tm),
        grid_spec=grid_spec,
        out_shape=jax.ShapeDtypeStruct((b, t, d), F32),
        compiler_params=_cparams(("parallel", "arbitrary")),
        name="moe_combine",
    )(tile, win, flags, y_sorted, tok_sorted.reshape(b, -1, 1, rwin), xa, mods, lng, lnb)


def _moe(xa, h, aff, mods, n_exp, w_gate_up, w_down, lng, lnb, layer, n_lat, tm, alpha):
    b, t, d = xa.shape
    m_ctx = t - n_lat
    aff = jnp.swapaxes(aff[:, :, :n_exp], 1, 2)
    cap_l = CAPACITY * n_lat // n_exp
    cap_x = CAPACITY * m_ctx // n_exp
    gate_l, idx_l = lax.top_k(aff[:, :, :n_lat], cap_l)
    gate_x, idx_x = lax.top_k(aff[:, :, n_lat:], cap_x)
    idx = jnp.concatenate([idx_l, idx_x + n_lat], axis=-1)
    gate = jnp.concatenate([gate_l, gate_x], axis=-1)
    cap = cap_l + cap_x
    idx_e = jnp.swapaxes(idx, 0, 1)
    bidx = jnp.arange(b)[None, :, None]
    xs = h[bidx, idx_e].reshape(n_exp, b * cap, d)
    gate_e = jnp.swapaxes(gate, 0, 1).reshape(n_exp, b * cap, 1)
    rows = b * cap
    tmr = rows
    for cand in (1088, 1024, 512, 256, 128, 64, 32, 16):
        if rows % cand == 0:
            tmr = cand
            break
    tf = min(512, w_down.shape[2])
    y = _expert_ffn(xs, gate_e, w_gate_up, w_down, layer, tmr, tf)
    n_pair = n_exp * cap
    rwin = next(r for r in (512, 256, 128, 64, 32, 16, 8) if n_pair % r == 0)
    order, tok, tile, win, flags = _combine_plan(idx.reshape(b, n_pair), t, tm, rwin)
    y_sorted = y[order // cap, jnp.arange(b)[:, None] * cap + order % cap]
    return _combine(y_sorted, tok, tile, win, flags, xa, mods, lng, lnb, n_lat, tm, rwin, alpha)


def _rope_tables(n_lat, m_ctx):
    t = np.arange(n_lat)
    rows = (t // GRID_W).astype(np.float32)
    cols = (t % GRID_W).astype(np.float32)
    n_freq = HEAD_DIM // 4
    inv_freq = jnp.asarray(ROPE_THETA, F32) ** (-jnp.arange(n_freq, dtype=F32) / n_freq)
    ang_r = jnp.asarray(rows)[:, None] * inv_freq
    ang_c = jnp.asarray(cols)[:, None] * inv_freq
    cr, sr, cc, sc = jnp.cos(ang_r), jnp.sin(ang_r), jnp.cos(ang_c), jnp.sin(ang_c)
    cos = jnp.concatenate([cr, cr, cc, cc], axis=-1)
    sin = jnp.concatenate([-sr, sr, -sc, sc], axis=-1)
    cos = jnp.concatenate([cos, jnp.ones((m_ctx, HEAD_DIM), F32)], axis=0)
    sin = jnp.concatenate([sin, jnp.zeros((m_ctx, HEAD_DIM), F32)], axis=0)
    return jnp.tile(cos, (1, A_HEADS)), jnp.tile(sin, (1, A_HEADS))


def kernel(x, c, ctx, c_ctx, w_mod, b_mod, w_in, qk_gain, rpb, conv_w, a_log, dt_bias, o_gain, w_branch, w_out,
           ln1_g, ln1_b, w_router, w_gate_up, w_down, ln2_g, ln2_b):
    b, n_lat, d = x.shape
    m_ctx = ctx.shape[1]
    depth = w_mod.shape[0]
    alpha = (2 * depth) ** 0.25
    tm = math.gcd(256, math.gcd(n_lat, m_ctx))
    blk = tm

    xa = jnp.concatenate([x, ctx], axis=1)
    n_rows = -(-(b + 1) // SUBLANE) * SUBLANE
    cc = jnp.zeros((n_rows, d), F32).at[:b].set(c).at[b].set(c_ctx)
    mod_all = _modulation(cc, w_mod, b_mod).reshape(depth, n_rows, N_MOD, d)
    cos, sin = _rope_tables(n_lat, m_ctx)
    gs_mean = _group_sum_matrix(A_Q_W, 1.0 / HEAD_DIM)
    gs_sum = _group_sum_matrix(C_W, 1.0)
    na_geoms, na_table = _na_classes(n_lat, n_lat + m_ctx)

    offs = np.cumsum((0, A_Q_W, A_KV_W, A_KV_W, B_W, B_W, B_W, 3 * C_W, C_W, 2 * C_HEADS, 2 * C_HEADS, N_BRANCH * d))
    for layer in range(depth):
        ml = mod_all[layer]
        mods = jnp.stack([ml[:b], jnp.broadcast_to(ml[b][None], (b, N_MOD, d))], axis=1)
        wl = w_in[layer]
        wa = wl[:, offs[0]:offs[3]].astype(BF16)
        wb = wl[:, offs[3]:offs[6]].astype(BF16)
        wc = wl[:, offs[6]:offs[8]].astype(BF16)
        wab = jnp.pad(wl[:, offs[8]:offs[10]], ((0, 0), (0, LANE - 4 * C_HEADS))).astype(BF16)
        wmg = wl[:, offs[10]:offs[11]].astype(BF16)
        gq = jnp.tile(qk_gain[layer, 0], A_HEADS)[None]
        gk = jnp.tile(qk_gain[layer, 1], A_KV_HEADS)[None]
        qa, ka, va, qb, kb, vb, qkvc, gate_c, ab = _in_projection(
            xa, mods, wa, wb, wc, wab, gq, gk, cos, sin, gs_mean, n_lat, tm)
        oa = _gqa(qa, ka, va, n_lat, min(GQA_TQ, tm), 2 if (n_lat + m_ctx) % (2 * LANE) == 0 else 1)
        ob = _neighborhood(qb, kb, vb, _na_bias_slabs(rpb[layer], na_geoms), na_table, n_lat)
        qc, kc, vc, comp = _gdn_prep(qkvc, ab, conv_w[layer], a_log[layer], dt_bias[layer], gs_sum, n_lat, tm)
        u, w, kd, qd, intra = _gdn_local(qc, kc, vc, comp, blk)
        oc0, oc1 = _gdn_scan(u, w, kd, qd, intra, comp, n_lat, blk)
        og = jnp.tile(o_gain[layer], C_HEADS)[None]
        xa, h, aff = _merge(xa, mods, oa, ob, oc0, oc1, gate_c, wmg, w_branch[layer].astype(BF16),
                            w_out[layer].astype(BF16), og, gs_mean, ln1_g[layer][None], ln1_b[layer][None],
                            w_router[layer], n_lat, tm, alpha)
        xa = _moe(xa, h, aff, mods, w_router.shape[-1], w_gate_up, w_down, ln2_g[layer][None], ln2_b[layer][None],
                  layer, n_lat, tm, alpha)
    return xa[:, :n_lat]
```

```python
import functools
import math

import numpy as np
import jax
import jax.numpy as jnp
from jax import lax
from jax.experimental import pallas as pl
from jax.experimental.pallas import tpu as pltpu

F32 = jnp.float32
BF16 = jnp.bfloat16
HIGHEST = lax.Precision.HIGHEST

GRID_W = 64
HEAD_DIM = 64
A_HEADS = 8
A_KV_HEADS = 2
B_HEADS = 8
C_HEADS = 8
NA_ROWS = 8
NA_COLS = 16
CONV_K = 4
CHUNK = 64
N_BRANCH = 3
BRANCH_W = 512
CAPACITY = 2
N_MOD = 6
LN_EPS = 1e-6
ROPE_THETA = 10000.0
NEG_BIG = -1e30
LOG2E = 1.4426950408889634

A_Q_W = A_HEADS * HEAD_DIM
A_KV_W = A_KV_HEADS * HEAD_DIM
B_W = B_HEADS * HEAD_DIM
C_W = C_HEADS * HEAD_DIM
LANE = 128
SUBLANE = 8
V7X_VMEM_BYTES = 64 * 1024 * 1024
VMEM_LIMIT = V7X_VMEM_BYTES * 7 // 8
GQA_TQ = 256
LOCAL_PAIRS = 4
SCAN_BATCH = 4
VT_ROWS = HEAD_DIM + 16
NA_WIN = NA_ROWS + 2


def _cparams(sem):
    return pltpu.CompilerParams(dimension_semantics=sem, vmem_limit_bytes=VMEM_LIMIT)


def _dot(a, b):
    return jnp.dot(a, b, preferred_element_type=F32)


def _dot_hi(a, b):
    return jnp.dot(a, b, precision=HIGHEST, preferred_element_type=F32)


def _dot_nt(a, b):
    return lax.dot_general(a, b, (((1,), (1,)), ((), ())), preferred_element_type=F32)


def _dot_tn(a, b):
    return lax.dot_general(a, b, (((0,), (0,)), ((), ())), preferred_element_type=F32)


def _split_bf16(x, pieces):
    out = []
    for _ in range(pieces - 1):
        h = x.astype(BF16)
        out.append(h)
        x = x - h.astype(F32)
    out.append(x.astype(BF16))
    return out


def _dot_exact_rhs(a, b, pieces=3):
    return sum(_dot(p, b) for p in _split_bf16(a, pieces))


def _dot_exact_lhs(a, b, pieces=3):
    return sum(_dot(a, p) for p in _split_bf16(b, pieces))


def _ln(x):
    mu = jnp.mean(x, axis=-1, keepdims=True)
    xc = x - mu
    var = jnp.mean(xc * xc, axis=-1, keepdims=True)
    return xc * lax.rsqrt(var + LN_EPS)


def _sigmoid(x):
    return 1.0 / (1.0 + jnp.exp(-x))


def _silu(x):
    return x * _sigmoid(x)


def _group_sum_matrix(width, scale):
    g = np.arange(width) // HEAD_DIM
    return jnp.asarray((g[:, None] == g[None, :]).astype(np.float32) * scale, dtype=BF16)


def _mod_kernel(c_ref, w_ref, b_ref, o_ref):
    c = c_ref[...]
    o_ref[0] = _dot_hi(_silu(c), w_ref[0]) + b_ref[0]


def _modulation(cc, w_mod, b_mod):
    depth, d, dm = w_mod.shape
    r = cc.shape[0]
    tn = min(dm, 1536)
    return pl.pallas_call(
        _mod_kernel,
        grid=(depth, dm // tn),
        in_specs=[pl.BlockSpec((r, d), lambda l, j: (0, 0)),
                  pl.BlockSpec((1, d, tn), lambda l, j: (l, 0, j)),
                  pl.BlockSpec((1, 1, tn), lambda l, j: (l, 0, j))],
        out_specs=pl.BlockSpec((1, r, tn), lambda l, j: (l, 0, j)),
        out_shape=jax.ShapeDtypeStruct((depth, r, dm), F32),
        compiler_params=_cparams(("parallel", "parallel")),
        name="modulation",
    )(cc, w_mod, b_mod.reshape(depth, 1, dm))


def _rope_swap(y):
    w = y.shape[-1]
    lane = lax.broadcasted_iota(jnp.int32, y.shape, 1)
    up = pltpu.roll(y, w - 16, 1)
    down = pltpu.roll(y, 16, 1)
    return jnp.where((lane % 32) < 16, up, down)


def _inproj_kernel(x_ref, mod_ref, wa_ref, wb_ref, wc_ref, wab_ref, gq_ref, gk_ref, cos_ref, sin_ref,
                   gs_ref, qa_ref, ka_ref, va_ref, qb_ref, kb_ref, vb_ref, qkvc_ref, gatec_ref, ab_ref):
    x = x_ref[0]
    mod = mod_ref[0, 0]
    h = (_ln(x) * (1.0 + mod[1:2]) + mod[0:1]).astype(BF16)
    scale = HEAD_DIM ** -0.5 * LOG2E

    za = _dot(h, wa_ref[...])
    q = za[:, :A_Q_W]
    k = za[:, A_Q_W:A_Q_W + A_KV_W]
    v = za[:, A_Q_W + A_KV_W:]
    cos = cos_ref[...]
    sin = sin_ref[...]
    gs = gs_ref[...]
    qn = q * lax.rsqrt(_dot_exact_rhs(q * q, gs, 2) + LN_EPS) * gq_ref[...]
    qn = qn * cos + _rope_swap(qn) * sin
    qa_ref[0] = (qn * scale).astype(BF16)
    kn = k * lax.rsqrt(_dot_exact_rhs(k * k, gs[:A_KV_W, :A_KV_W], 2) + LN_EPS) * gk_ref[...]
    kn = kn * cos[:, :A_KV_W] + _rope_swap(kn) * sin[:, :A_KV_W]
    ones = jnp.ones((VT_ROWS - HEAD_DIM, x.shape[0]), F32)
    vt = v.T
    for j in range(A_KV_HEADS):
        ka_ref[0, j] = kn[:, j * HEAD_DIM:(j + 1) * HEAD_DIM].astype(BF16)
        va_ref[0, j] = jnp.concatenate([vt[j * HEAD_DIM:(j + 1) * HEAD_DIM], ones], axis=0).astype(BF16)

    zb = _dot(h, wb_ref[...])
    qb_ref[0] = (zb[:, :B_W] * scale).astype(BF16)
    kbv = zb[:, B_W:2 * B_W]
    vt = zb[:, 2 * B_W:].T
    for j in range(B_HEADS):
        kb_ref[0, j] = kbv[:, j * HEAD_DIM:(j + 1) * HEAD_DIM].astype(BF16)
        vb_ref[0, j] = jnp.concatenate([vt[j * HEAD_DIM:(j + 1) * HEAD_DIM], ones], axis=0).astype(BF16)

    zc = _dot(h, wc_ref[...])
    qkvc_ref[0] = zc[:, :3 * C_W]
    gatec_ref[0] = zc[:, 3 * C_W:]
    ab_ref[0] = _dot(h, wab_ref[...])


def _in_projection(xa, mods, wa, wb, wc, wab, gq, gk, cos, sin, gs, n_lat, tm):
    b, t, d = xa.shape
    nl = n_lat // tm
    row = lambda bi, i: (bi, i, 0)
    const = lambda bi, i: (0, 0)
    outs = [
        (jax.ShapeDtypeStruct((b, t, A_Q_W), BF16), pl.BlockSpec((1, tm, A_Q_W), row)),
        (jax.ShapeDtypeStruct((b, A_KV_HEADS, t, HEAD_DIM), BF16),
         pl.BlockSpec((1, A_KV_HEADS, tm, HEAD_DIM), lambda bi, i: (bi, 0, i, 0))),
        (jax.ShapeDtypeStruct((b, A_KV_HEADS, VT_ROWS, t), BF16),
         pl.BlockSpec((1, A_KV_HEADS, VT_ROWS, tm), lambda bi, i: (bi, 0, 0, i))),
        (jax.ShapeDtypeStruct((b, t, B_W), BF16), pl.BlockSpec((1, tm, B_W), row)),
        (jax.ShapeDtypeStruct((b, B_HEADS, t, HEAD_DIM), BF16),
         pl.BlockSpec((1, B_HEADS, tm, HEAD_DIM), lambda bi, i: (bi, 0, i, 0))),
        (jax.ShapeDtypeStruct((b, B_HEADS, VT_ROWS, t), BF16),
         pl.BlockSpec((1, B_HEADS, VT_ROWS, tm), lambda bi, i: (bi, 0, 0, i))),
        (jax.ShapeDtypeStruct((b, t, 3 * C_W), F32), pl.BlockSpec((1, tm, 3 * C_W), row)),
        (jax.ShapeDtypeStruct((b, t, C_W), F32), pl.BlockSpec((1, tm, C_W), row)),
        (jax.ShapeDtypeStruct((b, t, LANE), F32), pl.BlockSpec((1, tm, LANE), row)),
    ]
    return pl.pallas_call(
        _inproj_kernel,
        grid=(b, t // tm),
        in_specs=[pl.BlockSpec((1, tm, d), row),
                  pl.BlockSpec((1, 1, N_MOD, d), lambda bi, i: (bi, (i >= nl).astype(jnp.int32), 0, 0)),
                  pl.BlockSpec(wa.shape, const), pl.BlockSpec(wb.shape, const),
                  pl.BlockSpec(wc.shape, const), pl.BlockSpec(wab.shape, const),
                  pl.BlockSpec(gq.shape, const), pl.BlockSpec(gk.shape, const),
                  pl.BlockSpec((tm, A_Q_W), lambda bi, i: (i, 0)),
                  pl.BlockSpec((tm, A_Q_W), lambda bi, i: (i, 0)),
                  pl.BlockSpec(gs.shape, const)],
        out_specs=[o[1] for o in outs],
        out_shape=[o[0] for o in outs],
        compiler_params=_cparams(("parallel", "parallel")),
        name="in_projection",
    )(xa, mods, wa, wb, wc, wab, gq, gk, cos, sin, gs)


def _gqa_kernel(q_ref, k_ref, v_ref, o_ref, *, tq, ck, n_lat, n_tot):
    i = pl.program_id(1)
    g = A_HEADS // A_KV_HEADS

    def attend(key_lo, key_hi, n_split):
        qs = [jnp.concatenate([q_ref[0, :, (kh * g + j) * HEAD_DIM:(kh * g + j + 1) * HEAD_DIM] for j in range(g)],
                              axis=0) for kh in range(A_KV_HEADS)]
        step = (key_hi - key_lo) // n_split
        units = [(kh, key_lo + r * step, key_lo + (r + 1) * step) for r in range(n_split) for kh in range(A_KV_HEADS)]
        ss, ms, ps, oes = {}, {}, {}, {}
        for n in range(len(units) + 2):
            if n < len(units):
                kh, lo, hi = units[n]
                ss[n] = _dot_nt(k_ref[0, kh, lo:hi, :], qs[kh])
            if 0 <= n - 1 < len(units):
                ms[n - 1] = jnp.max(ss[n - 1], axis=0, keepdims=True).astype(BF16).astype(F32)
                ps[n - 1] = jnp.exp2(ss.pop(n - 1).astype(BF16) - ms[n - 1].astype(BF16))
            if 0 <= n - 2 < len(units):
                kh, lo, hi = units[n - 2]
                oes[n - 2] = _dot(v_ref[0, kh, :, lo:hi], ps.pop(n - 2))
        for kh in range(A_KV_HEADS):
            mine = [n for n, u in enumerate(units) if u[0] == kh]
            m = functools.reduce(jnp.maximum, [ms[n] for n in mine])
            oe = sum(oes[n] * jnp.exp2(ms[n] - m) for n in mine)
            o = (oe[:HEAD_DIM] / oe[HEAD_DIM:HEAD_DIM + 1]).T.astype(BF16)
            for j in range(g):
                o_ref[0, :, (kh * g + j) * HEAD_DIM:(kh * g + j + 1) * HEAD_DIM] = o[j * tq:(j + 1) * tq]

    @pl.when(i < n_lat // tq)
    def _():
        attend(0, n_tot, ck)

    @pl.when(i >= n_lat // tq)
    def _():
        attend(n_lat, n_tot, 1)


def _gqa(qa, ka, va, n_lat, tq, ck):
    b, t, _ = qa.shape
    return pl.pallas_call(
        functools.partial(_gqa_kernel, tq=tq, ck=ck, n_lat=n_lat, n_tot=t),
        grid=(b, t // tq),
        in_specs=[pl.BlockSpec((1, tq, A_Q_W), lambda bi, i: (bi, i, 0)),
                  pl.BlockSpec((1, A_KV_HEADS, t, HEAD_DIM), lambda bi, i: (bi, 0, 0, 0)),
                  pl.BlockSpec((1, A_KV_HEADS, VT_ROWS, t), lambda bi, i: (bi, 0, 0, 0))],
        out_specs=pl.BlockSpec((1, tq, A_Q_W), lambda bi, i: (bi, i, 0)),
        out_shape=jax.ShapeDtypeStruct((b, t, A_Q_W), BF16),
        compiler_params=_cparams(("parallel", "parallel")),
        name="gqa",
    )(qa, ka, va)


def _na_window_start(r0, rows):
    return np.clip(r0 - NA_ROWS // 2, 0, rows - NA_ROWS - 1) // 2 * 2


def _na_classes(n_lat, n_tot):
    rows = n_lat // GRID_W
    assert rows >= NA_WIN and rows % 2 == 0
    geoms, table = [], []
    for r0 in range(0, rows, 2):
        geom = (int(_na_window_start(r0, rows)) - r0,) + tuple(
            int(np.clip(r0 + qr - NA_ROWS // 2, 0, rows - NA_ROWS)) - r0 for qr in range(2))
        if geom not in geoms:
            geoms.append(geom)
        table.append(geoms.index(geom))
    table += [len(geoms)] * ((n_tot - n_lat) // (2 * GRID_W))
    return geoms, np.asarray(table, np.int32)


def _na_bias_slabs(rpb, geoms):
    n_head = rpb.shape[0]
    cj = np.arange(GRID_W)[:, None]
    c = np.arange(GRID_W)[None, :]
    col_start = np.clip(c - NA_COLS // 2, 0, GRID_W - NA_COLS)
    col_ok = (cj >= col_start) & (cj < col_start + NA_COLS)
    col_off = np.clip(cj - c + (NA_COLS - 1), 0, 2 * NA_COLS - 2)
    spread = np.zeros((2 * NA_COLS - 1, GRID_W * GRID_W), np.float32)
    spread[col_off.reshape(-1), np.arange(GRID_W * GRID_W)] = 1.0
    kr = np.arange(NA_WIN)[:, None]
    qr = np.arange(2)[None, :]
    pick = np.zeros((len(geoms), NA_WIN, 2, 2 * NA_ROWS - 1), np.float32)
    ok = np.zeros((len(geoms), 1, NA_WIN, 2, GRID_W, GRID_W), bool)
    for n, (a, b0, b1) in enumerate(geoms):
        first = np.where(qr == 0, b0, b1)
        row_ok = (a + kr >= first) & (a + kr < first + NA_ROWS)
        row_off = np.clip(a + kr - qr + (NA_ROWS - 1), 0, 2 * NA_ROWS - 2)
        pick[n, kr, qr, row_off] = 1.0
        ok[n, 0] = row_ok[:, :, None, None] & col_ok[None, None]
    rows = jnp.einsum("nkqo,hoc->nhkqc", jnp.asarray(pick), rpb * LOG2E, precision=HIGHEST)
    full = jnp.dot(rows, jnp.asarray(spread), precision=HIGHEST)
    full = full.reshape(len(geoms), n_head, NA_WIN, 2, GRID_W, GRID_W)
    masked = jnp.where(jnp.asarray(ok), full, NEG_BIG)
    slabs = jnp.concatenate([masked[:, :, :, 0], masked[:, :, :, 1]], axis=-1)
    slabs = slabs.reshape(len(geoms), n_head, NA_WIN * GRID_W, 2 * GRID_W)
    return jnp.concatenate([slabs, jnp.full_like(slabs[:1], NEG_BIG)], axis=0).astype(F32)


def _na_kernel(cls_ref, q_ref, k_ref, vt_ref, *rest, n_lat, n_tot, pps):
    del cls_ref
    bias_refs, o_ref = rest[:pps], rest[pps]
    rows = n_lat // GRID_W
    tq = 2 * GRID_W
    hs = [slice(h * HEAD_DIM, (h + 1) * HEAD_DIM) for h in range(B_HEADS)]
    units = []
    for j in range(pps):
        r0 = 2 * (pps * pl.program_id(1) + j)
        start = jnp.clip(r0 - NA_ROWS // 2, 0, rows - NA_ROWS - 1) // 2 * 2
        nb = pl.ds(pl.multiple_of(start * GRID_W, 2 * GRID_W), NA_WIN * GRID_W)
        units += [(j, h, nb) for h in range(B_HEADS)]
    qs = [q_ref[0, j * tq:(j + 1) * tq, hs[h]] for j, h, _ in units]
    s_nb = [_dot_nt(k_ref[0, h, nb, :], q) + bias_refs[j][0, h] for (j, h, nb), q in zip(units, qs)]
    s_cx = [_dot_nt(k_ref[0, h, n_lat:n_tot, :], q) for (j, h, nb), q in zip(units, qs)]
    ms = [jnp.maximum(jnp.max(a, axis=0, keepdims=True), jnp.max(c, axis=0, keepdims=True))
          for a, c in zip(s_nb, s_cx)]
    p_nb = [jnp.exp2((a - m).astype(BF16)) for a, m in zip(s_nb, ms)]
    p_cx = [jnp.exp2((c - m).astype(BF16)) for c, m in zip(s_cx, ms)]
    for (j, h, nb), pn, pc in zip(units, p_nb, p_cx):
        oe = _dot(vt_ref[0, h, :, nb], pn) + _dot(vt_ref[0, h, :, n_lat:n_tot], pc)
        o_ref[0, j * tq:(j + 1) * tq, hs[h]] = (oe[:HEAD_DIM] / oe[HEAD_DIM:HEAD_DIM + 1]).T.astype(BF16)


def _neighborhood(qb, kb, vbt, slabs, table, n_lat):
    b, t, _ = qb.shape
    tq = 2 * GRID_W
    pps = 2 if (t // tq) % 2 == 0 else 1
    slab_specs = [pl.BlockSpec((1,) + slabs.shape[1:], lambda bi, pi, cls, j=j: (cls[pps * pi + j], 0, 0, 0))
                  for j in range(pps)]
    grid_spec = pltpu.PrefetchScalarGridSpec(
        num_scalar_prefetch=1,
        grid=(b, t // (pps * tq)),
        in_specs=[pl.BlockSpec((1, pps * tq, B_W), lambda bi, pi, cls: (bi, pi, 0)),
                  pl.BlockSpec((1, B_HEADS, t, HEAD_DIM), lambda bi, pi, cls: (bi, 0, 0, 0)),
                  pl.BlockSpec((1, B_HEADS, VT_ROWS, t), lambda bi, pi, cls: (bi, 0, 0, 0))] + slab_specs,
        out_specs=pl.BlockSpec((1, pps * tq, B_W), lambda bi, pi, cls: (bi, pi, 0)),
    )
    return pl.pallas_call(
        functools.partial(_na_kernel, n_lat=n_lat, n_tot=t, pps=pps),
        grid_spec=grid_spec,
        out_shape=jax.ShapeDtypeStruct((b, t, B_W), BF16),
        compiler_params=_cparams(("parallel", "parallel")),
        name="neighborhood",
    )(jnp.asarray(table), qb, kb, vbt, *([slabs] * pps))


GC_LANE, GL_LANE, BETA_LANE = 0, 2 * C_HEADS, 4 * C_HEADS


def _gdn_prep_kernel(x_ref, prev_ref, next_ref, ab_ref, cw_ref, alog_ref, dtb_ref, gs_ref,
                     q_ref, k_ref, v_ref, comp_ref, *, tm, n_lat, n_tot):
    i = pl.program_id(1)
    nl = n_lat // tm
    nt = n_tot // tm
    first = jnp.logical_or(i == 0, i == nl)
    last = jnp.logical_or(i == nl - 1, i == nt - 1)
    x = x_ref[0]
    pv = jnp.where(first, 0.0, prev_ref[0])
    nx = jnp.where(last, 0.0, next_ref[0])
    row = lax.broadcasted_iota(jnp.int32, (tm, 1), 0)
    prev1 = pv[SUBLANE - 1:SUBLANE]
    prev2 = pv[SUBLANE - 2:SUBLANE - 1]
    xm1 = jnp.where(row == 0, prev1, pltpu.roll(x, 1, 0))
    xm2 = jnp.where(row == 0, prev2, jnp.where(row == 1, prev1, pltpu.roll(x, 2, 0)))
    xp1 = jnp.where(row == tm - 1, nx[0:1], pltpu.roll(x, tm - 1, 0))
    cw = cw_ref[...]
    y = _silu(xm2 * cw[0:1] + xm1 * cw[1:2] + x * cw[2:3] + xp1 * cw[3:4])
    q = y[:, :C_W]
    k = y[:, C_W:2 * C_W]
    gs = gs_ref[...]
    q_ref[0] = q * lax.rsqrt(_dot_exact_rhs(q * q, gs, 2) + LN_EPS) * (HEAD_DIM ** -0.5)
    k_ref[0] = k * lax.rsqrt(_dot_exact_rhs(k * k, gs, 2) + LN_EPS)
    v_ref[0] = y[:, 2 * C_W:]
    ab = ab_ref[0]
    z = ab + dtb_ref[...]
    softplus = jnp.maximum(z, 0.0) + jnp.log(1.0 + jnp.exp(-jnp.abs(z)))
    lane = lax.broadcasted_iota(jnp.int32, ab.shape, 1)
    gb = jnp.where(lane < 2 * C_HEADS, -jnp.exp(alog_ref[...]) * softplus, _sigmoid(ab))
    ii = lax.broadcasted_iota(jnp.int32, (tm, tm), 0)
    jj = lax.broadcasted_iota(jnp.int32, (tm, tm), 1)
    same = (ii // CHUNK) == (jj // CHUNK)
    gb3 = _split_bf16(gb, 3)

    def masked_sum(mask):
        mb = jnp.where(mask, 1.0, 0.0).astype(BF16)
        return sum(_dot(mb, piece) for piece in gb3)

    prefix = masked_sum(jnp.logical_and(same, jj <= ii))
    suffix = masked_sum(jnp.logical_and(same, jj >= ii))
    total = masked_sum(same)
    comp = jnp.where(lane < C_HEADS, prefix,
                     jnp.where(lane < GL_LANE, suffix,
                               jnp.where(lane < BETA_LANE, pltpu.roll(total, GL_LANE, 1),
                                         jnp.where(lane < BETA_LANE + 2 * C_HEADS, pltpu.roll(gb, 2 * C_HEADS, 1),
                                                   0.0))))
    comp_ref[0] = comp


def _gdn_prep(qkvc, ab, conv_w, a_log, dt_bias, gs, n_lat, tm):
    b, t, _ = qkvc.shape
    c3 = 3 * C_W
    alog_row = jnp.zeros((1, LANE), F32).at[0, :2 * C_HEADS].set(a_log.reshape(-1))
    dtb_row = jnp.zeros((1, LANE), F32).at[0, :2 * C_HEADS].set(dt_bias.reshape(-1))
    row = lambda bi, i: (bi, i, 0)
    const2 = lambda bi, i: (0, 0)
    hb = tm // SUBLANE
    last8 = t // SUBLANE - 1
    tok = jax.ShapeDtypeStruct((b, t, C_W), F32)
    return pl.pallas_call(
        functools.partial(_gdn_prep_kernel, tm=tm, n_lat=n_lat, n_tot=t),
        grid=(b, t // tm),
        in_specs=[pl.BlockSpec((1, tm, c3), row),
                  pl.BlockSpec((1, SUBLANE, c3), lambda bi, i: (bi, jnp.maximum(i * hb - 1, 0), 0)),
                  pl.BlockSpec((1, SUBLANE, c3), lambda bi, i: (bi, jnp.minimum((i + 1) * hb, last8), 0)),
                  pl.BlockSpec((1, tm, LANE), row),
                  pl.BlockSpec((CONV_K, c3), const2),
                  pl.BlockSpec((1, LANE), const2), pl.BlockSpec((1, LANE), const2),
                  pl.BlockSpec(gs.shape, const2)],
        out_specs=[pl.BlockSpec((1, tm, C_W), row)] * 3 + [pl.BlockSpec((1, tm, LANE), row)],
        out_shape=[tok, tok, tok, jax.ShapeDtypeStruct((b, t, LANE), F32)],
        compiler_params=_cparams(("parallel", "parallel")),
        name="gdn_prep",
    )(qkvc, qkvc, qkvc, ab, conv_w, alog_row, dtb_row, gs)


def _half_block_rows(a, h, upper):
    return jnp.concatenate([a[s:s + h] for s in range(0 if upper else h, a.shape[0], 2 * h)], axis=0)


def _with_half_block_rows(full, part, h, upper):
    pieces = []
    for blk in range(full.shape[0] // h):
        if (blk % 2 == 0) == upper:
            pieces.append(part[(blk // 2) * h:(blk // 2 + 1) * h])
        else:
            pieces.append(full[blk * h:(blk + 1) * h])
    return jnp.concatenate(pieces, axis=0)


def _unit_tri_inverses(lms, uppers, eye, same_block):
    def mm(a, b_):
        return _dot(a.astype(BF16), b_.astype(BF16))

    lds = [jnp.where(same_block[SUBLANE], lm, 0.0) for lm in lms]
    xs = [eye - ld for ld in lds]
    lds = [ld.astype(BF16) for ld in lds]
    pws = [_dot(ld, ld).astype(BF16) for ld in lds]
    xs = [x + mm(x, pw) for x, pw in zip(xs, pws)]
    pws = [_dot(pw, pw).astype(BF16) for pw in pws]
    xs = [x + mm(x, pw) for x, pw in zip(xs, pws)]
    h = SUBLANE
    while h < CHUNK:
        level = jnp.logical_not(same_block[h]) if 2 * h == CHUNK else jnp.logical_and(
            same_block[2 * h], jnp.logical_not(same_block[h]))
        xbs = [x.astype(BF16) for x in xs]
        ls = [_half_block_rows(jnp.where(level, lm, 0.0), h, up).astype(BF16) for lm, up in zip(lms, uppers)]
        ys = [_dot(l, xb) for l, xb in zip(ls, xbs)]
        zs = [_with_half_block_rows(jnp.zeros_like(x), y, h, up).astype(BF16) for x, y, up in zip(xs, ys, uppers)]
        xrs = [_half_block_rows(x, h, up) for x, up in zip(xs, uppers)]
        upd = [_dot(xr.astype(BF16), z) for xr, z in zip(xrs, zs)]
        xs = [_with_half_block_rows(x, xr - u, h, up) for x, xr, u, up in zip(xs, xrs, upd, uppers)]
        h *= 2
    return xs


def _gdn_local_kernel(q_ref, k_ref, v_ref, comp_ref, ex_ref, u_ref, w_ref, kd_ref, qd_ref, in_ref):
    n = q_ref.shape[1]
    hp = 2 * HEAD_DIM
    n_pr = q_ref.shape[2] // hp
    ii = lax.broadcasted_iota(jnp.int32, (n, n), 0)
    jj = lax.broadcasted_iota(jnp.int32, (n, n), 1)
    same = (ii // CHUNK) == (jj // CHUNK)
    eye = (ii == jj).astype(F32)
    same_block = {h: (ii // h) == (jj // h) for h in (8, 16, 32)}
    lane = lax.broadcasted_iota(jnp.int32, (1, hp), 1)
    lane2 = lax.broadcasted_iota(jnp.int32, (1, 2 * hp), 1)
    incls = [jnp.logical_and(same, ii >= jj), jnp.logical_and(same, ii <= jj)]
    stricts = [jnp.logical_and(same, ii > jj), jnp.logical_and(same, ii < jj)]
    lms, intras, rhss = [], [], []
    for pr in range(n_pr):
        ls = slice(pr * hp, (pr + 1) * hp)
        q = q_ref[0, :, ls]
        k = k_ref[0, :, ls]
        v = v_ref[0, :, ls]
        e = _dot_exact_rhs(comp_ref[0], ex_ref[pr])
        kh = [jnp.where((lane // HEAD_DIM) == hh, k, 0.0).astype(BF16) for hh in range(2)]
        for p in range(2):
            gc = e[:, p * hp:(p + 1) * hp]
            gl = e[:, (2 + p) * hp:(3 + p) * hp]
            beta = e[:, (4 + p) * hp:(5 + p) * hp]
            eg = jnp.exp(gc)
            kb = k * beta
            kd_ref[p, 0, :, ls] = (k * jnp.exp(gl - gc)).astype(BF16)
            qd_ref[p, 0, :, ls] = (q * eg).astype(BF16)
            rhs = jnp.concatenate([v * beta, kb * eg], axis=1)
            kbq = jnp.concatenate([kb, q], axis=0).astype(BF16)
            for hh in range(2):
                gcb = jnp.broadcast_to(gc[:, hh * HEAD_DIM:hh * HEAD_DIM + 1], (n, n))
                decay = jnp.exp(jnp.where(incls[p], gcb - gcb.T, NEG_BIG))
                aq = _dot_nt(kbq, kh[hh])
                lms.append(jnp.where(stricts[p], aq[:n] * decay, 0.0))
                intras.append(jnp.where(incls[p], aq[n:] * decay, 0.0))
                rhss.append(jnp.where(((lane2 % hp) // HEAD_DIM) == hh, rhs, 0.0).astype(BF16))
    uppers = [p == 1 for _ in range(n_pr) for p in range(2) for _ in range(2)]
    tmats = _unit_tri_inverses(lms, uppers, eye, same_block)
    for pr in range(n_pr):
        ls = slice(pr * hp, (pr + 1) * hp)
        for p in range(2):
            at = 4 * pr + 2 * p
            uw = sum(_dot(tmats[at + hh].astype(BF16), rhss[at + hh]) for hh in range(2))
            tiles = []
            for c in range(n // CHUNK):
                halves = []
                for hh in range(2):
                    lo = c * CHUNK // hp * hp
                    piece = intras[at + hh][c * CHUNK:(c + 1) * CHUNK, lo:lo + hp]
                    halves.append(piece if c % 2 == hh else pltpu.roll(piece, HEAD_DIM, 1))
                tiles.append(jnp.where(lane < HEAD_DIM, halves[0], halves[1]))
            intra2 = jnp.concatenate(tiles, axis=0)
            u_ref[p, 0, :, ls] = uw[:, :hp].astype(BF16)
            w_ref[p, 0, :, ls] = uw[:, hp:].astype(BF16)
            in_ref[p, 0, :, ls] = intra2.astype(BF16)


def _gdn_local(q, k, v, comp, blk):
    b, t, _ = q.shape
    hp = 2 * HEAD_DIM
    n_pair = C_W // hp
    ex = np.zeros((n_pair, LANE, 6 * hp), np.float32)
    for pr in range(n_pair):
        for p in range(2):
            for hh in range(2):
                src = p * C_HEADS + pr * 2 + hh
                for kind, base in enumerate((GC_LANE, GL_LANE, BETA_LANE)):
                    col = (2 * kind + p) * hp + hh * HEAD_DIM
                    ex[pr, base + src, col:col + HEAD_DIM] = 1.0
    gp = LOCAL_PAIRS
    tok = lambda bi, pg, i: (bi, i, pg)
    dirs = lambda bi, pg, i: (0, bi, i, pg)
    out = jax.ShapeDtypeStruct((2, b, t, C_W), BF16)
    return pl.pallas_call(
        _gdn_local_kernel,
        grid=(b, n_pair // gp, t // blk),
        in_specs=[pl.BlockSpec((1, blk, gp * hp), tok)] * 3
                 + [pl.BlockSpec((1, blk, LANE), lambda bi, pg, i: (bi, i, 0)),
                    pl.BlockSpec((gp, LANE, 6 * hp), lambda bi, pg, i: (pg, 0, 0))],
        out_specs=[pl.BlockSpec((2, 1, blk, gp * hp), dirs)] * 5,
        out_shape=[out] * 5,
        compiler_params=_cparams(("parallel", "parallel", "parallel")),
        name="gdn_local",
    )(q, k, v, comp, jnp.asarray(ex, dtype=BF16))


def _gdn_scan_kernel(*refs, ncb, nb):
    ins = (refs[0:6], refs[6:12])
    ex_ref = refs[12]
    outs = refs[13:15]
    s_ref = refs[15]
    j = pl.program_id(1)

    @pl.when(j == 0)
    def _():
        s_ref[...] = jnp.zeros_like(s_ref)

    gw = 4 * HEAD_DIM
    n_grp = C_W // gw
    lane_head = lax.broadcasted_iota(jnp.int32, (1, gw), 1) // HEAD_DIM
    rr = lax.broadcasted_iota(jnp.int32, (gw, gw), 0) // HEAD_DIM
    cc_ = lax.broadcasted_iota(jnp.int32, (gw, gw), 1) // HEAD_DIM
    blockdiag = rr == cc_
    chains = [(bi, p, g) for bi in range(nb) for p in range(2) for g in range(n_grp)]
    for cc in range(ncb):
        rows = [slice(cc * CHUNK, (cc + 1) * CHUNK), slice((ncb - 1 - cc) * CHUNK, (ncb - cc) * CHUNK)]
        sdec = {(bi, p): jnp.exp(_dot_exact_rhs(ins[p][5][bi, rows[p].start:rows[p].start + SUBLANE, :], ex_ref[...]))
                for bi in range(nb) for p in range(2)}
        lanes = [slice(g * gw, (g + 1) * gw) for g in range(n_grp)]
        ss = [s_ref[bi, p, g] for bi, p, g in chains]
        sbs = [s.astype(BF16) for s in ss]
        wqs = [_dot(jnp.concatenate([ins[p][1][0, bi, rows[p], lanes[g]], ins[p][3][0, bi, rows[p], lanes[g]]], axis=0),
                    sb) for (bi, p, g), sb in zip(chains, sbs)]
        v_news = [ins[p][0][0, bi, rows[p], lanes[g]].astype(F32) - wq[:CHUNK] for (bi, p, g), wq in zip(chains, wqs)]
        vstacks = [jnp.concatenate([jnp.where(lane_head == h, v_new, 0.0).astype(BF16)
                                    for h in range(gw // HEAD_DIM)], axis=0) for v_new in v_news]
        for (bi, p, g), s, wq, v_new, vstack in zip(chains, ss, wqs, v_news, vstacks):
            o = wq[CHUNK:] + _dot(ins[p][4][0, bi, rows[p], lanes[g]], vstack)
            upd = _dot_tn(ins[p][2][0, bi, rows[p], lanes[g]], v_new.astype(BF16))
            decay = sdec[bi, p][0:1, p * C_W + g * gw:p * C_W + (g + 1) * gw]
            s_ref[bi, p, g] = s * decay + jnp.where(blockdiag, upd, 0.0)
            outs[p][bi, rows[p], lanes[g]] = o


def _gdn_scan(u, w, kd, qd, intra, comp, n_lat, blk):
    _, b, t, _ = u.shape
    nl = n_lat // blk
    nx = (t - n_lat) // blk
    ncb = blk // CHUNK
    ex = np.zeros((LANE, 2 * C_W), np.float32)
    for p in range(2):
        for h in range(C_HEADS):
            ex[GL_LANE + p * C_HEADS + h, p * C_W + h * HEAD_DIM:p * C_W + (h + 1) * HEAD_DIM] = 1.0

    def blk0(j):
        return jnp.where(j < nx, nl + j, j - nx)

    def blk1(j):
        return jnp.where(j < nx, nl + nx - 1 - j, nl - 1 - (j - nx))

    nb = SCAN_BATCH if b % SCAN_BATCH == 0 else 1
    specs = []
    for p, bf in enumerate((blk0, blk1)):
        specs += [pl.BlockSpec((1, nb, blk, C_W), lambda bi, j, p=p, bf=bf: (p, bi, bf(j), 0))] * 5
        specs += [pl.BlockSpec((nb, blk, LANE), lambda bi, j, bf=bf: (bi, bf(j), 0))]
    specs += [pl.BlockSpec((LANE, 2 * C_W), lambda bi, j: (0, 0))]
    out = jax.ShapeDtypeStruct((b, t, C_W), F32)
    gw = 4 * HEAD_DIM
    return pl.pallas_call(
        functools.partial(_gdn_scan_kernel, ncb=ncb, nb=nb),
        grid=(b // nb, nl + nx),
        in_specs=specs,
        out_specs=[pl.BlockSpec((nb, blk, C_W), lambda bi, j: (bi, blk0(j), 0)),
                   pl.BlockSpec((nb, blk, C_W), lambda bi, j: (bi, blk1(j), 0))],
        out_shape=[out, out],
        scratch_shapes=[pltpu.VMEM((nb, 2, C_W // gw, gw, gw), F32)],
        compiler_params=_cparams(("parallel", "arbitrary")),
        name="gdn_scan",
    )(u, w, kd, qd, intra, comp, u, w, kd, qd, intra, comp, jnp.asarray(ex, dtype=BF16))


def _merge_kernel(x_ref, mod_ref, oa_ref, ob_ref, oc0_ref, oc1_ref, gate_ref, wmg_ref, wbr_ref, wout_ref, og_ref,
                  gs_ref, lng_ref, lnb_ref, wr_ref, o_ref, h_ref, aff_ref, *, alpha, n_exp):
    mod = mod_ref[0, 0]
    tm = x_ref.shape[1]
    d = x_ref.shape[2]
    halves = [slice(0, tm // 2), slice(tm // 2, tm)]
    xs = [x_ref[0, r, :] for r in halves]
    hs = [(_ln(x) * (1.0 + mod[1:2]) + mod[0:1]).astype(BF16) for x in xs]
    gates = [_sigmoid(_dot(h, wmg_ref[...])) for h in hs]
    ocs = [oc0_ref[0, r, :] + oc1_ref[0, r, :] for r in halves]
    ocns = [oc * lax.rsqrt(_dot_exact_rhs(oc * oc, gs_ref[...], 2) + LN_EPS) * og_ref[...] * _silu(gate_ref[0, r, :])
            for oc, r in zip(ocs, halves)]
    ms = [g[:, :d] * _dot(oa_ref[0, r, :], wbr_ref[0]) + g[:, d:2 * d] * _dot(ob_ref[0, r, :], wbr_ref[1])
          + g[:, 2 * d:] * _dot(ocn.astype(BF16), wbr_ref[2]) for g, ocn, r in zip(gates, ocns, halves)]
    mixes = [_dot(m.astype(BF16), wout_ref[...]) for m in ms]
    for x, mix, r in zip(xs, mixes, halves):
        y = alpha * x + mod[2:3] * mix
        x1 = _ln(y) * lng_ref[...] + lnb_ref[...]
        o_ref[0, r, :] = x1
        h2 = _ln(x1) * (1.0 + mod[4:5]) + mod[3:4]
        h_ref[0, r, :] = h2.astype(BF16)
        hp = _split_bf16(h2, 3)
        logits = sum(_dot(hp[i], wr_ref[j]) for i, j in ((2, 0), (1, 1), (0, 2), (1, 0), (0, 1), (0, 0)))
        lane = lax.broadcasted_iota(jnp.int32, logits.shape, 1)
        logits = jnp.where(lane < n_exp, logits, NEG_BIG)
        ex = jnp.exp(logits - jnp.max(logits, axis=-1, keepdims=True))
        aff_ref[0, r, :] = ex / jnp.sum(ex, axis=-1, keepdims=True)


def _merge(xa, mods, oa, ob, oc0, oc1, gate_c, wmg, wbr, wout, og, gs, lng, lnb, w_router, n_lat, tm, alpha):
    b, t, d = xa.shape
    nl = n_lat // tm
    n_exp = w_router.shape[-1]
    wr_pad = jnp.pad(w_router, ((0, 0), (0, LANE - n_exp)))
    pieces, rest = [], wr_pad
    for _ in range(3):
        piece = lax.reduce_precision(rest, exponent_bits=8, mantissa_bits=7)
        pieces.append(piece.astype(BF16))
        rest = rest - piece
    wr = jnp.stack(pieces)
    row = lambda bi, i: (bi, i, 0)
    const2 = lambda bi, i: (0, 0)
    return pl.pallas_call(
        functools.partial(_merge_kernel, alpha=alpha, n_exp=n_exp),
        grid=(b, t // tm),
        in_specs=[pl.BlockSpec((1, tm, d), row),
                  pl.BlockSpec((1, 1, N_MOD, d), lambda bi, i: (bi, (i >= nl).astype(jnp.int32), 0, 0)),
                  pl.BlockSpec((1, tm, BRANCH_W), row), pl.BlockSpec((1, tm, BRANCH_W), row),
                  pl.BlockSpec((1, tm, BRANCH_W), row), pl.BlockSpec((1, tm, BRANCH_W), row),
                  pl.BlockSpec((1, tm, BRANCH_W), row),
                  pl.BlockSpec(wmg.shape, const2),
                  pl.BlockSpec(wbr.shape, lambda bi, i: (0, 0, 0)),
                  pl.BlockSpec(wout.shape, const2),
                  pl.BlockSpec(og.shape, const2), pl.BlockSpec(gs.shape, const2),
                  pl.BlockSpec(lng.shape, const2), pl.BlockSpec(lnb.shape, const2),
                  pl.BlockSpec(wr.shape, lambda bi, i: (0, 0, 0))],
        out_specs=[pl.BlockSpec((1, tm, d), row), pl.BlockSpec((1, tm, d), row), pl.BlockSpec((1, tm, LANE), row)],
        out_shape=[jax.ShapeDtypeStruct((b, t, d), F32), jax.ShapeDtypeStruct((b, t, d), BF16),
                   jax.ShapeDtypeStruct((b, t, LANE), F32)],
        compiler_params=_cparams(("parallel", "parallel")),
        name="merge",
    )(xa, mods, oa, ob, oc0, oc1, gate_c, wmg, wbr, wout, og, gs, lng, lnb, wr)


def _ffn_kernel(x_ref, wg_ref, wu_ref, wd_ref, gate_ref, o_ref, acc_ref):
    f = pl.program_id(2)

    @pl.when(f == 0)
    def _():
        acc_ref[...] = jnp.zeros_like(acc_ref)

    x = x_ref[0]
    g = _dot(x, wg_ref[0, 0].astype(BF16))
    u = _dot(x, wu_ref[0, 0].astype(BF16))
    a = (_silu(g) * u).astype(BF16)
    acc_ref[...] += _dot(a, wd_ref[0, 0].astype(BF16))

    @pl.when(f == pl.num_programs(2) - 1)
    def _():
        o_ref[0] = (acc_ref[...] * gate_ref[0]).astype(BF16)


def _expert_ffn(xs, gate, w_gate_up, w_down, layer, tm, tf):
    e, r, d = xs.shape
    f = w_down.shape[2]
    nf = f // tf
    return pl.pallas_call(
        _ffn_kernel,
        grid=(e, r // tm, nf),
        in_specs=[pl.BlockSpec((1, tm, d), lambda ei, i, fi: (ei, i, 0)),
                  pl.BlockSpec((1, 1, d, tf), lambda ei, i, fi: (layer, ei, 0, fi)),
                  pl.BlockSpec((1, 1, d, tf), lambda ei, i, fi: (layer, ei, 0, nf + fi)),
                  pl.BlockSpec((1, 1, tf, d), lambda ei, i, fi: (layer, ei, fi, 0)),
                  pl.BlockSpec((1, tm, 1), lambda ei, i, fi: (ei, i, 0))],
        out_specs=pl.BlockSpec((1, tm, d), lambda ei, i, fi: (ei, i, 0)),
        out_shape=jax.ShapeDtypeStruct((e, r, d), BF16),
        scratch_shapes=[pltpu.VMEM((tm, d), F32)],
        compiler_params=_cparams(("parallel", "parallel", "arbitrary")),
        name="expert_ffn",
    )(xs, w_gate_up, w_gate_up, w_down, gate)


VALID, FIRST, LAST = 1, 2, 4


def _combine_plan(tok_of_pair, n_tok, tm, rwin):
    bsz, n_pair = tok_of_pair.shape
    nt = n_tok // tm
    n_win = n_pair // rwin
    n_item = nt + n_win
    order = jnp.argsort(tok_of_pair, axis=1)
    tok = jnp.take_along_axis(tok_of_pair, order, axis=1)
    bounds = jnp.arange(nt + 1, dtype=jnp.int32) * tm
    off = jnp.sum(tok[:, None, :] < bounds[None, :, None], axis=-1, dtype=jnp.int32)
    first = jnp.minimum(off[:, :-1] // rwin, n_win - 1)
    last = jnp.minimum(jnp.maximum(off[:, 1:] - 1, off[:, :-1]) // rwin, n_win - 1)
    n_items = last - first + 1
    cum = jnp.cumsum(n_items, axis=1)
    k = jnp.arange(n_item, dtype=jnp.int32)
    tile = jnp.sum(cum[:, None, :] <= k[None, :, None], axis=-1, dtype=jnp.int32)
    valid = tile < nt
    tile = jnp.minimum(tile, nt - 1)
    end = jnp.take_along_axis(cum, tile, axis=1)
    start = end - jnp.take_along_axis(n_items, tile, axis=1)
    win = jnp.where(valid, jnp.take_along_axis(first, tile, axis=1) + (k[None] - start),
                    jnp.take_along_axis(last, tile, axis=1))
    flags = (valid * VALID + (valid & (k[None] == start)) * FIRST + (valid & (k[None] == end - 1)) * LAST)
    return order, tok, tile, win.astype(jnp.int32), flags.astype(jnp.int32)


def _combine_kernel(tile_ref, win_ref, flag_ref, y_ref, tok_ref, x_ref, mod_ref, lng_ref, lnb_ref, o_ref, acc_ref, *,
                    alpha, tm):
    b = pl.program_id(0)
    k = pl.program_id(1)
    flags = flag_ref[b, k]

    @pl.when((flags & FIRST) != 0)
    def _():
        acc_ref[...] = jnp.zeros_like(acc_ref)

    @pl.when((flags & VALID) != 0)
    def _():
        tok0 = tile_ref[b, k] * tm
        rows = lax.broadcasted_iota(jnp.int32, (tm, tok_ref.shape[-1]), 0) + tok0
        pick = jnp.where(rows == tok_ref[0, 0], 1.0, 0.0).astype(BF16)
        acc_ref[...] += _dot(pick, y_ref[0])

    @pl.when((flags & LAST) != 0)
    def _():
        mod = mod_ref[0, 0]
        y = alpha * x_ref[0] + mod[5:6] * acc_ref[...]
        o_ref[0] = _ln(y) * lng_ref[...] + lnb_ref[...]


def _combine(y_sorted, tok_sorted, tile, win, flags, xa, mods, lng, lnb, n_lat, tm, rwin, alpha):
    b, t, d = xa.shape
    nl = n_lat // tm
    n_item = tile.shape[1]
    grid_spec = pltpu.PrefetchScalarGridSpec(
        num_scalar_prefetch=3,
        grid=(b, n_item),
        in_specs=[pl.BlockSpec((1, rwin, d), lambda bi, k, tl, wn, fl: (bi, wn[bi, k], 0)),
                  pl.BlockSpec((1, 1, 1, rwin), lambda bi, k, tl, wn, fl: (bi, wn[bi, k], 0, 0)),
                  pl.BlockSpec((1, tm, d), lambda bi, k, tl, wn, fl: (bi, tl[bi, k], 0)),
                  pl.BlockSpec((1, 1, N_MOD, d),
                               lambda bi, k, tl, wn, fl: (bi, (tl[bi, k] >= nl).astype(jnp.int32), 0, 0)),
                  pl.BlockSpec(lng.shape, lambda bi, k, tl, wn, fl: (0, 0)),
                  pl.BlockSpec(lnb.shape, lambda bi, k, tl, wn, fl: (0, 0))],
        out_specs=pl.BlockSpec((1, tm, d), lambda bi, k, tl, wn, fl: (bi, tl[bi, k], 0)),
        scratch_shapes=[pltpu.VMEM((tm, d), F32)],
    )
    return pl.pallas_call(
        functools.partial(_combine_kernel, alpha=alpha, tm=tm),
        grid_spec=grid_spec,
        out_shape=jax.ShapeDtypeStruct((b, t, d), F32),
        compiler_params=_cparams(("parallel", "arbitrary")),
        name="moe_combine",
    )(tile, win, flags, y_sorted, tok_sorted.reshape(b, -1, 1, rwin), xa, mods, lng, lnb)


def _moe(xa, h, aff, mods, n_exp, w_gate_up, w_down, lng, lnb, layer, n_lat, tm, alpha):
    b, t, d = xa.shape
    m_ctx = t - n_lat
    aff = jnp.swapaxes(aff[:, :, :n_exp], 1, 2)
    cap_l = CAPACITY * n_lat // n_exp
    cap_x = CAPACITY * m_ctx // n_exp
    gate_l, idx_l = lax.top_k(aff[:, :, :n_lat], cap_l)
    gate_x, idx_x = lax.top_k(aff[:, :, n_lat:], cap_x)
    idx = jnp.concatenate([idx_l, idx_x + n_lat], axis=-1)
    gate = jnp.concatenate([gate_l, gate_x], axis=-1)
    cap = cap_l + cap_x
    idx_e = jnp.swapaxes(idx, 0, 1)
    bidx = jnp.arange(b)[None, :, None]
    xs = h[bidx, idx_e].reshape(n_exp, b * cap, d)
    gate_e = jnp.swapaxes(gate, 0, 1).reshape(n_exp, b * cap, 1)
    rows = b * cap
    tmr = rows
    for cand in (1088, 1024, 512, 256, 128, 64, 32, 16):
        if rows % cand == 0:
            tmr = cand
            break
    tf = min(512, w_down.shape[2])
    y = _expert_ffn(xs, gate_e, w_gate_up, w_down, layer, tmr, tf)
    n_pair = n_exp * cap
    rwin = next(r for r in (512, 256, 128, 64, 32, 16, 8) if n_pair % r == 0)
    order, tok, tile, win, flags = _combine_plan(idx.reshape(b, n_pair), t, tm, rwin)
    y_sorted = y[order // cap, jnp.arange(b)[:, None] * cap + order % cap]
    return _combine(y_sorted, tok, tile, win, flags, xa, mods, lng, lnb, n_lat, tm, rwin, alpha)


def _rope_tables(n_lat, m_ctx):
    t = np.arange(n_lat)
    rows = (t // GRID_W).astype(np.float32)
    cols = (t % GRID_W).astype(np.float32)
    n_freq = HEAD_DIM // 4
    inv_freq = jnp.asarray(ROPE_THETA, F32) ** (-jnp.arange(n_freq, dtype=F32) / n_freq)
    ang_r = jnp.asarray(rows)[:, None] * inv_freq
    ang_c = jnp.asarray(cols)[:, None] * inv_freq
    cr, sr, cc, sc = jnp.cos(ang_r), jnp.sin(ang_r), jnp.cos(ang_c), jnp.sin(ang_c)
    cos = jnp.concatenate([cr, cr, cc, cc], axis=-1)
    sin = jnp.concatenate([-sr, sr, -sc, sc], axis=-1)
    cos = jnp.concatenate([cos, jnp.ones((m_ctx, HEAD_DIM), F32)], axis=0)
    sin = jnp.concatenate([sin, jnp.zeros((m_ctx, HEAD_DIM), F32)], axis=0)
    return jnp.tile(cos, (1, A_HEADS)), jnp.tile(sin, (1, A_HEADS))


def kernel(x, c, ctx, c_ctx, w_mod, b_mod, w_in, qk_gain, rpb, conv_w, a_log, dt_bias, o_gain, w_branch, w_out,
           ln1_g, ln1_b, w_router, w_gate_up, w_down, ln2_g, ln2_b):
    b, n_lat, d = x.shape
    m_ctx = ctx.shape[1]
    depth = w_mod.shape[0]
    alpha = (2 * depth) ** 0.25
    tm = math.gcd(256, math.gcd(n_lat, m_ctx))
    blk = tm

    xa = jnp.concatenate([x, ctx], axis=1)
    n_rows = -(-(b + 1) // SUBLANE) * SUBLANE
    cc = jnp.zeros((n_rows, d), F32).at[:b].set(c).at[b].set(c_ctx)
    mod_all = _modulation(cc, w_mod, b_mod).reshape(depth, n_rows, N_MOD, d)
    cos, sin = _rope_tables(n_lat, m_ctx)
    gs_mean = _group_sum_matrix(A_Q_W, 1.0 / HEAD_DIM)
    gs_sum = _group_sum_matrix(C_W, 1.0)
    na_geoms, na_table = _na_classes(n_lat, n_lat + m_ctx)

    offs = np.cumsum((0, A_Q_W, A_KV_W, A_KV_W, B_W, B_W, B_W, 3 * C_W, C_W, 2 * C_HEADS, 2 * C_HEADS, N_BRANCH * d))
    for layer in range(depth):
        ml = mod_all[layer]
        mods = jnp.stack([ml[:b], jnp.broadcast_to(ml[b][None], (b, N_MOD, d))], axis=1)
        wl = w_in[layer]
        wa = wl[:, offs[0]:offs[3]].astype(BF16)
        wb = wl[:, offs[3]:offs[6]].astype(BF16)
        wc = wl[:, offs[6]:offs[8]].astype(BF16)
        wab = jnp.pad(wl[:, offs[8]:offs[10]], ((0, 0), (0, LANE - 4 * C_HEADS))).astype(BF16)
        wmg = wl[:, offs[10]:offs[11]].astype(BF16)
        gq = jnp.tile(qk_gain[layer, 0], A_HEADS)[None]
        gk = jnp.tile(qk_gain[layer, 1], A_KV_HEADS)[None]
        qa, ka, va, qb, kb, vb, qkvc, gate_c, ab = _in_projection(
            xa, mods, wa, wb, wc, wab, gq, gk, cos, sin, gs_mean, n_lat, tm)
        oa = _gqa(qa, ka, va, n_lat, min(GQA_TQ, tm), 2 if (n_lat + m_ctx) % (2 * LANE) == 0 else 1)
        ob = _neighborhood(qb, kb, vb, _na_bias_slabs(rpb[layer], na_geoms), na_table, n_lat)
        qc, kc, vc, comp = _gdn_prep(qkvc, ab, conv_w[layer], a_log[layer], dt_bias[layer], gs_sum, n_lat, tm)
        u, w, kd, qd, intra = _gdn_local(qc, kc, vc, comp, blk)
        oc0, oc1 = _gdn_scan(u, w, kd, qd, intra, comp, n_lat, blk)
        og = jnp.tile(o_gain[layer], C_HEADS)[None]
        xa, h, aff = _merge(xa, mods, oa, ob, oc0, oc1, gate_c, wmg, w_branch[layer].astype(BF16),
                            w_out[layer].astype(BF16), og, gs_mean, ln1_g[layer][None], ln1_b[layer][None],
                            w_router[layer], n_lat, tm, alpha)
        xa = _moe(xa, h, aff, mods, w_router.shape[-1], w_gate_up, w_down, ln2_g[layer][None], ln2_b[layer][None],
                  layer, n_lat, tm, alpha)
    return xa[:, :n_lat]
```
